```python
import math
import jax, jax.numpy as jnp
from jax import lax
import numpy as np

D_MODEL = 2048
BATCH = 4
SEQ = 2048
DEPTH = 1
DEC_BATCH = 32
DEC_SEQ = 16
PAST_LEN = 1024

CHUNK = 64
N_META = 16
D_CONV = D_MODEL // 2
CONV_W = 3
D_SSM = D_MODEL // 2
SSM_H = 16
SSM_G = D_SSM // SSM_H
SSM_P = 64
N_EGROUPS = 4
EXPERTS_PER_GROUP = 8
N_EXPERTS = N_EGROUPS * EXPERTS_PER_GROUP
TOP_K_IN_GROUP = 2
D_EXPERT = D_MODEL // 8
EPS = 1e-6
D_IN = 3 * D_CONV + D_SSM + 2 * D_MODEL
DT_MIN = 1e-3
DT_MAX = 1e-1

kernel_name = "gated_conv_s5_hmoe_stream_step"


def rmsnorm(x, g):
    xf = x.astype(jnp.float32)
    xf = xf * lax.rsqrt(jnp.mean(xf * xf, axis=-1, keepdims=True) + EPS)
    return (xf * g.astype(jnp.float32)).astype(x.dtype)


def short_conv(z, buf, w):
    L = z.shape[1]
    zp = jnp.concatenate([buf.astype(z.dtype), z], axis=1)
    y = w[0] * zp[:, 0:L]
    for k in range(1, CONV_W):
        y = y + w[k] * zp[:, k:k + L]
    return y, zp[:, -(CONV_W - 1):]


def s5_discretize(lam_re, lam_im, log_dt, b_re, b_im):
    f32 = jnp.float32
    dt = jnp.exp(log_dt.astype(f32))[:, None]
    lr, li = lam_re.astype(f32), lam_im.astype(f32)
    mag = jnp.exp(lr * dt)
    ab_re, ab_im = mag * jnp.cos(li * dt), mag * jnp.sin(li * dt)
    nr, ni = ab_re - 1.0, ab_im
    den = lr * lr + li * li
    k_re = (nr * lr + ni * li) / den
    k_im = (ni * lr - nr * li) / den
    br, bi = b_re.astype(f32), b_im.astype(f32)
    bb_re = k_re[..., None] * br - k_im[..., None] * bi
    bb_im = k_re[..., None] * bi + k_im[..., None] * br
    return ab_re, ab_im, bb_re, bb_im


def _scan_combine(e1, e2):
    a1r, a1i, b1r, b1i = e1
    a2r, a2i, b2r, b2i = e2
    ar = a2r * a1r - a2i * a1i
    ai = a2r * a1i + a2i * a1r
    br = a2r * b1r - a2i * b1i + b2r
    bi = a2r * b1i + a2i * b1r + b2i
    return ar, ai, br, bi


def s5_scan(u, s0_re, s0_im, ab_re, ab_im, bb_re, bb_im):
    bu_re = jnp.einsum('gph,blgh->blgp', bb_re, u)
    bu_im = jnp.einsum('gph,blgh->blgp', bb_im, u)
    s0r, s0i = s0_re.astype(jnp.float32), s0_im.astype(jnp.float32)
    bu_re = bu_re.at[:, 0].add(ab_re * s0r - ab_im * s0i)
    bu_im = bu_im.at[:, 0].add(ab_re * s0i + ab_im * s0r)
    a_re = jnp.broadcast_to(ab_re, bu_re.shape)
    a_im = jnp.broadcast_to(ab_im, bu_im.shape)
    _, _, s_re, s_im = lax.associative_scan(_scan_combine, (a_re, a_im, bu_re, bu_im), axis=1)
    return s_re, s_im


def hier_moe(v, w_coarse, w_fine, w_gate, w_up, w_down):
    bsz, L, D = v.shape
    t = v.reshape(-1, D)
    pc = jax.nn.softmax((t @ w_coarse).astype(jnp.float32), axis=-1)
    pg, gi = lax.top_k(pc, 1)
    lf = (t @ w_fine).astype(jnp.float32).reshape(-1, N_EGROUPS, EXPERTS_PER_GROUP)
    lf_g = jnp.take_along_axis(lf, gi[:, :, None], axis=1)[:, 0]
    vals, ei = lax.top_k(lf_g, TOP_K_IN_GROUP)
    wk = jax.nn.softmax(vals, axis=-1) * pg
    eid = gi * EXPERTS_PER_GROUP + ei
    gates = jnp.sum(jax.nn.one_hot(eid, N_EXPERTS, dtype=jnp.float32) * wk[..., None], axis=1)
    hg = jnp.einsum('td,edf->tef', t, w_gate)
    hu = jnp.einsum('td,edf->tef', t, w_up)
    act = jax.nn.silu(hg) * hu * gates[:, :, None].astype(t.dtype)
    out = jnp.einsum('tef,efd->td', act, w_down)
    return out.reshape(bsz, L, D)


def layer(h, conv_buf, s_re0, s_im0, lw):
    (norm1, w_in, conv_w, lam_re, lam_im, log_dt, b_re, b_im, c_re, c_im, d_skip,
     w_glu, w_conv_out, w_ssm_out, w_o, norm2, w_coarse, w_fine, w_gate, w_up, w_down) = lw
    bsz, L, _ = h.shape
    u = rmsnorm(h, norm1)
    p = u @ w_in
    xb, xc, xv, xs, ga, gb = jnp.split(
        p, [D_CONV, 2 * D_CONV, 3 * D_CONV, 3 * D_CONV + D_SSM, 3 * D_CONV + D_SSM + D_MODEL], axis=-1)
    yconv, new_buf = short_conv(xc * xv, conv_buf, conv_w)
    ya = (xb * yconv) @ w_conv_out
    us = xs.astype(jnp.float32).reshape(bsz, L, SSM_G, SSM_H)
    ab_re, ab_im, bb_re, bb_im = s5_discretize(lam_re, lam_im, log_dt, b_re, b_im)
    s_re, s_im = s5_scan(us, s_re0, s_im0, ab_re, ab_im, bb_re, bb_im)
    ys = (jnp.einsum('ghp,blgp->blgh', c_re.astype(jnp.float32), s_re)
          - jnp.einsum('ghp,blgp->blgh', c_im.astype(jnp.float32), s_im)
          + d_skip.astype(jnp.float32).reshape(SSM_G, SSM_H) * us)
    ys = jax.nn.gelu(ys.reshape(bsz, L, D_SSM).astype(h.dtype))
    ys = ys * jax.nn.sigmoid(ys @ w_glu)
    yb = ys @ w_ssm_out
    merged = jax.nn.sigmoid(ga) * ya + jax.nn.sigmoid(gb) * yb
    h = h + merged @ w_o
    h = h + hier_moe(rmsnorm(h, norm2), w_coarse, w_fine, w_gate, w_up, w_down)
    return h, new_buf, s_re[:, -1], s_im[:, -1]


def trunk(x, conv_state, ssm_re, ssm_im, stacked, final_norm):
    h = x
    new_conv, new_re, new_im = [], [], []
    for l in range(DEPTH):
        lw = tuple(w[l] for w in stacked)
        h, nb, nr, ni = layer(h, conv_state[l], ssm_re[l], ssm_im[l], lw)
        new_conv.append(nb)
        new_re.append(nr)
        new_im.append(ni)
    return rmsnorm(h, final_norm), jnp.stack(new_conv), jnp.stack(new_re), jnp.stack(new_im)


def setup_inputs(seed: int = 0) -> dict:
    key = jax.random.key(seed)
    ks = jax.random.split(key, 32)
    f32 = jnp.float32
    nrm = lambda k, shape, scale: jax.random.normal(k, shape, f32) * scale
    n_idx = jnp.arange(SSM_P, dtype=f32)
    log_dt = jax.random.uniform(ks[9], (DEPTH, SSM_G), f32, math.log(DT_MIN), math.log(DT_MAX))
    return {
        "x_prompt": nrm(ks[0], (BATCH, SEQ, D_MODEL), 1.0),
        "x_sample": nrm(ks[1], (DEC_BATCH, DEC_SEQ, D_MODEL), 1.0),
        "state_conv": nrm(ks[2], (DEPTH, DEC_BATCH, CONV_W - 1, D_CONV), 1.0),
        "state_ssm_re": nrm(ks[3], (DEPTH, DEC_BATCH, SSM_G, SSM_P), 0.1),
        "state_ssm_im": nrm(ks[4], (DEPTH, DEC_BATCH, SSM_G, SSM_P), 0.1),
        "meta_tokens": nrm(ks[5], (N_META, D_MODEL), 1.0),
        "norm1": 1.0 + nrm(ks[6], (DEPTH, D_MODEL), 0.02),
        "w_in": nrm(ks[7], (DEPTH, D_MODEL, D_IN), D_MODEL ** -0.5),
        "conv_w": nrm(ks[8], (DEPTH, CONV_W, D_CONV), CONV_W ** -0.5),
        "lam_re": -0.5 + nrm(ks[10], (DEPTH, SSM_G, SSM_P), 0.01),
        "lam_im": math.pi * n_idx + nrm(ks[11], (DEPTH, SSM_G, SSM_P), 0.01),
        "log_dt": log_dt,
        "ssm_b_re": nrm(ks[12], (DEPTH, SSM_G, SSM_P, SSM_H), (2 * SSM_H) ** -0.5),
        "ssm_b_im": nrm(ks[13], (DEPTH, SSM_G, SSM_P, SSM_H), (2 * SSM_H) ** -0.5),
        "ssm_c_re": nrm(ks[14], (DEPTH, SSM_G, SSM_H, SSM_P), (2 * SSM_P) ** -0.5),
        "ssm_c_im": nrm(ks[15], (DEPTH, SSM_G, SSM_H, SSM_P), (2 * SSM_P) ** -0.5),
        "ssm_d": nrm(ks[16], (DEPTH, D_SSM), 1.0),
        "w_glu": nrm(ks[17], (DEPTH, D_SSM, D_SSM), D_SSM ** -0.5),
        "w_conv_out": nrm(ks[18], (DEPTH, D_CONV, D_MODEL), D_CONV ** -0.5),
        "w_ssm_out": nrm(ks[19], (DEPTH, D_SSM, D_MODEL), D_SSM ** -0.5),
        "w_o": nrm(ks[20], (DEPTH, D_MODEL, D_MODEL), D_MODEL ** -0.5),
        "norm2": 1.0 + nrm(ks[21], (DEPTH, D_MODEL), 0.02),
        "w_coarse": nrm(ks[22], (DEPTH, D_MODEL, N_EGROUPS), D_MODEL ** -0.5),
        "w_fine": nrm(ks[23], (DEPTH, D_MODEL, N_EXPERTS), D_MODEL ** -0.5),
        "w_gate": nrm(ks[24], (DEPTH, N_EXPERTS, D_MODEL, D_EXPERT), D_MODEL ** -0.5),
        "w_up": nrm(ks[25], (DEPTH, N_EXPERTS, D_MODEL, D_EXPERT), D_MODEL ** -0.5),
        "w_down": nrm(ks[26], (DEPTH, N_EXPERTS, D_EXPERT, D_MODEL), D_EXPERT ** -0.5),
        "final_norm": 1.0 + nrm(ks[27], (D_MODEL,), 0.02),
    }


def reference(x_prompt, x_sample, state_conv, state_ssm_re, state_ssm_im, meta_tokens,
              norm1, w_in, conv_w, lam_re, lam_im, log_dt, ssm_b_re, ssm_b_im, ssm_c_re, ssm_c_im,
              ssm_d, w_glu, w_conv_out, w_ssm_out, w_o, norm2, w_coarse, w_fine, w_gate, w_up,
              w_down, final_norm):
    stacked = (norm1, w_in, conv_w, lam_re, lam_im, log_dt, ssm_b_re, ssm_b_im, ssm_c_re, ssm_c_im,
               ssm_d, w_glu, w_conv_out, w_ssm_out, w_o, norm2, w_coarse, w_fine, w_gate, w_up, w_down)
    bp = x_prompt.shape[0]
    meta = jnp.broadcast_to(meta_tokens.astype(x_prompt.dtype)[None], (bp, N_META, D_MODEL))
    xp = jnp.concatenate([meta, x_prompt], axis=1)
    conv0 = jnp.zeros((DEPTH, bp, CONV_W - 1, D_CONV), x_prompt.dtype)
    ssm0 = jnp.zeros((DEPTH, bp, SSM_G, SSM_P), jnp.float32)
    yp_full, new_conv_prompt, new_ssm_re_prompt, new_ssm_im_prompt = trunk(
        xp, conv0, ssm0, ssm0, stacked, final_norm)
    y_prompt = yp_full[:, N_META:]
    y_sample, new_conv_sample, new_ssm_re_sample, new_ssm_im_sample = trunk(
        x_sample, state_conv, state_ssm_re, state_ssm_im, stacked, final_norm)
    return (y_prompt, y_sample, new_conv_prompt, new_ssm_re_prompt, new_ssm_im_prompt,
            new_conv_sample, new_ssm_re_sample, new_ssm_im_sample)
```

```python
import functools

import jax
import jax.numpy as jnp
from jax import lax
from jax.experimental import pallas as pl
from jax.experimental.pallas import tpu as pltpu

F32 = jnp.float32
BF16 = jnp.bfloat16

D_MODEL = 2048
D_CONV = 1024
D_SSM = 1024
SSM_H = 16
SSM_G = 64
SSM_P = 64
N_META = 16
N_EGROUPS = 4
EXPERTS_PER_GROUP = 8
N_EXPERTS = 32
D_EXPERT = 256
EPS = 1e-6

CHUNK = 16
CHUNK_W = CHUNK * SSM_H

TM = 256
TM_MOE = 256
VMEM_LIMIT = 52 * 1024 * 1024


def _rmsnorm(x, g):
    return x * lax.rsqrt(jnp.mean(x * x, axis=-1, keepdims=True) + EPS) * g


def _sigmoid(x):
    return 1.0 / (1.0 + jnp.exp(-x))


def _gelu_tanh(x):
    c = 0.7978845608028654
    return 0.5 * x * (1.0 + jnp.tanh(c * (x + 0.044715 * (x * x * x))))


def _weight_copy(w_hbm, stage, sem, c, slot, rows, col0, ncols):
    return pltpu.make_async_copy(
        w_hbm.at[pl.ds(c * rows, rows), pl.ds(col0, ncols)], stage.at[slot], sem.at[slot])


def _load_weight_bf16(w_hbm, w_vmem, stage, sem, col0=0):
    k, n = w_vmem.shape
    rows = stage.shape[1]
    nchunk = k // rows
    _weight_copy(w_hbm, stage, sem, 0, 0, rows, col0, n).start()
    for c in range(nchunk):
        slot = c % 2
        if c + 1 < nchunk:
            _weight_copy(w_hbm, stage, sem, c + 1, 1 - slot, rows, col0, n).start()
        _weight_copy(w_hbm, stage, sem, c, slot, rows, col0, n).wait()
        w_vmem[pl.ds(c * rows, rows), :] = stage[slot].astype(BF16)


def _in_proj_mix_body(n_p_tiles, xp_ref, xsm_ref, meta_ref, g_ref, w_hbm,
                      xb_out, z_out, xs_out, zmeta_out, xsmeta_out,
                      w_vmem, stage, sem):
    i = pl.program_id(0)
    g = g_ref[...]

    def project(u, m_rows):
        del m_rows
        xb = jnp.dot(u, w_vmem[:, 0:D_CONV], preferred_element_type=F32)
        xc = jnp.dot(u, w_vmem[:, D_CONV:2 * D_CONV], preferred_element_type=F32)
        xv = jnp.dot(u, w_vmem[:, 2 * D_CONV:3 * D_CONV], preferred_element_type=F32)
        xs = jnp.dot(u, w_vmem[:, 3 * D_CONV:3 * D_CONV + D_SSM], preferred_element_type=F32)
        return xb, xc * xv, xs

    @pl.when(i == 0)
    def _():
        _load_weight_bf16(w_hbm, w_vmem, stage, sem, col0=0)
        um = _rmsnorm(meta_ref[...], g).astype(BF16)
        _, zm, xsm = project(um, N_META)
        zmeta_out[...] = zm
        xsmeta_out[...] = xsm.astype(BF16)

    x = jnp.where(i < n_p_tiles, xp_ref[...], xsm_ref[...])
    u = _rmsnorm(x, g).astype(BF16)
    xb, z, xs = project(u, TM)
    xb_out[...] = xb.astype(BF16)
    z_out[...] = z
    xs_out[...] = xs.astype(BF16)


def _in_proj_gate_body(n_p_tiles, xp_ref, xsm_ref, g_ref, w_hbm, ga_out, gb_out,
                       w_vmem, stage, sem):
    i = pl.program_id(0)

    @pl.when(i == 0)
    def _():
        _load_weight_bf16(w_hbm, w_vmem, stage, sem, col0=3 * D_CONV + D_SSM)

    x = jnp.where(i < n_p_tiles, xp_ref[...], xsm_ref[...])
    u = _rmsnorm(x, g_ref[...]).astype(BF16)
    ga = jnp.dot(u, w_vmem[:, 0:D_MODEL], preferred_element_type=F32)
    ga_out[...] = _sigmoid(ga).astype(BF16)
    gb = jnp.dot(u, w_vmem[:, D_MODEL:2 * D_MODEL], preferred_element_type=F32)
    gb_out[...] = _sigmoid(gb).astype(BF16)


def _two_stream_specs(n_p_tiles, n_s_tiles):
    xp_spec = pl.BlockSpec((TM, D_MODEL), lambda i: (jnp.minimum(i, n_p_tiles - 1), 0))
    xs_spec = pl.BlockSpec((TM, D_MODEL), lambda i: (jnp.clip(i - n_p_tiles, 0, n_s_tiles - 1), 0))
    return xp_spec, xs_spec


def _in_proj(xp, xsm, meta, norm1, w_in):
    t_p, t_s = xp.shape[0], xsm.shape[0]
    n_p, n_s = t_p // TM, t_s // TM
    t = t_p + t_s
    half = 3 * D_CONV + D_SSM
    xp_spec, xs_spec = _two_stream_specs(n_p, n_s)
    g_spec = pl.BlockSpec((1, D_MODEL), lambda i: (0, 0))
    any_spec = pl.BlockSpec(memory_space=pl.ANY)
    stage_rows = 128
    row = lambda w: pl.BlockSpec((TM, w), lambda i: (i, 0))
    const = lambda r, w: pl.BlockSpec((r, w), lambda i: (0, 0))
    params = pltpu.CompilerParams(dimension_semantics=("arbitrary",), vmem_limit_bytes=VMEM_LIMIT)
    scratch = [pltpu.VMEM((D_MODEL, half), BF16),
               pltpu.VMEM((2, stage_rows, half), F32),
               pltpu.SemaphoreType.DMA((2,))]

    xb, z, xs, zmeta, xsmeta = pl.pallas_call(
        functools.partial(_in_proj_mix_body, n_p),
        grid=(n_p + n_s,),
        in_specs=[xp_spec, xs_spec, const(N_META, D_MODEL), g_spec, any_spec],
        out_specs=[row(D_CONV), row(D_CONV), row(D_SSM), const(N_META, D_CONV), const(N_META, D_SSM)],
        out_shape=[jax.ShapeDtypeStruct((t, D_CONV), BF16),
                   jax.ShapeDtypeStruct((t, D_CONV), F32),
                   jax.ShapeDtypeStruct((t, D_SSM), BF16),
                   jax.ShapeDtypeStruct((N_META, D_CONV), F32),
                   jax.ShapeDtypeStruct((N_META, D_SSM), BF16)],
        scratch_shapes=scratch,
        compiler_params=params,
        name="in_proj_mix",
    )(xp, xsm, meta, norm1, w_in)

    sga, sgb = pl.pallas_call(
        functools.partial(_in_proj_gate_body, n_p),
        grid=(n_p + n_s,),
        in_specs=[xp_spec, xs_spec, g_spec, any_spec],
        out_specs=[row(D_MODEL), row(D_MODEL)],
        out_shape=[jax.ShapeDtypeStruct((t, D_MODEL), BF16),
                   jax.ShapeDtypeStruct((t, D_MODEL), BF16)],
        scratch_shapes=scratch,
        compiler_params=params,
        name="in_proj_gate",
    )(xp, xsm, norm1, w_in)
    return xb, z, xs, zmeta, xsmeta, sga, sgb


S5_GROUPS_PER_STEP = 4


def _s5_chunk_mats(lam_re, lam_im, log_dt, b_re, b_im, c_re, c_im):
    dt = jnp.exp(log_dt)[:, None]
    lr, li = lam_re, lam_im
    mag = jnp.exp(lr * dt)
    ab_re, ab_im = mag * jnp.cos(li * dt), mag * jnp.sin(li * dt)
    nr, ni = ab_re - 1.0, ab_im
    den = lr * lr + li * li
    k_re = (nr * lr + ni * li) / den
    k_im = (ni * lr - nr * li) / den
    bb_re = k_re[..., None] * b_re - k_im[..., None] * b_im
    bb_im = k_re[..., None] * b_im + k_im[..., None] * b_re
    d = jnp.arange(CHUNK + 1, dtype=F32)[:, None, None]
    pmag = jnp.exp(d * (lr * dt)[None])
    pw_re = pmag * jnp.cos(d * (li * dt)[None])
    pw_im = pmag * jnp.sin(d * (li * dt)[None])
    ab_b_re = pw_re[..., None] * bb_re[None] - pw_im[..., None] * bb_im[None]
    ab_b_im = pw_re[..., None] * bb_im[None] + pw_im[..., None] * bb_re[None]
    hi = lax.Precision.HIGHEST
    m = (jnp.einsum('gop,dgpi->dgoi', c_re, ab_b_re[:CHUNK], precision=hi)
         - jnp.einsum('gop,dgpi->dgoi', c_im, ab_b_im[:CHUNK], precision=hi))
    jj = jnp.arange(CHUNK)[:, None]
    tt = jnp.arange(CHUNK)[None, :]
    lag = tt - jj
    toe = m[jnp.clip(lag, 0, CHUNK - 1)]
    toe = jnp.where((lag >= 0)[:, :, None, None, None], toe, 0.0)
    toe = jnp.transpose(toe, (2, 0, 4, 1, 3)).reshape(SSM_G, CHUNK_W, CHUNK_W)
    p_re = jnp.transpose(ab_b_re[:CHUNK][::-1], (1, 0, 3, 2)).reshape(SSM_G, CHUNK_W, SSM_P)
    p_im = jnp.transpose(ab_b_im[:CHUNK][::-1], (1, 0, 3, 2)).reshape(SSM_G, CHUNK_W, SSM_P)
    pr, pi = pw_re[1:], pw_im[1:]
    q_re = c_re[None] * pr[:, :, None, :] - c_im[None] * pi[:, :, None, :]
    q_im = -(c_re[None] * pi[:, :, None, :] + c_im[None] * pr[:, :, None, :])
    q_re = jnp.transpose(q_re, (1, 3, 0, 2)).reshape(SSM_G, SSM_P, CHUNK_W)
    q_im = jnp.transpose(q_im, (1, 3, 0, 2)).reshape(SSM_G, SSM_P, CHUNK_W)
    a16_re, a16_im = pw_re[CHUNK][:, None, :], pw_im[CHUNK][:, None, :]
    return (toe.astype(BF16), p_re.astype(BF16), p_im.astype(BF16),
            q_re.astype(BF16), q_im.astype(BF16), a16_re, a16_im)


def _s5_body(n_pc, n_pb, n_sb, u_ref, toe_ref, pre_ref, pim_ref, qre_ref, qim_ref,
             are_ref, aim_ref, s0re_ref, s0im_ref,
             y_out, pfre_out, pfim_out, sfre_out, sfim_out,
             slre, slim, spre, spim):
    gb = S5_GROUPS_PER_STEP
    row_s = n_pc * n_pb
    row_m = row_s + n_sb
    for k in range(gb):
        u = u_ref[k]
        slre[k] = jnp.dot(u, pre_ref[k], preferred_element_type=F32)
        slim[k] = jnp.dot(u, pim_ref[k], preferred_element_type=F32)
        spre[k] = jnp.zeros(spre.shape[1:], F32)
        spim[k] = jnp.zeros(spim.shape[1:], F32)

    ar = [are_ref[k] for k in range(gb)]
    ai = [aim_ref[k] for k in range(gb)]
    sre = [jnp.broadcast_to(slre[k, row_m:row_m + 1, :], (n_pb, SSM_P)) for k in range(gb)]
    sim = [jnp.broadcast_to(slim[k, row_m:row_m + 1, :], (n_pb, SSM_P)) for k in range(gb)]
    for c in range(n_pc):
        r0 = c * n_pb
        for k in range(gb):
            spre[k, r0:r0 + n_pb, :] = sre[k]
            spim[k, r0:r0 + n_pb, :] = sim[k]
            nre = ar[k] * sre[k] - ai[k] * sim[k] + slre[k, r0:r0 + n_pb, :]
            nim = ar[k] * sim[k] + ai[k] * sre[k] + slim[k, r0:r0 + n_pb, :]
            sre[k], sim[k] = nre, nim
    for k in range(gb):
        pfre_out[k] = sre[k]
        pfim_out[k] = sim[k]
        s0r, s0i = s0re_ref[k], s0im_ref[k]
        spre[k, row_s:row_m, :] = s0r
        spim[k, row_s:row_m, :] = s0i
        sfre_out[k] = ar[k] * s0r - ai[k] * s0i + slre[k, row_s:row_m, :]
        sfim_out[k] = ar[k] * s0i + ai[k] * s0r + slim[k, row_s:row_m, :]
        y = jnp.dot(u_ref[k], toe_ref[k], preferred_element_type=F32)
        y += jnp.dot(spre[k].astype(BF16), qre_ref[k], preferred_element_type=F32)
        y += jnp.dot(spim[k].astype(BF16), qim_ref[k], preferred_element_type=F32)
        y_out[k] = y


def _s5(xs, xsmeta, state_re, state_im, mats, n_pb, n_sb, seq):
    toe, p_re, p_im, q_re, q_im, a_re, a_im = mats
    t_p = n_pb * seq
    n_pc = seq // CHUNK
    rows = n_pc * n_pb + n_sb + 1
    rows_pad = -(-rows // 16) * 16
    up = xs[:t_p].reshape(n_pb, n_pc, CHUNK, SSM_G, SSM_H)
    up = jnp.transpose(up, (3, 1, 0, 2, 4)).reshape(SSM_G, n_pc * n_pb, CHUNK_W)
    us = xs[t_p:].reshape(n_sb, CHUNK, SSM_G, SSM_H)
    us = jnp.transpose(us, (2, 0, 1, 3)).reshape(SSM_G, n_sb, CHUNK_W)
    um = jnp.transpose(xsmeta.reshape(CHUNK, SSM_G, SSM_H), (1, 0, 2)).reshape(SSM_G, 1, CHUNK_W)
    u = jnp.concatenate([up, us, um, jnp.zeros((SSM_G, rows_pad - rows, CHUNK_W), BF16)], axis=1)
    s0re = jnp.transpose(state_re, (1, 0, 2))
    s0im = jnp.transpose(state_im, (1, 0, 2))

    gb = S5_GROUPS_PER_STEP
    blk = lambda r, c: pl.BlockSpec((gb, r, c), lambda i: (i, 0, 0))
    y, pfre, pfim, sfre, sfim = pl.pallas_call(
        functools.partial(_s5_body, n_pc, n_pb, n_sb),
        grid=(SSM_G // gb,),
        in_specs=[blk(rows_pad, CHUNK_W), blk(CHUNK_W, CHUNK_W), blk(CHUNK_W, SSM_P), blk(CHUNK_W, SSM_P),
                  blk(SSM_P, CHUNK_W), blk(SSM_P, CHUNK_W), blk(1, SSM_P), blk(1, SSM_P),
                  blk(n_sb, SSM_P), blk(n_sb, SSM_P)],
        out_specs=[blk(rows_pad, CHUNK_W), blk(n_pb, SSM_P), blk(n_pb, SSM_P), blk(n_sb, SSM_P), blk(n_sb, SSM_P)],
        out_shape=[jax.ShapeDtypeStruct((SSM_G, rows_pad, CHUNK_W), F32),
                   jax.ShapeDtypeStruct((SSM_G, n_pb, SSM_P), F32),
                   jax.ShapeDtypeStruct((SSM_G, n_pb, SSM_P), F32),
                   jax.ShapeDtypeStruct((SSM_G, n_sb, SSM_P), F32),
                   jax.ShapeDtypeStruct((SSM_G, n_sb, SSM_P), F32)],
        scratch_shapes=[pltpu.VMEM((gb, rows_pad, SSM_P), F32) for _ in range(4)],
        compiler_params=pltpu.CompilerParams(dimension_semantics=("arbitrary",), vmem_limit_bytes=VMEM_LIMIT),
        name="s5_chunks",
    )(u, toe, p_re, p_im, q_re, q_im, a_re, a_im, s0re, s0im)

    yp = y[:, :n_pc * n_pb].reshape(SSM_G, n_pc, n_pb, CHUNK, SSM_H)
    yp = jnp.transpose(yp, (2, 1, 3, 0, 4)).reshape(t_p, D_SSM)
    ys = y[:, n_pc * n_pb:n_pc * n_pb + n_sb].reshape(SSM_G, n_sb, CHUNK, SSM_H)
    ys = jnp.transpose(ys, (1, 2, 0, 3)).reshape(n_sb * CHUNK, D_SSM)
    y_tok = jnp.concatenate([yp, ys], axis=0)
    tr = lambda a: jnp.transpose(a, (1, 0, 2))[None]
    return y_tok, tr(pfre), tr(pfim), tr(sfre), tr(sfim)


def _mixers_body(n_p_tiles, tiles_per_seq, xb_ref, z_ref, y5_ref, xs_ref, sga_ref, sgb_ref,
                 zmeta_ref, inj1_ref, inj2_ref, cw_ref, dskip_ref,
                 wc_hbm, wg_hbm, wso_hbm, merged_out,
                 wc, wg, wso, stage_a, stage_b, sem, carry):
    i = pl.program_id(0)

    @pl.when(i == 0)
    def _():
        _load_weight_bf16(wc_hbm, wc, stage_a, sem)
        _load_weight_bf16(wg_hbm, wg, stage_b, sem)
        _load_weight_bf16(wso_hbm, wso, stage_a, sem)

    @pl.when(jnp.logical_and(i < n_p_tiles, i % tiles_per_seq == 0))
    def _():
        carry[0:2, :] = zmeta_ref[N_META - 2:N_META, :]

    z = z_ref[...]
    row = lax.broadcasted_iota(jnp.int32, (TM, 1), 0)
    is_s = i >= n_p_tiles
    r1 = pltpu.roll(z, 1, 0)
    r2 = pltpu.roll(z, 2, 0)
    c1 = carry[1:2, :]
    c2 = carry[0:1, :]
    pos = jnp.where(is_s, row & (CHUNK - 1), row)
    first1 = pos == 0
    first2 = pos < 2
    fill1 = jnp.where(is_s, inj1_ref[...], jnp.broadcast_to(c1, z.shape))
    fill2 = jnp.where(is_s, inj2_ref[...], jnp.where(row == 0, c2, c1))
    zp1 = jnp.where(first1, fill1, r1)
    zp2 = jnp.where(first2, fill2, r2)
    carry[0:2, :] = z[TM - 2:TM, :]

    cw = cw_ref[...]
    conv = cw[0:1, :] * zp2 + cw[1:2, :] * zp1 + cw[2:3, :] * z
    a_in = (xb_ref[...].astype(F32) * conv).astype(BF16)
    ya = jnp.dot(a_in, wc[...], preferred_element_type=F32)

    ys = y5_ref[...] + dskip_ref[...] * xs_ref[...].astype(F32)
    ys = _gelu_tanh(ys)
    glu = jnp.dot(ys.astype(BF16), wg[...], preferred_element_type=F32)
    ys = ys * _sigmoid(glu)
    yb = jnp.dot(ys.astype(BF16), wso[...], preferred_element_type=F32)

    merged = sga_ref[...].astype(F32) * ya + sgb_ref[...].astype(F32) * yb
    merged_out[...] = merged.astype(BF16)


def _mixers(xb, z, y5, xs, sga, sgb, zmeta, inj1, inj2, conv_w, d_skip, w_conv_out, w_glu, w_ssm_out,
            n_p_tiles, tiles_per_seq):
    t = xb.shape[0]
    n_s_tiles = inj1.shape[0] // TM
    row = lambda w: pl.BlockSpec((TM, w), lambda i: (i, 0))
    const = lambda r, w: pl.BlockSpec((r, w), lambda i: (0, 0))
    inj = pl.BlockSpec((TM, D_CONV), lambda i: (jnp.clip(i - n_p_tiles, 0, n_s_tiles - 1), 0))
    any_spec = pl.BlockSpec(memory_space=pl.ANY)
    return pl.pallas_call(
        functools.partial(_mixers_body, n_p_tiles, tiles_per_seq),
        grid=(t // TM,),
        in_specs=[row(D_CONV), row(D_CONV), row(D_SSM), row(D_SSM), row(D_MODEL), row(D_MODEL),
                  const(N_META, D_CONV), inj, inj, const(3, D_CONV), const(1, D_SSM),
                  any_spec, any_spec, any_spec],
        out_specs=row(D_MODEL),
        out_shape=jax.ShapeDtypeStruct((t, D_MODEL), BF16),
        scratch_shapes=[pltpu.VMEM((D_CONV, D_MODEL), BF16),
                        pltpu.VMEM((D_SSM, D_SSM), BF16),
                        pltpu.VMEM((D_SSM, D_MODEL), BF16),
                        pltpu.VMEM((2, 256, D_MODEL), F32),
                        pltpu.VMEM((2, 256, D_SSM), F32),
                        pltpu.SemaphoreType.DMA((2,)),
                        pltpu.VMEM((8, D_CONV), F32)],
        compiler_params=pltpu.CompilerParams(dimension_semantics=("arbitrary",), vmem_limit_bytes=VMEM_LIMIT),
        name="mixers",
    )(xb, z, y5, xs, sga, sgb, zmeta, inj1, inj2, conv_w, d_skip, w_conv_out, w_glu, w_ssm_out)


ROUTE_W = 128
COARSE0 = N_EXPERTS


def _route(logits):
    col = lax.broadcasted_iota(jnp.int32, logits.shape, 1)
    colf = col.astype(F32)
    neg = jnp.float32(-jnp.inf)
    big = jnp.float32(1 << 20)
    is_c = jnp.logical_and(col >= COARSE0, col < COARSE0 + N_EGROUPS)
    lc = jnp.where(is_c, logits, neg)
    cmax = jnp.max(lc, axis=-1, keepdims=True)
    gi = jnp.min(jnp.where(lc == cmax, colf - COARSE0, big), axis=-1, keepdims=True)
    pg = 1.0 / jnp.sum(jnp.where(is_c, jnp.exp(lc - cmax), 0.0), axis=-1, keepdims=True)
    grp = (col >> 3).astype(F32)
    in_g = jnp.logical_and(col < N_EXPERTS, grp == gi)
    lf = jnp.where(in_g, logits, neg)
    m1 = jnp.max(lf, axis=-1, keepdims=True)
    i1 = jnp.min(jnp.where(lf == m1, colf, big), axis=-1, keepdims=True)
    lf2 = jnp.where(colf == i1, neg, lf)
    m2 = jnp.max(lf2, axis=-1, keepdims=True)
    i2 = jnp.min(jnp.where(lf2 == m2, colf, big), axis=-1, keepdims=True)
    e2 = jnp.exp(m2 - m1)
    w1 = pg / (1.0 + e2)
    w2 = pg * e2 / (1.0 + e2)
    rec = jnp.where(col == 0, i1, jnp.where(col == 1, i2, jnp.where(col == 2, w1, jnp.where(col == 3, w2, 0.0))))
    return rec


def _out_proj_body(n_p_tiles, merged_ref, xp_ref, xsm_ref, g2_ref, wr_ref, wo_hbm,
                   h2_out, v_out, rec_out, wo, stage, sem):
    i = pl.program_id(0)

    @pl.when(i == 0)
    def _():
        _load_weight_bf16(wo_hbm, wo, stage, sem)

    x = jnp.where(i < n_p_tiles, xp_ref[...], xsm_ref[...])
    h2 = x + jnp.dot(merged_ref[...], wo[...], preferred_element_type=F32)
    h2_out[...] = h2
    v = _rmsnorm(h2, g2_ref[...])
    v_out[...] = v
    logits = jnp.dot(v, wr_ref[...], preferred_element_type=F32, precision=lax.Precision.HIGHEST)
    rec_out[...] = _route(logits)


def _out_proj(merged, xp, xsm, norm2, w_router, w_o):
    t_p, t_s = xp.shape[0], xsm.shape[0]
    n_p, n_s = t_p // TM, t_s // TM
    t = t_p + t_s
    xp_spec, xs_spec = _two_stream_specs(n_p, n_s)
    row = lambda w: pl.BlockSpec((TM, w), lambda i: (i, 0))
    const = lambda r, w: pl.BlockSpec((r, w), lambda i: (0, 0))
    return pl.pallas_call(
        functools.partial(_out_proj_body, n_p),
        grid=(n_p + n_s,),
        in_specs=[row(D_MODEL), xp_spec, xs_spec, const(1, D_MODEL), const(D_MODEL, ROUTE_W),
                  pl.BlockSpec(memory_space=pl.ANY)],
        out_specs=[row(D_MODEL), row(D_MODEL), row(ROUTE_W)],
        out_shape=[jax.ShapeDtypeStruct((t, D_MODEL), F32),
                   jax.ShapeDtypeStruct((t, D_MODEL), F32),
                   jax.ShapeDtypeStruct((t, ROUTE_W), F32)],
        scratch_shapes=[pltpu.VMEM((D_MODEL, D_MODEL), BF16),
                        pltpu.VMEM((2, 256, D_MODEL), F32),
                        pltpu.SemaphoreType.DMA((2,))],
        compiler_params=pltpu.CompilerParams(dimension_semantics=("arbitrary",), vmem_limit_bytes=VMEM_LIMIT),
        name="out_proj_route",
    )(merged, xp, xsm, norm2, w_router, w_o)


def _moe_plan(rec, n_tiles):
    t = rec.shape[0]
    eid = rec[:, 0:2].astype(jnp.int32).reshape(-1)
    gate = rec[:, 2:4].reshape(-1)
    n_pairs = 2 * t
    onehot = (eid[:, None] == jnp.arange(N_EXPERTS, dtype=jnp.int32)[None, :]).astype(jnp.int32)
    csum = jnp.cumsum(onehot, axis=0)
    counts = csum[-1]
    rank = jnp.take_along_axis(csum, eid[:, None], axis=1)[:, 0] - 1
    tiles_e = (counts + TM_MOE - 1) // TM_MOE
    tile_end = jnp.cumsum(tiles_e)
    tile_start = tile_end - tiles_e
    n_valid = tile_end[-1]
    dest = tile_start[eid] * TM_MOE + rank
    pair_end = jnp.cumsum(counts)
    pair_start = pair_end - counts
    order = jnp.argsort(eid, stable=True).astype(jnp.int32)
    tile_ids = jnp.arange(n_tiles, dtype=jnp.int32)
    tile_e = jnp.minimum(jnp.searchsorted(tile_end, tile_ids, side='right'), N_EXPERTS - 1).astype(jnp.int32)
    last_e = tile_e[jnp.maximum(n_valid - 1, 0)]
    tile_e = jnp.where(tile_ids < n_valid, tile_e, last_e)
    r = jnp.arange(n_tiles * TM_MOE, dtype=jnp.int32)
    e_r = tile_e[r // TM_MOE]
    q = r - tile_start[e_r] * TM_MOE
    ok = jnp.logical_and(r // TM_MOE < n_valid, q < counts[e_r])
    pair_r = order[jnp.clip(pair_start[e_r] + q, 0, n_pairs - 1)]
    tok_r = jnp.where(ok, pair_r // 2, 0).astype(jnp.int32)
    gate_r = jnp.where(ok, gate[pair_r], 0.0)
    return tile_e, n_valid.astype(jnp.int32).reshape(1), tok_r, gate_r[:, None], dest.astype(jnp.int32)


def _row_copy(src_hbm, dst, sem, src_row, slot, dst_row):
    return pltpu.make_async_copy(
        src_hbm.at[pl.ds(src_row, 1), :], dst.at[slot, pl.ds(dst_row, 1), :], sem.at[slot])


def _moe_body(tile_e_ref, nvalid_ref, tok_ref, v_hbm, gate_ref, wg_ref, wu_ref, wd_ref, y_out,
              xbuf, sem, wg, wu, wd):
    i = pl.program_id(0)
    nv = nvalid_ref[0]

    def issue(tile, slot):
        def body(r, c):
            _row_copy(v_hbm, xbuf, sem, tok_ref[tile * TM_MOE + r], slot, r).start()
            return c
        lax.fori_loop(0, TM_MOE, body, 0, unroll=8)

    @pl.when(i == 0)
    def _():
        issue(0, 0)

    @pl.when(i + 1 < nv)
    def _():
        issue(i + 1, (i + 1) % 2)

    @pl.when(i < nv)
    def _():
        slot = i % 2

        def wbody(r, c):
            _row_copy(v_hbm, xbuf, sem, 0, slot, r).wait()
            return c
        lax.fori_loop(0, TM_MOE, wbody, 0, unroll=8)

        prev_e = tile_e_ref[jnp.maximum(i - 1, 0)]

        @pl.when(jnp.logical_or(i == 0, tile_e_ref[i] != prev_e))
        def _():
            wg[...] = wg_ref[0].astype(BF16)
            wu[...] = wu_ref[0].astype(BF16)
            wd[...] = wd_ref[0].astype(BF16)

        x = xbuf[slot].astype(BF16)
        hg = jnp.dot(x, wg[...], preferred_element_type=F32)
        hu = jnp.dot(x, wu[...], preferred_element_type=F32)
        act = hg * _sigmoid(hg) * hu * gate_ref[...]
        y_out[...] = jnp.dot(act.astype(BF16), wd[...], preferred_element_type=F32)

    @pl.when(i >= nv)
    def _():
        y_out[...] = jnp.zeros(y_out.shape, F32)


def _moe(v, plan, w_gate, w_up, w_down, n_tiles):
    tile_e, n_valid, tok_r, gate_r, _ = plan
    wspec = lambda a, b: pl.BlockSpec((1, a, b), lambda i, te, nv, tk: (te[i], 0, 0))
    clamp = lambda i, nv: jnp.minimum(i, jnp.maximum(nv[0] - 1, 0))
    return pl.pallas_call(
        _moe_body,
        grid_spec=pltpu.PrefetchScalarGridSpec(
            num_scalar_prefetch=3,
            grid=(n_tiles,),
            in_specs=[pl.BlockSpec(memory_space=pl.ANY),
                      pl.BlockSpec((TM_MOE, 1), lambda i, te, nv, tk: (clamp(i, nv), 0)),
                      wspec(D_MODEL, D_EXPERT), wspec(D_MODEL, D_EXPERT), wspec(D_EXPERT, D_MODEL)],
            out_specs=pl.BlockSpec((TM_MOE, D_MODEL), lambda i, te, nv, tk: (i, 0)),
            scratch_shapes=[pltpu.VMEM((2, TM_MOE, D_MODEL), F32),
                            pltpu.SemaphoreType.DMA((2,)),
                            pltpu.VMEM((D_MODEL, D_EXPERT), BF16),
                            pltpu.VMEM((D_MODEL, D_EXPERT), BF16),
                            pltpu.VMEM((D_EXPERT, D_MODEL), BF16)]),
        out_shape=jax.ShapeDtypeStruct((n_tiles * TM_MOE, D_MODEL), F32),
        compiler_params=pltpu.CompilerParams(dimension_semantics=("arbitrary",), vmem_limit_bytes=VMEM_LIMIT),
        name="moe_experts",
    )(tile_e, n_valid, tok_r, v, gate_r, w_gate, w_up, w_down)


def _combine_body(n_p_tiles, dest_ref, h2_ref, gf_ref, y_hbm, outp_ref, outs_ref, ybuf, sem):
    i = pl.program_id(0)
    n = pl.num_programs(0)

    def issue(tile, slot):
        def body(r, c):
            for k in range(2):
                _row_copy(y_hbm, ybuf, sem, dest_ref[2 * (tile * TM + r) + k], slot, k * TM + r).start()
            return c
        lax.fori_loop(0, TM, body, 0, unroll=8)

    @pl.when(i == 0)
    def _():
        issue(0, 0)

    @pl.when(i + 1 < n)
    def _():
        issue(i + 1, (i + 1) % 2)

    slot = i % 2

    def wbody(r, c):
        for k in range(2):
            _row_copy(y_hbm, ybuf, sem, 0, slot, k * TM + r).wait()
        return c
    lax.fori_loop(0, TM, wbody, 0, unroll=8)

    h = h2_ref[...] + ybuf[slot, 0:TM, :] + ybuf[slot, TM:2 * TM, :]
    out = _rmsnorm(h, gf_ref[...])

    @pl.when(i < n_p_tiles)
    def _():
        outp_ref[...] = out

    @pl.when(i >= n_p_tiles)
    def _():
        outs_ref[...] = out


def _combine(h2, y_sorted, dest, final_norm, t_p, t_s):
    n_p, n_s = t_p // TM, t_s // TM
    return pl.pallas_call(
        functools.partial(_combine_body, n_p),
        grid_spec=pltpu.PrefetchScalarGridSpec(
            num_scalar_prefetch=1,
            grid=(n_p + n_s,),
            in_specs=[pl.BlockSpec((TM, D_MODEL), lambda i, d: (i, 0)),
                      pl.BlockSpec((1, D_MODEL), lambda i, d: (0, 0)),
                      pl.BlockSpec(memory_space=pl.ANY)],
            out_specs=[pl.BlockSpec((TM, D_MODEL), lambda i, d: (jnp.minimum(i, n_p - 1), 0)),
                       pl.BlockSpec((TM, D_MODEL), lambda i, d: (jnp.clip(i - n_p, 0, n_s - 1), 0))],
            scratch_shapes=[pltpu.VMEM((2, 2 * TM, D_MODEL), F32),
                            pltpu.SemaphoreType.DMA((2,))]),
        out_shape=[jax.ShapeDtypeStruct((t_p, D_MODEL), F32),
                   jax.ShapeDtypeStruct((t_s, D_MODEL), F32)],
        compiler_params=pltpu.CompilerParams(dimension_semantics=("arbitrary",), vmem_limit_bytes=VMEM_LIMIT),
        name="combine_norm",
    )(dest, h2, final_norm, y_sorted)


def kernel(x_prompt, x_sample, state_conv, state_ssm_re, state_ssm_im, meta_tokens, norm1, w_in, conv_w,
           lam_re, lam_im, log_dt, ssm_b_re, ssm_b_im, ssm_c_re, ssm_c_im, ssm_d, w_glu, w_conv_out,
           w_ssm_out, w_o, norm2, w_coarse, w_fine, w_gate, w_up, w_down, final_norm):
    n_pb, seq, _ = x_prompt.shape
    n_sb, dec_seq, _ = x_sample.shape
    assert dec_seq == CHUNK and seq % TM == 0 and (n_sb * dec_seq) % TM == 0 and N_META == CHUNK
    t_p, t_s = n_pb * seq, n_sb * dec_seq
    xp = x_prompt.reshape(t_p, D_MODEL)
    xsm = x_sample.reshape(t_s, D_MODEL)

    xb, z, xs, zmeta, xsmeta, sga, sgb = _in_proj(xp, xsm, meta_tokens, norm1, w_in[0])

    mats = _s5_chunk_mats(lam_re[0], lam_im[0], log_dt[0], ssm_b_re[0], ssm_b_im[0], ssm_c_re[0], ssm_c_im[0])
    y5, pf_re, pf_im, sf_re, sf_im = _s5(xs, xsmeta, state_ssm_re[0], state_ssm_im[0], mats, n_pb, n_sb, seq)

    buf = state_conv[0]
    zero = jnp.zeros((n_sb, dec_seq, D_CONV), F32)
    inj1 = zero.at[:, 0].set(buf[:, 1]).reshape(t_s, D_CONV)
    inj2 = zero.at[:, 0].set(buf[:, 0]).at[:, 1].set(buf[:, 1]).reshape(t_s, D_CONV)
    merged = _mixers(xb, z, y5, xs, sga, sgb, zmeta, inj1, inj2, conv_w[0], ssm_d, w_conv_out[0], w_glu[0],
                     w_ssm_out[0], t_p // TM, seq // TM)

    w_router = jnp.concatenate(
        [w_fine[0], w_coarse[0], jnp.zeros((D_MODEL, ROUTE_W - N_EXPERTS - N_EGROUPS), F32)], axis=1)
    h2, v, rec = _out_proj(merged, xp, xsm, norm2, w_router, w_o[0])

    n_tiles = (2 * (t_p + t_s)) // TM_MOE + N_EXPERTS
    plan = _moe_plan(rec, n_tiles)
    y_sorted = _moe(v, plan, w_gate[0], w_up[0], w_down[0], n_tiles)
    y_p, y_s = _combine(h2, y_sorted, plan[4], final_norm.reshape(1, D_MODEL), t_p, t_s)

    z_p = z[:t_p].reshape(n_pb, seq, D_CONV)
    z_s = z[t_p:].reshape(n_sb, dec_seq, D_CONV)
    return (y_p.reshape(n_pb, seq, D_MODEL), y_s.reshape(n_sb, dec_seq, D_MODEL),
            z_p[:, seq - 2:][None], pf_re, pf_im,
            z_s[:, dec_seq - 2:][None], sf_re, sf_im)
```

```python
import functools

import jax
import jax.numpy as jnp
from jax import lax
from jax.experimental import pallas as pl
from jax.experimental.pallas import tpu as pltpu

F32 = jnp.float32
BF16 = jnp.bfloat16
I32 = jnp.int32

D_MODEL = 2048
D_CONV = 1024
D_SSM = 1024
SSM_H = 16
SSM_G = 64
SSM_P = 64
N_META = 16
N_EGROUPS = 4
EXPERTS_PER_GROUP = 8
N_EXPERTS = 32
D_EXPERT = 256
EPS = 1e-6

LANES = 128
ROW_TILES = D_MODEL // LANES

CHUNK = 16
CHUNK_W = CHUNK * SSM_H

TM = 256
TM_MOE = 256
VMEM_LIMIT = 52 * 1024 * 1024


def _rmsnorm(x, g):
    return x * lax.rsqrt(jnp.mean(x * x, axis=-1, keepdims=True) + EPS) * g


def _sigmoid(x):
    return 1.0 / (1.0 + jnp.exp(-x))


def _gelu_tanh(x):
    c = 0.7978845608028654
    return 0.5 * x * (1.0 + jnp.tanh(c * (x + 0.044715 * (x * x * x))))


def _weight_copy(w_hbm, stage, sem, c, slot, rows, col0, ncols):
    return pltpu.make_async_copy(
        w_hbm.at[pl.ds(c * rows, rows), pl.ds(col0, ncols)], stage.at[slot], sem.at[slot])


def _load_weight_bf16(w_hbm, w_vmem, stage, sem, col0=0):
    k, n = w_vmem.shape
    rows = stage.shape[1]
    nchunk = k // rows
    _weight_copy(w_hbm, stage, sem, 0, 0, rows, col0, n).start()
    for c in range(nchunk):
        slot = c % 2
        if c + 1 < nchunk:
            _weight_copy(w_hbm, stage, sem, c + 1, 1 - slot, rows, col0, n).start()
        _weight_copy(w_hbm, stage, sem, c, slot, rows, col0, n).wait()
        w_vmem[pl.ds(c * rows, rows), :] = stage[slot].astype(BF16)


def _in_proj_mix_body(n_p_tiles, xp_ref, xsm_ref, meta_ref, g_ref, w_hbm,
                      xb_out, z_out, xs_out, zmeta_out, xsmeta_out,
                      w_vmem, stage, sem):
    i = pl.program_id(0)
    g = g_ref[...]

    def project(u):
        xb = jnp.dot(u, w_vmem[:, 0:D_CONV], preferred_element_type=F32)
        xc = jnp.dot(u, w_vmem[:, D_CONV:2 * D_CONV], preferred_element_type=F32)
        xv = jnp.dot(u, w_vmem[:, 2 * D_CONV:3 * D_CONV], preferred_element_type=F32)
        xs = jnp.dot(u, w_vmem[:, 3 * D_CONV:3 * D_CONV + D_SSM], preferred_element_type=F32)
        return xb, xc * xv, xs

    @pl.when(i == 0)
    def _():
        _load_weight_bf16(w_hbm, w_vmem, stage, sem, col0=0)
        um = _rmsnorm(meta_ref[...], g).astype(BF16)
        _, zm, xsm = project(um)
        zmeta_out[...] = zm
        xsmeta_out[...] = xsm

    x = jnp.where(i < n_p_tiles, xp_ref[...], xsm_ref[...])
    u = _rmsnorm(x, g).astype(BF16)
    xb, z, xs = project(u)
    xb_out[...] = xb.astype(BF16)
    z_out[...] = z
    xs_out[...] = xs


def _in_proj_gate_body(n_p_tiles, xp_ref, xsm_ref, g_ref, w_hbm, ga_out, gb_out,
                       w_vmem, stage, sem):
    i = pl.program_id(0)

    @pl.when(i == 0)
    def _():
        _load_weight_bf16(w_hbm, w_vmem, stage, sem, col0=3 * D_CONV + D_SSM)

    x = jnp.where(i < n_p_tiles, xp_ref[...], xsm_ref[...])
    u = _rmsnorm(x, g_ref[...]).astype(BF16)
    ga = jnp.dot(u, w_vmem[:, 0:D_MODEL], preferred_element_type=F32)
    ga_out[...] = _sigmoid(ga).astype(BF16)
    gb = jnp.dot(u, w_vmem[:, D_MODEL:2 * D_MODEL], preferred_element_type=F32)
    gb_out[...] = _sigmoid(gb).astype(BF16)


def _two_stream_specs(n_p_tiles, n_s_tiles):
    xp_spec = pl.BlockSpec((TM, D_MODEL), lambda i: (jnp.minimum(i, n_p_tiles - 1), 0))
    xs_spec = pl.BlockSpec((TM, D_MODEL), lambda i: (jnp.clip(i - n_p_tiles, 0, n_s_tiles - 1), 0))
    return xp_spec, xs_spec


def _in_proj(xp, xsm, meta, norm1, w_in):
    t_p, t_s = xp.shape[0], xsm.shape[0]
    n_p, n_s = t_p // TM, t_s // TM
    t = t_p + t_s
    half = 3 * D_CONV + D_SSM
    xp_spec, xs_spec = _two_stream_specs(n_p, n_s)
    g_spec = pl.BlockSpec((1, D_MODEL), lambda i: (0, 0))
    any_spec = pl.BlockSpec(memory_space=pl.ANY)
    stage_rows = 128
    row = lambda w: pl.BlockSpec((TM, w), lambda i: (i, 0))
    const = lambda r, w: pl.BlockSpec((r, w), lambda i: (0, 0))
    params = pltpu.CompilerParams(dimension_semantics=("arbitrary",), vmem_limit_bytes=VMEM_LIMIT)
    scratch = [pltpu.VMEM((D_MODEL, half), BF16),
               pltpu.VMEM((2, stage_rows, half), F32),
               pltpu.SemaphoreType.DMA((2,))]

    xb, z, xs, zmeta, xsmeta = pl.pallas_call(
        functools.partial(_in_proj_mix_body, n_p),
        grid=(n_p + n_s,),
        in_specs=[xp_spec, xs_spec, const(N_META, D_MODEL), g_spec, any_spec],
        out_specs=[row(D_CONV), row(D_CONV), row(D_SSM), const(N_META, D_CONV), const(N_META, D_SSM)],
        out_shape=[jax.ShapeDtypeStruct((t, D_CONV), BF16),
                   jax.ShapeDtypeStruct((t, D_CONV), F32),
                   jax.ShapeDtypeStruct((t, D_SSM), F32),
                   jax.ShapeDtypeStruct((N_META, D_CONV), F32),
                   jax.ShapeDtypeStruct((N_META, D_SSM), F32)],
        scratch_shapes=scratch,
        compiler_params=params,
        name="in_proj_mix",
    )(xp, xsm, meta, norm1, w_in)

    sga, sgb = pl.pallas_call(
        functools.partial(_in_proj_gate_body, n_p),
        grid=(n_p + n_s,),
        in_specs=[xp_spec, xs_spec, g_spec, any_spec],
        out_specs=[row(D_MODEL), row(D_MODEL)],
        out_shape=[jax.ShapeDtypeStruct((t, D_MODEL), BF16),
                   jax.ShapeDtypeStruct((t, D_MODEL), BF16)],
        scratch_shapes=scratch,
        compiler_params=params,
        name="in_proj_gate",
    )(xp, xsm, norm1, w_in)
    return xb, z, xs, zmeta, xsmeta, sga, sgb


S5_GROUPS_PER_STEP = LANES // SSM_H
S5_PAIRS_PER_STEP = S5_GROUPS_PER_STEP // 2


def _s5_chunk_mats(lam_re, lam_im, log_dt, b_re, b_im, c_re, c_im):
    dt = jnp.exp(log_dt)[:, None]
    lr, li = lam_re, lam_im
    mag = jnp.exp(lr * dt)
    ab_re, ab_im = mag * jnp.cos(li * dt), mag * jnp.sin(li * dt)
    nr, ni = ab_re - 1.0, ab_im
    den = lr * lr + li * li
    k_re = (nr * lr + ni * li) / den
    k_im = (ni * lr - nr * li) / den
    bb_re = k_re[..., None] * b_re - k_im[..., None] * b_im
    bb_im = k_re[..., None] * b_im + k_im[..., None] * b_re
    d = jnp.arange(CHUNK + 1, dtype=F32)[:, None, None]
    pmag = jnp.exp(d * (lr * dt)[None])
    pw_re = pmag * jnp.cos(d * (li * dt)[None])
    pw_im = pmag * jnp.sin(d * (li * dt)[None])
    ab_b_re = pw_re[..., None] * bb_re[None] - pw_im[..., None] * bb_im[None]
    ab_b_im = pw_re[..., None] * bb_im[None] + pw_im[..., None] * bb_re[None]
    m = jnp.sum(c_re[None, :, :, :, None] * ab_b_re[:CHUNK, :, None, :, :]
                - c_im[None, :, :, :, None] * ab_b_im[:CHUNK, :, None, :, :], axis=3)
    rows = []
    zero = jnp.zeros((SSM_G, SSM_H, SSM_H), F32)
    for t in range(CHUNK):
        rows.append(jnp.concatenate([m[t - j] if t >= j else zero for j in range(CHUNK)], axis=2))
    toe_t = jnp.concatenate(rows, axis=1)
    p_re = jnp.transpose(ab_b_re[:CHUNK][::-1], (1, 2, 0, 3)).reshape(SSM_G, SSM_P, CHUNK_W)
    p_im = jnp.transpose(ab_b_im[:CHUNK][::-1], (1, 2, 0, 3)).reshape(SSM_G, SSM_P, CHUNK_W)
    p_t = jnp.concatenate([p_re, p_im], axis=1)
    pr, pi = pw_re[1:], pw_im[1:]
    q_re = c_re[None] * pr[:, :, None, :] - c_im[None] * pi[:, :, None, :]
    q_im = -(c_re[None] * pi[:, :, None, :] + c_im[None] * pr[:, :, None, :])
    q_re = jnp.transpose(q_re, (1, 0, 2, 3)).reshape(SSM_G, CHUNK_W, SSM_P)
    q_im = jnp.transpose(q_im, (1, 0, 2, 3)).reshape(SSM_G, CHUNK_W, SSM_P)
    even = (jnp.arange(SSM_G) % 2 == 0)[:, None, None]
    zq = jnp.zeros_like(q_re)
    pad2 = lambda q: jnp.where(even, jnp.concatenate([q, zq], axis=2), jnp.concatenate([zq, q], axis=2))
    q_t = jnp.stack([pad2(q_re), pad2(q_im)], axis=1)
    a16_re = pw_re[CHUNK].reshape(SSM_G // 2, 1, 2 * SSM_P)
    a16_im = pw_im[CHUNK].reshape(SSM_G // 2, 1, 2 * SSM_P)
    return toe_t.astype(BF16), p_t.astype(BF16), q_t.astype(BF16), a16_re, a16_im


def _s5_body(n_pc, n_pb, n_sb, xs_ref, xsmeta_ref, toe_ref, p_ref, q_ref, are_ref, aim_ref, s0re_ref, s0im_ref,
             y_out, pfre_out, pfim_out, sfre_out, sfim_out,
             xt_scr, u_scr, st_re, st_im, sl_re, sl_im, sp_re, sp_im, yt_scr):
    gb = S5_GROUPS_PER_STEP
    n_p_rows = n_pb * n_pc
    row_s = n_p_rows
    row_m = row_s + n_sb
    t_p = n_p_rows * CHUNK
    rows_pad = xt_scr.shape[0]

    xt_scr[row_m + 1:rows_pad, :] = jnp.zeros((rows_pad - row_m - 1, LANES), F32)
    for t in range(CHUNK):
        xt_scr[0:n_p_rows, :] = xs_ref[pl.ds(t, n_p_rows, stride=CHUNK), :]
        xt_scr[row_s:row_m, :] = xs_ref[pl.ds(t_p + t, n_sb, stride=CHUNK), :]
        xt_scr[row_m:row_m + 1, :] = xsmeta_ref[t:t + 1, :]
        xt = xt_scr[...].T.astype(BF16)
        for k in range(gb):
            u_scr[k, t * SSM_H:(t + 1) * SSM_H, :] = xt[k * SSM_H:(k + 1) * SSM_H, :]

    for k in range(gb):
        sl = jnp.dot(p_ref[k], u_scr[k], preferred_element_type=F32)
        half = (k % 2) * SSM_P
        st_re[k // 2, half:half + SSM_P, :] = sl[0:SSM_P, :]
        st_im[k // 2, half:half + SSM_P, :] = sl[SSM_P:2 * SSM_P, :]
    npair = S5_PAIRS_PER_STEP
    for j in range(npair):
        sl_re[j] = st_re[j].T
        sl_im[j] = st_im[j].T
        sp_re[j, row_m:rows_pad, :] = jnp.zeros((rows_pad - row_m, 2 * SSM_P), F32)
        sp_im[j, row_m:rows_pad, :] = jnp.zeros((rows_pad - row_m, 2 * SSM_P), F32)

    ar = [are_ref[j] for j in range(npair)]
    ai = [aim_ref[j] for j in range(npair)]
    sre = [jnp.broadcast_to(sl_re[j, row_m:row_m + 1, :], (n_pb, 2 * SSM_P)) for j in range(npair)]
    sim = [jnp.broadcast_to(sl_im[j, row_m:row_m + 1, :], (n_pb, 2 * SSM_P)) for j in range(npair)]
    for c in range(n_pc):
        rows = pl.ds(c, n_pb, stride=n_pc)
        for j in range(npair):
            sp_re[j, rows, :] = sre[j]
            sp_im[j, rows, :] = sim[j]
            nre = ar[j] * sre[j] - ai[j] * sim[j] + sl_re[j, rows, :]
            nim = ar[j] * sim[j] + ai[j] * sre[j] + sl_im[j, rows, :]
            sre[j], sim[j] = nre, nim
    for j in range(npair):
        pfre_out[j] = sre[j]
        pfim_out[j] = sim[j]
        s0r, s0i = s0re_ref[j], s0im_ref[j]
        sp_re[j, row_s:row_m, :] = s0r
        sp_im[j, row_s:row_m, :] = s0i
        sfre_out[j] = ar[j] * s0r - ai[j] * s0i + sl_re[j, row_s:row_m, :]
        sfim_out[j] = ar[j] * s0i + ai[j] * s0r + sl_im[j, row_s:row_m, :]

    nt = (((1,), (1,)), ((), ()))
    for k in range(gb):
        y = jnp.dot(toe_ref[k], u_scr[k], preferred_element_type=F32)
        y += lax.dot_general(q_ref[k, 0], sp_re[k // 2].astype(BF16), nt, preferred_element_type=F32)
        y += lax.dot_general(q_ref[k, 1], sp_im[k // 2].astype(BF16), nt, preferred_element_type=F32)
        for t in range(CHUNK):
            yt_scr[t, k * SSM_H:(k + 1) * SSM_H, :] = y[t * SSM_H:(t + 1) * SSM_H, :]
    for t in range(CHUNK):
        yt = yt_scr[t].T
        y_out[pl.ds(t, n_p_rows, stride=CHUNK), :] = yt[0:n_p_rows, :]
        y_out[pl.ds(t_p + t, n_sb, stride=CHUNK), :] = yt[row_s:row_m, :]


def _s5(xs, xsmeta, state_re, state_im, mats, n_pb, n_sb, seq):
    toe_t, p_t, q_t, a_re, a_im = mats
    t = xs.shape[0]
    n_pc = seq // CHUNK
    rows = n_pc * n_pb + n_sb + 1
    rows_pad = -(-rows // LANES) * LANES
    gb, npair = S5_GROUPS_PER_STEP, S5_PAIRS_PER_STEP
    pairs = lambda s: jnp.transpose(s.reshape(n_sb, SSM_G // 2, 2 * SSM_P), (1, 0, 2))
    blk3 = lambda n, r, c: pl.BlockSpec((n, r, c), lambda i: (i, 0, 0))
    y, pfre, pfim, sfre, sfim = pl.pallas_call(
        functools.partial(_s5_body, n_pc, n_pb, n_sb),
        grid=(SSM_G // gb,),
        in_specs=[pl.BlockSpec((t, LANES), lambda i: (0, i)),
                  pl.BlockSpec((N_META, LANES), lambda i: (0, i)),
                  blk3(gb, CHUNK_W, CHUNK_W), blk3(gb, 2 * SSM_P, CHUNK_W),
                  pl.BlockSpec((gb, 2, CHUNK_W, 2 * SSM_P), lambda i: (i, 0, 0, 0)),
                  blk3(npair, 1, 2 * SSM_P), blk3(npair, 1, 2 * SSM_P),
                  blk3(npair, n_sb, 2 * SSM_P), blk3(npair, n_sb, 2 * SSM_P)],
        out_specs=[pl.BlockSpec((t, LANES), lambda i: (0, i)),
                   blk3(npair, n_pb, 2 * SSM_P), blk3(npair, n_pb, 2 * SSM_P),
                   blk3(npair, n_sb, 2 * SSM_P), blk3(npair, n_sb, 2 * SSM_P)],
        out_shape=[jax.ShapeDtypeStruct((t, D_SSM), F32),
                   jax.ShapeDtypeStruct((SSM_G // 2, n_pb, 2 * SSM_P), F32),
                   jax.ShapeDtypeStruct((SSM_G // 2, n_pb, 2 * SSM_P), F32),
                   jax.ShapeDtypeStruct((SSM_G // 2, n_sb, 2 * SSM_P), F32),
                   jax.ShapeDtypeStruct((SSM_G // 2, n_sb, 2 * SSM_P), F32)],
        scratch_shapes=[pltpu.VMEM((rows_pad, LANES), F32),
                        pltpu.VMEM((gb, CHUNK_W, rows_pad), BF16),
                        pltpu.VMEM((npair, 2 * SSM_P, rows_pad), F32),
                        pltpu.VMEM((npair, 2 * SSM_P, rows_pad), F32),
                        pltpu.VMEM((npair, rows_pad, 2 * SSM_P), F32),
                        pltpu.VMEM((npair, rows_pad, 2 * SSM_P), F32),
                        pltpu.VMEM((npair, rows_pad, 2 * SSM_P), F32),
                        pltpu.VMEM((npair, rows_pad, 2 * SSM_P), F32),
                        pltpu.VMEM((CHUNK, LANES, rows_pad), F32)],
        compiler_params=pltpu.CompilerParams(dimension_semantics=("arbitrary",), vmem_limit_bytes=VMEM_LIMIT),
        name="s5_chunks",
    )(xs, xsmeta, toe_t, p_t, q_t, a_re, a_im, pairs(state_re), pairs(state_im))
    unpair = lambda a: jnp.transpose(a, (1, 0, 2)).reshape(a.shape[1], SSM_G, SSM_P)[None]
    return y, unpair(pfre), unpair(pfim), unpair(sfre), unpair(sfim)


def _mixers_body(n_p_tiles, tiles_per_seq, xb_ref, z_ref, y5_ref, xs_ref, sga_ref, sgb_ref,
                 zmeta_ref, inj1_ref, inj2_ref, cw_ref, dskip_ref,
                 wc_hbm, wg_hbm, wso_hbm, merged_out,
                 wc, wg, wso, stage_a, stage_b, sem, carry):
    i = pl.program_id(0)

    @pl.when(i == 0)
    def _():
        _load_weight_bf16(wc_hbm, wc, stage_a, sem)
        _load_weight_bf16(wg_hbm, wg, stage_b, sem)
        _load_weight_bf16(wso_hbm, wso, stage_a, sem)

    @pl.when(jnp.logical_and(i < n_p_tiles, i % tiles_per_seq == 0))
    def _():
        carry[0:2, :] = zmeta_ref[N_META - 2:N_META, :]

    z = z_ref[...]
    row = lax.broadcasted_iota(I32, (TM, 1), 0)
    is_s = i >= n_p_tiles
    r1 = pltpu.roll(z, 1, 0)
    r2 = pltpu.roll(z, 2, 0)
    c1 = carry[1:2, :]
    c2 = carry[0:1, :]
    pos = jnp.where(is_s, row & (CHUNK - 1), row)
    first1 = pos == 0
    first2 = pos < 2
    fill1 = jnp.where(is_s, inj1_ref[...], jnp.broadcast_to(c1, z.shape))
    fill2 = jnp.where(is_s, inj2_ref[...], jnp.where(row == 0, c2, c1))
    zp1 = jnp.where(first1, fill1, r1)
    zp2 = jnp.where(first2, fill2, r2)
    carry[0:2, :] = z[TM - 2:TM, :]

    cw = cw_ref[...]
    conv = cw[0:1, :] * zp2 + cw[1:2, :] * zp1 + cw[2:3, :] * z
    a_in = (xb_ref[...].astype(F32) * conv).astype(BF16)
    ya = jnp.dot(a_in, wc[...], preferred_element_type=F32)

    ys = y5_ref[...] + dskip_ref[...] * xs_ref[...]
    ys = _gelu_tanh(ys)
    glu = jnp.dot(ys.astype(BF16), wg[...], preferred_element_type=F32)
    ys = ys * _sigmoid(glu)
    yb = jnp.dot(ys.astype(BF16), wso[...], preferred_element_type=F32)

    merged = sga_ref[...].astype(F32) * ya + sgb_ref[...].astype(F32) * yb
    merged_out[...] = merged.astype(BF16)


def _mixers(xb, z, y5, xs, sga, sgb, zmeta, inj1, inj2, conv_w, d_skip, w_conv_out, w_glu, w_ssm_out,
            n_p_tiles, tiles_per_seq):
    t = xb.shape[0]
    n_s_tiles = inj1.shape[0] // TM
    row = lambda w: pl.BlockSpec((TM, w), lambda i: (i, 0))
    const = lambda r, w: pl.BlockSpec((r, w), lambda i: (0, 0))
    inj = pl.BlockSpec((TM, D_CONV), lambda i: (jnp.clip(i - n_p_tiles, 0, n_s_tiles - 1), 0))
    any_spec = pl.BlockSpec(memory_space=pl.ANY)
    return pl.pallas_call(
        functools.partial(_mixers_body, n_p_tiles, tiles_per_seq),
        grid=(t // TM,),
        in_specs=[row(D_CONV), row(D_CONV), row(D_SSM), row(D_SSM), row(D_MODEL), row(D_MODEL),
                  const(N_META, D_CONV), inj, inj, const(3, D_CONV), const(1, D_SSM),
                  any_spec, any_spec, any_spec],
        out_specs=row(D_MODEL),
        out_shape=jax.ShapeDtypeStruct((t, D_MODEL), BF16),
        scratch_shapes=[pltpu.VMEM((D_CONV, D_MODEL), BF16),
                        pltpu.VMEM((D_SSM, D_SSM), BF16),
                        pltpu.VMEM((D_SSM, D_MODEL), BF16),
                        pltpu.VMEM((2, 256, D_MODEL), F32),
                        pltpu.VMEM((2, 256, D_SSM), F32),
                        pltpu.SemaphoreType.DMA((2,)),
                        pltpu.VMEM((8, D_CONV), F32)],
        compiler_params=pltpu.CompilerParams(dimension_semantics=("arbitrary",), vmem_limit_bytes=VMEM_LIMIT),
        name="mixers",
    )(xb, z, y5, xs, sga, sgb, zmeta, inj1, inj2, conv_w, d_skip, w_conv_out, w_glu, w_ssm_out)


ROUTE_W = LANES
COARSE0 = N_EXPERTS


def _route(logits):
    col = lax.broadcasted_iota(I32, logits.shape, 1)
    colf = col.astype(F32)
    neg = jnp.float32(-jnp.inf)
    big = jnp.float32(1 << 20)
    is_c = jnp.logical_and(col >= COARSE0, col < COARSE0 + N_EGROUPS)
    lc = jnp.where(is_c, logits, neg)
    cmax = jnp.max(lc, axis=-1, keepdims=True)
    gi = jnp.min(jnp.where(lc == cmax, colf - COARSE0, big), axis=-1, keepdims=True)
    pg = 1.0 / jnp.sum(jnp.where(is_c, jnp.exp(lc - cmax), 0.0), axis=-1, keepdims=True)
    grp = (col >> 3).astype(F32)
    in_g = jnp.logical_and(col < N_EXPERTS, grp == gi)
    lf = jnp.where(in_g, logits, neg)
    m1 = jnp.max(lf, axis=-1, keepdims=True)
    i1 = jnp.min(jnp.where(lf == m1, colf, big), axis=-1, keepdims=True)
    lf2 = jnp.where(colf == i1, neg, lf)
    m2 = jnp.max(lf2, axis=-1, keepdims=True)
    i2 = jnp.min(jnp.where(lf2 == m2, colf, big), axis=-1, keepdims=True)
    e2 = jnp.exp(m2 - m1)
    w1 = pg / (1.0 + e2)
    w2 = pg * e2 / (1.0 + e2)
    rec = jnp.where(col == 0, i1, jnp.where(col == 1, i2, jnp.where(col == 2, w1, jnp.where(col == 3, w2, 0.0))))
    return rec


def _split_bf16(a):
    hi = a.astype(BF16)
    lo = (a - hi.astype(F32)).astype(BF16)
    return hi, lo


def _out_proj_body(n_p_tiles, merged_ref, xp_ref, xsm_ref, g2_ref, wr_ref, wo_hbm,
                   h2_out, v_out, rec_out, wo, stage, sem):
    i = pl.program_id(0)

    @pl.when(i == 0)
    def _():
        _load_weight_bf16(wo_hbm, wo, stage, sem)

    x = jnp.where(i < n_p_tiles, xp_ref[...], xsm_ref[...])
    h2 = x + jnp.dot(merged_ref[...], wo[...], preferred_element_type=F32)
    h2_out[...] = h2
    v = _rmsnorm(h2, g2_ref[...])
    for s in range(ROW_TILES):
        v_out[:, s, :] = v[:, s * LANES:(s + 1) * LANES]
    v_hi, v_lo = _split_bf16(v)
    w_hi, w_lo = _split_bf16(wr_ref[...])
    logits = (jnp.dot(v_hi, w_hi, preferred_element_type=F32)
              + jnp.dot(v_lo, w_hi, preferred_element_type=F32)
              + jnp.dot(v_hi, w_lo, preferred_element_type=F32))
    rec_out[...] = _route(logits)


def _out_proj(merged, xp, xsm, norm2, w_router, w_o):
    t_p, t_s = xp.shape[0], xsm.shape[0]
    n_p, n_s = t_p // TM, t_s // TM
    t = t_p + t_s
    xp_spec, xs_spec = _two_stream_specs(n_p, n_s)
    row = lambda w: pl.BlockSpec((TM, w), lambda i: (i, 0))
    const = lambda r, w: pl.BlockSpec((r, w), lambda i: (0, 0))
    return pl.pallas_call(
        functools.partial(_out_proj_body, n_p),
        grid=(n_p + n_s,),
        in_specs=[row(D_MODEL), xp_spec, xs_spec, const(1, D_MODEL), const(D_MODEL, ROUTE_W),
                  pl.BlockSpec(memory_space=pl.ANY)],
        out_specs=[row(D_MODEL), pl.BlockSpec((TM, ROW_TILES, LANES), lambda i: (i, 0, 0)), row(ROUTE_W)],
        out_shape=[jax.ShapeDtypeStruct((t, D_MODEL), F32),
                   jax.ShapeDtypeStruct((t, ROW_TILES, LANES), F32),
                   jax.ShapeDtypeStruct((t, ROUTE_W), F32)],
        scratch_shapes=[pltpu.VMEM((D_MODEL, D_MODEL), BF16),
                        pltpu.VMEM((2, 256, D_MODEL), F32),
                        pltpu.SemaphoreType.DMA((2,))],
        compiler_params=pltpu.CompilerParams(dimension_semantics=("arbitrary",), vmem_limit_bytes=VMEM_LIMIT),
        name="out_proj_route",
    )(merged, xp, xsm, norm2, w_router, w_o)


def _moe_plan(rec, n_tiles):
    t = rec.shape[0]
    n_pairs = 2 * t
    eid = rec[:, 0:2].astype(I32).reshape(-1)
    iota = jnp.arange(n_pairs, dtype=I32)
    _, order = lax.sort((eid, iota), num_keys=1, is_stable=True)
    _, inv = lax.sort((order, iota), num_keys=1)
    experts = jnp.arange(N_EXPERTS, dtype=I32)
    onehot = (eid[:, None] == experts[None, :]).astype(I32)
    counts = jnp.sum(onehot, axis=0)
    pair_start = jnp.cumsum(counts) - counts
    tiles_e = (counts + TM_MOE - 1) // TM_MOE
    tile_end = jnp.cumsum(tiles_e)
    tile_start = tile_end - tiles_e
    n_valid = tile_end[-1]
    tile_ids = jnp.arange(n_tiles, dtype=I32)
    tile_e = jnp.sum((tile_ids[:, None] >= tile_end[None, :]).astype(I32), axis=1)
    last_e = jnp.sum((n_valid - 1 >= tile_end).astype(I32))
    tile_e = jnp.minimum(jnp.where(tile_ids < n_valid, tile_e, last_e), N_EXPERTS - 1)
    tile_onehot = (tile_e[:, None] == experts[None, :]).astype(I32)
    tile_q0 = (tile_ids - jnp.sum(tile_onehot * tile_start[None, :], axis=1)) * TM_MOE
    shift = tile_start * TM_MOE - pair_start
    dest = inv + jnp.sum(onehot * shift[None, :], axis=1)
    return tile_e, tile_q0.astype(I32), n_valid.astype(I32).reshape(1), pair_start, counts, order, dest.astype(I32)


def _row_copy(src_hbm, dst, sem, src_row, slot, dst_row):
    return pltpu.make_async_copy(src_hbm.at[src_row], dst.at[slot, dst_row], sem.at[slot])


def _moe_body(tile_e_ref, tile_q0_ref, nvalid_ref, pstart_ref, cnt_ref, order_ref,
              v_hbm, wg_ref, wu_ref, wd_ref, y_out, xbuf, sem, wg, wu, wd):
    i = pl.program_id(0)
    nv = nvalid_ref[0]

    def issue(tile, slot):
        e = tile_e_ref[tile]
        base = pstart_ref[e] + tile_q0_ref[tile]
        last = pstart_ref[e] + cnt_ref[e] - 1
        for r in range(TM_MOE):
            tok = order_ref[jnp.minimum(base + r, last)] >> 1
            _row_copy(v_hbm, xbuf, sem, tok, slot, r).start()

    def compute(slot):
        x = jnp.concatenate([xbuf[slot, :, s, :] for s in range(ROW_TILES)], axis=-1).astype(BF16)
        hg = jnp.dot(x, wg[...], preferred_element_type=F32)
        hu = jnp.dot(x, wu[...], preferred_element_type=F32)
        act = hg * _sigmoid(hg) * hu
        y = jnp.dot(act.astype(BF16), wd[...], preferred_element_type=F32)
        for s in range(ROW_TILES):
            y_out[:, s, :] = y[:, s * LANES:(s + 1) * LANES]

    @pl.when(i == 0)
    def _():
        issue(0, 0)

    @pl.when(i < nv)
    def _():
        slot = i % 2

        def wbody(r, c):
            _row_copy(v_hbm, xbuf, sem, 0, slot, r).wait()
            return c
        lax.fori_loop(0, TM_MOE, wbody, 0, unroll=8)

        prev_e = tile_e_ref[jnp.maximum(i - 1, 0)]

        @pl.when(jnp.logical_or(i == 0, tile_e_ref[i] != prev_e))
        def _():
            wg[...] = wg_ref[0].astype(BF16)
            wu[...] = wu_ref[0].astype(BF16)
            wd[...] = wd_ref[0].astype(BF16)

    @pl.when(i + 1 < nv)
    def _():
        issue(i + 1, (i + 1) % 2)
        compute(i % 2)

    @pl.when(i + 1 == nv)
    def _():
        compute(i % 2)

    @pl.when(i >= nv)
    def _():
        y_out[...] = jnp.zeros(y_out.shape, F32)


def _moe(v3, plan, w_gate, w_up, w_down, n_tiles):
    tile_e, tile_q0, n_valid, pair_start, counts, order, _ = plan
    wspec = lambda a, b: pl.BlockSpec((1, a, b), lambda i, te, *_: (te[i], 0, 0))
    return pl.pallas_call(
        _moe_body,
        grid_spec=pltpu.PrefetchScalarGridSpec(
            num_scalar_prefetch=6,
            grid=(n_tiles,),
            in_specs=[pl.BlockSpec(memory_space=pl.ANY),
                      wspec(D_MODEL, D_EXPERT), wspec(D_MODEL, D_EXPERT), wspec(D_EXPERT, D_MODEL)],
            out_specs=pl.BlockSpec((TM_MOE, ROW_TILES, LANES), lambda i, *_: (i, 0, 0)),
            scratch_shapes=[pltpu.VMEM((2, TM_MOE, ROW_TILES, LANES), F32),
                            pltpu.SemaphoreType.DMA((2,)),
                            pltpu.VMEM((D_MODEL, D_EXPERT), BF16),
                            pltpu.VMEM((D_MODEL, D_EXPERT), BF16),
                            pltpu.VMEM((D_EXPERT, D_MODEL), BF16)]),
        out_shape=jax.ShapeDtypeStruct((n_tiles * TM_MOE, ROW_TILES, LANES), F32),
        compiler_params=pltpu.CompilerParams(dimension_semantics=("arbitrary",), vmem_limit_bytes=VMEM_LIMIT),
        name="moe_experts",
    )(tile_e, tile_q0, n_valid, pair_start, counts, order, v3, w_gate, w_up, w_down)


def _combine_body(n_p_tiles, dest_ref, h2_ref, rec_ref, gf_ref, y_hbm, outp_ref, outs_ref, ybuf, sem):
    i = pl.program_id(0)
    n = pl.num_programs(0)

    def issue(tile, slot):
        for r in range(TM):
            for k in range(2):
                _row_copy(y_hbm, ybuf, sem, dest_ref[2 * (tile * TM + r) + k], slot, k * TM + r).start(
                    priority=r % 2)

    @pl.when(i == 0)
    def _():
        issue(0, 0)

    @pl.when(i + 1 < n)
    def _():
        issue(i + 1, (i + 1) % 2)

    slot = i % 2

    def wbody(r, c):
        for k in range(2):
            _row_copy(y_hbm, ybuf, sem, 0, slot, k * TM + r).wait()
        return c
    lax.fori_loop(0, TM, wbody, 0, unroll=8)

    rec = rec_ref[...]
    w1 = rec[:, 2:3]
    w2 = rec[:, 3:4]
    h = jnp.concatenate(
        [w1 * ybuf[slot, 0:TM, s, :] + w2 * ybuf[slot, TM:2 * TM, s, :] for s in range(ROW_TILES)], axis=-1)
    out = _rmsnorm(h2_ref[...] + h, gf_ref[...])

    @pl.when(i < n_p_tiles)
    def _():
        outp_ref[...] = out

    @pl.when(i >= n_p_tiles)
    def _():
        outs_ref[...] = out


def _combine(h2, rec, y_sorted, dest, final_norm, t_p, t_s):
    n_p, n_s = t_p // TM, t_s // TM
    return pl.pallas_call(
        functools.partial(_combine_body, n_p),
        grid_spec=pltpu.PrefetchScalarGridSpec(
            num_scalar_prefetch=1,
            grid=(n_p + n_s,),
            in_specs=[pl.BlockSpec((TM, D_MODEL), lambda i, d: (i, 0)),
                      pl.BlockSpec((TM, ROUTE_W), lambda i, d: (i, 0)),
                      pl.BlockSpec((1, D_MODEL), lambda i, d: (0, 0)),
                      pl.BlockSpec(memory_space=pl.ANY)],
            out_specs=[pl.BlockSpec((TM, D_MODEL), lambda i, d: (jnp.minimum(i, n_p - 1), 0)),
                       pl.BlockSpec((TM, D_MODEL), lambda i, d: (jnp.clip(i - n_p, 0, n_s - 1), 0))],
            scratch_shapes=[pltpu.VMEM((2, 2 * TM, ROW_TILES, LANES), F32),
                            pltpu.SemaphoreType.DMA((2,))]),
        out_shape=[jax.ShapeDtypeStruct((t_p, D_MODEL), F32),
                   jax.ShapeDtypeStruct((t_s, D_MODEL), F32)],
        compiler_params=pltpu.CompilerParams(dimension_semantics=("arbitrary",), vmem_limit_bytes=VMEM_LIMIT),
        name="combine_norm",
    )(dest, h2, rec, final_norm, y_sorted)


def kernel(x_prompt, x_sample, state_conv, state_ssm_re, state_ssm_im, meta_tokens, norm1, w_in, conv_w,
           lam_re, lam_im, log_dt, ssm_b_re, ssm_b_im, ssm_c_re, ssm_c_im, ssm_d, w_glu, w_conv_out,
           w_ssm_out, w_o, norm2, w_coarse, w_fine, w_gate, w_up, w_down, final_norm):
    n_pb, seq, _ = x_prompt.shape
    n_sb, dec_seq, _ = x_sample.shape
    assert dec_seq == CHUNK and seq % TM == 0 and (n_sb * dec_seq) % TM == 0 and N_META == CHUNK
    t_p, t_s = n_pb * seq, n_sb * dec_seq
    xp = x_prompt.reshape(t_p, D_MODEL)
    xsm = x_sample.reshape(t_s, D_MODEL)

    xb, z, xs, zmeta, xsmeta, sga, sgb = _in_proj(xp, xsm, meta_tokens, norm1, w_in[0])

    mats = _s5_chunk_mats(lam_re[0], lam_im[0], log_dt[0], ssm_b_re[0], ssm_b_im[0], ssm_c_re[0], ssm_c_im[0])
    y5, pf_re, pf_im, sf_re, sf_im = _s5(xs, xsmeta, state_ssm_re[0], state_ssm_im[0], mats, n_pb, n_sb, seq)

    buf = state_conv[0]
    zero = jnp.zeros((n_sb, dec_seq, D_CONV), F32)
    inj1 = zero.at[:, 0].set(buf[:, 1]).reshape(t_s, D_CONV)
    inj2 = zero.at[:, 0].set(buf[:, 0]).at[:, 1].set(buf[:, 1]).reshape(t_s, D_CONV)
    merged = _mixers(xb, z, y5, xs, sga, sgb, zmeta, inj1, inj2, conv_w[0], ssm_d, w_conv_out[0], w_glu[0],
                     w_ssm_out[0], t_p // TM, seq // TM)

    w_router = jnp.concatenate(
        [w_fine[0], w_coarse[0], jnp.zeros((D_MODEL, ROUTE_W - N_EXPERTS - N_EGROUPS), F32)], axis=1)
    h2, v3, rec = _out_proj(merged, xp, xsm, norm2, w_router, w_o[0])

    n_tiles = (2 * (t_p + t_s)) // TM_MOE + N_EXPERTS
    plan = _moe_plan(rec, n_tiles)
    y_sorted = _moe(v3, plan, w_gate[0], w_up[0], w_down[0], n_tiles)
    y_p, y_s = _combine(h2, rec, y_sorted, plan[6], final_norm.reshape(1, D_MODEL), t_p, t_s)

    z_p = z[:t_p].reshape(n_pb, seq, D_CONV)
    z_s = z[t_p:].reshape(n_sb, dec_seq, D_CONV)
    return (y_p.reshape(n_pb, seq, D_MODEL), y_s.reshape(n_sb, dec_seq, D_MODEL),
            z_p[:, seq - 2:][None], pf_re, pf_im,
            z_s[:, dec_seq - 2:][None], sf_re, sf_im)
```

```python
import functools

import jax
import jax.numpy as jnp
from jax import lax
from jax.experimental import pallas as pl
from jax.experimental.pallas import tpu as pltpu

F32 = jnp.float32
BF16 = jnp.bfloat16
I32 = jnp.int32

D_MODEL = 2048
D_CONV = 1024
D_SSM = 1024
SSM_H = 16
SSM_G = 64
SSM_P = 64
N_META = 16
N_EGROUPS = 4
EXPERTS_PER_GROUP = 8
N_EXPERTS = 32
D_EXPERT = 256
EPS = 1e-6

LANES = 128
ROW_TILES = D_MODEL // LANES

CHUNK = 16
CHUNK_W = CHUNK * SSM_H

TM = 256
TM_MOE = 256
VMEM_LIMIT = 52 * 1024 * 1024


def _rmsnorm(x, g):
    return x * lax.rsqrt(jnp.mean(x * x, axis=-1, keepdims=True) + EPS) * g


def _sigmoid(x):
    return 1.0 / (1.0 + jnp.exp(-x))


def _gelu_tanh(x):
    c = 0.7978845608028654
    return 0.5 * x * (1.0 + jnp.tanh(c * (x + 0.044715 * (x * x * x))))


def _weight_copy(w_hbm, stage, sem, c, slot, rows, col0, ncols):
    return pltpu.make_async_copy(
        w_hbm.at[pl.ds(c * rows, rows), pl.ds(col0, ncols)], stage.at[slot], sem.at[slot])


def _load_weight_bf16(w_hbm, w_vmem, stage, sem, col0=0):
    k, n = w_vmem.shape
    rows = stage.shape[1]
    nchunk = k // rows
    _weight_copy(w_hbm, stage, sem, 0, 0, rows, col0, n).start()
    for c in range(nchunk):
        slot = c % 2
        if c + 1 < nchunk:
            _weight_copy(w_hbm, stage, sem, c + 1, 1 - slot, rows, col0, n).start()
        _weight_copy(w_hbm, stage, sem, c, slot, rows, col0, n).wait()
        w_vmem[pl.ds(c * rows, rows), :] = stage[slot].astype(BF16)


def _in_proj_mix_body(n_p_tiles, xp_ref, xsm_ref, meta_ref, g_ref, w_hbm,
                      xb_out, z_out, xs_out, zmeta_out, xsmeta_out,
                      w_vmem, stage, sem):
    i = pl.program_id(0)
    g = g_ref[...]

    def project(u):
        xb = jnp.dot(u, w_vmem[:, 0:D_CONV], preferred_element_type=F32)
        xc = jnp.dot(u, w_vmem[:, D_CONV:2 * D_CONV], preferred_element_type=F32)
        xv = jnp.dot(u, w_vmem[:, 2 * D_CONV:3 * D_CONV], preferred_element_type=F32)
        xs = jnp.dot(u, w_vmem[:, 3 * D_CONV:3 * D_CONV + D_SSM], preferred_element_type=F32)
        return xb, xc * xv, xs

    @pl.when(i == 0)
    def _():
        _load_weight_bf16(w_hbm, w_vmem, stage, sem, col0=0)
        um = _rmsnorm(meta_ref[...], g).astype(BF16)
        _, zm, xsm = project(um)
        zmeta_out[...] = zm
        xsmeta_out[...] = xsm

    x = jnp.where(i < n_p_tiles, xp_ref[...], xsm_ref[...])
    u = _rmsnorm(x, g).astype(BF16)
    xb, z, xs = project(u)
    xb_out[...] = xb.astype(BF16)
    z_out[...] = z
    xs_out[...] = xs


def _in_proj_gate_body(n_p_tiles, xp_ref, xsm_ref, g_ref, w_hbm, ga_out, gb_out,
                       w_vmem, stage, sem):
    i = pl.program_id(0)

    @pl.when(i == 0)
    def _():
        _load_weight_bf16(w_hbm, w_vmem, stage, sem, col0=3 * D_CONV + D_SSM)

    x = jnp.where(i < n_p_tiles, xp_ref[...], xsm_ref[...])
    u = _rmsnorm(x, g_ref[...]).astype(BF16)
    ga = jnp.dot(u, w_vmem[:, 0:D_MODEL], preferred_element_type=F32)
    ga_out[...] = _sigmoid(ga).astype(BF16)
    gb = jnp.dot(u, w_vmem[:, D_MODEL:2 * D_MODEL], preferred_element_type=F32)
    gb_out[...] = _sigmoid(gb).astype(BF16)


def _two_stream_specs(n_p_tiles, n_s_tiles):
    xp_spec = pl.BlockSpec((TM, D_MODEL), lambda i: (jnp.minimum(i, n_p_tiles - 1), 0))
    xs_spec = pl.BlockSpec((TM, D_MODEL), lambda i: (jnp.clip(i - n_p_tiles, 0, n_s_tiles - 1), 0))
    return xp_spec, xs_spec


def _in_proj(xp, xsm, meta, norm1, w_in):
    t_p, t_s = xp.shape[0], xsm.shape[0]
    n_p, n_s = t_p // TM, t_s // TM
    t = t_p + t_s
    half = 3 * D_CONV + D_SSM
    xp_spec, xs_spec = _two_stream_specs(n_p, n_s)
    g_spec = pl.BlockSpec((1, D_MODEL), lambda i: (0, 0))
    any_spec = pl.BlockSpec(memory_space=pl.ANY)
    stage_rows = 128
    row = lambda w: pl.BlockSpec((TM, w), lambda i: (i, 0))
    const = lambda r, w: pl.BlockSpec((r, w), lambda i: (0, 0))
    params = pltpu.CompilerParams(dimension_semantics=("arbitrary",), vmem_limit_bytes=VMEM_LIMIT)
    scratch = [pltpu.VMEM((D_MODEL, half), BF16),
               pltpu.VMEM((2, stage_rows, half), F32),
               pltpu.SemaphoreType.DMA((2,))]

    xb, z, xs, zmeta, xsmeta = pl.pallas_call(
        functools.partial(_in_proj_mix_body, n_p),
        grid=(n_p + n_s,),
        in_specs=[xp_spec, xs_spec, const(N_META, D_MODEL), g_spec, any_spec],
        out_specs=[row(D_CONV), row(D_CONV), row(D_SSM), const(N_META, D_CONV), const(N_META, D_SSM)],
        out_shape=[jax.ShapeDtypeStruct((t, D_CONV), BF16),
                   jax.ShapeDtypeStruct((t, D_CONV), F32),
                   jax.ShapeDtypeStruct((t, D_SSM), F32),
                   jax.ShapeDtypeStruct((N_META, D_CONV), F32),
                   jax.ShapeDtypeStruct((N_META, D_SSM), F32)],
        scratch_shapes=scratch,
        compiler_params=params,
        name="in_proj_mix",
    )(xp, xsm, meta, norm1, w_in)

    sga, sgb = pl.pallas_call(
        functools.partial(_in_proj_gate_body, n_p),
        grid=(n_p + n_s,),
        in_specs=[xp_spec, xs_spec, g_spec, any_spec],
        out_specs=[row(D_MODEL), row(D_MODEL)],
        out_shape=[jax.ShapeDtypeStruct((t, D_MODEL), BF16),
                   jax.ShapeDtypeStruct((t, D_MODEL), BF16)],
        scratch_shapes=scratch,
        compiler_params=params,
        name="in_proj_gate",
    )(xp, xsm, norm1, w_in)
    return xb, z, xs, zmeta, xsmeta, sga, sgb


S5_GROUPS_PER_STEP = LANES // SSM_H
S5_PAIRS_PER_STEP = S5_GROUPS_PER_STEP // 2


def _s5_chunk_mats(lam_re, lam_im, log_dt, b_re, b_im, c_re, c_im):
    dt = jnp.exp(log_dt)[:, None]
    lr, li = lam_re, lam_im
    mag = jnp.exp(lr * dt)
    ab_re, ab_im = mag * jnp.cos(li * dt), mag * jnp.sin(li * dt)
    nr, ni = ab_re - 1.0, ab_im
    den = lr * lr + li * li
    k_re = (nr * lr + ni * li) / den
    k_im = (ni * lr - nr * li) / den
    bb_re = k_re[..., None] * b_re - k_im[..., None] * b_im
    bb_im = k_re[..., None] * b_im + k_im[..., None] * b_re
    d = jnp.arange(CHUNK + 1, dtype=F32)[:, None, None]
    pmag = jnp.exp(d * (lr * dt)[None])
    pw_re = pmag * jnp.cos(d * (li * dt)[None])
    pw_im = pmag * jnp.sin(d * (li * dt)[None])
    ab_b_re = pw_re[..., None] * bb_re[None] - pw_im[..., None] * bb_im[None]
    ab_b_im = pw_re[..., None] * bb_im[None] + pw_im[..., None] * bb_re[None]
    m = jnp.sum(c_re[None, :, :, :, None] * ab_b_re[:CHUNK, :, None, :, :]
                - c_im[None, :, :, :, None] * ab_b_im[:CHUNK, :, None, :, :], axis=3)
    rows = []
    zero = jnp.zeros((SSM_G, SSM_H, SSM_H), F32)
    for t in range(CHUNK):
        rows.append(jnp.concatenate([m[t - j] if t >= j else zero for j in range(CHUNK)], axis=2))
    toe_t = jnp.concatenate(rows, axis=1)
    p_re = jnp.transpose(ab_b_re[:CHUNK][::-1], (1, 2, 0, 3)).reshape(SSM_G, SSM_P, CHUNK_W)
    p_im = jnp.transpose(ab_b_im[:CHUNK][::-1], (1, 2, 0, 3)).reshape(SSM_G, SSM_P, CHUNK_W)
    p_t = jnp.concatenate([p_re, p_im], axis=1)
    pr, pi = pw_re[1:], pw_im[1:]
    q_re = c_re[None] * pr[:, :, None, :] - c_im[None] * pi[:, :, None, :]
    q_im = -(c_re[None] * pi[:, :, None, :] + c_im[None] * pr[:, :, None, :])
    q_re = jnp.transpose(q_re, (1, 0, 2, 3)).reshape(SSM_G, CHUNK_W, SSM_P)
    q_im = jnp.transpose(q_im, (1, 0, 2, 3)).reshape(SSM_G, CHUNK_W, SSM_P)
    even = (jnp.arange(SSM_G) % 2 == 0)[:, None, None]
    zq = jnp.zeros_like(q_re)
    pad2 = lambda q: jnp.where(even, jnp.concatenate([q, zq], axis=2), jnp.concatenate([zq, q], axis=2))
    q_t = jnp.stack([pad2(q_re), pad2(q_im)], axis=1)
    a16_re = pw_re[CHUNK].reshape(SSM_G // 2, 1, 2 * SSM_P)
    a16_im = pw_im[CHUNK].reshape(SSM_G // 2, 1, 2 * SSM_P)
    return toe_t.astype(BF16), p_t.astype(BF16), q_t.astype(BF16), a16_re, a16_im


def _s5_body(n_pc, n_pb, n_sb, xs_ref, xsmeta_ref, toe_ref, p_ref, q_ref, are_ref, aim_ref, s0re_ref, s0im_ref,
             y_out, pfre_out, pfim_out, sfre_out, sfim_out,
             xt_scr, u_scr, st_re, st_im, sl_re, sl_im, sp_re, sp_im, yt_scr):
    gb = S5_GROUPS_PER_STEP
    n_p_rows = n_pb * n_pc
    row_s = n_p_rows
    row_m = row_s + n_sb
    t_p = n_p_rows * CHUNK
    rows_pad = xt_scr.shape[0]

    xt_scr[row_m + 1:rows_pad, :] = jnp.zeros((rows_pad - row_m - 1, LANES), F32)
    for t in range(CHUNK):
        xt_scr[0:n_p_rows, :] = xs_ref[pl.ds(t, n_p_rows, stride=CHUNK), :]
        xt_scr[row_s:row_m, :] = xs_ref[pl.ds(t_p + t, n_sb, stride=CHUNK), :]
        xt_scr[row_m:row_m + 1, :] = xsmeta_ref[t:t + 1, :]
        xt = xt_scr[...].T.astype(BF16)
        for k in range(gb):
            u_scr[k, t * SSM_H:(t + 1) * SSM_H, :] = xt[k * SSM_H:(k + 1) * SSM_H, :]

    for k in range(gb):
        sl = jnp.dot(p_ref[k], u_scr[k], preferred_element_type=F32)
        half = (k % 2) * SSM_P
        st_re[k // 2, half:half + SSM_P, :] = sl[0:SSM_P, :]
        st_im[k // 2, half:half + SSM_P, :] = sl[SSM_P:2 * SSM_P, :]
    npair = S5_PAIRS_PER_STEP
    for j in range(npair):
        sl_re[j] = st_re[j].T
        sl_im[j] = st_im[j].T
        sp_re[j, row_m:rows_pad, :] = jnp.zeros((rows_pad - row_m, 2 * SSM_P), F32)
        sp_im[j, row_m:rows_pad, :] = jnp.zeros((rows_pad - row_m, 2 * SSM_P), F32)

    ar = [are_ref[j] for j in range(npair)]
    ai = [aim_ref[j] for j in range(npair)]
    sre = [jnp.broadcast_to(sl_re[j, row_m:row_m + 1, :], (n_pb, 2 * SSM_P)) for j in range(npair)]
    sim = [jnp.broadcast_to(sl_im[j, row_m:row_m + 1, :], (n_pb, 2 * SSM_P)) for j in range(npair)]
    for c in range(n_pc):
        rows = pl.ds(c, n_pb, stride=n_pc)
        for j in range(npair):
            sp_re[j, rows, :] = sre[j]
            sp_im[j, rows, :] = sim[j]
            nre = ar[j] * sre[j] - ai[j] * sim[j] + sl_re[j, rows, :]
            nim = ar[j] * sim[j] + ai[j] * sre[j] + sl_im[j, rows, :]
            sre[j], sim[j] = nre, nim
    for j in range(npair):
        pfre_out[j] = sre[j]
        pfim_out[j] = sim[j]
        s0r, s0i = s0re_ref[j], s0im_ref[j]
        sp_re[j, row_s:row_m, :] = s0r
        sp_im[j, row_s:row_m, :] = s0i
        sfre_out[j] = ar[j] * s0r - ai[j] * s0i + sl_re[j, row_s:row_m, :]
        sfim_out[j] = ar[j] * s0i + ai[j] * s0r + sl_im[j, row_s:row_m, :]

    nt = (((1,), (1,)), ((), ()))
    for k in range(gb):
        y = jnp.dot(toe_ref[k], u_scr[k], preferred_element_type=F32)
        y += lax.dot_general(q_ref[k, 0], sp_re[k // 2].astype(BF16), nt, preferred_element_type=F32)
        y += lax.dot_general(q_ref[k, 1], sp_im[k // 2].astype(BF16), nt, preferred_element_type=F32)
        for t in range(CHUNK):
            yt_scr[t, k * SSM_H:(k + 1) * SSM_H, :] = y[t * SSM_H:(t + 1) * SSM_H, :]
    for t in range(CHUNK):
        yt = yt_scr[t].T
        y_out[pl.ds(t, n_p_rows, stride=CHUNK), :] = yt[0:n_p_rows, :]
        y_out[pl.ds(t_p + t, n_sb, stride=CHUNK), :] = yt[row_s:row_m, :]


def _s5(xs, xsmeta, state_re, state_im, mats, n_pb, n_sb, seq):
    toe_t, p_t, q_t, a_re, a_im = mats
    t = xs.shape[0]
    n_pc = seq // CHUNK
    rows = n_pc * n_pb + n_sb + 1
    rows_pad = -(-rows // LANES) * LANES
    gb, npair = S5_GROUPS_PER_STEP, S5_PAIRS_PER_STEP
    pairs = lambda s: jnp.transpose(s.reshape(n_sb, SSM_G // 2, 2 * SSM_P), (1, 0, 2))
    blk3 = lambda n, r, c: pl.BlockSpec((n, r, c), lambda i: (i, 0, 0))
    y, pfre, pfim, sfre, sfim = pl.pallas_call(
        functools.partial(_s5_body, n_pc, n_pb, n_sb),
        grid=(SSM_G // gb,),
        in_specs=[pl.BlockSpec((t, LANES), lambda i: (0, i)),
                  pl.BlockSpec((N_META, LANES), lambda i: (0, i)),
                  blk3(gb, CHUNK_W, CHUNK_W), blk3(gb, 2 * SSM_P, CHUNK_W),
                  pl.BlockSpec((gb, 2, CHUNK_W, 2 * SSM_P), lambda i: (i, 0, 0, 0)),
                  blk3(npair, 1, 2 * SSM_P), blk3(npair, 1, 2 * SSM_P),
                  blk3(npair, n_sb, 2 * SSM_P), blk3(npair, n_sb, 2 * SSM_P)],
        out_specs=[pl.BlockSpec((t, LANES), lambda i: (0, i)),
                   blk3(npair, n_pb, 2 * SSM_P), blk3(npair, n_pb, 2 * SSM_P),
                   blk3(npair, n_sb, 2 * SSM_P), blk3(npair, n_sb, 2 * SSM_P)],
        out_shape=[jax.ShapeDtypeStruct((t, D_SSM), F32),
                   jax.ShapeDtypeStruct((SSM_G // 2, n_pb, 2 * SSM_P), F32),
                   jax.ShapeDtypeStruct((SSM_G // 2, n_pb, 2 * SSM_P), F32),
                   jax.ShapeDtypeStruct((SSM_G // 2, n_sb, 2 * SSM_P), F32),
                   jax.ShapeDtypeStruct((SSM_G // 2, n_sb, 2 * SSM_P), F32)],
        scratch_shapes=[pltpu.VMEM((rows_pad, LANES), F32),
                        pltpu.VMEM((gb, CHUNK_W, rows_pad), BF16),
                        pltpu.VMEM((npair, 2 * SSM_P, rows_pad), F32),
                        pltpu.VMEM((npair, 2 * SSM_P, rows_pad), F32),
                        pltpu.VMEM((npair, rows_pad, 2 * SSM_P), F32),
                        pltpu.VMEM((npair, rows_pad, 2 * SSM_P), F32),
                        pltpu.VMEM((npair, rows_pad, 2 * SSM_P), F32),
                        pltpu.VMEM((npair, rows_pad, 2 * SSM_P), F32),
                        pltpu.VMEM((CHUNK, LANES, rows_pad), F32)],
        compiler_params=pltpu.CompilerParams(dimension_semantics=("arbitrary",), vmem_limit_bytes=VMEM_LIMIT),
        name="s5_chunks",
    )(xs, xsmeta, toe_t, p_t, q_t, a_re, a_im, pairs(state_re), pairs(state_im))
    unpair = lambda a: jnp.transpose(a, (1, 0, 2)).reshape(a.shape[1], SSM_G, SSM_P)[None]
    return y, unpair(pfre), unpair(pfim), unpair(sfre), unpair(sfim)


def _mixers_body(n_p_tiles, tiles_per_seq, xb_ref, z_ref, y5_ref, xs_ref, sga_ref, sgb_ref,
                 zmeta_ref, inj1_ref, inj2_ref, cw_ref, dskip_ref,
                 wc_hbm, wg_hbm, wso_hbm, merged_out,
                 wc, wg, wso, stage_a, stage_b, sem, carry):
    i = pl.program_id(0)

    @pl.when(i == 0)
    def _():
        _load_weight_bf16(wc_hbm, wc, stage_a, sem)
        _load_weight_bf16(wg_hbm, wg, stage_b, sem)
        _load_weight_bf16(wso_hbm, wso, stage_a, sem)

    @pl.when(jnp.logical_and(i < n_p_tiles, i % tiles_per_seq == 0))
    def _():
        carry[0:2, :] = zmeta_ref[N_META - 2:N_META, :]

    z = z_ref[...]
    row = lax.broadcasted_iota(I32, (TM, 1), 0)
    is_s = i >= n_p_tiles
    r1 = pltpu.roll(z, 1, 0)
    r2 = pltpu.roll(z, 2, 0)
    c1 = carry[1:2, :]
    c2 = carry[0:1, :]
    pos = jnp.where(is_s, row & (CHUNK - 1), row)
    first1 = pos == 0
    first2 = pos < 2
    fill1 = jnp.where(is_s, inj1_ref[...], jnp.broadcast_to(c1, z.shape))
    fill2 = jnp.where(is_s, inj2_ref[...], jnp.where(row == 0, c2, c1))
    zp1 = jnp.where(first1, fill1, r1)
    zp2 = jnp.where(first2, fill2, r2)
    carry[0:2, :] = z[TM - 2:TM, :]

    cw = cw_ref[...]
    conv = cw[0:1, :] * zp2 + cw[1:2, :] * zp1 + cw[2:3, :] * z
    a_in = (xb_ref[...].astype(F32) * conv).astype(BF16)
    ya = jnp.dot(a_in, wc[...], preferred_element_type=F32)

    ys = y5_ref[...] + dskip_ref[...] * xs_ref[...]
    ys = _gelu_tanh(ys)
    glu = jnp.dot(ys.astype(BF16), wg[...], preferred_element_type=F32)
    ys = ys * _sigmoid(glu)
    yb = jnp.dot(ys.astype(BF16), wso[...], preferred_element_type=F32)

    merged = sga_ref[...].astype(F32) * ya + sgb_ref[...].astype(F32) * yb
    merged_out[...] = merged.astype(BF16)


def _mixers(xb, z, y5, xs, sga, sgb, zmeta, inj1, inj2, conv_w, d_skip, w_conv_out, w_glu, w_ssm_out,
            n_p_tiles, tiles_per_seq):
    t = xb.shape[0]
    n_s_tiles = inj1.shape[0] // TM
    row = lambda w: pl.BlockSpec((TM, w), lambda i: (i, 0))
    const = lambda r, w: pl.BlockSpec((r, w), lambda i: (0, 0))
    inj = pl.BlockSpec((TM, D_CONV), lambda i: (jnp.clip(i - n_p_tiles, 0, n_s_tiles - 1), 0))
    any_spec = pl.BlockSpec(memory_space=pl.ANY)
    return pl.pallas_call(
        functools.partial(_mixers_body, n_p_tiles, tiles_per_seq),
        grid=(t // TM,),
        in_specs=[row(D_CONV), row(D_CONV), row(D_SSM), row(D_SSM), row(D_MODEL), row(D_MODEL),
                  const(N_META, D_CONV), inj, inj, const(3, D_CONV), const(1, D_SSM),
                  any_spec, any_spec, any_spec],
        out_specs=row(D_MODEL),
        out_shape=jax.ShapeDtypeStruct((t, D_MODEL), BF16),
        scratch_shapes=[pltpu.VMEM((D_CONV, D_MODEL), BF16),
                        pltpu.VMEM((D_SSM, D_SSM), BF16),
                        pltpu.VMEM((D_SSM, D_MODEL), BF16),
                        pltpu.VMEM((2, 256, D_MODEL), F32),
                        pltpu.VMEM((2, 256, D_SSM), F32),
                        pltpu.SemaphoreType.DMA((2,)),
                        pltpu.VMEM((8, D_CONV), F32)],
        compiler_params=pltpu.CompilerParams(dimension_semantics=("arbitrary",), vmem_limit_bytes=VMEM_LIMIT),
        name="mixers",
    )(xb, z, y5, xs, sga, sgb, zmeta, inj1, inj2, conv_w, d_skip, w_conv_out, w_glu, w_ssm_out)


ROUTE_W = LANES
COARSE0 = N_EXPERTS


def _route(logits, cnt):
    col = lax.broadcasted_iota(I32, logits.shape, 1)
    colf = col.astype(F32)
    neg = jnp.float32(-jnp.inf)
    big = jnp.float32(1 << 20)
    is_c = jnp.logical_and(col >= COARSE0, col < COARSE0 + N_EGROUPS)
    lc = jnp.where(is_c, logits, neg)
    cmax = jnp.max(lc, axis=-1, keepdims=True)
    gi = jnp.min(jnp.where(lc == cmax, colf - COARSE0, big), axis=-1, keepdims=True)
    pg = 1.0 / jnp.sum(jnp.where(is_c, jnp.exp(lc - cmax), 0.0), axis=-1, keepdims=True)
    grp = (col >> 3).astype(F32)
    in_g = jnp.logical_and(col < N_EXPERTS, grp == gi)
    lf = jnp.where(in_g, logits, neg)
    m1 = jnp.max(lf, axis=-1, keepdims=True)
    i1 = jnp.min(jnp.where(lf == m1, colf, big), axis=-1, keepdims=True)
    lf2 = jnp.where(colf == i1, neg, lf)
    m2 = jnp.max(lf2, axis=-1, keepdims=True)
    i2 = jnp.min(jnp.where(lf2 == m2, colf, big), axis=-1, keepdims=True)
    e2 = jnp.exp(m2 - m1)
    w1 = pg / (1.0 + e2)
    w2 = pg * e2 / (1.0 + e2)
    n = logits.shape[0]
    hit1 = colf == i1
    hit2 = colf == i2
    onehot = jnp.where(jnp.logical_or(hit1, hit2), 1.0, 0.0)
    rr = lax.broadcasted_iota(I32, (n, n), 0)
    cc = lax.broadcasted_iota(I32, (n, n), 1)
    tri = jnp.where(cc < rr, 1.0, 0.0).astype(BF16)
    pos = jnp.dot(tri, onehot.astype(BF16), preferred_element_type=F32) + cnt
    rank1 = jnp.sum(jnp.where(hit1, pos, 0.0), axis=-1, keepdims=True)
    rank2 = jnp.sum(jnp.where(hit2, pos, 0.0), axis=-1, keepdims=True)
    vals = (i1, i2, w1, w2, rank1, rank2)
    rec = jnp.zeros(logits.shape, F32)
    for c, val in enumerate(vals):
        rec = jnp.where(col == c, val, rec)
    return rec, cnt + jnp.sum(onehot, axis=0, keepdims=True)


def _split_bf16(a):
    hi = a.astype(BF16)
    lo = (a - hi.astype(F32)).astype(BF16)
    return hi, lo


def _out_proj_body(n_p_tiles, merged_ref, xp_ref, xsm_ref, g2_ref, wr_ref, wo_hbm,
                   h2_out, v_out, rec_out, cnt_out, wo, stage, sem):
    i = pl.program_id(0)

    @pl.when(i == 0)
    def _():
        _load_weight_bf16(wo_hbm, wo, stage, sem)
        cnt_out[...] = jnp.zeros(cnt_out.shape, F32)

    x = jnp.where(i < n_p_tiles, xp_ref[...], xsm_ref[...])
    h2 = x + jnp.dot(merged_ref[...], wo[...], preferred_element_type=F32)
    h2_out[...] = h2
    v = _rmsnorm(h2, g2_ref[...])
    for s in range(ROW_TILES):
        v_out[:, s, :] = v[:, s * LANES:(s + 1) * LANES]
    v_hi, v_lo = _split_bf16(v)
    w_hi, w_lo = _split_bf16(wr_ref[...])
    logits = (jnp.dot(v_hi, w_hi, preferred_element_type=F32)
              + jnp.dot(v_lo, w_hi, preferred_element_type=F32)
              + jnp.dot(v_hi, w_lo, preferred_element_type=F32))
    rec, cnt = _route(logits, cnt_out[...])
    rec_out[...] = rec
    cnt_out[...] = cnt


def _out_proj(merged, xp, xsm, norm2, w_router, w_o):
    t_p, t_s = xp.shape[0], xsm.shape[0]
    n_p, n_s = t_p // TM, t_s // TM
    t = t_p + t_s
    xp_spec, xs_spec = _two_stream_specs(n_p, n_s)
    row = lambda w: pl.BlockSpec((TM, w), lambda i: (i, 0))
    const = lambda r, w: pl.BlockSpec((r, w), lambda i: (0, 0))
    return pl.pallas_call(
        functools.partial(_out_proj_body, n_p),
        grid=(n_p + n_s,),
        in_specs=[row(D_MODEL), xp_spec, xs_spec, const(1, D_MODEL), const(D_MODEL, ROUTE_W),
                  pl.BlockSpec(memory_space=pl.ANY)],
        out_specs=[row(D_MODEL), pl.BlockSpec((TM, ROW_TILES, LANES), lambda i: (i, 0, 0)), row(ROUTE_W),
                   const(1, ROUTE_W)],
        out_shape=[jax.ShapeDtypeStruct((t, D_MODEL), F32),
                   jax.ShapeDtypeStruct((t, ROW_TILES, LANES), F32),
                   jax.ShapeDtypeStruct((t, ROUTE_W), F32),
                   jax.ShapeDtypeStruct((1, ROUTE_W), F32)],
        scratch_shapes=[pltpu.VMEM((D_MODEL, D_MODEL), BF16),
                        pltpu.VMEM((2, 256, D_MODEL), F32),
                        pltpu.SemaphoreType.DMA((2,))],
        compiler_params=pltpu.CompilerParams(dimension_semantics=("arbitrary",), vmem_limit_bytes=VMEM_LIMIT),
        name="out_proj_route",
    )(merged, xp, xsm, norm2, w_router, w_o)


def _moe_plan(rec, cnt, n_tiles):
    t = rec.shape[0]
    n_pairs = 2 * t
    eid = rec[:, 0:2].astype(I32).reshape(-1)
    rank = rec[:, 4:6].astype(I32).reshape(-1)
    counts = cnt[0, :N_EXPERTS].astype(I32)
    pair_start = jnp.cumsum(counts) - counts
    experts = jnp.arange(N_EXPERTS, dtype=I32)
    onehot = (eid[:, None] == experts[None, :]).astype(I32)
    pos = rank + jnp.sum(onehot * pair_start[None, :], axis=1)
    _, order = lax.sort((pos, jnp.arange(n_pairs, dtype=I32)), num_keys=1)
    tiles_e = (counts + TM_MOE - 1) // TM_MOE
    tile_end = jnp.cumsum(tiles_e)
    tile_start = tile_end - tiles_e
    n_valid = tile_end[-1]
    tile_ids = jnp.arange(n_tiles, dtype=I32)
    tile_e = jnp.sum((tile_ids[:, None] >= tile_end[None, :]).astype(I32), axis=1)
    last_e = jnp.sum((n_valid - 1 >= tile_end).astype(I32))
    tile_e = jnp.minimum(jnp.where(tile_ids < n_valid, tile_e, last_e), N_EXPERTS - 1)
    tile_onehot = (tile_e[:, None] == experts[None, :]).astype(I32)
    tile_q0 = (tile_ids - jnp.sum(tile_onehot * tile_start[None, :], axis=1)) * TM_MOE
    tile_q0 = jnp.where(tile_ids < n_valid, tile_q0, 0)
    dst = rank + jnp.sum(onehot * (tile_start * TM_MOE)[None, :], axis=1)
    pad_start = tile_start * TM_MOE + counts
    pad_len = tiles_e * TM_MOE - counts
    return (dst.astype(I32), pad_start.astype(I32), pad_len.astype(I32), tile_e, tile_q0.astype(I32),
            n_valid.astype(I32).reshape(1), pair_start, counts, order)


DISPATCH_BUFS = 3
ROW_PIECES = tuple(TM_MOE >> (b + 1) for b in range(TM_MOE.bit_length() - 1))


def _dispatch_body(dst_ref, pad_start_ref, pad_len_ref, nvalid_ref, v_hbm, x_hbm, buf, zbuf, rsem, ssem, zsem):
    i = pl.program_id(0)
    n = pl.num_programs(0)

    def read(tile, slot):
        return pltpu.make_async_copy(v_hbm.at[pl.ds(tile * TM, TM)], buf.at[slot], rsem.at[slot])

    def row_write(slot, r, dst_row):
        return pltpu.make_async_copy(buf.at[slot, r], x_hbm.at[dst_row], ssem.at[slot])

    def drain(slot):
        def body(r, c):
            row_write(slot, 0, 0).wait()
            row_write(slot, 0, 0).wait()
            return c
        lax.fori_loop(0, TM, body, 0, unroll=8)

    def pad_fill(go):
        def zero_rows(start, size):
            d = pltpu.make_async_copy(zbuf.at[pl.ds(0, size)], x_hbm.at[pl.ds(start, size)], zsem)
            d.start() if go else d.wait()

        def body(e, c):
            start, length = pad_start_ref[e], pad_len_ref[e]
            for size in ROW_PIECES:
                @pl.when((length & size) != 0)
                def _(size=size):
                    zero_rows(start + (length & (-2 * size)), size)
            return c
        lax.fori_loop(0, N_EXPERTS, body, 0)

        def unused(tile, c):
            for half in range(TM_MOE // ROW_PIECES[0]):
                zero_rows(tile * TM_MOE + half * ROW_PIECES[0], ROW_PIECES[0])
            return c
        lax.fori_loop(nvalid_ref[0], x_hbm.shape[0] // TM_MOE, unused, 0)

    @pl.when(i == 0)
    def _():
        zbuf[...] = jnp.zeros(zbuf.shape, F32)
        pad_fill(True)
        read(0, 0).start()

    @pl.when(i >= 2)
    def _():
        drain((i + 1) % DISPATCH_BUFS)

    @pl.when(i + 1 < n)
    def _():
        read(i + 1, (i + 1) % DISPATCH_BUFS).start()

    slot = i % DISPATCH_BUFS
    read(i, slot).wait()
    for r in range(TM):
        for k in range(2):
            row_write(slot, r, dst_ref[2 * (i * TM + r) + k]).start()

    @pl.when(i == n - 1)
    def _():
        if n >= 2:
            drain((i - 1) % DISPATCH_BUFS)
        drain(slot)
        pad_fill(False)


def _dispatch(v3, plan, n_tiles):
    dst, pad_start, pad_len, n_valid = plan[0], plan[1], plan[2], plan[5]
    t = v3.shape[0]
    return pl.pallas_call(
        _dispatch_body,
        grid_spec=pltpu.PrefetchScalarGridSpec(
            num_scalar_prefetch=4,
            grid=(t // TM,),
            in_specs=[pl.BlockSpec(memory_space=pl.ANY)],
            out_specs=pl.BlockSpec(memory_space=pl.ANY),
            scratch_shapes=[pltpu.VMEM((DISPATCH_BUFS, TM, ROW_TILES, LANES), F32),
                            pltpu.VMEM((ROW_PIECES[0], ROW_TILES, LANES), F32),
                            pltpu.SemaphoreType.DMA((DISPATCH_BUFS,)),
                            pltpu.SemaphoreType.DMA((DISPATCH_BUFS,)),
                            pltpu.SemaphoreType.DMA(())]),
        out_shape=jax.ShapeDtypeStruct((n_tiles * TM_MOE, ROW_TILES, LANES), F32),
        compiler_params=pltpu.CompilerParams(dimension_semantics=("arbitrary",), vmem_limit_bytes=VMEM_LIMIT),
        name="moe_dispatch",
    )(dst, pad_start, pad_len, n_valid, v3)


def _moe_body(trash0, tile_e_ref, tile_q0_ref, nvalid_ref, pstart_ref, cnt_ref, order_ref,
              x_ref, wg_ref, wu_ref, wd_ref, o_hbm, ybuf, ssem, wg, wu, wd):
    i = pl.program_id(0)
    nv = nvalid_ref[0]

    def row_write(slot, r, dst_row):
        return pltpu.make_async_copy(ybuf.at[slot, r], o_hbm.at[dst_row], ssem.at[slot])

    def scratch_rows(slot):
        return pltpu.make_async_copy(ybuf.at[slot], o_hbm.at[pl.ds(trash0 + slot * TM_MOE, TM_MOE)], ssem.at[slot])

    def drain_writes(slot):
        def body(r, c):
            row_write(slot, 0, 0).wait()
            return c
        lax.fori_loop(0, TM_MOE, body, 0, unroll=8)

    @pl.when(i == 0)
    def _():
        ybuf[...] = jnp.zeros(ybuf.shape, F32)
        for slot in range(2):
            scratch_rows(slot).start()
        for slot in range(2):
            scratch_rows(slot).wait()

    @pl.when(jnp.logical_and(i >= 2, i < nv + 2))
    def _():
        drain_writes(i % 2)

    @pl.when(i < nv)
    def _():
        slot = i % 2
        e = tile_e_ref[i]
        prev_e = tile_e_ref[jnp.maximum(i - 1, 0)]

        @pl.when(jnp.logical_or(i == 0, e != prev_e))
        def _():
            wg[...] = wg_ref[0].astype(BF16)
            wu[...] = wu_ref[0].astype(BF16)
            wd[...] = wd_ref[0].astype(BF16)

        x = jnp.concatenate([x_ref[:, s, :] for s in range(ROW_TILES)], axis=-1).astype(BF16)
        hg = jnp.dot(x, wg[...], preferred_element_type=F32)
        hu = jnp.dot(x, wu[...], preferred_element_type=F32)
        act = hg * _sigmoid(hg) * hu
        y = jnp.dot(act.astype(BF16), wd[...], preferred_element_type=F32)
        for s in range(ROW_TILES):
            ybuf[slot, :, s, :] = y[:, s * LANES:(s + 1) * LANES]
        valid = cnt_ref[e] - tile_q0_ref[i]
        first = pstart_ref[e] + tile_q0_ref[i]
        last = pstart_ref[e] + cnt_ref[e] - 1
        for r in range(TM_MOE):
            pair = order_ref[jnp.minimum(first + r, last)]
            row_write(slot, r, jnp.where(r < valid, pair, trash0 + slot * TM_MOE + r)).start()


def _moe(x_disp, plan, w_gate, w_up, w_down, n_tiles, n_pairs):
    tile_e, tile_q0, n_valid, pair_start, counts, order = plan[3:9]
    tile = lambda i, nv: jnp.minimum(i, jnp.maximum(nv[0] - 1, 0))
    wspec = lambda a, b: pl.BlockSpec((1, a, b), lambda i, te, tq, nv, *_: (te[tile(i, nv)], 0, 0))
    return pl.pallas_call(
        functools.partial(_moe_body, n_pairs),
        grid_spec=pltpu.PrefetchScalarGridSpec(
            num_scalar_prefetch=6,
            grid=(n_tiles + 2,),
            in_specs=[pl.BlockSpec((TM_MOE, ROW_TILES, LANES), lambda i, te, tq, nv, *_: (tile(i, nv), 0, 0)),
                      wspec(D_MODEL, D_EXPERT), wspec(D_MODEL, D_EXPERT), wspec(D_EXPERT, D_MODEL)],
            out_specs=pl.BlockSpec(memory_space=pl.ANY),
            scratch_shapes=[pltpu.VMEM((2, TM_MOE, ROW_TILES, LANES), F32),
                            pltpu.SemaphoreType.DMA((2,)),
                            pltpu.VMEM((D_MODEL, D_EXPERT), BF16),
                            pltpu.VMEM((D_MODEL, D_EXPERT), BF16),
                            pltpu.VMEM((D_EXPERT, D_MODEL), BF16)]),
        out_shape=jax.ShapeDtypeStruct((n_pairs + 2 * TM_MOE, ROW_TILES, LANES), F32),
        compiler_params=pltpu.CompilerParams(dimension_semantics=("arbitrary",), vmem_limit_bytes=VMEM_LIMIT),
        name="moe_experts",
    )(tile_e, tile_q0, n_valid, pair_start, counts, order, x_disp, w_gate, w_up, w_down)


def _combine_body(n_p_tiles, h2_ref, rec_ref, gf_ref, y_ref, outp_ref, outs_ref):
    i = pl.program_id(0)
    rec = rec_ref[...]
    w1 = rec[:, 2:3]
    w2 = rec[:, 3:4]
    h = jnp.concatenate(
        [w1 * y_ref[:, 0, s, :] + w2 * y_ref[:, 1, s, :] for s in range(ROW_TILES)], axis=-1)
    out = _rmsnorm(h2_ref[...] + h, gf_ref[...])

    @pl.when(i < n_p_tiles)
    def _():
        outp_ref[...] = out

    @pl.when(i >= n_p_tiles)
    def _():
        outs_ref[...] = out


def _combine(h2, rec, y_pairs, final_norm, t_p, t_s):
    n_p, n_s = t_p // TM, t_s // TM
    y4 = y_pairs.reshape(y_pairs.shape[0] // 2, 2, ROW_TILES, LANES)
    return pl.pallas_call(
        functools.partial(_combine_body, n_p),
        grid=(n_p + n_s,),
        in_specs=[pl.BlockSpec((TM, D_MODEL), lambda i: (i, 0)),
                  pl.BlockSpec((TM, ROUTE_W), lambda i: (i, 0)),
                  pl.BlockSpec((1, D_MODEL), lambda i: (0, 0)),
                  pl.BlockSpec((TM, 2, ROW_TILES, LANES), lambda i: (i, 0, 0, 0))],
        out_specs=[pl.BlockSpec((TM, D_MODEL), lambda i: (jnp.minimum(i, n_p - 1), 0)),
                   pl.BlockSpec((TM, D_MODEL), lambda i: (jnp.clip(i - n_p, 0, n_s - 1), 0))],
        out_shape=[jax.ShapeDtypeStruct((t_p, D_MODEL), F32),
                   jax.ShapeDtypeStruct((t_s, D_MODEL), F32)],
        compiler_params=pltpu.CompilerParams(dimension_semantics=("arbitrary",), vmem_limit_bytes=VMEM_LIMIT),
        name="combine_norm",
    )(h2, rec, final_norm, y4)


def kernel(x_prompt, x_sample, state_conv, state_ssm_re, state_ssm_im, meta_tokens, norm1, w_in, conv_w,
           lam_re, lam_im, log_dt, ssm_b_re, ssm_b_im, ssm_c_re, ssm_c_im, ssm_d, w_glu, w_conv_out,
           w_ssm_out, w_o, norm2, w_coarse, w_fine, w_gate, w_up, w_down, final_norm):
    n_pb, seq, _ = x_prompt.shape
    n_sb, dec_seq, _ = x_sample.shape
    assert dec_seq == CHUNK and seq % TM == 0 and (n_sb * dec_seq) % TM == 0 and N_META == CHUNK
    t_p, t_s = n_pb * seq, n_sb * dec_seq
    xp = x_prompt.reshape(t_p, D_MODEL)
    xsm = x_sample.reshape(t_s, D_MODEL)

    xb, z, xs, zmeta, xsmeta, sga, sgb = _in_proj(xp, xsm, meta_tokens, norm1, w_in[0])

    mats = _s5_chunk_mats(lam_re[0], lam_im[0], log_dt[0], ssm_b_re[0], ssm_b_im[0], ssm_c_re[0], ssm_c_im[0])
    y5, pf_re, pf_im, sf_re, sf_im = _s5(xs, xsmeta, state_ssm_re[0], state_ssm_im[0], mats, n_pb, n_sb, seq)

    buf = state_conv[0]
    zero = jnp.zeros((n_sb, dec_seq, D_CONV), F32)
    inj1 = zero.at[:, 0].set(buf[:, 1]).reshape(t_s, D_CONV)
    inj2 = zero.at[:, 0].set(buf[:, 0]).at[:, 1].set(buf[:, 1]).reshape(t_s, D_CONV)
    merged = _mixers(xb, z, y5, xs, sga, sgb, zmeta, inj1, inj2, conv_w[0], ssm_d, w_conv_out[0], w_glu[0],
                     w_ssm_out[0], t_p // TM, seq // TM)

    w_router = jnp.concatenate(
        [w_fine[0], w_coarse[0], jnp.zeros((D_MODEL, ROUTE_W - N_EXPERTS - N_EGROUPS), F32)], axis=1)
    h2, v3, rec, cnt = _out_proj(merged, xp, xsm, norm2, w_router, w_o[0])

    n_pairs = 2 * (t_p + t_s)
    n_tiles = n_pairs // TM_MOE + N_EXPERTS
    plan = _moe_plan(rec, cnt, n_tiles)
    x_disp = _dispatch(v3, plan, n_tiles)
    y_pairs = _moe(x_disp, plan, w_gate[0], w_up[0], w_down[0], n_tiles, n_pairs)
    y_p, y_s = _combine(h2, rec, y_pairs, final_norm.reshape(1, D_MODEL), t_p, t_s)

    new_conv_p = jnp.stack([z[(b + 1) * seq - 2:(b + 1) * seq] for b in range(n_pb)])
    new_conv_s = z[t_p:].reshape(n_sb, dec_seq, D_CONV)[:, dec_seq - 2:]
    return (y_p.reshape(n_pb, seq, D_MODEL), y_s.reshape(n_sb, dec_seq, D_MODEL),
            new_conv_p[None], pf_re, pf_im, new_conv_s[None], sf_re, sf_im)
```

```python
import functools

import jax
import jax.numpy as jnp
from jax import lax
from jax.experimental import pallas as pl
from jax.experimental.pallas import tpu as pltpu

F32 = jnp.float32
BF16 = jnp.bfloat16
I32 = jnp.int32

D_MODEL = 2048
D_CONV = 1024
D_SSM = 1024
SSM_H = 16
SSM_G = 64
SSM_P = 64
N_META = 16
N_EGROUPS = 4
EXPERTS_PER_GROUP = 8
N_EXPERTS = 32
D_EXPERT = 256
EPS = 1e-6

LANES = 128
SUBLANES = 8

CHUNK = 16
CHUNK_W = CHUNK * SSM_H

TM = 256
TM_MOE = 256
VMEM_LIMIT = 52 * 1024 * 1024


def _rmsnorm(x, g):
    return x * lax.rsqrt(jnp.mean(x * x, axis=-1, keepdims=True) + EPS) * g


def _sigmoid(x):
    return 1.0 / (1.0 + jnp.exp(-x))


def _gelu_tanh(x):
    c = 0.7978845608028654
    return 0.5 * x * (1.0 + jnp.tanh(c * (x + 0.044715 * (x * x * x))))


def _weight_copy(w_hbm, stage, sem, c, slot, rows, col0, ncols):
    return pltpu.make_async_copy(
        w_hbm.at[pl.ds(c * rows, rows), pl.ds(col0, ncols)], stage.at[slot], sem.at[slot])


def _load_weight_bf16(w_hbm, w_vmem, stage, sem, col0=0):
    k, n = w_vmem.shape
    rows = stage.shape[1]
    nchunk = k // rows
    _weight_copy(w_hbm, stage, sem, 0, 0, rows, col0, n).start()
    for c in range(nchunk):
        slot = c % 2
        if c + 1 < nchunk:
            _weight_copy(w_hbm, stage, sem, c + 1, 1 - slot, rows, col0, n).start()
        _weight_copy(w_hbm, stage, sem, c, slot, rows, col0, n).wait()
        w_vmem[pl.ds(c * rows, rows), :] = stage[slot].astype(BF16)


def _in_proj_mix_body(n_p_tiles, xp_ref, xsm_ref, meta_ref, g_ref, w_hbm,
                      xb_out, z_out, xs_out, zmeta_out, xsmeta_out,
                      w_vmem, stage, sem):
    i = pl.program_id(0)
    g = g_ref[...]

    def project(u):
        xb = jnp.dot(u, w_vmem[:, 0:D_CONV], preferred_element_type=F32)
        xc = jnp.dot(u, w_vmem[:, D_CONV:2 * D_CONV], preferred_element_type=F32)
        xv = jnp.dot(u, w_vmem[:, 2 * D_CONV:3 * D_CONV], preferred_element_type=F32)
        xs = jnp.dot(u, w_vmem[:, 3 * D_CONV:3 * D_CONV + D_SSM], preferred_element_type=F32)
        return xb, xc * xv, xs

    @pl.when(i == 0)
    def _():
        _load_weight_bf16(w_hbm, w_vmem, stage, sem, col0=0)
        um = _rmsnorm(meta_ref[...], g).astype(BF16)
        _, zm, xsm = project(um)
        zmeta_out[...] = zm
        xsmeta_out[...] = xsm

    x = jnp.where(i < n_p_tiles, xp_ref[...], xsm_ref[...])
    u = _rmsnorm(x, g).astype(BF16)
    xb, z, xs = project(u)
    xb_out[...] = xb.astype(BF16)
    z_out[...] = z
    xs_out[...] = xs


def _in_proj_gate_body(n_p_tiles, xp_ref, xsm_ref, g_ref, w_hbm, ga_out, gb_out,
                       w_vmem, stage, sem):
    i = pl.program_id(0)

    @pl.when(i == 0)
    def _():
        _load_weight_bf16(w_hbm, w_vmem, stage, sem, col0=3 * D_CONV + D_SSM)

    x = jnp.where(i < n_p_tiles, xp_ref[...], xsm_ref[...])
    u = _rmsnorm(x, g_ref[...]).astype(BF16)
    ga = jnp.dot(u, w_vmem[:, 0:D_MODEL], preferred_element_type=F32)
    ga_out[...] = _sigmoid(ga).astype(BF16)
    gb = jnp.dot(u, w_vmem[:, D_MODEL:2 * D_MODEL], preferred_element_type=F32)
    gb_out[...] = _sigmoid(gb).astype(BF16)


def _two_stream_specs(n_p_tiles, n_s_tiles):
    xp_spec = pl.BlockSpec((TM, D_MODEL), lambda i: (jnp.minimum(i, n_p_tiles - 1), 0))
    xs_spec = pl.BlockSpec((TM, D_MODEL), lambda i: (jnp.clip(i - n_p_tiles, 0, n_s_tiles - 1), 0))
    return xp_spec, xs_spec


def _in_proj(xp, xsm, meta, norm1, w_in):
    t_p, t_s = xp.shape[0], xsm.shape[0]
    n_p, n_s = t_p // TM, t_s // TM
    t = t_p + t_s
    half = 3 * D_CONV + D_SSM
    xp_spec, xs_spec = _two_stream_specs(n_p, n_s)
    g_spec = pl.BlockSpec((1, D_MODEL), lambda i: (0, 0))
    any_spec = pl.BlockSpec(memory_space=pl.ANY)
    stage_rows = 128
    row = lambda w: pl.BlockSpec((TM, w), lambda i: (i, 0))
    const = lambda r, w: pl.BlockSpec((r, w), lambda i: (0, 0))
    params = pltpu.CompilerParams(dimension_semantics=("arbitrary",), vmem_limit_bytes=VMEM_LIMIT)
    scratch = [pltpu.VMEM((D_MODEL, half), BF16),
               pltpu.VMEM((2, stage_rows, half), F32),
               pltpu.SemaphoreType.DMA((2,))]

    xb, z, xs, zmeta, xsmeta = pl.pallas_call(
        functools.partial(_in_proj_mix_body, n_p),
        grid=(n_p + n_s,),
        in_specs=[xp_spec, xs_spec, const(N_META, D_MODEL), g_spec, any_spec],
        out_specs=[row(D_CONV), row(D_CONV), row(D_SSM), const(N_META, D_CONV), const(N_META, D_SSM)],
        out_shape=[jax.ShapeDtypeStruct((t, D_CONV), BF16),
                   jax.ShapeDtypeStruct((t, D_CONV), F32),
                   jax.ShapeDtypeStruct((t, D_SSM), F32),
                   jax.ShapeDtypeStruct((N_META, D_CONV), F32),
                   jax.ShapeDtypeStruct((N_META, D_SSM), F32)],
        scratch_shapes=scratch,
        compiler_params=params,
        name="in_proj_mix",
    )(xp, xsm, meta, norm1, w_in)

    sga, sgb = pl.pallas_call(
        functools.partial(_in_proj_gate_body, n_p),
        grid=(n_p + n_s,),
        in_specs=[xp_spec, xs_spec, g_spec, any_spec],
        out_specs=[row(D_MODEL), row(D_MODEL)],
        out_shape=[jax.ShapeDtypeStruct((t, D_MODEL), BF16),
                   jax.ShapeDtypeStruct((t, D_MODEL), BF16)],
        scratch_shapes=scratch,
        compiler_params=params,
        name="in_proj_gate",
    )(xp, xsm, norm1, w_in)
    return xb, z, xs, zmeta, xsmeta, sga, sgb


S5_GROUPS_PER_STEP = LANES // SSM_H
S5_PAIRS_PER_STEP = S5_GROUPS_PER_STEP // 2


def _s5_chunk_mats(lam_re, lam_im, log_dt, b_re, b_im, c_re, c_im):
    dt = jnp.exp(log_dt)[:, None]
    lr, li = lam_re, lam_im
    mag = jnp.exp(lr * dt)
    ab_re, ab_im = mag * jnp.cos(li * dt), mag * jnp.sin(li * dt)
    nr, ni = ab_re - 1.0, ab_im
    den = lr * lr + li * li
    k_re = (nr * lr + ni * li) / den
    k_im = (ni * lr - nr * li) / den
    bb_re = k_re[..., None] * b_re - k_im[..., None] * b_im
    bb_im = k_re[..., None] * b_im + k_im[..., None] * b_re
    d = jnp.arange(CHUNK + 1, dtype=F32)[:, None, None]
    pmag = jnp.exp(d * (lr * dt)[None])
    pw_re = pmag * jnp.cos(d * (li * dt)[None])
    pw_im = pmag * jnp.sin(d * (li * dt)[None])
    ab_b_re = pw_re[..., None] * bb_re[None] - pw_im[..., None] * bb_im[None]
    ab_b_im = pw_re[..., None] * bb_im[None] + pw_im[..., None] * bb_re[None]
    m = jnp.sum(c_re[None, :, :, :, None] * ab_b_re[:CHUNK, :, None, :, :]
                - c_im[None, :, :, :, None] * ab_b_im[:CHUNK, :, None, :, :], axis=3)
    rows = []
    zero = jnp.zeros((SSM_G, SSM_H, SSM_H), F32)
    for t in range(CHUNK):
        rows.append(jnp.concatenate([m[t - j] if t >= j else zero for j in range(CHUNK)], axis=2))
    toe_t = jnp.concatenate(rows, axis=1)
    p_re = jnp.transpose(ab_b_re[:CHUNK][::-1], (1, 2, 0, 3)).reshape(SSM_G, SSM_P, CHUNK_W)
    p_im = jnp.transpose(ab_b_im[:CHUNK][::-1], (1, 2, 0, 3)).reshape(SSM_G, SSM_P, CHUNK_W)
    p_t = jnp.concatenate([p_re, p_im], axis=1)
    pr, pi = pw_re[1:], pw_im[1:]
    q_re = c_re[None] * pr[:, :, None, :] - c_im[None] * pi[:, :, None, :]
    q_im = -(c_re[None] * pi[:, :, None, :] + c_im[None] * pr[:, :, None, :])
    q_re = jnp.transpose(q_re, (1, 0, 2, 3)).reshape(SSM_G, CHUNK_W, SSM_P)
    q_im = jnp.transpose(q_im, (1, 0, 2, 3)).reshape(SSM_G, CHUNK_W, SSM_P)
    even = (jnp.arange(SSM_G) % 2 == 0)[:, None, None]
    zq = jnp.zeros_like(q_re)
    pad2 = lambda q: jnp.where(even, jnp.concatenate([q, zq], axis=2), jnp.concatenate([zq, q], axis=2))
    q_t = jnp.stack([pad2(q_re), pad2(q_im)], axis=1)
    a16_re = pw_re[CHUNK].reshape(SSM_G // 2, 1, 2 * SSM_P)
    a16_im = pw_im[CHUNK].reshape(SSM_G // 2, 1, 2 * SSM_P)
    return toe_t.astype(BF16), p_t.astype(BF16), q_t.astype(BF16), a16_re, a16_im


def _s5_body(n_pc, n_pb, n_sb, xs_ref, xsmeta_ref, toe_ref, p_ref, q_ref, are_ref, aim_ref, s0re_ref, s0im_ref,
             y_out, pfre_out, pfim_out, sfre_out, sfim_out,
             xt_scr, u_scr, st_re, st_im, sl_re, sl_im, sp_re, sp_im, yt_scr):
    gb = S5_GROUPS_PER_STEP
    n_p_rows = n_pb * n_pc
    row_s = n_p_rows
    row_m = row_s + n_sb
    t_p = n_p_rows * CHUNK
    rows_pad = xt_scr.shape[0]

    xt_scr[row_m + 1:rows_pad, :] = jnp.zeros((rows_pad - row_m - 1, LANES), F32)
    for t in range(CHUNK):
        xt_scr[0:n_p_rows, :] = xs_ref[pl.ds(t, n_p_rows, stride=CHUNK), :]
        xt_scr[row_s:row_m, :] = xs_ref[pl.ds(t_p + t, n_sb, stride=CHUNK), :]
        xt_scr[row_m:row_m + 1, :] = xsmeta_ref[t:t + 1, :]
        xt = xt_scr[...].T.astype(BF16)
        for k in range(gb):
            u_scr[k, t * SSM_H:(t + 1) * SSM_H, :] = xt[k * SSM_H:(k + 1) * SSM_H, :]

    for k in range(gb):
        sl = jnp.dot(p_ref[k], u_scr[k], preferred_element_type=F32)
        half = (k % 2) * SSM_P
        st_re[k // 2, half:half + SSM_P, :] = sl[0:SSM_P, :]
        st_im[k // 2, half:half + SSM_P, :] = sl[SSM_P:2 * SSM_P, :]
    npair = S5_PAIRS_PER_STEP
    for j in range(npair):
        sl_re[j] = st_re[j].T
        sl_im[j] = st_im[j].T
        sp_re[j, row_m:rows_pad, :] = jnp.zeros((rows_pad - row_m, 2 * SSM_P), F32)
        sp_im[j, row_m:rows_pad, :] = jnp.zeros((rows_pad - row_m, 2 * SSM_P), F32)

    ar = [are_ref[j] for j in range(npair)]
    ai = [aim_ref[j] for j in range(npair)]
    sre = [jnp.broadcast_to(sl_re[j, row_m:row_m + 1, :], (n_pb, 2 * SSM_P)) for j in range(npair)]
    sim = [jnp.broadcast_to(sl_im[j, row_m:row_m + 1, :], (n_pb, 2 * SSM_P)) for j in range(npair)]
    for c in range(n_pc):
        rows = pl.ds(c, n_pb, stride=n_pc)
        for j in range(npair):
            sp_re[j, rows, :] = sre[j]
            sp_im[j, rows, :] = sim[j]
            nre = ar[j] * sre[j] - ai[j] * sim[j] + sl_re[j, rows, :]
            nim = ar[j] * sim[j] + ai[j] * sre[j] + sl_im[j, rows, :]
            sre[j], sim[j] = nre, nim
    for j in range(npair):
        pfre_out[j] = sre[j]
        pfim_out[j] = sim[j]
        s0r, s0i = s0re_ref[j], s0im_ref[j]
        sp_re[j, row_s:row_m, :] = s0r
        sp_im[j, row_s:row_m, :] = s0i
        sfre_out[j] = ar[j] * s0r - ai[j] * s0i + sl_re[j, row_s:row_m, :]
        sfim_out[j] = ar[j] * s0i + ai[j] * s0r + sl_im[j, row_s:row_m, :]

    nt = (((1,), (1,)), ((), ()))
    for k in range(gb):
        y = jnp.dot(toe_ref[k], u_scr[k], preferred_element_type=F32)
        y += lax.dot_general(q_ref[k, 0], sp_re[k // 2].astype(BF16), nt, preferred_element_type=F32)
        y += lax.dot_general(q_ref[k, 1], sp_im[k // 2].astype(BF16), nt, preferred_element_type=F32)
        for t in range(CHUNK):
            yt_scr[t, k * SSM_H:(k + 1) * SSM_H, :] = y[t * SSM_H:(t + 1) * SSM_H, :]
    for t in range(CHUNK):
        yt = yt_scr[t].T
        y_out[pl.ds(t, n_p_rows, stride=CHUNK), :] = yt[0:n_p_rows, :]
        y_out[pl.ds(t_p + t, n_sb, stride=CHUNK), :] = yt[row_s:row_m, :]


def _s5(xs, xsmeta, state_re, state_im, mats, n_pb, n_sb, seq):
    toe_t, p_t, q_t, a_re, a_im = mats
    t = xs.shape[0]
    n_pc = seq // CHUNK
    rows = n_pc * n_pb + n_sb + 1
    rows_pad = -(-rows // LANES) * LANES
    gb, npair = S5_GROUPS_PER_STEP, S5_PAIRS_PER_STEP
    pairs = lambda s: jnp.transpose(s.reshape(n_sb, SSM_G // 2, 2 * SSM_P), (1, 0, 2))
    blk3 = lambda n, r, c: pl.BlockSpec((n, r, c), lambda i: (i, 0, 0))
    y, pfre, pfim, sfre, sfim = pl.pallas_call(
        functools.partial(_s5_body, n_pc, n_pb, n_sb),
        grid=(SSM_G // gb,),
        in_specs=[pl.BlockSpec((t, LANES), lambda i: (0, i)),
                  pl.BlockSpec((N_META, LANES), lambda i: (0, i)),
                  blk3(gb, CHUNK_W, CHUNK_W), blk3(gb, 2 * SSM_P, CHUNK_W),
                  pl.BlockSpec((gb, 2, CHUNK_W, 2 * SSM_P), lambda i: (i, 0, 0, 0)),
                  blk3(npair, 1, 2 * SSM_P), blk3(npair, 1, 2 * SSM_P),
                  blk3(npair, n_sb, 2 * SSM_P), blk3(npair, n_sb, 2 * SSM_P)],
        out_specs=[pl.BlockSpec((t, LANES), lambda i: (0, i)),
                   blk3(npair, n_pb, 2 * SSM_P), blk3(npair, n_pb, 2 * SSM_P),
                   blk3(npair, n_sb, 2 * SSM_P), blk3(npair, n_sb, 2 * SSM_P)],
        out_shape=[jax.ShapeDtypeStruct((t, D_SSM), F32),
                   jax.ShapeDtypeStruct((SSM_G // 2, n_pb, 2 * SSM_P), F32),
                   jax.ShapeDtypeStruct((SSM_G // 2, n_pb, 2 * SSM_P), F32),
                   jax.ShapeDtypeStruct((SSM_G // 2, n_sb, 2 * SSM_P), F32),
                   jax.ShapeDtypeStruct((SSM_G // 2, n_sb, 2 * SSM_P), F32)],
        scratch_shapes=[pltpu.VMEM((rows_pad, LANES), F32),
                        pltpu.VMEM((gb, CHUNK_W, rows_pad), BF16),
                        pltpu.VMEM((npair, 2 * SSM_P, rows_pad), F32),
                        pltpu.VMEM((npair, 2 * SSM_P, rows_pad), F32),
                        pltpu.VMEM((npair, rows_pad, 2 * SSM_P), F32),
                        pltpu.VMEM((npair, rows_pad, 2 * SSM_P), F32),
                        pltpu.VMEM((npair, rows_pad, 2 * SSM_P), F32),
                        pltpu.VMEM((npair, rows_pad, 2 * SSM_P), F32),
                        pltpu.VMEM((CHUNK, LANES, rows_pad), F32)],
        compiler_params=pltpu.CompilerParams(dimension_semantics=("arbitrary",), vmem_limit_bytes=VMEM_LIMIT),
        name="s5_chunks",
    )(xs, xsmeta, toe_t, p_t, q_t, a_re, a_im, pairs(state_re), pairs(state_im))
    unpair = lambda a: jnp.transpose(a, (1, 0, 2)).reshape(a.shape[1], SSM_G, SSM_P)[None]
    return y, unpair(pfre), unpair(pfim), unpair(sfre), unpair(sfim)


def _mixers_body(n_p_tiles, tiles_per_seq, xb_ref, z_ref, y5_ref, xs_ref, sga_ref, sgb_ref,
                 zmeta_ref, inj1_ref, inj2_ref, cw_ref, dskip_ref,
                 wc_hbm, wg_hbm, wso_hbm, merged_out,
                 wc, wg, wso, stage_a, stage_b, sem, carry):
    i = pl.program_id(0)

    @pl.when(i == 0)
    def _():
        _load_weight_bf16(wc_hbm, wc, stage_a, sem)
        _load_weight_bf16(wg_hbm, wg, stage_b, sem)
        _load_weight_bf16(wso_hbm, wso, stage_a, sem)

    @pl.when(jnp.logical_and(i < n_p_tiles, i % tiles_per_seq == 0))
    def _():
        carry[0:2, :] = zmeta_ref[N_META - 2:N_META, :]

    z = z_ref[...]
    row = lax.broadcasted_iota(I32, (TM, 1), 0)
    is_s = i >= n_p_tiles
    r1 = pltpu.roll(z, 1, 0)
    r2 = pltpu.roll(z, 2, 0)
    c1 = carry[1:2, :]
    c2 = carry[0:1, :]
    pos = jnp.where(is_s, row & (CHUNK - 1), row)
    first1 = pos == 0
    first2 = pos < 2
    fill1 = jnp.where(is_s, inj1_ref[...], jnp.broadcast_to(c1, z.shape))
    fill2 = jnp.where(is_s, inj2_ref[...], jnp.where(row == 0, c2, c1))
    zp1 = jnp.where(first1, fill1, r1)
    zp2 = jnp.where(first2, fill2, r2)
    carry[0:2, :] = z[TM - 2:TM, :]

    cw = cw_ref[...]
    conv = cw[0:1, :] * zp2 + cw[1:2, :] * zp1 + cw[2:3, :] * z
    a_in = (xb_ref[...].astype(F32) * conv).astype(BF16)
    ya = jnp.dot(a_in, wc[...], preferred_element_type=F32)

    ys = y5_ref[...] + dskip_ref[...] * xs_ref[...]
    ys = _gelu_tanh(ys)
    glu = jnp.dot(ys.astype(BF16), wg[...], preferred_element_type=F32)
    ys = ys * _sigmoid(glu)
    yb = jnp.dot(ys.astype(BF16), wso[...], preferred_element_type=F32)

    merged = sga_ref[...].astype(F32) * ya + sgb_ref[...].astype(F32) * yb
    merged_out[...] = merged.astype(BF16)


def _mixers(xb, z, y5, xs, sga, sgb, zmeta, inj1, inj2, conv_w, d_skip, w_conv_out, w_glu, w_ssm_out,
            n_p_tiles, tiles_per_seq):
    t = xb.shape[0]
    n_s_tiles = inj1.shape[0] // TM
    row = lambda w: pl.BlockSpec((TM, w), lambda i: (i, 0))
    const = lambda r, w: pl.BlockSpec((r, w), lambda i: (0, 0))
    inj = pl.BlockSpec((TM, D_CONV), lambda i: (jnp.clip(i - n_p_tiles, 0, n_s_tiles - 1), 0))
    any_spec = pl.BlockSpec(memory_space=pl.ANY)
    return pl.pallas_call(
        functools.partial(_mixers_body, n_p_tiles, tiles_per_seq),
        grid=(t // TM,),
        in_specs=[row(D_CONV), row(D_CONV), row(D_SSM), row(D_SSM), row(D_MODEL), row(D_MODEL),
                  const(N_META, D_CONV), inj, inj, const(3, D_CONV), const(1, D_SSM),
                  any_spec, any_spec, any_spec],
        out_specs=row(D_MODEL),
        out_shape=jax.ShapeDtypeStruct((t, D_MODEL), BF16),
        scratch_shapes=[pltpu.VMEM((D_CONV, D_MODEL), BF16),
                        pltpu.VMEM((D_SSM, D_SSM), BF16),
                        pltpu.VMEM((D_SSM, D_MODEL), BF16),
                        pltpu.VMEM((2, 256, D_MODEL), F32),
                        pltpu.VMEM((2, 256, D_SSM), F32),
                        pltpu.SemaphoreType.DMA((2,)),
                        pltpu.VMEM((8, D_CONV), F32)],
        compiler_params=pltpu.CompilerParams(dimension_semantics=("arbitrary",), vmem_limit_bytes=VMEM_LIMIT),
        name="mixers",
    )(xb, z, y5, xs, sga, sgb, zmeta, inj1, inj2, conv_w, d_skip, w_conv_out, w_glu, w_ssm_out)


ROUTE_W = LANES
COARSE0 = N_EXPERTS


def _route(logits, cnt):
    col = lax.broadcasted_iota(I32, logits.shape, 1)
    colf = col.astype(F32)
    neg = jnp.float32(-jnp.inf)
    big = jnp.float32(1 << 20)
    is_c = jnp.logical_and(col >= COARSE0, col < COARSE0 + N_EGROUPS)
    lc = jnp.where(is_c, logits, neg)
    cmax = jnp.max(lc, axis=-1, keepdims=True)
    gi = jnp.min(jnp.where(lc == cmax, colf - COARSE0, big), axis=-1, keepdims=True)
    pg = 1.0 / jnp.sum(jnp.where(is_c, jnp.exp(lc - cmax), 0.0), axis=-1, keepdims=True)
    grp = (col >> 3).astype(F32)
    in_g = jnp.logical_and(col < N_EXPERTS, grp == gi)
    lf = jnp.where(in_g, logits, neg)
    m1 = jnp.max(lf, axis=-1, keepdims=True)
    i1 = jnp.min(jnp.where(lf == m1, colf, big), axis=-1, keepdims=True)
    lf2 = jnp.where(colf == i1, neg, lf)
    m2 = jnp.max(lf2, axis=-1, keepdims=True)
    i2 = jnp.min(jnp.where(lf2 == m2, colf, big), axis=-1, keepdims=True)
    e2 = jnp.exp(m2 - m1)
    w1 = pg / (1.0 + e2)
    w2 = pg * e2 / (1.0 + e2)
    n = logits.shape[0]
    hit1 = colf == i1
    hit2 = colf == i2
    onehot = jnp.where(jnp.logical_or(hit1, hit2), 1.0, 0.0)
    rr = lax.broadcasted_iota(I32, (n, n), 0)
    cc = lax.broadcasted_iota(I32, (n, n), 1)
    tri = jnp.where(cc < rr, 1.0, 0.0).astype(BF16)
    pos = jnp.dot(tri, onehot.astype(BF16), preferred_element_type=F32) + cnt
    rank1 = jnp.sum(jnp.where(hit1, pos, 0.0), axis=-1, keepdims=True)
    rank2 = jnp.sum(jnp.where(hit2, pos, 0.0), axis=-1, keepdims=True)
    vals = (i1, i2, w1, w2, rank1, rank2)
    rec = jnp.zeros(logits.shape, F32)
    for c, val in enumerate(vals):
        rec = jnp.where(col == c, val, rec)
    return rec, cnt + jnp.sum(onehot, axis=0, keepdims=True)


def _split_bf16(a):
    hi = a.astype(BF16)
    lo = (a - hi.astype(F32)).astype(BF16)
    return hi, lo


def _out_proj_body(n_p_tiles, merged_ref, xp_ref, xsm_ref, g2_ref, wr_ref, wo_hbm,
                   h2_out, v_out, rec_out, cnt_out, wo, stage, sem):
    i = pl.program_id(0)

    @pl.when(i == 0)
    def _():
        _load_weight_bf16(wo_hbm, wo, stage, sem)
        cnt_out[...] = jnp.zeros(cnt_out.shape, F32)

    x = jnp.where(i < n_p_tiles, xp_ref[...], xsm_ref[...])
    h2 = x + jnp.dot(merged_ref[...], wo[...], preferred_element_type=F32)
    h2_out[...] = h2
    v = _rmsnorm(h2, g2_ref[...])
    v_out[...] = v
    v_hi, v_lo = _split_bf16(v)
    w_hi, w_lo = _split_bf16(wr_ref[...])
    logits = (jnp.dot(v_hi, w_hi, preferred_element_type=F32)
              + jnp.dot(v_lo, w_hi, preferred_element_type=F32)
              + jnp.dot(v_hi, w_lo, preferred_element_type=F32))
    rec, cnt = _route(logits, cnt_out[...])
    rec_out[...] = rec
    cnt_out[...] = cnt


def _out_proj(merged, xp, xsm, norm2, w_router, w_o):
    t_p, t_s = xp.shape[0], xsm.shape[0]
    n_p, n_s = t_p // TM, t_s // TM
    t = t_p + t_s
    xp_spec, xs_spec = _two_stream_specs(n_p, n_s)
    row = lambda w: pl.BlockSpec((TM, w), lambda i: (i, 0))
    const = lambda r, w: pl.BlockSpec((r, w), lambda i: (0, 0))
    return pl.pallas_call(
        functools.partial(_out_proj_body, n_p),
        grid=(n_p + n_s,),
        in_specs=[row(D_MODEL), xp_spec, xs_spec, const(1, D_MODEL), const(D_MODEL, ROUTE_W),
                  pl.BlockSpec(memory_space=pl.ANY)],
        out_specs=[row(D_MODEL), row(D_MODEL), row(ROUTE_W), const(1, ROUTE_W)],
        out_shape=[jax.ShapeDtypeStruct((t, D_MODEL), F32),
                   jax.ShapeDtypeStruct((t, D_MODEL), F32),
                   jax.ShapeDtypeStruct((t, ROUTE_W), F32),
                   jax.ShapeDtypeStruct((1, ROUTE_W), F32)],
        scratch_shapes=[pltpu.VMEM((D_MODEL, D_MODEL), BF16),
                        pltpu.VMEM((2, 256, D_MODEL), F32),
                        pltpu.SemaphoreType.DMA((2,))],
        compiler_params=pltpu.CompilerParams(dimension_semantics=("arbitrary",), vmem_limit_bytes=VMEM_LIMIT),
        name="out_proj_route",
    )(merged, xp, xsm, norm2, w_router, w_o)


def _moe_plan(rec, cnt, n_tiles):
    t = rec.shape[0]
    n_pairs = 2 * t
    eid = rec[:, 0:2].astype(I32).reshape(-1)
    rank = rec[:, 4:6].astype(I32).reshape(-1)
    counts = cnt[0, :N_EXPERTS].astype(I32)
    pair_start = jnp.cumsum(counts) - counts
    experts = jnp.arange(N_EXPERTS, dtype=I32)
    onehot = (eid[:, None] == experts[None, :]).astype(I32)
    pos = rank + jnp.sum(onehot * pair_start[None, :], axis=1)
    _, order = lax.sort((pos, jnp.arange(n_pairs, dtype=I32)), num_keys=1)
    tiles_e = (counts + TM_MOE - 1) // TM_MOE
    tile_end = jnp.cumsum(tiles_e)
    tile_start = tile_end - tiles_e
    n_valid = tile_end[-1]
    tile_ids = jnp.arange(n_tiles, dtype=I32)
    tile_e = jnp.sum((tile_ids[:, None] >= tile_end[None, :]).astype(I32), axis=1)
    last_e = jnp.sum((n_valid - 1 >= tile_end).astype(I32))
    tile_e = jnp.minimum(jnp.where(tile_ids < n_valid, tile_e, last_e), N_EXPERTS - 1)
    tile_onehot = (tile_e[:, None] == experts[None, :]).astype(I32)
    tile_q0 = (tile_ids - jnp.sum(tile_onehot * tile_start[None, :], axis=1)) * TM_MOE
    tile_q0 = jnp.where(tile_ids < n_valid, tile_q0, 0)
    dst = rank + jnp.sum(onehot * (tile_start * TM_MOE)[None, :], axis=1)
    pad_start = tile_start * TM_MOE + counts
    pad_len = tiles_e * TM_MOE - counts
    return (dst.astype(I32), pad_start.astype(I32), pad_len.astype(I32), tile_e, tile_q0.astype(I32),
            n_valid.astype(I32).reshape(1), pair_start, counts, order)


DISPATCH_BUFS = 3
ROW_PIECES = tuple(TM_MOE >> (b + 1) for b in range(TM_MOE.bit_length() - 1))


def _dispatch_body(dst_ref, pad_start_ref, pad_len_ref, nvalid_ref, v_hbm, x_hbm, buf, zbuf, rsem, ssem, zsem):
    i = pl.program_id(0)
    n = pl.num_programs(0)

    def read(tile, slot):
        return pltpu.make_async_copy(v_hbm.at[pl.ds(pl.multiple_of(tile * TM, TM), TM)], buf.at[slot],
                                     rsem.at[slot])

    def row_write(slot, r, dst_row):
        return pltpu.make_async_copy(buf.at[slot, pl.ds(r, 1)], x_hbm.at[pl.ds(dst_row, 1)], ssem.at[slot])

    def drain(slot):
        def body(r, c):
            row_write(slot, 0, 0).wait()
            row_write(slot, 0, 0).wait()
            return c
        lax.fori_loop(0, TM, body, 0, unroll=8)

    def pad_fill(go):
        def zero_rows(start, size):
            d = pltpu.make_async_copy(zbuf.at[pl.ds(0, size)], x_hbm.at[pl.ds(start, size)], zsem)
            d.start() if go else d.wait()

        def body(e, c):
            start, length = pad_start_ref[e], pad_len_ref[e]
            head = (-start) & (SUBLANES - 1)
            for h in range(SUBLANES - 1):
                @pl.when(h < head)
                def _(h=h):
                    zero_rows(start + h, 1)
            start, length = start + head, length - head
            for size in ROW_PIECES:
                if size >= SUBLANES:
                    @pl.when((length & size) != 0)
                    def _(size=size):
                        zero_rows(pl.multiple_of(start + (length & (-2 * size)), SUBLANES), size)
            return c
        lax.fori_loop(0, N_EXPERTS, body, 0)

        def unused(tile, c):
            for half in range(TM_MOE // ROW_PIECES[0]):
                zero_rows(pl.multiple_of(tile * TM_MOE + half * ROW_PIECES[0], SUBLANES), ROW_PIECES[0])
            return c
        lax.fori_loop(nvalid_ref[0], x_hbm.shape[0] // TM_MOE, unused, 0)

    @pl.when(i == 0)
    def _():
        zbuf[...] = jnp.zeros(zbuf.shape, F32)
        pad_fill(True)
        read(0, 0).start()

    @pl.when(i >= 2)
    def _():
        drain((i + 1) % DISPATCH_BUFS)

    @pl.when(i + 1 < n)
    def _():
        read(i + 1, (i + 1) % DISPATCH_BUFS).start()

    slot = i % DISPATCH_BUFS
    read(i, slot).wait()
    for r in range(TM):
        for k in range(2):
            row_write(slot, r, dst_ref[2 * (i * TM + r) + k]).start()

    @pl.when(i == n - 1)
    def _():
        if n >= 2:
            drain((i - 1) % DISPATCH_BUFS)
        drain(slot)
        pad_fill(False)


def _dispatch(v, plan, n_tiles):
    dst, pad_start, pad_len, n_valid = plan[0], plan[1], plan[2], plan[5]
    t = v.shape[0]
    return pl.pallas_call(
        _dispatch_body,
        grid_spec=pltpu.PrefetchScalarGridSpec(
            num_scalar_prefetch=4,
            grid=(t // TM,),
            in_specs=[pl.BlockSpec(memory_space=pl.ANY)],
            out_specs=pl.BlockSpec(memory_space=pl.ANY),
            scratch_shapes=[pltpu.VMEM((DISPATCH_BUFS, TM, D_MODEL), F32),
                            pltpu.VMEM((ROW_PIECES[0], D_MODEL), F32),
                            pltpu.SemaphoreType.DMA((DISPATCH_BUFS,)),
                            pltpu.SemaphoreType.DMA((DISPATCH_BUFS,)),
                            pltpu.SemaphoreType.DMA(())]),
        out_shape=jax.ShapeDtypeStruct((n_tiles * TM_MOE, D_MODEL), F32),
        compiler_params=pltpu.CompilerParams(dimension_semantics=("arbitrary",), vmem_limit_bytes=VMEM_LIMIT),
        name="moe_dispatch",
    )(dst, pad_start, pad_len, n_valid, v)


def _moe_body(n_tok, tile_e_ref, tile_q0_ref, nvalid_ref, pstart_ref, cnt_ref, order_ref,
              x_ref, wg_ref, wu_ref, wd_ref, o_hbm, ybuf, ssem, wg, wu, wd):
    i = pl.program_id(0)
    nv = nvalid_ref[0]
    plane = n_tok + TM_MOE

    def row_write(slot, r, dst_row):
        return pltpu.make_async_copy(ybuf.at[slot, pl.ds(r, 1)], o_hbm.at[pl.ds(dst_row, 1)], ssem.at[slot])

    def scratch_rows(slot):
        return pltpu.make_async_copy(ybuf.at[slot], o_hbm.at[pl.ds(slot * plane + n_tok, TM_MOE)], ssem.at[slot])

    def drain_writes(slot):
        def body(r, c):
            row_write(slot, 0, 0).wait()
            return c
        lax.fori_loop(0, TM_MOE, body, 0, unroll=8)

    @pl.when(i == 0)
    def _():
        ybuf[...] = jnp.zeros(ybuf.shape, F32)
        for slot in range(2):
            scratch_rows(slot).start()
        for slot in range(2):
            scratch_rows(slot).wait()

    @pl.when(jnp.logical_and(i >= 2, i < nv + 2))
    def _():
        drain_writes(i % 2)

    @pl.when(i < nv)
    def _():
        slot = i % 2
        e = tile_e_ref[i]
        prev_e = tile_e_ref[jnp.maximum(i - 1, 0)]

        @pl.when(jnp.logical_or(i == 0, e != prev_e))
        def _():
            wg[...] = wg_ref[0].astype(BF16)
            wu[...] = wu_ref[0].astype(BF16)
            wd[...] = wd_ref[0].astype(BF16)

        x = x_ref[...].astype(BF16)
        hg = jnp.dot(x, wg[...], preferred_element_type=F32)
        hu = jnp.dot(x, wu[...], preferred_element_type=F32)
        act = hg * _sigmoid(hg) * hu
        ybuf[slot] = jnp.dot(act.astype(BF16), wd[...], preferred_element_type=F32)
        valid = cnt_ref[e] - tile_q0_ref[i]
        first = pstart_ref[e] + tile_q0_ref[i]
        last = pstart_ref[e] + cnt_ref[e] - 1
        for r in range(TM_MOE):
            pair = order_ref[jnp.minimum(first + r, last)]
            row = (pair & 1) * plane + (pair >> 1)
            row_write(slot, r, jnp.where(r < valid, row, slot * plane + n_tok + r)).start()


def _moe(x_disp, plan, w_gate, w_up, w_down, n_tiles, n_tok):
    tile_e, tile_q0, n_valid, pair_start, counts, order = plan[3:9]
    tile = lambda i, nv: jnp.minimum(i, jnp.maximum(nv[0] - 1, 0))
    wspec = lambda a, b: pl.BlockSpec((1, a, b), lambda i, te, tq, nv, *_: (te[tile(i, nv)], 0, 0))
    return pl.pallas_call(
        functools.partial(_moe_body, n_tok),
        grid_spec=pltpu.PrefetchScalarGridSpec(
            num_scalar_prefetch=6,
            grid=(n_tiles + 2,),
            in_specs=[pl.BlockSpec((TM_MOE, D_MODEL), lambda i, te, tq, nv, *_: (tile(i, nv), 0)),
                      wspec(D_MODEL, D_EXPERT), wspec(D_MODEL, D_EXPERT), wspec(D_EXPERT, D_MODEL)],
            out_specs=pl.BlockSpec(memory_space=pl.ANY),
            scratch_shapes=[pltpu.VMEM((2, TM_MOE, D_MODEL), F32),
                            pltpu.SemaphoreType.DMA((2,)),
                            pltpu.VMEM((D_MODEL, D_EXPERT), BF16),
                            pltpu.VMEM((D_MODEL, D_EXPERT), BF16),
                            pltpu.VMEM((D_EXPERT, D_MODEL), BF16)]),
        out_shape=jax.ShapeDtypeStruct((2 * (n_tok + TM_MOE), D_MODEL), F32),
        compiler_params=pltpu.CompilerParams(dimension_semantics=("arbitrary",), vmem_limit_bytes=VMEM_LIMIT),
        name="moe_experts",
    )(tile_e, tile_q0, n_valid, pair_start, counts, order, x_disp, w_gate, w_up, w_down)


def _combine_body(n_p_tiles, h2_ref, rec_ref, gf_ref, y1_ref, y2_ref, outp_ref, outs_ref):
    i = pl.program_id(0)
    rec = rec_ref[...]
    h = rec[:, 2:3] * y1_ref[...] + rec[:, 3:4] * y2_ref[...]
    out = _rmsnorm(h2_ref[...] + h, gf_ref[...])

    @pl.when(i < n_p_tiles)
    def _():
        outp_ref[...] = out

    @pl.when(i >= n_p_tiles)
    def _():
        outs_ref[...] = out


def _combine(h2, rec, y_pairs, final_norm, t_p, t_s):
    n_p, n_s = t_p // TM, t_s // TM
    plane_tiles = (t_p + t_s + TM_MOE) // TM
    return pl.pallas_call(
        functools.partial(_combine_body, n_p),
        grid=(n_p + n_s,),
        in_specs=[pl.BlockSpec((TM, D_MODEL), lambda i: (i, 0)),
                  pl.BlockSpec((TM, ROUTE_W), lambda i: (i, 0)),
                  pl.BlockSpec((1, D_MODEL), lambda i: (0, 0)),
                  pl.BlockSpec((TM, D_MODEL), lambda i: (i, 0)),
                  pl.BlockSpec((TM, D_MODEL), lambda i: (i + plane_tiles, 0))],
        out_specs=[pl.BlockSpec((TM, D_MODEL), lambda i: (jnp.minimum(i, n_p - 1), 0)),
                   pl.BlockSpec((TM, D_MODEL), lambda i: (jnp.clip(i - n_p, 0, n_s - 1), 0))],
        out_shape=[jax.ShapeDtypeStruct((t_p, D_MODEL), F32),
                   jax.ShapeDtypeStruct((t_s, D_MODEL), F32)],
        compiler_params=pltpu.CompilerParams(dimension_semantics=("arbitrary",), vmem_limit_bytes=VMEM_LIMIT),
        name="combine_norm",
    )(h2, rec, final_norm, y_pairs, y_pairs)


def kernel(x_prompt, x_sample, state_conv, state_ssm_re, state_ssm_im, meta_tokens, norm1, w_in, conv_w,
           lam_re, lam_im, log_dt, ssm_b_re, ssm_b_im, ssm_c_re, ssm_c_im, ssm_d, w_glu, w_conv_out,
           w_ssm_out, w_o, norm2, w_coarse, w_fine, w_gate, w_up, w_down, final_norm):
    n_pb, seq, _ = x_prompt.shape
    n_sb, dec_seq, _ = x_sample.shape
    assert dec_seq == CHUNK and seq % TM == 0 and (n_sb * dec_seq) % TM == 0 and N_META == CHUNK
    t_p, t_s = n_pb * seq, n_sb * dec_seq
    xp = x_prompt.reshape(t_p, D_MODEL)
    xsm = x_sample.reshape(t_s, D_MODEL)

    xb, z, xs, zmeta, xsmeta, sga, sgb = _in_proj(xp, xsm, meta_tokens, norm1, w_in[0])

    mats = _s5_chunk_mats(lam_re[0], lam_im[0], log_dt[0], ssm_b_re[0], ssm_b_im[0], ssm_c_re[0], ssm_c_im[0])
    y5, pf_re, pf_im, sf_re, sf_im = _s5(xs, xsmeta, state_ssm_re[0], state_ssm_im[0], mats, n_pb, n_sb, seq)

    buf = state_conv[0]
    zero = jnp.zeros((n_sb, dec_seq, D_CONV), F32)
    inj1 = zero.at[:, 0].set(buf[:, 1]).reshape(t_s, D_CONV)
    inj2 = zero.at[:, 0].set(buf[:, 0]).at[:, 1].set(buf[:, 1]).reshape(t_s, D_CONV)
    merged = _mixers(xb, z, y5, xs, sga, sgb, zmeta, inj1, inj2, conv_w[0], ssm_d, w_conv_out[0], w_glu[0],
                     w_ssm_out[0], t_p // TM, seq // TM)

    w_router = jnp.concatenate(
        [w_fine[0], w_coarse[0], jnp.zeros((D_MODEL, ROUTE_W - N_EXPERTS - N_EGROUPS), F32)], axis=1)
    h2, v, rec, cnt = _out_proj(merged, xp, xsm, norm2, w_router, w_o[0])

    n_tiles = 2 * (t_p + t_s) // TM_MOE + N_EXPERTS
    plan = _moe_plan(rec, cnt, n_tiles)
    x_disp = _dispatch(v, plan, n_tiles)
    y_pairs = _moe(x_disp, plan, w_gate[0], w_up[0], w_down[0], n_tiles, t_p + t_s)
    y_p, y_s = _combine(h2, rec, y_pairs, final_norm.reshape(1, D_MODEL), t_p, t_s)

    new_conv_p = jnp.stack([z[(b + 1) * seq - 2:(b + 1) * seq] for b in range(n_pb)])
    new_conv_s = z[t_p:].reshape(n_sb, dec_seq, D_CONV)[:, dec_seq - 2:]
    return (y_p.reshape(n_pb, seq, D_MODEL), y_s.reshape(n_sb, dec_seq, D_MODEL),
            new_conv_p[None], pf_re, pf_im, new_conv_s[None], sf_re, sf_im)
```

```python
import functools

import jax
import jax.numpy as jnp
from jax import lax
from jax.experimental import pallas as pl
from jax.experimental.pallas import tpu as pltpu

F32 = jnp.float32
BF16 = jnp.bfloat16
I32 = jnp.int32

D_MODEL = 2048
D_CONV = 1024
D_SSM = 1024
SSM_H = 16
SSM_G = 64
SSM_P = 64
N_META = 16
N_EGROUPS = 4
EXPERTS_PER_GROUP = 8
N_EXPERTS = 32
D_EXPERT = 256
EPS = 1e-6

LANES = 128
SUBLANES = 8

CHUNK = 16
CHUNK_W = CHUNK * SSM_H

TM = 256
TM_MOE = 256
VMEM_LIMIT = 52 * 1024 * 1024


def _rmsnorm(x, g):
    return x * lax.rsqrt(jnp.mean(x * x, axis=-1, keepdims=True) + EPS) * g


def _sigmoid(x):
    return 1.0 / (1.0 + jnp.exp(-x))


def _gelu_tanh(x):
    c = 0.7978845608028654
    return 0.5 * x * (1.0 + jnp.tanh(c * (x + 0.044715 * (x * x * x))))


def _weight_copy(w_hbm, stage, sem, c, slot, rows, col0, ncols):
    return pltpu.make_async_copy(
        w_hbm.at[pl.ds(c * rows, rows), pl.ds(col0, ncols)], stage.at[slot], sem.at[slot])


def _load_weight_bf16(w_hbm, w_vmem, stage, sem, col0=0):
    k, n = w_vmem.shape
    rows = stage.shape[1]
    nchunk = k // rows
    _weight_copy(w_hbm, stage, sem, 0, 0, rows, col0, n).start()
    for c in range(nchunk):
        slot = c % 2
        if c + 1 < nchunk:
            _weight_copy(w_hbm, stage, sem, c + 1, 1 - slot, rows, col0, n).start()
        _weight_copy(w_hbm, stage, sem, c, slot, rows, col0, n).wait()
        w_vmem[pl.ds(c * rows, rows), :] = stage[slot].astype(BF16)


def _in_proj_mix_body(n_p_tiles, xp_ref, xsm_ref, meta_ref, g_ref, w_hbm,
                      xb_out, z_out, xs_out, zmeta_out, xsmeta_out,
                      w_vmem, stage, sem):
    i = pl.program_id(0)
    g = g_ref[...]

    def project(u):
        xb = jnp.dot(u, w_vmem[:, 0:D_CONV], preferred_element_type=F32)
        xc = jnp.dot(u, w_vmem[:, D_CONV:2 * D_CONV], preferred_element_type=F32)
        xv = jnp.dot(u, w_vmem[:, 2 * D_CONV:3 * D_CONV], preferred_element_type=F32)
        xs = jnp.dot(u, w_vmem[:, 3 * D_CONV:3 * D_CONV + D_SSM], preferred_element_type=F32)
        return xb, xc * xv, xs

    @pl.when(i == 0)
    def _():
        _load_weight_bf16(w_hbm, w_vmem, stage, sem, col0=0)
        um = _rmsnorm(meta_ref[...], g).astype(BF16)
        _, zm, xsm = project(um)
        zmeta_out[...] = zm
        xsmeta_out[...] = xsm

    x = jnp.where(i < n_p_tiles, xp_ref[...], xsm_ref[...])
    u = _rmsnorm(x, g).astype(BF16)
    xb, z, xs = project(u)
    xb_out[...] = xb.astype(BF16)
    z_out[...] = z
    xs_out[...] = xs


def _in_proj_gate_body(n_p_tiles, xp_ref, xsm_ref, g_ref, w_hbm, ga_out, gb_out,
                       w_vmem, stage, sem):
    i = pl.program_id(0)

    @pl.when(i == 0)
    def _():
        _load_weight_bf16(w_hbm, w_vmem, stage, sem, col0=3 * D_CONV + D_SSM)

    x = jnp.where(i < n_p_tiles, xp_ref[...], xsm_ref[...])
    u = _rmsnorm(x, g_ref[...]).astype(BF16)
    ga = jnp.dot(u, w_vmem[:, 0:D_MODEL], preferred_element_type=F32)
    ga_out[...] = _sigmoid(ga).astype(BF16)
    gb = jnp.dot(u, w_vmem[:, D_MODEL:2 * D_MODEL], preferred_element_type=F32)
    gb_out[...] = _sigmoid(gb).astype(BF16)


def _two_stream_specs(n_p_tiles, n_s_tiles):
    xp_spec = pl.BlockSpec((TM, D_MODEL), lambda i: (jnp.minimum(i, n_p_tiles - 1), 0))
    xs_spec = pl.BlockSpec((TM, D_MODEL), lambda i: (jnp.clip(i - n_p_tiles, 0, n_s_tiles - 1), 0))
    return xp_spec, xs_spec


def _in_proj(xp, xsm, meta, norm1, w_in):
    t_p, t_s = xp.shape[0], xsm.shape[0]
    n_p, n_s = t_p // TM, t_s // TM
    t = t_p + t_s
    half = 3 * D_CONV + D_SSM
    xp_spec, xs_spec = _two_stream_specs(n_p, n_s)
    g_spec = pl.BlockSpec((1, D_MODEL), lambda i: (0, 0))
    any_spec = pl.BlockSpec(memory_space=pl.ANY)
    stage_rows = 128
    row = lambda w: pl.BlockSpec((TM, w), lambda i: (i, 0))
    const = lambda r, w: pl.BlockSpec((r, w), lambda i: (0, 0))
    params = pltpu.CompilerParams(dimension_semantics=("arbitrary",), vmem_limit_bytes=VMEM_LIMIT)
    scratch = [pltpu.VMEM((D_MODEL, half), BF16),
               pltpu.VMEM((2, stage_rows, half), F32),
               pltpu.SemaphoreType.DMA((2,))]

    xb, z, xs, zmeta, xsmeta = pl.pallas_call(
        functools.partial(_in_proj_mix_body, n_p),
        grid=(n_p + n_s,),
        in_specs=[xp_spec, xs_spec, const(N_META, D_MODEL), g_spec, any_spec],
        out_specs=[row(D_CONV), row(D_CONV), row(D_SSM), const(N_META, D_CONV), const(N_META, D_SSM)],
        out_shape=[jax.ShapeDtypeStruct((t, D_CONV), BF16),
                   jax.ShapeDtypeStruct((t, D_CONV), F32),
                   jax.ShapeDtypeStruct((t, D_SSM), F32),
                   jax.ShapeDtypeStruct((N_META, D_CONV), F32),
                   jax.ShapeDtypeStruct((N_META, D_SSM), F32)],
        scratch_shapes=scratch,
        compiler_params=params,
        name="in_proj_mix",
    )(xp, xsm, meta, norm1, w_in)

    sga, sgb = pl.pallas_call(
        functools.partial(_in_proj_gate_body, n_p),
        grid=(n_p + n_s,),
        in_specs=[xp_spec, xs_spec, g_spec, any_spec],
        out_specs=[row(D_MODEL), row(D_MODEL)],
        out_shape=[jax.ShapeDtypeStruct((t, D_MODEL), BF16),
                   jax.ShapeDtypeStruct((t, D_MODEL), BF16)],
        scratch_shapes=scratch,
        compiler_params=params,
        name="in_proj_gate",
    )(xp, xsm, norm1, w_in)
    return xb, z, xs, zmeta, xsmeta, sga, sgb


S5_GROUPS_PER_STEP = LANES // SSM_H
S5_PAIRS_PER_STEP = S5_GROUPS_PER_STEP // 2


def _s5_chunk_mats(lam_re, lam_im, log_dt, b_re, b_im, c_re, c_im):
    dt = jnp.exp(log_dt)[:, None]
    lr, li = lam_re, lam_im
    mag = jnp.exp(lr * dt)
    ab_re, ab_im = mag * jnp.cos(li * dt), mag * jnp.sin(li * dt)
    nr, ni = ab_re - 1.0, ab_im
    den = lr * lr + li * li
    k_re = (nr * lr + ni * li) / den
    k_im = (ni * lr - nr * li) / den
    bb_re = k_re[..., None] * b_re - k_im[..., None] * b_im
    bb_im = k_re[..., None] * b_im + k_im[..., None] * b_re
    d = jnp.arange(CHUNK + 1, dtype=F32)[:, None, None]
    pmag = jnp.exp(d * (lr * dt)[None])
    pw_re = pmag * jnp.cos(d * (li * dt)[None])
    pw_im = pmag * jnp.sin(d * (li * dt)[None])
    ab_b_re = pw_re[..., None] * bb_re[None] - pw_im[..., None] * bb_im[None]
    ab_b_im = pw_re[..., None] * bb_im[None] + pw_im[..., None] * bb_re[None]
    m = jnp.sum(c_re[None, :, :, :, None] * ab_b_re[:CHUNK, :, None, :, :]
                - c_im[None, :, :, :, None] * ab_b_im[:CHUNK, :, None, :, :], axis=3)
    rows = []
    zero = jnp.zeros((SSM_G, SSM_H, SSM_H), F32)
    for t in range(CHUNK):
        rows.append(jnp.concatenate([m[t - j] if t >= j else zero for j in range(CHUNK)], axis=2))
    toe_t = jnp.concatenate(rows, axis=1)
    p_re = jnp.transpose(ab_b_re[:CHUNK][::-1], (1, 2, 0, 3)).reshape(SSM_G, SSM_P, CHUNK_W)
    p_im = jnp.transpose(ab_b_im[:CHUNK][::-1], (1, 2, 0, 3)).reshape(SSM_G, SSM_P, CHUNK_W)
    p_t = jnp.concatenate([p_re, p_im], axis=1)
    pr, pi = pw_re[1:], pw_im[1:]
    q_re = c_re[None] * pr[:, :, None, :] - c_im[None] * pi[:, :, None, :]
    q_im = -(c_re[None] * pi[:, :, None, :] + c_im[None] * pr[:, :, None, :])
    q_re = jnp.transpose(q_re, (1, 0, 2, 3)).reshape(SSM_G, CHUNK_W, SSM_P)
    q_im = jnp.transpose(q_im, (1, 0, 2, 3)).reshape(SSM_G, CHUNK_W, SSM_P)
    even = (jnp.arange(SSM_G) % 2 == 0)[:, None, None]
    zq = jnp.zeros_like(q_re)
    pad2 = lambda q: jnp.where(even, jnp.concatenate([q, zq], axis=2), jnp.concatenate([zq, q], axis=2))
    q_t = jnp.stack([pad2(q_re), pad2(q_im)], axis=1)
    a16_re = pw_re[CHUNK].reshape(SSM_G // 2, 1, 2 * SSM_P)
    a16_im = pw_im[CHUNK].reshape(SSM_G // 2, 1, 2 * SSM_P)
    return toe_t.astype(BF16), p_t.astype(BF16), q_t.astype(BF16), a16_re, a16_im


def _s5_body(n_pc, n_pb, n_sb, xs_ref, xsmeta_ref, toe_ref, p_ref, q_ref, are_ref, aim_ref, s0re_ref, s0im_ref,
             y_out, pfre_out, pfim_out, sfre_out, sfim_out,
             xt_scr, u_scr, st_re, st_im, sl_re, sl_im, sp_re, sp_im, yt_scr):
    gb = S5_GROUPS_PER_STEP
    n_p_rows = n_pb * n_pc
    row_s = n_p_rows
    row_m = row_s + n_sb
    t_p = n_p_rows * CHUNK
    rows_pad = xt_scr.shape[0]

    xt_scr[row_m + 1:rows_pad, :] = jnp.zeros((rows_pad - row_m - 1, LANES), F32)
    for t in range(CHUNK):
        xt_scr[0:n_p_rows, :] = xs_ref[pl.ds(t, n_p_rows, stride=CHUNK), :]
        xt_scr[row_s:row_m, :] = xs_ref[pl.ds(t_p + t, n_sb, stride=CHUNK), :]
        xt_scr[row_m:row_m + 1, :] = xsmeta_ref[t:t + 1, :]
        xt = xt_scr[...].T.astype(BF16)
        for k in range(gb):
            u_scr[k, t * SSM_H:(t + 1) * SSM_H, :] = xt[k * SSM_H:(k + 1) * SSM_H, :]

    for k in range(gb):
        sl = jnp.dot(p_ref[k], u_scr[k], preferred_element_type=F32)
        half = (k % 2) * SSM_P
        st_re[k // 2, half:half + SSM_P, :] = sl[0:SSM_P, :]
        st_im[k // 2, half:half + SSM_P, :] = sl[SSM_P:2 * SSM_P, :]
    npair = S5_PAIRS_PER_STEP
    for j in range(npair):
        sl_re[j] = st_re[j].T
        sl_im[j] = st_im[j].T
        sp_re[j, row_m:rows_pad, :] = jnp.zeros((rows_pad - row_m, 2 * SSM_P), F32)
        sp_im[j, row_m:rows_pad, :] = jnp.zeros((rows_pad - row_m, 2 * SSM_P), F32)

    ar = [are_ref[j] for j in range(npair)]
    ai = [aim_ref[j] for j in range(npair)]
    sre = [jnp.broadcast_to(sl_re[j, row_m:row_m + 1, :], (n_pb, 2 * SSM_P)) for j in range(npair)]
    sim = [jnp.broadcast_to(sl_im[j, row_m:row_m + 1, :], (n_pb, 2 * SSM_P)) for j in range(npair)]
    for c in range(n_pc):
        rows = pl.ds(c, n_pb, stride=n_pc)
        for j in range(npair):
            sp_re[j, rows, :] = sre[j]
            sp_im[j, rows, :] = sim[j]
            nre = ar[j] * sre[j] - ai[j] * sim[j] + sl_re[j, rows, :]
            nim = ar[j] * sim[j] + ai[j] * sre[j] + sl_im[j, rows, :]
            sre[j], sim[j] = nre, nim
    for j in range(npair):
        pfre_out[j] = sre[j]
        pfim_out[j] = sim[j]
        s0r, s0i = s0re_ref[j], s0im_ref[j]
        sp_re[j, row_s:row_m, :] = s0r
        sp_im[j, row_s:row_m, :] = s0i
        sfre_out[j] = ar[j] * s0r - ai[j] * s0i + sl_re[j, row_s:row_m, :]
        sfim_out[j] = ar[j] * s0i + ai[j] * s0r + sl_im[j, row_s:row_m, :]

    nt = (((1,), (1,)), ((), ()))
    for k in range(gb):
        y = jnp.dot(toe_ref[k], u_scr[k], preferred_element_type=F32)
        y += lax.dot_general(q_ref[k, 0], sp_re[k // 2].astype(BF16), nt, preferred_element_type=F32)
        y += lax.dot_general(q_ref[k, 1], sp_im[k // 2].astype(BF16), nt, preferred_element_type=F32)
        for t in range(CHUNK):
            yt_scr[t, k * SSM_H:(k + 1) * SSM_H, :] = y[t * SSM_H:(t + 1) * SSM_H, :]
    for t in range(CHUNK):
        yt = yt_scr[t].T
        y_out[pl.ds(t, n_p_rows, stride=CHUNK), :] = yt[0:n_p_rows, :]
        y_out[pl.ds(t_p + t, n_sb, stride=CHUNK), :] = yt[row_s:row_m, :]


def _s5(xs, xsmeta, state_re, state_im, mats, n_pb, n_sb, seq):
    toe_t, p_t, q_t, a_re, a_im = mats
    t = xs.shape[0]
    n_pc = seq // CHUNK
    rows = n_pc * n_pb + n_sb + 1
    rows_pad = -(-rows // LANES) * LANES
    gb, npair = S5_GROUPS_PER_STEP, S5_PAIRS_PER_STEP
    pairs = lambda s: jnp.transpose(s.reshape(n_sb, SSM_G // 2, 2 * SSM_P), (1, 0, 2))
    blk3 = lambda n, r, c: pl.BlockSpec((n, r, c), lambda i: (i, 0, 0))
    y, pfre, pfim, sfre, sfim = pl.pallas_call(
        functools.partial(_s5_body, n_pc, n_pb, n_sb),
        grid=(SSM_G // gb,),
        in_specs=[pl.BlockSpec((t, LANES), lambda i: (0, i)),
                  pl.BlockSpec((N_META, LANES), lambda i: (0, i)),
                  blk3(gb, CHUNK_W, CHUNK_W), blk3(gb, 2 * SSM_P, CHUNK_W),
                  pl.BlockSpec((gb, 2, CHUNK_W, 2 * SSM_P), lambda i: (i, 0, 0, 0)),
                  blk3(npair, 1, 2 * SSM_P), blk3(npair, 1, 2 * SSM_P),
                  blk3(npair, n_sb, 2 * SSM_P), blk3(npair, n_sb, 2 * SSM_P)],
        out_specs=[pl.BlockSpec((t, LANES), lambda i: (0, i)),
                   blk3(npair, n_pb, 2 * SSM_P), blk3(npair, n_pb, 2 * SSM_P),
                   blk3(npair, n_sb, 2 * SSM_P), blk3(npair, n_sb, 2 * SSM_P)],
        out_shape=[jax.ShapeDtypeStruct((t, D_SSM), F32),
                   jax.ShapeDtypeStruct((SSM_G // 2, n_pb, 2 * SSM_P), F32),
                   jax.ShapeDtypeStruct((SSM_G // 2, n_pb, 2 * SSM_P), F32),
                   jax.ShapeDtypeStruct((SSM_G // 2, n_sb, 2 * SSM_P), F32),
                   jax.ShapeDtypeStruct((SSM_G // 2, n_sb, 2 * SSM_P), F32)],
        scratch_shapes=[pltpu.VMEM((rows_pad, LANES), F32),
                        pltpu.VMEM((gb, CHUNK_W, rows_pad), BF16),
                        pltpu.VMEM((npair, 2 * SSM_P, rows_pad), F32),
                        pltpu.VMEM((npair, 2 * SSM_P, rows_pad), F32),
                        pltpu.VMEM((npair, rows_pad, 2 * SSM_P), F32),
                        pltpu.VMEM((npair, rows_pad, 2 * SSM_P), F32),
                        pltpu.VMEM((npair, rows_pad, 2 * SSM_P), F32),
                        pltpu.VMEM((npair, rows_pad, 2 * SSM_P), F32),
                        pltpu.VMEM((CHUNK, LANES, rows_pad), F32)],
        compiler_params=pltpu.CompilerParams(dimension_semantics=("arbitrary",), vmem_limit_bytes=VMEM_LIMIT),
        name="s5_chunks",
    )(xs, xsmeta, toe_t, p_t, q_t, a_re, a_im, pairs(state_re), pairs(state_im))
    unpair = lambda a: jnp.transpose(a, (1, 0, 2)).reshape(a.shape[1], SSM_G, SSM_P)[None]
    return y, unpair(pfre), unpair(pfim), unpair(sfre), unpair(sfim)


def _mixers_body(n_p_tiles, tiles_per_seq, xb_ref, z_ref, y5_ref, xs_ref, sga_ref, sgb_ref,
                 zmeta_ref, inj1_ref, inj2_ref, cw_ref, dskip_ref,
                 wc_hbm, wg_hbm, wso_hbm, merged_out,
                 wc, wg, wso, stage_a, stage_b, sem, carry):
    i = pl.program_id(0)

    @pl.when(i == 0)
    def _():
        _load_weight_bf16(wc_hbm, wc, stage_a, sem)
        _load_weight_bf16(wg_hbm, wg, stage_b, sem)
        _load_weight_bf16(wso_hbm, wso, stage_a, sem)

    @pl.when(jnp.logical_and(i < n_p_tiles, i % tiles_per_seq == 0))
    def _():
        carry[0:2, :] = zmeta_ref[N_META - 2:N_META, :]

    z = z_ref[...]
    row = lax.broadcasted_iota(I32, (TM, 1), 0)
    is_s = i >= n_p_tiles
    r1 = pltpu.roll(z, 1, 0)
    r2 = pltpu.roll(z, 2, 0)
    c1 = carry[1:2, :]
    c2 = carry[0:1, :]
    pos = jnp.where(is_s, row & (CHUNK - 1), row)
    first1 = pos == 0
    first2 = pos < 2
    fill1 = jnp.where(is_s, inj1_ref[...], jnp.broadcast_to(c1, z.shape))
    fill2 = jnp.where(is_s, inj2_ref[...], jnp.where(row == 0, c2, c1))
    zp1 = jnp.where(first1, fill1, r1)
    zp2 = jnp.where(first2, fill2, r2)
    carry[0:2, :] = z[TM - 2:TM, :]

    cw = cw_ref[...]
    conv = cw[0:1, :] * zp2 + cw[1:2, :] * zp1 + cw[2:3, :] * z
    a_in = (xb_ref[...].astype(F32) * conv).astype(BF16)
    ya = jnp.dot(a_in, wc[...], preferred_element_type=F32)

    ys = y5_ref[...] + dskip_ref[...] * xs_ref[...]
    ys = _gelu_tanh(ys)
    glu = jnp.dot(ys.astype(BF16), wg[...], preferred_element_type=F32)
    ys = ys * _sigmoid(glu)
    yb = jnp.dot(ys.astype(BF16), wso[...], preferred_element_type=F32)

    merged = sga_ref[...].astype(F32) * ya + sgb_ref[...].astype(F32) * yb
    merged_out[...] = merged.astype(BF16)


def _mixers(xb, z, y5, xs, sga, sgb, zmeta, inj1, inj2, conv_w, d_skip, w_conv_out, w_glu, w_ssm_out,
            n_p_tiles, tiles_per_seq):
    t = xb.shape[0]
    n_s_tiles = inj1.shape[0] // TM
    row = lambda w: pl.BlockSpec((TM, w), lambda i: (i, 0))
    const = lambda r, w: pl.BlockSpec((r, w), lambda i: (0, 0))
    inj = pl.BlockSpec((TM, D_CONV), lambda i: (jnp.clip(i - n_p_tiles, 0, n_s_tiles - 1), 0))
    any_spec = pl.BlockSpec(memory_space=pl.ANY)
    return pl.pallas_call(
        functools.partial(_mixers_body, n_p_tiles, tiles_per_seq),
        grid=(t // TM,),
        in_specs=[row(D_CONV), row(D_CONV), row(D_SSM), row(D_SSM), row(D_MODEL), row(D_MODEL),
                  const(N_META, D_CONV), inj, inj, const(3, D_CONV), const(1, D_SSM),
                  any_spec, any_spec, any_spec],
        out_specs=row(D_MODEL),
        out_shape=jax.ShapeDtypeStruct((t, D_MODEL), BF16),
        scratch_shapes=[pltpu.VMEM((D_CONV, D_MODEL), BF16),
                        pltpu.VMEM((D_SSM, D_SSM), BF16),
                        pltpu.VMEM((D_SSM, D_MODEL), BF16),
                        pltpu.VMEM((2, 256, D_MODEL), F32),
                        pltpu.VMEM((2, 256, D_SSM), F32),
                        pltpu.SemaphoreType.DMA((2,)),
                        pltpu.VMEM((8, D_CONV), F32)],
        compiler_params=pltpu.CompilerParams(dimension_semantics=("arbitrary",), vmem_limit_bytes=VMEM_LIMIT),
        name="mixers",
    )(xb, z, y5, xs, sga, sgb, zmeta, inj1, inj2, conv_w, d_skip, w_conv_out, w_glu, w_ssm_out)


ROUTE_W = LANES
COARSE0 = N_EXPERTS


def _route(logits, cnt):
    col = lax.broadcasted_iota(I32, logits.shape, 1)
    colf = col.astype(F32)
    neg = jnp.float32(-jnp.inf)
    big = jnp.float32(1 << 20)
    is_c = jnp.logical_and(col >= COARSE0, col < COARSE0 + N_EGROUPS)
    lc = jnp.where(is_c, logits, neg)
    cmax = jnp.max(lc, axis=-1, keepdims=True)
    gi = jnp.min(jnp.where(lc == cmax, colf - COARSE0, big), axis=-1, keepdims=True)
    pg = 1.0 / jnp.sum(jnp.where(is_c, jnp.exp(lc - cmax), 0.0), axis=-1, keepdims=True)
    grp = (col >> 3).astype(F32)
    in_g = jnp.logical_and(col < N_EXPERTS, grp == gi)
    lf = jnp.where(in_g, logits, neg)
    m1 = jnp.max(lf, axis=-1, keepdims=True)
    i1 = jnp.min(jnp.where(lf == m1, colf, big), axis=-1, keepdims=True)
    lf2 = jnp.where(colf == i1, neg, lf)
    m2 = jnp.max(lf2, axis=-1, keepdims=True)
    i2 = jnp.min(jnp.where(lf2 == m2, colf, big), axis=-1, keepdims=True)
    e2 = jnp.exp(m2 - m1)
    w1 = pg / (1.0 + e2)
    w2 = pg * e2 / (1.0 + e2)
    n = logits.shape[0]
    hit1 = colf == i1
    hit2 = colf == i2
    onehot = jnp.where(jnp.logical_or(hit1, hit2), 1.0, 0.0)
    rr = lax.broadcasted_iota(I32, (n, n), 0)
    cc = lax.broadcasted_iota(I32, (n, n), 1)
    tri = jnp.where(cc < rr, 1.0, 0.0).astype(BF16)
    pos = jnp.dot(tri, onehot.astype(BF16), preferred_element_type=F32) + cnt
    rank1 = jnp.sum(jnp.where(hit1, pos, 0.0), axis=-1, keepdims=True)
    rank2 = jnp.sum(jnp.where(hit2, pos, 0.0), axis=-1, keepdims=True)
    vals = (i1, i2, w1, w2, rank1, rank2)
    rec = jnp.zeros(logits.shape, F32)
    for c, val in enumerate(vals):
        rec = jnp.where(col == c, val, rec)
    return rec, cnt + jnp.sum(onehot, axis=0, keepdims=True)


def _split_bf16(a):
    hi = a.astype(BF16)
    lo = (a - hi.astype(F32)).astype(BF16)
    return hi, lo


def _out_proj_body(n_p_tiles, merged_ref, xp_ref, xsm_ref, g2_ref, wr_ref, wo_hbm,
                   h2_out, v_out, rec_out, cnt_out, wo, stage, sem):
    i = pl.program_id(0)

    @pl.when(i == 0)
    def _():
        _load_weight_bf16(wo_hbm, wo, stage, sem)
        cnt_out[...] = jnp.zeros(cnt_out.shape, F32)

    x = jnp.where(i < n_p_tiles, xp_ref[...], xsm_ref[...])
    h2 = x + jnp.dot(merged_ref[...], wo[...], preferred_element_type=F32)
    h2_out[...] = h2
    v = _rmsnorm(h2, g2_ref[...])
    v_out[...] = v
    v_hi, v_lo = _split_bf16(v)
    w_hi, w_lo = _split_bf16(wr_ref[...])
    logits = (jnp.dot(v_hi, w_hi, preferred_element_type=F32)
              + jnp.dot(v_lo, w_hi, preferred_element_type=F32)
              + jnp.dot(v_hi, w_lo, preferred_element_type=F32))
    rec, cnt = _route(logits, cnt_out[...])
    rec_out[...] = rec
    cnt_out[...] = cnt


def _out_proj(merged, xp, xsm, norm2, w_router, w_o):
    t_p, t_s = xp.shape[0], xsm.shape[0]
    n_p, n_s = t_p // TM, t_s // TM
    t = t_p + t_s
    xp_spec, xs_spec = _two_stream_specs(n_p, n_s)
    row = lambda w: pl.BlockSpec((TM, w), lambda i: (i, 0))
    const = lambda r, w: pl.BlockSpec((r, w), lambda i: (0, 0))
    return pl.pallas_call(
        functools.partial(_out_proj_body, n_p),
        grid=(n_p + n_s,),
        in_specs=[row(D_MODEL), xp_spec, xs_spec, const(1, D_MODEL), const(D_MODEL, ROUTE_W),
                  pl.BlockSpec(memory_space=pl.ANY)],
        out_specs=[row(D_MODEL), row(D_MODEL), row(ROUTE_W), const(1, ROUTE_W)],
        out_shape=[jax.ShapeDtypeStruct((t, D_MODEL), F32),
                   jax.ShapeDtypeStruct((t, D_MODEL), F32),
                   jax.ShapeDtypeStruct((t, ROUTE_W), F32),
                   jax.ShapeDtypeStruct((1, ROUTE_W), F32)],
        scratch_shapes=[pltpu.VMEM((D_MODEL, D_MODEL), BF16),
                        pltpu.VMEM((2, 256, D_MODEL), F32),
                        pltpu.SemaphoreType.DMA((2,))],
        compiler_params=pltpu.CompilerParams(dimension_semantics=("arbitrary",), vmem_limit_bytes=VMEM_LIMIT),
        name="out_proj_route",
    )(merged, xp, xsm, norm2, w_router, w_o)


def _moe_plan(rec, cnt, n_tiles):
    t = rec.shape[0]
    n_pairs = 2 * t
    eid = rec[:, 0:2].astype(I32).reshape(-1)
    rank = rec[:, 4:6].astype(I32).reshape(-1)
    counts = cnt[0, :N_EXPERTS].astype(I32)
    pair_start = jnp.cumsum(counts) - counts
    experts = jnp.arange(N_EXPERTS, dtype=I32)
    onehot = (eid[:, None] == experts[None, :]).astype(I32)
    pos = rank + jnp.sum(onehot * pair_start[None, :], axis=1)
    _, order = lax.sort((pos, jnp.arange(n_pairs, dtype=I32)), num_keys=1)
    tiles_e = (counts + TM_MOE - 1) // TM_MOE
    tile_end = jnp.cumsum(tiles_e)
    tile_start = tile_end - tiles_e
    n_valid = tile_end[-1]
    tile_ids = jnp.arange(n_tiles, dtype=I32)
    tile_e = jnp.sum((tile_ids[:, None] >= tile_end[None, :]).astype(I32), axis=1)
    last_e = jnp.sum((n_valid - 1 >= tile_end).astype(I32))
    tile_e = jnp.minimum(jnp.where(tile_ids < n_valid, tile_e, last_e), N_EXPERTS - 1)
    tile_onehot = (tile_e[:, None] == experts[None, :]).astype(I32)
    tile_q0 = (tile_ids - jnp.sum(tile_onehot * tile_start[None, :], axis=1)) * TM_MOE
    tile_q0 = jnp.where(tile_ids < n_valid, tile_q0, 0)
    dst = rank + jnp.sum(onehot * (tile_start * TM_MOE)[None, :], axis=1)
    pad_start = tile_start * TM_MOE + counts
    pad_len = tiles_e * TM_MOE - counts
    return (dst.astype(I32), pad_start.astype(I32), pad_len.astype(I32), tile_e, tile_q0.astype(I32),
            n_valid.astype(I32).reshape(1), pair_start, counts, order)


DISPATCH_BUFS = 3
ROW_PIECES = tuple(TM_MOE >> (b + 1) for b in range(TM_MOE.bit_length() - 1))


def _dispatch_body(dst_ref, pad_start_ref, pad_len_ref, nvalid_ref, v_hbm, x_hbm, buf, zbuf, rsem, ssem, zsem):
    i = pl.program_id(0)
    n = pl.num_programs(0)

    def read(tile, slot):
        return pltpu.make_async_copy(v_hbm.at[pl.ds(pl.multiple_of(tile * TM, TM), TM)], buf.at[slot],
                                     rsem.at[slot])

    def row_write(slot, r, dst_row):
        return pltpu.make_async_copy(buf.at[slot, pl.ds(r, 1)], x_hbm.at[pl.ds(dst_row, 1)], ssem.at[slot])

    def drain(slot):
        def body(r, c):
            row_write(slot, 0, 0).wait()
            row_write(slot, 0, 0).wait()
            return c
        lax.fori_loop(0, TM, body, 0, unroll=8)

    def pad_fill(go):
        def zero_rows(start, size):
            d = pltpu.make_async_copy(zbuf.at[pl.ds(0, size)], x_hbm.at[pl.ds(start, size)], zsem)
            d.start() if go else d.wait()

        def body(e, c):
            start, length = pad_start_ref[e], pad_len_ref[e]
            head = (-start) & (SUBLANES - 1)
            for h in range(SUBLANES - 1):
                @pl.when(h < head)
                def _(h=h):
                    zero_rows(start + h, 1)
            start, length = start + head, length - head
            for size in ROW_PIECES:
                if size >= SUBLANES:
                    @pl.when((length & size) != 0)
                    def _(size=size):
                        zero_rows(pl.multiple_of(start + (length & (-2 * size)), SUBLANES), size)
            return c
        lax.fori_loop(0, N_EXPERTS, body, 0)

        def unused(tile, c):
            for half in range(TM_MOE // ROW_PIECES[0]):
                zero_rows(pl.multiple_of(tile * TM_MOE + half * ROW_PIECES[0], SUBLANES), ROW_PIECES[0])
            return c
        lax.fori_loop(nvalid_ref[0], x_hbm.shape[0] // TM_MOE, unused, 0)

    @pl.when(i == 0)
    def _():
        zbuf[...] = jnp.zeros(zbuf.shape, F32)
        pad_fill(True)
        read(0, 0).start()

    @pl.when(i >= 2)
    def _():
        drain((i + 1) % DISPATCH_BUFS)

    @pl.when(i + 1 < n)
    def _():
        read(i + 1, (i + 1) % DISPATCH_BUFS).start()

    slot = i % DISPATCH_BUFS
    read(i, slot).wait()
    for r in range(TM):
        for k in range(2):
            row_write(slot, r, dst_ref[2 * (i * TM + r) + k]).start(priority=k)

    @pl.when(i == n - 1)
    def _():
        if n >= 2:
            drain((i - 1) % DISPATCH_BUFS)
        drain(slot)
        pad_fill(False)


def _dispatch(v, plan, n_tiles):
    dst, pad_start, pad_len, n_valid = plan[0], plan[1], plan[2], plan[5]
    t = v.shape[0]
    return pl.pallas_call(
        _dispatch_body,
        grid_spec=pltpu.PrefetchScalarGridSpec(
            num_scalar_prefetch=4,
            grid=(t // TM,),
            in_specs=[pl.BlockSpec(memory_space=pl.ANY)],
            out_specs=pl.BlockSpec(memory_space=pl.ANY),
            scratch_shapes=[pltpu.VMEM((DISPATCH_BUFS, TM, D_MODEL), F32),
                            pltpu.VMEM((ROW_PIECES[0], D_MODEL), F32),
                            pltpu.SemaphoreType.DMA((DISPATCH_BUFS,)),
                            pltpu.SemaphoreType.DMA((DISPATCH_BUFS,)),
                            pltpu.SemaphoreType.DMA(())]),
        out_shape=jax.ShapeDtypeStruct((n_tiles * TM_MOE, D_MODEL), F32),
        compiler_params=pltpu.CompilerParams(dimension_semantics=("arbitrary",), vmem_limit_bytes=VMEM_LIMIT),
        name="moe_dispatch",
    )(dst, pad_start, pad_len, n_valid, v)


def _moe_body(n_tok, plane, tile_e_ref, tile_q0_ref, nvalid_ref, pstart_ref, cnt_ref, orow_ref,
              x_ref, wg_ref, wu_ref, wd_ref, o_hbm, ybuf0, ybuf1, ybuf2, ssem, wg, wu, wd):
    i = pl.program_id(0)
    nv = nvalid_ref[0]
    bufs = (ybuf0, ybuf1, ybuf2)

    def scratch_row0(slot):
        return (slot & 1) * plane + n_tok + (slot >> 1) * TM_MOE

    def row_write(slot, r, dst_row):
        return pltpu.make_async_copy(bufs[slot].at[pl.ds(r, 1)], o_hbm.at[pl.ds(dst_row, 1)], ssem.at[slot])

    def scratch_rows(region):
        return pltpu.make_async_copy(ybuf0, o_hbm.at[pl.ds(scratch_row0(region), TM_MOE)], ssem.at[0])

    def start_writes(tile, slot):
        e = tile_e_ref[tile]
        valid = cnt_ref[e] - tile_q0_ref[tile]
        first = pstart_ref[e] + tile_q0_ref[tile]
        last = pstart_ref[e] + cnt_ref[e] - 1
        for r in range(TM_MOE):
            row = orow_ref[jnp.minimum(first + r, last)]
            row_write(slot, r, jnp.where(r < valid, row, scratch_row0(slot) + r)).start(priority=r % 2)

    def compute(slot):
        x = x_ref[...].astype(BF16)
        hg = jnp.dot(x, wg[...], preferred_element_type=F32)
        hu = jnp.dot(x, wu[...], preferred_element_type=F32)
        act = hg * _sigmoid(hg) * hu
        bufs[slot][...] = jnp.dot(act.astype(BF16), wd[...], preferred_element_type=F32)

    @pl.when(i == 0)
    def _():
        ybuf0[...] = jnp.zeros(ybuf0.shape, F32)
        for region in range(4):
            scratch_rows(region).start()
        for region in range(4):
            scratch_rows(region).wait()

    @pl.when(jnp.logical_and(i >= 3, i < nv + 3))
    def _():
        def body(r, c):
            pltpu.make_async_copy(ybuf0.at[pl.ds(0, 1)], o_hbm.at[pl.ds(0, 1)], ssem.at[i % 3]).wait()
            return c
        lax.fori_loop(0, TM_MOE, body, 0, unroll=8)

    @pl.when(i < nv)
    def _():
        prev_e = tile_e_ref[jnp.maximum(i - 1, 0)]

        @pl.when(jnp.logical_or(i == 0, tile_e_ref[i] != prev_e))
        def _():
            wg[...] = wg_ref[0].astype(BF16)
            wu[...] = wu_ref[0].astype(BF16)
            wd[...] = wd_ref[0].astype(BF16)

    @pl.when(i == 0)
    def _():
        compute(0)

    for slot in range(3):
        prev = (slot + 2) % 3

        @pl.when(jnp.logical_and(i % 3 == slot, jnp.logical_and(i >= 1, i < nv)))
        def _(slot=slot, prev=prev):
            start_writes(i - 1, prev)
            compute(slot)

        @pl.when(jnp.logical_and(i % 3 == slot, i == nv))
        def _(prev=prev):
            start_writes(i - 1, prev)


def _moe(x_disp, plan, w_gate, w_up, w_down, n_tiles, n_tok):
    tile_e, tile_q0, n_valid, pair_start, counts, order = plan[3:9]
    plane = n_tok + 2 * TM_MOE
    orow = (order & 1) * plane + (order >> 1)
    tile = lambda i, nv: jnp.minimum(i, jnp.maximum(nv[0] - 1, 0))
    wspec = lambda a, b: pl.BlockSpec((1, a, b), lambda i, te, tq, nv, *_: (te[tile(i, nv)], 0, 0))
    ybuf = pltpu.VMEM((TM_MOE, D_MODEL), F32)
    return pl.pallas_call(
        functools.partial(_moe_body, n_tok, plane),
        grid_spec=pltpu.PrefetchScalarGridSpec(
            num_scalar_prefetch=6,
            grid=(n_tiles + 3,),
            in_specs=[pl.BlockSpec((TM_MOE, D_MODEL), lambda i, te, tq, nv, *_: (tile(i, nv), 0)),
                      wspec(D_MODEL, D_EXPERT), wspec(D_MODEL, D_EXPERT), wspec(D_EXPERT, D_MODEL)],
            out_specs=pl.BlockSpec(memory_space=pl.ANY),
            scratch_shapes=[ybuf, ybuf, ybuf,
                            pltpu.SemaphoreType.DMA((3,)),
                            pltpu.VMEM((D_MODEL, D_EXPERT), BF16),
                            pltpu.VMEM((D_MODEL, D_EXPERT), BF16),
                            pltpu.VMEM((D_EXPERT, D_MODEL), BF16)]),
        out_shape=jax.ShapeDtypeStruct((2 * plane, D_MODEL), F32),
        compiler_params=pltpu.CompilerParams(dimension_semantics=("arbitrary",), vmem_limit_bytes=VMEM_LIMIT),
        name="moe_experts",
    )(tile_e, tile_q0, n_valid, pair_start, counts, orow, x_disp, w_gate, w_up, w_down)


def _combine_body(n_p_tiles, h2_ref, rec_ref, gf_ref, y1_ref, y2_ref, outp_ref, outs_ref):
    i = pl.program_id(0)
    rec = rec_ref[...]
    h = rec[:, 2:3] * y1_ref[...] + rec[:, 3:4] * y2_ref[...]
    out = _rmsnorm(h2_ref[...] + h, gf_ref[...])

    @pl.when(i < n_p_tiles)
    def _():
        outp_ref[...] = out

    @pl.when(i >= n_p_tiles)
    def _():
        outs_ref[...] = out


def _combine(h2, rec, y_pairs, final_norm, t_p, t_s):
    n_p, n_s = t_p // TM, t_s // TM
    plane_tiles = y_pairs.shape[0] // 2 // TM
    return pl.pallas_call(
        functools.partial(_combine_body, n_p),
        grid=(n_p + n_s,),
        in_specs=[pl.BlockSpec((TM, D_MODEL), lambda i: (i, 0)),
                  pl.BlockSpec((TM, ROUTE_W), lambda i: (i, 0)),
                  pl.BlockSpec((1, D_MODEL), lambda i: (0, 0)),
                  pl.BlockSpec((TM, D_MODEL), lambda i: (i, 0)),
                  pl.BlockSpec((TM, D_MODEL), lambda i: (i + plane_tiles, 0))],
        out_specs=[pl.BlockSpec((TM, D_MODEL), lambda i: (jnp.minimum(i, n_p - 1), 0)),
                   pl.BlockSpec((TM, D_MODEL), lambda i: (jnp.clip(i - n_p, 0, n_s - 1), 0))],
        out_shape=[jax.ShapeDtypeStruct((t_p, D_MODEL), F32),
                   jax.ShapeDtypeStruct((t_s, D_MODEL), F32)],
        compiler_params=pltpu.CompilerParams(dimension_semantics=("arbitrary",), vmem_limit_bytes=VMEM_LIMIT),
        name="combine_norm",
    )(h2, rec, final_norm, y_pairs, y_pairs)


def kernel(x_prompt, x_sample, state_conv, state_ssm_re, state_ssm_im, meta_tokens, norm1, w_in, conv_w,
           lam_re, lam_im, log_dt, ssm_b_re, ssm_b_im, ssm_c_re, ssm_c_im, ssm_d, w_glu, w_conv_out,
           w_ssm_out, w_o, norm2, w_coarse, w_fine, w_gate, w_up, w_down, final_norm):
    n_pb, seq, _ = x_prompt.shape
    n_sb, dec_seq, _ = x_sample.shape
    assert dec_seq == CHUNK and seq % TM == 0 and (n_sb * dec_seq) % TM == 0 and N_META == CHUNK
    t_p, t_s = n_pb * seq, n_sb * dec_seq
    xp = x_prompt.reshape(t_p, D_MODEL)
    xsm = x_sample.reshape(t_s, D_MODEL)

    xb, z, xs, zmeta, xsmeta, sga, sgb = _in_proj(xp, xsm, meta_tokens, norm1, w_in[0])

    mats = _s5_chunk_mats(lam_re[0], lam_im[0], log_dt[0], ssm_b_re[0], ssm_b_im[0], ssm_c_re[0], ssm_c_im[0])
    y5, pf_re, pf_im, sf_re, sf_im = _s5(xs, xsmeta, state_ssm_re[0], state_ssm_im[0], mats, n_pb, n_sb, seq)

    buf = state_conv[0]
    zero = jnp.zeros((n_sb, dec_seq, D_CONV), F32)
    inj1 = zero.at[:, 0].set(buf[:, 1]).reshape(t_s, D_CONV)
    inj2 = zero.at[:, 0].set(buf[:, 0]).at[:, 1].set(buf[:, 1]).reshape(t_s, D_CONV)
    merged = _mixers(xb, z, y5, xs, sga, sgb, zmeta, inj1, inj2, conv_w[0], ssm_d, w_conv_out[0], w_glu[0],
                     w_ssm_out[0], t_p // TM, seq // TM)

    w_router = jnp.concatenate(
        [w_fine[0], w_coarse[0], jnp.zeros((D_MODEL, ROUTE_W - N_EXPERTS - N_EGROUPS), F32)], axis=1)
    h2, v, rec, cnt = _out_proj(merged, xp, xsm, norm2, w_router, w_o[0])

    n_tiles = 2 * (t_p + t_s) // TM_MOE + N_EXPERTS
    plan = _moe_plan(rec, cnt, n_tiles)
    x_disp = _dispatch(v, plan, n_tiles)
    y_pairs = _moe(x_disp, plan, w_gate[0], w_up[0], w_down[0], n_tiles, t_p + t_s)
    y_p, y_s = _combine(h2, rec, y_pairs, final_norm.reshape(1, D_MODEL), t_p, t_s)

    new_conv_p = jnp.stack([z[(b + 1) * seq - 2:(b + 1) * seq] for b in range(n_pb)])
    new_conv_s = z[t_p:].reshape(n_sb, dec_seq, D_CONV)[:, dec_seq - 2:]
    return (y_p.reshape(n_pb, seq, D_MODEL), y_s.reshape(n_sb, dec_seq, D_MODEL),
            new_conv_p[None], pf_re, pf_im, new_conv_s[None], sf_re, sf_im)
```

```python
import functools

import jax
import jax.numpy as jnp
from jax import lax
from jax.experimental import pallas as pl
from jax.experimental.pallas import tpu as pltpu

F32 = jnp.float32
BF16 = jnp.bfloat16
I32 = jnp.int32
U32 = jnp.uint32

D_MODEL = 2048
D_CONV = 1024
D_SSM = 1024
SSM_H = 16
SSM_G = 64
SSM_P = 64
N_META = 16
N_EGROUPS = 4
EXPERTS_PER_GROUP = 8
N_EXPERTS = 32
D_EXPERT = 256
EPS = 1e-6

LANES = 128
SUBLANES = 8

CHUNK = 16
CHUNK_W = CHUNK * SSM_H

TM = 256
TM_MOE = 256
VMEM_LIMIT = 52 * 1024 * 1024


def _rmsnorm(x, g):
    return x * lax.rsqrt(jnp.mean(x * x, axis=-1, keepdims=True) + EPS) * g


def _sigmoid(x):
    return 1.0 / (1.0 + jnp.exp(-x))


def _gelu_tanh(x):
    c = 0.7978845608028654
    return 0.5 * x * (1.0 + jnp.tanh(c * (x + 0.044715 * (x * x * x))))


HALF = D_MODEL // 2


def _pack_halves(a):
    return pltpu.pack_elementwise([a[:, :HALF], a[:, HALF:]], packed_dtype=BF16)


def _unpack_halves(p):
    return (pltpu.unpack_elementwise(p, index=0, packed_dtype=BF16, unpacked_dtype=F32),
            pltpu.unpack_elementwise(p, index=1, packed_dtype=BF16, unpacked_dtype=F32))


def _weight_copy(w_hbm, stage, sem, c, slot, rows, col0, ncols):
    return pltpu.make_async_copy(
        w_hbm.at[pl.ds(c * rows, rows), pl.ds(col0, ncols)], stage.at[slot], sem.at[slot])


def _load_weight_bf16(w_hbm, w_vmem, stage, sem, col0=0):
    k, n = w_vmem.shape
    rows = stage.shape[1]
    nchunk = k // rows
    _weight_copy(w_hbm, stage, sem, 0, 0, rows, col0, n).start()
    for c in range(nchunk):
        slot = c % 2
        if c + 1 < nchunk:
            _weight_copy(w_hbm, stage, sem, c + 1, 1 - slot, rows, col0, n).start()
        _weight_copy(w_hbm, stage, sem, c, slot, rows, col0, n).wait()
        w_vmem[pl.ds(c * rows, rows), :] = stage[slot].astype(BF16)


def _in_proj_mix_body(n_p_tiles, xp_ref, xsm_ref, meta_ref, g_ref, w_hbm,
                      xb_out, z_out, xs_out, zmeta_out, xsmeta_out,
                      w_vmem, stage, sem):
    i = pl.program_id(0)
    g = g_ref[...]

    def project(u):
        xb = jnp.dot(u, w_vmem[:, 0:D_CONV], preferred_element_type=F32)
        xc = jnp.dot(u, w_vmem[:, D_CONV:2 * D_CONV], preferred_element_type=F32)
        xv = jnp.dot(u, w_vmem[:, 2 * D_CONV:3 * D_CONV], preferred_element_type=F32)
        xs = jnp.dot(u, w_vmem[:, 3 * D_CONV:3 * D_CONV + D_SSM], preferred_element_type=F32)
        return xb, xc * xv, xs

    @pl.when(i == 0)
    def _():
        _load_weight_bf16(w_hbm, w_vmem, stage, sem, col0=0)
        um = _rmsnorm(meta_ref[...], g).astype(BF16)
        _, zm, xsm = project(um)
        zmeta_out[...] = zm
        xsmeta_out[...] = xsm

    x = jnp.where(i < n_p_tiles, xp_ref[...], xsm_ref[...])
    u = _rmsnorm(x, g).astype(BF16)
    xb, z, xs = project(u)
    xb_out[...] = xb.astype(BF16)
    z_out[...] = z
    xs_out[...] = xs


def _in_proj_gate_body(n_p_tiles, xp_ref, xsm_ref, g_ref, w_hbm, ga_out, gb_out,
                       w_vmem, stage, sem):
    i = pl.program_id(0)

    @pl.when(i == 0)
    def _():
        _load_weight_bf16(w_hbm, w_vmem, stage, sem, col0=3 * D_CONV + D_SSM)

    x = jnp.where(i < n_p_tiles, xp_ref[...], xsm_ref[...])
    u = _rmsnorm(x, g_ref[...]).astype(BF16)
    ga = jnp.dot(u, w_vmem[:, 0:D_MODEL], preferred_element_type=F32)
    ga_out[...] = _sigmoid(ga).astype(BF16)
    gb = jnp.dot(u, w_vmem[:, D_MODEL:2 * D_MODEL], preferred_element_type=F32)
    gb_out[...] = _sigmoid(gb).astype(BF16)


def _two_stream_specs(n_p_tiles, n_s_tiles):
    xp_spec = pl.BlockSpec((TM, D_MODEL), lambda i: (jnp.minimum(i, n_p_tiles - 1), 0))
    xs_spec = pl.BlockSpec((TM, D_MODEL), lambda i: (jnp.clip(i - n_p_tiles, 0, n_s_tiles - 1), 0))
    return xp_spec, xs_spec


def _in_proj(xp, xsm, meta, norm1, w_in):
    t_p, t_s = xp.shape[0], xsm.shape[0]
    n_p, n_s = t_p // TM, t_s // TM
    t = t_p + t_s
    half = 3 * D_CONV + D_SSM
    xp_spec, xs_spec = _two_stream_specs(n_p, n_s)
    g_spec = pl.BlockSpec((1, D_MODEL), lambda i: (0, 0))
    any_spec = pl.BlockSpec(memory_space=pl.ANY)
    stage_rows = 128
    row = lambda w: pl.BlockSpec((TM, w), lambda i: (i, 0))
    const = lambda r, w: pl.BlockSpec((r, w), lambda i: (0, 0))
    params = pltpu.CompilerParams(dimension_semantics=("arbitrary",), vmem_limit_bytes=VMEM_LIMIT)
    scratch = [pltpu.VMEM((D_MODEL, half), BF16),
               pltpu.VMEM((2, stage_rows, half), F32),
               pltpu.SemaphoreType.DMA((2,))]

    xb, z, xs, zmeta, xsmeta = pl.pallas_call(
        functools.partial(_in_proj_mix_body, n_p),
        grid=(n_p + n_s,),
        in_specs=[xp_spec, xs_spec, const(N_META, D_MODEL), g_spec, any_spec],
        out_specs=[row(D_CONV), row(D_CONV), row(D_SSM), const(N_META, D_CONV), const(N_META, D_SSM)],
        out_shape=[jax.ShapeDtypeStruct((t, D_CONV), BF16),
                   jax.ShapeDtypeStruct((t, D_CONV), F32),
                   jax.ShapeDtypeStruct((t, D_SSM), F32),
                   jax.ShapeDtypeStruct((N_META, D_CONV), F32),
                   jax.ShapeDtypeStruct((N_META, D_SSM), F32)],
        scratch_shapes=scratch,
        compiler_params=params,
        name="in_proj_mix",
    )(xp, xsm, meta, norm1, w_in)

    sga, sgb = pl.pallas_call(
        functools.partial(_in_proj_gate_body, n_p),
        grid=(n_p + n_s,),
        in_specs=[xp_spec, xs_spec, g_spec, any_spec],
        out_specs=[row(D_MODEL), row(D_MODEL)],
        out_shape=[jax.ShapeDtypeStruct((t, D_MODEL), BF16),
                   jax.ShapeDtypeStruct((t, D_MODEL), BF16)],
        scratch_shapes=scratch,
        compiler_params=params,
        name="in_proj_gate",
    )(xp, xsm, norm1, w_in)
    return xb, z, xs, zmeta, xsmeta, sga, sgb


S5_GROUPS_PER_STEP = LANES // SSM_H
S5_PAIRS_PER_STEP = S5_GROUPS_PER_STEP // 2


def _s5_chunk_mats(lam_re, lam_im, log_dt, b_re, b_im, c_re, c_im):
    dt = jnp.exp(log_dt)[:, None]
    lr, li = lam_re, lam_im
    mag = jnp.exp(lr * dt)
    ab_re, ab_im = mag * jnp.cos(li * dt), mag * jnp.sin(li * dt)
    nr, ni = ab_re - 1.0, ab_im
    den = lr * lr + li * li
    k_re = (nr * lr + ni * li) / den
    k_im = (ni * lr - nr * li) / den
    bb_re = k_re[..., None] * b_re - k_im[..., None] * b_im
    bb_im = k_re[..., None] * b_im + k_im[..., None] * b_re
    d = jnp.arange(CHUNK + 1, dtype=F32)[:, None, None]
    pmag = jnp.exp(d * (lr * dt)[None])
    pw_re = pmag * jnp.cos(d * (li * dt)[None])
    pw_im = pmag * jnp.sin(d * (li * dt)[None])
    ab_b_re = pw_re[..., None] * bb_re[None] - pw_im[..., None] * bb_im[None]
    ab_b_im = pw_re[..., None] * bb_im[None] + pw_im[..., None] * bb_re[None]
    m = jnp.sum(c_re[None, :, :, :, None] * ab_b_re[:CHUNK, :, None, :, :]
                - c_im[None, :, :, :, None] * ab_b_im[:CHUNK, :, None, :, :], axis=3)
    rows = []
    zero = jnp.zeros((SSM_G, SSM_H, SSM_H), F32)
    for t in range(CHUNK):
        rows.append(jnp.concatenate([m[t - j] if t >= j else zero for j in range(CHUNK)], axis=2))
    toe_t = jnp.concatenate(rows, axis=1)
    p_re = jnp.transpose(ab_b_re[:CHUNK][::-1], (1, 2, 0, 3)).reshape(SSM_G, SSM_P, CHUNK_W)
    p_im = jnp.transpose(ab_b_im[:CHUNK][::-1], (1, 2, 0, 3)).reshape(SSM_G, SSM_P, CHUNK_W)
    p_t = jnp.concatenate([p_re, p_im], axis=1)
    pr, pi = pw_re[1:], pw_im[1:]
    q_re = c_re[None] * pr[:, :, None, :] - c_im[None] * pi[:, :, None, :]
    q_im = -(c_re[None] * pi[:, :, None, :] + c_im[None] * pr[:, :, None, :])
    q_re = jnp.transpose(q_re, (1, 0, 2, 3)).reshape(SSM_G, CHUNK_W, SSM_P)
    q_im = jnp.transpose(q_im, (1, 0, 2, 3)).reshape(SSM_G, CHUNK_W, SSM_P)
    even = (jnp.arange(SSM_G) % 2 == 0)[:, None, None]
    zq = jnp.zeros_like(q_re)
    pad2 = lambda q: jnp.where(even, jnp.concatenate([q, zq], axis=2), jnp.concatenate([zq, q], axis=2))
    q_t = jnp.stack([pad2(q_re), pad2(q_im)], axis=1)
    a16_re = pw_re[CHUNK].reshape(SSM_G // 2, 1, 2 * SSM_P)
    a16_im = pw_im[CHUNK].reshape(SSM_G // 2, 1, 2 * SSM_P)
    return toe_t.astype(BF16), p_t.astype(BF16), q_t.astype(BF16), a16_re, a16_im


def _s5_body(n_pc, n_pb, n_sb, xs_ref, xsmeta_ref, toe_ref, p_ref, q_ref, are_ref, aim_ref, s0re_ref, s0im_ref,
             y_out, pfre_out, pfim_out, sfre_out, sfim_out,
             xt_scr, u_scr, st_re, st_im, sl_re, sl_im, sp_re, sp_im, yt_scr):
    gb = S5_GROUPS_PER_STEP
    n_p_rows = n_pb * n_pc
    row_s = n_p_rows
    row_m = row_s + n_sb
    t_p = n_p_rows * CHUNK
    rows_pad = xt_scr.shape[0]

    xt_scr[row_m + 1:rows_pad, :] = jnp.zeros((rows_pad - row_m - 1, LANES), F32)
    for t in range(CHUNK):
        xt_scr[0:n_p_rows, :] = xs_ref[pl.ds(t, n_p_rows, stride=CHUNK), :]
        xt_scr[row_s:row_m, :] = xs_ref[pl.ds(t_p + t, n_sb, stride=CHUNK), :]
        xt_scr[row_m:row_m + 1, :] = xsmeta_ref[t:t + 1, :]
        xt = xt_scr[...].T.astype(BF16)
        for k in range(gb):
            u_scr[k, t * SSM_H:(t + 1) * SSM_H, :] = xt[k * SSM_H:(k + 1) * SSM_H, :]

    for k in range(gb):
        sl = jnp.dot(p_ref[k], u_scr[k], preferred_element_type=F32)
        half = (k % 2) * SSM_P
        st_re[k // 2, half:half + SSM_P, :] = sl[0:SSM_P, :]
        st_im[k // 2, half:half + SSM_P, :] = sl[SSM_P:2 * SSM_P, :]
    npair = S5_PAIRS_PER_STEP
    for j in range(npair):
        sl_re[j] = st_re[j].T
        sl_im[j] = st_im[j].T
        sp_re[j, row_m:rows_pad, :] = jnp.zeros((rows_pad - row_m, 2 * SSM_P), F32)
        sp_im[j, row_m:rows_pad, :] = jnp.zeros((rows_pad - row_m, 2 * SSM_P), F32)

    ar = [are_ref[j] for j in range(npair)]
    ai = [aim_ref[j] for j in range(npair)]
    sre = [jnp.broadcast_to(sl_re[j, row_m:row_m + 1, :], (n_pb, 2 * SSM_P)) for j in range(npair)]
    sim = [jnp.broadcast_to(sl_im[j, row_m:row_m + 1, :], (n_pb, 2 * SSM_P)) for j in range(npair)]
    for c in range(n_pc):
        rows = pl.ds(c, n_pb, stride=n_pc)
        for j in range(npair):
            sp_re[j, rows, :] = sre[j]
            sp_im[j, rows, :] = sim[j]
            nre = ar[j] * sre[j] - ai[j] * sim[j] + sl_re[j, rows, :]
            nim = ar[j] * sim[j] + ai[j] * sre[j] + sl_im[j, rows, :]
            sre[j], sim[j] = nre, nim
    for j in range(npair):
        pfre_out[j] = sre[j]
        pfim_out[j] = sim[j]
        s0r, s0i = s0re_ref[j], s0im_ref[j]
        sp_re[j, row_s:row_m, :] = s0r
        sp_im[j, row_s:row_m, :] = s0i
        sfre_out[j] = ar[j] * s0r - ai[j] * s0i + sl_re[j, row_s:row_m, :]
        sfim_out[j] = ar[j] * s0i + ai[j] * s0r + sl_im[j, row_s:row_m, :]

    nt = (((1,), (1,)), ((), ()))
    for k in range(gb):
        y = jnp.dot(toe_ref[k], u_scr[k], preferred_element_type=F32)
        y += lax.dot_general(q_ref[k, 0], sp_re[k // 2].astype(BF16), nt, preferred_element_type=F32)
        y += lax.dot_general(q_ref[k, 1], sp_im[k // 2].astype(BF16), nt, preferred_element_type=F32)
        for t in range(CHUNK):
            yt_scr[t, k * SSM_H:(k + 1) * SSM_H, :] = y[t * SSM_H:(t + 1) * SSM_H, :]
    for t in range(CHUNK):
        yt = yt_scr[t].T
        y_out[pl.ds(t, n_p_rows, stride=CHUNK), :] = yt[0:n_p_rows, :]
        y_out[pl.ds(t_p + t, n_sb, stride=CHUNK), :] = yt[row_s:row_m, :]


def _s5(xs, xsmeta, state_re, state_im, mats, n_pb, n_sb, seq):
    toe_t, p_t, q_t, a_re, a_im = mats
    t = xs.shape[0]
    n_pc = seq // CHUNK
    rows = n_pc * n_pb + n_sb + 1
    rows_pad = -(-rows // LANES) * LANES
    gb, npair = S5_GROUPS_PER_STEP, S5_PAIRS_PER_STEP
    pairs = lambda s: jnp.transpose(s.reshape(n_sb, SSM_G // 2, 2 * SSM_P), (1, 0, 2))
    blk3 = lambda n, r, c: pl.BlockSpec((n, r, c), lambda i: (i, 0, 0))
    y, pfre, pfim, sfre, sfim = pl.pallas_call(
        functools.partial(_s5_body, n_pc, n_pb, n_sb),
        grid=(SSM_G // gb,),
        in_specs=[pl.BlockSpec((t, LANES), lambda i: (0, i)),
                  pl.BlockSpec((N_META, LANES), lambda i: (0, i)),
                  blk3(gb, CHUNK_W, CHUNK_W), blk3(gb, 2 * SSM_P, CHUNK_W),
                  pl.BlockSpec((gb, 2, CHUNK_W, 2 * SSM_P), lambda i: (i, 0, 0, 0)),
                  blk3(npair, 1, 2 * SSM_P), blk3(npair, 1, 2 * SSM_P),
                  blk3(npair, n_sb, 2 * SSM_P), blk3(npair, n_sb, 2 * SSM_P)],
        out_specs=[pl.BlockSpec((t, LANES), lambda i: (0, i)),
                   blk3(npair, n_pb, 2 * SSM_P), blk3(npair, n_pb, 2 * SSM_P),
                   blk3(npair, n_sb, 2 * SSM_P), blk3(npair, n_sb, 2 * SSM_P)],
        out_shape=[jax.ShapeDtypeStruct((t, D_SSM), F32),
                   jax.ShapeDtypeStruct((SSM_G // 2, n_pb, 2 * SSM_P), F32),
                   jax.ShapeDtypeStruct((SSM_G // 2, n_pb, 2 * SSM_P), F32),
                   jax.ShapeDtypeStruct((SSM_G // 2, n_sb, 2 * SSM_P), F32),
                   jax.ShapeDtypeStruct((SSM_G // 2, n_sb, 2 * SSM_P), F32)],
        scratch_shapes=[pltpu.VMEM((rows_pad, LANES), F32),
                        pltpu.VMEM((gb, CHUNK_W, rows_pad), BF16),
                        pltpu.VMEM((npair, 2 * SSM_P, rows_pad), F32),
                        pltpu.VMEM((npair, 2 * SSM_P, rows_pad), F32),
                        pltpu.VMEM((npair, rows_pad, 2 * SSM_P), F32),
                        pltpu.VMEM((npair, rows_pad, 2 * SSM_P), F32),
                        pltpu.VMEM((npair, rows_pad, 2 * SSM_P), F32),
                        pltpu.VMEM((npair, rows_pad, 2 * SSM_P), F32),
                        pltpu.VMEM((CHUNK, LANES, rows_pad), F32)],
        compiler_params=pltpu.CompilerParams(dimension_semantics=("arbitrary",), vmem_limit_bytes=VMEM_LIMIT),
        name="s5_chunks",
    )(xs, xsmeta, toe_t, p_t, q_t, a_re, a_im, pairs(state_re), pairs(state_im))
    unpair = lambda a: jnp.transpose(a, (1, 0, 2)).reshape(a.shape[1], SSM_G, SSM_P)[None]
    return y, unpair(pfre), unpair(pfim), unpair(sfre), unpair(sfim)


def _mixers_body(n_p_tiles, tiles_per_seq, xb_ref, z_ref, y5_ref, xs_ref, sga_ref, sgb_ref,
                 zmeta_ref, inj1_ref, inj2_ref, cw_ref, dskip_ref,
                 wc_hbm, wg_hbm, wso_hbm, merged_out,
                 wc, wg, wso, stage_a, stage_b, sem, carry):
    i = pl.program_id(0)

    @pl.when(i == 0)
    def _():
        _load_weight_bf16(wc_hbm, wc, stage_a, sem)
        _load_weight_bf16(wg_hbm, wg, stage_b, sem)
        _load_weight_bf16(wso_hbm, wso, stage_a, sem)

    @pl.when(jnp.logical_and(i < n_p_tiles, i % tiles_per_seq == 0))
    def _():
        carry[0:2, :] = zmeta_ref[N_META - 2:N_META, :]

    z = z_ref[...]
    row = lax.broadcasted_iota(I32, (TM, 1), 0)
    is_s = i >= n_p_tiles
    r1 = pltpu.roll(z, 1, 0)
    r2 = pltpu.roll(z, 2, 0)
    c1 = carry[1:2, :]
    c2 = carry[0:1, :]
    pos = jnp.where(is_s, row & (CHUNK - 1), row)
    first1 = pos == 0
    first2 = pos < 2
    fill1 = jnp.where(is_s, inj1_ref[...], jnp.broadcast_to(c1, z.shape))
    fill2 = jnp.where(is_s, inj2_ref[...], jnp.where(row == 0, c2, c1))
    zp1 = jnp.where(first1, fill1, r1)
    zp2 = jnp.where(first2, fill2, r2)
    carry[0:2, :] = z[TM - 2:TM, :]

    cw = cw_ref[...]
    conv = cw[0:1, :] * zp2 + cw[1:2, :] * zp1 + cw[2:3, :] * z
    a_in = (xb_ref[...].astype(F32) * conv).astype(BF16)
    ya = jnp.dot(a_in, wc[...], preferred_element_type=F32)

    ys = y5_ref[...] + dskip_ref[...] * xs_ref[...]
    ys = _gelu_tanh(ys)
    glu = jnp.dot(ys.astype(BF16), wg[...], preferred_element_type=F32)
    ys = ys * _sigmoid(glu)
    yb = jnp.dot(ys.astype(BF16), wso[...], preferred_element_type=F32)

    merged = sga_ref[...].astype(F32) * ya + sgb_ref[...].astype(F32) * yb
    merged_out[...] = merged.astype(BF16)


def _mixers(xb, z, y5, xs, sga, sgb, zmeta, inj1, inj2, conv_w, d_skip, w_conv_out, w_glu, w_ssm_out,
            n_p_tiles, tiles_per_seq):
    t = xb.shape[0]
    n_s_tiles = inj1.shape[0] // TM
    row = lambda w: pl.BlockSpec((TM, w), lambda i: (i, 0))
    const = lambda r, w: pl.BlockSpec((r, w), lambda i: (0, 0))
    inj = pl.BlockSpec((TM, D_CONV), lambda i: (jnp.clip(i - n_p_tiles, 0, n_s_tiles - 1), 0))
    any_spec = pl.BlockSpec(memory_space=pl.ANY)
    return pl.pallas_call(
        functools.partial(_mixers_body, n_p_tiles, tiles_per_seq),
        grid=(t // TM,),
        in_specs=[row(D_CONV), row(D_CONV), row(D_SSM), row(D_SSM), row(D_MODEL), row(D_MODEL),
                  const(N_META, D_CONV), inj, inj, const(3, D_CONV), const(1, D_SSM),
                  any_spec, any_spec, any_spec],
        out_specs=row(D_MODEL),
        out_shape=jax.ShapeDtypeStruct((t, D_MODEL), BF16),
        scratch_shapes=[pltpu.VMEM((D_CONV, D_MODEL), BF16),
                        pltpu.VMEM((D_SSM, D_SSM), BF16),
                        pltpu.VMEM((D_SSM, D_MODEL), BF16),
                        pltpu.VMEM((2, 256, D_MODEL), F32),
                        pltpu.VMEM((2, 256, D_SSM), F32),
                        pltpu.SemaphoreType.DMA((2,)),
                        pltpu.VMEM((8, D_CONV), F32)],
        compiler_params=pltpu.CompilerParams(dimension_semantics=("arbitrary",), vmem_limit_bytes=VMEM_LIMIT),
        name="mixers",
    )(xb, z, y5, xs, sga, sgb, zmeta, inj1, inj2, conv_w, d_skip, w_conv_out, w_glu, w_ssm_out)


ROUTE_W = LANES
COARSE0 = N_EXPERTS


def _route(logits, cnt):
    col = lax.broadcasted_iota(I32, logits.shape, 1)
    colf = col.astype(F32)
    neg = jnp.float32(-jnp.inf)
    big = jnp.float32(1 << 20)
    is_c = jnp.logical_and(col >= COARSE0, col < COARSE0 + N_EGROUPS)
    lc = jnp.where(is_c, logits, neg)
    cmax = jnp.max(lc, axis=-1, keepdims=True)
    gi = jnp.min(jnp.where(lc == cmax, colf - COARSE0, big), axis=-1, keepdims=True)
    pg = 1.0 / jnp.sum(jnp.where(is_c, jnp.exp(lc - cmax), 0.0), axis=-1, keepdims=True)
    grp = (col >> 3).astype(F32)
    in_g = jnp.logical_and(col < N_EXPERTS, grp == gi)
    lf = jnp.where(in_g, logits, neg)
    m1 = jnp.max(lf, axis=-1, keepdims=True)
    i1 = jnp.min(jnp.where(lf == m1, colf, big), axis=-1, keepdims=True)
    lf2 = jnp.where(colf == i1, neg, lf)
    m2 = jnp.max(lf2, axis=-1, keepdims=True)
    i2 = jnp.min(jnp.where(lf2 == m2, colf, big), axis=-1, keepdims=True)
    e2 = jnp.exp(m2 - m1)
    w1 = pg / (1.0 + e2)
    w2 = pg * e2 / (1.0 + e2)
    n = logits.shape[0]
    hit1 = colf == i1
    hit2 = colf == i2
    onehot = jnp.where(jnp.logical_or(hit1, hit2), 1.0, 0.0)
    rr = lax.broadcasted_iota(I32, (n, n), 0)
    cc = lax.broadcasted_iota(I32, (n, n), 1)
    tri = jnp.where(cc < rr, 1.0, 0.0).astype(BF16)
    pos = jnp.dot(tri, onehot.astype(BF16), preferred_element_type=F32) + cnt
    rank1 = jnp.sum(jnp.where(hit1, pos, 0.0), axis=-1, keepdims=True)
    rank2 = jnp.sum(jnp.where(hit2, pos, 0.0), axis=-1, keepdims=True)
    vals = (i1, i2, w1, w2, rank1, rank2)
    rec = jnp.zeros(logits.shape, F32)
    for c, val in enumerate(vals):
        rec = jnp.where(col == c, val, rec)
    return rec, cnt + jnp.sum(onehot, axis=0, keepdims=True)


def _split_bf16(a):
    hi = a.astype(BF16)
    lo = (a - hi.astype(F32)).astype(BF16)
    return hi, lo


def _out_proj_body(n_p_tiles, merged_ref, xp_ref, xsm_ref, g2_ref, wr_ref, wo_hbm,
                   h2_out, v_out, rec_out, cnt_out, wo, stage, sem):
    i = pl.program_id(0)

    @pl.when(i == 0)
    def _():
        _load_weight_bf16(wo_hbm, wo, stage, sem)
        cnt_out[...] = jnp.zeros(cnt_out.shape, F32)

    x = jnp.where(i < n_p_tiles, xp_ref[...], xsm_ref[...])
    h2 = x + jnp.dot(merged_ref[...], wo[...], preferred_element_type=F32)
    h2_out[...] = h2
    v = _rmsnorm(h2, g2_ref[...])
    v_out[...] = _pack_halves(v)
    v_hi, v_lo = _split_bf16(v)
    w_hi, w_lo = _split_bf16(wr_ref[...])
    logits = (jnp.dot(v_hi, w_hi, preferred_element_type=F32)
              + jnp.dot(v_lo, w_hi, preferred_element_type=F32)
              + jnp.dot(v_hi, w_lo, preferred_element_type=F32))
    rec, cnt = _route(logits, cnt_out[...])
    rec_out[...] = rec
    cnt_out[...] = cnt


def _out_proj(merged, xp, xsm, norm2, w_router, w_o):
    t_p, t_s = xp.shape[0], xsm.shape[0]
    n_p, n_s = t_p // TM, t_s // TM
    t = t_p + t_s
    xp_spec, xs_spec = _two_stream_specs(n_p, n_s)
    row = lambda w: pl.BlockSpec((TM, w), lambda i: (i, 0))
    const = lambda r, w: pl.BlockSpec((r, w), lambda i: (0, 0))
    return pl.pallas_call(
        functools.partial(_out_proj_body, n_p),
        grid=(n_p + n_s,),
        in_specs=[row(D_MODEL), xp_spec, xs_spec, const(1, D_MODEL), const(D_MODEL, ROUTE_W),
                  pl.BlockSpec(memory_space=pl.ANY)],
        out_specs=[row(D_MODEL), row(HALF), row(ROUTE_W), const(1, ROUTE_W)],
        out_shape=[jax.ShapeDtypeStruct((t, D_MODEL), F32),
                   jax.ShapeDtypeStruct((t, HALF), U32),
                   jax.ShapeDtypeStruct((t, ROUTE_W), F32),
                   jax.ShapeDtypeStruct((1, ROUTE_W), F32)],
        scratch_shapes=[pltpu.VMEM((D_MODEL, D_MODEL), BF16),
                        pltpu.VMEM((2, 256, D_MODEL), F32),
                        pltpu.SemaphoreType.DMA((2,))],
        compiler_params=pltpu.CompilerParams(dimension_semantics=("arbitrary",), vmem_limit_bytes=VMEM_LIMIT),
        name="out_proj_route",
    )(merged, xp, xsm, norm2, w_router, w_o)


def _moe_plan(rec, cnt, n_tiles):
    t = rec.shape[0]
    n_pairs = 2 * t
    eid = rec[:, 0:2].astype(I32).reshape(-1)
    rank = rec[:, 4:6].astype(I32).reshape(-1)
    counts = cnt[0, :N_EXPERTS].astype(I32)
    pair_start = jnp.cumsum(counts) - counts
    experts = jnp.arange(N_EXPERTS, dtype=I32)
    onehot = (eid[:, None] == experts[None, :]).astype(I32)
    pos = rank + jnp.sum(onehot * pair_start[None, :], axis=1)
    _, order = lax.sort((pos, jnp.arange(n_pairs, dtype=I32)), num_keys=1)
    tiles_e = (counts + TM_MOE - 1) // TM_MOE
    tile_end = jnp.cumsum(tiles_e)
    tile_start = tile_end - tiles_e
    n_valid = tile_end[-1]
    tile_ids = jnp.arange(n_tiles, dtype=I32)
    tile_e = jnp.sum((tile_ids[:, None] >= tile_end[None, :]).astype(I32), axis=1)
    last_e = jnp.sum((n_valid - 1 >= tile_end).astype(I32))
    tile_e = jnp.minimum(jnp.where(tile_ids < n_valid, tile_e, last_e), N_EXPERTS - 1)
    tile_onehot = (tile_e[:, None] == experts[None, :]).astype(I32)
    tile_q0 = (tile_ids - jnp.sum(tile_onehot * tile_start[None, :], axis=1)) * TM_MOE
    tile_q0 = jnp.where(tile_ids < n_valid, tile_q0, 0)
    dst = rank + jnp.sum(onehot * (tile_start * TM_MOE)[None, :], axis=1)
    pad_start = tile_start * TM_MOE + counts
    pad_len = tiles_e * TM_MOE - counts
    return (dst.astype(I32), pad_start.astype(I32), pad_len.astype(I32), tile_e, tile_q0.astype(I32),
            n_valid.astype(I32).reshape(1), pair_start, counts, order)


DISPATCH_BUFS = 3
ROW_PIECES = tuple(TM_MOE >> (b + 1) for b in range(TM_MOE.bit_length() - 1))


def _dispatch_body(dst_ref, pad_start_ref, pad_len_ref, nvalid_ref, v_hbm, x_hbm, buf, zbuf, rsem, ssem, zsem):
    i = pl.program_id(0)
    n = pl.num_programs(0)

    def read(tile, slot):
        return pltpu.make_async_copy(v_hbm.at[pl.ds(pl.multiple_of(tile * TM, TM), TM)], buf.at[slot],
                                     rsem.at[slot])

    def row_write(slot, r, dst_row):
        return pltpu.make_async_copy(buf.at[slot, pl.ds(r, 1)], x_hbm.at[pl.ds(dst_row, 1)], ssem.at[slot])

    def drain(slot):
        def body(r, c):
            row_write(slot, 0, 0).wait()
            row_write(slot, 0, 0).wait()
            return c
        lax.fori_loop(0, TM, body, 0, unroll=8)

    def pad_fill(go):
        def zero_rows(start, size):
            d = pltpu.make_async_copy(zbuf.at[pl.ds(0, size)], x_hbm.at[pl.ds(start, size)], zsem)
            d.start() if go else d.wait()

        def body(e, c):
            start, length = pad_start_ref[e], pad_len_ref[e]
            head = (-start) & (SUBLANES - 1)
            for h in range(SUBLANES - 1):
                @pl.when(h < head)
                def _(h=h):
                    zero_rows(start + h, 1)
            start, length = start + head, length - head
            for size in ROW_PIECES:
                if size >= SUBLANES:
                    @pl.when((length & size) != 0)
                    def _(size=size):
                        zero_rows(pl.multiple_of(start + (length & (-2 * size)), SUBLANES), size)
            return c
        lax.fori_loop(0, N_EXPERTS, body, 0)

        def unused(tile, c):
            for half in range(TM_MOE // ROW_PIECES[0]):
                zero_rows(pl.multiple_of(tile * TM_MOE + half * ROW_PIECES[0], SUBLANES), ROW_PIECES[0])
            return c
        lax.fori_loop(nvalid_ref[0], x_hbm.shape[0] // TM_MOE, unused, 0)

    @pl.when(i == 0)
    def _():
        zbuf[...] = jnp.zeros(zbuf.shape, U32)
        pad_fill(True)
        read(0, 0).start()

    @pl.when(i >= 2)
    def _():
        drain((i + 1) % DISPATCH_BUFS)

    @pl.when(i + 1 < n)
    def _():
        read(i + 1, (i + 1) % DISPATCH_BUFS).start()

    slot = i % DISPATCH_BUFS
    read(i, slot).wait()
    for r in range(TM):
        for k in range(2):
            row_write(slot, r, dst_ref[2 * (i * TM + r) + k]).start(priority=k)

    @pl.when(i == n - 1)
    def _():
        if n >= 2:
            drain((i - 1) % DISPATCH_BUFS)
        drain(slot)
        pad_fill(False)


def _dispatch(v, plan, n_tiles):
    dst, pad_start, pad_len, n_valid = plan[0], plan[1], plan[2], plan[5]
    t = v.shape[0]
    return pl.pallas_call(
        _dispatch_body,
        grid_spec=pltpu.PrefetchScalarGridSpec(
            num_scalar_prefetch=4,
            grid=(t // TM,),
            in_specs=[pl.BlockSpec(memory_space=pl.ANY)],
            out_specs=pl.BlockSpec(memory_space=pl.ANY),
            scratch_shapes=[pltpu.VMEM((DISPATCH_BUFS, TM, HALF), U32),
                            pltpu.VMEM((ROW_PIECES[0], HALF), U32),
                            pltpu.SemaphoreType.DMA((DISPATCH_BUFS,)),
                            pltpu.SemaphoreType.DMA((DISPATCH_BUFS,)),
                            pltpu.SemaphoreType.DMA(())]),
        out_shape=jax.ShapeDtypeStruct((n_tiles * TM_MOE, HALF), U32),
        compiler_params=pltpu.CompilerParams(dimension_semantics=("arbitrary",), vmem_limit_bytes=VMEM_LIMIT),
        name="moe_dispatch",
    )(dst, pad_start, pad_len, n_valid, v)


def _moe_body(n_tok, plane, tile_e_ref, tile_q0_ref, nvalid_ref, pstart_ref, cnt_ref, orow_ref,
              x_ref, wg_ref, wu_ref, wd_ref, o_hbm, ybuf0, ybuf1, ybuf2, ssem, wg, wu, wd):
    i = pl.program_id(0)
    nv = nvalid_ref[0]
    bufs = (ybuf0, ybuf1, ybuf2)

    def scratch_row0(slot):
        return (slot & 1) * plane + n_tok + (slot >> 1) * TM_MOE

    def row_write(slot, r, dst_row):
        return pltpu.make_async_copy(bufs[slot].at[pl.ds(r, 1)], o_hbm.at[pl.ds(dst_row, 1)], ssem.at[slot])

    def scratch_rows(region):
        return pltpu.make_async_copy(ybuf0, o_hbm.at[pl.ds(scratch_row0(region), TM_MOE)], ssem.at[0])

    def start_writes(tile, slot):
        e = tile_e_ref[tile]
        valid = cnt_ref[e] - tile_q0_ref[tile]
        first = pstart_ref[e] + tile_q0_ref[tile]
        last = pstart_ref[e] + cnt_ref[e] - 1
        for r in range(TM_MOE):
            row = orow_ref[jnp.minimum(first + r, last)]
            row_write(slot, r, jnp.where(r < valid, row, scratch_row0(slot) + r)).start(priority=r % 2)

    def compute(slot):
        x_lo, x_hi = (h.astype(BF16) for h in _unpack_halves(x_ref[...]))
        hg = (jnp.dot(x_lo, wg[0:HALF, :], preferred_element_type=F32)
              + jnp.dot(x_hi, wg[HALF:D_MODEL, :], preferred_element_type=F32))
        hu = (jnp.dot(x_lo, wu[0:HALF, :], preferred_element_type=F32)
              + jnp.dot(x_hi, wu[HALF:D_MODEL, :], preferred_element_type=F32))
        act = hg * _sigmoid(hg) * hu
        bufs[slot][...] = _pack_halves(jnp.dot(act.astype(BF16), wd[...], preferred_element_type=F32))

    @pl.when(i == 0)
    def _():
        ybuf0[...] = jnp.zeros(ybuf0.shape, U32)
        for region in range(4):
            scratch_rows(region).start()
        for region in range(4):
            scratch_rows(region).wait()

    @pl.when(jnp.logical_and(i >= 3, i < nv + 3))
    def _():
        def body(r, c):
            pltpu.make_async_copy(ybuf0.at[pl.ds(0, 1)], o_hbm.at[pl.ds(0, 1)], ssem.at[i % 3]).wait()
            return c
        lax.fori_loop(0, TM_MOE, body, 0, unroll=8)

    @pl.when(i < nv)
    def _():
        prev_e = tile_e_ref[jnp.maximum(i - 1, 0)]

        @pl.when(jnp.logical_or(i == 0, tile_e_ref[i] != prev_e))
        def _():
            wg[...] = wg_ref[0].astype(BF16)
            wu[...] = wu_ref[0].astype(BF16)
            wd[...] = wd_ref[0].astype(BF16)

    @pl.when(i == 0)
    def _():
        compute(0)

    for slot in range(3):
        prev = (slot + 2) % 3

        @pl.when(jnp.logical_and(i % 3 == slot, jnp.logical_and(i >= 1, i < nv)))
        def _(slot=slot, prev=prev):
            start_writes(i - 1, prev)
            compute(slot)

        @pl.when(jnp.logical_and(i % 3 == slot, i == nv))
        def _(prev=prev):
            start_writes(i - 1, prev)


def _moe(x_disp, plan, w_gate, w_up, w_down, n_tiles, n_tok):
    tile_e, tile_q0, n_valid, pair_start, counts, order = plan[3:9]
    plane = n_tok + 2 * TM_MOE
    orow = (order & 1) * plane + (order >> 1)
    tile = lambda i, nv: jnp.minimum(i, jnp.maximum(nv[0] - 1, 0))
    wspec = lambda a, b: pl.BlockSpec((1, a, b), lambda i, te, tq, nv, *_: (te[tile(i, nv)], 0, 0))
    ybuf = pltpu.VMEM((TM_MOE, HALF), U32)
    return pl.pallas_call(
        functools.partial(_moe_body, n_tok, plane),
        grid_spec=pltpu.PrefetchScalarGridSpec(
            num_scalar_prefetch=6,
            grid=(n_tiles + 3,),
            in_specs=[pl.BlockSpec((TM_MOE, HALF), lambda i, te, tq, nv, *_: (tile(i, nv), 0)),
                      wspec(D_MODEL, D_EXPERT), wspec(D_MODEL, D_EXPERT), wspec(D_EXPERT, D_MODEL)],
            out_specs=pl.BlockSpec(memory_space=pl.ANY),
            scratch_shapes=[ybuf, ybuf, ybuf,
                            pltpu.SemaphoreType.DMA((3,)),
                            pltpu.VMEM((D_MODEL, D_EXPERT), BF16),
                            pltpu.VMEM((D_MODEL, D_EXPERT), BF16),
                            pltpu.VMEM((D_EXPERT, D_MODEL), BF16)]),
        out_shape=jax.ShapeDtypeStruct((2 * plane, HALF), U32),
        compiler_params=pltpu.CompilerParams(dimension_semantics=("arbitrary",), vmem_limit_bytes=VMEM_LIMIT),
        name="moe_experts",
    )(tile_e, tile_q0, n_valid, pair_start, counts, orow, x_disp, w_gate, w_up, w_down)


def _combine_body(n_p_tiles, h2_ref, rec_ref, gf_ref, y1_ref, y2_ref, outp_ref, outs_ref):
    i = pl.program_id(0)
    rec = rec_ref[...]
    y1 = jnp.concatenate(_unpack_halves(y1_ref[...]), axis=-1)
    y2 = jnp.concatenate(_unpack_halves(y2_ref[...]), axis=-1)
    h = rec[:, 2:3] * y1 + rec[:, 3:4] * y2
    out = _rmsnorm(h2_ref[...] + h, gf_ref[...])

    @pl.when(i < n_p_tiles)
    def _():
        outp_ref[...] = out

    @pl.when(i >= n_p_tiles)
    def _():
        outs_ref[...] = out


def _combine(h2, rec, y_pairs, final_norm, t_p, t_s):
    n_p, n_s = t_p // TM, t_s // TM
    plane_tiles = y_pairs.shape[0] // 2 // TM
    return pl.pallas_call(
        functools.partial(_combine_body, n_p),
        grid=(n_p + n_s,),
        in_specs=[pl.BlockSpec((TM, D_MODEL), lambda i: (i, 0)),
                  pl.BlockSpec((TM, ROUTE_W), lambda i: (i, 0)),
                  pl.BlockSpec((1, D_MODEL), lambda i: (0, 0)),
                  pl.BlockSpec((TM, HALF), lambda i: (i, 0)),
                  pl.BlockSpec((TM, HALF), lambda i: (i + plane_tiles, 0))],
        out_specs=[pl.BlockSpec((TM, D_MODEL), lambda i: (jnp.minimum(i, n_p - 1), 0)),
                   pl.BlockSpec((TM, D_MODEL), lambda i: (jnp.clip(i - n_p, 0, n_s - 1), 0))],
        out_shape=[jax.ShapeDtypeStruct((t_p, D_MODEL), F32),
                   jax.ShapeDtypeStruct((t_s, D_MODEL), F32)],
        compiler_params=pltpu.CompilerParams(dimension_semantics=("arbitrary",), vmem_limit_bytes=VMEM_LIMIT),
        name="combine_norm",
    )(h2, rec, final_norm, y_pairs, y_pairs)


def kernel(x_prompt, x_sample, state_conv, state_ssm_re, state_ssm_im, meta_tokens, norm1, w_in, conv_w,
           lam_re, lam_im, log_dt, ssm_b_re, ssm_b_im, ssm_c_re, ssm_c_im, ssm_d, w_glu, w_conv_out,
           w_ssm_out, w_o, norm2, w_coarse, w_fine, w_gate, w_up, w_down, final_norm):
    n_pb, seq, _ = x_prompt.shape
    n_sb, dec_seq, _ = x_sample.shape
    assert dec_seq == CHUNK and seq % TM == 0 and (n_sb * dec_seq) % TM == 0 and N_META == CHUNK
    t_p, t_s = n_pb * seq, n_sb * dec_seq
    xp = x_prompt.reshape(t_p, D_MODEL)
    xsm = x_sample.reshape(t_s, D_MODEL)

    xb, z, xs, zmeta, xsmeta, sga, sgb = _in_proj(xp, xsm, meta_tokens, norm1, w_in[0])

    mats = _s5_chunk_mats(lam_re[0], lam_im[0], log_dt[0], ssm_b_re[0], ssm_b_im[0], ssm_c_re[0], ssm_c_im[0])
    y5, pf_re, pf_im, sf_re, sf_im = _s5(xs, xsmeta, state_ssm_re[0], state_ssm_im[0], mats, n_pb, n_sb, seq)

    buf = state_conv[0]
    zero = jnp.zeros((n_sb, dec_seq, D_CONV), F32)
    inj1 = zero.at[:, 0].set(buf[:, 1]).reshape(t_s, D_CONV)
    inj2 = zero.at[:, 0].set(buf[:, 0]).at[:, 1].set(buf[:, 1]).reshape(t_s, D_CONV)
    merged = _mixers(xb, z, y5, xs, sga, sgb, zmeta, inj1, inj2, conv_w[0], ssm_d, w_conv_out[0], w_glu[0],
                     w_ssm_out[0], t_p // TM, seq // TM)

    w_router = jnp.concatenate(
        [w_fine[0], w_coarse[0], jnp.zeros((D_MODEL, ROUTE_W - N_EXPERTS - N_EGROUPS), F32)], axis=1)
    h2, v, rec, cnt = _out_proj(merged, xp, xsm, norm2, w_router, w_o[0])

    n_tiles = 2 * (t_p + t_s) // TM_MOE + N_EXPERTS
    plan = _moe_plan(rec, cnt, n_tiles)
    x_disp = _dispatch(v, plan, n_tiles)
    y_pairs = _moe(x_disp, plan, w_gate[0], w_up[0], w_down[0], n_tiles, t_p + t_s)
    y_p, y_s = _combine(h2, rec, y_pairs, final_norm.reshape(1, D_MODEL), t_p, t_s)

    new_conv_p = jnp.stack([z[(b + 1) * seq - 2:(b + 1) * seq] for b in range(n_pb)])
    new_conv_s = z[t_p:].reshape(n_sb, dec_seq, D_CONV)[:, dec_seq - 2:]
    return (y_p.reshape(n_pb, seq, D_MODEL), y_s.reshape(n_sb, dec_seq, D_MODEL),
            new_conv_p[None], pf_re, pf_im, new_conv_s[None], sf_re, sf_im)
```

```python
import functools

import jax
import jax.numpy as jnp
from jax import lax
from jax.experimental import pallas as pl
from jax.experimental.pallas import tpu as pltpu

F32 = jnp.float32
BF16 = jnp.bfloat16
I32 = jnp.int32
U32 = jnp.uint32

D_MODEL = 2048
D_CONV = 1024
D_SSM = 1024
SSM_H = 16
SSM_G = 64
SSM_P = 64
N_META = 16
N_EGROUPS = 4
EXPERTS_PER_GROUP = 8
N_EXPERTS = 32
D_EXPERT = 256
EPS = 1e-6

LANES = 128
SUBLANES = 8

CHUNK = 16
CHUNK_W = CHUNK * SSM_H

TM = 256
TM_MOE = 256
VMEM_LIMIT = 52 * 1024 * 1024


def _rmsnorm(x, g):
    return x * lax.rsqrt(jnp.mean(x * x, axis=-1, keepdims=True) + EPS) * g


def _sigmoid(x):
    return 1.0 / (1.0 + jnp.exp(-x))


def _gelu_tanh(x):
    c = 0.7978845608028654
    return 0.5 * x * (1.0 + jnp.tanh(c * (x + 0.044715 * (x * x * x))))


HALF = D_MODEL // 2


def _pack_halves(a):
    return pltpu.pack_elementwise([a[:, :HALF], a[:, HALF:]], packed_dtype=BF16)


def _unpack_halves(p):
    return (pltpu.unpack_elementwise(p, index=0, packed_dtype=BF16, unpacked_dtype=F32),
            pltpu.unpack_elementwise(p, index=1, packed_dtype=BF16, unpacked_dtype=F32))


def _weight_copy(w_hbm, stage, sem, c, slot, rows, col0, ncols):
    return pltpu.make_async_copy(
        w_hbm.at[pl.ds(c * rows, rows), pl.ds(col0, ncols)], stage.at[slot], sem.at[slot])


def _load_weight_bf16(w_hbm, w_vmem, stage, sem, col0=0):
    k, n = w_vmem.shape
    rows = stage.shape[1]
    nchunk = k // rows
    _weight_copy(w_hbm, stage, sem, 0, 0, rows, col0, n).start()
    for c in range(nchunk):
        slot = c % 2
        if c + 1 < nchunk:
            _weight_copy(w_hbm, stage, sem, c + 1, 1 - slot, rows, col0, n).start()
        _weight_copy(w_hbm, stage, sem, c, slot, rows, col0, n).wait()
        w_vmem[pl.ds(c * rows, rows), :] = stage[slot].astype(BF16)


def _in_proj_mix_body(n_p_tiles, xp_ref, xsm_ref, meta_ref, g_ref, w_hbm,
                      xb_out, z_out, xs_out, zmeta_out, xsmeta_out,
                      w_vmem, stage, sem):
    i = pl.program_id(0)
    g = g_ref[...]

    def project(u):
        xb = jnp.dot(u, w_vmem[:, 0:D_CONV], preferred_element_type=F32)
        xc = jnp.dot(u, w_vmem[:, D_CONV:2 * D_CONV], preferred_element_type=F32)
        xv = jnp.dot(u, w_vmem[:, 2 * D_CONV:3 * D_CONV], preferred_element_type=F32)
        xs = jnp.dot(u, w_vmem[:, 3 * D_CONV:3 * D_CONV + D_SSM], preferred_element_type=F32)
        return xb, xc * xv, xs

    @pl.when(i == 0)
    def _():
        _load_weight_bf16(w_hbm, w_vmem, stage, sem, col0=0)
        um = _rmsnorm(meta_ref[...], g).astype(BF16)
        _, zm, xsm = project(um)
        zmeta_out[...] = zm
        xsmeta_out[...] = xsm

    x = jnp.where(i < n_p_tiles, xp_ref[...], xsm_ref[...])
    u = _rmsnorm(x, g).astype(BF16)
    xb, z, xs = project(u)
    xb_out[...] = xb.astype(BF16)
    z_out[...] = z
    xs_out[...] = xs


def _in_proj_gate_body(n_p_tiles, xp_ref, xsm_ref, g_ref, w_hbm, ga_out, gb_out,
                       w_vmem, stage, sem):
    i = pl.program_id(0)

    @pl.when(i == 0)
    def _():
        _load_weight_bf16(w_hbm, w_vmem, stage, sem, col0=3 * D_CONV + D_SSM)

    x = jnp.where(i < n_p_tiles, xp_ref[...], xsm_ref[...])
    u = _rmsnorm(x, g_ref[...]).astype(BF16)
    ga = jnp.dot(u, w_vmem[:, 0:D_MODEL], preferred_element_type=F32)
    ga_out[...] = _sigmoid(ga).astype(BF16)
    gb = jnp.dot(u, w_vmem[:, D_MODEL:2 * D_MODEL], preferred_element_type=F32)
    gb_out[...] = _sigmoid(gb).astype(BF16)


def _two_stream_specs(n_p_tiles, n_s_tiles):
    xp_spec = pl.BlockSpec((TM, D_MODEL), lambda i: (jnp.minimum(i, n_p_tiles - 1), 0))
    xs_spec = pl.BlockSpec((TM, D_MODEL), lambda i: (jnp.clip(i - n_p_tiles, 0, n_s_tiles - 1), 0))
    return xp_spec, xs_spec


def _in_proj(xp, xsm, meta, norm1, w_in):
    t_p, t_s = xp.shape[0], xsm.shape[0]
    n_p, n_s = t_p // TM, t_s // TM
    t = t_p + t_s
    half = 3 * D_CONV + D_SSM
    xp_spec, xs_spec = _two_stream_specs(n_p, n_s)
    g_spec = pl.BlockSpec((1, D_MODEL), lambda i: (0, 0))
    any_spec = pl.BlockSpec(memory_space=pl.ANY)
    stage_rows = 128
    row = lambda w: pl.BlockSpec((TM, w), lambda i: (i, 0))
    const = lambda r, w: pl.BlockSpec((r, w), lambda i: (0, 0))
    params = pltpu.CompilerParams(dimension_semantics=("arbitrary",), vmem_limit_bytes=VMEM_LIMIT)
    scratch = [pltpu.VMEM((D_MODEL, half), BF16),
               pltpu.VMEM((2, stage_rows, half), F32),
               pltpu.SemaphoreType.DMA((2,))]

    xb, z, xs, zmeta, xsmeta = pl.pallas_call(
        functools.partial(_in_proj_mix_body, n_p),
        grid=(n_p + n_s,),
        in_specs=[xp_spec, xs_spec, const(N_META, D_MODEL), g_spec, any_spec],
        out_specs=[row(D_CONV), row(D_CONV), row(D_SSM), const(N_META, D_CONV), const(N_META, D_SSM)],
        out_shape=[jax.ShapeDtypeStruct((t, D_CONV), BF16),
                   jax.ShapeDtypeStruct((t, D_CONV), F32),
                   jax.ShapeDtypeStruct((t, D_SSM), F32),
                   jax.ShapeDtypeStruct((N_META, D_CONV), F32),
                   jax.ShapeDtypeStruct((N_META, D_SSM), F32)],
        scratch_shapes=scratch,
        compiler_params=params,
        name="in_proj_mix",
    )(xp, xsm, meta, norm1, w_in)

    sga, sgb = pl.pallas_call(
        functools.partial(_in_proj_gate_body, n_p),
        grid=(n_p + n_s,),
        in_specs=[xp_spec, xs_spec, g_spec, any_spec],
        out_specs=[row(D_MODEL), row(D_MODEL)],
        out_shape=[jax.ShapeDtypeStruct((t, D_MODEL), BF16),
                   jax.ShapeDtypeStruct((t, D_MODEL), BF16)],
        scratch_shapes=scratch,
        compiler_params=params,
        name="in_proj_gate",
    )(xp, xsm, norm1, w_in)
    return xb, z, xs, zmeta, xsmeta, sga, sgb


S5_GROUPS_PER_STEP = LANES // SSM_H
S5_PAIRS_PER_STEP = S5_GROUPS_PER_STEP // 2


def _s5_chunk_mats(lam_re, lam_im, log_dt, b_re, b_im, c_re, c_im):
    dt = jnp.exp(log_dt)[:, None]
    lr, li = lam_re, lam_im
    mag = jnp.exp(lr * dt)
    ab_re, ab_im = mag * jnp.cos(li * dt), mag * jnp.sin(li * dt)
    nr, ni = ab_re - 1.0, ab_im
    den = lr * lr + li * li
    k_re = (nr * lr + ni * li) / den
    k_im = (ni * lr - nr * li) / den
    bb_re = k_re[..., None] * b_re - k_im[..., None] * b_im
    bb_im = k_re[..., None] * b_im + k_im[..., None] * b_re
    d = jnp.arange(CHUNK + 1, dtype=F32)[:, None, None]
    pmag = jnp.exp(d * (lr * dt)[None])
    pw_re = pmag * jnp.cos(d * (li * dt)[None])
    pw_im = pmag * jnp.sin(d * (li * dt)[None])
    ab_b_re = pw_re[..., None] * bb_re[None] - pw_im[..., None] * bb_im[None]
    ab_b_im = pw_re[..., None] * bb_im[None] + pw_im[..., None] * bb_re[None]
    m = jnp.sum(c_re[None, :, :, :, None] * ab_b_re[:CHUNK, :, None, :, :]
                - c_im[None, :, :, :, None] * ab_b_im[:CHUNK, :, None, :, :], axis=3)
    rows = []
    zero = jnp.zeros((SSM_G, SSM_H, SSM_H), F32)
    for t in range(CHUNK):
        rows.append(jnp.concatenate([m[t - j] if t >= j else zero for j in range(CHUNK)], axis=2))
    toe_t = jnp.concatenate(rows, axis=1)
    p_re = jnp.transpose(ab_b_re[:CHUNK][::-1], (1, 2, 0, 3)).reshape(SSM_G, SSM_P, CHUNK_W)
    p_im = jnp.transpose(ab_b_im[:CHUNK][::-1], (1, 2, 0, 3)).reshape(SSM_G, SSM_P, CHUNK_W)
    p_t = jnp.concatenate([p_re, p_im], axis=1)
    pr, pi = pw_re[1:], pw_im[1:]
    q_re = c_re[None] * pr[:, :, None, :] - c_im[None] * pi[:, :, None, :]
    q_im = -(c_re[None] * pi[:, :, None, :] + c_im[None] * pr[:, :, None, :])
    q_re = jnp.transpose(q_re, (1, 0, 2, 3)).reshape(SSM_G, CHUNK_W, SSM_P)
    q_im = jnp.transpose(q_im, (1, 0, 2, 3)).reshape(SSM_G, CHUNK_W, SSM_P)
    even = (jnp.arange(SSM_G) % 2 == 0)[:, None, None]
    zq = jnp.zeros_like(q_re)
    pad2 = lambda q: jnp.where(even, jnp.concatenate([q, zq], axis=2), jnp.concatenate([zq, q], axis=2))
    q_t = jnp.stack([pad2(q_re), pad2(q_im)], axis=1)
    a16_re = pw_re[CHUNK].reshape(SSM_G // 2, 1, 2 * SSM_P)
    a16_im = pw_im[CHUNK].reshape(SSM_G // 2, 1, 2 * SSM_P)
    return toe_t.astype(BF16), p_t.astype(BF16), q_t.astype(BF16), a16_re, a16_im


def _s5_body(n_pc, n_pb, n_sb, xs_ref, xsmeta_ref, toe_ref, p_ref, q_ref, are_ref, aim_ref, s0re_ref, s0im_ref,
             y_out, pfre_out, pfim_out, sfre_out, sfim_out,
             xt_scr, u_scr, st_re, st_im, sl_re, sl_im, sp_re, sp_im, yt_scr):
    gb = S5_GROUPS_PER_STEP
    n_p_rows = n_pb * n_pc
    row_s = n_p_rows
    row_m = row_s + n_sb
    t_p = n_p_rows * CHUNK
    rows_pad = xt_scr.shape[0]

    xt_scr[row_m + 1:rows_pad, :] = jnp.zeros((rows_pad - row_m - 1, LANES), F32)
    for t in range(CHUNK):
        xt_scr[0:n_p_rows, :] = xs_ref[pl.ds(t, n_p_rows, stride=CHUNK), :]
        xt_scr[row_s:row_m, :] = xs_ref[pl.ds(t_p + t, n_sb, stride=CHUNK), :]
        xt_scr[row_m:row_m + 1, :] = xsmeta_ref[t:t + 1, :]
        xt = xt_scr[...].T.astype(BF16)
        for k in range(gb):
            u_scr[k, t * SSM_H:(t + 1) * SSM_H, :] = xt[k * SSM_H:(k + 1) * SSM_H, :]

    for k in range(gb):
        sl = jnp.dot(p_ref[k], u_scr[k], preferred_element_type=F32)
        half = (k % 2) * SSM_P
        st_re[k // 2, half:half + SSM_P, :] = sl[0:SSM_P, :]
        st_im[k // 2, half:half + SSM_P, :] = sl[SSM_P:2 * SSM_P, :]
    npair = S5_PAIRS_PER_STEP
    for j in range(npair):
        sl_re[j] = st_re[j].T
        sl_im[j] = st_im[j].T
        sp_re[j, row_m:rows_pad, :] = jnp.zeros((rows_pad - row_m, 2 * SSM_P), F32)
        sp_im[j, row_m:rows_pad, :] = jnp.zeros((rows_pad - row_m, 2 * SSM_P), F32)

    ar = [are_ref[j] for j in range(npair)]
    ai = [aim_ref[j] for j in range(npair)]
    sre = [jnp.broadcast_to(sl_re[j, row_m:row_m + 1, :], (n_pb, 2 * SSM_P)) for j in range(npair)]
    sim = [jnp.broadcast_to(sl_im[j, row_m:row_m + 1, :], (n_pb, 2 * SSM_P)) for j in range(npair)]
    for c in range(n_pc):
        rows = pl.ds(c, n_pb, stride=n_pc)
        for j in range(npair):
            sp_re[j, rows, :] = sre[j]
            sp_im[j, rows, :] = sim[j]
            nre = ar[j] * sre[j] - ai[j] * sim[j] + sl_re[j, rows, :]
            nim = ar[j] * sim[j] + ai[j] * sre[j] + sl_im[j, rows, :]
            sre[j], sim[j] = nre, nim
    for j in range(npair):
        pfre_out[j] = sre[j]
        pfim_out[j] = sim[j]
        s0r, s0i = s0re_ref[j], s0im_ref[j]
        sp_re[j, row_s:row_m, :] = s0r
        sp_im[j, row_s:row_m, :] = s0i
        sfre_out[j] = ar[j] * s0r - ai[j] * s0i + sl_re[j, row_s:row_m, :]
        sfim_out[j] = ar[j] * s0i + ai[j] * s0r + sl_im[j, row_s:row_m, :]

    nt = (((1,), (1,)), ((), ()))
    for k in range(gb):
        y = jnp.dot(toe_ref[k], u_scr[k], preferred_element_type=F32)
        y += lax.dot_general(q_ref[k, 0], sp_re[k // 2].astype(BF16), nt, preferred_element_type=F32)
        y += lax.dot_general(q_ref[k, 1], sp_im[k // 2].astype(BF16), nt, preferred_element_type=F32)
        for t in range(CHUNK):
            yt_scr[t, k * SSM_H:(k + 1) * SSM_H, :] = y[t * SSM_H:(t + 1) * SSM_H, :]
    for t in range(CHUNK):
        yt = yt_scr[t].T
        y_out[pl.ds(t, n_p_rows, stride=CHUNK), :] = yt[0:n_p_rows, :]
        y_out[pl.ds(t_p + t, n_sb, stride=CHUNK), :] = yt[row_s:row_m, :]


def _s5(xs, xsmeta, state_re, state_im, mats, n_pb, n_sb, seq):
    toe_t, p_t, q_t, a_re, a_im = mats
    t = xs.shape[0]
    n_pc = seq // CHUNK
    rows = n_pc * n_pb + n_sb + 1
    rows_pad = -(-rows // LANES) * LANES
    gb, npair = S5_GROUPS_PER_STEP, S5_PAIRS_PER_STEP
    pairs = lambda s: jnp.transpose(s.reshape(n_sb, SSM_G // 2, 2 * SSM_P), (1, 0, 2))
    blk3 = lambda n, r, c: pl.BlockSpec((n, r, c), lambda i: (i, 0, 0))
    y, pfre, pfim, sfre, sfim = pl.pallas_call(
        functools.partial(_s5_body, n_pc, n_pb, n_sb),
        grid=(SSM_G // gb,),
        in_specs=[pl.BlockSpec((t, LANES), lambda i: (0, i)),
                  pl.BlockSpec((N_META, LANES), lambda i: (0, i)),
                  blk3(gb, CHUNK_W, CHUNK_W), blk3(gb, 2 * SSM_P, CHUNK_W),
                  pl.BlockSpec((gb, 2, CHUNK_W, 2 * SSM_P), lambda i: (i, 0, 0, 0)),
                  blk3(npair, 1, 2 * SSM_P), blk3(npair, 1, 2 * SSM_P),
                  blk3(npair, n_sb, 2 * SSM_P), blk3(npair, n_sb, 2 * SSM_P)],
        out_specs=[pl.BlockSpec((t, LANES), lambda i: (0, i)),
                   blk3(npair, n_pb, 2 * SSM_P), blk3(npair, n_pb, 2 * SSM_P),
                   blk3(npair, n_sb, 2 * SSM_P), blk3(npair, n_sb, 2 * SSM_P)],
        out_shape=[jax.ShapeDtypeStruct((t, D_SSM), F32),
                   jax.ShapeDtypeStruct((SSM_G // 2, n_pb, 2 * SSM_P), F32),
                   jax.ShapeDtypeStruct((SSM_G // 2, n_pb, 2 * SSM_P), F32),
                   jax.ShapeDtypeStruct((SSM_G // 2, n_sb, 2 * SSM_P), F32),
                   jax.ShapeDtypeStruct((SSM_G // 2, n_sb, 2 * SSM_P), F32)],
        scratch_shapes=[pltpu.VMEM((rows_pad, LANES), F32),
                        pltpu.VMEM((gb, CHUNK_W, rows_pad), BF16),
                        pltpu.VMEM((npair, 2 * SSM_P, rows_pad), F32),
                        pltpu.VMEM((npair, 2 * SSM_P, rows_pad), F32),
                        pltpu.VMEM((npair, rows_pad, 2 * SSM_P), F32),
                        pltpu.VMEM((npair, rows_pad, 2 * SSM_P), F32),
                        pltpu.VMEM((npair, rows_pad, 2 * SSM_P), F32),
                        pltpu.VMEM((npair, rows_pad, 2 * SSM_P), F32),
                        pltpu.VMEM((CHUNK, LANES, rows_pad), F32)],
        compiler_params=pltpu.CompilerParams(dimension_semantics=("arbitrary",), vmem_limit_bytes=VMEM_LIMIT),
        name="s5_chunks",
    )(xs, xsmeta, toe_t, p_t, q_t, a_re, a_im, pairs(state_re), pairs(state_im))
    unpair = lambda a: jnp.transpose(a, (1, 0, 2)).reshape(a.shape[1], SSM_G, SSM_P)[None]
    return y, unpair(pfre), unpair(pfim), unpair(sfre), unpair(sfim)


def _mixers_body(n_p_tiles, tiles_per_seq, xb_ref, z_ref, y5_ref, xs_ref, sga_ref, sgb_ref,
                 zmeta_ref, inj1_ref, inj2_ref, cw_ref, dskip_ref,
                 wc_hbm, wg_hbm, wso_hbm, merged_out,
                 wc, wg, wso, stage_a, stage_b, sem, carry):
    i = pl.program_id(0)

    @pl.when(i == 0)
    def _():
        _load_weight_bf16(wc_hbm, wc, stage_a, sem)
        _load_weight_bf16(wg_hbm, wg, stage_b, sem)
        _load_weight_bf16(wso_hbm, wso, stage_a, sem)

    @pl.when(jnp.logical_and(i < n_p_tiles, i % tiles_per_seq == 0))
    def _():
        carry[0:2, :] = zmeta_ref[N_META - 2:N_META, :]

    z = z_ref[...]
    row = lax.broadcasted_iota(I32, (TM, 1), 0)
    is_s = i >= n_p_tiles
    r1 = pltpu.roll(z, 1, 0)
    r2 = pltpu.roll(z, 2, 0)
    c1 = carry[1:2, :]
    c2 = carry[0:1, :]
    pos = jnp.where(is_s, row & (CHUNK - 1), row)
    first1 = pos == 0
    first2 = pos < 2
    fill1 = jnp.where(is_s, inj1_ref[...], jnp.broadcast_to(c1, z.shape))
    fill2 = jnp.where(is_s, inj2_ref[...], jnp.where(row == 0, c2, c1))
    zp1 = jnp.where(first1, fill1, r1)
    zp2 = jnp.where(first2, fill2, r2)
    carry[0:2, :] = z[TM - 2:TM, :]

    cw = cw_ref[...]
    conv = cw[0:1, :] * zp2 + cw[1:2, :] * zp1 + cw[2:3, :] * z
    a_in = (xb_ref[...].astype(F32) * conv).astype(BF16)
    ya = jnp.dot(a_in, wc[...], preferred_element_type=F32)

    ys = y5_ref[...] + dskip_ref[...] * xs_ref[...]
    ys = _gelu_tanh(ys)
    glu = jnp.dot(ys.astype(BF16), wg[...], preferred_element_type=F32)
    ys = ys * _sigmoid(glu)
    yb = jnp.dot(ys.astype(BF16), wso[...], preferred_element_type=F32)

    merged = sga_ref[...].astype(F32) * ya + sgb_ref[...].astype(F32) * yb
    merged_out[...] = merged.astype(BF16)


def _mixers(xb, z, y5, xs, sga, sgb, zmeta, inj1, inj2, conv_w, d_skip, w_conv_out, w_glu, w_ssm_out,
            n_p_tiles, tiles_per_seq):
    t = xb.shape[0]
    n_s_tiles = inj1.shape[0] // TM
    row = lambda w: pl.BlockSpec((TM, w), lambda i: (i, 0))
    const = lambda r, w: pl.BlockSpec((r, w), lambda i: (0, 0))
    inj = pl.BlockSpec((TM, D_CONV), lambda i: (jnp.clip(i - n_p_tiles, 0, n_s_tiles - 1), 0))
    any_spec = pl.BlockSpec(memory_space=pl.ANY)
    return pl.pallas_call(
        functools.partial(_mixers_body, n_p_tiles, tiles_per_seq),
        grid=(t // TM,),
        in_specs=[row(D_CONV), row(D_CONV), row(D_SSM), row(D_SSM), row(D_MODEL), row(D_MODEL),
                  const(N_META, D_CONV), inj, inj, const(3, D_CONV), const(1, D_SSM),
                  any_spec, any_spec, any_spec],
        out_specs=row(D_MODEL),
        out_shape=jax.ShapeDtypeStruct((t, D_MODEL), BF16),
        scratch_shapes=[pltpu.VMEM((D_CONV, D_MODEL), BF16),
                        pltpu.VMEM((D_SSM, D_SSM), BF16),
                        pltpu.VMEM((D_SSM, D_MODEL), BF16),
                        pltpu.VMEM((2, 256, D_MODEL), F32),
                        pltpu.VMEM((2, 256, D_SSM), F32),
                        pltpu.SemaphoreType.DMA((2,)),
                        pltpu.VMEM((8, D_CONV), F32)],
        compiler_params=pltpu.CompilerParams(dimension_semantics=("arbitrary",), vmem_limit_bytes=VMEM_LIMIT),
        name="mixers",
    )(xb, z, y5, xs, sga, sgb, zmeta, inj1, inj2, conv_w, d_skip, w_conv_out, w_glu, w_ssm_out)


ROUTE_W = LANES
COARSE0 = N_EXPERTS


def _route(logits, cnt):
    col = lax.broadcasted_iota(I32, logits.shape, 1)
    colf = col.astype(F32)
    neg = jnp.float32(-jnp.inf)
    big = jnp.float32(1 << 20)
    is_c = jnp.logical_and(col >= COARSE0, col < COARSE0 + N_EGROUPS)
    lc = jnp.where(is_c, logits, neg)
    cmax = jnp.max(lc, axis=-1, keepdims=True)
    gi = jnp.min(jnp.where(lc == cmax, colf - COARSE0, big), axis=-1, keepdims=True)
    pg = 1.0 / jnp.sum(jnp.where(is_c, jnp.exp(lc - cmax), 0.0), axis=-1, keepdims=True)
    grp = (col >> 3).astype(F32)
    in_g = jnp.logical_and(col < N_EXPERTS, grp == gi)
    lf = jnp.where(in_g, logits, neg)
    m1 = jnp.max(lf, axis=-1, keepdims=True)
    i1 = jnp.min(jnp.where(lf == m1, colf, big), axis=-1, keepdims=True)
    lf2 = jnp.where(colf == i1, neg, lf)
    m2 = jnp.max(lf2, axis=-1, keepdims=True)
    i2 = jnp.min(jnp.where(lf2 == m2, colf, big), axis=-1, keepdims=True)
    e2 = jnp.exp(m2 - m1)
    w1 = pg / (1.0 + e2)
    w2 = pg * e2 / (1.0 + e2)
    n = logits.shape[0]
    hit1 = colf == i1
    hit2 = colf == i2
    onehot = jnp.where(jnp.logical_or(hit1, hit2), 1.0, 0.0)
    rr = lax.broadcasted_iota(I32, (n, n), 0)
    cc = lax.broadcasted_iota(I32, (n, n), 1)
    tri = jnp.where(cc < rr, 1.0, 0.0).astype(BF16)
    pos = jnp.dot(tri, onehot.astype(BF16), preferred_element_type=F32) + cnt
    rank1 = jnp.sum(jnp.where(hit1, pos, 0.0), axis=-1, keepdims=True)
    rank2 = jnp.sum(jnp.where(hit2, pos, 0.0), axis=-1, keepdims=True)
    vals = (i1, i2, w1, w2, rank1, rank2)
    rec = jnp.zeros(logits.shape, F32)
    for c, val in enumerate(vals):
        rec = jnp.where(col == c, val, rec)
    return rec, cnt + jnp.sum(onehot, axis=0, keepdims=True)


def _split_bf16(a):
    hi = a.astype(BF16)
    lo = (a - hi.astype(F32)).astype(BF16)
    return hi, lo


def _out_proj_body(n_p_tiles, merged_ref, xp_ref, xsm_ref, g2_ref, wr_ref, wo_hbm,
                   h2_out, v_out, rec_out, cnt_out, wo, stage, sem):
    i = pl.program_id(0)

    @pl.when(i == 0)
    def _():
        _load_weight_bf16(wo_hbm, wo, stage, sem)
        cnt_out[...] = jnp.zeros(cnt_out.shape, F32)

    x = jnp.where(i < n_p_tiles, xp_ref[...], xsm_ref[...])
    h2 = x + jnp.dot(merged_ref[...], wo[...], preferred_element_type=F32)
    h2_out[...] = h2
    v = _rmsnorm(h2, g2_ref[...])
    v_out[...] = _pack_halves(v)
    v_hi, v_lo = _split_bf16(v)
    w_hi, w_lo = _split_bf16(wr_ref[...])
    logits = (jnp.dot(v_hi, w_hi, preferred_element_type=F32)
              + jnp.dot(v_lo, w_hi, preferred_element_type=F32)
              + jnp.dot(v_hi, w_lo, preferred_element_type=F32))
    rec, cnt = _route(logits, cnt_out[...])
    rec_out[...] = rec
    cnt_out[...] = cnt


def _out_proj(merged, xp, xsm, norm2, w_router, w_o):
    t_p, t_s = xp.shape[0], xsm.shape[0]
    n_p, n_s = t_p // TM, t_s // TM
    t = t_p + t_s
    xp_spec, xs_spec = _two_stream_specs(n_p, n_s)
    row = lambda w: pl.BlockSpec((TM, w), lambda i: (i, 0))
    const = lambda r, w: pl.BlockSpec((r, w), lambda i: (0, 0))
    return pl.pallas_call(
        functools.partial(_out_proj_body, n_p),
        grid=(n_p + n_s,),
        in_specs=[row(D_MODEL), xp_spec, xs_spec, const(1, D_MODEL), const(D_MODEL, ROUTE_W),
                  pl.BlockSpec(memory_space=pl.ANY)],
        out_specs=[row(D_MODEL), row(HALF), row(ROUTE_W), const(1, ROUTE_W)],
        out_shape=[jax.ShapeDtypeStruct((t, D_MODEL), F32),
                   jax.ShapeDtypeStruct((t, HALF), U32),
                   jax.ShapeDtypeStruct((t, ROUTE_W), F32),
                   jax.ShapeDtypeStruct((1, ROUTE_W), F32)],
        scratch_shapes=[pltpu.VMEM((D_MODEL, D_MODEL), BF16),
                        pltpu.VMEM((2, 256, D_MODEL), F32),
                        pltpu.SemaphoreType.DMA((2,))],
        compiler_params=pltpu.CompilerParams(dimension_semantics=("arbitrary",), vmem_limit_bytes=VMEM_LIMIT),
        name="out_proj_route",
    )(merged, xp, xsm, norm2, w_router, w_o)


def _moe_plan(rec, cnt, n_tiles):
    t = rec.shape[0]
    n_pairs = 2 * t
    eid = rec[:, 0:2].astype(I32).reshape(-1)
    rank = rec[:, 4:6].astype(I32).reshape(-1)
    counts = cnt[0, :N_EXPERTS].astype(I32)
    pair_start = jnp.cumsum(counts) - counts
    experts = jnp.arange(N_EXPERTS, dtype=I32)
    onehot = (eid[:, None] == experts[None, :]).astype(I32)
    pos = rank + jnp.sum(onehot * pair_start[None, :], axis=1)
    _, order = lax.sort((pos, jnp.arange(n_pairs, dtype=I32)), num_keys=1)
    tiles_e = (counts + TM_MOE - 1) // TM_MOE
    tile_end = jnp.cumsum(tiles_e)
    tile_start = tile_end - tiles_e
    n_valid = tile_end[-1]
    tile_ids = jnp.arange(n_tiles, dtype=I32)
    tile_e = jnp.sum((tile_ids[:, None] >= tile_end[None, :]).astype(I32), axis=1)
    last_e = jnp.sum((n_valid - 1 >= tile_end).astype(I32))
    tile_e = jnp.minimum(jnp.where(tile_ids < n_valid, tile_e, last_e), N_EXPERTS - 1)
    tile_onehot = (tile_e[:, None] == experts[None, :]).astype(I32)
    tile_q0 = (tile_ids - jnp.sum(tile_onehot * tile_start[None, :], axis=1)) * TM_MOE
    tile_q0 = jnp.where(tile_ids < n_valid, tile_q0, 0)
    dst = rank + jnp.sum(onehot * (tile_start * TM_MOE)[None, :], axis=1)
    pad_start = tile_start * TM_MOE + counts
    pad_len = tiles_e * TM_MOE - counts
    return (dst.astype(I32), pad_start.astype(I32), pad_len.astype(I32), tile_e, tile_q0.astype(I32),
            n_valid.astype(I32).reshape(1), pair_start, counts, order)


DISPATCH_BUFS = 3
ROW_PIECES = tuple(TM_MOE >> (b + 1) for b in range(TM_MOE.bit_length() - 1))


def _dispatch_body(dst_ref, pad_start_ref, pad_len_ref, nvalid_ref, v_hbm, x_hbm, buf, zbuf, rsem, ssem, zsem):
    i = pl.program_id(0)
    n = pl.num_programs(0)

    def read(tile, slot):
        return pltpu.make_async_copy(v_hbm.at[pl.ds(pl.multiple_of(tile * TM, TM), TM)], buf.at[slot],
                                     rsem.at[slot])

    def row_write(slot, r, dst_row):
        return pltpu.make_async_copy(buf.at[slot, pl.ds(r, 1)], x_hbm.at[pl.ds(dst_row, 1)], ssem.at[slot])

    def drain(slot):
        for _ in range(2):
            pltpu.make_async_copy(buf.at[slot], x_hbm.at[pl.ds(0, TM)], ssem.at[slot]).wait()

    def pad_fill(go):
        def zero_rows(start, size):
            d = pltpu.make_async_copy(zbuf.at[pl.ds(0, size)], x_hbm.at[pl.ds(start, size)], zsem)
            d.start() if go else d.wait()

        def body(e, c):
            start, length = pad_start_ref[e], pad_len_ref[e]
            head = (-start) & (SUBLANES - 1)
            for h in range(SUBLANES - 1):
                @pl.when(h < head)
                def _(h=h):
                    zero_rows(start + h, 1)
            start, length = start + head, length - head
            for size in ROW_PIECES:
                if size >= SUBLANES:
                    @pl.when((length & size) != 0)
                    def _(size=size):
                        zero_rows(pl.multiple_of(start + (length & (-2 * size)), SUBLANES), size)
            return c
        lax.fori_loop(0, N_EXPERTS, body, 0)

        def unused(tile, c):
            for half in range(TM_MOE // ROW_PIECES[0]):
                zero_rows(pl.multiple_of(tile * TM_MOE + half * ROW_PIECES[0], SUBLANES), ROW_PIECES[0])
            return c
        lax.fori_loop(nvalid_ref[0], x_hbm.shape[0] // TM_MOE, unused, 0)

    @pl.when(i == 0)
    def _():
        zbuf[...] = jnp.zeros(zbuf.shape, U32)
        pad_fill(True)
        read(0, 0).start()

    @pl.when(i >= 2)
    def _():
        drain((i + 1) % DISPATCH_BUFS)

    @pl.when(i + 1 < n)
    def _():
        read(i + 1, (i + 1) % DISPATCH_BUFS).start()

    slot = i % DISPATCH_BUFS
    read(i, slot).wait()
    for r in range(TM):
        for k in range(2):
            row_write(slot, r, dst_ref[2 * (i * TM + r) + k]).start(priority=k)

    @pl.when(i == n - 1)
    def _():
        if n >= 2:
            drain((i - 1) % DISPATCH_BUFS)
        drain(slot)
        pad_fill(False)


def _dispatch(v, plan, n_tiles):
    dst, pad_start, pad_len, n_valid = plan[0], plan[1], plan[2], plan[5]
    t = v.shape[0]
    return pl.pallas_call(
        _dispatch_body,
        grid_spec=pltpu.PrefetchScalarGridSpec(
            num_scalar_prefetch=4,
            grid=(t // TM,),
            in_specs=[pl.BlockSpec(memory_space=pl.ANY)],
            out_specs=pl.BlockSpec(memory_space=pl.ANY),
            scratch_shapes=[pltpu.VMEM((DISPATCH_BUFS, TM, HALF), U32),
                            pltpu.VMEM((ROW_PIECES[0], HALF), U32),
                            pltpu.SemaphoreType.DMA((DISPATCH_BUFS,)),
                            pltpu.SemaphoreType.DMA((DISPATCH_BUFS,)),
                            pltpu.SemaphoreType.DMA(())]),
        out_shape=jax.ShapeDtypeStruct((n_tiles * TM_MOE, HALF), U32),
        compiler_params=pltpu.CompilerParams(dimension_semantics=("arbitrary",), vmem_limit_bytes=VMEM_LIMIT),
        name="moe_dispatch",
    )(dst, pad_start, pad_len, n_valid, v)


WEIGHT_SLOTS = 3


def _moe_body(n_tok, plane, tile_e_ref, tile_q0_ref, nvalid_ref, pstart_ref, cnt_ref, orow_ref,
              krank_ref, elist_ref, nexp_ref,
              x_ref, wg_hbm, wu_hbm, wd_hbm, o_hbm, ybuf0, ybuf1, ybuf2, ssem,
              stage_g, stage_u, stage_d, wsem, wg, wu, wd):
    i = pl.program_id(0)
    nv = nvalid_ref[0]
    bufs = (ybuf0, ybuf1, ybuf2)

    def scratch_row0(slot):
        return (slot & 1) * plane + n_tok + (slot >> 1) * TM_MOE

    def row_write(slot, r, dst_row):
        return pltpu.make_async_copy(bufs[slot].at[pl.ds(r, 1)], o_hbm.at[pl.ds(dst_row, 1)], ssem.at[slot])

    def scratch_rows(region):
        return pltpu.make_async_copy(ybuf0, o_hbm.at[pl.ds(scratch_row0(region), TM_MOE)], ssem.at[0])

    def start_writes(tile, slot):
        e = tile_e_ref[tile]
        valid = cnt_ref[e] - tile_q0_ref[tile]
        first = pstart_ref[e] + tile_q0_ref[tile]
        last = pstart_ref[e] + cnt_ref[e] - 1
        for r in range(TM_MOE):
            row = orow_ref[jnp.minimum(first + r, last)]
            row_write(slot, r, jnp.where(r < valid, row, scratch_row0(slot) + r)).start(priority=r % 2)

    def compute(slot):
        x_lo, x_hi = (h.astype(BF16) for h in _unpack_halves(x_ref[...]))
        hg = (jnp.dot(x_lo, wg[0:HALF, :], preferred_element_type=F32)
              + jnp.dot(x_hi, wg[HALF:D_MODEL, :], preferred_element_type=F32))
        hu = (jnp.dot(x_lo, wu[0:HALF, :], preferred_element_type=F32)
              + jnp.dot(x_hi, wu[HALF:D_MODEL, :], preferred_element_type=F32))
        act = hg * _sigmoid(hg) * hu
        bufs[slot][...] = _pack_halves(jnp.dot(act.astype(BF16), wd[...], preferred_element_type=F32))

    @pl.when(i == 0)
    def _():
        ybuf0[...] = jnp.zeros(ybuf0.shape, U32)
        for region in range(4):
            scratch_rows(region).start()
        for region in range(4):
            scratch_rows(region).wait()

    @pl.when(jnp.logical_and(i >= 3, i < nv + 3))
    def _():
        pltpu.make_async_copy(ybuf0, o_hbm.at[pl.ds(0, TM_MOE)], ssem.at[i % 3]).wait()

    def weight_copies(k, go):
        e = elist_ref[k]
        slot = k % WEIGHT_SLOTS
        for w_hbm, st in ((wg_hbm, stage_g), (wu_hbm, stage_u), (wd_hbm, stage_d)):
            d = pltpu.make_async_copy(w_hbm.at[e], st.at[slot], wsem.at[slot])
            d.start() if go else d.wait()

    @pl.when(i == 0)
    def _():
        for k in range(WEIGHT_SLOTS):
            @pl.when(k < nexp_ref[0])
            def _(k=k):
                weight_copies(k, True)

    @pl.when(i < nv)
    def _():
        prev_e = tile_e_ref[jnp.maximum(i - 1, 0)]

        @pl.when(jnp.logical_or(i == 0, tile_e_ref[i] != prev_e))
        def _():
            k = krank_ref[tile_e_ref[i]]
            slot = k % WEIGHT_SLOTS
            weight_copies(k, False)
            wg[...] = stage_g[slot].astype(BF16)
            wu[...] = stage_u[slot].astype(BF16)
            wd[...] = stage_d[slot].astype(BF16)

            @pl.when(k + WEIGHT_SLOTS < nexp_ref[0])
            def _():
                weight_copies(k + WEIGHT_SLOTS, True)

    @pl.when(i == 0)
    def _():
        compute(0)

    for slot in range(3):
        prev = (slot + 2) % 3

        @pl.when(jnp.logical_and(i % 3 == slot, jnp.logical_and(i >= 1, i < nv)))
        def _(slot=slot, prev=prev):
            start_writes(i - 1, prev)
            compute(slot)

        @pl.when(jnp.logical_and(i % 3 == slot, i == nv))
        def _(prev=prev):
            start_writes(i - 1, prev)


def _moe(x_disp, plan, w_gate, w_up, w_down, n_tiles, n_tok):
    tile_e, tile_q0, n_valid, pair_start, counts, order = plan[3:9]
    plane = n_tok + 2 * TM_MOE
    orow = (order & 1) * plane + (order >> 1)
    present = (counts > 0).astype(I32)
    krank = jnp.cumsum(present) - present
    experts = jnp.arange(N_EXPERTS, dtype=I32)
    elist = jnp.sum(jnp.where((krank[None, :] == experts[:, None]) & (present[None, :] > 0), experts[None, :], 0),
                    axis=1).astype(I32)
    nexp = jnp.sum(present).astype(I32).reshape(1)
    tile = lambda i, nv: jnp.minimum(i, jnp.maximum(nv[0] - 1, 0))
    any_spec = pl.BlockSpec(memory_space=pl.ANY)
    ybuf = pltpu.VMEM((TM_MOE, HALF), U32)
    return pl.pallas_call(
        functools.partial(_moe_body, n_tok, plane),
        grid_spec=pltpu.PrefetchScalarGridSpec(
            num_scalar_prefetch=9,
            grid=(n_tiles + 3,),
            in_specs=[pl.BlockSpec((TM_MOE, HALF), lambda i, te, tq, nv, *_: (tile(i, nv), 0)),
                      any_spec, any_spec, any_spec],
            out_specs=pl.BlockSpec(memory_space=pl.ANY),
            scratch_shapes=[ybuf, ybuf, ybuf,
                            pltpu.SemaphoreType.DMA((3,)),
                            pltpu.VMEM((WEIGHT_SLOTS, D_MODEL, D_EXPERT), F32),
                            pltpu.VMEM((WEIGHT_SLOTS, D_MODEL, D_EXPERT), F32),
                            pltpu.VMEM((WEIGHT_SLOTS, D_EXPERT, D_MODEL), F32),
                            pltpu.SemaphoreType.DMA((WEIGHT_SLOTS,)),
                            pltpu.VMEM((D_MODEL, D_EXPERT), BF16),
                            pltpu.VMEM((D_MODEL, D_EXPERT), BF16),
                            pltpu.VMEM((D_EXPERT, D_MODEL), BF16)]),
        out_shape=jax.ShapeDtypeStruct((2 * plane, HALF), U32),
        compiler_params=pltpu.CompilerParams(dimension_semantics=("arbitrary",), vmem_limit_bytes=VMEM_LIMIT),
        name="moe_experts",
    )(tile_e, tile_q0, n_valid, pair_start, counts, orow, krank.astype(I32), elist, nexp,
      x_disp, w_gate, w_up, w_down)


def _combine_body(n_p_tiles, h2_ref, rec_ref, gf_ref, y1_ref, y2_ref, outp_ref, outs_ref):
    i = pl.program_id(0)
    rec = rec_ref[...]
    y1 = jnp.concatenate(_unpack_halves(y1_ref[...]), axis=-1)
    y2 = jnp.concatenate(_unpack_halves(y2_ref[...]), axis=-1)
    h = rec[:, 2:3] * y1 + rec[:, 3:4] * y2
    out = _rmsnorm(h2_ref[...] + h, gf_ref[...])

    @pl.when(i < n_p_tiles)
    def _():
        outp_ref[...] = out

    @pl.when(i >= n_p_tiles)
    def _():
        outs_ref[...] = out


def _combine(h2, rec, y_pairs, final_norm, t_p, t_s):
    n_p, n_s = t_p // TM, t_s // TM
    plane_tiles = y_pairs.shape[0] // 2 // TM
    return pl.pallas_call(
        functools.partial(_combine_body, n_p),
        grid=(n_p + n_s,),
        in_specs=[pl.BlockSpec((TM, D_MODEL), lambda i: (i, 0)),
                  pl.BlockSpec((TM, ROUTE_W), lambda i: (i, 0)),
                  pl.BlockSpec((1, D_MODEL), lambda i: (0, 0)),
                  pl.BlockSpec((TM, HALF), lambda i: (i, 0)),
                  pl.BlockSpec((TM, HALF), lambda i: (i + plane_tiles, 0))],
        out_specs=[pl.BlockSpec((TM, D_MODEL), lambda i: (jnp.minimum(i, n_p - 1), 0)),
                   pl.BlockSpec((TM, D_MODEL), lambda i: (jnp.clip(i - n_p, 0, n_s - 1), 0))],
        out_shape=[jax.ShapeDtypeStruct((t_p, D_MODEL), F32),
                   jax.ShapeDtypeStruct((t_s, D_MODEL), F32)],
        compiler_params=pltpu.CompilerParams(dimension_semantics=("arbitrary",), vmem_limit_bytes=VMEM_LIMIT),
        name="combine_norm",
    )(h2, rec, final_norm, y_pairs, y_pairs)


def kernel(x_prompt, x_sample, state_conv, state_ssm_re, state_ssm_im, meta_tokens, norm1, w_in, conv_w,
           lam_re, lam_im, log_dt, ssm_b_re, ssm_b_im, ssm_c_re, ssm_c_im, ssm_d, w_glu, w_conv_out,
           w_ssm_out, w_o, norm2, w_coarse, w_fine, w_gate, w_up, w_down, final_norm):
    n_pb, seq, _ = x_prompt.shape
    n_sb, dec_seq, _ = x_sample.shape
    assert dec_seq == CHUNK and seq % TM == 0 and (n_sb * dec_seq) % TM == 0 and N_META == CHUNK
    t_p, t_s = n_pb * seq, n_sb * dec_seq
    xp = x_prompt.reshape(t_p, D_MODEL)
    xsm = x_sample.reshape(t_s, D_MODEL)

    xb, z, xs, zmeta, xsmeta, sga, sgb = _in_proj(xp, xsm, meta_tokens, norm1, w_in[0])

    mats = _s5_chunk_mats(lam_re[0], lam_im[0], log_dt[0], ssm_b_re[0], ssm_b_im[0], ssm_c_re[0], ssm_c_im[0])
    y5, pf_re, pf_im, sf_re, sf_im = _s5(xs, xsmeta, state_ssm_re[0], state_ssm_im[0], mats, n_pb, n_sb, seq)

    buf = state_conv[0]
    zero = jnp.zeros((n_sb, dec_seq, D_CONV), F32)
    inj1 = zero.at[:, 0].set(buf[:, 1]).reshape(t_s, D_CONV)
    inj2 = zero.at[:, 0].set(buf[:, 0]).at[:, 1].set(buf[:, 1]).reshape(t_s, D_CONV)
    merged = _mixers(xb, z, y5, xs, sga, sgb, zmeta, inj1, inj2, conv_w[0], ssm_d, w_conv_out[0], w_glu[0],
                     w_ssm_out[0], t_p // TM, seq // TM)

    w_router = jnp.concatenate(
        [w_fine[0], w_coarse[0], jnp.zeros((D_MODEL, ROUTE_W - N_EXPERTS - N_EGROUPS), F32)], axis=1)
    h2, v, rec, cnt = _out_proj(merged, xp, xsm, norm2, w_router, w_o[0])

    n_tiles = 2 * (t_p + t_s) // TM_MOE + N_EXPERTS
    plan = _moe_plan(rec, cnt, n_tiles)
    x_disp = _dispatch(v, plan, n_tiles)
    y_pairs = _moe(x_disp, plan, w_gate[0], w_up[0], w_down[0], n_tiles, t_p + t_s)
    y_p, y_s = _combine(h2, rec, y_pairs, final_norm.reshape(1, D_MODEL), t_p, t_s)

    new_conv_p = jnp.stack([z[(b + 1) * seq - 2:(b + 1) * seq] for b in range(n_pb)])
    new_conv_s = z[t_p:].reshape(n_sb, dec_seq, D_CONV)[:, dec_seq - 2:]
    return (y_p.reshape(n_pb, seq, D_MODEL), y_s.reshape(n_sb, dec_seq, D_MODEL),
            new_conv_p[None], pf_re, pf_im, new_conv_s[None], sf_re, sf_im)
```

```python
import functools

import jax
import jax.numpy as jnp
from jax import lax
from jax.experimental import pallas as pl
from jax.experimental.pallas import tpu as pltpu

F32 = jnp.float32
BF16 = jnp.bfloat16
I32 = jnp.int32
U32 = jnp.uint32

D_MODEL = 2048
D_CONV = 1024
D_SSM = 1024
SSM_H = 16
SSM_G = 64
SSM_P = 64
N_META = 16
N_EGROUPS = 4
EXPERTS_PER_GROUP = 8
N_EXPERTS = 32
D_EXPERT = 256
EPS = 1e-6

LANES = 128
SUBLANES = 8

CHUNK = 16
CHUNK_W = CHUNK * SSM_H

TM = 256
TM_MOE = 256
VMEM_LIMIT = 52 * 1024 * 1024


def _rmsnorm(x, g):
    return x * lax.rsqrt(jnp.mean(x * x, axis=-1, keepdims=True) + EPS) * g


def _sigmoid(x):
    return 1.0 / (1.0 + jnp.exp(-x))


def _gelu_tanh(x):
    c = 0.7978845608028654
    return 0.5 * x * (1.0 + jnp.tanh(c * (x + 0.044715 * (x * x * x))))


HALF = D_MODEL // 2


def _pack_halves(a):
    return pltpu.pack_elementwise([a[:, :HALF], a[:, HALF:]], packed_dtype=BF16)


def _unpack_halves(p):
    return (pltpu.unpack_elementwise(p, index=0, packed_dtype=BF16, unpacked_dtype=F32),
            pltpu.unpack_elementwise(p, index=1, packed_dtype=BF16, unpacked_dtype=F32))


def _weight_copy(w_hbm, stage, sem, c, slot, rows, col0, ncols):
    return pltpu.make_async_copy(
        w_hbm.at[pl.ds(c * rows, rows), pl.ds(col0, ncols)], stage.at[slot], sem.at[slot])


def _load_weight_bf16(w_hbm, w_vmem, stage, sem, col0=0):
    k, n = w_vmem.shape
    rows = stage.shape[1]
    nchunk = k // rows
    _weight_copy(w_hbm, stage, sem, 0, 0, rows, col0, n).start()
    for c in range(nchunk):
        slot = c % 2
        if c + 1 < nchunk:
            _weight_copy(w_hbm, stage, sem, c + 1, 1 - slot, rows, col0, n).start()
        _weight_copy(w_hbm, stage, sem, c, slot, rows, col0, n).wait()
        w_vmem[pl.ds(c * rows, rows), :] = stage[slot].astype(BF16)


def _in_proj_mix_body(n_p_tiles, xp_ref, xsm_ref, meta_ref, g_ref, w_hbm,
                      xb_out, z_out, xs_out, zmeta_out, xsmeta_out,
                      w_vmem, stage, sem):
    i = pl.program_id(0)
    g = g_ref[...]

    def project(u):
        xb = jnp.dot(u, w_vmem[:, 0:D_CONV], preferred_element_type=F32)
        xc = jnp.dot(u, w_vmem[:, D_CONV:2 * D_CONV], preferred_element_type=F32)
        xv = jnp.dot(u, w_vmem[:, 2 * D_CONV:3 * D_CONV], preferred_element_type=F32)
        xs = jnp.dot(u, w_vmem[:, 3 * D_CONV:3 * D_CONV + D_SSM], preferred_element_type=F32)
        return xb, xc * xv, xs

    @pl.when(i == 0)
    def _():
        _load_weight_bf16(w_hbm, w_vmem, stage, sem, col0=0)
        um = _rmsnorm(meta_ref[...], g).astype(BF16)
        _, zm, xsm = project(um)
        zmeta_out[...] = zm
        xsmeta_out[...] = xsm

    x = jnp.where(i < n_p_tiles, xp_ref[...], xsm_ref[...])
    u = _rmsnorm(x, g).astype(BF16)
    xb, z, xs = project(u)
    xb_out[...] = xb.astype(BF16)
    z_out[...] = z
    xs_out[...] = xs


def _in_proj_gate_body(n_p_tiles, xp_ref, xsm_ref, g_ref, w_hbm, ga_out, gb_out,
                       w_vmem, stage, sem):
    i = pl.program_id(0)

    @pl.when(i == 0)
    def _():
        _load_weight_bf16(w_hbm, w_vmem, stage, sem, col0=3 * D_CONV + D_SSM)

    x = jnp.where(i < n_p_tiles, xp_ref[...], xsm_ref[...])
    u = _rmsnorm(x, g_ref[...]).astype(BF16)
    ga = jnp.dot(u, w_vmem[:, 0:D_MODEL], preferred_element_type=F32)
    ga_out[...] = _sigmoid(ga).astype(BF16)
    gb = jnp.dot(u, w_vmem[:, D_MODEL:2 * D_MODEL], preferred_element_type=F32)
    gb_out[...] = _sigmoid(gb).astype(BF16)


def _two_stream_specs(n_p_tiles, n_s_tiles):
    xp_spec = pl.BlockSpec((TM, D_MODEL), lambda i: (jnp.minimum(i, n_p_tiles - 1), 0))
    xs_spec = pl.BlockSpec((TM, D_MODEL), lambda i: (jnp.clip(i - n_p_tiles, 0, n_s_tiles - 1), 0))
    return xp_spec, xs_spec


def _in_proj(xp, xsm, meta, norm1, w_in):
    t_p, t_s = xp.shape[0], xsm.shape[0]
    n_p, n_s = t_p // TM, t_s // TM
    t = t_p + t_s
    half = 3 * D_CONV + D_SSM
    xp_spec, xs_spec = _two_stream_specs(n_p, n_s)
    g_spec = pl.BlockSpec((1, D_MODEL), lambda i: (0, 0))
    any_spec = pl.BlockSpec(memory_space=pl.ANY)
    stage_rows = 128
    row = lambda w: pl.BlockSpec((TM, w), lambda i: (i, 0))
    const = lambda r, w: pl.BlockSpec((r, w), lambda i: (0, 0))
    params = pltpu.CompilerParams(dimension_semantics=("arbitrary",), vmem_limit_bytes=VMEM_LIMIT)
    scratch = [pltpu.VMEM((D_MODEL, half), BF16),
               pltpu.VMEM((2, stage_rows, half), F32),
               pltpu.SemaphoreType.DMA((2,))]

    xb, z, xs, zmeta, xsmeta = pl.pallas_call(
        functools.partial(_in_proj_mix_body, n_p),
        grid=(n_p + n_s,),
        in_specs=[xp_spec, xs_spec, const(N_META, D_MODEL), g_spec, any_spec],
        out_specs=[row(D_CONV), row(D_CONV), row(D_SSM), const(N_META, D_CONV), const(N_META, D_SSM)],
        out_shape=[jax.ShapeDtypeStruct((t, D_CONV), BF16),
                   jax.ShapeDtypeStruct((t, D_CONV), F32),
                   jax.ShapeDtypeStruct((t, D_SSM), F32),
                   jax.ShapeDtypeStruct((N_META, D_CONV), F32),
                   jax.ShapeDtypeStruct((N_META, D_SSM), F32)],
        scratch_shapes=scratch,
        compiler_params=params,
        name="in_proj_mix",
    )(xp, xsm, meta, norm1, w_in)

    sga, sgb = pl.pallas_call(
        functools.partial(_in_proj_gate_body, n_p),
        grid=(n_p + n_s,),
        in_specs=[xp_spec, xs_spec, g_spec, any_spec],
        out_specs=[row(D_MODEL), row(D_MODEL)],
        out_shape=[jax.ShapeDtypeStruct((t, D_MODEL), BF16),
                   jax.ShapeDtypeStruct((t, D_MODEL), BF16)],
        scratch_shapes=scratch,
        compiler_params=params,
        name="in_proj_gate",
    )(xp, xsm, norm1, w_in)
    return xb, z, xs, zmeta, xsmeta, sga, sgb


S5_GROUPS_PER_STEP = LANES // SSM_H
S5_PAIRS_PER_STEP = S5_GROUPS_PER_STEP // 2


def _s5_chunk_mats(lam_re, lam_im, log_dt, b_re, b_im, c_re, c_im):
    dt = jnp.exp(log_dt)[:, None]
    lr, li = lam_re, lam_im
    z = jnp.stack([lr * dt, li * dt], axis=1)
    mag = jnp.exp(lr * dt)
    ab_re, ab_im = mag * jnp.cos(li * dt), mag * jnp.sin(li * dt)
    nr, ni = ab_re - 1.0, ab_im
    den = lr * lr + li * li
    k_re = (nr * lr + ni * li) / den
    k_im = (ni * lr - nr * li) / den
    bb = jnp.stack([k_re[..., None] * b_re - k_im[..., None] * b_im,
                    k_re[..., None] * b_im + k_im[..., None] * b_re], axis=1)
    zcol = jnp.transpose(z, (0, 2, 1))
    return z, zcol, bb, jnp.stack([c_re, c_im], axis=1)


def _cmul(ar, ai, br, bi):
    return ar * br - ai * bi, ar * bi + ai * br


def _s5_operators(k, z_ref, zcol_ref, bb_ref, c_ref, toe_scr, p_scr, q_scr):
    hi = lax.Precision.HIGHEST
    zr, zi = z_ref[k, 0:1, :], z_ref[k, 1:2, :]
    mag = jnp.exp(zr)
    ar, ai = mag * jnp.cos(zi), mag * jnp.sin(zi)
    zc = zcol_ref[k]
    magc = jnp.exp(zc[:, 0:1])
    acr, aci = magc * jnp.cos(zc[:, 1:2]), magc * jnp.sin(zc[:, 1:2])
    c_re, c_im = c_ref[k, 0], c_ref[k, 1]
    bb_re, bb_im = bb_ref[k, 0], bb_ref[k, 1]

    pr, pi = [jnp.ones_like(ar)], [jnp.zeros_like(ar)]
    pcr, pci = [jnp.ones_like(acr)], [jnp.zeros_like(acr)]
    for _ in range(CHUNK):
        nr, ni = _cmul(pr[-1], pi[-1], ar, ai)
        pr.append(nr)
        pi.append(ni)
        ncr, nci = _cmul(pcr[-1], pci[-1], acr, aci)
        pcr.append(ncr)
        pci.append(nci)

    g_re = jnp.concatenate([c_re * pr[d] - c_im * pi[d] for d in range(CHUNK + 1)], axis=0)
    g_im = jnp.concatenate([c_re * pi[d] + c_im * pr[d] for d in range(CHUNK + 1)], axis=0)
    half = (k % 2) * SSM_P
    q_scr[k] = jnp.zeros(q_scr.shape[1:], BF16)
    q_scr[k, 0, :, half:half + SSM_P] = g_re[SSM_H:].astype(BF16)
    q_scr[k, 1, :, half:half + SSM_P] = (-g_im[SSM_H:]).astype(BF16)

    m_all = (jnp.dot(g_re[:CHUNK_W], bb_re, preferred_element_type=F32, precision=hi)
             - jnp.dot(g_im[:CHUNK_W], bb_im, preferred_element_type=F32, precision=hi))
    rep = jnp.where(lax.broadcasted_iota(I32, (SSM_H, CHUNK_W), 0)
                    == (lax.broadcasted_iota(I32, (SSM_H, CHUNK_W), 1) & (SSM_H - 1)), 1.0, 0.0)
    m_wide = jnp.dot(m_all, rep, preferred_element_type=F32, precision=hi)
    blk = lax.broadcasted_iota(I32, (1, CHUNK_W), 1) >> 4
    toe = jnp.where(blk == 0, m_wide, 0.0)
    for j in range(1, CHUNK):
        shifted = jnp.concatenate([jnp.zeros((j * SSM_H, CHUNK_W), F32), m_wide[:CHUNK_W - j * SSM_H]], axis=0)
        toe = jnp.where(blk == j, shifted, toe)
    toe_scr[k] = toe.astype(BF16)

    bw_re = jnp.dot(bb_re, rep, preferred_element_type=F32, precision=hi)
    bw_im = jnp.dot(bb_im, rep, preferred_element_type=F32, precision=hi)
    pw_re = jnp.zeros((SSM_P, CHUNK_W), F32)
    pw_im = jnp.zeros((SSM_P, CHUNK_W), F32)
    for j in range(CHUNK):
        pw_re = jnp.where(blk == j, pcr[CHUNK - 1 - j], pw_re)
        pw_im = jnp.where(blk == j, pci[CHUNK - 1 - j], pw_im)
    p_re, p_im = _cmul(pw_re, pw_im, bw_re, bw_im)
    p_scr[k] = jnp.concatenate([p_re, p_im], axis=0).astype(BF16)
    return pr[CHUNK], pi[CHUNK]


def _s5_body(n_pc, n_pb, n_sb, xs_ref, xsmeta_ref, z_ref, zcol_ref, bb_ref, c_ref, s0re_ref, s0im_ref,
             y_out, pfre_out, pfim_out, sfre_out, sfim_out,
             xt_scr, u_scr, st_re, st_im, sl_re, sl_im, sp_re, sp_im, yt_scr, toe_ref, p_ref, q_ref):
    gb = S5_GROUPS_PER_STEP
    n_p_rows = n_pb * n_pc
    row_s = n_p_rows
    row_m = row_s + n_sb
    t_p = n_p_rows * CHUNK
    rows_pad = xt_scr.shape[0]

    a16 = [_s5_operators(k, z_ref, zcol_ref, bb_ref, c_ref, toe_ref, p_ref, q_ref) for k in range(gb)]

    xt_scr[row_m + 1:rows_pad, :] = jnp.zeros((rows_pad - row_m - 1, LANES), F32)
    for t in range(CHUNK):
        xt_scr[0:n_p_rows, :] = xs_ref[pl.ds(t, n_p_rows, stride=CHUNK), :]
        xt_scr[row_s:row_m, :] = xs_ref[pl.ds(t_p + t, n_sb, stride=CHUNK), :]
        xt_scr[row_m:row_m + 1, :] = xsmeta_ref[t:t + 1, :]
        xt = xt_scr[...].T.astype(BF16)
        for k in range(gb):
            u_scr[k, t * SSM_H:(t + 1) * SSM_H, :] = xt[k * SSM_H:(k + 1) * SSM_H, :]

    for k in range(gb):
        sl = jnp.dot(p_ref[k], u_scr[k], preferred_element_type=F32)
        half = (k % 2) * SSM_P
        st_re[k // 2, half:half + SSM_P, :] = sl[0:SSM_P, :]
        st_im[k // 2, half:half + SSM_P, :] = sl[SSM_P:2 * SSM_P, :]
    npair = S5_PAIRS_PER_STEP
    for j in range(npair):
        sl_re[j] = st_re[j].T
        sl_im[j] = st_im[j].T
        sp_re[j, row_m:rows_pad, :] = jnp.zeros((rows_pad - row_m, 2 * SSM_P), F32)
        sp_im[j, row_m:rows_pad, :] = jnp.zeros((rows_pad - row_m, 2 * SSM_P), F32)

    ar = [jnp.concatenate([a16[2 * j][0], a16[2 * j + 1][0]], axis=1) for j in range(npair)]
    ai = [jnp.concatenate([a16[2 * j][1], a16[2 * j + 1][1]], axis=1) for j in range(npair)]
    sre = [jnp.broadcast_to(sl_re[j, row_m:row_m + 1, :], (n_pb, 2 * SSM_P)) for j in range(npair)]
    sim = [jnp.broadcast_to(sl_im[j, row_m:row_m + 1, :], (n_pb, 2 * SSM_P)) for j in range(npair)]
    for c in range(n_pc):
        rows = pl.ds(c, n_pb, stride=n_pc)
        for j in range(npair):
            sp_re[j, rows, :] = sre[j]
            sp_im[j, rows, :] = sim[j]
            nre = ar[j] * sre[j] - ai[j] * sim[j] + sl_re[j, rows, :]
            nim = ar[j] * sim[j] + ai[j] * sre[j] + sl_im[j, rows, :]
            sre[j], sim[j] = nre, nim
    for j in range(npair):
        pfre_out[j] = sre[j]
        pfim_out[j] = sim[j]
        s0r, s0i = s0re_ref[j], s0im_ref[j]
        sp_re[j, row_s:row_m, :] = s0r
        sp_im[j, row_s:row_m, :] = s0i
        sfre_out[j] = ar[j] * s0r - ai[j] * s0i + sl_re[j, row_s:row_m, :]
        sfim_out[j] = ar[j] * s0i + ai[j] * s0r + sl_im[j, row_s:row_m, :]

    nt = (((1,), (1,)), ((), ()))
    for k in range(gb):
        y = jnp.dot(toe_ref[k], u_scr[k], preferred_element_type=F32)
        y += lax.dot_general(q_ref[k, 0], sp_re[k // 2].astype(BF16), nt, preferred_element_type=F32)
        y += lax.dot_general(q_ref[k, 1], sp_im[k // 2].astype(BF16), nt, preferred_element_type=F32)
        for t in range(CHUNK):
            yt_scr[t, k * SSM_H:(k + 1) * SSM_H, :] = y[t * SSM_H:(t + 1) * SSM_H, :]
    for t in range(CHUNK):
        yt = yt_scr[t].T
        y_out[pl.ds(t, n_p_rows, stride=CHUNK), :] = yt[0:n_p_rows, :]
        y_out[pl.ds(t_p + t, n_sb, stride=CHUNK), :] = yt[row_s:row_m, :]


def _s5(xs, xsmeta, state_re, state_im, mats, n_pb, n_sb, seq):
    z, zcol, bb, c = mats
    t = xs.shape[0]
    n_pc = seq // CHUNK
    rows = n_pc * n_pb + n_sb + 1
    rows_pad = -(-rows // LANES) * LANES
    gb, npair = S5_GROUPS_PER_STEP, S5_PAIRS_PER_STEP
    pairs = lambda s: jnp.transpose(s.reshape(n_sb, SSM_G // 2, 2 * SSM_P), (1, 0, 2))
    blk3 = lambda n, r, c: pl.BlockSpec((n, r, c), lambda i: (i, 0, 0))
    y, pfre, pfim, sfre, sfim = pl.pallas_call(
        functools.partial(_s5_body, n_pc, n_pb, n_sb),
        grid=(SSM_G // gb,),
        in_specs=[pl.BlockSpec((t, LANES), lambda i: (0, i)),
                  pl.BlockSpec((N_META, LANES), lambda i: (0, i)),
                  blk3(gb, 2, SSM_P), blk3(gb, SSM_P, 2),
                  pl.BlockSpec((gb, 2, SSM_P, SSM_H), lambda i: (i, 0, 0, 0)),
                  pl.BlockSpec((gb, 2, SSM_H, SSM_P), lambda i: (i, 0, 0, 0)),
                  blk3(npair, n_sb, 2 * SSM_P), blk3(npair, n_sb, 2 * SSM_P)],
        out_specs=[pl.BlockSpec((t, LANES), lambda i: (0, i)),
                   blk3(npair, n_pb, 2 * SSM_P), blk3(npair, n_pb, 2 * SSM_P),
                   blk3(npair, n_sb, 2 * SSM_P), blk3(npair, n_sb, 2 * SSM_P)],
        out_shape=[jax.ShapeDtypeStruct((t, D_SSM), F32),
                   jax.ShapeDtypeStruct((SSM_G // 2, n_pb, 2 * SSM_P), F32),
                   jax.ShapeDtypeStruct((SSM_G // 2, n_pb, 2 * SSM_P), F32),
                   jax.ShapeDtypeStruct((SSM_G // 2, n_sb, 2 * SSM_P), F32),
                   jax.ShapeDtypeStruct((SSM_G // 2, n_sb, 2 * SSM_P), F32)],
        scratch_shapes=[pltpu.VMEM((rows_pad, LANES), F32),
                        pltpu.VMEM((gb, CHUNK_W, rows_pad), BF16),
                        pltpu.VMEM((npair, 2 * SSM_P, rows_pad), F32),
                        pltpu.VMEM((npair, 2 * SSM_P, rows_pad), F32),
                        pltpu.VMEM((npair, rows_pad, 2 * SSM_P), F32),
                        pltpu.VMEM((npair, rows_pad, 2 * SSM_P), F32),
                        pltpu.VMEM((npair, rows_pad, 2 * SSM_P), F32),
                        pltpu.VMEM((npair, rows_pad, 2 * SSM_P), F32),
                        pltpu.VMEM((CHUNK, LANES, rows_pad), F32),
                        pltpu.VMEM((gb, CHUNK_W, CHUNK_W), BF16),
                        pltpu.VMEM((gb, 2 * SSM_P, CHUNK_W), BF16),
                        pltpu.VMEM((gb, 2, CHUNK_W, 2 * SSM_P), BF16)],
        compiler_params=pltpu.CompilerParams(dimension_semantics=("arbitrary",), vmem_limit_bytes=VMEM_LIMIT),
        name="s5_chunks",
    )(xs, xsmeta, z, zcol, bb, c, pairs(state_re), pairs(state_im))
    unpair = lambda a: jnp.transpose(a, (1, 0, 2)).reshape(a.shape[1], SSM_G, SSM_P)[None]
    return y, unpair(pfre), unpair(pfim), unpair(sfre), unpair(sfim)


def _mixers_body(n_p_tiles, tiles_per_seq, xb_ref, z_ref, y5_ref, xs_ref, sga_ref, sgb_ref,
                 zmeta_ref, inj1_ref, inj2_ref, cw_ref, dskip_ref,
                 wc_hbm, wg_hbm, wso_hbm, merged_out,
                 wc, wg, wso, stage_a, stage_b, sem, carry):
    i = pl.program_id(0)

    @pl.when(i == 0)
    def _():
        _load_weight_bf16(wc_hbm, wc, stage_a, sem)
        _load_weight_bf16(wg_hbm, wg, stage_b, sem)
        _load_weight_bf16(wso_hbm, wso, stage_a, sem)

    @pl.when(jnp.logical_and(i < n_p_tiles, i % tiles_per_seq == 0))
    def _():
        carry[0:2, :] = zmeta_ref[N_META - 2:N_META, :]

    z = z_ref[...]
    row = lax.broadcasted_iota(I32, (TM, 1), 0)
    is_s = i >= n_p_tiles
    r1 = pltpu.roll(z, 1, 0)
    r2 = pltpu.roll(z, 2, 0)
    c1 = carry[1:2, :]
    c2 = carry[0:1, :]
    pos = jnp.where(is_s, row & (CHUNK - 1), row)
    first1 = pos == 0
    first2 = pos < 2
    fill1 = jnp.where(is_s, inj1_ref[...], jnp.broadcast_to(c1, z.shape))
    fill2 = jnp.where(is_s, inj2_ref[...], jnp.where(row == 0, c2, c1))
    zp1 = jnp.where(first1, fill1, r1)
    zp2 = jnp.where(first2, fill2, r2)
    carry[0:2, :] = z[TM - 2:TM, :]

    cw = cw_ref[...]
    conv = cw[0:1, :] * zp2 + cw[1:2, :] * zp1 + cw[2:3, :] * z
    a_in = (xb_ref[...].astype(F32) * conv).astype(BF16)
    ya = jnp.dot(a_in, wc[...], preferred_element_type=F32)

    ys = y5_ref[...] + dskip_ref[...] * xs_ref[...]
    ys = _gelu_tanh(ys)
    glu = jnp.dot(ys.astype(BF16), wg[...], preferred_element_type=F32)
    ys = ys * _sigmoid(glu)
    yb = jnp.dot(ys.astype(BF16), wso[...], preferred_element_type=F32)

    merged = sga_ref[...].astype(F32) * ya + sgb_ref[...].astype(F32) * yb
    merged_out[...] = merged.astype(BF16)


def _mixers(xb, z, y5, xs, sga, sgb, zmeta, inj1, inj2, conv_w, d_skip, w_conv_out, w_glu, w_ssm_out,
            n_p_tiles, tiles_per_seq):
    t = xb.shape[0]
    n_s_tiles = inj1.shape[0] // TM
    row = lambda w: pl.BlockSpec((TM, w), lambda i: (i, 0))
    const = lambda r, w: pl.BlockSpec((r, w), lambda i: (0, 0))
    inj = pl.BlockSpec((TM, D_CONV), lambda i: (jnp.clip(i - n_p_tiles, 0, n_s_tiles - 1), 0))
    any_spec = pl.BlockSpec(memory_space=pl.ANY)
    return pl.pallas_call(
        functools.partial(_mixers_body, n_p_tiles, tiles_per_seq),
        grid=(t // TM,),
        in_specs=[row(D_CONV), row(D_CONV), row(D_SSM), row(D_SSM), row(D_MODEL), row(D_MODEL),
                  const(N_META, D_CONV), inj, inj, const(3, D_CONV), const(1, D_SSM),
                  any_spec, any_spec, any_spec],
        out_specs=row(D_MODEL),
        out_shape=jax.ShapeDtypeStruct((t, D_MODEL), BF16),
        scratch_shapes=[pltpu.VMEM((D_CONV, D_MODEL), BF16),
                        pltpu.VMEM((D_SSM, D_SSM), BF16),
                        pltpu.VMEM((D_SSM, D_MODEL), BF16),
                        pltpu.VMEM((2, 256, D_MODEL), F32),
                        pltpu.VMEM((2, 256, D_SSM), F32),
                        pltpu.SemaphoreType.DMA((2,)),
                        pltpu.VMEM((8, D_CONV), F32)],
        compiler_params=pltpu.CompilerParams(dimension_semantics=("arbitrary",), vmem_limit_bytes=VMEM_LIMIT),
        name="mixers",
    )(xb, z, y5, xs, sga, sgb, zmeta, inj1, inj2, conv_w, d_skip, w_conv_out, w_glu, w_ssm_out)


ROUTE_W = LANES
COARSE0 = N_EXPERTS


def _route(logits, cnt):
    col = lax.broadcasted_iota(I32, logits.shape, 1)
    colf = col.astype(F32)
    neg = jnp.float32(-jnp.inf)
    big = jnp.float32(1 << 20)
    is_c = jnp.logical_and(col >= COARSE0, col < COARSE0 + N_EGROUPS)
    lc = jnp.where(is_c, logits, neg)
    cmax = jnp.max(lc, axis=-1, keepdims=True)
    gi = jnp.min(jnp.where(lc == cmax, colf - COARSE0, big), axis=-1, keepdims=True)
    pg = 1.0 / jnp.sum(jnp.where(is_c, jnp.exp(lc - cmax), 0.0), axis=-1, keepdims=True)
    grp = (col >> 3).astype(F32)
    in_g = jnp.logical_and(col < N_EXPERTS, grp == gi)
    lf = jnp.where(in_g, logits, neg)
    m1 = jnp.max(lf, axis=-1, keepdims=True)
    i1 = jnp.min(jnp.where(lf == m1, colf, big), axis=-1, keepdims=True)
    lf2 = jnp.where(colf == i1, neg, lf)
    m2 = jnp.max(lf2, axis=-1, keepdims=True)
    i2 = jnp.min(jnp.where(lf2 == m2, colf, big), axis=-1, keepdims=True)
    e2 = jnp.exp(m2 - m1)
    w1 = pg / (1.0 + e2)
    w2 = pg * e2 / (1.0 + e2)
    n = logits.shape[0]
    hit1 = colf == i1
    hit2 = colf == i2
    onehot = jnp.where(jnp.logical_or(hit1, hit2), 1.0, 0.0)
    rr = lax.broadcasted_iota(I32, (n, n), 0)
    cc = lax.broadcasted_iota(I32, (n, n), 1)
    tri = jnp.where(cc < rr, 1.0, 0.0).astype(BF16)
    pos = jnp.dot(tri, onehot.astype(BF16), preferred_element_type=F32) + cnt
    rank1 = jnp.sum(jnp.where(hit1, pos, 0.0), axis=-1, keepdims=True)
    rank2 = jnp.sum(jnp.where(hit2, pos, 0.0), axis=-1, keepdims=True)
    vals = (i1, i2, w1, w2, rank1, rank2)
    rec = jnp.zeros(logits.shape, F32)
    for c, val in enumerate(vals):
        rec = jnp.where(col == c, val, rec)
    return rec, cnt + jnp.sum(onehot, axis=0, keepdims=True)


def _split_bf16(a):
    hi = a.astype(BF16)
    lo = (a - hi.astype(F32)).astype(BF16)
    return hi, lo


def _out_proj_body(n_p_tiles, merged_ref, xp_ref, xsm_ref, g2_ref, wr_ref, wo_hbm,
                   h2_out, v_out, rec_out, cnt_out, wo, stage, sem):
    i = pl.program_id(0)

    @pl.when(i == 0)
    def _():
        _load_weight_bf16(wo_hbm, wo, stage, sem)
        cnt_out[...] = jnp.zeros(cnt_out.shape, F32)

    x = jnp.where(i < n_p_tiles, xp_ref[...], xsm_ref[...])
    h2 = x + jnp.dot(merged_ref[...], wo[...], preferred_element_type=F32)
    h2_out[...] = h2
    v = _rmsnorm(h2, g2_ref[...])
    v_out[...] = _pack_halves(v)
    v_hi, v_lo = _split_bf16(v)
    w_hi, w_lo = _split_bf16(wr_ref[...])
    logits = (jnp.dot(v_hi, w_hi, preferred_element_type=F32)
              + jnp.dot(v_lo, w_hi, preferred_element_type=F32)
              + jnp.dot(v_hi, w_lo, preferred_element_type=F32))
    rec, cnt = _route(logits, cnt_out[...])
    rec_out[...] = rec
    cnt_out[...] = cnt


def _out_proj(merged, xp, xsm, norm2, w_router, w_o):
    t_p, t_s = xp.shape[0], xsm.shape[0]
    n_p, n_s = t_p // TM, t_s // TM
    t = t_p + t_s
    xp_spec, xs_spec = _two_stream_specs(n_p, n_s)
    row = lambda w: pl.BlockSpec((TM, w), lambda i: (i, 0))
    const = lambda r, w: pl.BlockSpec((r, w), lambda i: (0, 0))
    return pl.pallas_call(
        functools.partial(_out_proj_body, n_p),
        grid=(n_p + n_s,),
        in_specs=[row(D_MODEL), xp_spec, xs_spec, const(1, D_MODEL), const(D_MODEL, ROUTE_W),
                  pl.BlockSpec(memory_space=pl.ANY)],
        out_specs=[row(D_MODEL), row(HALF), row(ROUTE_W), const(1, ROUTE_W)],
        out_shape=[jax.ShapeDtypeStruct((t, D_MODEL), F32),
                   jax.ShapeDtypeStruct((t, HALF), U32),
                   jax.ShapeDtypeStruct((t, ROUTE_W), F32),
                   jax.ShapeDtypeStruct((1, ROUTE_W), F32)],
        scratch_shapes=[pltpu.VMEM((D_MODEL, D_MODEL), BF16),
                        pltpu.VMEM((2, 256, D_MODEL), F32),
                        pltpu.SemaphoreType.DMA((2,))],
        compiler_params=pltpu.CompilerParams(dimension_semantics=("arbitrary",), vmem_limit_bytes=VMEM_LIMIT),
        name="out_proj_route",
    )(merged, xp, xsm, norm2, w_router, w_o)


def _moe_plan(rec, cnt, n_tiles):
    t = rec.shape[0]
    n_pairs = 2 * t
    eid = rec[:, 0:2].astype(I32).reshape(-1)
    rank = rec[:, 4:6].astype(I32).reshape(-1)
    counts = cnt[0, :N_EXPERTS].astype(I32)
    pair_start = jnp.cumsum(counts) - counts
    experts = jnp.arange(N_EXPERTS, dtype=I32)
    onehot = (eid[:, None] == experts[None, :]).astype(I32)
    pos = rank + jnp.sum(onehot * pair_start[None, :], axis=1)
    _, order = lax.sort((pos, jnp.arange(n_pairs, dtype=I32)), num_keys=1)
    tiles_e = (counts + TM_MOE - 1) // TM_MOE
    tile_end = jnp.cumsum(tiles_e)
    tile_start = tile_end - tiles_e
    n_valid = tile_end[-1]
    tile_ids = jnp.arange(n_tiles, dtype=I32)
    tile_e = jnp.sum((tile_ids[:, None] >= tile_end[None, :]).astype(I32), axis=1)
    last_e = jnp.sum((n_valid - 1 >= tile_end).astype(I32))
    tile_e = jnp.minimum(jnp.where(tile_ids < n_valid, tile_e, last_e), N_EXPERTS - 1)
    tile_onehot = (tile_e[:, None] == experts[None, :]).astype(I32)
    tile_q0 = (tile_ids - jnp.sum(tile_onehot * tile_start[None, :], axis=1)) * TM_MOE
    tile_q0 = jnp.where(tile_ids < n_valid, tile_q0, 0)
    dst = rank + jnp.sum(onehot * (tile_start * TM_MOE)[None, :], axis=1)
    pad_start = tile_start * TM_MOE + counts
    pad_len = tiles_e * TM_MOE - counts
    return (dst.astype(I32), pad_start.astype(I32), pad_len.astype(I32), tile_e, tile_q0.astype(I32),
            n_valid.astype(I32).reshape(1), pair_start, counts, order)


DISPATCH_BUFS = 3
ROW_PIECES = tuple(TM_MOE >> (b + 1) for b in range(TM_MOE.bit_length() - 1))


def _dispatch_body(dst_ref, pad_start_ref, pad_len_ref, nvalid_ref, v_hbm, x_hbm, buf, zbuf, rsem, ssem, zsem):
    i = pl.program_id(0)
    n = pl.num_programs(0)

    def read(tile, slot):
        return pltpu.make_async_copy(v_hbm.at[pl.ds(pl.multiple_of(tile * TM, TM), TM)], buf.at[slot],
                                     rsem.at[slot])

    def row_write(slot, r, dst_row):
        return pltpu.make_async_copy(buf.at[slot, pl.ds(r, 1)], x_hbm.at[pl.ds(dst_row, 1)], ssem.at[slot])

    def drain(slot):
        for _ in range(2):
            pltpu.make_async_copy(buf.at[slot], x_hbm.at[pl.ds(0, TM)], ssem.at[slot]).wait()

    def pad_fill(go):
        def zero_rows(start, size):
            d = pltpu.make_async_copy(zbuf.at[pl.ds(0, size)], x_hbm.at[pl.ds(start, size)], zsem)
            d.start() if go else d.wait()

        def body(e, c):
            start, length = pad_start_ref[e], pad_len_ref[e]
            head = (-start) & (SUBLANES - 1)
            for h in range(SUBLANES - 1):
                @pl.when(h < head)
                def _(h=h):
                    zero_rows(start + h, 1)
            start, length = start + head, length - head
            for size in ROW_PIECES:
                if size >= SUBLANES:
                    @pl.when((length & size) != 0)
                    def _(size=size):
                        zero_rows(pl.multiple_of(start + (length & (-2 * size)), SUBLANES), size)
            return c
        lax.fori_loop(0, N_EXPERTS, body, 0)

        def unused(tile, c):
            for half in range(TM_MOE // ROW_PIECES[0]):
                zero_rows(pl.multiple_of(tile * TM_MOE + half * ROW_PIECES[0], SUBLANES), ROW_PIECES[0])
            return c
        lax.fori_loop(nvalid_ref[0], x_hbm.shape[0] // TM_MOE, unused, 0)

    @pl.when(i == 0)
    def _():
        zbuf[...] = jnp.zeros(zbuf.shape, U32)
        pad_fill(True)
        read(0, 0).start()

    @pl.when(i >= 2)
    def _():
        drain((i + 1) % DISPATCH_BUFS)

    @pl.when(i + 1 < n)
    def _():
        read(i + 1, (i + 1) % DISPATCH_BUFS).start()

    slot = i % DISPATCH_BUFS
    read(i, slot).wait()
    for r in range(TM):
        for k in range(2):
            row_write(slot, r, dst_ref[2 * (i * TM + r) + k]).start(priority=k)

    @pl.when(i == n - 1)
    def _():
        if n >= 2:
            drain((i - 1) % DISPATCH_BUFS)
        drain(slot)
        pad_fill(False)


def _dispatch(v, plan, n_tiles):
    dst, pad_start, pad_len, n_valid = plan[0], plan[1], plan[2], plan[5]
    t = v.shape[0]
    return pl.pallas_call(
        _dispatch_body,
        grid_spec=pltpu.PrefetchScalarGridSpec(
            num_scalar_prefetch=4,
            grid=(t // TM,),
            in_specs=[pl.BlockSpec(memory_space=pl.ANY)],
            out_specs=pl.BlockSpec(memory_space=pl.ANY),
            scratch_shapes=[pltpu.VMEM((DISPATCH_BUFS, TM, HALF), U32),
                            pltpu.VMEM((ROW_PIECES[0], HALF), U32),
                            pltpu.SemaphoreType.DMA((DISPATCH_BUFS,)),
                            pltpu.SemaphoreType.DMA((DISPATCH_BUFS,)),
                            pltpu.SemaphoreType.DMA(())]),
        out_shape=jax.ShapeDtypeStruct((n_tiles * TM_MOE, HALF), U32),
        compiler_params=pltpu.CompilerParams(dimension_semantics=("arbitrary",), vmem_limit_bytes=VMEM_LIMIT),
        name="moe_dispatch",
    )(dst, pad_start, pad_len, n_valid, v)


WEIGHT_SLOTS = 3


def _moe_body(n_tok, plane, tile_e_ref, tile_q0_ref, nvalid_ref, pstart_ref, cnt_ref, orow_ref,
              krank_ref, elist_ref, nexp_ref,
              x_ref, wg_hbm, wu_hbm, wd_hbm, o_hbm, ybuf0, ybuf1, ybuf2, ssem,
              stage_g, stage_u, stage_d, wsem, wg, wu, wd):
    i = pl.program_id(0)
    nv = nvalid_ref[0]
    bufs = (ybuf0, ybuf1, ybuf2)

    def scratch_row0(slot):
        return (slot & 1) * plane + n_tok + (slot >> 1) * TM_MOE

    def row_write(slot, r, dst_row):
        return pltpu.make_async_copy(bufs[slot].at[pl.ds(r, 1)], o_hbm.at[pl.ds(dst_row, 1)], ssem.at[slot])

    def scratch_rows(region):
        return pltpu.make_async_copy(ybuf0, o_hbm.at[pl.ds(scratch_row0(region), TM_MOE)], ssem.at[0])

    def start_writes(tile, slot):
        e = tile_e_ref[tile]
        valid = cnt_ref[e] - tile_q0_ref[tile]
        first = pstart_ref[e] + tile_q0_ref[tile]
        last = pstart_ref[e] + cnt_ref[e] - 1
        for r in range(TM_MOE):
            row = orow_ref[jnp.minimum(first + r, last)]
            row_write(slot, r, jnp.where(r < valid, row, scratch_row0(slot) + r)).start(priority=r % 2)

    def compute(slot):
        x_lo, x_hi = (h.astype(BF16) for h in _unpack_halves(x_ref[...]))
        hg = (jnp.dot(x_lo, wg[0:HALF, :], preferred_element_type=F32)
              + jnp.dot(x_hi, wg[HALF:D_MODEL, :], preferred_element_type=F32))
        hu = (jnp.dot(x_lo, wu[0:HALF, :], preferred_element_type=F32)
              + jnp.dot(x_hi, wu[HALF:D_MODEL, :], preferred_element_type=F32))
        act = hg * _sigmoid(hg) * hu
        bufs[slot][...] = _pack_halves(jnp.dot(act.astype(BF16), wd[...], preferred_element_type=F32))

    @pl.when(i == 0)
    def _():
        ybuf0[...] = jnp.zeros(ybuf0.shape, U32)
        for region in range(4):
            scratch_rows(region).start()
        for region in range(4):
            scratch_rows(region).wait()

    @pl.when(jnp.logical_and(i >= 3, i < nv + 3))
    def _():
        pltpu.make_async_copy(ybuf0, o_hbm.at[pl.ds(0, TM_MOE)], ssem.at[i % 3]).wait()

    def weight_copies(k, go):
        e = elist_ref[k]
        slot = k % WEIGHT_SLOTS
        for w_hbm, st in ((wg_hbm, stage_g), (wu_hbm, stage_u), (wd_hbm, stage_d)):
            d = pltpu.make_async_copy(w_hbm.at[e], st.at[slot], wsem.at[slot])
            d.start() if go else d.wait()

    @pl.when(i == 0)
    def _():
        for k in range(WEIGHT_SLOTS):
            @pl.when(k < nexp_ref[0])
            def _(k=k):
                weight_copies(k, True)

    @pl.when(i < nv)
    def _():
        prev_e = tile_e_ref[jnp.maximum(i - 1, 0)]

        @pl.when(jnp.logical_or(i == 0, tile_e_ref[i] != prev_e))
        def _():
            k = krank_ref[tile_e_ref[i]]
            slot = k % WEIGHT_SLOTS
            weight_copies(k, False)
            wg[...] = stage_g[slot].astype(BF16)
            wu[...] = stage_u[slot].astype(BF16)
            wd[...] = stage_d[slot].astype(BF16)

            @pl.when(k + WEIGHT_SLOTS < nexp_ref[0])
            def _():
                weight_copies(k + WEIGHT_SLOTS, True)

    @pl.when(i == 0)
    def _():
        compute(0)

    for slot in range(3):
        prev = (slot + 2) % 3

        @pl.when(jnp.logical_and(i % 3 == slot, jnp.logical_and(i >= 1, i < nv)))
        def _(slot=slot, prev=prev):
            start_writes(i - 1, prev)
            compute(slot)

        @pl.when(jnp.logical_and(i % 3 == slot, i == nv))
        def _(prev=prev):
            start_writes(i - 1, prev)


def _moe(x_disp, plan, w_gate, w_up, w_down, n_tiles, n_tok):
    tile_e, tile_q0, n_valid, pair_start, counts, order = plan[3:9]
    plane = n_tok + 2 * TM_MOE
    orow = (order & 1) * plane + (order >> 1)
    present = (counts > 0).astype(I32)
    krank = jnp.cumsum(present) - present
    experts = jnp.arange(N_EXPERTS, dtype=I32)
    elist = jnp.sum(jnp.where((krank[None, :] == experts[:, None]) & (present[None, :] > 0), experts[None, :], 0),
                    axis=1).astype(I32)
    nexp = jnp.sum(present).astype(I32).reshape(1)
    tile = lambda i, nv: jnp.minimum(i, jnp.maximum(nv[0] - 1, 0))
    any_spec = pl.BlockSpec(memory_space=pl.ANY)
    ybuf = pltpu.VMEM((TM_MOE, HALF), U32)
    return pl.pallas_call(
        functools.partial(_moe_body, n_tok, plane),
        grid_spec=pltpu.PrefetchScalarGridSpec(
            num_scalar_prefetch=9,
            grid=(n_tiles + 3,),
            in_specs=[pl.BlockSpec((TM_MOE, HALF), lambda i, te, tq, nv, *_: (tile(i, nv), 0)),
                      any_spec, any_spec, any_spec],
            out_specs=pl.BlockSpec(memory_space=pl.ANY),
            scratch_shapes=[ybuf, ybuf, ybuf,
                            pltpu.SemaphoreType.DMA((3,)),
                            pltpu.VMEM((WEIGHT_SLOTS, D_MODEL, D_EXPERT), F32),
                            pltpu.VMEM((WEIGHT_SLOTS, D_MODEL, D_EXPERT), F32),
                            pltpu.VMEM((WEIGHT_SLOTS, D_EXPERT, D_MODEL), F32),
                            pltpu.SemaphoreType.DMA((WEIGHT_SLOTS,)),
                            pltpu.VMEM((D_MODEL, D_EXPERT), BF16),
                            pltpu.VMEM((D_MODEL, D_EXPERT), BF16),
                            pltpu.VMEM((D_EXPERT, D_MODEL), BF16)]),
        out_shape=jax.ShapeDtypeStruct((2 * plane, HALF), U32),
        compiler_params=pltpu.CompilerParams(dimension_semantics=("arbitrary",), vmem_limit_bytes=VMEM_LIMIT),
        name="moe_experts",
    )(tile_e, tile_q0, n_valid, pair_start, counts, orow, krank.astype(I32), elist, nexp,
      x_disp, w_gate, w_up, w_down)


def _combine_body(n_p_tiles, h2_ref, rec_ref, gf_ref, y1_ref, y2_ref, outp_ref, outs_ref):
    i = pl.program_id(0)
    rec = rec_ref[...]
    y1 = jnp.concatenate(_unpack_halves(y1_ref[...]), axis=-1)
    y2 = jnp.concatenate(_unpack_halves(y2_ref[...]), axis=-1)
    h = rec[:, 2:3] * y1 + rec[:, 3:4] * y2
    out = _rmsnorm(h2_ref[...] + h, gf_ref[...])

    @pl.when(i < n_p_tiles)
    def _():
        outp_ref[...] = out

    @pl.when(i >= n_p_tiles)
    def _():
        outs_ref[...] = out


def _combine(h2, rec, y_pairs, final_norm, t_p, t_s):
    n_p, n_s = t_p // TM, t_s // TM
    plane_tiles = y_pairs.shape[0] // 2 // TM
    return pl.pallas_call(
        functools.partial(_combine_body, n_p),
        grid=(n_p + n_s,),
        in_specs=[pl.BlockSpec((TM, D_MODEL), lambda i: (i, 0)),
                  pl.BlockSpec((TM, ROUTE_W), lambda i: (i, 0)),
                  pl.BlockSpec((1, D_MODEL), lambda i: (0, 0)),
                  pl.BlockSpec((TM, HALF), lambda i: (i, 0)),
                  pl.BlockSpec((TM, HALF), lambda i: (i + plane_tiles, 0))],
        out_specs=[pl.BlockSpec((TM, D_MODEL), lambda i: (jnp.minimum(i, n_p - 1), 0)),
                   pl.BlockSpec((TM, D_MODEL), lambda i: (jnp.clip(i - n_p, 0, n_s - 1), 0))],
        out_shape=[jax.ShapeDtypeStruct((t_p, D_MODEL), F32),
                   jax.ShapeDtypeStruct((t_s, D_MODEL), F32)],
        compiler_params=pltpu.CompilerParams(dimension_semantics=("arbitrary",), vmem_limit_bytes=VMEM_LIMIT),
        name="combine_norm",
    )(h2, rec, final_norm, y_pairs, y_pairs)


def kernel(x_prompt, x_sample, state_conv, state_ssm_re, state_ssm_im, meta_tokens, norm1, w_in, conv_w,
           lam_re, lam_im, log_dt, ssm_b_re, ssm_b_im, ssm_c_re, ssm_c_im, ssm_d, w_glu, w_conv_out,
           w_ssm_out, w_o, norm2, w_coarse, w_fine, w_gate, w_up, w_down, final_norm):
    n_pb, seq, _ = x_prompt.shape
    n_sb, dec_seq, _ = x_sample.shape
    assert dec_seq == CHUNK and seq % TM == 0 and (n_sb * dec_seq) % TM == 0 and N_META == CHUNK
    t_p, t_s = n_pb * seq, n_sb * dec_seq
    xp = x_prompt.reshape(t_p, D_MODEL)
    xsm = x_sample.reshape(t_s, D_MODEL)

    xb, z, xs, zmeta, xsmeta, sga, sgb = _in_proj(xp, xsm, meta_tokens, norm1, w_in[0])

    mats = _s5_chunk_mats(lam_re[0], lam_im[0], log_dt[0], ssm_b_re[0], ssm_b_im[0], ssm_c_re[0], ssm_c_im[0])
    y5, pf_re, pf_im, sf_re, sf_im = _s5(xs, xsmeta, state_ssm_re[0], state_ssm_im[0], mats, n_pb, n_sb, seq)

    buf = state_conv[0]
    zero = jnp.zeros((n_sb, dec_seq, D_CONV), F32)
    inj1 = zero.at[:, 0].set(buf[:, 1]).reshape(t_s, D_CONV)
    inj2 = zero.at[:, 0].set(buf[:, 0]).at[:, 1].set(buf[:, 1]).reshape(t_s, D_CONV)
    merged = _mixers(xb, z, y5, xs, sga, sgb, zmeta, inj1, inj2, conv_w[0], ssm_d, w_conv_out[0], w_glu[0],
                     w_ssm_out[0], t_p // TM, seq // TM)

    w_router = jnp.concatenate(
        [w_fine[0], w_coarse[0], jnp.zeros((D_MODEL, ROUTE_W - N_EXPERTS - N_EGROUPS), F32)], axis=1)
    h2, v, rec, cnt = _out_proj(merged, xp, xsm, norm2, w_router, w_o[0])

    n_tiles = 2 * (t_p + t_s) // TM_MOE + N_EXPERTS
    plan = _moe_plan(rec, cnt, n_tiles)
    x_disp = _dispatch(v, plan, n_tiles)
    y_pairs = _moe(x_disp, plan, w_gate[0], w_up[0], w_down[0], n_tiles, t_p + t_s)
    y_p, y_s = _combine(h2, rec, y_pairs, final_norm.reshape(1, D_MODEL), t_p, t_s)

    new_conv_p = jnp.stack([z[(b + 1) * seq - 2:(b + 1) * seq] for b in range(n_pb)])
    new_conv_s = z[t_p:].reshape(n_sb, dec_seq, D_CONV)[:, dec_seq - 2:]
    return (y_p.reshape(n_pb, seq, D_MODEL), y_s.reshape(n_sb, dec_seq, D_MODEL),
            new_conv_p[None], pf_re, pf_im, new_conv_s[None], sf_re, sf_im)
```

```python
import functools

import jax
import jax.numpy as jnp
from jax import lax
from jax.experimental import pallas as pl
from jax.experimental.pallas import tpu as pltpu

F32 = jnp.float32
BF16 = jnp.bfloat16
I32 = jnp.int32
U32 = jnp.uint32

D_MODEL = 2048
D_CONV = 1024
D_SSM = 1024
SSM_H = 16
SSM_G = 64
SSM_P = 64
N_META = 16
N_EGROUPS = 4
EXPERTS_PER_GROUP = 8
N_EXPERTS = 32
D_EXPERT = 256
EPS = 1e-6

LANES = 128
SUBLANES = 8

CHUNK = 16
CHUNK_W = CHUNK * SSM_H

TM = 256
TM_MOE = 256
VMEM_LIMIT = 52 * 1024 * 1024


def _rmsnorm(x, g):
    return x * lax.rsqrt(jnp.mean(x * x, axis=-1, keepdims=True) + EPS) * g


def _sigmoid(x):
    return 1.0 / (1.0 + jnp.exp(-x))


def _gelu_tanh(x):
    c = 0.7978845608028654
    return 0.5 * x * (1.0 + jnp.tanh(c * (x + 0.044715 * (x * x * x))))


def _split_bf16(a):
    hi = a.astype(BF16)
    lo = (a - hi.astype(F32)).astype(BF16)
    return hi, lo


def _dot3(a, b):
    a_hi, a_lo = _split_bf16(a)
    b_hi, b_lo = _split_bf16(b)
    return (jnp.dot(a_hi, b_hi, preferred_element_type=F32) + jnp.dot(a_lo, b_hi, preferred_element_type=F32)
            + jnp.dot(a_hi, b_lo, preferred_element_type=F32))


HALF = D_MODEL // 2


def _pack_halves(a):
    return pltpu.pack_elementwise([a[:, :HALF], a[:, HALF:]], packed_dtype=BF16)


def _unpack_halves(p):
    return (pltpu.unpack_elementwise(p, index=0, packed_dtype=BF16, unpacked_dtype=F32),
            pltpu.unpack_elementwise(p, index=1, packed_dtype=BF16, unpacked_dtype=F32))


def _weight_copy(w_hbm, stage, sem, c, slot, rows, col0, ncols):
    return pltpu.make_async_copy(
        w_hbm.at[pl.ds(c * rows, rows), pl.ds(col0, ncols)], stage.at[slot], sem.at[slot])


def _load_weight_bf16(w_hbm, w_vmem, stage, sem, col0=0):
    k, n = w_vmem.shape
    rows = stage.shape[1]
    nchunk = k // rows
    _weight_copy(w_hbm, stage, sem, 0, 0, rows, col0, n).start()
    for c in range(nchunk):
        slot = c % 2
        if c + 1 < nchunk:
            _weight_copy(w_hbm, stage, sem, c + 1, 1 - slot, rows, col0, n).start()
        _weight_copy(w_hbm, stage, sem, c, slot, rows, col0, n).wait()
        w_vmem[pl.ds(c * rows, rows), :] = stage[slot].astype(BF16)


def _in_proj_mix_body(n_p_tiles, xp_ref, xsm_ref, meta_ref, g_ref, w_hbm,
                      xb_out, z_out, xs_out, zmeta_out, xsmeta_out,
                      w_vmem, stage, sem):
    i = pl.program_id(0)
    g = g_ref[...]

    def project(u):
        xb = jnp.dot(u, w_vmem[:, 0:D_CONV], preferred_element_type=F32)
        xc = jnp.dot(u, w_vmem[:, D_CONV:2 * D_CONV], preferred_element_type=F32)
        xv = jnp.dot(u, w_vmem[:, 2 * D_CONV:3 * D_CONV], preferred_element_type=F32)
        xs = jnp.dot(u, w_vmem[:, 3 * D_CONV:3 * D_CONV + D_SSM], preferred_element_type=F32)
        return xb, xc * xv, xs

    @pl.when(i == 0)
    def _():
        _load_weight_bf16(w_hbm, w_vmem, stage, sem, col0=0)
        um = _rmsnorm(meta_ref[...], g).astype(BF16)
        _, zm, xsm = project(um)
        zmeta_out[...] = zm
        xsmeta_out[...] = xsm

    x = jnp.where(i < n_p_tiles, xp_ref[...], xsm_ref[...])
    u = _rmsnorm(x, g).astype(BF16)
    xb, z, xs = project(u)
    xb_out[...] = xb.astype(BF16)
    z_out[...] = z
    xs_out[...] = xs


def _in_proj_gate_body(n_p_tiles, xp_ref, xsm_ref, g_ref, w_hbm, ga_out, gb_out,
                       w_vmem, stage, sem):
    i = pl.program_id(0)

    @pl.when(i == 0)
    def _():
        _load_weight_bf16(w_hbm, w_vmem, stage, sem, col0=3 * D_CONV + D_SSM)

    x = jnp.where(i < n_p_tiles, xp_ref[...], xsm_ref[...])
    u = _rmsnorm(x, g_ref[...]).astype(BF16)
    ga = jnp.dot(u, w_vmem[:, 0:D_MODEL], preferred_element_type=F32)
    ga_out[...] = _sigmoid(ga).astype(BF16)
    gb = jnp.dot(u, w_vmem[:, D_MODEL:2 * D_MODEL], preferred_element_type=F32)
    gb_out[...] = _sigmoid(gb).astype(BF16)


def _two_stream_specs(n_p_tiles, n_s_tiles):
    xp_spec = pl.BlockSpec((TM, D_MODEL), lambda i: (jnp.minimum(i, n_p_tiles - 1), 0))
    xs_spec = pl.BlockSpec((TM, D_MODEL), lambda i: (jnp.clip(i - n_p_tiles, 0, n_s_tiles - 1), 0))
    return xp_spec, xs_spec


def _in_proj(xp, xsm, meta, norm1, w_in):
    t_p, t_s = xp.shape[0], xsm.shape[0]
    n_p, n_s = t_p // TM, t_s // TM
    t = t_p + t_s
    half = 3 * D_CONV + D_SSM
    xp_spec, xs_spec = _two_stream_specs(n_p, n_s)
    g_spec = pl.BlockSpec((1, D_MODEL), lambda i: (0, 0))
    any_spec = pl.BlockSpec(memory_space=pl.ANY)
    stage_rows = 128
    row = lambda w: pl.BlockSpec((TM, w), lambda i: (i, 0))
    const = lambda r, w: pl.BlockSpec((r, w), lambda i: (0, 0))
    params = pltpu.CompilerParams(dimension_semantics=("arbitrary",), vmem_limit_bytes=VMEM_LIMIT)
    scratch = [pltpu.VMEM((D_MODEL, half), BF16),
               pltpu.VMEM((2, stage_rows, half), F32),
               pltpu.SemaphoreType.DMA((2,))]

    xb, z, xs, zmeta, xsmeta = pl.pallas_call(
        functools.partial(_in_proj_mix_body, n_p),
        grid=(n_p + n_s,),
        in_specs=[xp_spec, xs_spec, const(N_META, D_MODEL), g_spec, any_spec],
        out_specs=[row(D_CONV), row(D_CONV), row(D_SSM), const(N_META, D_CONV), const(N_META, D_SSM)],
        out_shape=[jax.ShapeDtypeStruct((t, D_CONV), BF16),
                   jax.ShapeDtypeStruct((t, D_CONV), F32),
                   jax.ShapeDtypeStruct((t, D_SSM), F32),
                   jax.ShapeDtypeStruct((N_META, D_CONV), F32),
                   jax.ShapeDtypeStruct((N_META, D_SSM), F32)],
        scratch_shapes=scratch,
        compiler_params=params,
        name="in_proj_mix",
    )(xp, xsm, meta, norm1, w_in)

    sga, sgb = pl.pallas_call(
        functools.partial(_in_proj_gate_body, n_p),
        grid=(n_p + n_s,),
        in_specs=[xp_spec, xs_spec, g_spec, any_spec],
        out_specs=[row(D_MODEL), row(D_MODEL)],
        out_shape=[jax.ShapeDtypeStruct((t, D_MODEL), BF16),
                   jax.ShapeDtypeStruct((t, D_MODEL), BF16)],
        scratch_shapes=scratch,
        compiler_params=params,
        name="in_proj_gate",
    )(xp, xsm, norm1, w_in)
    return xb, z, xs, zmeta, xsmeta, sga, sgb


S5_GROUPS_PER_STEP = LANES // SSM_H
S5_PAIRS_PER_STEP = S5_GROUPS_PER_STEP // 2


def _s5_chunk_mats(lam_re, lam_im, log_dt, b_re, b_im, c_re, c_im):
    dt = jnp.exp(log_dt)[:, None]
    lr, li = lam_re, lam_im
    z = jnp.stack([lr * dt, li * dt], axis=1)
    mag = jnp.exp(lr * dt)
    ab_re, ab_im = mag * jnp.cos(li * dt), mag * jnp.sin(li * dt)
    nr, ni = ab_re - 1.0, ab_im
    den = lr * lr + li * li
    k_re = (nr * lr + ni * li) / den
    k_im = (ni * lr - nr * li) / den
    bb = jnp.stack([k_re[..., None] * b_re - k_im[..., None] * b_im,
                    k_re[..., None] * b_im + k_im[..., None] * b_re], axis=1)
    return z, bb, jnp.stack([c_re, c_im], axis=1)


def _cmul(ar, ai, br, bi):
    return ar * br - ai * bi, ar * bi + ai * br


def _s5_operators(pair, parity, z_ref, bb_ref, c_ref, toe_scr, p_scr, q_scr, a16_scr, g_scr, m_scr):
    k = 2 * pair + parity
    zr, zi = z_ref[k, 0:1, :], z_ref[k, 1:2, :]
    mag = jnp.exp(zr)
    ar, ai = mag * jnp.cos(zi), mag * jnp.sin(zi)
    eye = lax.broadcasted_iota(I32, (SSM_P, SSM_P), 0) == lax.broadcasted_iota(I32, (SSM_P, SSM_P), 1)
    acr = jnp.sum(jnp.where(eye, ar, 0.0), axis=1, keepdims=True)
    aci = jnp.sum(jnp.where(eye, ai, 0.0), axis=1, keepdims=True)
    c_re, c_im = c_ref[k, 0], c_ref[k, 1]
    bb_re, bb_im = bb_ref[k, 0], bb_ref[k, 1]

    blk = lax.broadcasted_iota(I32, (1, CHUNK_W), 1) >> 4

    pr, pi = jnp.ones_like(ar), jnp.zeros_like(ar)
    pcr, pci = jnp.ones_like(acr), jnp.zeros_like(acr)
    pw_re = jnp.zeros((SSM_P, CHUNK_W), F32)
    pw_im = jnp.zeros((SSM_P, CHUNK_W), F32)
    for d in range(CHUNK + 1):
        g_scr[0, d * SSM_H:(d + 1) * SSM_H, :] = c_re * pr - c_im * pi
        g_scr[1, d * SSM_H:(d + 1) * SSM_H, :] = c_re * pi + c_im * pr
        if d < CHUNK:
            pw_re = jnp.where(blk == CHUNK - 1 - d, pcr, pw_re)
            pw_im = jnp.where(blk == CHUNK - 1 - d, pci, pw_im)
            pcr, pci = _cmul(pcr, pci, acr, aci)
            pr, pi = _cmul(pr, pi, ar, ai)

    half = parity * SSM_P
    a16_scr[pair, 0, :, half:half + SSM_P] = pr
    a16_scr[pair, 1, :, half:half + SSM_P] = pi
    q_scr[k] = jnp.zeros(q_scr.shape[1:], BF16)
    q_scr[k, 0, :, half:half + SSM_P] = g_scr[0, SSM_H:, :].astype(BF16)
    q_scr[k, 1, :, half:half + SSM_P] = (-g_scr[1, SSM_H:, :]).astype(BF16)

    rep = jnp.where(lax.broadcasted_iota(I32, (SSM_H, CHUNK_W), 0)
                    == (lax.broadcasted_iota(I32, (SSM_H, CHUNK_W), 1) & (SSM_H - 1)), 1.0, 0.0).astype(BF16)

    def widen(b):
        b_hi, b_lo = _split_bf16(b)
        return (jnp.dot(b_hi, rep, preferred_element_type=F32) + jnp.dot(b_lo, rep, preferred_element_type=F32))

    bw_re, bw_im = widen(bb_re), widen(bb_im)
    p_re, p_im = _cmul(pw_re, pw_im, bw_re, bw_im)
    p_scr[k, 0:SSM_P, :] = p_re.astype(BF16)
    p_scr[k, SSM_P:2 * SSM_P, :] = p_im.astype(BF16)

    m_scr[...] = _dot3(g_scr[0, 0:CHUNK_W, :], bw_re) - _dot3(g_scr[1, 0:CHUNK_W, :], bw_im)
    for t in range(CHUNK):
        acc = jnp.where(blk == 0, m_scr[t * SSM_H:(t + 1) * SSM_H, :], 0.0)
        for j in range(1, t + 1):
            acc = jnp.where(blk == j, m_scr[(t - j) * SSM_H:(t - j + 1) * SSM_H, :], acc)
        toe_scr[k, t * SSM_H:(t + 1) * SSM_H, :] = acc.astype(BF16)


def _s5_body(n_pc, n_pb, n_sb, xs_ref, xsmeta_ref, z_ref, bb_ref, c_ref, s0re_ref, s0im_ref,
             y_out, pfre_out, pfim_out, sfre_out, sfim_out,
             xt_scr, u_scr, st_re, st_im, sl_re, sl_im, sp_re, sp_im, yt_scr, toe_ref, p_ref, q_ref, a16_scr,
             g_scr, m_scr):
    gb = S5_GROUPS_PER_STEP
    n_p_rows = n_pb * n_pc
    row_s = n_p_rows
    row_m = row_s + n_sb
    t_p = n_p_rows * CHUNK
    rows_pad = xt_scr.shape[0]

    for k in range(gb):
        _s5_operators(k // 2, k % 2, z_ref, bb_ref, c_ref, toe_ref, p_ref, q_ref, a16_scr, g_scr, m_scr)

    xt_scr[row_m + 1:rows_pad, :] = jnp.zeros((rows_pad - row_m - 1, LANES), F32)
    for t in range(CHUNK):
        xt_scr[0:n_p_rows, :] = xs_ref[pl.ds(t, n_p_rows, stride=CHUNK), :]
        xt_scr[row_s:row_m, :] = xs_ref[pl.ds(t_p + t, n_sb, stride=CHUNK), :]
        xt_scr[row_m:row_m + 1, :] = xsmeta_ref[t:t + 1, :]
        xt = xt_scr[...].T.astype(BF16)
        for k in range(gb):
            u_scr[k, t * SSM_H:(t + 1) * SSM_H, :] = xt[k * SSM_H:(k + 1) * SSM_H, :]

    for k in range(gb):
        sl = jnp.dot(p_ref[k], u_scr[k], preferred_element_type=F32)
        half = (k % 2) * SSM_P
        st_re[k // 2, half:half + SSM_P, :] = sl[0:SSM_P, :]
        st_im[k // 2, half:half + SSM_P, :] = sl[SSM_P:2 * SSM_P, :]
    npair = S5_PAIRS_PER_STEP
    for j in range(npair):
        sl_re[j] = st_re[j].T
        sl_im[j] = st_im[j].T
        sp_re[j, row_m:rows_pad, :] = jnp.zeros((rows_pad - row_m, 2 * SSM_P), F32)
        sp_im[j, row_m:rows_pad, :] = jnp.zeros((rows_pad - row_m, 2 * SSM_P), F32)

    ar = [a16_scr[j, 0] for j in range(npair)]
    ai = [a16_scr[j, 1] for j in range(npair)]
    sre = [jnp.broadcast_to(sl_re[j, row_m:row_m + 1, :], (n_pb, 2 * SSM_P)) for j in range(npair)]
    sim = [jnp.broadcast_to(sl_im[j, row_m:row_m + 1, :], (n_pb, 2 * SSM_P)) for j in range(npair)]
    for c in range(n_pc):
        rows = pl.ds(c, n_pb, stride=n_pc)
        for j in range(npair):
            sp_re[j, rows, :] = sre[j]
            sp_im[j, rows, :] = sim[j]
            nre = ar[j] * sre[j] - ai[j] * sim[j] + sl_re[j, rows, :]
            nim = ar[j] * sim[j] + ai[j] * sre[j] + sl_im[j, rows, :]
            sre[j], sim[j] = nre, nim
    for j in range(npair):
        pfre_out[j] = sre[j]
        pfim_out[j] = sim[j]
        s0r, s0i = s0re_ref[j], s0im_ref[j]
        sp_re[j, row_s:row_m, :] = s0r
        sp_im[j, row_s:row_m, :] = s0i
        sfre_out[j] = ar[j] * s0r - ai[j] * s0i + sl_re[j, row_s:row_m, :]
        sfim_out[j] = ar[j] * s0i + ai[j] * s0r + sl_im[j, row_s:row_m, :]

    nt = (((1,), (1,)), ((), ()))
    for k in range(gb):
        y = jnp.dot(toe_ref[k], u_scr[k], preferred_element_type=F32)
        y += lax.dot_general(q_ref[k, 0], sp_re[k // 2].astype(BF16), nt, preferred_element_type=F32)
        y += lax.dot_general(q_ref[k, 1], sp_im[k // 2].astype(BF16), nt, preferred_element_type=F32)
        for t in range(CHUNK):
            yt_scr[t, k * SSM_H:(k + 1) * SSM_H, :] = y[t * SSM_H:(t + 1) * SSM_H, :]
    for t in range(CHUNK):
        yt = yt_scr[t].T
        y_out[pl.ds(t, n_p_rows, stride=CHUNK), :] = yt[0:n_p_rows, :]
        y_out[pl.ds(t_p + t, n_sb, stride=CHUNK), :] = yt[row_s:row_m, :]


def _s5(xs, xsmeta, state_re, state_im, mats, n_pb, n_sb, seq):
    z, bb, c = mats
    t = xs.shape[0]
    n_pc = seq // CHUNK
    rows = n_pc * n_pb + n_sb + 1
    rows_pad = -(-rows // LANES) * LANES
    gb, npair = S5_GROUPS_PER_STEP, S5_PAIRS_PER_STEP
    pairs = lambda s: jnp.transpose(s.reshape(n_sb, SSM_G // 2, 2 * SSM_P), (1, 0, 2))
    blk3 = lambda n, r, c: pl.BlockSpec((n, r, c), lambda i: (i, 0, 0))
    y, pfre, pfim, sfre, sfim = pl.pallas_call(
        functools.partial(_s5_body, n_pc, n_pb, n_sb),
        grid=(SSM_G // gb,),
        in_specs=[pl.BlockSpec((t, LANES), lambda i: (0, i)),
                  pl.BlockSpec((N_META, LANES), lambda i: (0, i)),
                  blk3(gb, 2, SSM_P),
                  pl.BlockSpec((gb, 2, SSM_P, SSM_H), lambda i: (i, 0, 0, 0)),
                  pl.BlockSpec((gb, 2, SSM_H, SSM_P), lambda i: (i, 0, 0, 0)),
                  blk3(npair, n_sb, 2 * SSM_P), blk3(npair, n_sb, 2 * SSM_P)],
        out_specs=[pl.BlockSpec((t, LANES), lambda i: (0, i)),
                   blk3(npair, n_pb, 2 * SSM_P), blk3(npair, n_pb, 2 * SSM_P),
                   blk3(npair, n_sb, 2 * SSM_P), blk3(npair, n_sb, 2 * SSM_P)],
        out_shape=[jax.ShapeDtypeStruct((t, D_SSM), F32),
                   jax.ShapeDtypeStruct((SSM_G // 2, n_pb, 2 * SSM_P), F32),
                   jax.ShapeDtypeStruct((SSM_G // 2, n_pb, 2 * SSM_P), F32),
                   jax.ShapeDtypeStruct((SSM_G // 2, n_sb, 2 * SSM_P), F32),
                   jax.ShapeDtypeStruct((SSM_G // 2, n_sb, 2 * SSM_P), F32)],
        scratch_shapes=[pltpu.VMEM((rows_pad, LANES), F32),
                        pltpu.VMEM((gb, CHUNK_W, rows_pad), BF16),
                        pltpu.VMEM((npair, 2 * SSM_P, rows_pad), F32),
                        pltpu.VMEM((npair, 2 * SSM_P, rows_pad), F32),
                        pltpu.VMEM((npair, rows_pad, 2 * SSM_P), F32),
                        pltpu.VMEM((npair, rows_pad, 2 * SSM_P), F32),
                        pltpu.VMEM((npair, rows_pad, 2 * SSM_P), F32),
                        pltpu.VMEM((npair, rows_pad, 2 * SSM_P), F32),
                        pltpu.VMEM((CHUNK, LANES, rows_pad), F32),
                        pltpu.VMEM((gb, CHUNK_W, CHUNK_W), BF16),
                        pltpu.VMEM((gb, 2 * SSM_P, CHUNK_W), BF16),
                        pltpu.VMEM((gb, 2, CHUNK_W, 2 * SSM_P), BF16),
                        pltpu.VMEM((npair, 2, 1, 2 * SSM_P), F32),
                        pltpu.VMEM((2, (CHUNK + 1) * SSM_H, SSM_P), F32),
                        pltpu.VMEM((CHUNK_W, CHUNK_W), F32)],
        compiler_params=pltpu.CompilerParams(dimension_semantics=("arbitrary",), vmem_limit_bytes=VMEM_LIMIT),
        name="s5_chunks",
    )(xs, xsmeta, z, bb, c, pairs(state_re), pairs(state_im))
    unpair = lambda a: jnp.transpose(a, (1, 0, 2)).reshape(a.shape[1], SSM_G, SSM_P)[None]
    return y, unpair(pfre), unpair(pfim), unpair(sfre), unpair(sfim)


def _mixers_body(n_p_tiles, tiles_per_seq, xb_ref, z_ref, y5_ref, xs_ref, sga_ref, sgb_ref,
                 zmeta_ref, inj1_ref, inj2_ref, cw_ref, dskip_ref,
                 wc_hbm, wg_hbm, wso_hbm, merged_out,
                 wc, wg, wso, stage_a, stage_b, sem, carry):
    i = pl.program_id(0)

    @pl.when(i == 0)
    def _():
        _load_weight_bf16(wc_hbm, wc, stage_a, sem)
        _load_weight_bf16(wg_hbm, wg, stage_b, sem)
        _load_weight_bf16(wso_hbm, wso, stage_a, sem)

    @pl.when(jnp.logical_and(i < n_p_tiles, i % tiles_per_seq == 0))
    def _():
        carry[0:2, :] = zmeta_ref[N_META - 2:N_META, :]

    z = z_ref[...]
    row = lax.broadcasted_iota(I32, (TM, 1), 0)
    is_s = i >= n_p_tiles
    r1 = pltpu.roll(z, 1, 0)
    r2 = pltpu.roll(z, 2, 0)
    c1 = carry[1:2, :]
    c2 = carry[0:1, :]
    pos = jnp.where(is_s, row & (CHUNK - 1), row)
    first1 = pos == 0
    first2 = pos < 2
    fill1 = jnp.where(is_s, inj1_ref[...], jnp.broadcast_to(c1, z.shape))
    fill2 = jnp.where(is_s, inj2_ref[...], jnp.where(row == 0, c2, c1))
    zp1 = jnp.where(first1, fill1, r1)
    zp2 = jnp.where(first2, fill2, r2)
    carry[0:2, :] = z[TM - 2:TM, :]

    cw = cw_ref[...]
    conv = cw[0:1, :] * zp2 + cw[1:2, :] * zp1 + cw[2:3, :] * z
    a_in = (xb_ref[...].astype(F32) * conv).astype(BF16)
    ya = jnp.dot(a_in, wc[...], preferred_element_type=F32)

    ys = y5_ref[...] + dskip_ref[...] * xs_ref[...]
    ys = _gelu_tanh(ys)
    glu = jnp.dot(ys.astype(BF16), wg[...], preferred_element_type=F32)
    ys = ys * _sigmoid(glu)
    yb = jnp.dot(ys.astype(BF16), wso[...], preferred_element_type=F32)

    merged = sga_ref[...].astype(F32) * ya + sgb_ref[...].astype(F32) * yb
    merged_out[...] = merged.astype(BF16)


def _mixers(xb, z, y5, xs, sga, sgb, zmeta, inj1, inj2, conv_w, d_skip, w_conv_out, w_glu, w_ssm_out,
            n_p_tiles, tiles_per_seq):
    t = xb.shape[0]
    n_s_tiles = inj1.shape[0] // TM
    row = lambda w: pl.BlockSpec((TM, w), lambda i: (i, 0))
    const = lambda r, w: pl.BlockSpec((r, w), lambda i: (0, 0))
    inj = pl.BlockSpec((TM, D_CONV), lambda i: (jnp.clip(i - n_p_tiles, 0, n_s_tiles - 1), 0))
    any_spec = pl.BlockSpec(memory_space=pl.ANY)
    return pl.pallas_call(
        functools.partial(_mixers_body, n_p_tiles, tiles_per_seq),
        grid=(t // TM,),
        in_specs=[row(D_CONV), row(D_CONV), row(D_SSM), row(D_SSM), row(D_MODEL), row(D_MODEL),
                  const(N_META, D_CONV), inj, inj, const(3, D_CONV), const(1, D_SSM),
                  any_spec, any_spec, any_spec],
        out_specs=row(D_MODEL),
        out_shape=jax.ShapeDtypeStruct((t, D_MODEL), BF16),
        scratch_shapes=[pltpu.VMEM((D_CONV, D_MODEL), BF16),
                        pltpu.VMEM((D_SSM, D_SSM), BF16),
                        pltpu.VMEM((D_SSM, D_MODEL), BF16),
                        pltpu.VMEM((2, 256, D_MODEL), F32),
                        pltpu.VMEM((2, 256, D_SSM), F32),
                        pltpu.SemaphoreType.DMA((2,)),
                        pltpu.VMEM((8, D_CONV), F32)],
        compiler_params=pltpu.CompilerParams(dimension_semantics=("arbitrary",), vmem_limit_bytes=VMEM_LIMIT),
        name="mixers",
    )(xb, z, y5, xs, sga, sgb, zmeta, inj1, inj2, conv_w, d_skip, w_conv_out, w_glu, w_ssm_out)


ROUTE_W = LANES
COARSE0 = N_EXPERTS


def _route(logits, cnt):
    col = lax.broadcasted_iota(I32, logits.shape, 1)
    colf = col.astype(F32)
    neg = jnp.float32(-jnp.inf)
    big = jnp.float32(1 << 20)
    is_c = jnp.logical_and(col >= COARSE0, col < COARSE0 + N_EGROUPS)
    lc = jnp.where(is_c, logits, neg)
    cmax = jnp.max(lc, axis=-1, keepdims=True)
    gi = jnp.min(jnp.where(lc == cmax, colf - COARSE0, big), axis=-1, keepdims=True)
    pg = 1.0 / jnp.sum(jnp.where(is_c, jnp.exp(lc - cmax), 0.0), axis=-1, keepdims=True)
    grp = (col >> 3).astype(F32)
    in_g = jnp.logical_and(col < N_EXPERTS, grp == gi)
    lf = jnp.where(in_g, logits, neg)
    m1 = jnp.max(lf, axis=-1, keepdims=True)
    i1 = jnp.min(jnp.where(lf == m1, colf, big), axis=-1, keepdims=True)
    lf2 = jnp.where(colf == i1, neg, lf)
    m2 = jnp.max(lf2, axis=-1, keepdims=True)
    i2 = jnp.min(jnp.where(lf2 == m2, colf, big), axis=-1, keepdims=True)
    e2 = jnp.exp(m2 - m1)
    w1 = pg / (1.0 + e2)
    w2 = pg * e2 / (1.0 + e2)
    n = logits.shape[0]
    hit1 = colf == i1
    hit2 = colf == i2
    onehot = jnp.where(jnp.logical_or(hit1, hit2), 1.0, 0.0)
    rr = lax.broadcasted_iota(I32, (n, n), 0)
    cc = lax.broadcasted_iota(I32, (n, n), 1)
    tri = jnp.where(cc < rr, 1.0, 0.0).astype(BF16)
    pos = jnp.dot(tri, onehot.astype(BF16), preferred_element_type=F32) + cnt
    rank1 = jnp.sum(jnp.where(hit1, pos, 0.0), axis=-1, keepdims=True)
    rank2 = jnp.sum(jnp.where(hit2, pos, 0.0), axis=-1, keepdims=True)
    vals = (i1, i2, w1, w2, rank1, rank2)
    rec = jnp.zeros(logits.shape, F32)
    for c, val in enumerate(vals):
        rec = jnp.where(col == c, val, rec)
    return rec, cnt + jnp.sum(onehot, axis=0, keepdims=True)


def _out_proj_body(n_p_tiles, merged_ref, xp_ref, xsm_ref, g2_ref, wr_ref, wo_hbm,
                   h2_out, v_out, rec_out, cnt_out, wo, stage, sem):
    i = pl.program_id(0)

    @pl.when(i == 0)
    def _():
        _load_weight_bf16(wo_hbm, wo, stage, sem)
        cnt_out[...] = jnp.zeros(cnt_out.shape, F32)

    x = jnp.where(i < n_p_tiles, xp_ref[...], xsm_ref[...])
    h2 = x + jnp.dot(merged_ref[...], wo[...], preferred_element_type=F32)
    h2_out[...] = h2
    v = _rmsnorm(h2, g2_ref[...])
    v_out[...] = _pack_halves(v)
    rec, cnt = _route(_dot3(v, wr_ref[...]), cnt_out[...])
    rec_out[...] = rec
    cnt_out[...] = cnt


def _out_proj(merged, xp, xsm, norm2, w_router, w_o):
    t_p, t_s = xp.shape[0], xsm.shape[0]
    n_p, n_s = t_p // TM, t_s // TM
    t = t_p + t_s
    xp_spec, xs_spec = _two_stream_specs(n_p, n_s)
    row = lambda w: pl.BlockSpec((TM, w), lambda i: (i, 0))
    const = lambda r, w: pl.BlockSpec((r, w), lambda i: (0, 0))
    return pl.pallas_call(
        functools.partial(_out_proj_body, n_p),
        grid=(n_p + n_s,),
        in_specs=[row(D_MODEL), xp_spec, xs_spec, const(1, D_MODEL), const(D_MODEL, ROUTE_W),
                  pl.BlockSpec(memory_space=pl.ANY)],
        out_specs=[row(D_MODEL), row(HALF), row(ROUTE_W), const(1, ROUTE_W)],
        out_shape=[jax.ShapeDtypeStruct((t, D_MODEL), F32),
                   jax.ShapeDtypeStruct((t, HALF), U32),
                   jax.ShapeDtypeStruct((t, ROUTE_W), F32),
                   jax.ShapeDtypeStruct((1, ROUTE_W), F32)],
        scratch_shapes=[pltpu.VMEM((D_MODEL, D_MODEL), BF16),
                        pltpu.VMEM((2, 256, D_MODEL), F32),
                        pltpu.SemaphoreType.DMA((2,))],
        compiler_params=pltpu.CompilerParams(dimension_semantics=("arbitrary",), vmem_limit_bytes=VMEM_LIMIT),
        name="out_proj_route",
    )(merged, xp, xsm, norm2, w_router, w_o)


def _moe_plan(rec, cnt, n_tiles):
    t = rec.shape[0]
    n_pairs = 2 * t
    eid = rec[:, 0:2].astype(I32).reshape(-1)
    rank = rec[:, 4:6].astype(I32).reshape(-1)
    counts = cnt[0, :N_EXPERTS].astype(I32)
    pair_start = jnp.cumsum(counts) - counts
    experts = jnp.arange(N_EXPERTS, dtype=I32)
    onehot = (eid[:, None] == experts[None, :]).astype(I32)
    pos = rank + jnp.sum(onehot * pair_start[None, :], axis=1)
    _, order = lax.sort((pos, jnp.arange(n_pairs, dtype=I32)), num_keys=1)
    tiles_e = (counts + TM_MOE - 1) // TM_MOE
    tile_end = jnp.cumsum(tiles_e)
    tile_start = tile_end - tiles_e
    n_valid = tile_end[-1]
    tile_ids = jnp.arange(n_tiles, dtype=I32)
    tile_e = jnp.sum((tile_ids[:, None] >= tile_end[None, :]).astype(I32), axis=1)
    last_e = jnp.sum((n_valid - 1 >= tile_end).astype(I32))
    tile_e = jnp.minimum(jnp.where(tile_ids < n_valid, tile_e, last_e), N_EXPERTS - 1)
    tile_onehot = (tile_e[:, None] == experts[None, :]).astype(I32)
    tile_q0 = (tile_ids - jnp.sum(tile_onehot * tile_start[None, :], axis=1)) * TM_MOE
    tile_q0 = jnp.where(tile_ids < n_valid, tile_q0, 0)
    dst = rank + jnp.sum(onehot * (tile_start * TM_MOE)[None, :], axis=1)
    pad_start = tile_start * TM_MOE + counts
    pad_len = tiles_e * TM_MOE - counts
    return (dst.astype(I32), pad_start.astype(I32), pad_len.astype(I32), tile_e, tile_q0.astype(I32),
            n_valid.astype(I32).reshape(1), pair_start, counts, order)


DISPATCH_BUFS = 3
ROW_PIECES = tuple(TM_MOE >> (b + 1) for b in range(TM_MOE.bit_length() - 1))


def _dispatch_body(dst_ref, pad_start_ref, pad_len_ref, nvalid_ref, v_hbm, x_hbm, buf, zbuf, rsem, ssem, zsem):
    i = pl.program_id(0)
    n = pl.num_programs(0)

    def read(tile, slot):
        return pltpu.make_async_copy(v_hbm.at[pl.ds(pl.multiple_of(tile * TM, TM), TM)], buf.at[slot],
                                     rsem.at[slot])

    def row_write(slot, r, dst_row):
        return pltpu.make_async_copy(buf.at[slot, pl.ds(r, 1)], x_hbm.at[pl.ds(dst_row, 1)], ssem.at[slot])

    def drain(slot):
        for _ in range(2):
            pltpu.make_async_copy(buf.at[slot], x_hbm.at[pl.ds(0, TM)], ssem.at[slot]).wait()

    def pad_fill(go):
        def zero_rows(start, size):
            d = pltpu.make_async_copy(zbuf.at[pl.ds(0, size)], x_hbm.at[pl.ds(start, size)], zsem)
            d.start() if go else d.wait()

        def body(e, c):
            start, length = pad_start_ref[e], pad_len_ref[e]
            head = (-start) & (SUBLANES - 1)
            for h in range(SUBLANES - 1):
                @pl.when(h < head)
                def _(h=h):
                    zero_rows(start + h, 1)
            start, length = start + head, length - head
            for size in ROW_PIECES:
                if size >= SUBLANES:
                    @pl.when((length & size) != 0)
                    def _(size=size):
                        zero_rows(pl.multiple_of(start + (length & (-2 * size)), SUBLANES), size)
            return c
        lax.fori_loop(0, N_EXPERTS, body, 0)

        def unused(tile, c):
            for half in range(TM_MOE // ROW_PIECES[0]):
                zero_rows(pl.multiple_of(tile * TM_MOE + half * ROW_PIECES[0], SUBLANES), ROW_PIECES[0])
            return c
        lax.fori_loop(nvalid_ref[0], x_hbm.shape[0] // TM_MOE, unused, 0)

    @pl.when(i == 0)
    def _():
        zbuf[...] = jnp.zeros(zbuf.shape, U32)
        pad_fill(True)
        read(0, 0).start()

    @pl.when(i >= 2)
    def _():
        drain((i + 1) % DISPATCH_BUFS)

    @pl.when(i + 1 < n)
    def _():
        read(i + 1, (i + 1) % DISPATCH_BUFS).start()

    slot = i % DISPATCH_BUFS
    read(i, slot).wait()
    for r in range(TM):
        for k in range(2):
            row_write(slot, r, dst_ref[2 * (i * TM + r) + k]).start(priority=k)

    @pl.when(i == n - 1)
    def _():
        if n >= 2:
            drain((i - 1) % DISPATCH_BUFS)
        drain(slot)
        pad_fill(False)


def _dispatch(v, plan, n_tiles):
    dst, pad_start, pad_len, n_valid = plan[0], plan[1], plan[2], plan[5]
    t = v.shape[0]
    return pl.pallas_call(
        _dispatch_body,
        grid_spec=pltpu.PrefetchScalarGridSpec(
            num_scalar_prefetch=4,
            grid=(t // TM,),
            in_specs=[pl.BlockSpec(memory_space=pl.ANY)],
            out_specs=pl.BlockSpec(memory_space=pl.ANY),
            scratch_shapes=[pltpu.VMEM((DISPATCH_BUFS, TM, HALF), U32),
                            pltpu.VMEM((ROW_PIECES[0], HALF), U32),
                            pltpu.SemaphoreType.DMA((DISPATCH_BUFS,)),
                            pltpu.SemaphoreType.DMA((DISPATCH_BUFS,)),
                            pltpu.SemaphoreType.DMA(())]),
        out_shape=jax.ShapeDtypeStruct((n_tiles * TM_MOE, HALF), U32),
        compiler_params=pltpu.CompilerParams(dimension_semantics=("arbitrary",), vmem_limit_bytes=VMEM_LIMIT),
        name="moe_dispatch",
    )(dst, pad_start, pad_len, n_valid, v)


WEIGHT_SLOTS = 3


def _moe_body(n_tok, plane, tile_e_ref, tile_q0_ref, nvalid_ref, pstart_ref, cnt_ref, orow_ref,
              krank_ref, elist_ref, nexp_ref,
              x_ref, wg_hbm, wu_hbm, wd_hbm, o_hbm, ybuf0, ybuf1, ybuf2, ssem,
              stage_g, stage_u, stage_d, wsem, wg, wu, wd):
    i = pl.program_id(0)
    nv = nvalid_ref[0]
    bufs = (ybuf0, ybuf1, ybuf2)

    def scratch_row0(slot):
        return (slot & 1) * plane + n_tok + (slot >> 1) * TM_MOE

    def row_write(slot, r, dst_row):
        return pltpu.make_async_copy(bufs[slot].at[pl.ds(r, 1)], o_hbm.at[pl.ds(dst_row, 1)], ssem.at[slot])

    def scratch_rows(region):
        return pltpu.make_async_copy(ybuf0, o_hbm.at[pl.ds(scratch_row0(region), TM_MOE)], ssem.at[0])

    def start_writes(tile, slot):
        e = tile_e_ref[tile]
        valid = cnt_ref[e] - tile_q0_ref[tile]
        first = pstart_ref[e] + tile_q0_ref[tile]
        last = pstart_ref[e] + cnt_ref[e] - 1
        for r in range(TM_MOE):
            row = orow_ref[jnp.minimum(first + r, last)]
            row_write(slot, r, jnp.where(r < valid, row, scratch_row0(slot) + r)).start(priority=r % 2)

    def compute(slot):
        x_lo, x_hi = (h.astype(BF16) for h in _unpack_halves(x_ref[...]))
        hg = (jnp.dot(x_lo, wg[0:HALF, :], preferred_element_type=F32)
              + jnp.dot(x_hi, wg[HALF:D_MODEL, :], preferred_element_type=F32))
        hu = (jnp.dot(x_lo, wu[0:HALF, :], preferred_element_type=F32)
              + jnp.dot(x_hi, wu[HALF:D_MODEL, :], preferred_element_type=F32))
        act = hg * _sigmoid(hg) * hu
        bufs[slot][...] = _pack_halves(jnp.dot(act.astype(BF16), wd[...], preferred_element_type=F32))

    @pl.when(i == 0)
    def _():
        ybuf0[...] = jnp.zeros(ybuf0.shape, U32)
        for region in range(4):
            scratch_rows(region).start()
        for region in range(4):
            scratch_rows(region).wait()

    @pl.when(jnp.logical_and(i >= 3, i < nv + 3))
    def _():
        pltpu.make_async_copy(ybuf0, o_hbm.at[pl.ds(0, TM_MOE)], ssem.at[i % 3]).wait()

    def weight_copies(k, go):
        e = elist_ref[k]
        slot = k % WEIGHT_SLOTS
        for w_hbm, st in ((wg_hbm, stage_g), (wu_hbm, stage_u), (wd_hbm, stage_d)):
            d = pltpu.make_async_copy(w_hbm.at[e], st.at[slot], wsem.at[slot])
            d.start() if go else d.wait()

    @pl.when(i == 0)
    def _():
        for k in range(WEIGHT_SLOTS):
            @pl.when(k < nexp_ref[0])
            def _(k=k):
                weight_copies(k, True)

    @pl.when(i < nv)
    def _():
        prev_e = tile_e_ref[jnp.maximum(i - 1, 0)]

        @pl.when(jnp.logical_or(i == 0, tile_e_ref[i] != prev_e))
        def _():
            k = krank_ref[tile_e_ref[i]]
            slot = k % WEIGHT_SLOTS
            weight_copies(k, False)
            wg[...] = stage_g[slot].astype(BF16)
            wu[...] = stage_u[slot].astype(BF16)
            wd[...] = stage_d[slot].astype(BF16)

            @pl.when(k + WEIGHT_SLOTS < nexp_ref[0])
            def _():
                weight_copies(k + WEIGHT_SLOTS, True)

    @pl.when(i == 0)
    def _():
        compute(0)

    for slot in range(3):
        prev = (slot + 2) % 3

        @pl.when(jnp.logical_and(i % 3 == slot, jnp.logical_and(i >= 1, i < nv)))
        def _(slot=slot, prev=prev):
            start_writes(i - 1, prev)
            compute(slot)

        @pl.when(jnp.logical_and(i % 3 == slot, i == nv))
        def _(prev=prev):
            start_writes(i - 1, prev)


def _moe(x_disp, plan, w_gate, w_up, w_down, n_tiles, n_tok):
    tile_e, tile_q0, n_valid, pair_start, counts, order = plan[3:9]
    plane = n_tok + 2 * TM_MOE
    orow = (order & 1) * plane + (order >> 1)
    present = (counts > 0).astype(I32)
    krank = jnp.cumsum(present) - present
    experts = jnp.arange(N_EXPERTS, dtype=I32)
    elist = jnp.sum(jnp.where((krank[None, :] == experts[:, None]) & (present[None, :] > 0), experts[None, :], 0),
                    axis=1).astype(I32)
    nexp = jnp.sum(present).astype(I32).reshape(1)
    tile = lambda i, nv: jnp.minimum(i, jnp.maximum(nv[0] - 1, 0))
    any_spec = pl.BlockSpec(memory_space=pl.ANY)
    ybuf = pltpu.VMEM((TM_MOE, HALF), U32)
    return pl.pallas_call(
        functools.partial(_moe_body, n_tok, plane),
        grid_spec=pltpu.PrefetchScalarGridSpec(
            num_scalar_prefetch=9,
            grid=(n_tiles + 3,),
            in_specs=[pl.BlockSpec((TM_MOE, HALF), lambda i, te, tq, nv, *_: (tile(i, nv), 0)),
                      any_spec, any_spec, any_spec],
            out_specs=pl.BlockSpec(memory_space=pl.ANY),
            scratch_shapes=[ybuf, ybuf, ybuf,
                            pltpu.SemaphoreType.DMA((3,)),
                            pltpu.VMEM((WEIGHT_SLOTS, D_MODEL, D_EXPERT), F32),
                            pltpu.VMEM((WEIGHT_SLOTS, D_MODEL, D_EXPERT), F32),
                            pltpu.VMEM((WEIGHT_SLOTS, D_EXPERT, D_MODEL), F32),
                            pltpu.SemaphoreType.DMA((WEIGHT_SLOTS,)),
                            pltpu.VMEM((D_MODEL, D_EXPERT), BF16),
                            pltpu.VMEM((D_MODEL, D_EXPERT), BF16),
                            pltpu.VMEM((D_EXPERT, D_MODEL), BF16)]),
        out_shape=jax.ShapeDtypeStruct((2 * plane, HALF), U32),
        compiler_params=pltpu.CompilerParams(dimension_semantics=("arbitrary",), vmem_limit_bytes=VMEM_LIMIT),
        name="moe_experts",
    )(tile_e, tile_q0, n_valid, pair_start, counts, orow, krank.astype(I32), elist, nexp,
      x_disp, w_gate, w_up, w_down)


def _combine_body(n_p_tiles, h2_ref, rec_ref, gf_ref, y1_ref, y2_ref, outp_ref, outs_ref):
    i = pl.program_id(0)
    rec = rec_ref[...]
    y1 = jnp.concatenate(_unpack_halves(y1_ref[...]), axis=-1)
    y2 = jnp.concatenate(_unpack_halves(y2_ref[...]), axis=-1)
    h = rec[:, 2:3] * y1 + rec[:, 3:4] * y2
    out = _rmsnorm(h2_ref[...] + h, gf_ref[...])

    @pl.when(i < n_p_tiles)
    def _():
        outp_ref[...] = out

    @pl.when(i >= n_p_tiles)
    def _():
        outs_ref[...] = out


def _combine(h2, rec, y_pairs, final_norm, t_p, t_s):
    n_p, n_s = t_p // TM, t_s // TM
    plane_tiles = y_pairs.shape[0] // 2 // TM
    return pl.pallas_call(
        functools.partial(_combine_body, n_p),
        grid=(n_p + n_s,),
        in_specs=[pl.BlockSpec((TM, D_MODEL), lambda i: (i, 0)),
                  pl.BlockSpec((TM, ROUTE_W), lambda i: (i, 0)),
                  pl.BlockSpec((1, D_MODEL), lambda i: (0, 0)),
                  pl.BlockSpec((TM, HALF), lambda i: (i, 0)),
                  pl.BlockSpec((TM, HALF), lambda i: (i + plane_tiles, 0))],
        out_specs=[pl.BlockSpec((TM, D_MODEL), lambda i: (jnp.minimum(i, n_p - 1), 0)),
                   pl.BlockSpec((TM, D_MODEL), lambda i: (jnp.clip(i - n_p, 0, n_s - 1), 0))],
        out_shape=[jax.ShapeDtypeStruct((t_p, D_MODEL), F32),
                   jax.ShapeDtypeStruct((t_s, D_MODEL), F32)],
        compiler_params=pltpu.CompilerParams(dimension_semantics=("arbitrary",), vmem_limit_bytes=VMEM_LIMIT),
        name="combine_norm",
    )(h2, rec, final_norm, y_pairs, y_pairs)


def kernel(x_prompt, x_sample, state_conv, state_ssm_re, state_ssm_im, meta_tokens, norm1, w_in, conv_w,
           lam_re, lam_im, log_dt, ssm_b_re, ssm_b_im, ssm_c_re, ssm_c_im, ssm_d, w_glu, w_conv_out,
           w_ssm_out, w_o, norm2, w_coarse, w_fine, w_gate, w_up, w_down, final_norm):
    n_pb, seq, _ = x_prompt.shape
    n_sb, dec_seq, _ = x_sample.shape
    assert dec_seq == CHUNK and seq % TM == 0 and (n_sb * dec_seq) % TM == 0 and N_META == CHUNK
    t_p, t_s = n_pb * seq, n_sb * dec_seq
    xp = x_prompt.reshape(t_p, D_MODEL)
    xsm = x_sample.reshape(t_s, D_MODEL)

    xb, z, xs, zmeta, xsmeta, sga, sgb = _in_proj(xp, xsm, meta_tokens, norm1, w_in[0])

    mats = _s5_chunk_mats(lam_re[0], lam_im[0], log_dt[0], ssm_b_re[0], ssm_b_im[0], ssm_c_re[0], ssm_c_im[0])
    y5, pf_re, pf_im, sf_re, sf_im = _s5(xs, xsmeta, state_ssm_re[0], state_ssm_im[0], mats, n_pb, n_sb, seq)

    buf = state_conv[0]
    zero = jnp.zeros((n_sb, dec_seq, D_CONV), F32)
    inj1 = zero.at[:, 0].set(buf[:, 1]).reshape(t_s, D_CONV)
    inj2 = zero.at[:, 0].set(buf[:, 0]).at[:, 1].set(buf[:, 1]).reshape(t_s, D_CONV)
    merged = _mixers(xb, z, y5, xs, sga, sgb, zmeta, inj1, inj2, conv_w[0], ssm_d, w_conv_out[0], w_glu[0],
                     w_ssm_out[0], t_p // TM, seq // TM)

    w_router = jnp.concatenate(
        [w_fine[0], w_coarse[0], jnp.zeros((D_MODEL, ROUTE_W - N_EXPERTS - N_EGROUPS), F32)], axis=1)
    h2, v, rec, cnt = _out_proj(merged, xp, xsm, norm2, w_router, w_o[0])

    n_tiles = 2 * (t_p + t_s) // TM_MOE + N_EXPERTS
    plan = _moe_plan(rec, cnt, n_tiles)
    x_disp = _dispatch(v, plan, n_tiles)
    y_pairs = _moe(x_disp, plan, w_gate[0], w_up[0], w_down[0], n_tiles, t_p + t_s)
    y_p, y_s = _combine(h2, rec, y_pairs, final_norm.reshape(1, D_MODEL), t_p, t_s)

    new_conv_p = jnp.stack([z[(b + 1) * seq - 2:(b + 1) * seq] for b in range(n_pb)])
    new_conv_s = z[t_p:].reshape(n_sb, dec_seq, D_CONV)[:, dec_seq - 2:]
    return (y_p.reshape(n_pb, seq, D_MODEL), y_s.reshape(n_sb, dec_seq, D_MODEL),
            new_conv_p[None], pf_re, pf_im, new_conv_s[None], sf_re, sf_im)
```

```python
import functools

import jax
import jax.numpy as jnp
from jax import lax
from jax.experimental import pallas as pl
from jax.experimental.pallas import tpu as pltpu

F32 = jnp.float32
BF16 = jnp.bfloat16
I32 = jnp.int32
U32 = jnp.uint32

D_MODEL = 2048
D_CONV = 1024
D_SSM = 1024
SSM_H = 16
SSM_G = 64
SSM_P = 64
N_META = 16
N_EGROUPS = 4
EXPERTS_PER_GROUP = 8
N_EXPERTS = 32
D_EXPERT = 256
EPS = 1e-6

LANES = 128
SUBLANES = 8

CHUNK = 16
CHUNK_W = CHUNK * SSM_H

TM = 256
TM_MOE = 256
VMEM_LIMIT = 52 * 1024 * 1024


def _rmsnorm(x, g):
    return x * lax.rsqrt(jnp.mean(x * x, axis=-1, keepdims=True) + EPS) * g


def _sigmoid(x):
    return 1.0 / (1.0 + jnp.exp(-x))


def _gelu_tanh(x):
    c = 0.7978845608028654
    return 0.5 * x * (1.0 + jnp.tanh(c * (x + 0.044715 * (x * x * x))))


def _split_bf16(a):
    hi = a.astype(BF16)
    lo = (a - hi.astype(F32)).astype(BF16)
    return hi, lo


def _dot3(a, b):
    a_hi, a_lo = _split_bf16(a)
    b_hi, b_lo = _split_bf16(b)
    return (jnp.dot(a_hi, b_hi, preferred_element_type=F32) + jnp.dot(a_lo, b_hi, preferred_element_type=F32)
            + jnp.dot(a_hi, b_lo, preferred_element_type=F32))


HALF = D_MODEL // 2


def _pack_halves(a):
    return pltpu.pack_elementwise([a[:, :HALF], a[:, HALF:]], packed_dtype=BF16)


def _unpack_halves(p):
    return (pltpu.unpack_elementwise(p, index=0, packed_dtype=BF16, unpacked_dtype=F32),
            pltpu.unpack_elementwise(p, index=1, packed_dtype=BF16, unpacked_dtype=F32))


def _weight_copy(w_hbm, stage, sem, c, slot, rows, col0, ncols):
    return pltpu.make_async_copy(
        w_hbm.at[pl.ds(c * rows, rows), pl.ds(col0, ncols)], stage.at[slot], sem.at[slot])


def _load_weight_bf16(w_hbm, w_vmem, stage, sem, col0=0):
    k, n = w_vmem.shape
    rows = stage.shape[1]
    nchunk = k // rows
    _weight_copy(w_hbm, stage, sem, 0, 0, rows, col0, n).start()
    for c in range(nchunk):
        slot = c % 2
        if c + 1 < nchunk:
            _weight_copy(w_hbm, stage, sem, c + 1, 1 - slot, rows, col0, n).start()
        _weight_copy(w_hbm, stage, sem, c, slot, rows, col0, n).wait()
        w_vmem[pl.ds(c * rows, rows), :] = stage[slot].astype(BF16)


def _in_proj_mix_body(n_p_tiles, xp_ref, xsm_ref, meta_ref, g_ref, w_hbm,
                      xb_out, z_out, xs_out, zmeta_out, xsmeta_out,
                      w_vmem, stage, sem):
    i = pl.program_id(0)
    g = g_ref[...]

    def project(u):
        xb = jnp.dot(u, w_vmem[:, 0:D_CONV], preferred_element_type=F32)
        xc = jnp.dot(u, w_vmem[:, D_CONV:2 * D_CONV], preferred_element_type=F32)
        xv = jnp.dot(u, w_vmem[:, 2 * D_CONV:3 * D_CONV], preferred_element_type=F32)
        xs = jnp.dot(u, w_vmem[:, 3 * D_CONV:3 * D_CONV + D_SSM], preferred_element_type=F32)
        return xb, xc * xv, xs

    @pl.when(i == 0)
    def _():
        _load_weight_bf16(w_hbm, w_vmem, stage, sem, col0=0)
        um = _rmsnorm(meta_ref[...], g).astype(BF16)
        _, zm, xsm = project(um)
        zmeta_out[...] = zm
        xsmeta_out[...] = xsm

    x = jnp.where(i < n_p_tiles, xp_ref[...], xsm_ref[...])
    u = _rmsnorm(x, g).astype(BF16)
    xb, z, xs = project(u)
    xb_out[...] = xb.astype(BF16)
    z_out[...] = z
    xs_out[...] = xs


def _in_proj_gate_body(n_p_tiles, xp_ref, xsm_ref, g_ref, w_hbm, ga_out, gb_out,
                       w_vmem, stage, sem):
    i = pl.program_id(0)

    @pl.when(i == 0)
    def _():
        _load_weight_bf16(w_hbm, w_vmem, stage, sem, col0=3 * D_CONV + D_SSM)

    x = jnp.where(i < n_p_tiles, xp_ref[...], xsm_ref[...])
    u = _rmsnorm(x, g_ref[...]).astype(BF16)
    ga = jnp.dot(u, w_vmem[:, 0:D_MODEL], preferred_element_type=F32)
    ga_out[...] = _sigmoid(ga).astype(BF16)
    gb = jnp.dot(u, w_vmem[:, D_MODEL:2 * D_MODEL], preferred_element_type=F32)
    gb_out[...] = _sigmoid(gb).astype(BF16)


def _two_stream_specs(n_p_tiles, n_s_tiles, tm=TM):
    xp_spec = pl.BlockSpec((tm, D_MODEL), lambda i: (jnp.minimum(i, n_p_tiles - 1), 0))
    mode = pl.Buffered(1) if n_s_tiles == 1 else None
    xs_spec = pl.BlockSpec((tm, D_MODEL), lambda i: (jnp.clip(i - n_p_tiles, 0, n_s_tiles - 1), 0),
                           pipeline_mode=mode)
    return xp_spec, xs_spec


TM_IN = 512


def _in_proj(xp, xsm, meta, norm1, w_in):
    t_p, t_s = xp.shape[0], xsm.shape[0]
    n_p, n_s = t_p // TM_IN, t_s // TM_IN
    t = t_p + t_s
    half = 3 * D_CONV + D_SSM
    xp_spec, xs_spec = _two_stream_specs(n_p, n_s, TM_IN)
    g_spec = pl.BlockSpec((1, D_MODEL), lambda i: (0, 0))
    any_spec = pl.BlockSpec(memory_space=pl.ANY)
    stage_rows = 128
    row = lambda w: pl.BlockSpec((TM_IN, w), lambda i: (i, 0))
    const = lambda r, w: pl.BlockSpec((r, w), lambda i: (0, 0))
    params = pltpu.CompilerParams(dimension_semantics=("arbitrary",), vmem_limit_bytes=VMEM_LIMIT)
    scratch = [pltpu.VMEM((D_MODEL, half), BF16),
               pltpu.VMEM((2, stage_rows, half), F32),
               pltpu.SemaphoreType.DMA((2,))]

    xb, z, xs, zmeta, xsmeta = pl.pallas_call(
        functools.partial(_in_proj_mix_body, n_p),
        grid=(n_p + n_s,),
        in_specs=[xp_spec, xs_spec, const(N_META, D_MODEL), g_spec, any_spec],
        out_specs=[row(D_CONV), row(D_CONV), row(D_SSM), const(N_META, D_CONV), const(N_META, D_SSM)],
        out_shape=[jax.ShapeDtypeStruct((t, D_CONV), BF16),
                   jax.ShapeDtypeStruct((t, D_CONV), F32),
                   jax.ShapeDtypeStruct((t, D_SSM), F32),
                   jax.ShapeDtypeStruct((N_META, D_CONV), F32),
                   jax.ShapeDtypeStruct((N_META, D_SSM), F32)],
        scratch_shapes=scratch,
        compiler_params=params,
        name="in_proj_mix",
    )(xp, xsm, meta, norm1, w_in)

    sga, sgb = pl.pallas_call(
        functools.partial(_in_proj_gate_body, n_p),
        grid=(n_p + n_s,),
        in_specs=[xp_spec, xs_spec, g_spec, any_spec],
        out_specs=[row(D_MODEL), row(D_MODEL)],
        out_shape=[jax.ShapeDtypeStruct((t, D_MODEL), BF16),
                   jax.ShapeDtypeStruct((t, D_MODEL), BF16)],
        scratch_shapes=scratch,
        compiler_params=params,
        name="in_proj_gate",
    )(xp, xsm, norm1, w_in)
    return xb, z, xs, zmeta, xsmeta, sga, sgb


S5_GROUPS_PER_STEP = LANES // SSM_H
S5_PAIRS_PER_STEP = S5_GROUPS_PER_STEP // 2


def _s5_chunk_mats(lam_re, lam_im, log_dt, b_re, b_im, c_re, c_im):
    dt = jnp.exp(log_dt)[:, None]
    lr, li = lam_re, lam_im
    z = jnp.stack([lr * dt, li * dt], axis=1)
    mag = jnp.exp(lr * dt)
    ab_re, ab_im = mag * jnp.cos(li * dt), mag * jnp.sin(li * dt)
    nr, ni = ab_re - 1.0, ab_im
    den = lr * lr + li * li
    k_re = (nr * lr + ni * li) / den
    k_im = (ni * lr - nr * li) / den
    bb = jnp.stack([k_re[..., None] * b_re - k_im[..., None] * b_im,
                    k_re[..., None] * b_im + k_im[..., None] * b_re], axis=1)
    return z, bb, jnp.stack([c_re, c_im], axis=1)


def _cmul(ar, ai, br, bi):
    return ar * br - ai * bi, ar * bi + ai * br


def _s5_operators(pair, parity, z_ref, bb_ref, c_ref, toe_scr, p_scr, q_scr, a16_scr, g_scr, m_scr):
    k = 2 * pair + parity
    zr, zi = z_ref[k, 0:1, :], z_ref[k, 1:2, :]
    mag = jnp.exp(zr)
    ar, ai = mag * jnp.cos(zi), mag * jnp.sin(zi)
    eye = lax.broadcasted_iota(I32, (SSM_P, SSM_P), 0) == lax.broadcasted_iota(I32, (SSM_P, SSM_P), 1)
    acr = jnp.sum(jnp.where(eye, ar, 0.0), axis=1, keepdims=True)
    aci = jnp.sum(jnp.where(eye, ai, 0.0), axis=1, keepdims=True)
    c_re, c_im = c_ref[k, 0], c_ref[k, 1]
    bb_re, bb_im = bb_ref[k, 0], bb_ref[k, 1]

    blk = lax.broadcasted_iota(I32, (1, CHUNK_W), 1) >> 4

    pr, pi = jnp.ones_like(ar), jnp.zeros_like(ar)
    pcr, pci = jnp.ones_like(acr), jnp.zeros_like(acr)
    pw_re = jnp.zeros((SSM_P, CHUNK_W), F32)
    pw_im = jnp.zeros((SSM_P, CHUNK_W), F32)
    for d in range(CHUNK + 1):
        g_scr[0, d * SSM_H:(d + 1) * SSM_H, :] = c_re * pr - c_im * pi
        g_scr[1, d * SSM_H:(d + 1) * SSM_H, :] = c_re * pi + c_im * pr
        if d < CHUNK:
            pw_re = jnp.where(blk == CHUNK - 1 - d, pcr, pw_re)
            pw_im = jnp.where(blk == CHUNK - 1 - d, pci, pw_im)
            pcr, pci = _cmul(pcr, pci, acr, aci)
            pr, pi = _cmul(pr, pi, ar, ai)

    half = parity * SSM_P
    a16_scr[pair, 0, :, half:half + SSM_P] = pr
    a16_scr[pair, 1, :, half:half + SSM_P] = pi
    q_scr[k] = jnp.zeros(q_scr.shape[1:], BF16)
    q_scr[k, 0, :, half:half + SSM_P] = g_scr[0, SSM_H:, :].astype(BF16)
    q_scr[k, 1, :, half:half + SSM_P] = (-g_scr[1, SSM_H:, :]).astype(BF16)

    rep = jnp.where(lax.broadcasted_iota(I32, (SSM_H, CHUNK_W), 0)
                    == (lax.broadcasted_iota(I32, (SSM_H, CHUNK_W), 1) & (SSM_H - 1)), 1.0, 0.0).astype(BF16)

    def widen(b):
        b_hi, b_lo = _split_bf16(b)
        return (jnp.dot(b_hi, rep, preferred_element_type=F32) + jnp.dot(b_lo, rep, preferred_element_type=F32))

    bw_re, bw_im = widen(bb_re), widen(bb_im)
    p_re, p_im = _cmul(pw_re, pw_im, bw_re, bw_im)
    p_scr[k, 0:SSM_P, :] = p_re.astype(BF16)
    p_scr[k, SSM_P:2 * SSM_P, :] = p_im.astype(BF16)

    m_scr[...] = _dot3(g_scr[0, 0:CHUNK_W, :], bw_re) - _dot3(g_scr[1, 0:CHUNK_W, :], bw_im)
    for t in range(CHUNK):
        acc = jnp.where(blk == 0, m_scr[t * SSM_H:(t + 1) * SSM_H, :], 0.0)
        for j in range(1, t + 1):
            acc = jnp.where(blk == j, m_scr[(t - j) * SSM_H:(t - j + 1) * SSM_H, :], acc)
        toe_scr[k, t * SSM_H:(t + 1) * SSM_H, :] = acc.astype(BF16)


def _s5_body(n_pc, n_pb, n_sb, xs_ref, xsmeta_ref, z_ref, bb_ref, c_ref, s0re_ref, s0im_ref,
             y_out, pfre_out, pfim_out, sfre_out, sfim_out,
             xt_scr, u_scr, st_re, st_im, sl_re, sl_im, sp_re, sp_im, yt_scr, toe_ref, p_ref, q_ref, a16_scr,
             g_scr, m_scr):
    gb = S5_GROUPS_PER_STEP
    n_p_rows = n_pb * n_pc
    row_s = n_p_rows
    row_m = row_s + n_sb
    t_p = n_p_rows * CHUNK
    rows_pad = xt_scr.shape[0]

    for k in range(gb):
        _s5_operators(k // 2, k % 2, z_ref, bb_ref, c_ref, toe_ref, p_ref, q_ref, a16_scr, g_scr, m_scr)

    xt_scr[row_m + 1:rows_pad, :] = jnp.zeros((rows_pad - row_m - 1, LANES), F32)
    for t in range(CHUNK):
        xt_scr[0:n_p_rows, :] = xs_ref[pl.ds(t, n_p_rows, stride=CHUNK), :]
        xt_scr[row_s:row_m, :] = xs_ref[pl.ds(t_p + t, n_sb, stride=CHUNK), :]
        xt_scr[row_m:row_m + 1, :] = xsmeta_ref[t:t + 1, :]
        xt = xt_scr[...].T.astype(BF16)
        for k in range(gb):
            u_scr[k, t * SSM_H:(t + 1) * SSM_H, :] = xt[k * SSM_H:(k + 1) * SSM_H, :]

    for k in range(gb):
        sl = jnp.dot(p_ref[k], u_scr[k], preferred_element_type=F32)
        half = (k % 2) * SSM_P
        st_re[k // 2, half:half + SSM_P, :] = sl[0:SSM_P, :]
        st_im[k // 2, half:half + SSM_P, :] = sl[SSM_P:2 * SSM_P, :]
    npair = S5_PAIRS_PER_STEP
    for j in range(npair):
        sl_re[j] = st_re[j].T
        sl_im[j] = st_im[j].T
        sp_re[j, row_m:rows_pad, :] = jnp.zeros((rows_pad - row_m, 2 * SSM_P), F32)
        sp_im[j, row_m:rows_pad, :] = jnp.zeros((rows_pad - row_m, 2 * SSM_P), F32)

    ar = [a16_scr[j, 0] for j in range(npair)]
    ai = [a16_scr[j, 1] for j in range(npair)]
    sre = [jnp.broadcast_to(sl_re[j, row_m:row_m + 1, :], (n_pb, 2 * SSM_P)) for j in range(npair)]
    sim = [jnp.broadcast_to(sl_im[j, row_m:row_m + 1, :], (n_pb, 2 * SSM_P)) for j in range(npair)]
    for c in range(n_pc):
        rows = pl.ds(c, n_pb, stride=n_pc)
        for j in range(npair):
            sp_re[j, rows, :] = sre[j]
            sp_im[j, rows, :] = sim[j]
            nre = ar[j] * sre[j] - ai[j] * sim[j] + sl_re[j, rows, :]
            nim = ar[j] * sim[j] + ai[j] * sre[j] + sl_im[j, rows, :]
            sre[j], sim[j] = nre, nim
    for j in range(npair):
        pfre_out[j] = sre[j]
        pfim_out[j] = sim[j]
        s0r, s0i = s0re_ref[j], s0im_ref[j]
        sp_re[j, row_s:row_m, :] = s0r
        sp_im[j, row_s:row_m, :] = s0i
        sfre_out[j] = ar[j] * s0r - ai[j] * s0i + sl_re[j, row_s:row_m, :]
        sfim_out[j] = ar[j] * s0i + ai[j] * s0r + sl_im[j, row_s:row_m, :]

    nt = (((1,), (1,)), ((), ()))
    for k in range(gb):
        y = jnp.dot(toe_ref[k], u_scr[k], preferred_element_type=F32)
        y += lax.dot_general(q_ref[k, 0], sp_re[k // 2].astype(BF16), nt, preferred_element_type=F32)
        y += lax.dot_general(q_ref[k, 1], sp_im[k // 2].astype(BF16), nt, preferred_element_type=F32)
        for t in range(CHUNK):
            yt_scr[t, k * SSM_H:(k + 1) * SSM_H, :] = y[t * SSM_H:(t + 1) * SSM_H, :]
    for t in range(CHUNK):
        yt = yt_scr[t].T
        y_out[pl.ds(t, n_p_rows, stride=CHUNK), :] = yt[0:n_p_rows, :]
        y_out[pl.ds(t_p + t, n_sb, stride=CHUNK), :] = yt[row_s:row_m, :]


def _s5(xs, xsmeta, state_re, state_im, mats, n_pb, n_sb, seq):
    z, bb, c = mats
    t = xs.shape[0]
    n_pc = seq // CHUNK
    rows = n_pc * n_pb + n_sb + 1
    rows_pad = -(-rows // LANES) * LANES
    gb, npair = S5_GROUPS_PER_STEP, S5_PAIRS_PER_STEP
    pairs = lambda s: jnp.transpose(s.reshape(n_sb, SSM_G // 2, 2 * SSM_P), (1, 0, 2))
    blk3 = lambda n, r, c: pl.BlockSpec((n, r, c), lambda i: (i, 0, 0))
    y, pfre, pfim, sfre, sfim = pl.pallas_call(
        functools.partial(_s5_body, n_pc, n_pb, n_sb),
        grid=(SSM_G // gb,),
        in_specs=[pl.BlockSpec((t, LANES), lambda i: (0, i)),
                  pl.BlockSpec((N_META, LANES), lambda i: (0, i)),
                  blk3(gb, 2, SSM_P),
                  pl.BlockSpec((gb, 2, SSM_P, SSM_H), lambda i: (i, 0, 0, 0)),
                  pl.BlockSpec((gb, 2, SSM_H, SSM_P), lambda i: (i, 0, 0, 0)),
                  blk3(npair, n_sb, 2 * SSM_P), blk3(npair, n_sb, 2 * SSM_P)],
        out_specs=[pl.BlockSpec((t, LANES), lambda i: (0, i)),
                   blk3(npair, n_pb, 2 * SSM_P), blk3(npair, n_pb, 2 * SSM_P),
                   blk3(npair, n_sb, 2 * SSM_P), blk3(npair, n_sb, 2 * SSM_P)],
        out_shape=[jax.ShapeDtypeStruct((t, D_SSM), F32),
                   jax.ShapeDtypeStruct((SSM_G // 2, n_pb, 2 * SSM_P), F32),
                   jax.ShapeDtypeStruct((SSM_G // 2, n_pb, 2 * SSM_P), F32),
                   jax.ShapeDtypeStruct((SSM_G // 2, n_sb, 2 * SSM_P), F32),
                   jax.ShapeDtypeStruct((SSM_G // 2, n_sb, 2 * SSM_P), F32)],
        scratch_shapes=[pltpu.VMEM((rows_pad, LANES), F32),
                        pltpu.VMEM((gb, CHUNK_W, rows_pad), BF16),
                        pltpu.VMEM((npair, 2 * SSM_P, rows_pad), F32),
                        pltpu.VMEM((npair, 2 * SSM_P, rows_pad), F32),
                        pltpu.VMEM((npair, rows_pad, 2 * SSM_P), F32),
                        pltpu.VMEM((npair, rows_pad, 2 * SSM_P), F32),
                        pltpu.VMEM((npair, rows_pad, 2 * SSM_P), F32),
                        pltpu.VMEM((npair, rows_pad, 2 * SSM_P), F32),
                        pltpu.VMEM((CHUNK, LANES, rows_pad), F32),
                        pltpu.VMEM((gb, CHUNK_W, CHUNK_W), BF16),
                        pltpu.VMEM((gb, 2 * SSM_P, CHUNK_W), BF16),
                        pltpu.VMEM((gb, 2, CHUNK_W, 2 * SSM_P), BF16),
                        pltpu.VMEM((npair, 2, 1, 2 * SSM_P), F32),
                        pltpu.VMEM((2, (CHUNK + 1) * SSM_H, SSM_P), F32),
                        pltpu.VMEM((CHUNK_W, CHUNK_W), F32)],
        compiler_params=pltpu.CompilerParams(dimension_semantics=("arbitrary",), vmem_limit_bytes=VMEM_LIMIT),
        name="s5_chunks",
    )(xs, xsmeta, z, bb, c, pairs(state_re), pairs(state_im))
    unpair = lambda a: jnp.transpose(a, (1, 0, 2)).reshape(a.shape[1], SSM_G, SSM_P)[None]
    return y, unpair(pfre), unpair(pfim), unpair(sfre), unpair(sfim)


def _mixers_body(n_p_tiles, tiles_per_seq, xb_ref, z_ref, y5_ref, xs_ref, sga_ref, sgb_ref,
                 zmeta_ref, inj1_ref, inj2_ref, cw_ref, dskip_ref,
                 wc_hbm, wg_hbm, wso_hbm, merged_out,
                 wc, wg, wso, stage_a, stage_b, sem, carry):
    i = pl.program_id(0)

    @pl.when(i == 0)
    def _():
        _load_weight_bf16(wc_hbm, wc, stage_a, sem)
        _load_weight_bf16(wg_hbm, wg, stage_b, sem)
        _load_weight_bf16(wso_hbm, wso, stage_a, sem)

    @pl.when(jnp.logical_and(i < n_p_tiles, i % tiles_per_seq == 0))
    def _():
        carry[0:2, :] = zmeta_ref[N_META - 2:N_META, :]

    z = z_ref[...]
    row = lax.broadcasted_iota(I32, (TM, 1), 0)
    is_s = i >= n_p_tiles
    r1 = pltpu.roll(z, 1, 0)
    r2 = pltpu.roll(z, 2, 0)
    c1 = carry[1:2, :]
    c2 = carry[0:1, :]
    pos = jnp.where(is_s, row & (CHUNK - 1), row)
    first1 = pos == 0
    first2 = pos < 2
    fill1 = jnp.where(is_s, inj1_ref[...], jnp.broadcast_to(c1, z.shape))
    fill2 = jnp.where(is_s, inj2_ref[...], jnp.where(row == 0, c2, c1))
    zp1 = jnp.where(first1, fill1, r1)
    zp2 = jnp.where(first2, fill2, r2)
    carry[0:2, :] = z[TM - 2:TM, :]

    cw = cw_ref[...]
    conv = cw[0:1, :] * zp2 + cw[1:2, :] * zp1 + cw[2:3, :] * z
    a_in = (xb_ref[...].astype(F32) * conv).astype(BF16)
    ya = jnp.dot(a_in, wc[...], preferred_element_type=F32)

    ys = y5_ref[...] + dskip_ref[...] * xs_ref[...]
    ys = _gelu_tanh(ys)
    glu = jnp.dot(ys.astype(BF16), wg[...], preferred_element_type=F32)
    ys = ys * _sigmoid(glu)
    yb = jnp.dot(ys.astype(BF16), wso[...], preferred_element_type=F32)

    merged = sga_ref[...].astype(F32) * ya + sgb_ref[...].astype(F32) * yb
    merged_out[...] = merged.astype(BF16)


def _mixers(xb, z, y5, xs, sga, sgb, zmeta, inj1, inj2, conv_w, d_skip, w_conv_out, w_glu, w_ssm_out,
            n_p_tiles, tiles_per_seq):
    t = xb.shape[0]
    n_s_tiles = inj1.shape[0] // TM
    row = lambda w: pl.BlockSpec((TM, w), lambda i: (i, 0))
    const = lambda r, w: pl.BlockSpec((r, w), lambda i: (0, 0))
    inj = pl.BlockSpec((TM, D_CONV), lambda i: (jnp.clip(i - n_p_tiles, 0, n_s_tiles - 1), 0))
    any_spec = pl.BlockSpec(memory_space=pl.ANY)
    return pl.pallas_call(
        functools.partial(_mixers_body, n_p_tiles, tiles_per_seq),
        grid=(t // TM,),
        in_specs=[row(D_CONV), row(D_CONV), row(D_SSM), row(D_SSM), row(D_MODEL), row(D_MODEL),
                  const(N_META, D_CONV), inj, inj, const(3, D_CONV), const(1, D_SSM),
                  any_spec, any_spec, any_spec],
        out_specs=row(D_MODEL),
        out_shape=jax.ShapeDtypeStruct((t, D_MODEL), BF16),
        scratch_shapes=[pltpu.VMEM((D_CONV, D_MODEL), BF16),
                        pltpu.VMEM((D_SSM, D_SSM), BF16),
                        pltpu.VMEM((D_SSM, D_MODEL), BF16),
                        pltpu.VMEM((2, 256, D_MODEL), F32),
                        pltpu.VMEM((2, 256, D_SSM), F32),
                        pltpu.SemaphoreType.DMA((2,)),
                        pltpu.VMEM((8, D_CONV), F32)],
        compiler_params=pltpu.CompilerParams(dimension_semantics=("arbitrary",), vmem_limit_bytes=VMEM_LIMIT),
        name="mixers",
    )(xb, z, y5, xs, sga, sgb, zmeta, inj1, inj2, conv_w, d_skip, w_conv_out, w_glu, w_ssm_out)


ROUTE_W = LANES
COARSE0 = N_EXPERTS


def _route(logits, cnt):
    col = lax.broadcasted_iota(I32, logits.shape, 1)
    colf = col.astype(F32)
    neg = jnp.float32(-jnp.inf)
    big = jnp.float32(1 << 20)
    is_c = jnp.logical_and(col >= COARSE0, col < COARSE0 + N_EGROUPS)
    lc = jnp.where(is_c, logits, neg)
    cmax = jnp.max(lc, axis=-1, keepdims=True)
    gi = jnp.min(jnp.where(lc == cmax, colf - COARSE0, big), axis=-1, keepdims=True)
    pg = 1.0 / jnp.sum(jnp.where(is_c, jnp.exp(lc - cmax), 0.0), axis=-1, keepdims=True)
    grp = (col >> 3).astype(F32)
    in_g = jnp.logical_and(col < N_EXPERTS, grp == gi)
    lf = jnp.where(in_g, logits, neg)
    m1 = jnp.max(lf, axis=-1, keepdims=True)
    i1 = jnp.min(jnp.where(lf == m1, colf, big), axis=-1, keepdims=True)
    lf2 = jnp.where(colf == i1, neg, lf)
    m2 = jnp.max(lf2, axis=-1, keepdims=True)
    i2 = jnp.min(jnp.where(lf2 == m2, colf, big), axis=-1, keepdims=True)
    e2 = jnp.exp(m2 - m1)
    w1 = pg / (1.0 + e2)
    w2 = pg * e2 / (1.0 + e2)
    n = logits.shape[0]
    hit1 = colf == i1
    hit2 = colf == i2
    onehot = jnp.where(jnp.logical_or(hit1, hit2), 1.0, 0.0)
    rr = lax.broadcasted_iota(I32, (n, n), 0)
    cc = lax.broadcasted_iota(I32, (n, n), 1)
    tri = jnp.where(cc < rr, 1.0, 0.0).astype(BF16)
    pos = jnp.dot(tri, onehot.astype(BF16), preferred_element_type=F32) + cnt
    rank1 = jnp.sum(jnp.where(hit1, pos, 0.0), axis=-1, keepdims=True)
    rank2 = jnp.sum(jnp.where(hit2, pos, 0.0), axis=-1, keepdims=True)
    vals = (i1, i2, w1, w2, rank1, rank2)
    rec = jnp.zeros(logits.shape, F32)
    for c, val in enumerate(vals):
        rec = jnp.where(col == c, val, rec)
    return rec, cnt + jnp.sum(onehot, axis=0, keepdims=True)


def _out_proj_body(n_p_tiles, merged_ref, xp_ref, xsm_ref, g2_ref, wr_ref, wo_hbm,
                   h2_out, v_out, rec_out, cnt_out, wo, stage, sem, h2_scr):
    i = pl.program_id(0)
    n = pl.num_programs(0) - 1

    @pl.when(i == 0)
    def _():
        _load_weight_bf16(wo_hbm, wo, stage, sem)
        cnt_out[...] = jnp.zeros(cnt_out.shape, F32)

    def route_prev():
        v = _rmsnorm(h2_scr[...], g2_ref[...])
        v_out[...] = _pack_halves(v)
        v_hi, v_lo = _split_bf16(v)
        both = jnp.dot(v_hi, wr_ref[...], preferred_element_type=F32)
        logits = (both[:, :ROUTE_W] + both[:, ROUTE_W:]
                  + jnp.dot(v_lo, wr_ref[:, :ROUTE_W], preferred_element_type=F32))
        rec, cnt = _route(logits, cnt_out[...])
        rec_out[...] = rec
        cnt_out[...] = cnt

    def project():
        x = jnp.where(i < n_p_tiles, xp_ref[...], xsm_ref[...])
        h2 = x + jnp.dot(merged_ref[...], wo[...], preferred_element_type=F32)
        h2_out[...] = h2
        return h2

    @pl.when(i == 0)
    def _():
        h2_scr[...] = project()

    @pl.when(jnp.logical_and(i > 0, i < n))
    def _():
        route_prev()
        h2_scr[...] = project()

    @pl.when(i == n)
    def _():
        route_prev()


def _out_proj(merged, xp, xsm, norm2, w_router, w_o):
    t_p, t_s = xp.shape[0], xsm.shape[0]
    n_p, n_s = t_p // TM, t_s // TM
    n = n_p + n_s
    t = t_p + t_s
    xp_spec, xs_spec = _two_stream_specs(n_p, n_s)
    cur = lambda w: pl.BlockSpec((TM, w), lambda i: (jnp.minimum(i, n - 1), 0))
    prev = lambda w: pl.BlockSpec((TM, w), lambda i: (jnp.maximum(i - 1, 0), 0))
    const = lambda r, w: pl.BlockSpec((r, w), lambda i: (0, 0))
    return pl.pallas_call(
        functools.partial(_out_proj_body, n_p),
        grid=(n + 1,),
        in_specs=[cur(D_MODEL), xp_spec, xs_spec, const(1, D_MODEL), const(D_MODEL, 2 * ROUTE_W),
                  pl.BlockSpec(memory_space=pl.ANY)],
        out_specs=[cur(D_MODEL), prev(HALF), prev(ROUTE_W), const(1, ROUTE_W)],
        out_shape=[jax.ShapeDtypeStruct((t, D_MODEL), F32),
                   jax.ShapeDtypeStruct((t, HALF), U32),
                   jax.ShapeDtypeStruct((t, ROUTE_W), F32),
                   jax.ShapeDtypeStruct((1, ROUTE_W), F32)],
        scratch_shapes=[pltpu.VMEM((D_MODEL, D_MODEL), BF16),
                        pltpu.VMEM((2, 256, D_MODEL), F32),
                        pltpu.SemaphoreType.DMA((2,)),
                        pltpu.VMEM((TM, D_MODEL), F32)],
        compiler_params=pltpu.CompilerParams(dimension_semantics=("arbitrary",), vmem_limit_bytes=VMEM_LIMIT),
        name="out_proj_route",
    )(merged, xp, xsm, norm2, w_router, w_o)


def _moe_plan(rec, cnt, n_tiles):
    t = rec.shape[0]
    n_pairs = 2 * t
    eid = rec[:, 0:2].astype(I32).reshape(-1)
    rank = rec[:, 4:6].astype(I32).reshape(-1)
    counts = cnt[0, :N_EXPERTS].astype(I32)
    pair_start = jnp.cumsum(counts) - counts
    experts = jnp.arange(N_EXPERTS, dtype=I32)
    onehot = (eid[:, None] == experts[None, :]).astype(I32)
    pos = rank + jnp.sum(onehot * pair_start[None, :], axis=1)
    _, order = lax.sort((pos, jnp.arange(n_pairs, dtype=I32)), num_keys=1)
    tiles_e = (counts + TM_MOE - 1) // TM_MOE
    tile_end = jnp.cumsum(tiles_e)
    tile_start = tile_end - tiles_e
    n_valid = tile_end[-1]
    tile_ids = jnp.arange(n_tiles, dtype=I32)
    tile_e = jnp.sum((tile_ids[:, None] >= tile_end[None, :]).astype(I32), axis=1)
    last_e = jnp.sum((n_valid - 1 >= tile_end).astype(I32))
    tile_e = jnp.minimum(jnp.where(tile_ids < n_valid, tile_e, last_e), N_EXPERTS - 1)
    tile_onehot = (tile_e[:, None] == experts[None, :]).astype(I32)
    tile_q0 = (tile_ids - jnp.sum(tile_onehot * tile_start[None, :], axis=1)) * TM_MOE
    tile_q0 = jnp.where(tile_ids < n_valid, tile_q0, 0)
    dst = rank + jnp.sum(onehot * (tile_start * TM_MOE)[None, :], axis=1)
    pad_start = tile_start * TM_MOE + counts
    pad_len = tiles_e * TM_MOE - counts
    return (dst.astype(I32), pad_start.astype(I32), pad_len.astype(I32), tile_e, tile_q0.astype(I32),
            n_valid.astype(I32).reshape(1), pair_start, counts, order)


DISPATCH_BUFS = 3
ROW_PIECES = tuple(TM_MOE >> (b + 1) for b in range(TM_MOE.bit_length() - 1))


def _dispatch_body(dst_ref, pad_start_ref, pad_len_ref, nvalid_ref, v_hbm, x_hbm, buf, zbuf, rsem, ssem, zsem):
    i = pl.program_id(0)
    n = pl.num_programs(0)

    def read(tile, slot):
        return pltpu.make_async_copy(v_hbm.at[pl.ds(pl.multiple_of(tile * TM, TM), TM)], buf.at[slot],
                                     rsem.at[slot])

    def row_write(slot, r, dst_row):
        return pltpu.make_async_copy(buf.at[slot, pl.ds(r, 1)], x_hbm.at[pl.ds(dst_row, 1)], ssem.at[slot])

    def drain(slot):
        for _ in range(2):
            pltpu.make_async_copy(buf.at[slot], x_hbm.at[pl.ds(0, TM)], ssem.at[slot]).wait()

    def pad_fill(go):
        def zero_rows(start, size):
            d = pltpu.make_async_copy(zbuf.at[pl.ds(0, size)], x_hbm.at[pl.ds(start, size)], zsem)
            d.start() if go else d.wait()

        def body(e, c):
            start, length = pad_start_ref[e], pad_len_ref[e]
            head = (-start) & (SUBLANES - 1)
            for h in range(SUBLANES - 1):
                @pl.when(h < head)
                def _(h=h):
                    zero_rows(start + h, 1)
            start, length = start + head, length - head
            for size in ROW_PIECES:
                if size >= SUBLANES:
                    @pl.when((length & size) != 0)
                    def _(size=size):
                        zero_rows(pl.multiple_of(start + (length & (-2 * size)), SUBLANES), size)
            return c
        lax.fori_loop(0, N_EXPERTS, body, 0)

        def unused(tile, c):
            for half in range(TM_MOE // ROW_PIECES[0]):
                zero_rows(pl.multiple_of(tile * TM_MOE + half * ROW_PIECES[0], SUBLANES), ROW_PIECES[0])
            return c
        lax.fori_loop(nvalid_ref[0], x_hbm.shape[0] // TM_MOE, unused, 0)

    @pl.when(i == 0)
    def _():
        zbuf[...] = jnp.zeros(zbuf.shape, U32)
        pad_fill(True)
        read(0, 0).start()

    @pl.when(i >= 2)
    def _():
        drain((i + 1) % DISPATCH_BUFS)

    @pl.when(i + 1 < n)
    def _():
        read(i + 1, (i + 1) % DISPATCH_BUFS).start()

    slot = i % DISPATCH_BUFS
    read(i, slot).wait()
    for r in range(TM):
        for k in range(2):
            row_write(slot, r, dst_ref[2 * (i * TM + r) + k]).start(priority=k)

    @pl.when(i == n - 1)
    def _():
        if n >= 2:
            drain((i - 1) % DISPATCH_BUFS)
        drain(slot)
        pad_fill(False)


def _dispatch(v, plan, n_tiles):
    dst, pad_start, pad_len, n_valid = plan[0], plan[1], plan[2], plan[5]
    t = v.shape[0]
    return pl.pallas_call(
        _dispatch_body,
        grid_spec=pltpu.PrefetchScalarGridSpec(
            num_scalar_prefetch=4,
            grid=(t // TM,),
            in_specs=[pl.BlockSpec(memory_space=pl.ANY)],
            out_specs=pl.BlockSpec(memory_space=pl.ANY),
            scratch_shapes=[pltpu.VMEM((DISPATCH_BUFS, TM, HALF), U32),
                            pltpu.VMEM((ROW_PIECES[0], HALF), U32),
                            pltpu.SemaphoreType.DMA((DISPATCH_BUFS,)),
                            pltpu.SemaphoreType.DMA((DISPATCH_BUFS,)),
                            pltpu.SemaphoreType.DMA(())]),
        out_shape=jax.ShapeDtypeStruct((n_tiles * TM_MOE, HALF), U32),
        compiler_params=pltpu.CompilerParams(dimension_semantics=("arbitrary",), vmem_limit_bytes=VMEM_LIMIT),
        name="moe_dispatch",
    )(dst, pad_start, pad_len, n_valid, v)


WEIGHT_SLOTS = 3


def _moe_body(n_tok, plane, tile_e_ref, tile_q0_ref, nvalid_ref, pstart_ref, cnt_ref, orow_ref,
              krank_ref, elist_ref, nexp_ref,
              x_ref, wg_hbm, wu_hbm, wd_hbm, o_hbm, ybuf0, ybuf1, ybuf2, ssem,
              stage_g, stage_u, stage_d, wsem, wg, wu, wd):
    i = pl.program_id(0)
    nv = nvalid_ref[0]
    bufs = (ybuf0, ybuf1, ybuf2)

    def scratch_row0(slot):
        return (slot & 1) * plane + n_tok + (slot >> 1) * TM_MOE

    def row_write(slot, r, dst_row):
        return pltpu.make_async_copy(bufs[slot].at[pl.ds(r, 1)], o_hbm.at[pl.ds(dst_row, 1)], ssem.at[slot])

    def scratch_rows(region):
        return pltpu.make_async_copy(ybuf0, o_hbm.at[pl.ds(scratch_row0(region), TM_MOE)], ssem.at[0])

    def start_writes(tile, slot):
        e = tile_e_ref[tile]
        valid = cnt_ref[e] - tile_q0_ref[tile]
        first = pstart_ref[e] + tile_q0_ref[tile]
        for r in range(TM_MOE):
            row_write(slot, r, jnp.where(r < valid, orow_ref[first + r], scratch_row0(slot) + r)).start(
                priority=r % 2)

    def compute(slot):
        x_lo, x_hi = (h.astype(BF16) for h in _unpack_halves(x_ref[...]))
        hg = (jnp.dot(x_lo, wg[0:HALF, :], preferred_element_type=F32)
              + jnp.dot(x_hi, wg[HALF:D_MODEL, :], preferred_element_type=F32))
        hu = (jnp.dot(x_lo, wu[0:HALF, :], preferred_element_type=F32)
              + jnp.dot(x_hi, wu[HALF:D_MODEL, :], preferred_element_type=F32))
        act = hg * _sigmoid(hg) * hu
        bufs[slot][...] = _pack_halves(jnp.dot(act.astype(BF16), wd[...], preferred_element_type=F32))

    @pl.when(i == 0)
    def _():
        ybuf0[...] = jnp.zeros(ybuf0.shape, U32)
        for region in range(4):
            scratch_rows(region).start()
        for region in range(4):
            scratch_rows(region).wait()

    @pl.when(jnp.logical_and(i >= 3, i < nv + 3))
    def _():
        pltpu.make_async_copy(ybuf0, o_hbm.at[pl.ds(0, TM_MOE)], ssem.at[i % 3]).wait()

    def weight_copies(k, go):
        e = elist_ref[k]
        slot = k % WEIGHT_SLOTS
        for w_hbm, st in ((wg_hbm, stage_g), (wu_hbm, stage_u), (wd_hbm, stage_d)):
            d = pltpu.make_async_copy(w_hbm.at[e], st.at[slot], wsem.at[slot])
            d.start() if go else d.wait()

    @pl.when(i == 0)
    def _():
        for k in range(WEIGHT_SLOTS):
            @pl.when(k < nexp_ref[0])
            def _(k=k):
                weight_copies(k, True)

    @pl.when(i < nv)
    def _():
        prev_e = tile_e_ref[jnp.maximum(i - 1, 0)]

        @pl.when(jnp.logical_or(i == 0, tile_e_ref[i] != prev_e))
        def _():
            k = krank_ref[tile_e_ref[i]]
            slot = k % WEIGHT_SLOTS
            weight_copies(k, False)
            wg[...] = stage_g[slot].astype(BF16)
            wu[...] = stage_u[slot].astype(BF16)
            wd[...] = stage_d[slot].astype(BF16)

            @pl.when(k + WEIGHT_SLOTS < nexp_ref[0])
            def _():
                weight_copies(k + WEIGHT_SLOTS, True)

    @pl.when(i == 0)
    def _():
        compute(0)

    for slot in range(3):
        prev = (slot + 2) % 3

        @pl.when(jnp.logical_and(i % 3 == slot, jnp.logical_and(i >= 1, i < nv)))
        def _(slot=slot, prev=prev):
            start_writes(i - 1, prev)
            compute(slot)

        @pl.when(jnp.logical_and(i % 3 == slot, i == nv))
        def _(prev=prev):
            start_writes(i - 1, prev)


def _moe(x_disp, plan, w_gate, w_up, w_down, n_tiles, n_tok):
    tile_e, tile_q0, n_valid, pair_start, counts, order = plan[3:9]
    plane = n_tok + 2 * TM_MOE
    orow = jnp.pad((order & 1) * plane + (order >> 1), (0, TM_MOE))
    present = (counts > 0).astype(I32)
    krank = jnp.cumsum(present) - present
    experts = jnp.arange(N_EXPERTS, dtype=I32)
    elist = jnp.sum(jnp.where((krank[None, :] == experts[:, None]) & (present[None, :] > 0), experts[None, :], 0),
                    axis=1).astype(I32)
    nexp = jnp.sum(present).astype(I32).reshape(1)
    tile = lambda i, nv: jnp.minimum(i, jnp.maximum(nv[0] - 1, 0))
    any_spec = pl.BlockSpec(memory_space=pl.ANY)
    ybuf = pltpu.VMEM((TM_MOE, HALF), U32)
    return pl.pallas_call(
        functools.partial(_moe_body, n_tok, plane),
        grid_spec=pltpu.PrefetchScalarGridSpec(
            num_scalar_prefetch=9,
            grid=(n_tiles + 3,),
            in_specs=[pl.BlockSpec((TM_MOE, HALF), lambda i, te, tq, nv, *_: (tile(i, nv), 0)),
                      any_spec, any_spec, any_spec],
            out_specs=pl.BlockSpec(memory_space=pl.ANY),
            scratch_shapes=[ybuf, ybuf, ybuf,
                            pltpu.SemaphoreType.DMA((3,)),
                            pltpu.VMEM((WEIGHT_SLOTS, D_MODEL, D_EXPERT), F32),
                            pltpu.VMEM((WEIGHT_SLOTS, D_MODEL, D_EXPERT), F32),
                            pltpu.VMEM((WEIGHT_SLOTS, D_EXPERT, D_MODEL), F32),
                            pltpu.SemaphoreType.DMA((WEIGHT_SLOTS,)),
                            pltpu.VMEM((D_MODEL, D_EXPERT), BF16),
                            pltpu.VMEM((D_MODEL, D_EXPERT), BF16),
                            pltpu.VMEM((D_EXPERT, D_MODEL), BF16)]),
        out_shape=jax.ShapeDtypeStruct((2 * plane, HALF), U32),
        compiler_params=pltpu.CompilerParams(dimension_semantics=("arbitrary",), vmem_limit_bytes=VMEM_LIMIT),
        name="moe_experts",
    )(tile_e, tile_q0, n_valid, pair_start, counts, orow, krank.astype(I32), elist, nexp,
      x_disp, w_gate, w_up, w_down)


def _combine_body(n_p_tiles, h2_ref, rec_ref, gf_ref, y1_ref, y2_ref, outp_ref, outs_ref):
    i = pl.program_id(0)
    rec = rec_ref[...]
    y1 = jnp.concatenate(_unpack_halves(y1_ref[...]), axis=-1)
    y2 = jnp.concatenate(_unpack_halves(y2_ref[...]), axis=-1)
    h = rec[:, 2:3] * y1 + rec[:, 3:4] * y2
    out = _rmsnorm(h2_ref[...] + h, gf_ref[...])

    @pl.when(i < n_p_tiles)
    def _():
        outp_ref[...] = out

    @pl.when(i >= n_p_tiles)
    def _():
        outs_ref[...] = out


def _combine(h2, rec, y_pairs, final_norm, t_p, t_s):
    n_p, n_s = t_p // TM, t_s // TM
    plane_tiles = y_pairs.shape[0] // 2 // TM
    return pl.pallas_call(
        functools.partial(_combine_body, n_p),
        grid=(n_p + n_s,),
        in_specs=[pl.BlockSpec((TM, D_MODEL), lambda i: (i, 0)),
                  pl.BlockSpec((TM, ROUTE_W), lambda i: (i, 0)),
                  pl.BlockSpec((1, D_MODEL), lambda i: (0, 0)),
                  pl.BlockSpec((TM, HALF), lambda i: (i, 0)),
                  pl.BlockSpec((TM, HALF), lambda i: (i + plane_tiles, 0))],
        out_specs=[pl.BlockSpec((TM, D_MODEL), lambda i: (jnp.minimum(i, n_p - 1), 0)),
                   pl.BlockSpec((TM, D_MODEL), lambda i: (jnp.clip(i - n_p, 0, n_s - 1), 0))],
        out_shape=[jax.ShapeDtypeStruct((t_p, D_MODEL), F32),
                   jax.ShapeDtypeStruct((t_s, D_MODEL), F32)],
        compiler_params=pltpu.CompilerParams(dimension_semantics=("arbitrary",), vmem_limit_bytes=VMEM_LIMIT),
        name="combine_norm",
    )(h2, rec, final_norm, y_pairs, y_pairs)


def kernel(x_prompt, x_sample, state_conv, state_ssm_re, state_ssm_im, meta_tokens, norm1, w_in, conv_w,
           lam_re, lam_im, log_dt, ssm_b_re, ssm_b_im, ssm_c_re, ssm_c_im, ssm_d, w_glu, w_conv_out,
           w_ssm_out, w_o, norm2, w_coarse, w_fine, w_gate, w_up, w_down, final_norm):
    n_pb, seq, _ = x_prompt.shape
    n_sb, dec_seq, _ = x_sample.shape
    assert dec_seq == CHUNK and seq % TM == 0 and (n_sb * dec_seq) % TM == 0 and N_META == CHUNK
    t_p, t_s = n_pb * seq, n_sb * dec_seq
    xp = x_prompt.reshape(t_p, D_MODEL)
    xsm = x_sample.reshape(t_s, D_MODEL)

    xb, z, xs, zmeta, xsmeta, sga, sgb = _in_proj(xp, xsm, meta_tokens, norm1, w_in[0])

    mats = _s5_chunk_mats(lam_re[0], lam_im[0], log_dt[0], ssm_b_re[0], ssm_b_im[0], ssm_c_re[0], ssm_c_im[0])
    y5, pf_re, pf_im, sf_re, sf_im = _s5(xs, xsmeta, state_ssm_re[0], state_ssm_im[0], mats, n_pb, n_sb, seq)

    buf = state_conv[0]
    zero = jnp.zeros((n_sb, dec_seq, D_CONV), F32)
    inj1 = zero.at[:, 0].set(buf[:, 1]).reshape(t_s, D_CONV)
    inj2 = zero.at[:, 0].set(buf[:, 0]).at[:, 1].set(buf[:, 1]).reshape(t_s, D_CONV)
    merged = _mixers(xb, z, y5, xs, sga, sgb, zmeta, inj1, inj2, conv_w[0], ssm_d, w_conv_out[0], w_glu[0],
                     w_ssm_out[0], t_p // TM, seq // TM)

    w_router = jnp.concatenate(
        [w_fine[0], w_coarse[0], jnp.zeros((D_MODEL, ROUTE_W - N_EXPERTS - N_EGROUPS), F32)], axis=1)
    w_router = jnp.concatenate(_split_bf16(w_router), axis=1)
    h2, v, rec, cnt = _out_proj(merged, xp, xsm, norm2, w_router, w_o[0])

    n_tiles = 2 * (t_p + t_s) // TM_MOE + N_EXPERTS
    plan = _moe_plan(rec, cnt, n_tiles)
    x_disp = _dispatch(v, plan, n_tiles)
    y_pairs = _moe(x_disp, plan, w_gate[0], w_up[0], w_down[0], n_tiles, t_p + t_s)
    y_p, y_s = _combine(h2, rec, y_pairs, final_norm.reshape(1, D_MODEL), t_p, t_s)

    new_conv_p = jnp.stack([z[(b + 1) * seq - 2:(b + 1) * seq] for b in range(n_pb)])
    new_conv_s = z[t_p:].reshape(n_sb, dec_seq, D_CONV)[:, dec_seq - 2:]
    return (y_p.reshape(n_pb, seq, D_MODEL), y_s.reshape(n_sb, dec_seq, D_MODEL),
            new_conv_p[None], pf_re, pf_im, new_conv_s[None], sf_re, sf_im)
```

```python
import functools

import jax
import jax.numpy as jnp
from jax import lax
from jax.experimental import pallas as pl
from jax.experimental.pallas import tpu as pltpu

F32 = jnp.float32
BF16 = jnp.bfloat16
I32 = jnp.int32
U32 = jnp.uint32

D_MODEL = 2048
D_CONV = 1024
D_SSM = 1024
SSM_H = 16
SSM_G = 64
SSM_P = 64
N_META = 16
N_EGROUPS = 4
EXPERTS_PER_GROUP = 8
N_EXPERTS = 32
D_EXPERT = 256
EPS = 1e-6

LANES = 128
SUBLANES = 8

CHUNK = 16
CHUNK_W = CHUNK * SSM_H

TM = 256
TM_MOE = 256
VMEM_LIMIT = 52 * 1024 * 1024


def _rmsnorm(x, g):
    return x * lax.rsqrt(jnp.mean(x * x, axis=-1, keepdims=True) + EPS) * g


def _sigmoid(x):
    return 1.0 / (1.0 + jnp.exp(-x))


def _gelu_tanh(x):
    c = 0.7978845608028654
    return 0.5 * x * (1.0 + jnp.tanh(c * (x + 0.044715 * (x * x * x))))


def _split_bf16(a):
    hi = a.astype(BF16)
    lo = (a - hi.astype(F32)).astype(BF16)
    return hi, lo


def _dot3(a, b):
    a_hi, a_lo = _split_bf16(a)
    b_hi, b_lo = _split_bf16(b)
    return (jnp.dot(a_hi, b_hi, preferred_element_type=F32) + jnp.dot(a_lo, b_hi, preferred_element_type=F32)
            + jnp.dot(a_hi, b_lo, preferred_element_type=F32))


HALF = D_MODEL // 2


def _pack_halves(a):
    return pltpu.pack_elementwise([a[:, :HALF], a[:, HALF:]], packed_dtype=BF16)


def _unpack_halves(p):
    return (pltpu.unpack_elementwise(p, index=0, packed_dtype=BF16, unpacked_dtype=F32),
            pltpu.unpack_elementwise(p, index=1, packed_dtype=BF16, unpacked_dtype=F32))


def _weight_copy(w_hbm, stage, sem, c, slot, rows, col0, ncols):
    return pltpu.make_async_copy(
        w_hbm.at[pl.ds(c * rows, rows), pl.ds(col0, ncols)], stage.at[slot], sem.at[slot])


def _load_weight_bf16(w_hbm, w_vmem, stage, sem, col0=0):
    k, n = w_vmem.shape
    rows = stage.shape[1]
    nchunk = k // rows
    _weight_copy(w_hbm, stage, sem, 0, 0, rows, col0, n).start()
    for c in range(nchunk):
        slot = c % 2
        if c + 1 < nchunk:
            _weight_copy(w_hbm, stage, sem, c + 1, 1 - slot, rows, col0, n).start()
        _weight_copy(w_hbm, stage, sem, c, slot, rows, col0, n).wait()
        w_vmem[pl.ds(c * rows, rows), :] = stage[slot].astype(BF16)


def _in_proj_mix_body(n_p_tiles, xp_ref, xsm_ref, meta_ref, g_ref, w_hbm,
                      xb_out, z_out, xs_out, zmeta_out, xsmeta_out,
                      w_vmem, stage, sem):
    i = pl.program_id(0)
    g = g_ref[...]

    def project(u):
        xb = jnp.dot(u, w_vmem[:, 0:D_CONV], preferred_element_type=F32)
        xc = jnp.dot(u, w_vmem[:, D_CONV:2 * D_CONV], preferred_element_type=F32)
        xv = jnp.dot(u, w_vmem[:, 2 * D_CONV:3 * D_CONV], preferred_element_type=F32)
        xs = jnp.dot(u, w_vmem[:, 3 * D_CONV:3 * D_CONV + D_SSM], preferred_element_type=F32)
        return xb, xc * xv, xs

    @pl.when(i == 0)
    def _():
        _load_weight_bf16(w_hbm, w_vmem, stage, sem, col0=0)
        um = _rmsnorm(meta_ref[...], g).astype(BF16)
        _, zm, xsm = project(um)
        zmeta_out[...] = zm
        xsmeta_out[...] = xsm

    x = jnp.where(i < n_p_tiles, xp_ref[...], xsm_ref[...])
    u = _rmsnorm(x, g).astype(BF16)
    xb, z, xs = project(u)
    xb_out[...] = xb.astype(BF16)
    z_out[...] = z
    xs_out[...] = xs


def _in_proj_gate_body(n_p_tiles, xp_ref, xsm_ref, g_ref, w_hbm, ga_out, gb_out,
                       w_vmem, stage, sem):
    i = pl.program_id(0)

    @pl.when(i == 0)
    def _():
        _load_weight_bf16(w_hbm, w_vmem, stage, sem, col0=3 * D_CONV + D_SSM)

    x = jnp.where(i < n_p_tiles, xp_ref[...], xsm_ref[...])
    u = _rmsnorm(x, g_ref[...]).astype(BF16)
    ga = jnp.dot(u, w_vmem[:, 0:D_MODEL], preferred_element_type=F32)
    ga_out[...] = _sigmoid(ga).astype(BF16)
    gb = jnp.dot(u, w_vmem[:, D_MODEL:2 * D_MODEL], preferred_element_type=F32)
    gb_out[...] = _sigmoid(gb).astype(BF16)


def _two_stream_specs(n_p_tiles, n_s_tiles, tm=TM):
    xp_spec = pl.BlockSpec((tm, D_MODEL), lambda i: (jnp.minimum(i, n_p_tiles - 1), 0))
    mode = pl.Buffered(1) if n_s_tiles == 1 else None
    xs_spec = pl.BlockSpec((tm, D_MODEL), lambda i: (jnp.clip(i - n_p_tiles, 0, n_s_tiles - 1), 0),
                           pipeline_mode=mode)
    return xp_spec, xs_spec


TM_IN = TM


def _in_proj(xp, xsm, meta, norm1, w_in):
    t_p, t_s = xp.shape[0], xsm.shape[0]
    n_p, n_s = t_p // TM_IN, t_s // TM_IN
    t = t_p + t_s
    half = 3 * D_CONV + D_SSM
    xp_spec, xs_spec = _two_stream_specs(n_p, n_s, TM_IN)
    g_spec = pl.BlockSpec((1, D_MODEL), lambda i: (0, 0))
    any_spec = pl.BlockSpec(memory_space=pl.ANY)
    stage_rows = 128
    row = lambda w: pl.BlockSpec((TM_IN, w), lambda i: (i, 0))
    const = lambda r, w: pl.BlockSpec((r, w), lambda i: (0, 0))
    params = pltpu.CompilerParams(dimension_semantics=("arbitrary",), vmem_limit_bytes=VMEM_LIMIT)
    scratch = [pltpu.VMEM((D_MODEL, half), BF16),
               pltpu.VMEM((2, stage_rows, half), F32),
               pltpu.SemaphoreType.DMA((2,))]

    xb, z, xs, zmeta, xsmeta = pl.pallas_call(
        functools.partial(_in_proj_mix_body, n_p),
        grid=(n_p + n_s,),
        in_specs=[xp_spec, xs_spec, const(N_META, D_MODEL), g_spec, any_spec],
        out_specs=[row(D_CONV), row(D_CONV), row(D_SSM), const(N_META, D_CONV), const(N_META, D_SSM)],
        out_shape=[jax.ShapeDtypeStruct((t, D_CONV), BF16),
                   jax.ShapeDtypeStruct((t, D_CONV), F32),
                   jax.ShapeDtypeStruct((t, D_SSM), F32),
                   jax.ShapeDtypeStruct((N_META, D_CONV), F32),
                   jax.ShapeDtypeStruct((N_META, D_SSM), F32)],
        scratch_shapes=scratch,
        compiler_params=params,
        name="in_proj_mix",
    )(xp, xsm, meta, norm1, w_in)

    sga, sgb = pl.pallas_call(
        functools.partial(_in_proj_gate_body, n_p),
        grid=(n_p + n_s,),
        in_specs=[xp_spec, xs_spec, g_spec, any_spec],
        out_specs=[row(D_MODEL), row(D_MODEL)],
        out_shape=[jax.ShapeDtypeStruct((t, D_MODEL), BF16),
                   jax.ShapeDtypeStruct((t, D_MODEL), BF16)],
        scratch_shapes=scratch,
        compiler_params=params,
        name="in_proj_gate",
    )(xp, xsm, norm1, w_in)
    return xb, z, xs, zmeta, xsmeta, sga, sgb


S5_GROUPS_PER_STEP = LANES // SSM_H
S5_PAIRS_PER_STEP = S5_GROUPS_PER_STEP // 2


def _s5_chunk_mats(lam_re, lam_im, log_dt, b_re, b_im, c_re, c_im):
    dt = jnp.exp(log_dt)[:, None]
    lr, li = lam_re, lam_im
    z = jnp.stack([lr * dt, li * dt], axis=1)
    mag = jnp.exp(lr * dt)
    ab_re, ab_im = mag * jnp.cos(li * dt), mag * jnp.sin(li * dt)
    nr, ni = ab_re - 1.0, ab_im
    den = lr * lr + li * li
    k_re = (nr * lr + ni * li) / den
    k_im = (ni * lr - nr * li) / den
    bb = jnp.stack([k_re[..., None] * b_re - k_im[..., None] * b_im,
                    k_re[..., None] * b_im + k_im[..., None] * b_re], axis=1)
    return z, bb, jnp.stack([c_re, c_im], axis=1)


def _cmul(ar, ai, br, bi):
    return ar * br - ai * bi, ar * bi + ai * br


def _s5_operators(pair, parity, z_ref, bb_ref, c_ref, toe_scr, p_scr, q_scr, a16_scr, g_scr, m_scr):
    k = 2 * pair + parity
    zr, zi = z_ref[k, 0:1, :], z_ref[k, 1:2, :]
    mag = jnp.exp(zr)
    ar, ai = mag * jnp.cos(zi), mag * jnp.sin(zi)
    eye = lax.broadcasted_iota(I32, (SSM_P, SSM_P), 0) == lax.broadcasted_iota(I32, (SSM_P, SSM_P), 1)
    acr = jnp.sum(jnp.where(eye, ar, 0.0), axis=1, keepdims=True)
    aci = jnp.sum(jnp.where(eye, ai, 0.0), axis=1, keepdims=True)
    c_re, c_im = c_ref[k, 0], c_ref[k, 1]
    bb_re, bb_im = bb_ref[k, 0], bb_ref[k, 1]

    blk = lax.broadcasted_iota(I32, (1, CHUNK_W), 1) >> 4

    pr, pi = jnp.ones_like(ar), jnp.zeros_like(ar)
    pcr, pci = jnp.ones_like(acr), jnp.zeros_like(acr)
    pw_re = jnp.zeros((SSM_P, CHUNK_W), F32)
    pw_im = jnp.zeros((SSM_P, CHUNK_W), F32)
    for d in range(CHUNK + 1):
        g_scr[0, d * SSM_H:(d + 1) * SSM_H, :] = c_re * pr - c_im * pi
        g_scr[1, d * SSM_H:(d + 1) * SSM_H, :] = c_re * pi + c_im * pr
        if d < CHUNK:
            pw_re = jnp.where(blk == CHUNK - 1 - d, pcr, pw_re)
            pw_im = jnp.where(blk == CHUNK - 1 - d, pci, pw_im)
            pcr, pci = _cmul(pcr, pci, acr, aci)
            pr, pi = _cmul(pr, pi, ar, ai)

    half = parity * SSM_P
    a16_scr[pair, 0, :, half:half + SSM_P] = pr
    a16_scr[pair, 1, :, half:half + SSM_P] = pi
    q_scr[k] = jnp.zeros(q_scr.shape[1:], BF16)
    q_scr[k, 0, :, half:half + SSM_P] = g_scr[0, SSM_H:, :].astype(BF16)
    q_scr[k, 1, :, half:half + SSM_P] = (-g_scr[1, SSM_H:, :]).astype(BF16)

    rep = jnp.where(lax.broadcasted_iota(I32, (SSM_H, CHUNK_W), 0)
                    == (lax.broadcasted_iota(I32, (SSM_H, CHUNK_W), 1) & (SSM_H - 1)), 1.0, 0.0).astype(BF16)

    def widen(b):
        b_hi, b_lo = _split_bf16(b)
        return (jnp.dot(b_hi, rep, preferred_element_type=F32) + jnp.dot(b_lo, rep, preferred_element_type=F32))

    bw_re, bw_im = widen(bb_re), widen(bb_im)
    p_re, p_im = _cmul(pw_re, pw_im, bw_re, bw_im)
    p_scr[k, 0:SSM_P, :] = p_re.astype(BF16)
    p_scr[k, SSM_P:2 * SSM_P, :] = p_im.astype(BF16)

    m_scr[...] = _dot3(g_scr[0, 0:CHUNK_W, :], bw_re) - _dot3(g_scr[1, 0:CHUNK_W, :], bw_im)
    for t in range(CHUNK):
        acc = jnp.where(blk == 0, m_scr[t * SSM_H:(t + 1) * SSM_H, :], 0.0)
        for j in range(1, t + 1):
            acc = jnp.where(blk == j, m_scr[(t - j) * SSM_H:(t - j + 1) * SSM_H, :], acc)
        toe_scr[k, t * SSM_H:(t + 1) * SSM_H, :] = acc.astype(BF16)


def _s5_body(n_pc, n_pb, n_sb, xs_ref, xsmeta_ref, z_ref, bb_ref, c_ref, s0re_ref, s0im_ref,
             y_out, pfre_out, pfim_out, sfre_out, sfim_out,
             xt_scr, u_scr, st_re, st_im, sl_re, sl_im, sp_re, sp_im, yt_scr, toe_ref, p_ref, q_ref, a16_scr,
             g_scr, m_scr):
    gb = S5_GROUPS_PER_STEP
    n_p_rows = n_pb * n_pc
    row_s = n_p_rows
    row_m = row_s + n_sb
    t_p = n_p_rows * CHUNK
    rows_pad = xt_scr.shape[0]

    for k in range(gb):
        _s5_operators(k // 2, k % 2, z_ref, bb_ref, c_ref, toe_ref, p_ref, q_ref, a16_scr, g_scr, m_scr)

    xt_scr[row_m + 1:rows_pad, :] = jnp.zeros((rows_pad - row_m - 1, LANES), F32)
    for t in range(CHUNK):
        xt_scr[0:n_p_rows, :] = xs_ref[pl.ds(t, n_p_rows, stride=CHUNK), :]
        xt_scr[row_s:row_m, :] = xs_ref[pl.ds(t_p + t, n_sb, stride=CHUNK), :]
        xt_scr[row_m:row_m + 1, :] = xsmeta_ref[t:t + 1, :]
        xt = xt_scr[...].T.astype(BF16)
        for k in range(gb):
            u_scr[k, t * SSM_H:(t + 1) * SSM_H, :] = xt[k * SSM_H:(k + 1) * SSM_H, :]

    for k in range(gb):
        sl = jnp.dot(p_ref[k], u_scr[k], preferred_element_type=F32)
        half = (k % 2) * SSM_P
        st_re[k // 2, half:half + SSM_P, :] = sl[0:SSM_P, :]
        st_im[k // 2, half:half + SSM_P, :] = sl[SSM_P:2 * SSM_P, :]
    npair = S5_PAIRS_PER_STEP
    for j in range(npair):
        sl_re[j] = st_re[j].T
        sl_im[j] = st_im[j].T
        sp_re[j, row_m:rows_pad, :] = jnp.zeros((rows_pad - row_m, 2 * SSM_P), F32)
        sp_im[j, row_m:rows_pad, :] = jnp.zeros((rows_pad - row_m, 2 * SSM_P), F32)

    ar = [a16_scr[j, 0] for j in range(npair)]
    ai = [a16_scr[j, 1] for j in range(npair)]
    sre = [jnp.broadcast_to(sl_re[j, row_m:row_m + 1, :], (n_pb, 2 * SSM_P)) for j in range(npair)]
    sim = [jnp.broadcast_to(sl_im[j, row_m:row_m + 1, :], (n_pb, 2 * SSM_P)) for j in range(npair)]
    for c in range(n_pc):
        rows = pl.ds(c, n_pb, stride=n_pc)
        for j in range(npair):
            sp_re[j, rows, :] = sre[j]
            sp_im[j, rows, :] = sim[j]
            nre = ar[j] * sre[j] - ai[j] * sim[j] + sl_re[j, rows, :]
            nim = ar[j] * sim[j] + ai[j] * sre[j] + sl_im[j, rows, :]
            sre[j], sim[j] = nre, nim
    for j in range(npair):
        pfre_out[j] = sre[j]
        pfim_out[j] = sim[j]
        s0r, s0i = s0re_ref[j], s0im_ref[j]
        sp_re[j, row_s:row_m, :] = s0r
        sp_im[j, row_s:row_m, :] = s0i
        sfre_out[j] = ar[j] * s0r - ai[j] * s0i + sl_re[j, row_s:row_m, :]
        sfim_out[j] = ar[j] * s0i + ai[j] * s0r + sl_im[j, row_s:row_m, :]

    nt = (((1,), (1,)), ((), ()))
    for k in range(gb):
        y = jnp.dot(toe_ref[k], u_scr[k], preferred_element_type=F32)
        y += lax.dot_general(q_ref[k, 0], sp_re[k // 2].astype(BF16), nt, preferred_element_type=F32)
        y += lax.dot_general(q_ref[k, 1], sp_im[k // 2].astype(BF16), nt, preferred_element_type=F32)
        for t in range(CHUNK):
            yt_scr[t, k * SSM_H:(k + 1) * SSM_H, :] = y[t * SSM_H:(t + 1) * SSM_H, :]
    for t in range(CHUNK):
        yt = yt_scr[t].T
        y_out[pl.ds(t, n_p_rows, stride=CHUNK), :] = yt[0:n_p_rows, :]
        y_out[pl.ds(t_p + t, n_sb, stride=CHUNK), :] = yt[row_s:row_m, :]


def _s5(xs, xsmeta, state_re, state_im, mats, n_pb, n_sb, seq):
    z, bb, c = mats
    t = xs.shape[0]
    n_pc = seq // CHUNK
    rows = n_pc * n_pb + n_sb + 1
    rows_pad = -(-rows // LANES) * LANES
    gb, npair = S5_GROUPS_PER_STEP, S5_PAIRS_PER_STEP
    pairs = lambda s: jnp.transpose(s.reshape(n_sb, SSM_G // 2, 2 * SSM_P), (1, 0, 2))
    blk3 = lambda n, r, c: pl.BlockSpec((n, r, c), lambda i: (i, 0, 0))
    y, pfre, pfim, sfre, sfim = pl.pallas_call(
        functools.partial(_s5_body, n_pc, n_pb, n_sb),
        grid=(SSM_G // gb,),
        in_specs=[pl.BlockSpec((t, LANES), lambda i: (0, i)),
                  pl.BlockSpec((N_META, LANES), lambda i: (0, i)),
                  blk3(gb, 2, SSM_P),
                  pl.BlockSpec((gb, 2, SSM_P, SSM_H), lambda i: (i, 0, 0, 0)),
                  pl.BlockSpec((gb, 2, SSM_H, SSM_P), lambda i: (i, 0, 0, 0)),
                  blk3(npair, n_sb, 2 * SSM_P), blk3(npair, n_sb, 2 * SSM_P)],
        out_specs=[pl.BlockSpec((t, LANES), lambda i: (0, i)),
                   blk3(npair, n_pb, 2 * SSM_P), blk3(npair, n_pb, 2 * SSM_P),
                   blk3(npair, n_sb, 2 * SSM_P), blk3(npair, n_sb, 2 * SSM_P)],
        out_shape=[jax.ShapeDtypeStruct((t, D_SSM), F32),
                   jax.ShapeDtypeStruct((SSM_G // 2, n_pb, 2 * SSM_P), F32),
                   jax.ShapeDtypeStruct((SSM_G // 2, n_pb, 2 * SSM_P), F32),
                   jax.ShapeDtypeStruct((SSM_G // 2, n_sb, 2 * SSM_P), F32),
                   jax.ShapeDtypeStruct((SSM_G // 2, n_sb, 2 * SSM_P), F32)],
        scratch_shapes=[pltpu.VMEM((rows_pad, LANES), F32),
                        pltpu.VMEM((gb, CHUNK_W, rows_pad), BF16),
                        pltpu.VMEM((npair, 2 * SSM_P, rows_pad), F32),
                        pltpu.VMEM((npair, 2 * SSM_P, rows_pad), F32),
                        pltpu.VMEM((npair, rows_pad, 2 * SSM_P), F32),
                        pltpu.VMEM((npair, rows_pad, 2 * SSM_P), F32),
                        pltpu.VMEM((npair, rows_pad, 2 * SSM_P), F32),
                        pltpu.VMEM((npair, rows_pad, 2 * SSM_P), F32),
                        pltpu.VMEM((CHUNK, LANES, rows_pad), F32),
                        pltpu.VMEM((gb, CHUNK_W, CHUNK_W), BF16),
                        pltpu.VMEM((gb, 2 * SSM_P, CHUNK_W), BF16),
                        pltpu.VMEM((gb, 2, CHUNK_W, 2 * SSM_P), BF16),
                        pltpu.VMEM((npair, 2, 1, 2 * SSM_P), F32),
                        pltpu.VMEM((2, (CHUNK + 1) * SSM_H, SSM_P), F32),
                        pltpu.VMEM((CHUNK_W, CHUNK_W), F32)],
        compiler_params=pltpu.CompilerParams(dimension_semantics=("arbitrary",), vmem_limit_bytes=VMEM_LIMIT),
        name="s5_chunks",
    )(xs, xsmeta, z, bb, c, pairs(state_re), pairs(state_im))
    unpair = lambda a: jnp.transpose(a, (1, 0, 2)).reshape(a.shape[1], SSM_G, SSM_P)[None]
    return y, unpair(pfre), unpair(pfim), unpair(sfre), unpair(sfim)


def _mixers_body(n_p_tiles, tiles_per_seq, xb_ref, z_ref, y5_ref, xs_ref, sga_ref, sgb_ref,
                 zmeta_ref, inj1_ref, inj2_ref, cw_ref, dskip_ref,
                 wc_hbm, wg_hbm, wso_hbm, merged_out,
                 wc, wg, wso, stage_a, stage_b, sem, carry):
    i = pl.program_id(0)

    @pl.when(i == 0)
    def _():
        _load_weight_bf16(wc_hbm, wc, stage_a, sem)
        _load_weight_bf16(wg_hbm, wg, stage_b, sem)
        _load_weight_bf16(wso_hbm, wso, stage_a, sem)

    @pl.when(jnp.logical_and(i < n_p_tiles, i % tiles_per_seq == 0))
    def _():
        carry[0:2, :] = zmeta_ref[N_META - 2:N_META, :]

    z = z_ref[...]
    row = lax.broadcasted_iota(I32, (TM, 1), 0)
    is_s = i >= n_p_tiles
    r1 = pltpu.roll(z, 1, 0)
    r2 = pltpu.roll(z, 2, 0)
    c1 = carry[1:2, :]
    c2 = carry[0:1, :]
    pos = jnp.where(is_s, row & (CHUNK - 1), row)
    first1 = pos == 0
    first2 = pos < 2
    fill1 = jnp.where(is_s, inj1_ref[...], jnp.broadcast_to(c1, z.shape))
    fill2 = jnp.where(is_s, inj2_ref[...], jnp.where(row == 0, c2, c1))
    zp1 = jnp.where(first1, fill1, r1)
    zp2 = jnp.where(first2, fill2, r2)
    carry[0:2, :] = z[TM - 2:TM, :]

    cw = cw_ref[...]
    conv = cw[0:1, :] * zp2 + cw[1:2, :] * zp1 + cw[2:3, :] * z
    a_in = (xb_ref[...].astype(F32) * conv).astype(BF16)
    ya = jnp.dot(a_in, wc[...], preferred_element_type=F32)

    ys = y5_ref[...] + dskip_ref[...] * xs_ref[...]
    ys = _gelu_tanh(ys)
    glu = jnp.dot(ys.astype(BF16), wg[...], preferred_element_type=F32)
    ys = ys * _sigmoid(glu)
    yb = jnp.dot(ys.astype(BF16), wso[...], preferred_element_type=F32)

    merged = sga_ref[...].astype(F32) * ya + sgb_ref[...].astype(F32) * yb
    merged_out[...] = merged.astype(BF16)


def _mixers(xb, z, y5, xs, sga, sgb, zmeta, inj1, inj2, conv_w, d_skip, w_conv_out, w_glu, w_ssm_out,
            n_p_tiles, tiles_per_seq):
    t = xb.shape[0]
    n_s_tiles = inj1.shape[0] // TM
    row = lambda w: pl.BlockSpec((TM, w), lambda i: (i, 0))
    const = lambda r, w: pl.BlockSpec((r, w), lambda i: (0, 0))
    inj = pl.BlockSpec((TM, D_CONV), lambda i: (jnp.clip(i - n_p_tiles, 0, n_s_tiles - 1), 0))
    any_spec = pl.BlockSpec(memory_space=pl.ANY)
    return pl.pallas_call(
        functools.partial(_mixers_body, n_p_tiles, tiles_per_seq),
        grid=(t // TM,),
        in_specs=[row(D_CONV), row(D_CONV), row(D_SSM), row(D_SSM), row(D_MODEL), row(D_MODEL),
                  const(N_META, D_CONV), inj, inj, const(3, D_CONV), const(1, D_SSM),
                  any_spec, any_spec, any_spec],
        out_specs=row(D_MODEL),
        out_shape=jax.ShapeDtypeStruct((t, D_MODEL), BF16),
        scratch_shapes=[pltpu.VMEM((D_CONV, D_MODEL), BF16),
                        pltpu.VMEM((D_SSM, D_SSM), BF16),
                        pltpu.VMEM((D_SSM, D_MODEL), BF16),
                        pltpu.VMEM((2, 256, D_MODEL), F32),
                        pltpu.VMEM((2, 256, D_SSM), F32),
                        pltpu.SemaphoreType.DMA((2,)),
                        pltpu.VMEM((8, D_CONV), F32)],
        compiler_params=pltpu.CompilerParams(dimension_semantics=("arbitrary",), vmem_limit_bytes=VMEM_LIMIT),
        name="mixers",
    )(xb, z, y5, xs, sga, sgb, zmeta, inj1, inj2, conv_w, d_skip, w_conv_out, w_glu, w_ssm_out)


ROUTE_W = LANES
COARSE0 = N_EXPERTS


def _route(logits, cnt):
    col = lax.broadcasted_iota(I32, logits.shape, 1)
    colf = col.astype(F32)
    neg = jnp.float32(-jnp.inf)
    big = jnp.float32(1 << 20)
    is_c = jnp.logical_and(col >= COARSE0, col < COARSE0 + N_EGROUPS)
    lc = jnp.where(is_c, logits, neg)
    cmax = jnp.max(lc, axis=-1, keepdims=True)
    gi = jnp.min(jnp.where(lc == cmax, colf - COARSE0, big), axis=-1, keepdims=True)
    pg = 1.0 / jnp.sum(jnp.where(is_c, jnp.exp(lc - cmax), 0.0), axis=-1, keepdims=True)
    grp = (col >> 3).astype(F32)
    in_g = jnp.logical_and(col < N_EXPERTS, grp == gi)
    lf = jnp.where(in_g, logits, neg)
    m1 = jnp.max(lf, axis=-1, keepdims=True)
    i1 = jnp.min(jnp.where(lf == m1, colf, big), axis=-1, keepdims=True)
    lf2 = jnp.where(colf == i1, neg, lf)
    m2 = jnp.max(lf2, axis=-1, keepdims=True)
    i2 = jnp.min(jnp.where(lf2 == m2, colf, big), axis=-1, keepdims=True)
    e2 = jnp.exp(m2 - m1)
    w1 = pg / (1.0 + e2)
    w2 = pg * e2 / (1.0 + e2)
    n = logits.shape[0]
    hit1 = colf == i1
    hit2 = colf == i2
    onehot = jnp.where(jnp.logical_or(hit1, hit2), 1.0, 0.0)
    rr = lax.broadcasted_iota(I32, (n, n), 0)
    cc = lax.broadcasted_iota(I32, (n, n), 1)
    tri = jnp.where(cc < rr, 1.0, 0.0).astype(BF16)
    pos = jnp.dot(tri, onehot.astype(BF16), preferred_element_type=F32) + cnt
    rank1 = jnp.sum(jnp.where(hit1, pos, 0.0), axis=-1, keepdims=True)
    rank2 = jnp.sum(jnp.where(hit2, pos, 0.0), axis=-1, keepdims=True)
    vals = (i1, i2, w1, w2, rank1, rank2)
    rec = jnp.zeros(logits.shape, F32)
    for c, val in enumerate(vals):
        rec = jnp.where(col == c, val, rec)
    return rec, cnt + jnp.sum(onehot, axis=0, keepdims=True)


def _out_proj_body(n_p_tiles, merged_ref, xp_ref, xsm_ref, g2_ref, wr_ref, wo_hbm,
                   h2_out, v_out, rec_out, cnt_out, wo, stage, sem, h2_scr):
    i = pl.program_id(0)
    n = pl.num_programs(0) - 1

    @pl.when(i == 0)
    def _():
        _load_weight_bf16(wo_hbm, wo, stage, sem)
        cnt_out[...] = jnp.zeros(cnt_out.shape, F32)

    def route_prev():
        v = _rmsnorm(h2_scr[...], g2_ref[...])
        v_out[...] = _pack_halves(v)
        v_hi, v_lo = _split_bf16(v)
        both = jnp.dot(v_hi, wr_ref[...], preferred_element_type=F32)
        logits = (both[:, :ROUTE_W] + both[:, ROUTE_W:]
                  + jnp.dot(v_lo, wr_ref[:, :ROUTE_W], preferred_element_type=F32))
        rec, cnt = _route(logits, cnt_out[...])
        rec_out[...] = rec
        cnt_out[...] = cnt

    def project():
        x = jnp.where(i < n_p_tiles, xp_ref[...], xsm_ref[...])
        h2 = x + jnp.dot(merged_ref[...], wo[...], preferred_element_type=F32)
        h2_out[...] = h2
        return h2

    @pl.when(i == 0)
    def _():
        h2_scr[...] = project()

    @pl.when(jnp.logical_and(i > 0, i < n))
    def _():
        route_prev()
        h2_scr[...] = project()

    @pl.when(i == n)
    def _():
        route_prev()


def _out_proj(merged, xp, xsm, norm2, w_router, w_o):
    t_p, t_s = xp.shape[0], xsm.shape[0]
    n_p, n_s = t_p // TM, t_s // TM
    n = n_p + n_s
    t = t_p + t_s
    xp_spec, xs_spec = _two_stream_specs(n_p, n_s)
    cur = lambda w: pl.BlockSpec((TM, w), lambda i: (jnp.minimum(i, n - 1), 0))
    prev = lambda w: pl.BlockSpec((TM, w), lambda i: (jnp.maximum(i - 1, 0), 0))
    const = lambda r, w: pl.BlockSpec((r, w), lambda i: (0, 0))
    return pl.pallas_call(
        functools.partial(_out_proj_body, n_p),
        grid=(n + 1,),
        in_specs=[cur(D_MODEL), xp_spec, xs_spec, const(1, D_MODEL), const(D_MODEL, 2 * ROUTE_W),
                  pl.BlockSpec(memory_space=pl.ANY)],
        out_specs=[cur(D_MODEL), prev(HALF), prev(ROUTE_W), const(1, ROUTE_W)],
        out_shape=[jax.ShapeDtypeStruct((t, D_MODEL), F32),
                   jax.ShapeDtypeStruct((t, HALF), U32),
                   jax.ShapeDtypeStruct((t, ROUTE_W), F32),
                   jax.ShapeDtypeStruct((1, ROUTE_W), F32)],
        scratch_shapes=[pltpu.VMEM((D_MODEL, D_MODEL), BF16),
                        pltpu.VMEM((2, 256, D_MODEL), F32),
                        pltpu.SemaphoreType.DMA((2,)),
                        pltpu.VMEM((TM, D_MODEL), F32)],
        compiler_params=pltpu.CompilerParams(dimension_semantics=("arbitrary",), vmem_limit_bytes=VMEM_LIMIT),
        name="out_proj_route",
    )(merged, xp, xsm, norm2, w_router, w_o)


def _moe_plan(rec, cnt, n_tiles):
    t = rec.shape[0]
    n_pairs = 2 * t
    eid = rec[:, 0:2].astype(I32).reshape(-1)
    rank = rec[:, 4:6].astype(I32).reshape(-1)
    counts = cnt[0, :N_EXPERTS].astype(I32)
    pair_start = jnp.cumsum(counts) - counts
    experts = jnp.arange(N_EXPERTS, dtype=I32)
    onehot = (eid[:, None] == experts[None, :]).astype(I32)
    pos = rank + jnp.sum(onehot * pair_start[None, :], axis=1)
    _, order = lax.sort((pos, jnp.arange(n_pairs, dtype=I32)), num_keys=1)
    tiles_e = (counts + TM_MOE - 1) // TM_MOE
    tile_end = jnp.cumsum(tiles_e)
    tile_start = tile_end - tiles_e
    n_valid = tile_end[-1]
    tile_ids = jnp.arange(n_tiles, dtype=I32)
    tile_e = jnp.sum((tile_ids[:, None] >= tile_end[None, :]).astype(I32), axis=1)
    last_e = jnp.sum((n_valid - 1 >= tile_end).astype(I32))
    tile_e = jnp.minimum(jnp.where(tile_ids < n_valid, tile_e, last_e), N_EXPERTS - 1)
    tile_onehot = (tile_e[:, None] == experts[None, :]).astype(I32)
    tile_q0 = (tile_ids - jnp.sum(tile_onehot * tile_start[None, :], axis=1)) * TM_MOE
    tile_q0 = jnp.where(tile_ids < n_valid, tile_q0, 0)
    dst = rank + jnp.sum(onehot * (tile_start * TM_MOE)[None, :], axis=1)
    pad_start = tile_start * TM_MOE + counts
    pad_len = tiles_e * TM_MOE - counts
    return (dst.astype(I32), pad_start.astype(I32), pad_len.astype(I32), tile_e, tile_q0.astype(I32),
            n_valid.astype(I32).reshape(1), pair_start, counts, order)


DISPATCH_BUFS = 3
ROW_PIECES = tuple(TM_MOE >> (b + 1) for b in range(TM_MOE.bit_length() - 1))


def _dispatch_body(dst_ref, pad_start_ref, pad_len_ref, nvalid_ref, v_hbm, x_hbm, buf, zbuf, rsem, ssem, zsem):
    i = pl.program_id(0)
    n = pl.num_programs(0)

    def read(tile, slot):
        return pltpu.make_async_copy(v_hbm.at[pl.ds(pl.multiple_of(tile * TM, TM), TM)], buf.at[slot],
                                     rsem.at[slot])

    def row_write(slot, r, dst_row):
        return pltpu.make_async_copy(buf.at[slot, pl.ds(r, 1)], x_hbm.at[pl.ds(dst_row, 1)], ssem.at[slot])

    def drain(slot):
        for _ in range(2):
            pltpu.make_async_copy(buf.at[slot], x_hbm.at[pl.ds(0, TM)], ssem.at[slot]).wait()

    def pad_fill(go):
        def zero_rows(start, size):
            d = pltpu.make_async_copy(zbuf.at[pl.ds(0, size)], x_hbm.at[pl.ds(start, size)], zsem)
            d.start() if go else d.wait()

        def body(e, c):
            start, length = pad_start_ref[e], pad_len_ref[e]
            head = (-start) & (SUBLANES - 1)
            for h in range(SUBLANES - 1):
                @pl.when(h < head)
                def _(h=h):
                    zero_rows(start + h, 1)
            start, length = start + head, length - head
            for size in ROW_PIECES:
                if size >= SUBLANES:
                    @pl.when((length & size) != 0)
                    def _(size=size):
                        zero_rows(pl.multiple_of(start + (length & (-2 * size)), SUBLANES), size)
            return c
        lax.fori_loop(0, N_EXPERTS, body, 0)

        def unused(tile, c):
            for half in range(TM_MOE // ROW_PIECES[0]):
                zero_rows(pl.multiple_of(tile * TM_MOE + half * ROW_PIECES[0], SUBLANES), ROW_PIECES[0])
            return c
        lax.fori_loop(nvalid_ref[0], x_hbm.shape[0] // TM_MOE, unused, 0)

    @pl.when(i == 0)
    def _():
        zbuf[...] = jnp.zeros(zbuf.shape, U32)
        pad_fill(True)
        read(0, 0).start()

    @pl.when(i >= 2)
    def _():
        drain((i + 1) % DISPATCH_BUFS)

    @pl.when(i + 1 < n)
    def _():
        read(i + 1, (i + 1) % DISPATCH_BUFS).start()

    slot = i % DISPATCH_BUFS
    read(i, slot).wait()
    for r in range(TM):
        for k in range(2):
            row_write(slot, r, dst_ref[2 * (i * TM + r) + k]).start(priority=k)

    @pl.when(i == n - 1)
    def _():
        if n >= 2:
            drain((i - 1) % DISPATCH_BUFS)
        drain(slot)
        pad_fill(False)


def _dispatch(v, plan, n_tiles):
    dst, pad_start, pad_len, n_valid = plan[0], plan[1], plan[2], plan[5]
    t = v.shape[0]
    return pl.pallas_call(
        _dispatch_body,
        grid_spec=pltpu.PrefetchScalarGridSpec(
            num_scalar_prefetch=4,
            grid=(t // TM,),
            in_specs=[pl.BlockSpec(memory_space=pl.ANY)],
            out_specs=pl.BlockSpec(memory_space=pl.ANY),
            scratch_shapes=[pltpu.VMEM((DISPATCH_BUFS, TM, HALF), U32),
                            pltpu.VMEM((ROW_PIECES[0], HALF), U32),
                            pltpu.SemaphoreType.DMA((DISPATCH_BUFS,)),
                            pltpu.SemaphoreType.DMA((DISPATCH_BUFS,)),
                            pltpu.SemaphoreType.DMA(())]),
        out_shape=jax.ShapeDtypeStruct((n_tiles * TM_MOE, HALF), U32),
        compiler_params=pltpu.CompilerParams(dimension_semantics=("arbitrary",), vmem_limit_bytes=VMEM_LIMIT),
        name="moe_dispatch",
    )(dst, pad_start, pad_len, n_valid, v)


WEIGHT_SLOTS = 3


def _moe_body(n_tok, plane, tile_e_ref, tile_q0_ref, nvalid_ref, pstart_ref, cnt_ref, orow_ref,
              krank_ref, elist_ref, nexp_ref,
              x_ref, wg_hbm, wu_hbm, wd_hbm, o_hbm, ybuf0, ybuf1, ybuf2, ssem,
              stage_g, stage_u, stage_d, wsem, wg, wu, wd):
    i = pl.program_id(0)
    nv = nvalid_ref[0]
    bufs = (ybuf0, ybuf1, ybuf2)

    def scratch_row0(slot):
        return (slot & 1) * plane + n_tok + (slot >> 1) * TM_MOE

    def row_write(slot, r, dst_row):
        return pltpu.make_async_copy(bufs[slot].at[pl.ds(r, 1)], o_hbm.at[pl.ds(dst_row, 1)], ssem.at[slot])

    def scratch_rows(region):
        return pltpu.make_async_copy(ybuf0, o_hbm.at[pl.ds(scratch_row0(region), TM_MOE)], ssem.at[0])

    def start_writes(tile, slot, rows):
        e = tile_e_ref[tile]
        valid = cnt_ref[e] - tile_q0_ref[tile]
        first = pstart_ref[e] + tile_q0_ref[tile]
        for r in rows:
            row_write(slot, r, jnp.where(r < valid, orow_ref[first + r], scratch_row0(slot) + r)).start(
                priority=r % 2)

    def compute(slot, writes=None):
        quarter = TM_MOE // 4
        batch = lambda q: start_writes(*writes, range(q * quarter, (q + 1) * quarter)) if writes else None
        x_lo, x_hi = (h.astype(BF16) for h in _unpack_halves(x_ref[...]))
        batch(0)
        hg = (jnp.dot(x_lo, wg[0:HALF, :], preferred_element_type=F32)
              + jnp.dot(x_hi, wg[HALF:D_MODEL, :], preferred_element_type=F32))
        batch(1)
        hu = (jnp.dot(x_lo, wu[0:HALF, :], preferred_element_type=F32)
              + jnp.dot(x_hi, wu[HALF:D_MODEL, :], preferred_element_type=F32))
        batch(2)
        act = hg * _sigmoid(hg) * hu
        y = jnp.dot(act.astype(BF16), wd[...], preferred_element_type=F32)
        batch(3)
        bufs[slot][...] = _pack_halves(y)

    @pl.when(i == 0)
    def _():
        ybuf0[...] = jnp.zeros(ybuf0.shape, U32)
        for region in range(4):
            scratch_rows(region).start()
        for region in range(4):
            scratch_rows(region).wait()

    @pl.when(jnp.logical_and(i >= 3, i < nv + 3))
    def _():
        pltpu.make_async_copy(ybuf0, o_hbm.at[pl.ds(0, TM_MOE)], ssem.at[i % 3]).wait()

    def weight_copies(k, go):
        e = elist_ref[k]
        slot = k % WEIGHT_SLOTS
        for w_hbm, st in ((wg_hbm, stage_g), (wu_hbm, stage_u), (wd_hbm, stage_d)):
            d = pltpu.make_async_copy(w_hbm.at[e], st.at[slot], wsem.at[slot])
            d.start() if go else d.wait()

    @pl.when(i == 0)
    def _():
        for k in range(WEIGHT_SLOTS):
            @pl.when(k < nexp_ref[0])
            def _(k=k):
                weight_copies(k, True)

    @pl.when(i < nv)
    def _():
        prev_e = tile_e_ref[jnp.maximum(i - 1, 0)]

        @pl.when(jnp.logical_or(i == 0, tile_e_ref[i] != prev_e))
        def _():
            k = krank_ref[tile_e_ref[i]]
            slot = k % WEIGHT_SLOTS
            weight_copies(k, False)
            wg[...] = stage_g[slot].astype(BF16)
            wu[...] = stage_u[slot].astype(BF16)
            wd[...] = stage_d[slot].astype(BF16)

            @pl.when(k + WEIGHT_SLOTS < nexp_ref[0])
            def _():
                weight_copies(k + WEIGHT_SLOTS, True)

    @pl.when(i == 0)
    def _():
        compute(0)

    for slot in range(3):
        prev = (slot + 2) % 3

        @pl.when(jnp.logical_and(i % 3 == slot, jnp.logical_and(i >= 1, i < nv)))
        def _(slot=slot, prev=prev):
            compute(slot, writes=(i - 1, prev))

        @pl.when(jnp.logical_and(i % 3 == slot, i == nv))
        def _(prev=prev):
            start_writes(i - 1, prev, range(TM_MOE))


def _moe(x_disp, plan, w_gate, w_up, w_down, n_tiles, n_tok):
    tile_e, tile_q0, n_valid, pair_start, counts, order = plan[3:9]
    plane = n_tok + 2 * TM_MOE
    orow = jnp.pad((order & 1) * plane + (order >> 1), (0, TM_MOE))
    present = (counts > 0).astype(I32)
    krank = jnp.cumsum(present) - present
    experts = jnp.arange(N_EXPERTS, dtype=I32)
    elist = jnp.sum(jnp.where((krank[None, :] == experts[:, None]) & (present[None, :] > 0), experts[None, :], 0),
                    axis=1).astype(I32)
    nexp = jnp.sum(present).astype(I32).reshape(1)
    tile = lambda i, nv: jnp.minimum(i, jnp.maximum(nv[0] - 1, 0))
    any_spec = pl.BlockSpec(memory_space=pl.ANY)
    ybuf = pltpu.VMEM((TM_MOE, HALF), U32)
    return pl.pallas_call(
        functools.partial(_moe_body, n_tok, plane),
        grid_spec=pltpu.PrefetchScalarGridSpec(
            num_scalar_prefetch=9,
            grid=(n_tiles + 3,),
            in_specs=[pl.BlockSpec((TM_MOE, HALF), lambda i, te, tq, nv, *_: (tile(i, nv), 0)),
                      any_spec, any_spec, any_spec],
            out_specs=pl.BlockSpec(memory_space=pl.ANY),
            scratch_shapes=[ybuf, ybuf, ybuf,
                            pltpu.SemaphoreType.DMA((3,)),
                            pltpu.VMEM((WEIGHT_SLOTS, D_MODEL, D_EXPERT), F32),
                            pltpu.VMEM((WEIGHT_SLOTS, D_MODEL, D_EXPERT), F32),
                            pltpu.VMEM((WEIGHT_SLOTS, D_EXPERT, D_MODEL), F32),
                            pltpu.SemaphoreType.DMA((WEIGHT_SLOTS,)),
                            pltpu.VMEM((D_MODEL, D_EXPERT), BF16),
                            pltpu.VMEM((D_MODEL, D_EXPERT), BF16),
                            pltpu.VMEM((D_EXPERT, D_MODEL), BF16)]),
        out_shape=jax.ShapeDtypeStruct((2 * plane, HALF), U32),
        compiler_params=pltpu.CompilerParams(dimension_semantics=("arbitrary",), vmem_limit_bytes=VMEM_LIMIT),
        name="moe_experts",
    )(tile_e, tile_q0, n_valid, pair_start, counts, orow, krank.astype(I32), elist, nexp,
      x_disp, w_gate, w_up, w_down)


def _combine_body(n_p_tiles, h2_ref, rec_ref, gf_ref, y1_ref, y2_ref, outp_ref, outs_ref):
    i = pl.program_id(0)
    rec = rec_ref[...]
    y1 = jnp.concatenate(_unpack_halves(y1_ref[...]), axis=-1)
    y2 = jnp.concatenate(_unpack_halves(y2_ref[...]), axis=-1)
    h = rec[:, 2:3] * y1 + rec[:, 3:4] * y2
    out = _rmsnorm(h2_ref[...] + h, gf_ref[...])

    @pl.when(i < n_p_tiles)
    def _():
        outp_ref[...] = out

    @pl.when(i >= n_p_tiles)
    def _():
        outs_ref[...] = out


def _combine(h2, rec, y_pairs, final_norm, t_p, t_s):
    n_p, n_s = t_p // TM, t_s // TM
    plane_tiles = y_pairs.shape[0] // 2 // TM
    return pl.pallas_call(
        functools.partial(_combine_body, n_p),
        grid=(n_p + n_s,),
        in_specs=[pl.BlockSpec((TM, D_MODEL), lambda i: (i, 0)),
                  pl.BlockSpec((TM, ROUTE_W), lambda i: (i, 0)),
                  pl.BlockSpec((1, D_MODEL), lambda i: (0, 0)),
                  pl.BlockSpec((TM, HALF), lambda i: (i, 0)),
                  pl.BlockSpec((TM, HALF), lambda i: (i + plane_tiles, 0))],
        out_specs=[pl.BlockSpec((TM, D_MODEL), lambda i: (jnp.minimum(i, n_p - 1), 0)),
                   pl.BlockSpec((TM, D_MODEL), lambda i: (jnp.clip(i - n_p, 0, n_s - 1), 0))],
        out_shape=[jax.ShapeDtypeStruct((t_p, D_MODEL), F32),
                   jax.ShapeDtypeStruct((t_s, D_MODEL), F32)],
        compiler_params=pltpu.CompilerParams(dimension_semantics=("arbitrary",), vmem_limit_bytes=VMEM_LIMIT),
        name="combine_norm",
    )(h2, rec, final_norm, y_pairs, y_pairs)


def kernel(x_prompt, x_sample, state_conv, state_ssm_re, state_ssm_im, meta_tokens, norm1, w_in, conv_w,
           lam_re, lam_im, log_dt, ssm_b_re, ssm_b_im, ssm_c_re, ssm_c_im, ssm_d, w_glu, w_conv_out,
           w_ssm_out, w_o, norm2, w_coarse, w_fine, w_gate, w_up, w_down, final_norm):
    n_pb, seq, _ = x_prompt.shape
    n_sb, dec_seq, _ = x_sample.shape
    assert dec_seq == CHUNK and seq % TM == 0 and (n_sb * dec_seq) % TM == 0 and N_META == CHUNK
    t_p, t_s = n_pb * seq, n_sb * dec_seq
    xp = x_prompt.reshape(t_p, D_MODEL)
    xsm = x_sample.reshape(t_s, D_MODEL)

    xb, z, xs, zmeta, xsmeta, sga, sgb = _in_proj(xp, xsm, meta_tokens, norm1, w_in[0])

    mats = _s5_chunk_mats(lam_re[0], lam_im[0], log_dt[0], ssm_b_re[0], ssm_b_im[0], ssm_c_re[0], ssm_c_im[0])
    y5, pf_re, pf_im, sf_re, sf_im = _s5(xs, xsmeta, state_ssm_re[0], state_ssm_im[0], mats, n_pb, n_sb, seq)

    buf = state_conv[0]
    zero = jnp.zeros((n_sb, dec_seq, D_CONV), F32)
    inj1 = zero.at[:, 0].set(buf[:, 1]).reshape(t_s, D_CONV)
    inj2 = zero.at[:, 0].set(buf[:, 0]).at[:, 1].set(buf[:, 1]).reshape(t_s, D_CONV)
    merged = _mixers(xb, z, y5, xs, sga, sgb, zmeta, inj1, inj2, conv_w[0], ssm_d, w_conv_out[0], w_glu[0],
                     w_ssm_out[0], t_p // TM, seq // TM)

    w_router = jnp.concatenate(
        [w_fine[0], w_coarse[0], jnp.zeros((D_MODEL, ROUTE_W - N_EXPERTS - N_EGROUPS), F32)], axis=1)
    w_router = jnp.concatenate(_split_bf16(w_router), axis=1)
    h2, v, rec, cnt = _out_proj(merged, xp, xsm, norm2, w_router, w_o[0])

    n_tiles = 2 * (t_p + t_s) // TM_MOE + N_EXPERTS
    plan = _moe_plan(rec, cnt, n_tiles)
    x_disp = _dispatch(v, plan, n_tiles)
    y_pairs = _moe(x_disp, plan, w_gate[0], w_up[0], w_down[0], n_tiles, t_p + t_s)
    y_p, y_s = _combine(h2, rec, y_pairs, final_norm.reshape(1, D_MODEL), t_p, t_s)

    new_conv_p = jnp.stack([z[(b + 1) * seq - 2:(b + 1) * seq] for b in range(n_pb)])
    new_conv_s = z[t_p:].reshape(n_sb, dec_seq, D_CONV)[:, dec_seq - 2:]
    return (y_p.reshape(n_pb, seq, D_MODEL), y_s.reshape(n_sb, dec_seq, D_MODEL),
            new_conv_p[None], pf_re, pf_im, new_conv_s[None], sf_re, sf_im)
```

```python
import functools

import jax
import jax.numpy as jnp
from jax import lax
from jax.experimental import pallas as pl
from jax.experimental.pallas import tpu as pltpu

F32 = jnp.float32
BF16 = jnp.bfloat16
I32 = jnp.int32
U32 = jnp.uint32

D_MODEL = 2048
D_CONV = 1024
D_SSM = 1024
SSM_H = 16
SSM_G = 64
SSM_P = 64
N_META = 16
N_EGROUPS = 4
EXPERTS_PER_GROUP = 8
N_EXPERTS = 32
D_EXPERT = 256
EPS = 1e-6

LANES = 128
SUBLANES = 8

CHUNK = 16
CHUNK_W = CHUNK * SSM_H

TM = 256
TM_MOE = 256
VMEM_LIMIT = 52 * 1024 * 1024


def _rmsnorm(x, g):
    return x * lax.rsqrt(jnp.mean(x * x, axis=-1, keepdims=True) + EPS) * g


def _sigmoid(x):
    return 1.0 / (1.0 + jnp.exp(-x))


def _gelu_tanh(x):
    c = 0.7978845608028654
    return 0.5 * x * (1.0 + jnp.tanh(c * (x + 0.044715 * (x * x * x))))


def _split_bf16(a):
    hi = a.astype(BF16)
    lo = (a - hi.astype(F32)).astype(BF16)
    return hi, lo


def _dot3(a, b):
    a_hi, a_lo = _split_bf16(a)
    b_hi, b_lo = _split_bf16(b)
    return (jnp.dot(a_hi, b_hi, preferred_element_type=F32) + jnp.dot(a_lo, b_hi, preferred_element_type=F32)
            + jnp.dot(a_hi, b_lo, preferred_element_type=F32))


HALF = D_MODEL // 2


def _pack_halves(a):
    return pltpu.pack_elementwise([a[:, :HALF], a[:, HALF:]], packed_dtype=BF16)


def _unpack_halves(p):
    return (pltpu.unpack_elementwise(p, index=0, packed_dtype=BF16, unpacked_dtype=F32),
            pltpu.unpack_elementwise(p, index=1, packed_dtype=BF16, unpacked_dtype=F32))


def _weight_copy(w_hbm, stage, sem, c, slot, rows, col0, ncols):
    return pltpu.make_async_copy(
        w_hbm.at[pl.ds(c * rows, rows), pl.ds(col0, ncols)], stage.at[slot], sem.at[slot])


def _load_weight_bf16(w_hbm, w_vmem, stage, sem, col0=0):
    k, n = w_vmem.shape
    rows = stage.shape[1]
    nchunk = k // rows
    _weight_copy(w_hbm, stage, sem, 0, 0, rows, col0, n).start()
    for c in range(nchunk):
        slot = c % 2
        if c + 1 < nchunk:
            _weight_copy(w_hbm, stage, sem, c + 1, 1 - slot, rows, col0, n).start()
        _weight_copy(w_hbm, stage, sem, c, slot, rows, col0, n).wait()
        w_vmem[pl.ds(c * rows, rows), :] = stage[slot].astype(BF16)


def _in_proj_mix_body(n_p_tiles, xp_ref, xsm_ref, meta_ref, g_ref, w_hbm,
                      xb_out, z_out, xs_out, zmeta_out, xsmeta_out,
                      w_vmem, stage, sem):
    i = pl.program_id(0)
    g = g_ref[...]

    def project(u):
        xb = jnp.dot(u, w_vmem[:, 0:D_CONV], preferred_element_type=F32)
        xc = jnp.dot(u, w_vmem[:, D_CONV:2 * D_CONV], preferred_element_type=F32)
        xv = jnp.dot(u, w_vmem[:, 2 * D_CONV:3 * D_CONV], preferred_element_type=F32)
        xs = jnp.dot(u, w_vmem[:, 3 * D_CONV:3 * D_CONV + D_SSM], preferred_element_type=F32)
        return xb, xc * xv, xs

    @pl.when(i == 0)
    def _():
        _load_weight_bf16(w_hbm, w_vmem, stage, sem, col0=0)
        um = _rmsnorm(meta_ref[...], g).astype(BF16)
        _, zm, xsm = project(um)
        zmeta_out[...] = zm
        xsmeta_out[...] = xsm

    x = jnp.where(i < n_p_tiles, xp_ref[...], xsm_ref[...])
    u = _rmsnorm(x, g).astype(BF16)
    xb, z, xs = project(u)
    xb_out[...] = xb.astype(BF16)
    z_out[...] = z
    xs_out[...] = xs


def _in_proj_gate_body(n_p_tiles, xp_ref, xsm_ref, g_ref, w_hbm, ga_out, gb_out,
                       w_vmem, stage, sem):
    i = pl.program_id(0)

    @pl.when(i == 0)
    def _():
        _load_weight_bf16(w_hbm, w_vmem, stage, sem, col0=3 * D_CONV + D_SSM)

    x = jnp.where(i < n_p_tiles, xp_ref[...], xsm_ref[...])
    u = _rmsnorm(x, g_ref[...]).astype(BF16)
    g = jnp.dot(u, w_vmem[...], preferred_element_type=F32)
    ga_out[...] = _sigmoid(g[:, 0:D_MODEL]).astype(BF16)
    gb_out[...] = _sigmoid(g[:, D_MODEL:2 * D_MODEL]).astype(BF16)


def _two_stream_specs(n_p_tiles, n_s_tiles, tm=TM):
    xp_spec = pl.BlockSpec((tm, D_MODEL), lambda i: (jnp.minimum(i, n_p_tiles - 1), 0))
    mode = pl.Buffered(1) if n_s_tiles == 1 else None
    xs_spec = pl.BlockSpec((tm, D_MODEL), lambda i: (jnp.clip(i - n_p_tiles, 0, n_s_tiles - 1), 0),
                           pipeline_mode=mode)
    return xp_spec, xs_spec


TM_IN = TM


def _in_proj(xp, xsm, meta, norm1, w_in):
    t_p, t_s = xp.shape[0], xsm.shape[0]
    n_p, n_s = t_p // TM_IN, t_s // TM_IN
    t = t_p + t_s
    half = 3 * D_CONV + D_SSM
    xp_spec, xs_spec = _two_stream_specs(n_p, n_s, TM_IN)
    g_spec = pl.BlockSpec((1, D_MODEL), lambda i: (0, 0))
    any_spec = pl.BlockSpec(memory_space=pl.ANY)
    stage_rows = 128
    row = lambda w: pl.BlockSpec((TM_IN, w), lambda i: (i, 0))
    const = lambda r, w: pl.BlockSpec((r, w), lambda i: (0, 0))
    params = pltpu.CompilerParams(dimension_semantics=("arbitrary",), vmem_limit_bytes=VMEM_LIMIT)
    scratch = [pltpu.VMEM((D_MODEL, half), BF16),
               pltpu.VMEM((2, stage_rows, half), F32),
               pltpu.SemaphoreType.DMA((2,))]

    xb, z, xs, zmeta, xsmeta = pl.pallas_call(
        functools.partial(_in_proj_mix_body, n_p),
        grid=(n_p + n_s,),
        in_specs=[xp_spec, xs_spec, const(N_META, D_MODEL), g_spec, any_spec],
        out_specs=[row(D_CONV), row(D_CONV), row(D_SSM), const(N_META, D_CONV), const(N_META, D_SSM)],
        out_shape=[jax.ShapeDtypeStruct((t, D_CONV), BF16),
                   jax.ShapeDtypeStruct((t, D_CONV), F32),
                   jax.ShapeDtypeStruct((t, D_SSM), F32),
                   jax.ShapeDtypeStruct((N_META, D_CONV), F32),
                   jax.ShapeDtypeStruct((N_META, D_SSM), F32)],
        scratch_shapes=scratch,
        compiler_params=params,
        name="in_proj_mix",
    )(xp, xsm, meta, norm1, w_in)

    sga, sgb = pl.pallas_call(
        functools.partial(_in_proj_gate_body, n_p),
        grid=(n_p + n_s,),
        in_specs=[xp_spec, xs_spec, g_spec, any_spec],
        out_specs=[row(D_MODEL), row(D_MODEL)],
        out_shape=[jax.ShapeDtypeStruct((t, D_MODEL), BF16),
                   jax.ShapeDtypeStruct((t, D_MODEL), BF16)],
        scratch_shapes=scratch,
        compiler_params=params,
        name="in_proj_gate",
    )(xp, xsm, norm1, w_in)
    return xb, z, xs, zmeta, xsmeta, sga, sgb


S5_GROUPS_PER_STEP = LANES // SSM_H
S5_PAIRS_PER_STEP = S5_GROUPS_PER_STEP // 2


def _s5_chunk_mats(lam_re, lam_im, log_dt, b_re, b_im, c_re, c_im):
    dt = jnp.exp(log_dt)[:, None]
    lr, li = lam_re, lam_im
    z = jnp.stack([lr * dt, li * dt], axis=1)
    mag = jnp.exp(lr * dt)
    ab_re, ab_im = mag * jnp.cos(li * dt), mag * jnp.sin(li * dt)
    nr, ni = ab_re - 1.0, ab_im
    den = lr * lr + li * li
    k_re = (nr * lr + ni * li) / den
    k_im = (ni * lr - nr * li) / den
    bb = jnp.stack([k_re[..., None] * b_re - k_im[..., None] * b_im,
                    k_re[..., None] * b_im + k_im[..., None] * b_re], axis=1)
    return z, bb, jnp.stack([c_re, c_im], axis=1)


def _cmul(ar, ai, br, bi):
    return ar * br - ai * bi, ar * bi + ai * br


def _s5_operators(pair, parity, z_ref, bb_ref, c_ref, toe_scr, p_scr, q_scr, a16_scr, g_scr, m_scr):
    k = 2 * pair + parity
    zr, zi = z_ref[k, 0:1, :], z_ref[k, 1:2, :]
    mag = jnp.exp(zr)
    ar, ai = mag * jnp.cos(zi), mag * jnp.sin(zi)
    eye = lax.broadcasted_iota(I32, (SSM_P, SSM_P), 0) == lax.broadcasted_iota(I32, (SSM_P, SSM_P), 1)
    acr = jnp.sum(jnp.where(eye, ar, 0.0), axis=1, keepdims=True)
    aci = jnp.sum(jnp.where(eye, ai, 0.0), axis=1, keepdims=True)
    c_re, c_im = c_ref[k, 0], c_ref[k, 1]
    bb_re, bb_im = bb_ref[k, 0], bb_ref[k, 1]

    blk = lax.broadcasted_iota(I32, (1, CHUNK_W), 1) >> 4

    pr, pi = jnp.ones_like(ar), jnp.zeros_like(ar)
    pcr, pci = jnp.ones_like(acr), jnp.zeros_like(acr)
    pw_re = jnp.zeros((SSM_P, CHUNK_W), F32)
    pw_im = jnp.zeros((SSM_P, CHUNK_W), F32)
    for d in range(CHUNK + 1):
        g_scr[0, d * SSM_H:(d + 1) * SSM_H, :] = c_re * pr - c_im * pi
        g_scr[1, d * SSM_H:(d + 1) * SSM_H, :] = c_re * pi + c_im * pr
        if d < CHUNK:
            pw_re = jnp.where(blk == CHUNK - 1 - d, pcr, pw_re)
            pw_im = jnp.where(blk == CHUNK - 1 - d, pci, pw_im)
            pcr, pci = _cmul(pcr, pci, acr, aci)
            pr, pi = _cmul(pr, pi, ar, ai)

    half = parity * SSM_P
    a16_scr[pair, 0, :, half:half + SSM_P] = pr
    a16_scr[pair, 1, :, half:half + SSM_P] = pi
    q_scr[k] = jnp.zeros(q_scr.shape[1:], BF16)
    q_scr[k, 0, :, half:half + SSM_P] = g_scr[0, SSM_H:, :].astype(BF16)
    q_scr[k, 1, :, half:half + SSM_P] = (-g_scr[1, SSM_H:, :]).astype(BF16)

    rep = jnp.where(lax.broadcasted_iota(I32, (SSM_H, CHUNK_W), 0)
                    == (lax.broadcasted_iota(I32, (SSM_H, CHUNK_W), 1) & (SSM_H - 1)), 1.0, 0.0).astype(BF16)

    def widen(b):
        b_hi, b_lo = _split_bf16(b)
        return (jnp.dot(b_hi, rep, preferred_element_type=F32) + jnp.dot(b_lo, rep, preferred_element_type=F32))

    bw_re, bw_im = widen(bb_re), widen(bb_im)
    p_re, p_im = _cmul(pw_re, pw_im, bw_re, bw_im)
    p_scr[k, 0:SSM_P, :] = p_re.astype(BF16)
    p_scr[k, SSM_P:2 * SSM_P, :] = p_im.astype(BF16)

    m_scr[...] = _dot3(g_scr[0, 0:CHUNK_W, :], bw_re) - _dot3(g_scr[1, 0:CHUNK_W, :], bw_im)
    for t in range(CHUNK):
        acc = jnp.where(blk == 0, m_scr[t * SSM_H:(t + 1) * SSM_H, :], 0.0)
        for j in range(1, t + 1):
            acc = jnp.where(blk == j, m_scr[(t - j) * SSM_H:(t - j + 1) * SSM_H, :], acc)
        toe_scr[k, t * SSM_H:(t + 1) * SSM_H, :] = acc.astype(BF16)


def _s5_body(n_pc, n_pb, n_sb, xs_ref, xsmeta_ref, z_ref, bb_ref, c_ref, s0re_ref, s0im_ref,
             y_out, pfre_out, pfim_out, sfre_out, sfim_out,
             u_scr, sl_re, sl_im, sp_re, sp_im, yt_scr, toe_ref, p_ref, q_ref, a16_scr,
             g_scr, m_scr):
    gb = S5_GROUPS_PER_STEP
    n_p_rows = n_pb * n_pc
    row_s = n_p_rows
    row_m = row_s + n_sb
    t_p = n_p_rows * CHUNK
    rows_pad = u_scr.shape[2]

    for k in range(gb):
        _s5_operators(k // 2, k % 2, z_ref, bb_ref, c_ref, toe_ref, p_ref, q_ref, a16_scr, g_scr, m_scr)

    first_row = lax.broadcasted_iota(I32, (SUBLANES, 1), 0) == 0
    tail = jnp.zeros((rows_pad - row_m - SUBLANES, LANES), F32)
    for t in range(CHUNK):
        meta_rows = jnp.where(first_row, xsmeta_ref[t:t + 1, :], 0.0)
        rows_t = jnp.concatenate([xs_ref[pl.ds(t, n_p_rows, stride=CHUNK), :],
                                  xs_ref[pl.ds(t_p + t, n_sb, stride=CHUNK), :], meta_rows, tail], axis=0)
        xt = rows_t.T.astype(BF16)
        for k in range(gb):
            u_scr[k, t * SSM_H:(t + 1) * SSM_H, :] = xt[k * SSM_H:(k + 1) * SSM_H, :]

    npair = S5_PAIRS_PER_STEP
    for j in range(npair):
        sl0 = jnp.dot(p_ref[2 * j], u_scr[2 * j], preferred_element_type=F32)
        sl1 = jnp.dot(p_ref[2 * j + 1], u_scr[2 * j + 1], preferred_element_type=F32)
        sl_re[j] = jnp.concatenate([sl0[0:SSM_P, :], sl1[0:SSM_P, :]], axis=0).T
        sl_im[j] = jnp.concatenate([sl0[SSM_P:2 * SSM_P, :], sl1[SSM_P:2 * SSM_P, :]], axis=0).T
        sp_re[j, row_m:rows_pad, :] = jnp.zeros((rows_pad - row_m, 2 * SSM_P), F32)
        sp_im[j, row_m:rows_pad, :] = jnp.zeros((rows_pad - row_m, 2 * SSM_P), F32)

    ar = [a16_scr[j, 0] for j in range(npair)]
    ai = [a16_scr[j, 1] for j in range(npair)]
    sre = [jnp.broadcast_to(sl_re[j, row_m:row_m + 1, :], (n_pb, 2 * SSM_P)) for j in range(npair)]
    sim = [jnp.broadcast_to(sl_im[j, row_m:row_m + 1, :], (n_pb, 2 * SSM_P)) for j in range(npair)]
    for c in range(n_pc):
        rows = pl.ds(c, n_pb, stride=n_pc)
        for j in range(npair):
            sp_re[j, rows, :] = sre[j]
            sp_im[j, rows, :] = sim[j]
            nre = ar[j] * sre[j] - ai[j] * sim[j] + sl_re[j, rows, :]
            nim = ar[j] * sim[j] + ai[j] * sre[j] + sl_im[j, rows, :]
            sre[j], sim[j] = nre, nim
    for j in range(npair):
        pfre_out[j] = sre[j]
        pfim_out[j] = sim[j]
        s0r, s0i = s0re_ref[j], s0im_ref[j]
        sp_re[j, row_s:row_m, :] = s0r
        sp_im[j, row_s:row_m, :] = s0i
        sfre_out[j] = ar[j] * s0r - ai[j] * s0i + sl_re[j, row_s:row_m, :]
        sfim_out[j] = ar[j] * s0i + ai[j] * s0r + sl_im[j, row_s:row_m, :]

    nt = (((1,), (1,)), ((), ()))
    for k in range(gb):
        y = jnp.dot(toe_ref[k], u_scr[k], preferred_element_type=F32)
        y += lax.dot_general(q_ref[k, 0], sp_re[k // 2].astype(BF16), nt, preferred_element_type=F32)
        y += lax.dot_general(q_ref[k, 1], sp_im[k // 2].astype(BF16), nt, preferred_element_type=F32)
        for t in range(CHUNK):
            yt_scr[t, k * SSM_H:(k + 1) * SSM_H, :] = y[t * SSM_H:(t + 1) * SSM_H, :]
    for t in range(CHUNK):
        yt = yt_scr[t].T
        y_out[pl.ds(t, n_p_rows, stride=CHUNK), :] = yt[0:n_p_rows, :]
        y_out[pl.ds(t_p + t, n_sb, stride=CHUNK), :] = yt[row_s:row_m, :]


def _s5(xs, xsmeta, state_re, state_im, mats, n_pb, n_sb, seq):
    z, bb, c = mats
    t = xs.shape[0]
    n_pc = seq // CHUNK
    rows = n_pc * n_pb + n_sb + 1
    rows_pad = -(-rows // LANES) * LANES
    gb, npair = S5_GROUPS_PER_STEP, S5_PAIRS_PER_STEP
    pairs = lambda s: jnp.transpose(s.reshape(n_sb, SSM_G // 2, 2 * SSM_P), (1, 0, 2))
    blk3 = lambda n, r, c: pl.BlockSpec((n, r, c), lambda i: (i, 0, 0))
    y, pfre, pfim, sfre, sfim = pl.pallas_call(
        functools.partial(_s5_body, n_pc, n_pb, n_sb),
        grid=(SSM_G // gb,),
        in_specs=[pl.BlockSpec((t, LANES), lambda i: (0, i)),
                  pl.BlockSpec((N_META, LANES), lambda i: (0, i)),
                  blk3(gb, 2, SSM_P),
                  pl.BlockSpec((gb, 2, SSM_P, SSM_H), lambda i: (i, 0, 0, 0)),
                  pl.BlockSpec((gb, 2, SSM_H, SSM_P), lambda i: (i, 0, 0, 0)),
                  blk3(npair, n_sb, 2 * SSM_P), blk3(npair, n_sb, 2 * SSM_P)],
        out_specs=[pl.BlockSpec((t, LANES), lambda i: (0, i)),
                   blk3(npair, n_pb, 2 * SSM_P), blk3(npair, n_pb, 2 * SSM_P),
                   blk3(npair, n_sb, 2 * SSM_P), blk3(npair, n_sb, 2 * SSM_P)],
        out_shape=[jax.ShapeDtypeStruct((t, D_SSM), F32),
                   jax.ShapeDtypeStruct((SSM_G // 2, n_pb, 2 * SSM_P), F32),
                   jax.ShapeDtypeStruct((SSM_G // 2, n_pb, 2 * SSM_P), F32),
                   jax.ShapeDtypeStruct((SSM_G // 2, n_sb, 2 * SSM_P), F32),
                   jax.ShapeDtypeStruct((SSM_G // 2, n_sb, 2 * SSM_P), F32)],
        scratch_shapes=[pltpu.VMEM((gb, CHUNK_W, rows_pad), BF16),
                        pltpu.VMEM((npair, rows_pad, 2 * SSM_P), F32),
                        pltpu.VMEM((npair, rows_pad, 2 * SSM_P), F32),
                        pltpu.VMEM((npair, rows_pad, 2 * SSM_P), F32),
                        pltpu.VMEM((npair, rows_pad, 2 * SSM_P), F32),
                        pltpu.VMEM((CHUNK, LANES, rows_pad), F32),
                        pltpu.VMEM((gb, CHUNK_W, CHUNK_W), BF16),
                        pltpu.VMEM((gb, 2 * SSM_P, CHUNK_W), BF16),
                        pltpu.VMEM((gb, 2, CHUNK_W, 2 * SSM_P), BF16),
                        pltpu.VMEM((npair, 2, 1, 2 * SSM_P), F32),
                        pltpu.VMEM((2, (CHUNK + 1) * SSM_H, SSM_P), F32),
                        pltpu.VMEM((CHUNK_W, CHUNK_W), F32)],
        compiler_params=pltpu.CompilerParams(dimension_semantics=("arbitrary",), vmem_limit_bytes=VMEM_LIMIT),
        name="s5_chunks",
    )(xs, xsmeta, z, bb, c, pairs(state_re), pairs(state_im))
    unpair = lambda a: jnp.transpose(a, (1, 0, 2)).reshape(a.shape[1], SSM_G, SSM_P)[None]
    return y, unpair(pfre), unpair(pfim), unpair(sfre), unpair(sfim)


def _mixers_body(n_p_tiles, tiles_per_seq, xb_ref, z_ref, y5_ref, xs_ref, sga_ref, sgb_ref,
                 zmeta_ref, inj1_ref, inj2_ref, cw_ref, dskip_ref,
                 wc_hbm, wg_hbm, wso_hbm, merged_out,
                 wc, wg, wso, stage_a, stage_b, sem, carry):
    i = pl.program_id(0)

    @pl.when(i == 0)
    def _():
        _load_weight_bf16(wc_hbm, wc, stage_a, sem)
        _load_weight_bf16(wg_hbm, wg, stage_b, sem)
        _load_weight_bf16(wso_hbm, wso, stage_a, sem)

    @pl.when(jnp.logical_and(i < n_p_tiles, i % tiles_per_seq == 0))
    def _():
        carry[0:2, :] = zmeta_ref[N_META - 2:N_META, :]

    z = z_ref[...]
    row = lax.broadcasted_iota(I32, (TM, 1), 0)
    is_s = i >= n_p_tiles
    r1 = pltpu.roll(z, 1, 0)
    r2 = pltpu.roll(z, 2, 0)
    c1 = carry[1:2, :]
    c2 = carry[0:1, :]
    pos = jnp.where(is_s, row & (CHUNK - 1), row)
    first1 = pos == 0
    first2 = pos < 2
    fill1 = jnp.where(is_s, inj1_ref[...], jnp.broadcast_to(c1, z.shape))
    fill2 = jnp.where(is_s, inj2_ref[...], jnp.where(row == 0, c2, c1))
    zp1 = jnp.where(first1, fill1, r1)
    zp2 = jnp.where(first2, fill2, r2)
    carry[0:2, :] = z[TM - 2:TM, :]

    cw = cw_ref[...]
    conv = cw[0:1, :] * zp2 + cw[1:2, :] * zp1 + cw[2:3, :] * z
    a_in = (xb_ref[...].astype(F32) * conv).astype(BF16)
    ya = jnp.dot(a_in, wc[...], preferred_element_type=F32)

    ys = y5_ref[...] + dskip_ref[...] * xs_ref[...]
    ys = _gelu_tanh(ys)
    glu = jnp.dot(ys.astype(BF16), wg[...], preferred_element_type=F32)
    ys = ys * _sigmoid(glu)
    yb = jnp.dot(ys.astype(BF16), wso[...], preferred_element_type=F32)

    merged = sga_ref[...].astype(F32) * ya + sgb_ref[...].astype(F32) * yb
    merged_out[...] = merged.astype(BF16)


def _mixers(xb, z, y5, xs, sga, sgb, zmeta, inj1, inj2, conv_w, d_skip, w_conv_out, w_glu, w_ssm_out,
            n_p_tiles, tiles_per_seq):
    t = xb.shape[0]
    n_s_tiles = inj1.shape[0] // TM
    row = lambda w: pl.BlockSpec((TM, w), lambda i: (i, 0))
    const = lambda r, w: pl.BlockSpec((r, w), lambda i: (0, 0))
    inj = pl.BlockSpec((TM, D_CONV), lambda i: (jnp.clip(i - n_p_tiles, 0, n_s_tiles - 1), 0))
    any_spec = pl.BlockSpec(memory_space=pl.ANY)
    return pl.pallas_call(
        functools.partial(_mixers_body, n_p_tiles, tiles_per_seq),
        grid=(t // TM,),
        in_specs=[row(D_CONV), row(D_CONV), row(D_SSM), row(D_SSM), row(D_MODEL), row(D_MODEL),
                  const(N_META, D_CONV), inj, inj, const(3, D_CONV), const(1, D_SSM),
                  any_spec, any_spec, any_spec],
        out_specs=row(D_MODEL),
        out_shape=jax.ShapeDtypeStruct((t, D_MODEL), BF16),
        scratch_shapes=[pltpu.VMEM((D_CONV, D_MODEL), BF16),
                        pltpu.VMEM((D_SSM, D_SSM), BF16),
                        pltpu.VMEM((D_SSM, D_MODEL), BF16),
                        pltpu.VMEM((2, 256, D_MODEL), F32),
                        pltpu.VMEM((2, 256, D_SSM), F32),
                        pltpu.SemaphoreType.DMA((2,)),
                        pltpu.VMEM((8, D_CONV), F32)],
        compiler_params=pltpu.CompilerParams(dimension_semantics=("arbitrary",), vmem_limit_bytes=VMEM_LIMIT),
        name="mixers",
    )(xb, z, y5, xs, sga, sgb, zmeta, inj1, inj2, conv_w, d_skip, w_conv_out, w_glu, w_ssm_out)


ROUTE_W = LANES
COARSE0 = N_EXPERTS


def _route(logits, cnt):
    col = lax.broadcasted_iota(I32, logits.shape, 1)
    colf = col.astype(F32)
    neg = jnp.float32(-jnp.inf)
    big = jnp.float32(1 << 20)
    is_c = jnp.logical_and(col >= COARSE0, col < COARSE0 + N_EGROUPS)
    lc = jnp.where(is_c, logits, neg)
    cmax = jnp.max(lc, axis=-1, keepdims=True)
    gi = jnp.min(jnp.where(lc == cmax, colf - COARSE0, big), axis=-1, keepdims=True)
    pg = 1.0 / jnp.sum(jnp.where(is_c, jnp.exp(lc - cmax), 0.0), axis=-1, keepdims=True)
    grp = (col >> 3).astype(F32)
    in_g = jnp.logical_and(col < N_EXPERTS, grp == gi)
    lf = jnp.where(in_g, logits, neg)
    m1 = jnp.max(lf, axis=-1, keepdims=True)
    i1 = jnp.min(jnp.where(lf == m1, colf, big), axis=-1, keepdims=True)
    lf2 = jnp.where(colf == i1, neg, lf)
    m2 = jnp.max(lf2, axis=-1, keepdims=True)
    i2 = jnp.min(jnp.where(lf2 == m2, colf, big), axis=-1, keepdims=True)
    e2 = jnp.exp(m2 - m1)
    w1 = pg / (1.0 + e2)
    w2 = pg * e2 / (1.0 + e2)
    n = logits.shape[0]
    hit1 = colf == i1
    hit2 = colf == i2
    onehot = jnp.where(jnp.logical_or(hit1, hit2), 1.0, 0.0)
    rr = lax.broadcasted_iota(I32, (n, n), 0)
    cc = lax.broadcasted_iota(I32, (n, n), 1)
    tri = jnp.where(cc < rr, 1.0, 0.0).astype(BF16)
    pos = jnp.dot(tri, onehot.astype(BF16), preferred_element_type=F32) + cnt
    rank1 = jnp.sum(jnp.where(hit1, pos, 0.0), axis=-1, keepdims=True)
    rank2 = jnp.sum(jnp.where(hit2, pos, 0.0), axis=-1, keepdims=True)
    vals = (i1, i2, w1, w2, rank1, rank2)
    rec = jnp.zeros(logits.shape, F32)
    for c, val in enumerate(vals):
        rec = jnp.where(col == c, val, rec)
    return rec, cnt + jnp.sum(onehot, axis=0, keepdims=True)


def _out_proj_body(n_p_tiles, merged_ref, xp_ref, xsm_ref, g2_ref, wr_ref, wo_hbm,
                   h2_out, v_out, rec_out, cnt_out, wo, stage, sem, h2_scr):
    i = pl.program_id(0)
    n = pl.num_programs(0) - 1

    @pl.when(i == 0)
    def _():
        _load_weight_bf16(wo_hbm, wo, stage, sem)
        cnt_out[...] = jnp.zeros(cnt_out.shape, F32)

    def route_prev():
        v = _rmsnorm(h2_scr[...], g2_ref[...])
        v_out[...] = _pack_halves(v)
        v_hi, v_lo = _split_bf16(v)
        both = jnp.dot(v_hi, wr_ref[...], preferred_element_type=F32)
        logits = (both[:, :ROUTE_W] + both[:, ROUTE_W:]
                  + jnp.dot(v_lo, wr_ref[:, :ROUTE_W], preferred_element_type=F32))
        rec, cnt = _route(logits, cnt_out[...])
        rec_out[...] = rec
        cnt_out[...] = cnt

    def project():
        x = jnp.where(i < n_p_tiles, xp_ref[...], xsm_ref[...])
        h2 = x + jnp.dot(merged_ref[...], wo[...], preferred_element_type=F32)
        h2_out[...] = h2
        return h2

    @pl.when(i == 0)
    def _():
        h2_scr[...] = project()

    @pl.when(jnp.logical_and(i > 0, i < n))
    def _():
        route_prev()
        h2_scr[...] = project()

    @pl.when(i == n)
    def _():
        route_prev()


def _out_proj(merged, xp, xsm, norm2, w_router, w_o):
    t_p, t_s = xp.shape[0], xsm.shape[0]
    n_p, n_s = t_p // TM, t_s // TM
    n = n_p + n_s
    t = t_p + t_s
    xp_spec, xs_spec = _two_stream_specs(n_p, n_s)
    cur = lambda w: pl.BlockSpec((TM, w), lambda i: (jnp.minimum(i, n - 1), 0))
    prev = lambda w: pl.BlockSpec((TM, w), lambda i: (jnp.maximum(i - 1, 0), 0))
    const = lambda r, w: pl.BlockSpec((r, w), lambda i: (0, 0))
    return pl.pallas_call(
        functools.partial(_out_proj_body, n_p),
        grid=(n + 1,),
        in_specs=[cur(D_MODEL), xp_spec, xs_spec, const(1, D_MODEL), const(D_MODEL, 2 * ROUTE_W),
                  pl.BlockSpec(memory_space=pl.ANY)],
        out_specs=[cur(D_MODEL), prev(HALF), prev(ROUTE_W), const(1, ROUTE_W)],
        out_shape=[jax.ShapeDtypeStruct((t, D_MODEL), F32),
                   jax.ShapeDtypeStruct((t, HALF), U32),
                   jax.ShapeDtypeStruct((t, ROUTE_W), F32),
                   jax.ShapeDtypeStruct((1, ROUTE_W), F32)],
        scratch_shapes=[pltpu.VMEM((D_MODEL, D_MODEL), BF16),
                        pltpu.VMEM((2, 256, D_MODEL), F32),
                        pltpu.SemaphoreType.DMA((2,)),
                        pltpu.VMEM((TM, D_MODEL), F32)],
        compiler_params=pltpu.CompilerParams(dimension_semantics=("arbitrary",), vmem_limit_bytes=VMEM_LIMIT),
        name="out_proj_route",
    )(merged, xp, xsm, norm2, w_router, w_o)


def _moe_plan(rec, cnt, n_tiles):
    t = rec.shape[0]
    n_pairs = 2 * t
    eid = rec[:, 0:2].astype(I32).reshape(-1)
    rank = rec[:, 4:6].astype(I32).reshape(-1)
    counts = cnt[0, :N_EXPERTS].astype(I32)
    pair_start = jnp.cumsum(counts) - counts
    experts = jnp.arange(N_EXPERTS, dtype=I32)
    onehot = (eid[:, None] == experts[None, :]).astype(I32)
    pos = rank + jnp.sum(onehot * pair_start[None, :], axis=1)
    _, order = lax.sort((pos, jnp.arange(n_pairs, dtype=I32)), num_keys=1)
    tiles_e = (counts + TM_MOE - 1) // TM_MOE
    tile_end = jnp.cumsum(tiles_e)
    tile_start = tile_end - tiles_e
    n_valid = tile_end[-1]
    tile_ids = jnp.arange(n_tiles, dtype=I32)
    tile_e = jnp.sum((tile_ids[:, None] >= tile_end[None, :]).astype(I32), axis=1)
    last_e = jnp.sum((n_valid - 1 >= tile_end).astype(I32))
    tile_e = jnp.minimum(jnp.where(tile_ids < n_valid, tile_e, last_e), N_EXPERTS - 1)
    tile_onehot = (tile_e[:, None] == experts[None, :]).astype(I32)
    tile_q0 = (tile_ids - jnp.sum(tile_onehot * tile_start[None, :], axis=1)) * TM_MOE
    tile_q0 = jnp.where(tile_ids < n_valid, tile_q0, 0)
    dst = rank + jnp.sum(onehot * (tile_start * TM_MOE)[None, :], axis=1)
    pad_start = tile_start * TM_MOE + counts
    pad_len = tiles_e * TM_MOE - counts
    return (dst.astype(I32), pad_start.astype(I32), pad_len.astype(I32), tile_e, tile_q0.astype(I32),
            n_valid.astype(I32).reshape(1), pair_start, counts, order)


DISPATCH_BUFS = 3
ROW_PIECES = tuple(TM_MOE >> (b + 1) for b in range(TM_MOE.bit_length() - 1))


def _dispatch_body(dst_ref, pad_start_ref, pad_len_ref, nvalid_ref, v_hbm, x_hbm, buf, zbuf, rsem, ssem, zsem):
    i = pl.program_id(0)
    n = pl.num_programs(0)

    def read(tile, slot):
        return pltpu.make_async_copy(v_hbm.at[pl.ds(pl.multiple_of(tile * TM, TM), TM)], buf.at[slot],
                                     rsem.at[slot])

    def row_write(slot, r, dst_row):
        return pltpu.make_async_copy(buf.at[slot, pl.ds(r, 1)], x_hbm.at[pl.ds(dst_row, 1)], ssem.at[slot])

    def drain(slot):
        for _ in range(2):
            pltpu.make_async_copy(buf.at[slot], x_hbm.at[pl.ds(0, TM)], ssem.at[slot]).wait()

    def pad_fill(go):
        def zero_rows(start, size):
            d = pltpu.make_async_copy(zbuf.at[pl.ds(0, size)], x_hbm.at[pl.ds(start, size)], zsem)
            d.start() if go else d.wait()

        def body(e, c):
            start, length = pad_start_ref[e], pad_len_ref[e]
            head = (-start) & (SUBLANES - 1)
            for h in range(SUBLANES - 1):
                @pl.when(h < head)
                def _(h=h):
                    zero_rows(start + h, 1)
            start, length = start + head, length - head
            for size in ROW_PIECES:
                if size >= SUBLANES:
                    @pl.when((length & size) != 0)
                    def _(size=size):
                        zero_rows(pl.multiple_of(start + (length & (-2 * size)), SUBLANES), size)
            return c
        lax.fori_loop(0, N_EXPERTS, body, 0)

        def unused(tile, c):
            for half in range(TM_MOE // ROW_PIECES[0]):
                zero_rows(pl.multiple_of(tile * TM_MOE + half * ROW_PIECES[0], SUBLANES), ROW_PIECES[0])
            return c
        lax.fori_loop(nvalid_ref[0], x_hbm.shape[0] // TM_MOE, unused, 0)

    @pl.when(i == 0)
    def _():
        zbuf[...] = jnp.zeros(zbuf.shape, U32)
        pad_fill(True)
        read(0, 0).start()

    @pl.when(i >= 2)
    def _():
        drain((i + 1) % DISPATCH_BUFS)

    @pl.when(i + 1 < n)
    def _():
        read(i + 1, (i + 1) % DISPATCH_BUFS).start()

    slot = i % DISPATCH_BUFS
    read(i, slot).wait()
    for r in range(TM):
        for k in range(2):
            row_write(slot, r, dst_ref[2 * (i * TM + r) + k]).start(priority=k)

    @pl.when(i == n - 1)
    def _():
        if n >= 2:
            drain((i - 1) % DISPATCH_BUFS)
        drain(slot)
        pad_fill(False)


def _dispatch(v, plan, n_tiles):
    dst, pad_start, pad_len, n_valid = plan[0], plan[1], plan[2], plan[5]
    t = v.shape[0]
    return pl.pallas_call(
        _dispatch_body,
        grid_spec=pltpu.PrefetchScalarGridSpec(
            num_scalar_prefetch=4,
            grid=(t // TM,),
            in_specs=[pl.BlockSpec(memory_space=pl.ANY)],
            out_specs=pl.BlockSpec(memory_space=pl.ANY),
            scratch_shapes=[pltpu.VMEM((DISPATCH_BUFS, TM, HALF), U32),
                            pltpu.VMEM((ROW_PIECES[0], HALF), U32),
                            pltpu.SemaphoreType.DMA((DISPATCH_BUFS,)),
                            pltpu.SemaphoreType.DMA((DISPATCH_BUFS,)),
                            pltpu.SemaphoreType.DMA(())]),
        out_shape=jax.ShapeDtypeStruct((n_tiles * TM_MOE, HALF), U32),
        compiler_params=pltpu.CompilerParams(dimension_semantics=("arbitrary",), vmem_limit_bytes=VMEM_LIMIT),
        name="moe_dispatch",
    )(dst, pad_start, pad_len, n_valid, v)


WEIGHT_SLOTS = 3


def _moe_body(n_tok, plane, tile_e_ref, tile_q0_ref, nvalid_ref, pstart_ref, cnt_ref, orow_ref,
              krank_ref, elist_ref, nexp_ref,
              x_ref, wg_hbm, wu_hbm, wd_hbm, o_hbm, ybuf0, ybuf1, ybuf2, ssem,
              stage_g, stage_u, stage_d, wsem, wg, wu, wd):
    i = pl.program_id(0)
    nv = nvalid_ref[0]
    bufs = (ybuf0, ybuf1, ybuf2)

    def scratch_row0(slot):
        return (slot & 1) * plane + n_tok + (slot >> 1) * TM_MOE

    def row_write(slot, r, dst_row):
        return pltpu.make_async_copy(bufs[slot].at[pl.ds(r, 1)], o_hbm.at[pl.ds(dst_row, 1)], ssem.at[slot])

    def scratch_rows(region):
        return pltpu.make_async_copy(ybuf0, o_hbm.at[pl.ds(scratch_row0(region), TM_MOE)], ssem.at[0])

    def start_writes(tile, slot, rows):
        e = tile_e_ref[tile]
        valid = cnt_ref[e] - tile_q0_ref[tile]
        first = pstart_ref[e] + tile_q0_ref[tile]
        for r in rows:
            row_write(slot, r, jnp.where(r < valid, orow_ref[first + r], scratch_row0(slot) + r)).start(
                priority=r % 2)

    def compute(slot, writes=None):
        quarter = TM_MOE // 4
        batch = lambda q: start_writes(*writes, range(q * quarter, (q + 1) * quarter)) if writes else None
        x_lo, x_hi = (h.astype(BF16) for h in _unpack_halves(x_ref[...]))
        batch(0)
        hg = (jnp.dot(x_lo, wg[0:HALF, :], preferred_element_type=F32)
              + jnp.dot(x_hi, wg[HALF:D_MODEL, :], preferred_element_type=F32))
        batch(1)
        hu = (jnp.dot(x_lo, wu[0:HALF, :], preferred_element_type=F32)
              + jnp.dot(x_hi, wu[HALF:D_MODEL, :], preferred_element_type=F32))
        batch(2)
        act = hg * _sigmoid(hg) * hu
        y = jnp.dot(act.astype(BF16), wd[...], preferred_element_type=F32)
        batch(3)
        bufs[slot][...] = _pack_halves(y)

    @pl.when(i == 0)
    def _():
        ybuf0[...] = jnp.zeros(ybuf0.shape, U32)
        for region in range(4):
            scratch_rows(region).start()
        for region in range(4):
            scratch_rows(region).wait()

    @pl.when(jnp.logical_and(i >= 3, i < nv + 3))
    def _():
        pltpu.make_async_copy(ybuf0, o_hbm.at[pl.ds(0, TM_MOE)], ssem.at[i % 3]).wait()

    def weight_copies(k, go):
        e = elist_ref[k]
        slot = k % WEIGHT_SLOTS
        for w_hbm, st in ((wg_hbm, stage_g), (wu_hbm, stage_u), (wd_hbm, stage_d)):
            d = pltpu.make_async_copy(w_hbm.at[e], st.at[slot], wsem.at[slot])
            d.start() if go else d.wait()

    @pl.when(i == 0)
    def _():
        for k in range(WEIGHT_SLOTS):
            @pl.when(k < nexp_ref[0])
            def _(k=k):
                weight_copies(k, True)

    @pl.when(i < nv)
    def _():
        prev_e = tile_e_ref[jnp.maximum(i - 1, 0)]

        @pl.when(jnp.logical_or(i == 0, tile_e_ref[i] != prev_e))
        def _():
            k = krank_ref[tile_e_ref[i]]
            slot = k % WEIGHT_SLOTS
            weight_copies(k, False)
            wg[...] = stage_g[slot].astype(BF16)
            wu[...] = stage_u[slot].astype(BF16)
            wd[...] = stage_d[slot].astype(BF16)

            @pl.when(k + WEIGHT_SLOTS < nexp_ref[0])
            def _():
                weight_copies(k + WEIGHT_SLOTS, True)

    @pl.when(i == 0)
    def _():
        compute(0)

    for slot in range(3):
        prev = (slot + 2) % 3

        @pl.when(jnp.logical_and(i % 3 == slot, jnp.logical_and(i >= 1, i < nv)))
        def _(slot=slot, prev=prev):
            compute(slot, writes=(i - 1, prev))

        @pl.when(jnp.logical_and(i % 3 == slot, i == nv))
        def _(prev=prev):
            start_writes(i - 1, prev, range(TM_MOE))


def _moe(x_disp, plan, w_gate, w_up, w_down, n_tiles, n_tok):
    tile_e, tile_q0, n_valid, pair_start, counts, order = plan[3:9]
    plane = n_tok + 2 * TM_MOE
    orow = jnp.pad((order & 1) * plane + (order >> 1), (0, TM_MOE))
    present = (counts > 0).astype(I32)
    krank = jnp.cumsum(present) - present
    experts = jnp.arange(N_EXPERTS, dtype=I32)
    elist = jnp.sum(jnp.where((krank[None, :] == experts[:, None]) & (present[None, :] > 0), experts[None, :], 0),
                    axis=1).astype(I32)
    nexp = jnp.sum(present).astype(I32).reshape(1)
    tile = lambda i, nv: jnp.minimum(i, jnp.maximum(nv[0] - 1, 0))
    any_spec = pl.BlockSpec(memory_space=pl.ANY)
    ybuf = pltpu.VMEM((TM_MOE, HALF), U32)
    return pl.pallas_call(
        functools.partial(_moe_body, n_tok, plane),
        grid_spec=pltpu.PrefetchScalarGridSpec(
            num_scalar_prefetch=9,
            grid=(n_tiles + 3,),
            in_specs=[pl.BlockSpec((TM_MOE, HALF), lambda i, te, tq, nv, *_: (tile(i, nv), 0)),
                      any_spec, any_spec, any_spec],
            out_specs=pl.BlockSpec(memory_space=pl.ANY),
            scratch_shapes=[ybuf, ybuf, ybuf,
                            pltpu.SemaphoreType.DMA((3,)),
                            pltpu.VMEM((WEIGHT_SLOTS, D_MODEL, D_EXPERT), F32),
                            pltpu.VMEM((WEIGHT_SLOTS, D_MODEL, D_EXPERT), F32),
                            pltpu.VMEM((WEIGHT_SLOTS, D_EXPERT, D_MODEL), F32),
                            pltpu.SemaphoreType.DMA((WEIGHT_SLOTS,)),
                            pltpu.VMEM((D_MODEL, D_EXPERT), BF16),
                            pltpu.VMEM((D_MODEL, D_EXPERT), BF16),
                            pltpu.VMEM((D_EXPERT, D_MODEL), BF16)]),
        out_shape=jax.ShapeDtypeStruct((2 * plane, HALF), U32),
        compiler_params=pltpu.CompilerParams(dimension_semantics=("arbitrary",), vmem_limit_bytes=VMEM_LIMIT),
        name="moe_experts",
    )(tile_e, tile_q0, n_valid, pair_start, counts, orow, krank.astype(I32), elist, nexp,
      x_disp, w_gate, w_up, w_down)


def _combine_body(n_p_tiles, h2_ref, rec_ref, gf_ref, y1_ref, y2_ref, outp_ref, outs_ref):
    i = pl.program_id(0)
    rec = rec_ref[...]
    y1 = jnp.concatenate(_unpack_halves(y1_ref[...]), axis=-1)
    y2 = jnp.concatenate(_unpack_halves(y2_ref[...]), axis=-1)
    h = rec[:, 2:3] * y1 + rec[:, 3:4] * y2
    out = _rmsnorm(h2_ref[...] + h, gf_ref[...])

    @pl.when(i < n_p_tiles)
    def _():
        outp_ref[...] = out

    @pl.when(i >= n_p_tiles)
    def _():
        outs_ref[...] = out


def _combine(h2, rec, y_pairs, final_norm, t_p, t_s):
    n_p, n_s = t_p // TM, t_s // TM
    plane_tiles = y_pairs.shape[0] // 2 // TM
    return pl.pallas_call(
        functools.partial(_combine_body, n_p),
        grid=(n_p + n_s,),
        in_specs=[pl.BlockSpec((TM, D_MODEL), lambda i: (i, 0)),
                  pl.BlockSpec((TM, ROUTE_W), lambda i: (i, 0)),
                  pl.BlockSpec((1, D_MODEL), lambda i: (0, 0)),
                  pl.BlockSpec((TM, HALF), lambda i: (i, 0)),
                  pl.BlockSpec((TM, HALF), lambda i: (i + plane_tiles, 0))],
        out_specs=[pl.BlockSpec((TM, D_MODEL), lambda i: (jnp.minimum(i, n_p - 1), 0)),
                   pl.BlockSpec((TM, D_MODEL), lambda i: (jnp.clip(i - n_p, 0, n_s - 1), 0))],
        out_shape=[jax.ShapeDtypeStruct((t_p, D_MODEL), F32),
                   jax.ShapeDtypeStruct((t_s, D_MODEL), F32)],
        compiler_params=pltpu.CompilerParams(dimension_semantics=("arbitrary",), vmem_limit_bytes=VMEM_LIMIT),
        name="combine_norm",
    )(h2, rec, final_norm, y_pairs, y_pairs)


def kernel(x_prompt, x_sample, state_conv, state_ssm_re, state_ssm_im, meta_tokens, norm1, w_in, conv_w,
           lam_re, lam_im, log_dt, ssm_b_re, ssm_b_im, ssm_c_re, ssm_c_im, ssm_d, w_glu, w_conv_out,
           w_ssm_out, w_o, norm2, w_coarse, w_fine, w_gate, w_up, w_down, final_norm):
    n_pb, seq, _ = x_prompt.shape
    n_sb, dec_seq, _ = x_sample.shape
    assert dec_seq == CHUNK and seq % TM == 0 and (n_sb * dec_seq) % TM == 0 and N_META == CHUNK
    t_p, t_s = n_pb * seq, n_sb * dec_seq
    xp = x_prompt.reshape(t_p, D_MODEL)
    xsm = x_sample.reshape(t_s, D_MODEL)

    xb, z, xs, zmeta, xsmeta, sga, sgb = _in_proj(xp, xsm, meta_tokens, norm1, w_in[0])

    mats = _s5_chunk_mats(lam_re[0], lam_im[0], log_dt[0], ssm_b_re[0], ssm_b_im[0], ssm_c_re[0], ssm_c_im[0])
    y5, pf_re, pf_im, sf_re, sf_im = _s5(xs, xsmeta, state_ssm_re[0], state_ssm_im[0], mats, n_pb, n_sb, seq)

    buf = state_conv[0]
    zero = jnp.zeros((n_sb, dec_seq, D_CONV), F32)
    inj1 = zero.at[:, 0].set(buf[:, 1]).reshape(t_s, D_CONV)
    inj2 = zero.at[:, 0].set(buf[:, 0]).at[:, 1].set(buf[:, 1]).reshape(t_s, D_CONV)
    merged = _mixers(xb, z, y5, xs, sga, sgb, zmeta, inj1, inj2, conv_w[0], ssm_d, w_conv_out[0], w_glu[0],
                     w_ssm_out[0], t_p // TM, seq // TM)

    w_router = jnp.concatenate(
        [w_fine[0], w_coarse[0], jnp.zeros((D_MODEL, ROUTE_W - N_EXPERTS - N_EGROUPS), F32)], axis=1)
    w_router = jnp.concatenate(_split_bf16(w_router), axis=1)
    h2, v, rec, cnt = _out_proj(merged, xp, xsm, norm2, w_router, w_o[0])

    n_tiles = 2 * (t_p + t_s) // TM_MOE + N_EXPERTS
    plan = _moe_plan(rec, cnt, n_tiles)
    x_disp = _dispatch(v, plan, n_tiles)
    y_pairs = _moe(x_disp, plan, w_gate[0], w_up[0], w_down[0], n_tiles, t_p + t_s)
    y_p, y_s = _combine(h2, rec, y_pairs, final_norm.reshape(1, D_MODEL), t_p, t_s)

    new_conv_p = jnp.stack([z[(b + 1) * seq - 2:(b + 1) * seq] for b in range(n_pb)])
    new_conv_s = z[t_p:].reshape(n_sb, dec_seq, D_CONV)[:, dec_seq - 2:]
    return (y_p.reshape(n_pb, seq, D_MODEL), y_s.reshape(n_sb, dec_seq, D_MODEL),
            new_conv_p[None], pf_re, pf_im, new_conv_s[None], sf_re, sf_im)
```

```python
import functools

import jax
import jax.numpy as jnp
from jax import lax
from jax.experimental import pallas as pl
from jax.experimental.pallas import tpu as pltpu

F32 = jnp.float32
BF16 = jnp.bfloat16
I32 = jnp.int32
U32 = jnp.uint32

D_MODEL = 2048
D_CONV = 1024
D_SSM = 1024
SSM_H = 16
SSM_G = 64
SSM_P = 64
N_META = 16
N_EGROUPS = 4
EXPERTS_PER_GROUP = 8
N_EXPERTS = 32
D_EXPERT = 256
EPS = 1e-6

LANES = 128
SUBLANES = 8

CHUNK = 16
CHUNK_W = CHUNK * SSM_H

TM = 256
TM_MOE = 256
VMEM_LIMIT = 52 * 1024 * 1024


def _rmsnorm(x, g):
    return x * lax.rsqrt(jnp.mean(x * x, axis=-1, keepdims=True) + EPS) * g


def _sigmoid(x):
    return 1.0 / (1.0 + jnp.exp(-x))


def _gelu_tanh(x):
    c = 0.7978845608028654
    return 0.5 * x * (1.0 + jnp.tanh(c * (x + 0.044715 * (x * x * x))))


def _split_bf16(a):
    hi = a.astype(BF16)
    lo = (a - hi.astype(F32)).astype(BF16)
    return hi, lo


def _dot3(a, b):
    a_hi, a_lo = _split_bf16(a)
    b_hi, b_lo = _split_bf16(b)
    return (jnp.dot(a_hi, b_hi, preferred_element_type=F32) + jnp.dot(a_lo, b_hi, preferred_element_type=F32)
            + jnp.dot(a_hi, b_lo, preferred_element_type=F32))


HALF = D_MODEL // 2


def _pack_halves(a):
    return pltpu.pack_elementwise([a[:, :HALF], a[:, HALF:]], packed_dtype=BF16)


def _unpack_halves(p):
    return (pltpu.unpack_elementwise(p, index=0, packed_dtype=BF16, unpacked_dtype=F32),
            pltpu.unpack_elementwise(p, index=1, packed_dtype=BF16, unpacked_dtype=F32))


def _weight_copy(w_hbm, stage, sem, c, slot, rows, col0, ncols):
    return pltpu.make_async_copy(
        w_hbm.at[pl.ds(c * rows, rows), pl.ds(col0, ncols)], stage.at[slot], sem.at[slot])


def _load_weight_bf16(w_hbm, w_vmem, stage, sem, col0=0):
    k, n = w_vmem.shape
    rows = stage.shape[1]
    nchunk = k // rows
    _weight_copy(w_hbm, stage, sem, 0, 0, rows, col0, n).start()
    for c in range(nchunk):
        slot = c % 2
        if c + 1 < nchunk:
            _weight_copy(w_hbm, stage, sem, c + 1, 1 - slot, rows, col0, n).start()
        _weight_copy(w_hbm, stage, sem, c, slot, rows, col0, n).wait()
        w_vmem[pl.ds(c * rows, rows), :] = stage[slot].astype(BF16)


def _in_proj_mix_body(n_p_tiles, xp_ref, xsm_ref, meta_ref, g_ref, w_hbm,
                      xb_out, z_out, xs_out, zmeta_out, xsmeta_out,
                      w_vmem, stage, sem):
    i = pl.program_id(0)
    g = g_ref[...]

    def project(u):
        xb = jnp.dot(u, w_vmem[:, 0:D_CONV], preferred_element_type=F32)
        xc = jnp.dot(u, w_vmem[:, D_CONV:2 * D_CONV], preferred_element_type=F32)
        xv = jnp.dot(u, w_vmem[:, 2 * D_CONV:3 * D_CONV], preferred_element_type=F32)
        xs = jnp.dot(u, w_vmem[:, 3 * D_CONV:3 * D_CONV + D_SSM], preferred_element_type=F32)
        return xb, xc * xv, xs

    @pl.when(i == 0)
    def _():
        _load_weight_bf16(w_hbm, w_vmem, stage, sem, col0=0)
        um = _rmsnorm(meta_ref[...], g).astype(BF16)
        _, zm, xsm = project(um)
        zmeta_out[...] = zm
        xsmeta_out[...] = xsm

    x = jnp.where(i < n_p_tiles, xp_ref[...], xsm_ref[...])
    u = _rmsnorm(x, g).astype(BF16)
    xb, z, xs = project(u)
    xb_out[...] = xb.astype(BF16)
    z_out[...] = z
    xs_out[...] = xs


def _in_proj_gate_body(n_p_tiles, xp_ref, xsm_ref, g_ref, w_hbm, ga_out, gb_out,
                       w_vmem, stage, sem):
    i = pl.program_id(0)

    @pl.when(i == 0)
    def _():
        _load_weight_bf16(w_hbm, w_vmem, stage, sem, col0=3 * D_CONV + D_SSM)

    x = jnp.where(i < n_p_tiles, xp_ref[...], xsm_ref[...])
    u = _rmsnorm(x, g_ref[...]).astype(BF16)
    g = jnp.dot(u, w_vmem[...], preferred_element_type=F32)
    ga_out[...] = _sigmoid(g[:, 0:D_MODEL]).astype(BF16)
    gb_out[...] = _sigmoid(g[:, D_MODEL:2 * D_MODEL]).astype(BF16)


def _two_stream_specs(n_p_tiles, n_s_tiles, tm=TM):
    xp_spec = pl.BlockSpec((tm, D_MODEL), lambda i: (jnp.minimum(i, n_p_tiles - 1), 0))
    mode = pl.Buffered(1) if n_s_tiles == 1 else None
    xs_spec = pl.BlockSpec((tm, D_MODEL), lambda i: (jnp.clip(i - n_p_tiles, 0, n_s_tiles - 1), 0),
                           pipeline_mode=mode)
    return xp_spec, xs_spec


TM_IN = TM


def _in_proj(xp, xsm, meta, norm1, w_in):
    t_p, t_s = xp.shape[0], xsm.shape[0]
    n_p, n_s = t_p // TM_IN, t_s // TM_IN
    t = t_p + t_s
    half = 3 * D_CONV + D_SSM
    xp_spec, xs_spec = _two_stream_specs(n_p, n_s, TM_IN)
    g_spec = pl.BlockSpec((1, D_MODEL), lambda i: (0, 0))
    any_spec = pl.BlockSpec(memory_space=pl.ANY)
    stage_rows = 128
    row = lambda w: pl.BlockSpec((TM_IN, w), lambda i: (i, 0))
    const = lambda r, w: pl.BlockSpec((r, w), lambda i: (0, 0))
    params = pltpu.CompilerParams(dimension_semantics=("arbitrary",), vmem_limit_bytes=VMEM_LIMIT)
    scratch = [pltpu.VMEM((D_MODEL, half), BF16),
               pltpu.VMEM((2, stage_rows, half), F32),
               pltpu.SemaphoreType.DMA((2,))]

    xb, z, xs, zmeta, xsmeta = pl.pallas_call(
        functools.partial(_in_proj_mix_body, n_p),
        grid=(n_p + n_s,),
        in_specs=[xp_spec, xs_spec, const(N_META, D_MODEL), g_spec, any_spec],
        out_specs=[row(D_CONV), row(D_CONV), row(D_SSM), const(N_META, D_CONV), const(N_META, D_SSM)],
        out_shape=[jax.ShapeDtypeStruct((t, D_CONV), BF16),
                   jax.ShapeDtypeStruct((t, D_CONV), F32),
                   jax.ShapeDtypeStruct((t, D_SSM), F32),
                   jax.ShapeDtypeStruct((N_META, D_CONV), F32),
                   jax.ShapeDtypeStruct((N_META, D_SSM), F32)],
        scratch_shapes=scratch,
        compiler_params=params,
        name="in_proj_mix",
    )(xp, xsm, meta, norm1, w_in)

    sga, sgb = pl.pallas_call(
        functools.partial(_in_proj_gate_body, n_p),
        grid=(n_p + n_s,),
        in_specs=[xp_spec, xs_spec, g_spec, any_spec],
        out_specs=[row(D_MODEL), row(D_MODEL)],
        out_shape=[jax.ShapeDtypeStruct((t, D_MODEL), BF16),
                   jax.ShapeDtypeStruct((t, D_MODEL), BF16)],
        scratch_shapes=scratch,
        compiler_params=params,
        name="in_proj_gate",
    )(xp, xsm, norm1, w_in)
    return xb, z, xs, zmeta, xsmeta, sga, sgb


S5_GROUPS_PER_STEP = LANES // SSM_H
S5_PAIRS_PER_STEP = S5_GROUPS_PER_STEP // 2


def _s5_chunk_mats(lam_re, lam_im, log_dt, b_re, b_im, c_re, c_im):
    dt = jnp.exp(log_dt)[:, None]
    lr, li = lam_re, lam_im
    z = jnp.stack([lr * dt, li * dt], axis=1)
    mag = jnp.exp(lr * dt)
    ab_re, ab_im = mag * jnp.cos(li * dt), mag * jnp.sin(li * dt)
    nr, ni = ab_re - 1.0, ab_im
    den = lr * lr + li * li
    k_re = (nr * lr + ni * li) / den
    k_im = (ni * lr - nr * li) / den
    bb = jnp.stack([k_re[..., None] * b_re - k_im[..., None] * b_im,
                    k_re[..., None] * b_im + k_im[..., None] * b_re], axis=1)
    return z, bb, jnp.stack([c_re, c_im], axis=1)


def _cmul(ar, ai, br, bi):
    return ar * br - ai * bi, ar * bi + ai * br


def _s5_operators(pair, parity, z_ref, bb_ref, c_ref, toe_scr, p_scr, q_scr, a16_scr, g_scr, m_scr):
    k = 2 * pair + parity
    zr, zi = z_ref[k, 0:1, :], z_ref[k, 1:2, :]
    mag = jnp.exp(zr)
    ar, ai = mag * jnp.cos(zi), mag * jnp.sin(zi)
    eye = lax.broadcasted_iota(I32, (SSM_P, SSM_P), 0) == lax.broadcasted_iota(I32, (SSM_P, SSM_P), 1)
    acr = jnp.sum(jnp.where(eye, ar, 0.0), axis=1, keepdims=True)
    aci = jnp.sum(jnp.where(eye, ai, 0.0), axis=1, keepdims=True)
    c_re, c_im = c_ref[k, 0], c_ref[k, 1]
    bb_re, bb_im = bb_ref[k, 0], bb_ref[k, 1]

    blk = lax.broadcasted_iota(I32, (1, CHUNK_W), 1) >> 4

    pr, pi = jnp.ones_like(ar), jnp.zeros_like(ar)
    pcr, pci = jnp.ones_like(acr), jnp.zeros_like(acr)
    pw_re = jnp.zeros((SSM_P, CHUNK_W), F32)
    pw_im = jnp.zeros((SSM_P, CHUNK_W), F32)
    for d in range(CHUNK + 1):
        g_scr[0, d * SSM_H:(d + 1) * SSM_H, :] = c_re * pr - c_im * pi
        g_scr[1, d * SSM_H:(d + 1) * SSM_H, :] = c_re * pi + c_im * pr
        if d < CHUNK:
            pw_re = jnp.where(blk == CHUNK - 1 - d, pcr, pw_re)
            pw_im = jnp.where(blk == CHUNK - 1 - d, pci, pw_im)
            pcr, pci = _cmul(pcr, pci, acr, aci)
            pr, pi = _cmul(pr, pi, ar, ai)

    half = parity * SSM_P
    a16_scr[pair, 0, :, half:half + SSM_P] = pr
    a16_scr[pair, 1, :, half:half + SSM_P] = pi
    q_scr[k] = jnp.zeros(q_scr.shape[1:], BF16)
    q_scr[k, 0, :, half:half + SSM_P] = g_scr[0, SSM_H:, :].astype(BF16)
    q_scr[k, 1, :, half:half + SSM_P] = (-g_scr[1, SSM_H:, :]).astype(BF16)

    rep = jnp.where(lax.broadcasted_iota(I32, (SSM_H, CHUNK_W), 0)
                    == (lax.broadcasted_iota(I32, (SSM_H, CHUNK_W), 1) & (SSM_H - 1)), 1.0, 0.0).astype(BF16)

    def widen(b):
        b_hi, b_lo = _split_bf16(b)
        return (jnp.dot(b_hi, rep, preferred_element_type=F32) + jnp.dot(b_lo, rep, preferred_element_type=F32))

    bw_re, bw_im = widen(bb_re), widen(bb_im)
    p_re, p_im = _cmul(pw_re, pw_im, bw_re, bw_im)
    p_scr[k, 0:SSM_P, :] = p_re.astype(BF16)
    p_scr[k, SSM_P:2 * SSM_P, :] = p_im.astype(BF16)

    m_scr[...] = _dot3(g_scr[0, 0:CHUNK_W, :], bw_re) - _dot3(g_scr[1, 0:CHUNK_W, :], bw_im)
    for t in range(CHUNK):
        acc = jnp.where(blk == 0, m_scr[t * SSM_H:(t + 1) * SSM_H, :], 0.0)
        for j in range(1, t + 1):
            acc = jnp.where(blk == j, m_scr[(t - j) * SSM_H:(t - j + 1) * SSM_H, :], acc)
        toe_scr[k, t * SSM_H:(t + 1) * SSM_H, :] = acc.astype(BF16)


def _s5_body(n_pc, n_pb, n_sb, xs_ref, xsmeta_ref, z_ref, bb_ref, c_ref, s0re_ref, s0im_ref,
             y_out, pfre_out, pfim_out, sfre_out, sfim_out,
             u_scr, sl_re, sl_im, sp_re, sp_im, yt_scr, toe_ref, p_ref, q_ref, a16_scr,
             g_scr, m_scr):
    gb = S5_GROUPS_PER_STEP
    n_p_rows = n_pb * n_pc
    row_s = n_p_rows
    row_m = row_s + n_sb
    t_p = n_p_rows * CHUNK
    rows_pad = u_scr.shape[2]

    for k in range(gb):
        _s5_operators(k // 2, k % 2, z_ref, bb_ref, c_ref, toe_ref, p_ref, q_ref, a16_scr, g_scr, m_scr)

    first_row = lax.broadcasted_iota(I32, (SUBLANES, 1), 0) == 0
    tail = jnp.zeros((rows_pad - row_m - SUBLANES, LANES), F32)
    for t in range(CHUNK):
        meta_rows = jnp.where(first_row, xsmeta_ref[t:t + 1, :], 0.0)
        rows_t = jnp.concatenate([xs_ref[pl.ds(t, n_p_rows, stride=CHUNK), :],
                                  xs_ref[pl.ds(t_p + t, n_sb, stride=CHUNK), :], meta_rows, tail], axis=0)
        xt = rows_t.T.astype(BF16)
        for k in range(gb):
            u_scr[k, t * SSM_H:(t + 1) * SSM_H, :] = xt[k * SSM_H:(k + 1) * SSM_H, :]

    npair = S5_PAIRS_PER_STEP
    for j in range(npair):
        sl0 = jnp.dot(p_ref[2 * j], u_scr[2 * j], preferred_element_type=F32)
        sl1 = jnp.dot(p_ref[2 * j + 1], u_scr[2 * j + 1], preferred_element_type=F32)
        sl_re[j] = jnp.concatenate([sl0[0:SSM_P, :], sl1[0:SSM_P, :]], axis=0).T
        sl_im[j] = jnp.concatenate([sl0[SSM_P:2 * SSM_P, :], sl1[SSM_P:2 * SSM_P, :]], axis=0).T
        sp_re[j, row_m:rows_pad, :] = jnp.zeros((rows_pad - row_m, 2 * SSM_P), F32)
        sp_im[j, row_m:rows_pad, :] = jnp.zeros((rows_pad - row_m, 2 * SSM_P), F32)

    ar = [a16_scr[j, 0] for j in range(npair)]
    ai = [a16_scr[j, 1] for j in range(npair)]
    sre = [jnp.broadcast_to(sl_re[j, row_m:row_m + 1, :], (n_pb, 2 * SSM_P)) for j in range(npair)]
    sim = [jnp.broadcast_to(sl_im[j, row_m:row_m + 1, :], (n_pb, 2 * SSM_P)) for j in range(npair)]
    for c in range(n_pc):
        rows = pl.ds(c, n_pb, stride=n_pc)
        for j in range(npair):
            sp_re[j, rows, :] = sre[j]
            sp_im[j, rows, :] = sim[j]
            nre = ar[j] * sre[j] - ai[j] * sim[j] + sl_re[j, rows, :]
            nim = ar[j] * sim[j] + ai[j] * sre[j] + sl_im[j, rows, :]
            sre[j], sim[j] = nre, nim
    for j in range(npair):
        pfre_out[j] = sre[j]
        pfim_out[j] = sim[j]
        s0r, s0i = s0re_ref[j], s0im_ref[j]
        sp_re[j, row_s:row_m, :] = s0r
        sp_im[j, row_s:row_m, :] = s0i
        sfre_out[j] = ar[j] * s0r - ai[j] * s0i + sl_re[j, row_s:row_m, :]
        sfim_out[j] = ar[j] * s0i + ai[j] * s0r + sl_im[j, row_s:row_m, :]

    nt = (((1,), (1,)), ((), ()))
    for k in range(gb):
        y = jnp.dot(toe_ref[k], u_scr[k], preferred_element_type=F32)
        y += lax.dot_general(q_ref[k, 0], sp_re[k // 2].astype(BF16), nt, preferred_element_type=F32)
        y += lax.dot_general(q_ref[k, 1], sp_im[k // 2].astype(BF16), nt, preferred_element_type=F32)
        for t in range(CHUNK):
            yt_scr[t, k * SSM_H:(k + 1) * SSM_H, :] = y[t * SSM_H:(t + 1) * SSM_H, :]
    for t in range(CHUNK):
        yt = yt_scr[t].T
        y_out[pl.ds(t, n_p_rows, stride=CHUNK), :] = yt[0:n_p_rows, :]
        y_out[pl.ds(t_p + t, n_sb, stride=CHUNK), :] = yt[row_s:row_m, :]


def _s5(xs, xsmeta, state_re, state_im, mats, n_pb, n_sb, seq):
    z, bb, c = mats
    t = xs.shape[0]
    n_pc = seq // CHUNK
    rows = n_pc * n_pb + n_sb + 1
    rows_pad = -(-rows // LANES) * LANES
    gb, npair = S5_GROUPS_PER_STEP, S5_PAIRS_PER_STEP
    pairs = lambda s: jnp.transpose(s.reshape(n_sb, SSM_G // 2, 2 * SSM_P), (1, 0, 2))
    blk3 = lambda n, r, c: pl.BlockSpec((n, r, c), lambda i: (i, 0, 0))
    y, pfre, pfim, sfre, sfim = pl.pallas_call(
        functools.partial(_s5_body, n_pc, n_pb, n_sb),
        grid=(SSM_G // gb,),
        in_specs=[pl.BlockSpec((t, LANES), lambda i: (0, i)),
                  pl.BlockSpec((N_META, LANES), lambda i: (0, i)),
                  blk3(gb, 2, SSM_P),
                  pl.BlockSpec((gb, 2, SSM_P, SSM_H), lambda i: (i, 0, 0, 0)),
                  pl.BlockSpec((gb, 2, SSM_H, SSM_P), lambda i: (i, 0, 0, 0)),
                  blk3(npair, n_sb, 2 * SSM_P), blk3(npair, n_sb, 2 * SSM_P)],
        out_specs=[pl.BlockSpec((t, LANES), lambda i: (0, i)),
                   blk3(npair, n_pb, 2 * SSM_P), blk3(npair, n_pb, 2 * SSM_P),
                   blk3(npair, n_sb, 2 * SSM_P), blk3(npair, n_sb, 2 * SSM_P)],
        out_shape=[jax.ShapeDtypeStruct((t, D_SSM), F32),
                   jax.ShapeDtypeStruct((SSM_G // 2, n_pb, 2 * SSM_P), F32),
                   jax.ShapeDtypeStruct((SSM_G // 2, n_pb, 2 * SSM_P), F32),
                   jax.ShapeDtypeStruct((SSM_G // 2, n_sb, 2 * SSM_P), F32),
                   jax.ShapeDtypeStruct((SSM_G // 2, n_sb, 2 * SSM_P), F32)],
        scratch_shapes=[pltpu.VMEM((gb, CHUNK_W, rows_pad), BF16),
                        pltpu.VMEM((npair, rows_pad, 2 * SSM_P), F32),
                        pltpu.VMEM((npair, rows_pad, 2 * SSM_P), F32),
                        pltpu.VMEM((npair, rows_pad, 2 * SSM_P), F32),
                        pltpu.VMEM((npair, rows_pad, 2 * SSM_P), F32),
                        pltpu.VMEM((CHUNK, LANES, rows_pad), F32),
                        pltpu.VMEM((gb, CHUNK_W, CHUNK_W), BF16),
                        pltpu.VMEM((gb, 2 * SSM_P, CHUNK_W), BF16),
                        pltpu.VMEM((gb, 2, CHUNK_W, 2 * SSM_P), BF16),
                        pltpu.VMEM((npair, 2, 1, 2 * SSM_P), F32),
                        pltpu.VMEM((2, (CHUNK + 1) * SSM_H, SSM_P), F32),
                        pltpu.VMEM((CHUNK_W, CHUNK_W), F32)],
        compiler_params=pltpu.CompilerParams(dimension_semantics=("arbitrary",), vmem_limit_bytes=VMEM_LIMIT),
        name="s5_chunks",
    )(xs, xsmeta, z, bb, c, pairs(state_re), pairs(state_im))
    unpair = lambda a: jnp.transpose(a, (1, 0, 2)).reshape(a.shape[1], SSM_G, SSM_P)[None]
    return y, unpair(pfre), unpair(pfim), unpair(sfre), unpair(sfim)


def _mixers_body(n_p_tiles, tiles_per_seq, xb_ref, z_ref, y5_ref, xs_ref, sga_ref, sgb_ref,
                 zmeta_ref, inj1_ref, inj2_ref, cw_ref, dskip_ref,
                 wc_hbm, wg_hbm, wso_hbm, merged_out,
                 wc, wg, wso, stage_a, stage_b, sem, carry):
    i = pl.program_id(0)

    @pl.when(i == 0)
    def _():
        _load_weight_bf16(wc_hbm, wc, stage_a, sem)
        _load_weight_bf16(wg_hbm, wg, stage_b, sem)
        _load_weight_bf16(wso_hbm, wso, stage_a, sem)

    @pl.when(jnp.logical_and(i < n_p_tiles, i % tiles_per_seq == 0))
    def _():
        carry[0:2, :] = zmeta_ref[N_META - 2:N_META, :]

    z = z_ref[...]
    row = lax.broadcasted_iota(I32, (TM, 1), 0)
    is_s = i >= n_p_tiles
    r1 = pltpu.roll(z, 1, 0)
    r2 = pltpu.roll(z, 2, 0)
    c1 = carry[1:2, :]
    c2 = carry[0:1, :]
    pos = jnp.where(is_s, row & (CHUNK - 1), row)
    first1 = pos == 0
    first2 = pos < 2
    fill1 = jnp.where(is_s, inj1_ref[...], jnp.broadcast_to(c1, z.shape))
    fill2 = jnp.where(is_s, inj2_ref[...], jnp.where(row == 0, c2, c1))
    zp1 = jnp.where(first1, fill1, r1)
    zp2 = jnp.where(first2, fill2, r2)
    carry[0:2, :] = z[TM - 2:TM, :]

    cw = cw_ref[...]
    conv = cw[0:1, :] * zp2 + cw[1:2, :] * zp1 + cw[2:3, :] * z
    a_in = (xb_ref[...].astype(F32) * conv).astype(BF16)
    ya = jnp.dot(a_in, wc[...], preferred_element_type=F32)

    ys = y5_ref[...] + dskip_ref[...] * xs_ref[...]
    ys = _gelu_tanh(ys)
    glu = jnp.dot(ys.astype(BF16), wg[...], preferred_element_type=F32)
    ys = ys * _sigmoid(glu)
    yb = jnp.dot(ys.astype(BF16), wso[...], preferred_element_type=F32)

    merged = sga_ref[...].astype(F32) * ya + sgb_ref[...].astype(F32) * yb
    merged_out[...] = merged.astype(BF16)


def _mixers(xb, z, y5, xs, sga, sgb, zmeta, inj1, inj2, conv_w, d_skip, w_conv_out, w_glu, w_ssm_out,
            n_p_tiles, tiles_per_seq):
    t = xb.shape[0]
    n_s_tiles = inj1.shape[0] // TM
    row = lambda w: pl.BlockSpec((TM, w), lambda i: (i, 0))
    const = lambda r, w: pl.BlockSpec((r, w), lambda i: (0, 0))
    inj = pl.BlockSpec((TM, D_CONV), lambda i: (jnp.clip(i - n_p_tiles, 0, n_s_tiles - 1), 0))
    any_spec = pl.BlockSpec(memory_space=pl.ANY)
    return pl.pallas_call(
        functools.partial(_mixers_body, n_p_tiles, tiles_per_seq),
        grid=(t // TM,),
        in_specs=[row(D_CONV), row(D_CONV), row(D_SSM), row(D_SSM), row(D_MODEL), row(D_MODEL),
                  const(N_META, D_CONV), inj, inj, const(3, D_CONV), const(1, D_SSM),
                  any_spec, any_spec, any_spec],
        out_specs=row(D_MODEL),
        out_shape=jax.ShapeDtypeStruct((t, D_MODEL), BF16),
        scratch_shapes=[pltpu.VMEM((D_CONV, D_MODEL), BF16),
                        pltpu.VMEM((D_SSM, D_SSM), BF16),
                        pltpu.VMEM((D_SSM, D_MODEL), BF16),
                        pltpu.VMEM((2, 256, D_MODEL), F32),
                        pltpu.VMEM((2, 256, D_SSM), F32),
                        pltpu.SemaphoreType.DMA((2,)),
                        pltpu.VMEM((8, D_CONV), F32)],
        compiler_params=pltpu.CompilerParams(dimension_semantics=("arbitrary",), vmem_limit_bytes=VMEM_LIMIT),
        name="mixers",
    )(xb, z, y5, xs, sga, sgb, zmeta, inj1, inj2, conv_w, d_skip, w_conv_out, w_glu, w_ssm_out)


ROUTE_W = LANES
COARSE0 = N_EXPERTS


def _route(logits, cnt):
    col = lax.broadcasted_iota(I32, logits.shape, 1)
    colf = col.astype(F32)
    neg = jnp.float32(-jnp.inf)
    big = jnp.float32(1 << 20)
    is_c = jnp.logical_and(col >= COARSE0, col < COARSE0 + N_EGROUPS)
    lc = jnp.where(is_c, logits, neg)
    cmax = jnp.max(lc, axis=-1, keepdims=True)
    gi = jnp.min(jnp.where(lc == cmax, colf - COARSE0, big), axis=-1, keepdims=True)
    pg = 1.0 / jnp.sum(jnp.where(is_c, jnp.exp(lc - cmax), 0.0), axis=-1, keepdims=True)
    grp = (col >> 3).astype(F32)
    in_g = jnp.logical_and(col < N_EXPERTS, grp == gi)
    lf = jnp.where(in_g, logits, neg)
    m1 = jnp.max(lf, axis=-1, keepdims=True)
    i1 = jnp.min(jnp.where(lf == m1, colf, big), axis=-1, keepdims=True)
    lf2 = jnp.where(colf == i1, neg, lf)
    m2 = jnp.max(lf2, axis=-1, keepdims=True)
    i2 = jnp.min(jnp.where(lf2 == m2, colf, big), axis=-1, keepdims=True)
    e2 = jnp.exp(m2 - m1)
    w1 = pg / (1.0 + e2)
    w2 = pg * e2 / (1.0 + e2)
    n = logits.shape[0]
    hit1 = colf == i1
    hit2 = colf == i2
    onehot = jnp.where(jnp.logical_or(hit1, hit2), 1.0, 0.0)
    rr = lax.broadcasted_iota(I32, (n, n), 0)
    cc = lax.broadcasted_iota(I32, (n, n), 1)
    tri = jnp.where(cc < rr, 1.0, 0.0).astype(BF16)
    pos = jnp.dot(tri, onehot.astype(BF16), preferred_element_type=F32) + cnt
    rank1 = jnp.sum(jnp.where(hit1, pos, 0.0), axis=-1, keepdims=True)
    rank2 = jnp.sum(jnp.where(hit2, pos, 0.0), axis=-1, keepdims=True)
    vals = (i1, i2, w1, w2, rank1, rank2)
    rec = jnp.zeros(logits.shape, F32)
    for c, val in enumerate(vals):
        rec = jnp.where(col == c, val, rec)
    return rec, cnt + jnp.sum(onehot, axis=0, keepdims=True)


def _out_proj_body(n_p_tiles, merged_ref, xp_ref, xsm_ref, g2_ref, wr_ref, wo_hbm,
                   h2_out, v_out, rec_out, cnt_out, wo, stage, sem, h2_scr):
    i = pl.program_id(0)
    n = pl.num_programs(0) - 1

    @pl.when(i == 0)
    def _():
        _load_weight_bf16(wo_hbm, wo, stage, sem)
        cnt_out[...] = jnp.zeros(cnt_out.shape, F32)

    def route_prev():
        v = _rmsnorm(h2_scr[...], g2_ref[...])
        v_out[...] = _pack_halves(v)
        v_hi, v_lo = _split_bf16(v)
        both = jnp.dot(v_hi, wr_ref[...], preferred_element_type=F32)
        logits = (both[:, :ROUTE_W] + both[:, ROUTE_W:]
                  + jnp.dot(v_lo, wr_ref[:, :ROUTE_W], preferred_element_type=F32))
        rec, cnt = _route(logits, cnt_out[...])
        rec_out[...] = rec
        cnt_out[...] = cnt

    def project():
        x = jnp.where(i < n_p_tiles, xp_ref[...], xsm_ref[...])
        h2 = x + jnp.dot(merged_ref[...], wo[...], preferred_element_type=F32)
        h2_out[...] = h2
        return h2

    @pl.when(i == 0)
    def _():
        h2_scr[...] = project()

    @pl.when(jnp.logical_and(i > 0, i < n))
    def _():
        route_prev()
        h2_scr[...] = project()

    @pl.when(i == n)
    def _():
        route_prev()


def _out_proj(merged, xp, xsm, norm2, w_router, w_o):
    t_p, t_s = xp.shape[0], xsm.shape[0]
    n_p, n_s = t_p // TM, t_s // TM
    n = n_p + n_s
    t = t_p + t_s
    xp_spec, xs_spec = _two_stream_specs(n_p, n_s)
    cur = lambda w: pl.BlockSpec((TM, w), lambda i: (jnp.minimum(i, n - 1), 0))
    prev = lambda w: pl.BlockSpec((TM, w), lambda i: (jnp.maximum(i - 1, 0), 0))
    const = lambda r, w: pl.BlockSpec((r, w), lambda i: (0, 0))
    return pl.pallas_call(
        functools.partial(_out_proj_body, n_p),
        grid=(n + 1,),
        in_specs=[cur(D_MODEL), xp_spec, xs_spec, const(1, D_MODEL), const(D_MODEL, 2 * ROUTE_W),
                  pl.BlockSpec(memory_space=pl.ANY)],
        out_specs=[cur(D_MODEL), prev(HALF), prev(ROUTE_W), const(1, ROUTE_W)],
        out_shape=[jax.ShapeDtypeStruct((t, D_MODEL), F32),
                   jax.ShapeDtypeStruct((t, HALF), U32),
                   jax.ShapeDtypeStruct((t, ROUTE_W), F32),
                   jax.ShapeDtypeStruct((1, ROUTE_W), F32)],
        scratch_shapes=[pltpu.VMEM((D_MODEL, D_MODEL), BF16),
                        pltpu.VMEM((2, 256, D_MODEL), F32),
                        pltpu.SemaphoreType.DMA((2,)),
                        pltpu.VMEM((TM, D_MODEL), F32)],
        compiler_params=pltpu.CompilerParams(dimension_semantics=("arbitrary",), vmem_limit_bytes=VMEM_LIMIT),
        name="out_proj_route",
    )(merged, xp, xsm, norm2, w_router, w_o)


def _moe_plan(rec, cnt, n_tiles):
    t = rec.shape[0]
    n_pairs = 2 * t
    eid = rec[:, 0:2].astype(I32).reshape(-1)
    rank = rec[:, 4:6].astype(I32).reshape(-1)
    counts = cnt[0, :N_EXPERTS].astype(I32)
    pair_start = jnp.cumsum(counts) - counts
    experts = jnp.arange(N_EXPERTS, dtype=I32)
    onehot = (eid[:, None] == experts[None, :]).astype(I32)
    pos = rank + jnp.sum(onehot * pair_start[None, :], axis=1)
    _, order = lax.sort((pos, jnp.arange(n_pairs, dtype=I32)), num_keys=1)
    tiles_e = (counts + TM_MOE - 1) // TM_MOE
    tile_end = jnp.cumsum(tiles_e)
    tile_start = tile_end - tiles_e
    n_valid = tile_end[-1]
    tile_ids = jnp.arange(n_tiles, dtype=I32)
    tile_e = jnp.sum((tile_ids[:, None] >= tile_end[None, :]).astype(I32), axis=1)
    last_e = jnp.sum((n_valid - 1 >= tile_end).astype(I32))
    tile_e = jnp.minimum(jnp.where(tile_ids < n_valid, tile_e, last_e), N_EXPERTS - 1)
    tile_onehot = (tile_e[:, None] == experts[None, :]).astype(I32)
    tile_q0 = (tile_ids - jnp.sum(tile_onehot * tile_start[None, :], axis=1)) * TM_MOE
    tile_q0 = jnp.where(tile_ids < n_valid, tile_q0, 0)
    dst = rank + jnp.sum(onehot * (tile_start * TM_MOE)[None, :], axis=1)
    pad_start = tile_start * TM_MOE + counts
    pad_len = tiles_e * TM_MOE - counts
    return (dst.astype(I32), pad_start.astype(I32), pad_len.astype(I32), tile_e, tile_q0.astype(I32),
            n_valid.astype(I32).reshape(1), pair_start, counts, order)


DISPATCH_BUFS = 3
ROW_PIECES = tuple(TM_MOE >> (b + 1) for b in range(TM_MOE.bit_length() - 1))


def _dispatch_body(dst_ref, pad_start_ref, pad_len_ref, nvalid_ref, v_hbm, x_hbm, buf, zbuf, rsem, ssem, zsem):
    i = pl.program_id(0)
    n = pl.num_programs(0)

    def read(tile, slot):
        return pltpu.make_async_copy(v_hbm.at[pl.ds(pl.multiple_of(tile * TM, TM), TM)], buf.at[slot],
                                     rsem.at[slot])

    def row_write(slot, r, dst_row):
        return pltpu.make_async_copy(buf.at[slot, pl.ds(r, 1)], x_hbm.at[pl.ds(dst_row, 1)], ssem.at[slot])

    def drain(slot):
        for _ in range(2):
            pltpu.make_async_copy(buf.at[slot], x_hbm.at[pl.ds(0, TM)], ssem.at[slot]).wait()

    def pad_fill(go):
        def zero_rows(start, size):
            d = pltpu.make_async_copy(zbuf.at[pl.ds(0, size)], x_hbm.at[pl.ds(start, size)], zsem)
            d.start() if go else d.wait()

        def body(e, c):
            start, length = pad_start_ref[e], pad_len_ref[e]
            head = (-start) & (SUBLANES - 1)
            for h in range(SUBLANES - 1):
                @pl.when(h < head)
                def _(h=h):
                    zero_rows(start + h, 1)
            start, length = start + head, length - head
            for size in ROW_PIECES:
                if size >= SUBLANES:
                    @pl.when((length & size) != 0)
                    def _(size=size):
                        zero_rows(pl.multiple_of(start + (length & (-2 * size)), SUBLANES), size)
            return c
        lax.fori_loop(0, N_EXPERTS, body, 0)

        def unused(tile, c):
            for half in range(TM_MOE // ROW_PIECES[0]):
                zero_rows(pl.multiple_of(tile * TM_MOE + half * ROW_PIECES[0], SUBLANES), ROW_PIECES[0])
            return c
        lax.fori_loop(nvalid_ref[0], x_hbm.shape[0] // TM_MOE, unused, 0)

    @pl.when(i == 0)
    def _():
        zbuf[...] = jnp.zeros(zbuf.shape, U32)
        pad_fill(True)
        read(0, 0).start()

    @pl.when(i >= 2)
    def _():
        drain((i + 1) % DISPATCH_BUFS)

    @pl.when(i + 1 < n)
    def _():
        read(i + 1, (i + 1) % DISPATCH_BUFS).start()

    slot = i % DISPATCH_BUFS
    read(i, slot).wait()
    for r in range(TM):
        for k in range(2):
            row_write(slot, r, dst_ref[2 * (i * TM + r) + k]).start(priority=k)

    @pl.when(i == n - 1)
    def _():
        if n >= 2:
            drain((i - 1) % DISPATCH_BUFS)
        drain(slot)
        pad_fill(False)


def _dispatch(v, plan, n_tiles):
    dst, pad_start, pad_len, n_valid = plan[0], plan[1], plan[2], plan[5]
    t = v.shape[0]
    return pl.pallas_call(
        _dispatch_body,
        grid_spec=pltpu.PrefetchScalarGridSpec(
            num_scalar_prefetch=4,
            grid=(t // TM,),
            in_specs=[pl.BlockSpec(memory_space=pl.ANY)],
            out_specs=pl.BlockSpec(memory_space=pl.ANY),
            scratch_shapes=[pltpu.VMEM((DISPATCH_BUFS, TM, HALF), U32),
                            pltpu.VMEM((ROW_PIECES[0], HALF), U32),
                            pltpu.SemaphoreType.DMA((DISPATCH_BUFS,)),
                            pltpu.SemaphoreType.DMA((DISPATCH_BUFS,)),
                            pltpu.SemaphoreType.DMA(())]),
        out_shape=jax.ShapeDtypeStruct((n_tiles * TM_MOE, HALF), U32),
        compiler_params=pltpu.CompilerParams(dimension_semantics=("arbitrary",), vmem_limit_bytes=VMEM_LIMIT),
        name="moe_dispatch",
    )(dst, pad_start, pad_len, n_valid, v)


WEIGHT_SLOTS = 3


def _moe_body(tile_e_ref, tile_q0_ref, nvalid_ref, pstart_ref, cnt_ref, orow_ref,
              krank_ref, elist_ref, nexp_ref,
              x_ref, wg_hbm, wu_hbm, wd_hbm, o_hbm, ybuf0, ybuf1, ybuf2, ssem,
              stage_g, stage_u, stage_d, wsem, wg, wu, wd):
    i = pl.program_id(0)
    nv = nvalid_ref[0]
    bufs = (ybuf0, ybuf1, ybuf2)

    def row_write(slot, r, dst_row):
        return pltpu.make_async_copy(bufs[slot].at[pl.ds(r, 1)], o_hbm.at[pl.ds(dst_row, 1)], ssem.at[slot])

    def start_writes(tile, slot, rows):
        e = tile_e_ref[tile]
        valid = cnt_ref[e] - tile_q0_ref[tile]
        first = pstart_ref[e] + tile_q0_ref[tile]
        for r in rows:
            @pl.when(r < valid)
            def _(r=r):
                row_write(slot, r, orow_ref[first + r]).start(priority=r % 2)

    def compute(slot, writes=None):
        quarter = TM_MOE // 4
        batch = lambda q: start_writes(*writes, range(q * quarter, (q + 1) * quarter)) if writes else None
        x_lo, x_hi = (h.astype(BF16) for h in _unpack_halves(x_ref[...]))
        batch(0)
        hg = (jnp.dot(x_lo, wg[0:HALF, :], preferred_element_type=F32)
              + jnp.dot(x_hi, wg[HALF:D_MODEL, :], preferred_element_type=F32))
        batch(1)
        hu = (jnp.dot(x_lo, wu[0:HALF, :], preferred_element_type=F32)
              + jnp.dot(x_hi, wu[HALF:D_MODEL, :], preferred_element_type=F32))
        batch(2)
        act = hg * _sigmoid(hg) * hu
        y = jnp.dot(act.astype(BF16), wd[...], preferred_element_type=F32)
        batch(3)
        bufs[slot][...] = _pack_halves(y)

    @pl.when(jnp.logical_and(i >= 3, i < nv + 3))
    def _():
        tile = i - 3
        valid = jnp.minimum(cnt_ref[tile_e_ref[tile]] - tile_q0_ref[tile], TM_MOE)

        def wait_rows(size):
            pltpu.make_async_copy(ybuf0.at[pl.ds(0, size)], o_hbm.at[pl.ds(0, size)], ssem.at[i % 3]).wait()

        @pl.when(valid == TM_MOE)
        def _():
            wait_rows(TM_MOE)

        for size in ROW_PIECES:
            @pl.when(jnp.logical_and(valid < TM_MOE, (valid & size) != 0))
            def _(size=size):
                wait_rows(size)

    def weight_copies(k, go):
        e = elist_ref[k]
        slot = k % WEIGHT_SLOTS
        for w_hbm, st in ((wg_hbm, stage_g), (wu_hbm, stage_u), (wd_hbm, stage_d)):
            d = pltpu.make_async_copy(w_hbm.at[e], st.at[slot], wsem.at[slot])
            d.start() if go else d.wait()

    @pl.when(i == 0)
    def _():
        for k in range(WEIGHT_SLOTS):
            @pl.when(k < nexp_ref[0])
            def _(k=k):
                weight_copies(k, True)

    @pl.when(i < nv)
    def _():
        prev_e = tile_e_ref[jnp.maximum(i - 1, 0)]

        @pl.when(jnp.logical_or(i == 0, tile_e_ref[i] != prev_e))
        def _():
            k = krank_ref[tile_e_ref[i]]
            slot = k % WEIGHT_SLOTS
            weight_copies(k, False)
            wg[...] = stage_g[slot].astype(BF16)
            wu[...] = stage_u[slot].astype(BF16)
            wd[...] = stage_d[slot].astype(BF16)

            @pl.when(k + WEIGHT_SLOTS < nexp_ref[0])
            def _():
                weight_copies(k + WEIGHT_SLOTS, True)

    @pl.when(i == 0)
    def _():
        compute(0)

    for slot in range(3):
        prev = (slot + 2) % 3

        @pl.when(jnp.logical_and(i % 3 == slot, jnp.logical_and(i >= 1, i < nv)))
        def _(slot=slot, prev=prev):
            compute(slot, writes=(i - 1, prev))

        @pl.when(jnp.logical_and(i % 3 == slot, i == nv))
        def _(prev=prev):
            start_writes(i - 1, prev, range(TM_MOE))


def _moe(x_disp, plan, w_gate, w_up, w_down, n_tiles, n_tok):
    tile_e, tile_q0, n_valid, pair_start, counts, order = plan[3:9]
    orow = jnp.pad((order & 1) * n_tok + (order >> 1), (0, TM_MOE))
    present = (counts > 0).astype(I32)
    krank = jnp.cumsum(present) - present
    experts = jnp.arange(N_EXPERTS, dtype=I32)
    elist = jnp.sum(jnp.where((krank[None, :] == experts[:, None]) & (present[None, :] > 0), experts[None, :], 0),
                    axis=1).astype(I32)
    nexp = jnp.sum(present).astype(I32).reshape(1)
    tile = lambda i, nv: jnp.minimum(i, jnp.maximum(nv[0] - 1, 0))
    any_spec = pl.BlockSpec(memory_space=pl.ANY)
    ybuf = pltpu.VMEM((TM_MOE, HALF), U32)
    return pl.pallas_call(
        _moe_body,
        grid_spec=pltpu.PrefetchScalarGridSpec(
            num_scalar_prefetch=9,
            grid=(n_tiles + 3,),
            in_specs=[pl.BlockSpec((TM_MOE, HALF), lambda i, te, tq, nv, *_: (tile(i, nv), 0)),
                      any_spec, any_spec, any_spec],
            out_specs=pl.BlockSpec(memory_space=pl.ANY),
            scratch_shapes=[ybuf, ybuf, ybuf,
                            pltpu.SemaphoreType.DMA((3,)),
                            pltpu.VMEM((WEIGHT_SLOTS, D_MODEL, D_EXPERT), F32),
                            pltpu.VMEM((WEIGHT_SLOTS, D_MODEL, D_EXPERT), F32),
                            pltpu.VMEM((WEIGHT_SLOTS, D_EXPERT, D_MODEL), F32),
                            pltpu.SemaphoreType.DMA((WEIGHT_SLOTS,)),
                            pltpu.VMEM((D_MODEL, D_EXPERT), BF16),
                            pltpu.VMEM((D_MODEL, D_EXPERT), BF16),
                            pltpu.VMEM((D_EXPERT, D_MODEL), BF16)]),
        out_shape=jax.ShapeDtypeStruct((2 * n_tok, HALF), U32),
        compiler_params=pltpu.CompilerParams(dimension_semantics=("arbitrary",), vmem_limit_bytes=VMEM_LIMIT),
        name="moe_experts",
    )(tile_e, tile_q0, n_valid, pair_start, counts, orow, krank.astype(I32), elist, nexp,
      x_disp, w_gate, w_up, w_down)


def _combine_body(n_p_tiles, h2_ref, rec_ref, gf_ref, y1_ref, y2_ref, outp_ref, outs_ref):
    i = pl.program_id(0)
    rec = rec_ref[...]
    y1 = jnp.concatenate(_unpack_halves(y1_ref[...]), axis=-1)
    y2 = jnp.concatenate(_unpack_halves(y2_ref[...]), axis=-1)
    h = rec[:, 2:3] * y1 + rec[:, 3:4] * y2
    out = _rmsnorm(h2_ref[...] + h, gf_ref[...])

    @pl.when(i < n_p_tiles)
    def _():
        outp_ref[...] = out

    @pl.when(i >= n_p_tiles)
    def _():
        outs_ref[...] = out


def _combine(h2, rec, y_pairs, final_norm, t_p, t_s):
    n_p, n_s = t_p // TM, t_s // TM
    plane_tiles = y_pairs.shape[0] // 2 // TM
    return pl.pallas_call(
        functools.partial(_combine_body, n_p),
        grid=(n_p + n_s,),
        in_specs=[pl.BlockSpec((TM, D_MODEL), lambda i: (i, 0)),
                  pl.BlockSpec((TM, ROUTE_W), lambda i: (i, 0)),
                  pl.BlockSpec((1, D_MODEL), lambda i: (0, 0)),
                  pl.BlockSpec((TM, HALF), lambda i: (i, 0)),
                  pl.BlockSpec((TM, HALF), lambda i: (i + plane_tiles, 0))],
        out_specs=[pl.BlockSpec((TM, D_MODEL), lambda i: (jnp.minimum(i, n_p - 1), 0)),
                   pl.BlockSpec((TM, D_MODEL), lambda i: (jnp.clip(i - n_p, 0, n_s - 1), 0))],
        out_shape=[jax.ShapeDtypeStruct((t_p, D_MODEL), F32),
                   jax.ShapeDtypeStruct((t_s, D_MODEL), F32)],
        compiler_params=pltpu.CompilerParams(dimension_semantics=("arbitrary",), vmem_limit_bytes=VMEM_LIMIT),
        name="combine_norm",
    )(h2, rec, final_norm, y_pairs, y_pairs)


def kernel(x_prompt, x_sample, state_conv, state_ssm_re, state_ssm_im, meta_tokens, norm1, w_in, conv_w,
           lam_re, lam_im, log_dt, ssm_b_re, ssm_b_im, ssm_c_re, ssm_c_im, ssm_d, w_glu, w_conv_out,
           w_ssm_out, w_o, norm2, w_coarse, w_fine, w_gate, w_up, w_down, final_norm):
    n_pb, seq, _ = x_prompt.shape
    n_sb, dec_seq, _ = x_sample.shape
    assert dec_seq == CHUNK and seq % TM == 0 and (n_sb * dec_seq) % TM == 0 and N_META == CHUNK
    t_p, t_s = n_pb * seq, n_sb * dec_seq
    xp = x_prompt.reshape(t_p, D_MODEL)
    xsm = x_sample.reshape(t_s, D_MODEL)

    xb, z, xs, zmeta, xsmeta, sga, sgb = _in_proj(xp, xsm, meta_tokens, norm1, w_in[0])

    mats = _s5_chunk_mats(lam_re[0], lam_im[0], log_dt[0], ssm_b_re[0], ssm_b_im[0], ssm_c_re[0], ssm_c_im[0])
    y5, pf_re, pf_im, sf_re, sf_im = _s5(xs, xsmeta, state_ssm_re[0], state_ssm_im[0], mats, n_pb, n_sb, seq)

    buf = state_conv[0]
    zero = jnp.zeros((n_sb, dec_seq, D_CONV), F32)
    inj1 = zero.at[:, 0].set(buf[:, 1]).reshape(t_s, D_CONV)
    inj2 = zero.at[:, 0].set(buf[:, 0]).at[:, 1].set(buf[:, 1]).reshape(t_s, D_CONV)
    merged = _mixers(xb, z, y5, xs, sga, sgb, zmeta, inj1, inj2, conv_w[0], ssm_d, w_conv_out[0], w_glu[0],
                     w_ssm_out[0], t_p // TM, seq // TM)

    w_router = jnp.concatenate(
        [w_fine[0], w_coarse[0], jnp.zeros((D_MODEL, ROUTE_W - N_EXPERTS - N_EGROUPS), F32)], axis=1)
    w_router = jnp.concatenate(_split_bf16(w_router), axis=1)
    h2, v, rec, cnt = _out_proj(merged, xp, xsm, norm2, w_router, w_o[0])

    n_tiles = 2 * (t_p + t_s) // TM_MOE + N_EXPERTS
    plan = _moe_plan(rec, cnt, n_tiles)
    x_disp = _dispatch(v, plan, n_tiles)
    y_pairs = _moe(x_disp, plan, w_gate[0], w_up[0], w_down[0], n_tiles, t_p + t_s)
    y_p, y_s = _combine(h2, rec, y_pairs, final_norm.reshape(1, D_MODEL), t_p, t_s)

    new_conv_p = jnp.stack([z[(b + 1) * seq - 2:(b + 1) * seq] for b in range(n_pb)])
    new_conv_s = z[t_p:].reshape(n_sb, dec_seq, D_CONV)[:, dec_seq - 2:]
    return (y_p.reshape(n_pb, seq, D_MODEL), y_s.reshape(n_sb, dec_seq, D_MODEL),
            new_conv_p[None], pf_re, pf_im, new_conv_s[None], sf_re, sf_im)
```

```python
import functools

import jax
import jax.numpy as jnp
from jax import lax
from jax.experimental import pallas as pl
from jax.experimental.pallas import tpu as pltpu

F32 = jnp.float32
BF16 = jnp.bfloat16
I32 = jnp.int32
U32 = jnp.uint32

D_MODEL = 2048
D_CONV = 1024
D_SSM = 1024
SSM_H = 16
SSM_G = 64
SSM_P = 64
N_META = 16
N_EGROUPS = 4
EXPERTS_PER_GROUP = 8
N_EXPERTS = 32
D_EXPERT = 256
EPS = 1e-6

LANES = 128
SUBLANES = 8

CHUNK = 16
CHUNK_W = CHUNK * SSM_H

TM = 256
TM_MOE = 256
VMEM_LIMIT = 52 * 1024 * 1024


def _rmsnorm(x, g):
    return x * lax.rsqrt(jnp.mean(x * x, axis=-1, keepdims=True) + EPS) * g


def _sigmoid(x):
    return 1.0 / (1.0 + jnp.exp(-x))


def _gelu_tanh(x):
    c = 0.7978845608028654
    return 0.5 * x * (1.0 + jnp.tanh(c * (x + 0.044715 * (x * x * x))))


def _split_bf16(a):
    hi = a.astype(BF16)
    lo = (a - hi.astype(F32)).astype(BF16)
    return hi, lo


def _dot3(a, b):
    a_hi, a_lo = _split_bf16(a)
    b_hi, b_lo = _split_bf16(b)
    return (jnp.dot(a_hi, b_hi, preferred_element_type=F32) + jnp.dot(a_lo, b_hi, preferred_element_type=F32)
            + jnp.dot(a_hi, b_lo, preferred_element_type=F32))


HALF = D_MODEL // 2


def _pack_halves(a):
    return pltpu.pack_elementwise([a[:, :HALF], a[:, HALF:]], packed_dtype=BF16)


def _unpack_halves(p):
    return (pltpu.unpack_elementwise(p, index=0, packed_dtype=BF16, unpacked_dtype=F32),
            pltpu.unpack_elementwise(p, index=1, packed_dtype=BF16, unpacked_dtype=F32))


def _weight_copy(w_hbm, stage, sem, c, slot, rows, col0, ncols):
    return pltpu.make_async_copy(
        w_hbm.at[pl.ds(c * rows, rows), pl.ds(col0, ncols)], stage.at[slot], sem.at[slot])


def _load_weight_bf16(w_hbm, w_vmem, stage, sem, col0=0):
    k, n = w_vmem.shape
    rows = stage.shape[1]
    nchunk = k // rows
    _weight_copy(w_hbm, stage, sem, 0, 0, rows, col0, n).start()
    for c in range(nchunk):
        slot = c % 2
        if c + 1 < nchunk:
            _weight_copy(w_hbm, stage, sem, c + 1, 1 - slot, rows, col0, n).start()
        _weight_copy(w_hbm, stage, sem, c, slot, rows, col0, n).wait()
        w_vmem[pl.ds(c * rows, rows), :] = stage[slot].astype(BF16)


def _in_proj_mix_body(n_p_tiles, xp_ref, xsm_ref, meta_ref, g_ref, w_hbm,
                      xb_out, z_out, xs_out, zmeta_out, xsmeta_out,
                      w_vmem, stage, sem):
    i = pl.program_id(0)
    g = g_ref[...]

    def project(u):
        xb = jnp.dot(u, w_vmem[:, 0:D_CONV], preferred_element_type=F32)
        xc = jnp.dot(u, w_vmem[:, D_CONV:2 * D_CONV], preferred_element_type=F32)
        xv = jnp.dot(u, w_vmem[:, 2 * D_CONV:3 * D_CONV], preferred_element_type=F32)
        xs = jnp.dot(u, w_vmem[:, 3 * D_CONV:3 * D_CONV + D_SSM], preferred_element_type=F32)
        return xb, xc * xv, xs

    @pl.when(i == 0)
    def _():
        _load_weight_bf16(w_hbm, w_vmem, stage, sem, col0=0)
        um = _rmsnorm(meta_ref[...], g).astype(BF16)
        _, zm, xsm = project(um)
        zmeta_out[...] = zm
        xsmeta_out[...] = xsm

    x = jnp.where(i < n_p_tiles, xp_ref[...], xsm_ref[...])
    u = _rmsnorm(x, g).astype(BF16)
    xb, z, xs = project(u)
    xb_out[...] = xb.astype(BF16)
    z_out[...] = z
    xs_out[...] = xs


def _in_proj_gate_body(n_p_tiles, xp_ref, xsm_ref, g_ref, w_hbm, ga_out, gb_out,
                       w_vmem, stage, sem):
    i = pl.program_id(0)

    @pl.when(i == 0)
    def _():
        _load_weight_bf16(w_hbm, w_vmem, stage, sem, col0=3 * D_CONV + D_SSM)

    x = jnp.where(i < n_p_tiles, xp_ref[...], xsm_ref[...])
    u = _rmsnorm(x, g_ref[...]).astype(BF16)
    g = jnp.dot(u, w_vmem[...], preferred_element_type=F32)
    ga_out[...] = _sigmoid(g[:, 0:D_MODEL]).astype(BF16)
    gb_out[...] = _sigmoid(g[:, D_MODEL:2 * D_MODEL]).astype(BF16)


def _two_stream_specs(n_p_tiles, n_s_tiles, tm=TM):
    xp_spec = pl.BlockSpec((tm, D_MODEL), lambda i: (jnp.minimum(i, n_p_tiles - 1), 0))
    mode = pl.Buffered(1) if n_s_tiles == 1 else None
    xs_spec = pl.BlockSpec((tm, D_MODEL), lambda i: (jnp.clip(i - n_p_tiles, 0, n_s_tiles - 1), 0),
                           pipeline_mode=mode)
    return xp_spec, xs_spec


TM_IN = TM


def _in_proj(xp, xsm, meta, norm1, w_in):
    t_p, t_s = xp.shape[0], xsm.shape[0]
    n_p, n_s = t_p // TM_IN, t_s // TM_IN
    t = t_p + t_s
    half = 3 * D_CONV + D_SSM
    xp_spec, xs_spec = _two_stream_specs(n_p, n_s, TM_IN)
    g_spec = pl.BlockSpec((1, D_MODEL), lambda i: (0, 0))
    any_spec = pl.BlockSpec(memory_space=pl.ANY)
    stage_rows = 128
    row = lambda w: pl.BlockSpec((TM_IN, w), lambda i: (i, 0))
    const = lambda r, w: pl.BlockSpec((r, w), lambda i: (0, 0))
    params = pltpu.CompilerParams(dimension_semantics=("arbitrary",), vmem_limit_bytes=VMEM_LIMIT)
    scratch = [pltpu.VMEM((D_MODEL, half), BF16),
               pltpu.VMEM((2, stage_rows, half), F32),
               pltpu.SemaphoreType.DMA((2,))]

    xb, z, xs, zmeta, xsmeta = pl.pallas_call(
        functools.partial(_in_proj_mix_body, n_p),
        grid=(n_p + n_s,),
        in_specs=[xp_spec, xs_spec, const(N_META, D_MODEL), g_spec, any_spec],
        out_specs=[row(D_CONV), row(D_CONV), row(D_SSM), const(N_META, D_CONV), const(N_META, D_SSM)],
        out_shape=[jax.ShapeDtypeStruct((t, D_CONV), BF16),
                   jax.ShapeDtypeStruct((t, D_CONV), F32),
                   jax.ShapeDtypeStruct((t, D_SSM), F32),
                   jax.ShapeDtypeStruct((N_META, D_CONV), F32),
                   jax.ShapeDtypeStruct((N_META, D_SSM), F32)],
        scratch_shapes=scratch,
        compiler_params=params,
        name="in_proj_mix",
    )(xp, xsm, meta, norm1, w_in)

    sga, sgb = pl.pallas_call(
        functools.partial(_in_proj_gate_body, n_p),
        grid=(n_p + n_s,),
        in_specs=[xp_spec, xs_spec, g_spec, any_spec],
        out_specs=[row(D_MODEL), row(D_MODEL)],
        out_shape=[jax.ShapeDtypeStruct((t, D_MODEL), BF16),
                   jax.ShapeDtypeStruct((t, D_MODEL), BF16)],
        scratch_shapes=scratch,
        compiler_params=params,
        name="in_proj_gate",
    )(xp, xsm, norm1, w_in)
    return xb, z, xs, zmeta, xsmeta, sga, sgb


S5_GROUPS_PER_STEP = LANES // SSM_H
S5_PAIRS_PER_STEP = S5_GROUPS_PER_STEP // 2


def _s5_chunk_mats(lam_re, lam_im, log_dt, b_re, b_im, c_re, c_im):
    dt = jnp.exp(log_dt)[:, None]
    lr, li = lam_re, lam_im
    z = jnp.stack([lr * dt, li * dt], axis=1)
    mag = jnp.exp(lr * dt)
    ab_re, ab_im = mag * jnp.cos(li * dt), mag * jnp.sin(li * dt)
    nr, ni = ab_re - 1.0, ab_im
    den = lr * lr + li * li
    k_re = (nr * lr + ni * li) / den
    k_im = (ni * lr - nr * li) / den
    bb = jnp.stack([k_re[..., None] * b_re - k_im[..., None] * b_im,
                    k_re[..., None] * b_im + k_im[..., None] * b_re], axis=1)
    return z, bb, jnp.stack([c_re, c_im], axis=1)


def _cmul(ar, ai, br, bi):
    return ar * br - ai * bi, ar * bi + ai * br


def _s5_operators(pair, parity, z_ref, bb_ref, c_ref, toe_scr, p_scr, q_scr, a16_scr, g_scr, m_scr):
    k = 2 * pair + parity
    zr, zi = z_ref[k, 0:1, :], z_ref[k, 1:2, :]
    mag = jnp.exp(zr)
    ar, ai = mag * jnp.cos(zi), mag * jnp.sin(zi)
    eye = lax.broadcasted_iota(I32, (SSM_P, SSM_P), 0) == lax.broadcasted_iota(I32, (SSM_P, SSM_P), 1)
    acr = jnp.sum(jnp.where(eye, ar, 0.0), axis=1, keepdims=True)
    aci = jnp.sum(jnp.where(eye, ai, 0.0), axis=1, keepdims=True)
    c_re, c_im = c_ref[k, 0], c_ref[k, 1]
    bb_re, bb_im = bb_ref[k, 0], bb_ref[k, 1]

    blk = lax.broadcasted_iota(I32, (1, CHUNK_W), 1) >> 4

    pr, pi = jnp.ones_like(ar), jnp.zeros_like(ar)
    pcr, pci = jnp.ones_like(acr), jnp.zeros_like(acr)
    pw_re = jnp.zeros((SSM_P, CHUNK_W), F32)
    pw_im = jnp.zeros((SSM_P, CHUNK_W), F32)
    for d in range(CHUNK + 1):
        g_scr[0, d * SSM_H:(d + 1) * SSM_H, :] = c_re * pr - c_im * pi
        g_scr[1, d * SSM_H:(d + 1) * SSM_H, :] = c_re * pi + c_im * pr
        if d < CHUNK:
            pw_re = jnp.where(blk == CHUNK - 1 - d, pcr, pw_re)
            pw_im = jnp.where(blk == CHUNK - 1 - d, pci, pw_im)
            pcr, pci = _cmul(pcr, pci, acr, aci)
            pr, pi = _cmul(pr, pi, ar, ai)

    half = parity * SSM_P
    a16_scr[pair, 0, :, half:half + SSM_P] = pr
    a16_scr[pair, 1, :, half:half + SSM_P] = pi
    q_scr[k] = jnp.zeros(q_scr.shape[1:], BF16)
    q_scr[k, 0, :, half:half + SSM_P] = g_scr[0, SSM_H:, :].astype(BF16)
    q_scr[k, 1, :, half:half + SSM_P] = (-g_scr[1, SSM_H:, :]).astype(BF16)

    rep = jnp.where(lax.broadcasted_iota(I32, (SSM_H, CHUNK_W), 0)
                    == (lax.broadcasted_iota(I32, (SSM_H, CHUNK_W), 1) & (SSM_H - 1)), 1.0, 0.0).astype(BF16)

    def widen(b):
        b_hi, b_lo = _split_bf16(b)
        return (jnp.dot(b_hi, rep, preferred_element_type=F32) + jnp.dot(b_lo, rep, preferred_element_type=F32))

    bw_re, bw_im = widen(bb_re), widen(bb_im)
    p_re, p_im = _cmul(pw_re, pw_im, bw_re, bw_im)
    p_scr[k, 0:SSM_P, :] = p_re.astype(BF16)
    p_scr[k, SSM_P:2 * SSM_P, :] = p_im.astype(BF16)

    m_scr[...] = _dot3(g_scr[0, 0:CHUNK_W, :], bw_re) - _dot3(g_scr[1, 0:CHUNK_W, :], bw_im)
    for t in range(CHUNK):
        acc = jnp.where(blk == 0, m_scr[t * SSM_H:(t + 1) * SSM_H, :], 0.0)
        for j in range(1, t + 1):
            acc = jnp.where(blk == j, m_scr[(t - j) * SSM_H:(t - j + 1) * SSM_H, :], acc)
        toe_scr[k, t * SSM_H:(t + 1) * SSM_H, :] = acc.astype(BF16)


def _s5_body(n_pc, n_pb, n_sb, xs_ref, xsmeta_ref, z_ref, bb_ref, c_ref, s0re_ref, s0im_ref,
             y_out, pfre_out, pfim_out, sfre_out, sfim_out,
             u_scr, sl_re, sl_im, sp_re, sp_im, yt_scr, toe_ref, p_ref, q_ref, a16_scr,
             g_scr, m_scr):
    gb = S5_GROUPS_PER_STEP
    n_p_rows = n_pb * n_pc
    row_s = n_p_rows
    row_m = row_s + n_sb
    t_p = n_p_rows * CHUNK
    rows_pad = u_scr.shape[2]

    for k in range(gb):
        _s5_operators(k // 2, k % 2, z_ref, bb_ref, c_ref, toe_ref, p_ref, q_ref, a16_scr, g_scr, m_scr)

    first_row = lax.broadcasted_iota(I32, (SUBLANES, 1), 0) == 0
    tail = jnp.zeros((rows_pad - row_m - SUBLANES, LANES), F32)
    for t in range(CHUNK):
        meta_rows = jnp.where(first_row, xsmeta_ref[t:t + 1, :], 0.0)
        rows_t = jnp.concatenate([xs_ref[pl.ds(t, n_p_rows, stride=CHUNK), :],
                                  xs_ref[pl.ds(t_p + t, n_sb, stride=CHUNK), :], meta_rows, tail], axis=0)
        xt = rows_t.T.astype(BF16)
        for k in range(gb):
            u_scr[k, t * SSM_H:(t + 1) * SSM_H, :] = xt[k * SSM_H:(k + 1) * SSM_H, :]

    npair = S5_PAIRS_PER_STEP
    for j in range(npair):
        sl0 = jnp.dot(p_ref[2 * j], u_scr[2 * j], preferred_element_type=F32)
        sl1 = jnp.dot(p_ref[2 * j + 1], u_scr[2 * j + 1], preferred_element_type=F32)
        sl_re[j] = jnp.concatenate([sl0[0:SSM_P, :], sl1[0:SSM_P, :]], axis=0).T
        sl_im[j] = jnp.concatenate([sl0[SSM_P:2 * SSM_P, :], sl1[SSM_P:2 * SSM_P, :]], axis=0).T
        sp_re[j, row_m:rows_pad, :] = jnp.zeros((rows_pad - row_m, 2 * SSM_P), F32)
        sp_im[j, row_m:rows_pad, :] = jnp.zeros((rows_pad - row_m, 2 * SSM_P), F32)

    ar = [a16_scr[j, 0] for j in range(npair)]
    ai = [a16_scr[j, 1] for j in range(npair)]
    sre = [jnp.broadcast_to(sl_re[j, row_m:row_m + 1, :], (n_pb, 2 * SSM_P)) for j in range(npair)]
    sim = [jnp.broadcast_to(sl_im[j, row_m:row_m + 1, :], (n_pb, 2 * SSM_P)) for j in range(npair)]
    for c in range(n_pc):
        rows = pl.ds(c, n_pb, stride=n_pc)
        for j in range(npair):
            sp_re[j, rows, :] = sre[j]
            sp_im[j, rows, :] = sim[j]
            nre = ar[j] * sre[j] - ai[j] * sim[j] + sl_re[j, rows, :]
            nim = ar[j] * sim[j] + ai[j] * sre[j] + sl_im[j, rows, :]
            sre[j], sim[j] = nre, nim
    for j in range(npair):
        pfre_out[j] = sre[j]
        pfim_out[j] = sim[j]
        s0r, s0i = s0re_ref[j], s0im_ref[j]
        sp_re[j, row_s:row_m, :] = s0r
        sp_im[j, row_s:row_m, :] = s0i
        sfre_out[j] = ar[j] * s0r - ai[j] * s0i + sl_re[j, row_s:row_m, :]
        sfim_out[j] = ar[j] * s0i + ai[j] * s0r + sl_im[j, row_s:row_m, :]

    nt = (((1,), (1,)), ((), ()))
    for k in range(gb):
        y = jnp.dot(toe_ref[k], u_scr[k], preferred_element_type=F32)
        y += lax.dot_general(q_ref[k, 0], sp_re[k // 2].astype(BF16), nt, preferred_element_type=F32)
        y += lax.dot_general(q_ref[k, 1], sp_im[k // 2].astype(BF16), nt, preferred_element_type=F32)
        for t in range(CHUNK):
            yt_scr[t, k * SSM_H:(k + 1) * SSM_H, :] = y[t * SSM_H:(t + 1) * SSM_H, :]
    for t in range(CHUNK):
        yt = yt_scr[t].T
        y_out[pl.ds(t, n_p_rows, stride=CHUNK), :] = yt[0:n_p_rows, :]
        y_out[pl.ds(t_p + t, n_sb, stride=CHUNK), :] = yt[row_s:row_m, :]


def _s5(xs, xsmeta, state_re, state_im, mats, n_pb, n_sb, seq):
    z, bb, c = mats
    t = xs.shape[0]
    n_pc = seq // CHUNK
    rows = n_pc * n_pb + n_sb + 1
    rows_pad = -(-rows // LANES) * LANES
    gb, npair = S5_GROUPS_PER_STEP, S5_PAIRS_PER_STEP
    pairs = lambda s: jnp.transpose(s.reshape(n_sb, SSM_G // 2, 2 * SSM_P), (1, 0, 2))
    blk3 = lambda n, r, c: pl.BlockSpec((n, r, c), lambda i: (i, 0, 0))
    y, pfre, pfim, sfre, sfim = pl.pallas_call(
        functools.partial(_s5_body, n_pc, n_pb, n_sb),
        grid=(SSM_G // gb,),
        in_specs=[pl.BlockSpec((t, LANES), lambda i: (0, i)),
                  pl.BlockSpec((N_META, LANES), lambda i: (0, i)),
                  blk3(gb, 2, SSM_P),
                  pl.BlockSpec((gb, 2, SSM_P, SSM_H), lambda i: (i, 0, 0, 0)),
                  pl.BlockSpec((gb, 2, SSM_H, SSM_P), lambda i: (i, 0, 0, 0)),
                  blk3(npair, n_sb, 2 * SSM_P), blk3(npair, n_sb, 2 * SSM_P)],
        out_specs=[pl.BlockSpec((t, LANES), lambda i: (0, i)),
                   blk3(npair, n_pb, 2 * SSM_P), blk3(npair, n_pb, 2 * SSM_P),
                   blk3(npair, n_sb, 2 * SSM_P), blk3(npair, n_sb, 2 * SSM_P)],
        out_shape=[jax.ShapeDtypeStruct((t, D_SSM), F32),
                   jax.ShapeDtypeStruct((SSM_G // 2, n_pb, 2 * SSM_P), F32),
                   jax.ShapeDtypeStruct((SSM_G // 2, n_pb, 2 * SSM_P), F32),
                   jax.ShapeDtypeStruct((SSM_G // 2, n_sb, 2 * SSM_P), F32),
                   jax.ShapeDtypeStruct((SSM_G // 2, n_sb, 2 * SSM_P), F32)],
        scratch_shapes=[pltpu.VMEM((gb, CHUNK_W, rows_pad), BF16),
                        pltpu.VMEM((npair, rows_pad, 2 * SSM_P), F32),
                        pltpu.VMEM((npair, rows_pad, 2 * SSM_P), F32),
                        pltpu.VMEM((npair, rows_pad, 2 * SSM_P), F32),
                        pltpu.VMEM((npair, rows_pad, 2 * SSM_P), F32),
                        pltpu.VMEM((CHUNK, LANES, rows_pad), F32),
                        pltpu.VMEM((gb, CHUNK_W, CHUNK_W), BF16),
                        pltpu.VMEM((gb, 2 * SSM_P, CHUNK_W), BF16),
                        pltpu.VMEM((gb, 2, CHUNK_W, 2 * SSM_P), BF16),
                        pltpu.VMEM((npair, 2, 1, 2 * SSM_P), F32),
                        pltpu.VMEM((2, (CHUNK + 1) * SSM_H, SSM_P), F32),
                        pltpu.VMEM((CHUNK_W, CHUNK_W), F32)],
        compiler_params=pltpu.CompilerParams(dimension_semantics=("arbitrary",), vmem_limit_bytes=VMEM_LIMIT),
        name="s5_chunks",
    )(xs, xsmeta, z, bb, c, pairs(state_re), pairs(state_im))
    unpair = lambda a: jnp.transpose(a, (1, 0, 2)).reshape(a.shape[1], SSM_G, SSM_P)[None]
    return y, unpair(pfre), unpair(pfim), unpair(sfre), unpair(sfim)


def _mixers_body(n_p_tiles, tiles_per_seq, xb_ref, z_ref, y5_ref, xs_ref, sga_ref, sgb_ref,
                 zmeta_ref, inj1_ref, inj2_ref, cw_ref, dskip_ref,
                 wc_hbm, wg_hbm, wso_hbm, merged_out,
                 wc, wg, wso, stage_a, stage_b, sem, carry):
    i = pl.program_id(0)

    @pl.when(i == 0)
    def _():
        _load_weight_bf16(wc_hbm, wc, stage_a, sem)
        _load_weight_bf16(wg_hbm, wg, stage_b, sem)
        _load_weight_bf16(wso_hbm, wso, stage_a, sem)

    @pl.when(jnp.logical_and(i < n_p_tiles, i % tiles_per_seq == 0))
    def _():
        carry[0:2, :] = zmeta_ref[N_META - 2:N_META, :]

    z = z_ref[...]
    row = lax.broadcasted_iota(I32, (TM, 1), 0)
    is_s = i >= n_p_tiles
    r1 = pltpu.roll(z, 1, 0)
    r2 = pltpu.roll(z, 2, 0)
    c1 = carry[1:2, :]
    c2 = carry[0:1, :]
    pos = jnp.where(is_s, row & (CHUNK - 1), row)
    first1 = pos == 0
    first2 = pos < 2
    fill1 = jnp.where(is_s, inj1_ref[...], jnp.broadcast_to(c1, z.shape))
    fill2 = jnp.where(is_s, inj2_ref[...], jnp.where(row == 0, c2, c1))
    zp1 = jnp.where(first1, fill1, r1)
    zp2 = jnp.where(first2, fill2, r2)
    carry[0:2, :] = z[TM - 2:TM, :]

    cw = cw_ref[...]
    conv = cw[0:1, :] * zp2 + cw[1:2, :] * zp1 + cw[2:3, :] * z
    a_in = (xb_ref[...].astype(F32) * conv).astype(BF16)
    ya = jnp.dot(a_in, wc[...], preferred_element_type=F32)

    ys = y5_ref[...] + dskip_ref[...] * xs_ref[...]
    ys = _gelu_tanh(ys)
    glu = jnp.dot(ys.astype(BF16), wg[...], preferred_element_type=F32)
    ys = ys * _sigmoid(glu)
    yb = jnp.dot(ys.astype(BF16), wso[...], preferred_element_type=F32)

    merged = sga_ref[...].astype(F32) * ya + sgb_ref[...].astype(F32) * yb
    merged_out[...] = merged.astype(BF16)


def _mixers(xb, z, y5, xs, sga, sgb, zmeta, inj1, inj2, conv_w, d_skip, w_conv_out, w_glu, w_ssm_out,
            n_p_tiles, tiles_per_seq):
    t = xb.shape[0]
    n_s_tiles = inj1.shape[0] // TM
    row = lambda w: pl.BlockSpec((TM, w), lambda i: (i, 0))
    const = lambda r, w: pl.BlockSpec((r, w), lambda i: (0, 0))
    inj = pl.BlockSpec((TM, D_CONV), lambda i: (jnp.clip(i - n_p_tiles, 0, n_s_tiles - 1), 0))
    any_spec = pl.BlockSpec(memory_space=pl.ANY)
    return pl.pallas_call(
        functools.partial(_mixers_body, n_p_tiles, tiles_per_seq),
        grid=(t // TM,),
        in_specs=[row(D_CONV), row(D_CONV), row(D_SSM), row(D_SSM), row(D_MODEL), row(D_MODEL),
                  const(N_META, D_CONV), inj, inj, const(3, D_CONV), const(1, D_SSM),
                  any_spec, any_spec, any_spec],
        out_specs=row(D_MODEL),
        out_shape=jax.ShapeDtypeStruct((t, D_MODEL), BF16),
        scratch_shapes=[pltpu.VMEM((D_CONV, D_MODEL), BF16),
                        pltpu.VMEM((D_SSM, D_SSM), BF16),
                        pltpu.VMEM((D_SSM, D_MODEL), BF16),
                        pltpu.VMEM((2, 256, D_MODEL), F32),
                        pltpu.VMEM((2, 256, D_SSM), F32),
                        pltpu.SemaphoreType.DMA((2,)),
                        pltpu.VMEM((8, D_CONV), F32)],
        compiler_params=pltpu.CompilerParams(dimension_semantics=("arbitrary",), vmem_limit_bytes=VMEM_LIMIT),
        name="mixers",
    )(xb, z, y5, xs, sga, sgb, zmeta, inj1, inj2, conv_w, d_skip, w_conv_out, w_glu, w_ssm_out)


ROUTE_W = LANES
COARSE0 = N_EXPERTS


def _route(logits, cnt):
    col = lax.broadcasted_iota(I32, logits.shape, 1)
    colf = col.astype(F32)
    neg = jnp.float32(-jnp.inf)
    big = jnp.float32(1 << 20)
    is_c = jnp.logical_and(col >= COARSE0, col < COARSE0 + N_EGROUPS)
    lc = jnp.where(is_c, logits, neg)
    cmax = jnp.max(lc, axis=-1, keepdims=True)
    gi = jnp.min(jnp.where(lc == cmax, colf - COARSE0, big), axis=-1, keepdims=True)
    pg = 1.0 / jnp.sum(jnp.where(is_c, jnp.exp(lc - cmax), 0.0), axis=-1, keepdims=True)
    grp = (col >> 3).astype(F32)
    in_g = jnp.logical_and(col < N_EXPERTS, grp == gi)
    lf = jnp.where(in_g, logits, neg)
    m1 = jnp.max(lf, axis=-1, keepdims=True)
    i1 = jnp.min(jnp.where(lf == m1, colf, big), axis=-1, keepdims=True)
    lf2 = jnp.where(colf == i1, neg, lf)
    m2 = jnp.max(lf2, axis=-1, keepdims=True)
    i2 = jnp.min(jnp.where(lf2 == m2, colf, big), axis=-1, keepdims=True)
    e2 = jnp.exp(m2 - m1)
    w1 = pg / (1.0 + e2)
    w2 = pg * e2 / (1.0 + e2)
    n = logits.shape[0]
    hit1 = colf == i1
    hit2 = colf == i2
    onehot = jnp.where(jnp.logical_or(hit1, hit2), 1.0, 0.0)
    rr = lax.broadcasted_iota(I32, (n, n), 0)
    cc = lax.broadcasted_iota(I32, (n, n), 1)
    tri = jnp.where(cc < rr, 1.0, 0.0).astype(BF16)
    pos = jnp.dot(tri, onehot.astype(BF16), preferred_element_type=F32) + cnt
    rank1 = jnp.sum(jnp.where(hit1, pos, 0.0), axis=-1, keepdims=True)
    rank2 = jnp.sum(jnp.where(hit2, pos, 0.0), axis=-1, keepdims=True)
    vals = (i1, i2, w1, w2, rank1, rank2)
    rec = jnp.zeros(logits.shape, F32)
    for c, val in enumerate(vals):
        rec = jnp.where(col == c, val, rec)
    return rec, cnt + jnp.sum(onehot, axis=0, keepdims=True)


def _out_proj_body(n_p_tiles, merged_ref, xp_ref, xsm_ref, g2_ref, wr_ref, wo_hbm,
                   h2_out, v_out, rec_out, cnt_out, wo, stage, sem, h2_scr):
    i = pl.program_id(0)
    n = pl.num_programs(0) - 1

    @pl.when(i == 0)
    def _():
        _load_weight_bf16(wo_hbm, wo, stage, sem)
        cnt_out[...] = jnp.zeros(cnt_out.shape, F32)

    def route_prev():
        v = _rmsnorm(h2_scr[...], g2_ref[...])
        v_out[...] = _pack_halves(v)
        v_hi, v_lo = _split_bf16(v)
        both = jnp.dot(v_hi, wr_ref[...], preferred_element_type=F32)
        logits = (both[:, :ROUTE_W] + both[:, ROUTE_W:]
                  + jnp.dot(v_lo, wr_ref[:, :ROUTE_W], preferred_element_type=F32))
        rec, cnt = _route(logits, cnt_out[...])
        rec_out[...] = rec
        cnt_out[...] = cnt

    def project():
        x = jnp.where(i < n_p_tiles, xp_ref[...], xsm_ref[...])
        h2 = x + jnp.dot(merged_ref[...], wo[...], preferred_element_type=F32)
        h2_out[...] = h2
        return h2

    @pl.when(i == 0)
    def _():
        h2_scr[...] = project()

    @pl.when(jnp.logical_and(i > 0, i < n))
    def _():
        route_prev()
        h2_scr[...] = project()

    @pl.when(i == n)
    def _():
        route_prev()


def _out_proj(merged, xp, xsm, norm2, w_router, w_o):
    t_p, t_s = xp.shape[0], xsm.shape[0]
    n_p, n_s = t_p // TM, t_s // TM
    n = n_p + n_s
    t = t_p + t_s
    xp_spec, xs_spec = _two_stream_specs(n_p, n_s)
    cur = lambda w: pl.BlockSpec((TM, w), lambda i: (jnp.minimum(i, n - 1), 0))
    prev = lambda w: pl.BlockSpec((TM, w), lambda i: (jnp.maximum(i - 1, 0), 0))
    const = lambda r, w: pl.BlockSpec((r, w), lambda i: (0, 0))
    return pl.pallas_call(
        functools.partial(_out_proj_body, n_p),
        grid=(n + 1,),
        in_specs=[cur(D_MODEL), xp_spec, xs_spec, const(1, D_MODEL), const(D_MODEL, 2 * ROUTE_W),
                  pl.BlockSpec(memory_space=pl.ANY)],
        out_specs=[cur(D_MODEL), prev(HALF), prev(ROUTE_W), const(1, ROUTE_W)],
        out_shape=[jax.ShapeDtypeStruct((t, D_MODEL), F32),
                   jax.ShapeDtypeStruct((t, HALF), U32),
                   jax.ShapeDtypeStruct((t, ROUTE_W), F32),
                   jax.ShapeDtypeStruct((1, ROUTE_W), F32)],
        scratch_shapes=[pltpu.VMEM((D_MODEL, D_MODEL), BF16),
                        pltpu.VMEM((2, 256, D_MODEL), F32),
                        pltpu.SemaphoreType.DMA((2,)),
                        pltpu.VMEM((TM, D_MODEL), F32)],
        compiler_params=pltpu.CompilerParams(dimension_semantics=("arbitrary",), vmem_limit_bytes=VMEM_LIMIT),
        name="out_proj_route",
    )(merged, xp, xsm, norm2, w_router, w_o)


def _moe_plan(rec, cnt, n_tiles):
    t = rec.shape[0]
    n_pairs = 2 * t
    eid = rec[:, 0:2].astype(I32).reshape(-1)
    rank = rec[:, 4:6].astype(I32).reshape(-1)
    counts = cnt[0, :N_EXPERTS].astype(I32)
    pair_start = jnp.cumsum(counts) - counts
    experts = jnp.arange(N_EXPERTS, dtype=I32)
    onehot = (eid[:, None] == experts[None, :]).astype(I32)
    pos = rank + jnp.sum(onehot * pair_start[None, :], axis=1)
    _, order = lax.sort((pos, jnp.arange(n_pairs, dtype=I32)), num_keys=1)
    tiles_e = (counts + TM_MOE - 1) // TM_MOE
    tile_end = jnp.cumsum(tiles_e)
    tile_start = tile_end - tiles_e
    n_valid = tile_end[-1]
    tile_ids = jnp.arange(n_tiles, dtype=I32)
    tile_e = jnp.sum((tile_ids[:, None] >= tile_end[None, :]).astype(I32), axis=1)
    last_e = jnp.sum((n_valid - 1 >= tile_end).astype(I32))
    tile_e = jnp.minimum(jnp.where(tile_ids < n_valid, tile_e, last_e), N_EXPERTS - 1)
    tile_onehot = (tile_e[:, None] == experts[None, :]).astype(I32)
    tile_q0 = (tile_ids - jnp.sum(tile_onehot * tile_start[None, :], axis=1)) * TM_MOE
    tile_q0 = jnp.where(tile_ids < n_valid, tile_q0, 0)
    dst = rank + jnp.sum(onehot * (tile_start * TM_MOE)[None, :], axis=1)
    pad_start = tile_start * TM_MOE + counts
    pad_len = tiles_e * TM_MOE - counts
    return (dst.astype(I32), pad_start.astype(I32), pad_len.astype(I32), tile_e, tile_q0.astype(I32),
            n_valid.astype(I32).reshape(1), pair_start, counts, order)


DISPATCH_BUFS = 3
ROW_PIECES = tuple(TM_MOE >> (b + 1) for b in range(TM_MOE.bit_length() - 1))


def _dispatch_body(dst_ref, pad_start_ref, pad_len_ref, nvalid_ref, v_hbm, x_hbm, buf, zbuf, rsem, ssem, zsem):
    i = pl.program_id(0)
    n = pl.num_programs(0)

    def read(tile, slot):
        return pltpu.make_async_copy(v_hbm.at[pl.ds(pl.multiple_of(tile * TM, TM), TM)], buf.at[slot],
                                     rsem.at[slot])

    def row_write(slot, r, dst_row):
        return pltpu.make_async_copy(buf.at[slot, pl.ds(r, 1)], x_hbm.at[pl.ds(dst_row, 1)], ssem.at[slot])

    def drain(slot):
        for _ in range(2):
            pltpu.make_async_copy(buf.at[slot], x_hbm.at[pl.ds(0, TM)], ssem.at[slot]).wait()

    def pad_fill(go):
        def zero_rows(start, size):
            d = pltpu.make_async_copy(zbuf.at[pl.ds(0, size)], x_hbm.at[pl.ds(start, size)], zsem)
            d.start() if go else d.wait()

        def body(e, c):
            start, length = pad_start_ref[e], pad_len_ref[e]
            head = (-start) & (SUBLANES - 1)
            for h in range(SUBLANES - 1):
                @pl.when(h < head)
                def _(h=h):
                    zero_rows(start + h, 1)
            start, length = start + head, length - head
            for size in ROW_PIECES:
                if size >= SUBLANES:
                    @pl.when((length & size) != 0)
                    def _(size=size):
                        zero_rows(pl.multiple_of(start + (length & (-2 * size)), SUBLANES), size)
            return c
        lax.fori_loop(0, N_EXPERTS, body, 0)

        def unused(tile, c):
            for half in range(TM_MOE // ROW_PIECES[0]):
                zero_rows(pl.multiple_of(tile * TM_MOE + half * ROW_PIECES[0], SUBLANES), ROW_PIECES[0])
            return c
        lax.fori_loop(nvalid_ref[0], x_hbm.shape[0] // TM_MOE, unused, 0)

    @pl.when(i == 0)
    def _():
        zbuf[...] = jnp.zeros(zbuf.shape, U32)
        pad_fill(True)
        read(0, 0).start()

    @pl.when(i >= 2)
    def _():
        drain((i + 1) % DISPATCH_BUFS)

    @pl.when(i + 1 < n)
    def _():
        read(i + 1, (i + 1) % DISPATCH_BUFS).start()

    slot = i % DISPATCH_BUFS
    read(i, slot).wait()
    for r in range(TM):
        for k in range(2):
            row_write(slot, r, dst_ref[2 * (i * TM + r) + k]).start(priority=k)

    @pl.when(i == n - 1)
    def _():
        if n >= 2:
            drain((i - 1) % DISPATCH_BUFS)
        drain(slot)
        pad_fill(False)


def _dispatch(v, plan, n_tiles):
    dst, pad_start, pad_len, n_valid = plan[0], plan[1], plan[2], plan[5]
    t = v.shape[0]
    return pl.pallas_call(
        _dispatch_body,
        grid_spec=pltpu.PrefetchScalarGridSpec(
            num_scalar_prefetch=4,
            grid=(t // TM,),
            in_specs=[pl.BlockSpec(memory_space=pl.ANY)],
            out_specs=pl.BlockSpec(memory_space=pl.ANY),
            scratch_shapes=[pltpu.VMEM((DISPATCH_BUFS, TM, HALF), U32),
                            pltpu.VMEM((ROW_PIECES[0], HALF), U32),
                            pltpu.SemaphoreType.DMA((DISPATCH_BUFS,)),
                            pltpu.SemaphoreType.DMA((DISPATCH_BUFS,)),
                            pltpu.SemaphoreType.DMA(())]),
        out_shape=jax.ShapeDtypeStruct((n_tiles * TM_MOE, HALF), U32),
        compiler_params=pltpu.CompilerParams(dimension_semantics=("arbitrary",), vmem_limit_bytes=VMEM_LIMIT),
        name="moe_dispatch",
    )(dst, pad_start, pad_len, n_valid, v)


WEIGHT_SLOTS = 3


def _moe_body(tile_e_ref, tile_q0_ref, nvalid_ref, pstart_ref, cnt_ref, orow_ref,
              krank_ref, elist_ref, nexp_ref,
              x_ref, wg_hbm, wu_hbm, wd_hbm, o_hbm, ybuf0, ybuf1, ybuf2, ssem,
              stage_g, stage_u, stage_d, wsem, wg, wu, wd):
    i = pl.program_id(0)
    nv = nvalid_ref[0]
    bufs = (ybuf0, ybuf1, ybuf2)

    def row_write(slot, r, dst_row):
        return pltpu.make_async_copy(bufs[slot].at[pl.ds(r, 1)], o_hbm.at[pl.ds(dst_row, 1)], ssem.at[slot])

    def start_writes(tile, slot, rows):
        e = tile_e_ref[tile]
        valid = cnt_ref[e] - tile_q0_ref[tile]
        first = pstart_ref[e] + tile_q0_ref[tile]
        for r in rows:
            @pl.when(r < valid)
            def _(r=r):
                row_write(slot, r, orow_ref[first + r]).start(priority=r % 2)

    def compute(slot, writes=None):
        quarter = TM_MOE // 4
        batch = lambda q: start_writes(*writes, range(q * quarter, (q + 1) * quarter)) if writes else None
        x_lo, x_hi = (h.astype(BF16) for h in _unpack_halves(x_ref[...]))
        batch(0)
        hg = (jnp.dot(x_lo, wg[0:HALF, :], preferred_element_type=F32)
              + jnp.dot(x_hi, wg[HALF:D_MODEL, :], preferred_element_type=F32))
        batch(1)
        hu = (jnp.dot(x_lo, wu[0:HALF, :], preferred_element_type=F32)
              + jnp.dot(x_hi, wu[HALF:D_MODEL, :], preferred_element_type=F32))
        batch(2)
        act = hg * _sigmoid(hg) * hu
        y = jnp.dot(act.astype(BF16), wd[...], preferred_element_type=F32)
        batch(3)
        bufs[slot][...] = _pack_halves(y)

    @pl.when(jnp.logical_and(i >= 3, i < nv + 3))
    def _():
        tile = i - 3
        valid = jnp.minimum(cnt_ref[tile_e_ref[tile]] - tile_q0_ref[tile], TM_MOE)

        def wait_rows(size):
            pltpu.make_async_copy(ybuf0.at[pl.ds(0, size)], o_hbm.at[pl.ds(0, size)], ssem.at[i % 3]).wait()

        @pl.when(valid == TM_MOE)
        def _():
            wait_rows(TM_MOE)

        for size in ROW_PIECES:
            @pl.when(jnp.logical_and(valid < TM_MOE, (valid & size) != 0))
            def _(size=size):
                wait_rows(size)

    def weight_copies(k, go):
        e = elist_ref[k]
        slot = k % WEIGHT_SLOTS
        for w_hbm, st in ((wg_hbm, stage_g), (wu_hbm, stage_u), (wd_hbm, stage_d)):
            d = pltpu.make_async_copy(w_hbm.at[e], st.at[slot], wsem.at[slot])
            d.start() if go else d.wait()

    def take_weights(e):
        k = krank_ref[e]
        slot = k % WEIGHT_SLOTS
        weight_copies(k, False)
        wg[...] = stage_g[slot].astype(BF16)
        wu[...] = stage_u[slot].astype(BF16)
        wd[...] = stage_d[slot].astype(BF16)

        @pl.when(k + WEIGHT_SLOTS < nexp_ref[0])
        def _():
            weight_copies(k + WEIGHT_SLOTS, True)

    @pl.when(i == 0)
    def _():
        for k in range(WEIGHT_SLOTS):
            @pl.when(k < nexp_ref[0])
            def _(k=k):
                weight_copies(k, True)
        take_weights(tile_e_ref[0])
        compute(0)

    for slot in range(3):
        prev = (slot + 2) % 3

        @pl.when(jnp.logical_and(i % 3 == slot, jnp.logical_and(i >= 1, i <= nv)))
        def _(slot=slot, prev=prev):
            compute(slot, writes=(i - 1, prev))

    @pl.when(i + 1 < nv)
    def _():
        e_next = tile_e_ref[i + 1]

        @pl.when(e_next != tile_e_ref[i])
        def _():
            take_weights(e_next)


def _moe(x_disp, plan, w_gate, w_up, w_down, n_tiles, n_tok):
    tile_e, tile_q0, n_valid, pair_start, counts, order = plan[3:9]
    orow = jnp.pad((order & 1) * n_tok + (order >> 1), (0, TM_MOE))
    present = (counts > 0).astype(I32)
    krank = jnp.cumsum(present) - present
    experts = jnp.arange(N_EXPERTS, dtype=I32)
    elist = jnp.sum(jnp.where((krank[None, :] == experts[:, None]) & (present[None, :] > 0), experts[None, :], 0),
                    axis=1).astype(I32)
    nexp = jnp.sum(present).astype(I32).reshape(1)
    tile = lambda i, nv: jnp.minimum(i, jnp.maximum(nv[0] - 1, 0))
    any_spec = pl.BlockSpec(memory_space=pl.ANY)
    ybuf = pltpu.VMEM((TM_MOE, HALF), U32)
    return pl.pallas_call(
        _moe_body,
        grid_spec=pltpu.PrefetchScalarGridSpec(
            num_scalar_prefetch=9,
            grid=(n_tiles + 3,),
            in_specs=[pl.BlockSpec((TM_MOE, HALF), lambda i, te, tq, nv, *_: (tile(i, nv), 0)),
                      any_spec, any_spec, any_spec],
            out_specs=pl.BlockSpec(memory_space=pl.ANY),
            scratch_shapes=[ybuf, ybuf, ybuf,
                            pltpu.SemaphoreType.DMA((3,)),
                            pltpu.VMEM((WEIGHT_SLOTS, D_MODEL, D_EXPERT), F32),
                            pltpu.VMEM((WEIGHT_SLOTS, D_MODEL, D_EXPERT), F32),
                            pltpu.VMEM((WEIGHT_SLOTS, D_EXPERT, D_MODEL), F32),
                            pltpu.SemaphoreType.DMA((WEIGHT_SLOTS,)),
                            pltpu.VMEM((D_MODEL, D_EXPERT), BF16),
                            pltpu.VMEM((D_MODEL, D_EXPERT), BF16),
                            pltpu.VMEM((D_EXPERT, D_MODEL), BF16)]),
        out_shape=jax.ShapeDtypeStruct((2 * n_tok, HALF), U32),
        compiler_params=pltpu.CompilerParams(dimension_semantics=("arbitrary",), vmem_limit_bytes=VMEM_LIMIT),
        name="moe_experts",
    )(tile_e, tile_q0, n_valid, pair_start, counts, orow, krank.astype(I32), elist, nexp,
      x_disp, w_gate, w_up, w_down)


def _combine_body(n_p_tiles, h2_ref, rec_ref, gf_ref, y1_ref, y2_ref, outp_ref, outs_ref):
    i = pl.program_id(0)
    rec = rec_ref[...]
    y1 = jnp.concatenate(_unpack_halves(y1_ref[...]), axis=-1)
    y2 = jnp.concatenate(_unpack_halves(y2_ref[...]), axis=-1)
    h = rec[:, 2:3] * y1 + rec[:, 3:4] * y2
    out = _rmsnorm(h2_ref[...] + h, gf_ref[...])

    @pl.when(i < n_p_tiles)
    def _():
        outp_ref[...] = out

    @pl.when(i >= n_p_tiles)
    def _():
        outs_ref[...] = out


def _combine(h2, rec, y_pairs, final_norm, t_p, t_s):
    n_p, n_s = t_p // TM, t_s // TM
    plane_tiles = y_pairs.shape[0] // 2 // TM
    return pl.pallas_call(
        functools.partial(_combine_body, n_p),
        grid=(n_p + n_s,),
        in_specs=[pl.BlockSpec((TM, D_MODEL), lambda i: (i, 0)),
                  pl.BlockSpec((TM, ROUTE_W), lambda i: (i, 0)),
                  pl.BlockSpec((1, D_MODEL), lambda i: (0, 0)),
                  pl.BlockSpec((TM, HALF), lambda i: (i, 0)),
                  pl.BlockSpec((TM, HALF), lambda i: (i + plane_tiles, 0))],
        out_specs=[pl.BlockSpec((TM, D_MODEL), lambda i: (jnp.minimum(i, n_p - 1), 0)),
                   pl.BlockSpec((TM, D_MODEL), lambda i: (jnp.clip(i - n_p, 0, n_s - 1), 0))],
        out_shape=[jax.ShapeDtypeStruct((t_p, D_MODEL), F32),
                   jax.ShapeDtypeStruct((t_s, D_MODEL), F32)],
        compiler_params=pltpu.CompilerParams(dimension_semantics=("arbitrary",), vmem_limit_bytes=VMEM_LIMIT),
        name="combine_norm",
    )(h2, rec, final_norm, y_pairs, y_pairs)


def kernel(x_prompt, x_sample, state_conv, state_ssm_re, state_ssm_im, meta_tokens, norm1, w_in, conv_w,
           lam_re, lam_im, log_dt, ssm_b_re, ssm_b_im, ssm_c_re, ssm_c_im, ssm_d, w_glu, w_conv_out,
           w_ssm_out, w_o, norm2, w_coarse, w_fine, w_gate, w_up, w_down, final_norm):
    n_pb, seq, _ = x_prompt.shape
    n_sb, dec_seq, _ = x_sample.shape
    assert dec_seq == CHUNK and seq % TM == 0 and (n_sb * dec_seq) % TM == 0 and N_META == CHUNK
    t_p, t_s = n_pb * seq, n_sb * dec_seq
    xp = x_prompt.reshape(t_p, D_MODEL)
    xsm = x_sample.reshape(t_s, D_MODEL)

    xb, z, xs, zmeta, xsmeta, sga, sgb = _in_proj(xp, xsm, meta_tokens, norm1, w_in[0])

    mats = _s5_chunk_mats(lam_re[0], lam_im[0], log_dt[0], ssm_b_re[0], ssm_b_im[0], ssm_c_re[0], ssm_c_im[0])
    y5, pf_re, pf_im, sf_re, sf_im = _s5(xs, xsmeta, state_ssm_re[0], state_ssm_im[0], mats, n_pb, n_sb, seq)

    buf = state_conv[0]
    zero = jnp.zeros((n_sb, dec_seq, D_CONV), F32)
    inj1 = zero.at[:, 0].set(buf[:, 1]).reshape(t_s, D_CONV)
    inj2 = zero.at[:, 0].set(buf[:, 0]).at[:, 1].set(buf[:, 1]).reshape(t_s, D_CONV)
    merged = _mixers(xb, z, y5, xs, sga, sgb, zmeta, inj1, inj2, conv_w[0], ssm_d, w_conv_out[0], w_glu[0],
                     w_ssm_out[0], t_p // TM, seq // TM)

    w_router = jnp.concatenate(
        [w_fine[0], w_coarse[0], jnp.zeros((D_MODEL, ROUTE_W - N_EXPERTS - N_EGROUPS), F32)], axis=1)
    w_router = jnp.concatenate(_split_bf16(w_router), axis=1)
    h2, v, rec, cnt = _out_proj(merged, xp, xsm, norm2, w_router, w_o[0])

    n_tiles = 2 * (t_p + t_s) // TM_MOE + N_EXPERTS
    plan = _moe_plan(rec, cnt, n_tiles)
    x_disp = _dispatch(v, plan, n_tiles)
    y_pairs = _moe(x_disp, plan, w_gate[0], w_up[0], w_down[0], n_tiles, t_p + t_s)
    y_p, y_s = _combine(h2, rec, y_pairs, final_norm.reshape(1, D_MODEL), t_p, t_s)

    new_conv_p = jnp.stack([z[(b + 1) * seq - 2:(b + 1) * seq] for b in range(n_pb)])
    new_conv_s = z[t_p:].reshape(n_sb, dec_seq, D_CONV)[:, dec_seq - 2:]
    return (y_p.reshape(n_pb, seq, D_MODEL), y_s.reshape(n_sb, dec_seq, D_MODEL),
            new_conv_p[None], pf_re, pf_im, new_conv_s[None], sf_re, sf_im)
```

```python
import functools

import jax
import jax.numpy as jnp
from jax import lax
from jax.experimental import pallas as pl
from jax.experimental.pallas import tpu as pltpu

F32 = jnp.float32
BF16 = jnp.bfloat16
I32 = jnp.int32
U32 = jnp.uint32

D_MODEL = 2048
D_CONV = 1024
D_SSM = 1024
SSM_H = 16
SSM_G = 64
SSM_P = 64
N_META = 16
N_EGROUPS = 4
EXPERTS_PER_GROUP = 8
N_EXPERTS = 32
D_EXPERT = 256
EPS = 1e-6

LANES = 128
SUBLANES = 8

CHUNK = 16
CHUNK_W = CHUNK * SSM_H

TM = 256
TM_MOE = 256
VMEM_LIMIT = 52 * 1024 * 1024


def _rmsnorm(x, g):
    return x * lax.rsqrt(jnp.mean(x * x, axis=-1, keepdims=True) + EPS) * g


def _sigmoid(x):
    return 1.0 / (1.0 + jnp.exp(-x))


def _gelu_tanh(x):
    c = 0.7978845608028654
    return 0.5 * x * (1.0 + jnp.tanh(c * (x + 0.044715 * (x * x * x))))


def _split_bf16(a):
    hi = a.astype(BF16)
    lo = (a - hi.astype(F32)).astype(BF16)
    return hi, lo


def _dot3(a, b):
    a_hi, a_lo = _split_bf16(a)
    b_hi, b_lo = _split_bf16(b)
    return (jnp.dot(a_hi, b_hi, preferred_element_type=F32) + jnp.dot(a_lo, b_hi, preferred_element_type=F32)
            + jnp.dot(a_hi, b_lo, preferred_element_type=F32))


HALF = D_MODEL // 2


def _pack_halves(a):
    return pltpu.pack_elementwise([a[:, :HALF], a[:, HALF:]], packed_dtype=BF16)


def _unpack_halves(p):
    return (pltpu.unpack_elementwise(p, index=0, packed_dtype=BF16, unpacked_dtype=F32),
            pltpu.unpack_elementwise(p, index=1, packed_dtype=BF16, unpacked_dtype=F32))


def _weight_copy(w_hbm, stage, sem, c, slot, rows, col0, ncols):
    return pltpu.make_async_copy(
        w_hbm.at[pl.ds(c * rows, rows), pl.ds(col0, ncols)], stage.at[slot], sem.at[slot])


def _load_weight_bf16(w_hbm, w_vmem, stage, sem, col0=0):
    k, n = w_vmem.shape
    rows = stage.shape[1]
    nchunk = k // rows
    _weight_copy(w_hbm, stage, sem, 0, 0, rows, col0, n).start()
    for c in range(nchunk):
        slot = c % 2
        if c + 1 < nchunk:
            _weight_copy(w_hbm, stage, sem, c + 1, 1 - slot, rows, col0, n).start()
        _weight_copy(w_hbm, stage, sem, c, slot, rows, col0, n).wait()
        w_vmem[pl.ds(c * rows, rows), :] = stage[slot].astype(BF16)


WEIGHT_CHUNK = 512


def _stream_weight_bf16(w_hbm, w_vmem, stage, sem, col0, on_chunk):
    _, n = w_vmem.shape
    w = stage.shape[2]
    nchunk = n // w

    def copy(c, slot):
        return pltpu.make_async_copy(w_hbm.at[:, pl.ds(col0 + c * w, w)], stage.at[slot], sem.at[slot])

    copy(0, 0).start()
    for c in range(nchunk):
        slot = c % 2
        if c + 1 < nchunk:
            copy(c + 1, 1 - slot).start()
        copy(c, slot).wait()
        w_vmem[:, c * w:(c + 1) * w] = stage[slot].astype(BF16)
        on_chunk(c)


def _in_proj_mix_body(n_p_tiles, xp_ref, xsm_ref, meta_ref, g_ref, w_hbm,
                      xb_out, z_out, xs_out, zmeta_out, xsmeta_out,
                      w_vmem, stage, sem):
    i = pl.program_id(0)
    g = g_ref[...]
    cw = stage.shape[2]

    @pl.when(i == 0)
    def _():
        u = _rmsnorm(xp_ref[...], g).astype(BF16)
        um = _rmsnorm(meta_ref[...], g).astype(BF16)

        def on_chunk(c):
            n0 = c * cw
            cols = slice(n0 % D_CONV, n0 % D_CONV + cw)
            p = jnp.dot(u, w_vmem[:, n0:n0 + cw], preferred_element_type=F32)
            pm = jnp.dot(um, w_vmem[:, n0:n0 + cw], preferred_element_type=F32)
            if n0 < D_CONV:
                xb_out[:, cols] = p.astype(BF16)
            elif n0 < 2 * D_CONV:
                z_out[:, cols] = p
                zmeta_out[:, cols] = pm
            elif n0 < 3 * D_CONV:
                z_out[:, cols] = z_out[:, cols] * p
                zmeta_out[:, cols] = zmeta_out[:, cols] * pm
            else:
                xs_out[:, cols] = p
                xsmeta_out[:, cols] = pm

        _stream_weight_bf16(w_hbm, w_vmem, stage, sem, 0, on_chunk)

    @pl.when(i > 0)
    def _():
        x = jnp.where(i < n_p_tiles, xp_ref[...], xsm_ref[...])
        u = _rmsnorm(x, g).astype(BF16)
        xb = jnp.dot(u, w_vmem[:, 0:D_CONV], preferred_element_type=F32)
        xc = jnp.dot(u, w_vmem[:, D_CONV:2 * D_CONV], preferred_element_type=F32)
        xv = jnp.dot(u, w_vmem[:, 2 * D_CONV:3 * D_CONV], preferred_element_type=F32)
        xs = jnp.dot(u, w_vmem[:, 3 * D_CONV:3 * D_CONV + D_SSM], preferred_element_type=F32)
        xb_out[...] = xb.astype(BF16)
        z_out[...] = xc * xv
        xs_out[...] = xs


def _in_proj_gate_body(n_p_tiles, xp_ref, xsm_ref, g_ref, w_hbm, ga_out, gb_out,
                       w_vmem, stage, sem):
    i = pl.program_id(0)
    cw = stage.shape[2]

    @pl.when(i == 0)
    def _():
        u = _rmsnorm(xp_ref[...], g_ref[...]).astype(BF16)

        def on_chunk(c):
            n0 = c * cw
            out = ga_out if n0 < D_MODEL else gb_out
            gate = _sigmoid(jnp.dot(u, w_vmem[:, n0:n0 + cw], preferred_element_type=F32))
            out[:, n0 % D_MODEL:n0 % D_MODEL + cw] = gate.astype(BF16)

        _stream_weight_bf16(w_hbm, w_vmem, stage, sem, 3 * D_CONV + D_SSM, on_chunk)

    @pl.when(i > 0)
    def _():
        x = jnp.where(i < n_p_tiles, xp_ref[...], xsm_ref[...])
        u = _rmsnorm(x, g_ref[...]).astype(BF16)
        gates = jnp.dot(u, w_vmem[...], preferred_element_type=F32)
        ga_out[...] = _sigmoid(gates[:, 0:D_MODEL]).astype(BF16)
        gb_out[...] = _sigmoid(gates[:, D_MODEL:2 * D_MODEL]).astype(BF16)


def _two_stream_specs(n_p_tiles, n_s_tiles, tm=TM):
    xp_spec = pl.BlockSpec((tm, D_MODEL), lambda i: (jnp.minimum(i, n_p_tiles - 1), 0))
    mode = pl.Buffered(1) if n_s_tiles == 1 else None
    xs_spec = pl.BlockSpec((tm, D_MODEL), lambda i: (jnp.clip(i - n_p_tiles, 0, n_s_tiles - 1), 0),
                           pipeline_mode=mode)
    return xp_spec, xs_spec


TM_IN = TM


def _in_proj(xp, xsm, meta, norm1, w_in):
    t_p, t_s = xp.shape[0], xsm.shape[0]
    n_p, n_s = t_p // TM_IN, t_s // TM_IN
    t = t_p + t_s
    half = 3 * D_CONV + D_SSM
    xp_spec, xs_spec = _two_stream_specs(n_p, n_s, TM_IN)
    g_spec = pl.BlockSpec((1, D_MODEL), lambda i: (0, 0))
    any_spec = pl.BlockSpec(memory_space=pl.ANY)
    row = lambda w: pl.BlockSpec((TM_IN, w), lambda i: (i, 0))
    const = lambda r, w: pl.BlockSpec((r, w), lambda i: (0, 0))
    params = pltpu.CompilerParams(dimension_semantics=("arbitrary",), vmem_limit_bytes=VMEM_LIMIT)
    scratch = [pltpu.VMEM((D_MODEL, half), BF16),
               pltpu.VMEM((2, D_MODEL, WEIGHT_CHUNK), F32),
               pltpu.SemaphoreType.DMA((2,))]

    xb, z, xs, zmeta, xsmeta = pl.pallas_call(
        functools.partial(_in_proj_mix_body, n_p),
        grid=(n_p + n_s,),
        in_specs=[xp_spec, xs_spec, const(N_META, D_MODEL), g_spec, any_spec],
        out_specs=[row(D_CONV), row(D_CONV), row(D_SSM), const(N_META, D_CONV), const(N_META, D_SSM)],
        out_shape=[jax.ShapeDtypeStruct((t, D_CONV), BF16),
                   jax.ShapeDtypeStruct((t, D_CONV), F32),
                   jax.ShapeDtypeStruct((t, D_SSM), F32),
                   jax.ShapeDtypeStruct((N_META, D_CONV), F32),
                   jax.ShapeDtypeStruct((N_META, D_SSM), F32)],
        scratch_shapes=scratch,
        compiler_params=params,
        name="in_proj_mix",
    )(xp, xsm, meta, norm1, w_in)

    sga, sgb = pl.pallas_call(
        functools.partial(_in_proj_gate_body, n_p),
        grid=(n_p + n_s,),
        in_specs=[xp_spec, xs_spec, g_spec, any_spec],
        out_specs=[row(D_MODEL), row(D_MODEL)],
        out_shape=[jax.ShapeDtypeStruct((t, D_MODEL), BF16),
                   jax.ShapeDtypeStruct((t, D_MODEL), BF16)],
        scratch_shapes=scratch,
        compiler_params=params,
        name="in_proj_gate",
    )(xp, xsm, norm1, w_in)
    return xb, z, xs, zmeta, xsmeta, sga, sgb


S5_GROUPS_PER_STEP = LANES // SSM_H
S5_PAIRS_PER_STEP = S5_GROUPS_PER_STEP // 2


def _s5_chunk_mats(lam_re, lam_im, log_dt, b_re, b_im, c_re, c_im):
    dt = jnp.exp(log_dt)[:, None]
    lr, li = lam_re, lam_im
    z = jnp.stack([lr * dt, li * dt], axis=1)
    mag = jnp.exp(lr * dt)
    ab_re, ab_im = mag * jnp.cos(li * dt), mag * jnp.sin(li * dt)
    nr, ni = ab_re - 1.0, ab_im
    den = lr * lr + li * li
    k_re = (nr * lr + ni * li) / den
    k_im = (ni * lr - nr * li) / den
    bb = jnp.stack([k_re[..., None] * b_re - k_im[..., None] * b_im,
                    k_re[..., None] * b_im + k_im[..., None] * b_re], axis=1)
    return z, bb, jnp.stack([c_re, c_im], axis=1)


def _cmul(ar, ai, br, bi):
    return ar * br - ai * bi, ar * bi + ai * br


def _s5_operators(pair, parity, z_ref, bb_ref, c_ref, toe_scr, p_scr, q_scr, a16_scr, g_scr, m_scr):
    k = 2 * pair + parity
    zr, zi = z_ref[k, 0:1, :], z_ref[k, 1:2, :]
    mag = jnp.exp(zr)
    ar, ai = mag * jnp.cos(zi), mag * jnp.sin(zi)
    eye = lax.broadcasted_iota(I32, (SSM_P, SSM_P), 0) == lax.broadcasted_iota(I32, (SSM_P, SSM_P), 1)
    acr = jnp.sum(jnp.where(eye, ar, 0.0), axis=1, keepdims=True)
    aci = jnp.sum(jnp.where(eye, ai, 0.0), axis=1, keepdims=True)
    c_re, c_im = c_ref[k, 0], c_ref[k, 1]
    bb_re, bb_im = bb_ref[k, 0], bb_ref[k, 1]

    blk = lax.broadcasted_iota(I32, (1, CHUNK_W), 1) >> 4

    pr, pi = jnp.ones_like(ar), jnp.zeros_like(ar)
    pcr, pci = jnp.ones_like(acr), jnp.zeros_like(acr)
    pw_re = jnp.zeros((SSM_P, CHUNK_W), F32)
    pw_im = jnp.zeros((SSM_P, CHUNK_W), F32)
    for d in range(CHUNK + 1):
        g_scr[0, d * SSM_H:(d + 1) * SSM_H, :] = c_re * pr - c_im * pi
        g_scr[1, d * SSM_H:(d + 1) * SSM_H, :] = c_re * pi + c_im * pr
        if d < CHUNK:
            pw_re = jnp.where(blk == CHUNK - 1 - d, pcr, pw_re)
            pw_im = jnp.where(blk == CHUNK - 1 - d, pci, pw_im)
            pcr, pci = _cmul(pcr, pci, acr, aci)
            pr, pi = _cmul(pr, pi, ar, ai)

    half = parity * SSM_P
    a16_scr[pair, 0, :, half:half + SSM_P] = pr
    a16_scr[pair, 1, :, half:half + SSM_P] = pi
    q_scr[k] = jnp.zeros(q_scr.shape[1:], BF16)
    q_scr[k, 0, :, half:half + SSM_P] = g_scr[0, SSM_H:, :].astype(BF16)
    q_scr[k, 1, :, half:half + SSM_P] = (-g_scr[1, SSM_H:, :]).astype(BF16)

    rep = jnp.where(lax.broadcasted_iota(I32, (SSM_H, CHUNK_W), 0)
                    == (lax.broadcasted_iota(I32, (SSM_H, CHUNK_W), 1) & (SSM_H - 1)), 1.0, 0.0).astype(BF16)

    def widen(b):
        b_hi, b_lo = _split_bf16(b)
        return (jnp.dot(b_hi, rep, preferred_element_type=F32) + jnp.dot(b_lo, rep, preferred_element_type=F32))

    bw_re, bw_im = widen(bb_re), widen(bb_im)
    p_re, p_im = _cmul(pw_re, pw_im, bw_re, bw_im)
    p_scr[k, 0:SSM_P, :] = p_re.astype(BF16)
    p_scr[k, SSM_P:2 * SSM_P, :] = p_im.astype(BF16)

    m_scr[...] = _dot3(g_scr[0, 0:CHUNK_W, :], bw_re) - _dot3(g_scr[1, 0:CHUNK_W, :], bw_im)
    for t in range(CHUNK):
        acc = jnp.where(blk == 0, m_scr[t * SSM_H:(t + 1) * SSM_H, :], 0.0)
        for j in range(1, t + 1):
            acc = jnp.where(blk == j, m_scr[(t - j) * SSM_H:(t - j + 1) * SSM_H, :], acc)
        toe_scr[k, t * SSM_H:(t + 1) * SSM_H, :] = acc.astype(BF16)


def _s5_body(n_pc, n_pb, n_sb, xs_ref, xsmeta_ref, z_ref, bb_ref, c_ref, s0re_ref, s0im_ref,
             y_out, pfre_out, pfim_out, sfre_out, sfim_out,
             u_scr, sl_re, sl_im, sp_re, sp_im, yt_scr, toe_ref, p_ref, q_ref, a16_scr,
             g_scr, m_scr):
    gb = S5_GROUPS_PER_STEP
    n_p_rows = n_pb * n_pc
    row_s = n_p_rows
    row_m = row_s + n_sb
    t_p = n_p_rows * CHUNK
    rows_pad = u_scr.shape[2]

    for k in range(gb):
        _s5_operators(k // 2, k % 2, z_ref, bb_ref, c_ref, toe_ref, p_ref, q_ref, a16_scr, g_scr, m_scr)

    first_row = lax.broadcasted_iota(I32, (SUBLANES, 1), 0) == 0
    tail = jnp.zeros((rows_pad - row_m - SUBLANES, LANES), F32)
    for t in range(CHUNK):
        meta_rows = jnp.where(first_row, xsmeta_ref[t:t + 1, :], 0.0)
        rows_t = jnp.concatenate([xs_ref[pl.ds(t, n_p_rows, stride=CHUNK), :],
                                  xs_ref[pl.ds(t_p + t, n_sb, stride=CHUNK), :], meta_rows, tail], axis=0)
        xt = rows_t.T.astype(BF16)
        for k in range(gb):
            u_scr[k, t * SSM_H:(t + 1) * SSM_H, :] = xt[k * SSM_H:(k + 1) * SSM_H, :]

    npair = S5_PAIRS_PER_STEP
    for j in range(npair):
        sl0 = jnp.dot(p_ref[2 * j], u_scr[2 * j], preferred_element_type=F32)
        sl1 = jnp.dot(p_ref[2 * j + 1], u_scr[2 * j + 1], preferred_element_type=F32)
        sl_re[j] = jnp.concatenate([sl0[0:SSM_P, :], sl1[0:SSM_P, :]], axis=0).T
        sl_im[j] = jnp.concatenate([sl0[SSM_P:2 * SSM_P, :], sl1[SSM_P:2 * SSM_P, :]], axis=0).T
        sp_re[j, row_m:rows_pad, :] = jnp.zeros((rows_pad - row_m, 2 * SSM_P), F32)
        sp_im[j, row_m:rows_pad, :] = jnp.zeros((rows_pad - row_m, 2 * SSM_P), F32)

    ar = [a16_scr[j, 0] for j in range(npair)]
    ai = [a16_scr[j, 1] for j in range(npair)]
    sre = [jnp.broadcast_to(sl_re[j, row_m:row_m + 1, :], (n_pb, 2 * SSM_P)) for j in range(npair)]
    sim = [jnp.broadcast_to(sl_im[j, row_m:row_m + 1, :], (n_pb, 2 * SSM_P)) for j in range(npair)]
    for c in range(n_pc):
        rows = pl.ds(c, n_pb, stride=n_pc)
        for j in range(npair):
            sp_re[j, rows, :] = sre[j]
            sp_im[j, rows, :] = sim[j]
            nre = ar[j] * sre[j] - ai[j] * sim[j] + sl_re[j, rows, :]
            nim = ar[j] * sim[j] + ai[j] * sre[j] + sl_im[j, rows, :]
            sre[j], sim[j] = nre, nim
    for j in range(npair):
        pfre_out[j] = sre[j]
        pfim_out[j] = sim[j]
        s0r, s0i = s0re_ref[j], s0im_ref[j]
        sp_re[j, row_s:row_m, :] = s0r
        sp_im[j, row_s:row_m, :] = s0i
        sfre_out[j] = ar[j] * s0r - ai[j] * s0i + sl_re[j, row_s:row_m, :]
        sfim_out[j] = ar[j] * s0i + ai[j] * s0r + sl_im[j, row_s:row_m, :]

    nt = (((1,), (1,)), ((), ()))
    for k in range(gb):
        y = jnp.dot(toe_ref[k], u_scr[k], preferred_element_type=F32)
        y += lax.dot_general(q_ref[k, 0], sp_re[k // 2].astype(BF16), nt, preferred_element_type=F32)
        y += lax.dot_general(q_ref[k, 1], sp_im[k // 2].astype(BF16), nt, preferred_element_type=F32)
        for t in range(CHUNK):
            yt_scr[t, k * SSM_H:(k + 1) * SSM_H, :] = y[t * SSM_H:(t + 1) * SSM_H, :]
    for t in range(CHUNK):
        yt = yt_scr[t].T
        y_out[pl.ds(t, n_p_rows, stride=CHUNK), :] = yt[0:n_p_rows, :]
        y_out[pl.ds(t_p + t, n_sb, stride=CHUNK), :] = yt[row_s:row_m, :]


def _s5(xs, xsmeta, state_re, state_im, mats, n_pb, n_sb, seq):
    z, bb, c = mats
    t = xs.shape[0]
    n_pc = seq // CHUNK
    rows = n_pc * n_pb + n_sb + 1
    rows_pad = -(-rows // LANES) * LANES
    gb, npair = S5_GROUPS_PER_STEP, S5_PAIRS_PER_STEP
    pairs = lambda s: jnp.transpose(s.reshape(n_sb, SSM_G // 2, 2 * SSM_P), (1, 0, 2))
    blk3 = lambda n, r, c: pl.BlockSpec((n, r, c), lambda i: (i, 0, 0))
    y, pfre, pfim, sfre, sfim = pl.pallas_call(
        functools.partial(_s5_body, n_pc, n_pb, n_sb),
        grid=(SSM_G // gb,),
        in_specs=[pl.BlockSpec((t, LANES), lambda i: (0, i)),
                  pl.BlockSpec((N_META, LANES), lambda i: (0, i)),
                  blk3(gb, 2, SSM_P),
                  pl.BlockSpec((gb, 2, SSM_P, SSM_H), lambda i: (i, 0, 0, 0)),
                  pl.BlockSpec((gb, 2, SSM_H, SSM_P), lambda i: (i, 0, 0, 0)),
                  blk3(npair, n_sb, 2 * SSM_P), blk3(npair, n_sb, 2 * SSM_P)],
        out_specs=[pl.BlockSpec((t, LANES), lambda i: (0, i)),
                   blk3(npair, n_pb, 2 * SSM_P), blk3(npair, n_pb, 2 * SSM_P),
                   blk3(npair, n_sb, 2 * SSM_P), blk3(npair, n_sb, 2 * SSM_P)],
        out_shape=[jax.ShapeDtypeStruct((t, D_SSM), F32),
                   jax.ShapeDtypeStruct((SSM_G // 2, n_pb, 2 * SSM_P), F32),
                   jax.ShapeDtypeStruct((SSM_G // 2, n_pb, 2 * SSM_P), F32),
                   jax.ShapeDtypeStruct((SSM_G // 2, n_sb, 2 * SSM_P), F32),
                   jax.ShapeDtypeStruct((SSM_G // 2, n_sb, 2 * SSM_P), F32)],
        scratch_shapes=[pltpu.VMEM((gb, CHUNK_W, rows_pad), BF16),
                        pltpu.VMEM((npair, rows_pad, 2 * SSM_P), F32),
                        pltpu.VMEM((npair, rows_pad, 2 * SSM_P), F32),
                        pltpu.VMEM((npair, rows_pad, 2 * SSM_P), F32),
                        pltpu.VMEM((npair, rows_pad, 2 * SSM_P), F32),
                        pltpu.VMEM((CHUNK, LANES, rows_pad), F32),
                        pltpu.VMEM((gb, CHUNK_W, CHUNK_W), BF16),
                        pltpu.VMEM((gb, 2 * SSM_P, CHUNK_W), BF16),
                        pltpu.VMEM((gb, 2, CHUNK_W, 2 * SSM_P), BF16),
                        pltpu.VMEM((npair, 2, 1, 2 * SSM_P), F32),
                        pltpu.VMEM((2, (CHUNK + 1) * SSM_H, SSM_P), F32),
                        pltpu.VMEM((CHUNK_W, CHUNK_W), F32)],
        compiler_params=pltpu.CompilerParams(dimension_semantics=("arbitrary",), vmem_limit_bytes=VMEM_LIMIT),
        name="s5_chunks",
    )(xs, xsmeta, z, bb, c, pairs(state_re), pairs(state_im))
    unpair = lambda a: jnp.transpose(a, (1, 0, 2)).reshape(a.shape[1], SSM_G, SSM_P)[None]
    return y, unpair(pfre), unpair(pfim), unpair(sfre), unpair(sfim)


def _mixers_body(n_p_tiles, tiles_per_seq, xb_ref, z_ref, y5_ref, xs_ref, sga_ref, sgb_ref,
                 zmeta_ref, inj1_ref, inj2_ref, cw_ref, dskip_ref,
                 wc_hbm, wg_hbm, wso_hbm, merged_out,
                 wc, wg, wso, stage_a, stage_b, sem, carry):
    i = pl.program_id(0)

    @pl.when(i == 0)
    def _():
        _load_weight_bf16(wc_hbm, wc, stage_a, sem)
        _load_weight_bf16(wg_hbm, wg, stage_b, sem)
        _load_weight_bf16(wso_hbm, wso, stage_a, sem)

    @pl.when(jnp.logical_and(i < n_p_tiles, i % tiles_per_seq == 0))
    def _():
        carry[0:2, :] = zmeta_ref[N_META - 2:N_META, :]

    z = z_ref[...]
    row = lax.broadcasted_iota(I32, (TM, 1), 0)
    is_s = i >= n_p_tiles
    r1 = pltpu.roll(z, 1, 0)
    r2 = pltpu.roll(z, 2, 0)
    c1 = carry[1:2, :]
    c2 = carry[0:1, :]
    pos = jnp.where(is_s, row & (CHUNK - 1), row)
    first1 = pos == 0
    first2 = pos < 2
    fill1 = jnp.where(is_s, inj1_ref[...], jnp.broadcast_to(c1, z.shape))
    fill2 = jnp.where(is_s, inj2_ref[...], jnp.where(row == 0, c2, c1))
    zp1 = jnp.where(first1, fill1, r1)
    zp2 = jnp.where(first2, fill2, r2)
    carry[0:2, :] = z[TM - 2:TM, :]

    cw = cw_ref[...]
    conv = cw[0:1, :] * zp2 + cw[1:2, :] * zp1 + cw[2:3, :] * z
    a_in = (xb_ref[...].astype(F32) * conv).astype(BF16)
    ya = jnp.dot(a_in, wc[...], preferred_element_type=F32)

    ys = y5_ref[...] + dskip_ref[...] * xs_ref[...]
    ys = _gelu_tanh(ys)
    glu = jnp.dot(ys.astype(BF16), wg[...], preferred_element_type=F32)
    ys = ys * _sigmoid(glu)
    yb = jnp.dot(ys.astype(BF16), wso[...], preferred_element_type=F32)

    merged = sga_ref[...].astype(F32) * ya + sgb_ref[...].astype(F32) * yb
    merged_out[...] = merged.astype(BF16)


def _mixers(xb, z, y5, xs, sga, sgb, zmeta, inj1, inj2, conv_w, d_skip, w_conv_out, w_glu, w_ssm_out,
            n_p_tiles, tiles_per_seq):
    t = xb.shape[0]
    n_s_tiles = inj1.shape[0] // TM
    row = lambda w: pl.BlockSpec((TM, w), lambda i: (i, 0))
    const = lambda r, w: pl.BlockSpec((r, w), lambda i: (0, 0))
    inj = pl.BlockSpec((TM, D_CONV), lambda i: (jnp.clip(i - n_p_tiles, 0, n_s_tiles - 1), 0))
    any_spec = pl.BlockSpec(memory_space=pl.ANY)
    return pl.pallas_call(
        functools.partial(_mixers_body, n_p_tiles, tiles_per_seq),
        grid=(t // TM,),
        in_specs=[row(D_CONV), row(D_CONV), row(D_SSM), row(D_SSM), row(D_MODEL), row(D_MODEL),
                  const(N_META, D_CONV), inj, inj, const(3, D_CONV), const(1, D_SSM),
                  any_spec, any_spec, any_spec],
        out_specs=row(D_MODEL),
        out_shape=jax.ShapeDtypeStruct((t, D_MODEL), BF16),
        scratch_shapes=[pltpu.VMEM((D_CONV, D_MODEL), BF16),
                        pltpu.VMEM((D_SSM, D_SSM), BF16),
                        pltpu.VMEM((D_SSM, D_MODEL), BF16),
                        pltpu.VMEM((2, 256, D_MODEL), F32),
                        pltpu.VMEM((2, 256, D_SSM), F32),
                        pltpu.SemaphoreType.DMA((2,)),
                        pltpu.VMEM((8, D_CONV), F32)],
        compiler_params=pltpu.CompilerParams(dimension_semantics=("arbitrary",), vmem_limit_bytes=VMEM_LIMIT),
        name="mixers",
    )(xb, z, y5, xs, sga, sgb, zmeta, inj1, inj2, conv_w, d_skip, w_conv_out, w_glu, w_ssm_out)


ROUTE_W = LANES
COARSE0 = N_EXPERTS


def _route(logits, cnt):
    col = lax.broadcasted_iota(I32, logits.shape, 1)
    colf = col.astype(F32)
    neg = jnp.float32(-jnp.inf)
    big = jnp.float32(1 << 20)
    is_c = jnp.logical_and(col >= COARSE0, col < COARSE0 + N_EGROUPS)
    lc = jnp.where(is_c, logits, neg)
    cmax = jnp.max(lc, axis=-1, keepdims=True)
    gi = jnp.min(jnp.where(lc == cmax, colf - COARSE0, big), axis=-1, keepdims=True)
    pg = 1.0 / jnp.sum(jnp.where(is_c, jnp.exp(lc - cmax), 0.0), axis=-1, keepdims=True)
    grp = (col >> 3).astype(F32)
    in_g = jnp.logical_and(col < N_EXPERTS, grp == gi)
    lf = jnp.where(in_g, logits, neg)
    m1 = jnp.max(lf, axis=-1, keepdims=True)
    i1 = jnp.min(jnp.where(lf == m1, colf, big), axis=-1, keepdims=True)
    lf2 = jnp.where(colf == i1, neg, lf)
    m2 = jnp.max(lf2, axis=-1, keepdims=True)
    i2 = jnp.min(jnp.where(lf2 == m2, colf, big), axis=-1, keepdims=True)
    e2 = jnp.exp(m2 - m1)
    w1 = pg / (1.0 + e2)
    w2 = pg * e2 / (1.0 + e2)
    n = logits.shape[0]
    hit1 = colf == i1
    hit2 = colf == i2
    onehot = jnp.where(jnp.logical_or(hit1, hit2), 1.0, 0.0)
    rr = lax.broadcasted_iota(I32, (n, n), 0)
    cc = lax.broadcasted_iota(I32, (n, n), 1)
    tri = jnp.where(cc < rr, 1.0, 0.0).astype(BF16)
    pos = jnp.dot(tri, onehot.astype(BF16), preferred_element_type=F32) + cnt
    rank1 = jnp.sum(jnp.where(hit1, pos, 0.0), axis=-1, keepdims=True)
    rank2 = jnp.sum(jnp.where(hit2, pos, 0.0), axis=-1, keepdims=True)
    vals = (i1, i2, w1, w2, rank1, rank2)
    rec = jnp.zeros(logits.shape, F32)
    for c, val in enumerate(vals):
        rec = jnp.where(col == c, val, rec)
    return rec, cnt + jnp.sum(onehot, axis=0, keepdims=True)


def _out_proj_body(n_p_tiles, merged_ref, xp_ref, xsm_ref, g2_ref, wr_ref, wo_hbm,
                   h2_out, v_out, rec_out, cnt_out, wo, stage, sem, h2_scr):
    i = pl.program_id(0)
    n = pl.num_programs(0) - 1

    def route_prev():
        v = _rmsnorm(h2_scr[...], g2_ref[...])
        v_out[...] = _pack_halves(v)
        v_hi, v_lo = _split_bf16(v)
        both = jnp.dot(v_hi, wr_ref[...], preferred_element_type=F32)
        logits = (both[:, :ROUTE_W] + both[:, ROUTE_W:]
                  + jnp.dot(v_lo, wr_ref[:, :ROUTE_W], preferred_element_type=F32))
        rec, cnt = _route(logits, cnt_out[...])
        rec_out[...] = rec
        cnt_out[...] = cnt

    def project():
        x = jnp.where(i < n_p_tiles, xp_ref[...], xsm_ref[...])
        h2 = x + jnp.dot(merged_ref[...], wo[...], preferred_element_type=F32)
        h2_out[...] = h2
        return h2

    @pl.when(i == 0)
    def _():
        cnt_out[...] = jnp.zeros(cnt_out.shape, F32)
        cw = stage.shape[2]
        merged = merged_ref[...]

        def on_chunk(c):
            cols = slice(c * cw, (c + 1) * cw)
            h2 = xp_ref[:, cols] + jnp.dot(merged, wo[:, cols], preferred_element_type=F32)
            h2_out[:, cols] = h2
            h2_scr[:, cols] = h2

        _stream_weight_bf16(wo_hbm, wo, stage, sem, 0, on_chunk)

    @pl.when(jnp.logical_and(i > 0, i < n))
    def _():
        route_prev()
        h2_scr[...] = project()

    @pl.when(i == n)
    def _():
        route_prev()


def _out_proj(merged, xp, xsm, norm2, w_router, w_o):
    t_p, t_s = xp.shape[0], xsm.shape[0]
    n_p, n_s = t_p // TM, t_s // TM
    n = n_p + n_s
    t = t_p + t_s
    xp_spec, xs_spec = _two_stream_specs(n_p, n_s)
    cur = lambda w: pl.BlockSpec((TM, w), lambda i: (jnp.minimum(i, n - 1), 0))
    prev = lambda w: pl.BlockSpec((TM, w), lambda i: (jnp.maximum(i - 1, 0), 0))
    const = lambda r, w: pl.BlockSpec((r, w), lambda i: (0, 0))
    return pl.pallas_call(
        functools.partial(_out_proj_body, n_p),
        grid=(n + 1,),
        in_specs=[cur(D_MODEL), xp_spec, xs_spec, const(1, D_MODEL), const(D_MODEL, 2 * ROUTE_W),
                  pl.BlockSpec(memory_space=pl.ANY)],
        out_specs=[cur(D_MODEL), prev(HALF), prev(ROUTE_W), const(1, ROUTE_W)],
        out_shape=[jax.ShapeDtypeStruct((t, D_MODEL), F32),
                   jax.ShapeDtypeStruct((t, HALF), U32),
                   jax.ShapeDtypeStruct((t, ROUTE_W), F32),
                   jax.ShapeDtypeStruct((1, ROUTE_W), F32)],
        scratch_shapes=[pltpu.VMEM((D_MODEL, D_MODEL), BF16),
                        pltpu.VMEM((2, D_MODEL, WEIGHT_CHUNK), F32),
                        pltpu.SemaphoreType.DMA((2,)),
                        pltpu.VMEM((TM, D_MODEL), F32)],
        compiler_params=pltpu.CompilerParams(dimension_semantics=("arbitrary",), vmem_limit_bytes=VMEM_LIMIT),
        name="out_proj_route",
    )(merged, xp, xsm, norm2, w_router, w_o)


def _moe_plan(rec, cnt, n_tiles):
    t = rec.shape[0]
    n_pairs = 2 * t
    eid = rec[:, 0:2].astype(I32).reshape(-1)
    rank = rec[:, 4:6].astype(I32).reshape(-1)
    counts = cnt[0, :N_EXPERTS].astype(I32)
    pair_start = jnp.cumsum(counts) - counts
    experts = jnp.arange(N_EXPERTS, dtype=I32)
    onehot = (eid[:, None] == experts[None, :]).astype(I32)
    pos = rank + jnp.sum(onehot * pair_start[None, :], axis=1)
    _, order = lax.sort((pos, jnp.arange(n_pairs, dtype=I32)), num_keys=1)
    tiles_e = (counts + TM_MOE - 1) // TM_MOE
    tile_end = jnp.cumsum(tiles_e)
    tile_start = tile_end - tiles_e
    n_valid = tile_end[-1]
    tile_ids = jnp.arange(n_tiles, dtype=I32)
    tile_e = jnp.sum((tile_ids[:, None] >= tile_end[None, :]).astype(I32), axis=1)
    last_e = jnp.sum((n_valid - 1 >= tile_end).astype(I32))
    tile_e = jnp.minimum(jnp.where(tile_ids < n_valid, tile_e, last_e), N_EXPERTS - 1)
    tile_onehot = (tile_e[:, None] == experts[None, :]).astype(I32)
    tile_q0 = (tile_ids - jnp.sum(tile_onehot * tile_start[None, :], axis=1)) * TM_MOE
    tile_q0 = jnp.where(tile_ids < n_valid, tile_q0, 0)
    dst = rank + jnp.sum(onehot * (tile_start * TM_MOE)[None, :], axis=1)
    pad_start = tile_start * TM_MOE + counts
    pad_len = tiles_e * TM_MOE - counts
    return (dst.astype(I32), pad_start.astype(I32), pad_len.astype(I32), tile_e, tile_q0.astype(I32),
            n_valid.astype(I32).reshape(1), pair_start, counts, order)


DISPATCH_BUFS = 3
ROW_PIECES = tuple(TM_MOE >> (b + 1) for b in range(TM_MOE.bit_length() - 1))


def _dispatch_body(dst_ref, pad_start_ref, pad_len_ref, nvalid_ref, v_hbm, x_hbm, buf, zbuf, rsem, ssem, zsem):
    i = pl.program_id(0)
    n = pl.num_programs(0)

    def read(tile, slot):
        return pltpu.make_async_copy(v_hbm.at[pl.ds(pl.multiple_of(tile * TM, TM), TM)], buf.at[slot],
                                     rsem.at[slot])

    def row_write(slot, r, dst_row):
        return pltpu.make_async_copy(buf.at[slot, pl.ds(r, 1)], x_hbm.at[pl.ds(dst_row, 1)], ssem.at[slot])

    def drain(slot):
        for _ in range(2):
            pltpu.make_async_copy(buf.at[slot], x_hbm.at[pl.ds(0, TM)], ssem.at[slot]).wait()

    def pad_fill(go):
        def zero_rows(start, size):
            d = pltpu.make_async_copy(zbuf.at[pl.ds(0, size)], x_hbm.at[pl.ds(start, size)], zsem)
            d.start() if go else d.wait()

        def body(e, c):
            start, length = pad_start_ref[e], pad_len_ref[e]
            head = (-start) & (SUBLANES - 1)
            for h in range(SUBLANES - 1):
                @pl.when(h < head)
                def _(h=h):
                    zero_rows(start + h, 1)
            start, length = start + head, length - head
            for size in ROW_PIECES:
                if size >= SUBLANES:
                    @pl.when((length & size) != 0)
                    def _(size=size):
                        zero_rows(pl.multiple_of(start + (length & (-2 * size)), SUBLANES), size)
            return c
        lax.fori_loop(0, N_EXPERTS, body, 0)

        def unused(tile, c):
            for half in range(TM_MOE // ROW_PIECES[0]):
                zero_rows(pl.multiple_of(tile * TM_MOE + half * ROW_PIECES[0], SUBLANES), ROW_PIECES[0])
            return c
        lax.fori_loop(nvalid_ref[0], x_hbm.shape[0] // TM_MOE, unused, 0)

    @pl.when(i == 0)
    def _():
        zbuf[...] = jnp.zeros(zbuf.shape, U32)
        pad_fill(True)
        read(0, 0).start()

    @pl.when(i >= 2)
    def _():
        drain((i + 1) % DISPATCH_BUFS)

    @pl.when(i + 1 < n)
    def _():
        read(i + 1, (i + 1) % DISPATCH_BUFS).start()

    slot = i % DISPATCH_BUFS
    read(i, slot).wait()
    for r in range(TM):
        for k in range(2):
            row_write(slot, r, dst_ref[2 * (i * TM + r) + k]).start(priority=k)

    @pl.when(i == n - 1)
    def _():
        if n >= 2:
            drain((i - 1) % DISPATCH_BUFS)
        drain(slot)
        pad_fill(False)


def _dispatch(v, plan, n_tiles):
    dst, pad_start, pad_len, n_valid = plan[0], plan[1], plan[2], plan[5]
    t = v.shape[0]
    return pl.pallas_call(
        _dispatch_body,
        grid_spec=pltpu.PrefetchScalarGridSpec(
            num_scalar_prefetch=4,
            grid=(t // TM,),
            in_specs=[pl.BlockSpec(memory_space=pl.ANY)],
            out_specs=pl.BlockSpec(memory_space=pl.ANY),
            scratch_shapes=[pltpu.VMEM((DISPATCH_BUFS, TM, HALF), U32),
                            pltpu.VMEM((ROW_PIECES[0], HALF), U32),
                            pltpu.SemaphoreType.DMA((DISPATCH_BUFS,)),
                            pltpu.SemaphoreType.DMA((DISPATCH_BUFS,)),
                            pltpu.SemaphoreType.DMA(())]),
        out_shape=jax.ShapeDtypeStruct((n_tiles * TM_MOE, HALF), U32),
        compiler_params=pltpu.CompilerParams(dimension_semantics=("arbitrary",), vmem_limit_bytes=VMEM_LIMIT),
        name="moe_dispatch",
    )(dst, pad_start, pad_len, n_valid, v)


WEIGHT_SLOTS = 3


def _moe_body(n_tok, plane, tile_e_ref, tile_q0_ref, nvalid_ref, pstart_ref, cnt_ref, orow_ref,
              krank_ref, elist_ref, nexp_ref,
              x_ref, wg_hbm, wu_hbm, wd_hbm, o_hbm, ybuf0, ybuf1, ybuf2, ssem,
              stage_g, stage_u, stage_d, wsem, wg, wu, wd):
    i = pl.program_id(0)
    nv = nvalid_ref[0]
    bufs = (ybuf0, ybuf1, ybuf2)

    def scratch_row0(slot):
        return (slot & 1) * plane + n_tok + (slot >> 1) * TM_MOE

    def row_write(slot, r, dst_row):
        return pltpu.make_async_copy(bufs[slot].at[pl.ds(r, 1)], o_hbm.at[pl.ds(dst_row, 1)], ssem.at[slot])

    def scratch_rows(region):
        return pltpu.make_async_copy(ybuf0, o_hbm.at[pl.ds(scratch_row0(region), TM_MOE)], ssem.at[0])

    def start_writes(tile, slot, rows):
        e = tile_e_ref[tile]
        valid = cnt_ref[e] - tile_q0_ref[tile]
        first = pstart_ref[e] + tile_q0_ref[tile]
        for r in rows:
            row_write(slot, r, jnp.where(r < valid, orow_ref[first + r], scratch_row0(slot) + r)).start(
                priority=r % 2)

    def compute(slot, writes=None):
        quarter = TM_MOE // 4
        batch = lambda q: start_writes(*writes, range(q * quarter, (q + 1) * quarter)) if writes else None
        x_lo, x_hi = (h.astype(BF16) for h in _unpack_halves(x_ref[...]))
        batch(0)
        hg = (jnp.dot(x_lo, wg[0:HALF, :], preferred_element_type=F32)
              + jnp.dot(x_hi, wg[HALF:D_MODEL, :], preferred_element_type=F32))
        batch(1)
        hu = (jnp.dot(x_lo, wu[0:HALF, :], preferred_element_type=F32)
              + jnp.dot(x_hi, wu[HALF:D_MODEL, :], preferred_element_type=F32))
        batch(2)
        act = hg * _sigmoid(hg) * hu
        y = jnp.dot(act.astype(BF16), wd[...], preferred_element_type=F32)
        batch(3)
        bufs[slot][...] = _pack_halves(y)

    @pl.when(i == 0)
    def _():
        ybuf0[...] = jnp.zeros(ybuf0.shape, U32)
        for region in range(4):
            scratch_rows(region).start()
        for region in range(4):
            scratch_rows(region).wait()

    @pl.when(jnp.logical_and(i >= 3, i < nv + 3))
    def _():
        pltpu.make_async_copy(ybuf0, o_hbm.at[pl.ds(0, TM_MOE)], ssem.at[i % 3]).wait()

    def weight_copies(k, go):
        e = elist_ref[k]
        slot = k % WEIGHT_SLOTS
        for w_hbm, st in ((wg_hbm, stage_g), (wu_hbm, stage_u), (wd_hbm, stage_d)):
            d = pltpu.make_async_copy(w_hbm.at[e], st.at[slot], wsem.at[slot])
            d.start() if go else d.wait()

    @pl.when(i == 0)
    def _():
        for k in range(WEIGHT_SLOTS):
            @pl.when(k < nexp_ref[0])
            def _(k=k):
                weight_copies(k, True)

    @pl.when(i < nv)
    def _():
        prev_e = tile_e_ref[jnp.maximum(i - 1, 0)]

        @pl.when(jnp.logical_or(i == 0, tile_e_ref[i] != prev_e))
        def _():
            k = krank_ref[tile_e_ref[i]]
            slot = k % WEIGHT_SLOTS
            weight_copies(k, False)
            wg[...] = stage_g[slot].astype(BF16)
            wu[...] = stage_u[slot].astype(BF16)
            wd[...] = stage_d[slot].astype(BF16)

            @pl.when(k + WEIGHT_SLOTS < nexp_ref[0])
            def _():
                weight_copies(k + WEIGHT_SLOTS, True)

    @pl.when(i == 0)
    def _():
        compute(0)

    for slot in range(3):
        prev = (slot + 2) % 3

        @pl.when(jnp.logical_and(i % 3 == slot, jnp.logical_and(i >= 1, i < nv)))
        def _(slot=slot, prev=prev):
            compute(slot, writes=(i - 1, prev))

        @pl.when(jnp.logical_and(i % 3 == slot, i == nv))
        def _(prev=prev):
            start_writes(i - 1, prev, range(TM_MOE))


def _moe(x_disp, plan, w_gate, w_up, w_down, n_tiles, n_tok):
    tile_e, tile_q0, n_valid, pair_start, counts, order = plan[3:9]
    plane = n_tok + 2 * TM_MOE
    orow = jnp.pad((order & 1) * plane + (order >> 1), (0, TM_MOE))
    present = (counts > 0).astype(I32)
    krank = jnp.cumsum(present) - present
    experts = jnp.arange(N_EXPERTS, dtype=I32)
    elist = jnp.sum(jnp.where((krank[None, :] == experts[:, None]) & (present[None, :] > 0), experts[None, :], 0),
                    axis=1).astype(I32)
    nexp = jnp.sum(present).astype(I32).reshape(1)
    tile = lambda i, nv: jnp.minimum(i, jnp.maximum(nv[0] - 1, 0))
    any_spec = pl.BlockSpec(memory_space=pl.ANY)
    ybuf = pltpu.VMEM((TM_MOE, HALF), U32)
    return pl.pallas_call(
        functools.partial(_moe_body, n_tok, plane),
        grid_spec=pltpu.PrefetchScalarGridSpec(
            num_scalar_prefetch=9,
            grid=(n_tiles + 3,),
            in_specs=[pl.BlockSpec((TM_MOE, HALF), lambda i, te, tq, nv, *_: (tile(i, nv), 0)),
                      any_spec, any_spec, any_spec],
            out_specs=pl.BlockSpec(memory_space=pl.ANY),
            scratch_shapes=[ybuf, ybuf, ybuf,
                            pltpu.SemaphoreType.DMA((3,)),
                            pltpu.VMEM((WEIGHT_SLOTS, D_MODEL, D_EXPERT), F32),
                            pltpu.VMEM((WEIGHT_SLOTS, D_MODEL, D_EXPERT), F32),
                            pltpu.VMEM((WEIGHT_SLOTS, D_EXPERT, D_MODEL), F32),
                            pltpu.SemaphoreType.DMA((WEIGHT_SLOTS,)),
                            pltpu.VMEM((D_MODEL, D_EXPERT), BF16),
                            pltpu.VMEM((D_MODEL, D_EXPERT), BF16),
                            pltpu.VMEM((D_EXPERT, D_MODEL), BF16)]),
        out_shape=jax.ShapeDtypeStruct((2 * plane, HALF), U32),
        compiler_params=pltpu.CompilerParams(dimension_semantics=("arbitrary",), vmem_limit_bytes=VMEM_LIMIT),
        name="moe_experts",
    )(tile_e, tile_q0, n_valid, pair_start, counts, orow, krank.astype(I32), elist, nexp,
      x_disp, w_gate, w_up, w_down)


def _combine_body(n_p_tiles, h2_ref, rec_ref, gf_ref, y1_ref, y2_ref, outp_ref, outs_ref):
    i = pl.program_id(0)
    rec = rec_ref[...]
    y1 = jnp.concatenate(_unpack_halves(y1_ref[...]), axis=-1)
    y2 = jnp.concatenate(_unpack_halves(y2_ref[...]), axis=-1)
    h = rec[:, 2:3] * y1 + rec[:, 3:4] * y2
    out = _rmsnorm(h2_ref[...] + h, gf_ref[...])

    @pl.when(i < n_p_tiles)
    def _():
        outp_ref[...] = out

    @pl.when(i >= n_p_tiles)
    def _():
        outs_ref[...] = out


def _combine(h2, rec, y_pairs, final_norm, t_p, t_s):
    n_p, n_s = t_p // TM, t_s // TM
    plane_tiles = y_pairs.shape[0] // 2 // TM
    return pl.pallas_call(
        functools.partial(_combine_body, n_p),
        grid=(n_p + n_s,),
        in_specs=[pl.BlockSpec((TM, D_MODEL), lambda i: (i, 0)),
                  pl.BlockSpec((TM, ROUTE_W), lambda i: (i, 0)),
                  pl.BlockSpec((1, D_MODEL), lambda i: (0, 0)),
                  pl.BlockSpec((TM, HALF), lambda i: (i, 0)),
                  pl.BlockSpec((TM, HALF), lambda i: (i + plane_tiles, 0))],
        out_specs=[pl.BlockSpec((TM, D_MODEL), lambda i: (jnp.minimum(i, n_p - 1), 0)),
                   pl.BlockSpec((TM, D_MODEL), lambda i: (jnp.clip(i - n_p, 0, n_s - 1), 0))],
        out_shape=[jax.ShapeDtypeStruct((t_p, D_MODEL), F32),
                   jax.ShapeDtypeStruct((t_s, D_MODEL), F32)],
        compiler_params=pltpu.CompilerParams(dimension_semantics=("arbitrary",), vmem_limit_bytes=VMEM_LIMIT),
        name="combine_norm",
    )(h2, rec, final_norm, y_pairs, y_pairs)


def kernel(x_prompt, x_sample, state_conv, state_ssm_re, state_ssm_im, meta_tokens, norm1, w_in, conv_w,
           lam_re, lam_im, log_dt, ssm_b_re, ssm_b_im, ssm_c_re, ssm_c_im, ssm_d, w_glu, w_conv_out,
           w_ssm_out, w_o, norm2, w_coarse, w_fine, w_gate, w_up, w_down, final_norm):
    n_pb, seq, _ = x_prompt.shape
    n_sb, dec_seq, _ = x_sample.shape
    assert dec_seq == CHUNK and seq % TM == 0 and (n_sb * dec_seq) % TM == 0 and N_META == CHUNK
    t_p, t_s = n_pb * seq, n_sb * dec_seq
    xp = x_prompt.reshape(t_p, D_MODEL)
    xsm = x_sample.reshape(t_s, D_MODEL)

    xb, z, xs, zmeta, xsmeta, sga, sgb = _in_proj(xp, xsm, meta_tokens, norm1, w_in[0])

    mats = _s5_chunk_mats(lam_re[0], lam_im[0], log_dt[0], ssm_b_re[0], ssm_b_im[0], ssm_c_re[0], ssm_c_im[0])
    y5, pf_re, pf_im, sf_re, sf_im = _s5(xs, xsmeta, state_ssm_re[0], state_ssm_im[0], mats, n_pb, n_sb, seq)

    buf = state_conv[0]
    zero = jnp.zeros((n_sb, dec_seq, D_CONV), F32)
    inj1 = zero.at[:, 0].set(buf[:, 1]).reshape(t_s, D_CONV)
    inj2 = zero.at[:, 0].set(buf[:, 0]).at[:, 1].set(buf[:, 1]).reshape(t_s, D_CONV)
    merged = _mixers(xb, z, y5, xs, sga, sgb, zmeta, inj1, inj2, conv_w[0], ssm_d, w_conv_out[0], w_glu[0],
                     w_ssm_out[0], t_p // TM, seq // TM)

    w_router = jnp.concatenate(
        [w_fine[0], w_coarse[0], jnp.zeros((D_MODEL, ROUTE_W - N_EXPERTS - N_EGROUPS), F32)], axis=1)
    w_router = jnp.concatenate(_split_bf16(w_router), axis=1)
    h2, v, rec, cnt = _out_proj(merged, xp, xsm, norm2, w_router, w_o[0])

    n_tiles = 2 * (t_p + t_s) // TM_MOE + N_EXPERTS
    plan = _moe_plan(rec, cnt, n_tiles)
    x_disp = _dispatch(v, plan, n_tiles)
    y_pairs = _moe(x_disp, plan, w_gate[0], w_up[0], w_down[0], n_tiles, t_p + t_s)
    y_p, y_s = _combine(h2, rec, y_pairs, final_norm.reshape(1, D_MODEL), t_p, t_s)

    new_conv_p = jnp.stack([z[(b + 1) * seq - 2:(b + 1) * seq] for b in range(n_pb)])
    new_conv_s = z[t_p:].reshape(n_sb, dec_seq, D_CONV)[:, dec_seq - 2:]
    return (y_p.reshape(n_pb, seq, D_MODEL), y_s.reshape(n_sb, dec_seq, D_MODEL),
            new_conv_p[None], pf_re, pf_im, new_conv_s[None], sf_re, sf_im)
```

```python
import functools

import jax
import jax.numpy as jnp
from jax import lax
from jax.experimental import pallas as pl
from jax.experimental.pallas import tpu as pltpu

F32 = jnp.float32
BF16 = jnp.bfloat16
I32 = jnp.int32
U32 = jnp.uint32

D_MODEL = 2048
D_CONV = 1024
D_SSM = 1024
SSM_H = 16
SSM_G = 64
SSM_P = 64
N_META = 16
N_EGROUPS = 4
EXPERTS_PER_GROUP = 8
N_EXPERTS = 32
D_EXPERT = 256
EPS = 1e-6

LANES = 128
SUBLANES = 8

CHUNK = 16
CHUNK_W = CHUNK * SSM_H

TM = 256
TM_MOE = 256
VMEM_LIMIT = 52 * 1024 * 1024


def _rmsnorm(x, g):
    return x * lax.rsqrt(jnp.mean(x * x, axis=-1, keepdims=True) + EPS) * g


def _sigmoid(x):
    return 1.0 / (1.0 + jnp.exp(-x))


def _gelu_tanh(x):
    c = 0.7978845608028654
    return 0.5 * x * (1.0 + jnp.tanh(c * (x + 0.044715 * (x * x * x))))


def _split_bf16(a):
    hi = a.astype(BF16)
    lo = (a - hi.astype(F32)).astype(BF16)
    return hi, lo


def _dot3(a, b):
    a_hi, a_lo = _split_bf16(a)
    b_hi, b_lo = _split_bf16(b)
    return (jnp.dot(a_hi, b_hi, preferred_element_type=F32) + jnp.dot(a_lo, b_hi, preferred_element_type=F32)
            + jnp.dot(a_hi, b_lo, preferred_element_type=F32))


HALF = D_MODEL // 2


def _pack_halves(a):
    return pltpu.pack_elementwise([a[:, :HALF], a[:, HALF:]], packed_dtype=BF16)


def _unpack_halves(p):
    return (pltpu.unpack_elementwise(p, index=0, packed_dtype=BF16, unpacked_dtype=F32),
            pltpu.unpack_elementwise(p, index=1, packed_dtype=BF16, unpacked_dtype=F32))


def _weight_copy(w_hbm, stage, sem, c, slot, rows, col0, ncols):
    return pltpu.make_async_copy(
        w_hbm.at[pl.ds(c * rows, rows), pl.ds(col0, ncols)], stage.at[slot], sem.at[slot])


def _load_weight_bf16(w_hbm, w_vmem, stage, sem, col0=0):
    k, n = w_vmem.shape
    rows = stage.shape[1]
    nchunk = k // rows
    _weight_copy(w_hbm, stage, sem, 0, 0, rows, col0, n).start()
    for c in range(nchunk):
        slot = c % 2
        if c + 1 < nchunk:
            _weight_copy(w_hbm, stage, sem, c + 1, 1 - slot, rows, col0, n).start()
        _weight_copy(w_hbm, stage, sem, c, slot, rows, col0, n).wait()
        w_vmem[pl.ds(c * rows, rows), :] = stage[slot].astype(BF16)


WEIGHT_CHUNK = 512


def _stream_weight_bf16(w_hbm, w_vmem, stage, sem, col0, on_chunk):
    _, n = w_vmem.shape
    w = stage.shape[2]
    nchunk = n // w

    def copy(c, slot):
        return pltpu.make_async_copy(w_hbm.at[:, pl.ds(col0 + c * w, w)], stage.at[slot], sem.at[slot])

    copy(0, 0).start()
    for c in range(nchunk):
        slot = c % 2
        if c + 1 < nchunk:
            copy(c + 1, 1 - slot).start()
        copy(c, slot).wait()
        w_vmem[:, c * w:(c + 1) * w] = stage[slot].astype(BF16)
        on_chunk(c)


def _in_proj_mix_body(n_p_tiles, xp_ref, xsm_ref, meta_ref, g_ref, w_hbm,
                      xb_out, z_out, xs_out, zmeta_out, xsmeta_out,
                      w_vmem, stage, sem):
    i = pl.program_id(0)
    g = g_ref[...]
    cw = stage.shape[2]

    @pl.when(i == 0)
    def _():
        u = _rmsnorm(xp_ref[...], g).astype(BF16)
        um = _rmsnorm(meta_ref[...], g).astype(BF16)

        def on_chunk(c):
            n0 = c * cw
            cols = slice(n0 % D_CONV, n0 % D_CONV + cw)
            p = jnp.dot(u, w_vmem[:, n0:n0 + cw], preferred_element_type=F32)
            pm = jnp.dot(um, w_vmem[:, n0:n0 + cw], preferred_element_type=F32)
            if n0 < D_CONV:
                xb_out[:, cols] = p.astype(BF16)
            elif n0 < 2 * D_CONV:
                z_out[:, cols] = p
                zmeta_out[:, cols] = pm
            elif n0 < 3 * D_CONV:
                z_out[:, cols] = z_out[:, cols] * p
                zmeta_out[:, cols] = zmeta_out[:, cols] * pm
            else:
                xs_out[:, cols] = p
                xsmeta_out[:, cols] = pm

        _stream_weight_bf16(w_hbm, w_vmem, stage, sem, 0, on_chunk)

    @pl.when(i > 0)
    def _():
        x = jnp.where(i < n_p_tiles, xp_ref[...], xsm_ref[...])
        u = _rmsnorm(x, g).astype(BF16)
        xb = jnp.dot(u, w_vmem[:, 0:D_CONV], preferred_element_type=F32)
        xc = jnp.dot(u, w_vmem[:, D_CONV:2 * D_CONV], preferred_element_type=F32)
        xv = jnp.dot(u, w_vmem[:, 2 * D_CONV:3 * D_CONV], preferred_element_type=F32)
        xs = jnp.dot(u, w_vmem[:, 3 * D_CONV:3 * D_CONV + D_SSM], preferred_element_type=F32)
        xb_out[...] = xb.astype(BF16)
        z_out[...] = xc * xv
        xs_out[...] = xs


def _in_proj_gate_body(n_p_tiles, xp_ref, xsm_ref, g_ref, w_hbm, ga_out, gb_out,
                       w_vmem, stage, sem):
    i = pl.program_id(0)
    cw = stage.shape[2]

    @pl.when(i == 0)
    def _():
        u = _rmsnorm(xp_ref[...], g_ref[...]).astype(BF16)

        def on_chunk(c):
            n0 = c * cw
            out = ga_out if n0 < D_MODEL else gb_out
            gate = _sigmoid(jnp.dot(u, w_vmem[:, n0:n0 + cw], preferred_element_type=F32))
            out[:, n0 % D_MODEL:n0 % D_MODEL + cw] = gate.astype(BF16)

        _stream_weight_bf16(w_hbm, w_vmem, stage, sem, 3 * D_CONV + D_SSM, on_chunk)

    @pl.when(i > 0)
    def _():
        x = jnp.where(i < n_p_tiles, xp_ref[...], xsm_ref[...])
        u = _rmsnorm(x, g_ref[...]).astype(BF16)
        gates = jnp.dot(u, w_vmem[...], preferred_element_type=F32)
        ga_out[...] = _sigmoid(gates[:, 0:D_MODEL]).astype(BF16)
        gb_out[...] = _sigmoid(gates[:, D_MODEL:2 * D_MODEL]).astype(BF16)


def _two_stream_specs(n_p_tiles, n_s_tiles, tm=TM):
    xp_spec = pl.BlockSpec((tm, D_MODEL), lambda i: (jnp.minimum(i, n_p_tiles - 1), 0))
    mode = pl.Buffered(1) if n_s_tiles == 1 else None
    xs_spec = pl.BlockSpec((tm, D_MODEL), lambda i: (jnp.clip(i - n_p_tiles, 0, n_s_tiles - 1), 0),
                           pipeline_mode=mode)
    return xp_spec, xs_spec


TM_IN = TM


def _in_proj(xp, xsm, meta, norm1, w_in):
    t_p, t_s = xp.shape[0], xsm.shape[0]
    n_p, n_s = t_p // TM_IN, t_s // TM_IN
    t = t_p + t_s
    half = 3 * D_CONV + D_SSM
    xp_spec, xs_spec = _two_stream_specs(n_p, n_s, TM_IN)
    g_spec = pl.BlockSpec((1, D_MODEL), lambda i: (0, 0))
    any_spec = pl.BlockSpec(memory_space=pl.ANY)
    row = lambda w: pl.BlockSpec((TM_IN, w), lambda i: (i, 0))
    const = lambda r, w: pl.BlockSpec((r, w), lambda i: (0, 0))
    params = pltpu.CompilerParams(dimension_semantics=("arbitrary",), vmem_limit_bytes=VMEM_LIMIT)
    scratch = [pltpu.VMEM((D_MODEL, half), BF16),
               pltpu.VMEM((2, D_MODEL, WEIGHT_CHUNK), F32),
               pltpu.SemaphoreType.DMA((2,))]

    xb, z, xs, zmeta, xsmeta = pl.pallas_call(
        functools.partial(_in_proj_mix_body, n_p),
        grid=(n_p + n_s,),
        in_specs=[xp_spec, xs_spec, const(N_META, D_MODEL), g_spec, any_spec],
        out_specs=[row(D_CONV), row(D_CONV), row(D_SSM), const(N_META, D_CONV), const(N_META, D_SSM)],
        out_shape=[jax.ShapeDtypeStruct((t, D_CONV), BF16),
                   jax.ShapeDtypeStruct((t, D_CONV), F32),
                   jax.ShapeDtypeStruct((t, D_SSM), F32),
                   jax.ShapeDtypeStruct((N_META, D_CONV), F32),
                   jax.ShapeDtypeStruct((N_META, D_SSM), F32)],
        scratch_shapes=scratch,
        compiler_params=params,
        name="in_proj_mix",
    )(xp, xsm, meta, norm1, w_in)

    sga, sgb = pl.pallas_call(
        functools.partial(_in_proj_gate_body, n_p),
        grid=(n_p + n_s,),
        in_specs=[xp_spec, xs_spec, g_spec, any_spec],
        out_specs=[row(D_MODEL), row(D_MODEL)],
        out_shape=[jax.ShapeDtypeStruct((t, D_MODEL), BF16),
                   jax.ShapeDtypeStruct((t, D_MODEL), BF16)],
        scratch_shapes=scratch,
        compiler_params=params,
        name="in_proj_gate",
    )(xp, xsm, norm1, w_in)
    return xb, z, xs, zmeta, xsmeta, sga, sgb


S5_GROUPS_PER_STEP = LANES // SSM_H
S5_PAIRS_PER_STEP = S5_GROUPS_PER_STEP // 2


def _s5_chunk_mats(lam_re, lam_im, log_dt, b_re, b_im, c_re, c_im):
    dt = jnp.exp(log_dt)[:, None]
    lr, li = lam_re, lam_im
    mag = jnp.exp(lr * dt)
    ab_re, ab_im = mag * jnp.cos(li * dt), mag * jnp.sin(li * dt)
    nr, ni = ab_re - 1.0, ab_im
    den = lr * lr + li * li
    k_re = (nr * lr + ni * li) / den
    k_im = (ni * lr - nr * li) / den
    bb_re = k_re[..., None] * b_re - k_im[..., None] * b_im
    bb_im = k_re[..., None] * b_im + k_im[..., None] * b_re
    return lr * dt, li * dt, bb_re, bb_im, c_re, c_im


def _cmul(ar, ai, br, bi):
    return ar * br - ai * bi, ar * bi + ai * br


def _s5_operators(pair, parity, params, toe_scr, p_scr, q_scr, a16_scr, g_scr, m_scr):
    k = 2 * pair + parity
    zr_ref, zi_ref, bbr_ref, bbi_ref, cr_ref, ci_ref = params
    zr, zi = zr_ref[k:k + 1, :], zi_ref[k:k + 1, :]
    mag = jnp.exp(zr)
    ar, ai = mag * jnp.cos(zi), mag * jnp.sin(zi)
    eye = lax.broadcasted_iota(I32, (SSM_P, SSM_P), 0) == lax.broadcasted_iota(I32, (SSM_P, SSM_P), 1)
    acr = jnp.sum(jnp.where(eye, ar, 0.0), axis=1, keepdims=True)
    aci = jnp.sum(jnp.where(eye, ai, 0.0), axis=1, keepdims=True)
    c_re, c_im = cr_ref[k], ci_ref[k]
    bb_re, bb_im = bbr_ref[k], bbi_ref[k]

    blk = lax.broadcasted_iota(I32, (1, CHUNK_W), 1) >> 4

    pr, pi = jnp.ones_like(ar), jnp.zeros_like(ar)
    pcr, pci = jnp.ones_like(acr), jnp.zeros_like(acr)
    pw_re = jnp.zeros((SSM_P, CHUNK_W), F32)
    pw_im = jnp.zeros((SSM_P, CHUNK_W), F32)
    for d in range(CHUNK + 1):
        g_scr[0, d * SSM_H:(d + 1) * SSM_H, :] = c_re * pr - c_im * pi
        g_scr[1, d * SSM_H:(d + 1) * SSM_H, :] = c_re * pi + c_im * pr
        if d < CHUNK:
            pw_re = jnp.where(blk == CHUNK - 1 - d, pcr, pw_re)
            pw_im = jnp.where(blk == CHUNK - 1 - d, pci, pw_im)
            pcr, pci = _cmul(pcr, pci, acr, aci)
            pr, pi = _cmul(pr, pi, ar, ai)

    half = parity * SSM_P
    a16_scr[pair, 0, :, half:half + SSM_P] = pr
    a16_scr[pair, 1, :, half:half + SSM_P] = pi
    q_scr[k] = jnp.zeros(q_scr.shape[1:], BF16)
    q_scr[k, 0, :, half:half + SSM_P] = g_scr[0, SSM_H:, :].astype(BF16)
    q_scr[k, 1, :, half:half + SSM_P] = (-g_scr[1, SSM_H:, :]).astype(BF16)

    rep = jnp.where(lax.broadcasted_iota(I32, (SSM_H, CHUNK_W), 0)
                    == (lax.broadcasted_iota(I32, (SSM_H, CHUNK_W), 1) & (SSM_H - 1)), 1.0, 0.0).astype(BF16)

    def widen(b):
        b_hi, b_lo = _split_bf16(b)
        return (jnp.dot(b_hi, rep, preferred_element_type=F32) + jnp.dot(b_lo, rep, preferred_element_type=F32))

    bw_re, bw_im = widen(bb_re), widen(bb_im)
    p_re, p_im = _cmul(pw_re, pw_im, bw_re, bw_im)
    p_scr[k, 0:SSM_P, :] = p_re.astype(BF16)
    p_scr[k, SSM_P:2 * SSM_P, :] = p_im.astype(BF16)

    m_scr[...] = _dot3(g_scr[0, 0:CHUNK_W, :], bw_re) - _dot3(g_scr[1, 0:CHUNK_W, :], bw_im)
    for t in range(CHUNK):
        acc = jnp.where(blk == 0, m_scr[t * SSM_H:(t + 1) * SSM_H, :], 0.0)
        for j in range(1, t + 1):
            acc = jnp.where(blk == j, m_scr[(t - j) * SSM_H:(t - j + 1) * SSM_H, :], acc)
        toe_scr[k, t * SSM_H:(t + 1) * SSM_H, :] = acc.astype(BF16)


def _s5_body(n_pc, n_pb, n_sb, xs_ref, xsmeta_ref, zr_ref, zi_ref, bbr_ref, bbi_ref, cr_ref, ci_ref,
             s0re_ref, s0im_ref,
             y_out, pfre_out, pfim_out, sfre_out, sfim_out,
             u_scr, sl_re, sl_im, sp_re, sp_im, yt_scr, toe_ref, p_ref, q_ref, a16_scr,
             g_scr, m_scr):
    gb = S5_GROUPS_PER_STEP
    n_p_rows = n_pb * n_pc
    row_s = n_p_rows
    row_m = row_s + n_sb
    t_p = n_p_rows * CHUNK
    rows_pad = u_scr.shape[2]

    for k in range(gb):
        _s5_operators(k // 2, k % 2, (zr_ref, zi_ref, bbr_ref, bbi_ref, cr_ref, ci_ref),
                      toe_ref, p_ref, q_ref, a16_scr, g_scr, m_scr)

    first_row = lax.broadcasted_iota(I32, (SUBLANES, 1), 0) == 0
    tail = jnp.zeros((rows_pad - row_m - SUBLANES, LANES), F32)
    for t in range(CHUNK):
        meta_rows = jnp.where(first_row, xsmeta_ref[t:t + 1, :], 0.0)
        rows_t = jnp.concatenate([xs_ref[pl.ds(t, n_p_rows, stride=CHUNK), :],
                                  xs_ref[pl.ds(t_p + t, n_sb, stride=CHUNK), :], meta_rows, tail], axis=0)
        xt = rows_t.T.astype(BF16)
        for k in range(gb):
            u_scr[k, t * SSM_H:(t + 1) * SSM_H, :] = xt[k * SSM_H:(k + 1) * SSM_H, :]

    npair = S5_PAIRS_PER_STEP
    for j in range(npair):
        sl0 = jnp.dot(p_ref[2 * j], u_scr[2 * j], preferred_element_type=F32)
        sl1 = jnp.dot(p_ref[2 * j + 1], u_scr[2 * j + 1], preferred_element_type=F32)
        sl_re[j] = jnp.concatenate([sl0[0:SSM_P, :], sl1[0:SSM_P, :]], axis=0).T
        sl_im[j] = jnp.concatenate([sl0[SSM_P:2 * SSM_P, :], sl1[SSM_P:2 * SSM_P, :]], axis=0).T
        sp_re[j, row_m:rows_pad, :] = jnp.zeros((rows_pad - row_m, 2 * SSM_P), F32)
        sp_im[j, row_m:rows_pad, :] = jnp.zeros((rows_pad - row_m, 2 * SSM_P), F32)

    ar = [a16_scr[j, 0] for j in range(npair)]
    ai = [a16_scr[j, 1] for j in range(npair)]
    sre = [jnp.broadcast_to(sl_re[j, row_m:row_m + 1, :], (n_pb, 2 * SSM_P)) for j in range(npair)]
    sim = [jnp.broadcast_to(sl_im[j, row_m:row_m + 1, :], (n_pb, 2 * SSM_P)) for j in range(npair)]
    for c in range(n_pc):
        rows = pl.ds(c, n_pb, stride=n_pc)
        for j in range(npair):
            sp_re[j, rows, :] = sre[j]
            sp_im[j, rows, :] = sim[j]
            nre = ar[j] * sre[j] - ai[j] * sim[j] + sl_re[j, rows, :]
            nim = ar[j] * sim[j] + ai[j] * sre[j] + sl_im[j, rows, :]
            sre[j], sim[j] = nre, nim
    for j in range(npair):
        pfre_out[j] = sre[j]
        pfim_out[j] = sim[j]
        s0r, s0i = s0re_ref[j], s0im_ref[j]
        sp_re[j, row_s:row_m, :] = s0r
        sp_im[j, row_s:row_m, :] = s0i
        sfre_out[j] = ar[j] * s0r - ai[j] * s0i + sl_re[j, row_s:row_m, :]
        sfim_out[j] = ar[j] * s0i + ai[j] * s0r + sl_im[j, row_s:row_m, :]

    nt = (((1,), (1,)), ((), ()))
    for k in range(gb):
        y = jnp.dot(toe_ref[k], u_scr[k], preferred_element_type=F32)
        y += lax.dot_general(q_ref[k, 0], sp_re[k // 2].astype(BF16), nt, preferred_element_type=F32)
        y += lax.dot_general(q_ref[k, 1], sp_im[k // 2].astype(BF16), nt, preferred_element_type=F32)
        for t in range(CHUNK):
            yt_scr[t, k * SSM_H:(k + 1) * SSM_H, :] = y[t * SSM_H:(t + 1) * SSM_H, :]
    for t in range(CHUNK):
        yt = yt_scr[t].T
        y_out[pl.ds(t, n_p_rows, stride=CHUNK), :] = yt[0:n_p_rows, :]
        y_out[pl.ds(t_p + t, n_sb, stride=CHUNK), :] = yt[row_s:row_m, :]


def _s5(xs, xsmeta, state_re, state_im, mats, n_pb, n_sb, seq):
    t = xs.shape[0]
    n_pc = seq // CHUNK
    rows = n_pc * n_pb + n_sb + 1
    rows_pad = -(-rows // LANES) * LANES
    gb, npair = S5_GROUPS_PER_STEP, S5_PAIRS_PER_STEP
    pairs = lambda s: jnp.transpose(s.reshape(n_sb, SSM_G // 2, 2 * SSM_P), (1, 0, 2))
    blk3 = lambda n, r, c: pl.BlockSpec((n, r, c), lambda i: (i, 0, 0))
    y, pfre, pfim, sfre, sfim = pl.pallas_call(
        functools.partial(_s5_body, n_pc, n_pb, n_sb),
        grid=(SSM_G // gb,),
        in_specs=[pl.BlockSpec((t, LANES), lambda i: (0, i)),
                  pl.BlockSpec((N_META, LANES), lambda i: (0, i)),
                  pl.BlockSpec((gb, SSM_P), lambda i: (i, 0)), pl.BlockSpec((gb, SSM_P), lambda i: (i, 0)),
                  blk3(gb, SSM_P, SSM_H), blk3(gb, SSM_P, SSM_H),
                  blk3(gb, SSM_H, SSM_P), blk3(gb, SSM_H, SSM_P),
                  blk3(npair, n_sb, 2 * SSM_P), blk3(npair, n_sb, 2 * SSM_P)],
        out_specs=[pl.BlockSpec((t, LANES), lambda i: (0, i)),
                   blk3(npair, n_pb, 2 * SSM_P), blk3(npair, n_pb, 2 * SSM_P),
                   blk3(npair, n_sb, 2 * SSM_P), blk3(npair, n_sb, 2 * SSM_P)],
        out_shape=[jax.ShapeDtypeStruct((t, D_SSM), F32),
                   jax.ShapeDtypeStruct((SSM_G // 2, n_pb, 2 * SSM_P), F32),
                   jax.ShapeDtypeStruct((SSM_G // 2, n_pb, 2 * SSM_P), F32),
                   jax.ShapeDtypeStruct((SSM_G // 2, n_sb, 2 * SSM_P), F32),
                   jax.ShapeDtypeStruct((SSM_G // 2, n_sb, 2 * SSM_P), F32)],
        scratch_shapes=[pltpu.VMEM((gb, CHUNK_W, rows_pad), BF16),
                        pltpu.VMEM((npair, rows_pad, 2 * SSM_P), F32),
                        pltpu.VMEM((npair, rows_pad, 2 * SSM_P), F32),
                        pltpu.VMEM((npair, rows_pad, 2 * SSM_P), F32),
                        pltpu.VMEM((npair, rows_pad, 2 * SSM_P), F32),
                        pltpu.VMEM((CHUNK, LANES, rows_pad), F32),
                        pltpu.VMEM((gb, CHUNK_W, CHUNK_W), BF16),
                        pltpu.VMEM((gb, 2 * SSM_P, CHUNK_W), BF16),
                        pltpu.VMEM((gb, 2, CHUNK_W, 2 * SSM_P), BF16),
                        pltpu.VMEM((npair, 2, 1, 2 * SSM_P), F32),
                        pltpu.VMEM((2, (CHUNK + 1) * SSM_H, SSM_P), F32),
                        pltpu.VMEM((CHUNK_W, CHUNK_W), F32)],
        compiler_params=pltpu.CompilerParams(dimension_semantics=("arbitrary",), vmem_limit_bytes=VMEM_LIMIT),
        name="s5_chunks",
    )(xs, xsmeta, *mats, pairs(state_re), pairs(state_im))
    unpair = lambda a: jnp.transpose(a, (1, 0, 2)).reshape(a.shape[1], SSM_G, SSM_P)[None]
    return y, unpair(pfre), unpair(pfim), unpair(sfre), unpair(sfim)


def _mixers_body(n_p_tiles, tiles_per_seq, xb_ref, z_ref, y5_ref, xs_ref, sga_ref, sgb_ref,
                 zmeta_ref, cstate_ref, cw_ref, dskip_ref,
                 wc_hbm, wg_hbm, wso_hbm, merged_out,
                 wc, wg, wso, stage_a, stage_b, sem, carry):
    i = pl.program_id(0)

    @pl.when(i == 0)
    def _():
        _load_weight_bf16(wc_hbm, wc, stage_a, sem)
        _load_weight_bf16(wg_hbm, wg, stage_b, sem)
        _load_weight_bf16(wso_hbm, wso, stage_a, sem)

    @pl.when(jnp.logical_and(i < n_p_tiles, i % tiles_per_seq == 0))
    def _():
        carry[0:2, :] = zmeta_ref[N_META - 2:N_META, :]

    z = z_ref[...]
    row = lax.broadcasted_iota(I32, (TM, 1), 0)
    is_s = i >= n_p_tiles
    r1 = pltpu.roll(z, 1, 0)
    r2 = pltpu.roll(z, 2, 0)
    c1 = carry[1:2, :]
    c2 = carry[0:1, :]
    pos = jnp.where(is_s, row & (CHUNK - 1), row)
    first1 = pos == 0
    first2 = pos < 2
    seqs = TM // CHUNK
    rr = lax.broadcasted_iota(I32, (TM, seqs), 0)
    cc = lax.broadcasted_iota(I32, (TM, seqs), 1) * CHUNK
    at0 = jnp.where(rr == cc, 1.0, 0.0).astype(BF16)
    at1 = jnp.where(rr == cc + 1, 1.0, 0.0).astype(BF16)

    def place(sel, rows):
        r_hi, r_lo = _split_bf16(rows)
        return jnp.dot(sel, r_hi, preferred_element_type=F32) + jnp.dot(sel, r_lo, preferred_element_type=F32)

    old, new = cstate_ref[:, 0:D_CONV], cstate_ref[:, D_CONV:2 * D_CONV]
    fill1 = jnp.where(is_s, place(at0, new), jnp.broadcast_to(c1, z.shape))
    fill2 = jnp.where(is_s, place(at0, old) + place(at1, new), jnp.where(row == 0, c2, c1))
    zp1 = jnp.where(first1, fill1, r1)
    zp2 = jnp.where(first2, fill2, r2)
    carry[0:2, :] = z[TM - 2:TM, :]

    cw = cw_ref[...]
    conv = cw[0:1, :] * zp2 + cw[1:2, :] * zp1 + cw[2:3, :] * z
    a_in = (xb_ref[...].astype(F32) * conv).astype(BF16)
    ya = jnp.dot(a_in, wc[...], preferred_element_type=F32)

    ys = y5_ref[...] + dskip_ref[...] * xs_ref[...]
    ys = _gelu_tanh(ys)
    glu = jnp.dot(ys.astype(BF16), wg[...], preferred_element_type=F32)
    ys = ys * _sigmoid(glu)
    yb = jnp.dot(ys.astype(BF16), wso[...], preferred_element_type=F32)

    merged = sga_ref[...].astype(F32) * ya + sgb_ref[...].astype(F32) * yb
    merged_out[...] = merged.astype(BF16)


def _mixers(xb, z, y5, xs, sga, sgb, zmeta, conv_state, conv_w, d_skip, w_conv_out, w_glu, w_ssm_out,
            n_p_tiles, tiles_per_seq):
    t = xb.shape[0]
    seqs = TM // CHUNK
    n_s_tiles = conv_state.shape[0] // seqs
    row = lambda w: pl.BlockSpec((TM, w), lambda i: (i, 0))
    const = lambda r, w: pl.BlockSpec((r, w), lambda i: (0, 0))
    cstate = pl.BlockSpec((seqs, 2 * D_CONV), lambda i: (jnp.clip(i - n_p_tiles, 0, n_s_tiles - 1), 0))
    any_spec = pl.BlockSpec(memory_space=pl.ANY)
    return pl.pallas_call(
        functools.partial(_mixers_body, n_p_tiles, tiles_per_seq),
        grid=(t // TM,),
        in_specs=[row(D_CONV), row(D_CONV), row(D_SSM), row(D_SSM), row(D_MODEL), row(D_MODEL),
                  const(N_META, D_CONV), cstate, const(3, D_CONV), const(1, D_SSM),
                  any_spec, any_spec, any_spec],
        out_specs=row(D_MODEL),
        out_shape=jax.ShapeDtypeStruct((t, D_MODEL), BF16),
        scratch_shapes=[pltpu.VMEM((D_CONV, D_MODEL), BF16),
                        pltpu.VMEM((D_SSM, D_SSM), BF16),
                        pltpu.VMEM((D_SSM, D_MODEL), BF16),
                        pltpu.VMEM((2, 256, D_MODEL), F32),
                        pltpu.VMEM((2, 256, D_SSM), F32),
                        pltpu.SemaphoreType.DMA((2,)),
                        pltpu.VMEM((8, D_CONV), F32)],
        compiler_params=pltpu.CompilerParams(dimension_semantics=("arbitrary",), vmem_limit_bytes=VMEM_LIMIT),
        name="mixers",
    )(xb, z, y5, xs, sga, sgb, zmeta, conv_state, conv_w, d_skip, w_conv_out, w_glu, w_ssm_out)


ROUTE_W = LANES
COARSE0 = N_EXPERTS


def _route(logits, cnt):
    col = lax.broadcasted_iota(I32, logits.shape, 1)
    colf = col.astype(F32)
    neg = jnp.float32(-jnp.inf)
    big = jnp.float32(1 << 20)
    is_c = jnp.logical_and(col >= COARSE0, col < COARSE0 + N_EGROUPS)
    lc = jnp.where(is_c, logits, neg)
    cmax = jnp.max(lc, axis=-1, keepdims=True)
    gi = jnp.min(jnp.where(lc == cmax, colf - COARSE0, big), axis=-1, keepdims=True)
    pg = 1.0 / jnp.sum(jnp.where(is_c, jnp.exp(lc - cmax), 0.0), axis=-1, keepdims=True)
    grp = (col >> 3).astype(F32)
    in_g = jnp.logical_and(col < N_EXPERTS, grp == gi)
    lf = jnp.where(in_g, logits, neg)
    m1 = jnp.max(lf, axis=-1, keepdims=True)
    i1 = jnp.min(jnp.where(lf == m1, colf, big), axis=-1, keepdims=True)
    lf2 = jnp.where(colf == i1, neg, lf)
    m2 = jnp.max(lf2, axis=-1, keepdims=True)
    i2 = jnp.min(jnp.where(lf2 == m2, colf, big), axis=-1, keepdims=True)
    e2 = jnp.exp(m2 - m1)
    w1 = pg / (1.0 + e2)
    w2 = pg * e2 / (1.0 + e2)
    n = logits.shape[0]
    hit1 = colf == i1
    hit2 = colf == i2
    onehot = jnp.where(jnp.logical_or(hit1, hit2), 1.0, 0.0)
    rr = lax.broadcasted_iota(I32, (n, n), 0)
    cc = lax.broadcasted_iota(I32, (n, n), 1)
    tri = jnp.where(cc < rr, 1.0, 0.0).astype(BF16)
    pos = jnp.dot(tri, onehot.astype(BF16), preferred_element_type=F32) + cnt
    rank1 = jnp.sum(jnp.where(hit1, pos, 0.0), axis=-1, keepdims=True)
    rank2 = jnp.sum(jnp.where(hit2, pos, 0.0), axis=-1, keepdims=True)
    vals = (i1, i2, w1, w2, rank1, rank2)
    rec = jnp.zeros(logits.shape, F32)
    for c, val in enumerate(vals):
        rec = jnp.where(col == c, val, rec)
    return rec, cnt + jnp.sum(onehot, axis=0, keepdims=True)


def _out_proj_body(n_p_tiles, merged_ref, xp_ref, xsm_ref, g2_ref, wr_ref, wo_hbm,
                   h2_out, v_out, rec_out, cnt_out, wo, stage, sem, h2_scr):
    i = pl.program_id(0)
    n = pl.num_programs(0) - 1

    def route_prev():
        v = _rmsnorm(h2_scr[...], g2_ref[...])
        v_out[...] = _pack_halves(v)
        v_hi, v_lo = _split_bf16(v)
        both = jnp.dot(v_hi, wr_ref[...], preferred_element_type=F32)
        logits = (both[:, :ROUTE_W] + both[:, ROUTE_W:]
                  + jnp.dot(v_lo, wr_ref[:, :ROUTE_W], preferred_element_type=F32))
        rec, cnt = _route(logits, cnt_out[...])
        rec_out[...] = rec
        cnt_out[...] = cnt

    def project():
        x = jnp.where(i < n_p_tiles, xp_ref[...], xsm_ref[...])
        h2 = x + jnp.dot(merged_ref[...], wo[...], preferred_element_type=F32)
        h2_out[...] = h2
        return h2

    @pl.when(i == 0)
    def _():
        cnt_out[...] = jnp.zeros(cnt_out.shape, F32)
        cw = stage.shape[2]
        merged = merged_ref[...]

        def on_chunk(c):
            cols = slice(c * cw, (c + 1) * cw)
            h2 = xp_ref[:, cols] + jnp.dot(merged, wo[:, cols], preferred_element_type=F32)
            h2_out[:, cols] = h2
            h2_scr[:, cols] = h2

        _stream_weight_bf16(wo_hbm, wo, stage, sem, 0, on_chunk)

    @pl.when(jnp.logical_and(i > 0, i < n))
    def _():
        route_prev()
        h2_scr[...] = project()

    @pl.when(i == n)
    def _():
        route_prev()


def _out_proj(merged, xp, xsm, norm2, w_router, w_o):
    t_p, t_s = xp.shape[0], xsm.shape[0]
    n_p, n_s = t_p // TM, t_s // TM
    n = n_p + n_s
    t = t_p + t_s
    xp_spec, xs_spec = _two_stream_specs(n_p, n_s)
    cur = lambda w: pl.BlockSpec((TM, w), lambda i: (jnp.minimum(i, n - 1), 0))
    prev = lambda w: pl.BlockSpec((TM, w), lambda i: (jnp.maximum(i - 1, 0), 0))
    const = lambda r, w: pl.BlockSpec((r, w), lambda i: (0, 0))
    return pl.pallas_call(
        functools.partial(_out_proj_body, n_p),
        grid=(n + 1,),
        in_specs=[cur(D_MODEL), xp_spec, xs_spec, const(1, D_MODEL), const(D_MODEL, 2 * ROUTE_W),
                  pl.BlockSpec(memory_space=pl.ANY)],
        out_specs=[cur(D_MODEL), prev(HALF), prev(ROUTE_W), const(1, ROUTE_W)],
        out_shape=[jax.ShapeDtypeStruct((t, D_MODEL), F32),
                   jax.ShapeDtypeStruct((t, HALF), U32),
                   jax.ShapeDtypeStruct((t, ROUTE_W), F32),
                   jax.ShapeDtypeStruct((1, ROUTE_W), F32)],
        scratch_shapes=[pltpu.VMEM((D_MODEL, D_MODEL), BF16),
                        pltpu.VMEM((2, D_MODEL, WEIGHT_CHUNK), F32),
                        pltpu.SemaphoreType.DMA((2,)),
                        pltpu.VMEM((TM, D_MODEL), F32)],
        compiler_params=pltpu.CompilerParams(dimension_semantics=("arbitrary",), vmem_limit_bytes=VMEM_LIMIT),
        name="out_proj_route",
    )(merged, xp, xsm, norm2, w_router, w_o)


def _moe_plan(rec, cnt, n_tiles):
    t = rec.shape[0]
    n_pairs = 2 * t
    eid = rec[:, 0:2].astype(I32).reshape(-1)
    rank = rec[:, 4:6].astype(I32).reshape(-1)
    counts = cnt[0, :N_EXPERTS].astype(I32)
    pair_start = jnp.cumsum(counts) - counts
    experts = jnp.arange(N_EXPERTS, dtype=I32)
    onehot = (eid[:, None] == experts[None, :]).astype(I32)
    pos = rank + jnp.sum(onehot * pair_start[None, :], axis=1)
    _, order = lax.sort((pos, jnp.arange(n_pairs, dtype=I32)), num_keys=1)
    tiles_e = (counts + TM_MOE - 1) // TM_MOE
    tile_end = jnp.cumsum(tiles_e)
    tile_start = tile_end - tiles_e
    n_valid = tile_end[-1]
    tile_ids = jnp.arange(n_tiles, dtype=I32)
    tile_e = jnp.sum((tile_ids[:, None] >= tile_end[None, :]).astype(I32), axis=1)
    last_e = jnp.sum((n_valid - 1 >= tile_end).astype(I32))
    tile_e = jnp.minimum(jnp.where(tile_ids < n_valid, tile_e, last_e), N_EXPERTS - 1)
    tile_onehot = (tile_e[:, None] == experts[None, :]).astype(I32)
    tile_q0 = (tile_ids - jnp.sum(tile_onehot * tile_start[None, :], axis=1)) * TM_MOE
    tile_q0 = jnp.where(tile_ids < n_valid, tile_q0, 0)
    dst = rank + jnp.sum(onehot * (tile_start * TM_MOE)[None, :], axis=1)
    pad_start = tile_start * TM_MOE + counts
    pad_len = tiles_e * TM_MOE - counts
    return (dst.astype(I32), pad_start.astype(I32), pad_len.astype(I32), tile_e, tile_q0.astype(I32),
            n_valid.astype(I32).reshape(1), pair_start, counts, order)


DISPATCH_BUFS = 3
ROW_PIECES = tuple(TM_MOE >> (b + 1) for b in range(TM_MOE.bit_length() - 1))


def _dispatch_body(dst_ref, pad_start_ref, pad_len_ref, nvalid_ref, v_hbm, x_hbm, buf, zbuf, rsem, ssem, zsem):
    i = pl.program_id(0)
    n = pl.num_programs(0)

    def read(tile, slot):
        return pltpu.make_async_copy(v_hbm.at[pl.ds(pl.multiple_of(tile * TM, TM), TM)], buf.at[slot],
                                     rsem.at[slot])

    def row_write(slot, r, dst_row):
        return pltpu.make_async_copy(buf.at[slot, pl.ds(r, 1)], x_hbm.at[pl.ds(dst_row, 1)], ssem.at[slot])

    def drain(slot):
        for _ in range(2):
            pltpu.make_async_copy(buf.at[slot], x_hbm.at[pl.ds(0, TM)], ssem.at[slot]).wait()

    def pad_fill(go):
        def zero_rows(start, size):
            d = pltpu.make_async_copy(zbuf.at[pl.ds(0, size)], x_hbm.at[pl.ds(start, size)], zsem)
            d.start() if go else d.wait()

        def body(e, c):
            start, length = pad_start_ref[e], pad_len_ref[e]
            head = (-start) & (SUBLANES - 1)
            for h in range(SUBLANES - 1):
                @pl.when(h < head)
                def _(h=h):
                    zero_rows(start + h, 1)
            start, length = start + head, length - head
            for size in ROW_PIECES:
                if size >= SUBLANES:
                    @pl.when((length & size) != 0)
                    def _(size=size):
                        zero_rows(pl.multiple_of(start + (length & (-2 * size)), SUBLANES), size)
            return c
        lax.fori_loop(0, N_EXPERTS, body, 0)

        def unused(tile, c):
            for half in range(TM_MOE // ROW_PIECES[0]):
                zero_rows(pl.multiple_of(tile * TM_MOE + half * ROW_PIECES[0], SUBLANES), ROW_PIECES[0])
            return c
        lax.fori_loop(nvalid_ref[0], x_hbm.shape[0] // TM_MOE, unused, 0)

    @pl.when(i == 0)
    def _():
        zbuf[...] = jnp.zeros(zbuf.shape, U32)
        pad_fill(True)
        read(0, 0).start()

    @pl.when(i >= 2)
    def _():
        drain((i + 1) % DISPATCH_BUFS)

    @pl.when(i + 1 < n)
    def _():
        read(i + 1, (i + 1) % DISPATCH_BUFS).start()

    slot = i % DISPATCH_BUFS
    read(i, slot).wait()
    for r in range(TM):
        for k in range(2):
            row_write(slot, r, dst_ref[2 * (i * TM + r) + k]).start(priority=k)

    @pl.when(i == n - 1)
    def _():
        if n >= 2:
            drain((i - 1) % DISPATCH_BUFS)
        drain(slot)
        pad_fill(False)


def _dispatch(v, plan, n_tiles):
    dst, pad_start, pad_len, n_valid = plan[0], plan[1], plan[2], plan[5]
    t = v.shape[0]
    return pl.pallas_call(
        _dispatch_body,
        grid_spec=pltpu.PrefetchScalarGridSpec(
            num_scalar_prefetch=4,
            grid=(t // TM,),
            in_specs=[pl.BlockSpec(memory_space=pl.ANY)],
            out_specs=pl.BlockSpec(memory_space=pl.ANY),
            scratch_shapes=[pltpu.VMEM((DISPATCH_BUFS, TM, HALF), U32),
                            pltpu.VMEM((ROW_PIECES[0], HALF), U32),
                            pltpu.SemaphoreType.DMA((DISPATCH_BUFS,)),
                            pltpu.SemaphoreType.DMA((DISPATCH_BUFS,)),
                            pltpu.SemaphoreType.DMA(())]),
        out_shape=jax.ShapeDtypeStruct((n_tiles * TM_MOE, HALF), U32),
        compiler_params=pltpu.CompilerParams(dimension_semantics=("arbitrary",), vmem_limit_bytes=VMEM_LIMIT),
        name="moe_dispatch",
    )(dst, pad_start, pad_len, n_valid, v)


WEIGHT_SLOTS = 3


def _moe_body(n_tok, plane, tile_e_ref, tile_q0_ref, nvalid_ref, pstart_ref, cnt_ref, orow_ref,
              krank_ref, elist_ref, nexp_ref,
              x_ref, wg_hbm, wu_hbm, wd_hbm, o_hbm, ybuf0, ybuf1, ybuf2, ssem,
              stage_g, stage_u, stage_d, wsem, wg, wu, wd):
    i = pl.program_id(0)
    nv = nvalid_ref[0]
    bufs = (ybuf0, ybuf1, ybuf2)

    def scratch_row0(slot):
        return (slot & 1) * plane + n_tok + (slot >> 1) * TM_MOE

    def row_write(slot, r, dst_row):
        return pltpu.make_async_copy(bufs[slot].at[pl.ds(r, 1)], o_hbm.at[pl.ds(dst_row, 1)], ssem.at[slot])

    def scratch_rows(region):
        return pltpu.make_async_copy(ybuf0, o_hbm.at[pl.ds(scratch_row0(region), TM_MOE)], ssem.at[0])

    def start_writes(tile, slot, rows):
        e = tile_e_ref[tile]
        valid = cnt_ref[e] - tile_q0_ref[tile]
        first = pstart_ref[e] + tile_q0_ref[tile]
        for r in rows:
            row_write(slot, r, jnp.where(r < valid, orow_ref[first + r], scratch_row0(slot) + r)).start(
                priority=r % 2)

    def compute(slot, writes=None):
        quarter = TM_MOE // 4
        batch = lambda q: start_writes(*writes, range(q * quarter, (q + 1) * quarter)) if writes else None
        x_lo, x_hi = (h.astype(BF16) for h in _unpack_halves(x_ref[...]))
        batch(0)
        hg = (jnp.dot(x_lo, wg[0:HALF, :], preferred_element_type=F32)
              + jnp.dot(x_hi, wg[HALF:D_MODEL, :], preferred_element_type=F32))
        batch(1)
        hu = (jnp.dot(x_lo, wu[0:HALF, :], preferred_element_type=F32)
              + jnp.dot(x_hi, wu[HALF:D_MODEL, :], preferred_element_type=F32))
        batch(2)
        act = hg * _sigmoid(hg) * hu
        y = jnp.dot(act.astype(BF16), wd[...], preferred_element_type=F32)
        batch(3)
        bufs[slot][...] = _pack_halves(y)

    @pl.when(i == 0)
    def _():
        ybuf0[...] = jnp.zeros(ybuf0.shape, U32)
        for region in range(4):
            scratch_rows(region).start()
        for region in range(4):
            scratch_rows(region).wait()

    @pl.when(jnp.logical_and(i >= 3, i < nv + 3))
    def _():
        pltpu.make_async_copy(ybuf0, o_hbm.at[pl.ds(0, TM_MOE)], ssem.at[i % 3]).wait()

    def weight_copies(k, go):
        e = elist_ref[k]
        slot = k % WEIGHT_SLOTS
        for w_hbm, st in ((wg_hbm, stage_g), (wu_hbm, stage_u), (wd_hbm, stage_d)):
            d = pltpu.make_async_copy(w_hbm.at[e], st.at[slot], wsem.at[slot])
            d.start() if go else d.wait()

    @pl.when(i == 0)
    def _():
        for k in range(WEIGHT_SLOTS):
            @pl.when(k < nexp_ref[0])
            def _(k=k):
                weight_copies(k, True)

    @pl.when(i < nv)
    def _():
        prev_e = tile_e_ref[jnp.maximum(i - 1, 0)]

        @pl.when(jnp.logical_or(i == 0, tile_e_ref[i] != prev_e))
        def _():
            k = krank_ref[tile_e_ref[i]]
            slot = k % WEIGHT_SLOTS
            weight_copies(k, False)
            wg[...] = stage_g[slot].astype(BF16)
            wu[...] = stage_u[slot].astype(BF16)
            wd[...] = stage_d[slot].astype(BF16)

            @pl.when(k + WEIGHT_SLOTS < nexp_ref[0])
            def _():
                weight_copies(k + WEIGHT_SLOTS, True)

    @pl.when(i == 0)
    def _():
        compute(0)

    for slot in range(3):
        prev = (slot + 2) % 3

        @pl.when(jnp.logical_and(i % 3 == slot, jnp.logical_and(i >= 1, i < nv)))
        def _(slot=slot, prev=prev):
            compute(slot, writes=(i - 1, prev))

        @pl.when(jnp.logical_and(i % 3 == slot, i == nv))
        def _(prev=prev):
            start_writes(i - 1, prev, range(TM_MOE))


def _moe(x_disp, plan, w_gate, w_up, w_down, n_tiles, n_tok):
    tile_e, tile_q0, n_valid, pair_start, counts, order = plan[3:9]
    plane = n_tok + 2 * TM_MOE
    orow = jnp.pad((order & 1) * plane + (order >> 1), (0, TM_MOE))
    present = (counts > 0).astype(I32)
    krank = jnp.cumsum(present) - present
    experts = jnp.arange(N_EXPERTS, dtype=I32)
    elist = jnp.sum(jnp.where((krank[None, :] == experts[:, None]) & (present[None, :] > 0), experts[None, :], 0),
                    axis=1).astype(I32)
    nexp = jnp.sum(present).astype(I32).reshape(1)
    tile = lambda i, nv: jnp.minimum(i, jnp.maximum(nv[0] - 1, 0))
    any_spec = pl.BlockSpec(memory_space=pl.ANY)
    ybuf = pltpu.VMEM((TM_MOE, HALF), U32)
    return pl.pallas_call(
        functools.partial(_moe_body, n_tok, plane),
        grid_spec=pltpu.PrefetchScalarGridSpec(
            num_scalar_prefetch=9,
            grid=(n_tiles + 3,),
            in_specs=[pl.BlockSpec((TM_MOE, HALF), lambda i, te, tq, nv, *_: (tile(i, nv), 0)),
                      any_spec, any_spec, any_spec],
            out_specs=pl.BlockSpec(memory_space=pl.ANY),
            scratch_shapes=[ybuf, ybuf, ybuf,
                            pltpu.SemaphoreType.DMA((3,)),
                            pltpu.VMEM((WEIGHT_SLOTS, D_MODEL, D_EXPERT), F32),
                            pltpu.VMEM((WEIGHT_SLOTS, D_MODEL, D_EXPERT), F32),
                            pltpu.VMEM((WEIGHT_SLOTS, D_EXPERT, D_MODEL), F32),
                            pltpu.SemaphoreType.DMA((WEIGHT_SLOTS,)),
                            pltpu.VMEM((D_MODEL, D_EXPERT), BF16),
                            pltpu.VMEM((D_MODEL, D_EXPERT), BF16),
                            pltpu.VMEM((D_EXPERT, D_MODEL), BF16)]),
        out_shape=jax.ShapeDtypeStruct((2 * plane, HALF), U32),
        compiler_params=pltpu.CompilerParams(dimension_semantics=("arbitrary",), vmem_limit_bytes=VMEM_LIMIT),
        name="moe_experts",
    )(tile_e, tile_q0, n_valid, pair_start, counts, orow, krank.astype(I32), elist, nexp,
      x_disp, w_gate, w_up, w_down)


def _combine_body(n_p_tiles, h2_ref, rec_ref, gf_ref, y1_ref, y2_ref, outp_ref, outs_ref):
    i = pl.program_id(0)
    rec = rec_ref[...]
    y1 = jnp.concatenate(_unpack_halves(y1_ref[...]), axis=-1)
    y2 = jnp.concatenate(_unpack_halves(y2_ref[...]), axis=-1)
    h = rec[:, 2:3] * y1 + rec[:, 3:4] * y2
    out = _rmsnorm(h2_ref[...] + h, gf_ref[...])

    @pl.when(i < n_p_tiles)
    def _():
        outp_ref[...] = out

    @pl.when(i >= n_p_tiles)
    def _():
        outs_ref[...] = out


def _combine(h2, rec, y_pairs, final_norm, t_p, t_s):
    n_p, n_s = t_p // TM, t_s // TM
    plane_tiles = y_pairs.shape[0] // 2 // TM
    return pl.pallas_call(
        functools.partial(_combine_body, n_p),
        grid=(n_p + n_s,),
        in_specs=[pl.BlockSpec((TM, D_MODEL), lambda i: (i, 0)),
                  pl.BlockSpec((TM, ROUTE_W), lambda i: (i, 0)),
                  pl.BlockSpec((1, D_MODEL), lambda i: (0, 0)),
                  pl.BlockSpec((TM, HALF), lambda i: (i, 0)),
                  pl.BlockSpec((TM, HALF), lambda i: (i + plane_tiles, 0))],
        out_specs=[pl.BlockSpec((TM, D_MODEL), lambda i: (jnp.minimum(i, n_p - 1), 0)),
                   pl.BlockSpec((TM, D_MODEL), lambda i: (jnp.clip(i - n_p, 0, n_s - 1), 0))],
        out_shape=[jax.ShapeDtypeStruct((t_p, D_MODEL), F32),
                   jax.ShapeDtypeStruct((t_s, D_MODEL), F32)],
        compiler_params=pltpu.CompilerParams(dimension_semantics=("arbitrary",), vmem_limit_bytes=VMEM_LIMIT),
        name="combine_norm",
    )(h2, rec, final_norm, y_pairs, y_pairs)


def kernel(x_prompt, x_sample, state_conv, state_ssm_re, state_ssm_im, meta_tokens, norm1, w_in, conv_w,
           lam_re, lam_im, log_dt, ssm_b_re, ssm_b_im, ssm_c_re, ssm_c_im, ssm_d, w_glu, w_conv_out,
           w_ssm_out, w_o, norm2, w_coarse, w_fine, w_gate, w_up, w_down, final_norm):
    n_pb, seq, _ = x_prompt.shape
    n_sb, dec_seq, _ = x_sample.shape
    assert dec_seq == CHUNK and seq % TM == 0 and (n_sb * dec_seq) % TM == 0 and N_META == CHUNK
    t_p, t_s = n_pb * seq, n_sb * dec_seq
    xp = x_prompt.reshape(t_p, D_MODEL)
    xsm = x_sample.reshape(t_s, D_MODEL)

    xb, z, xs, zmeta, xsmeta, sga, sgb = _in_proj(xp, xsm, meta_tokens, norm1, w_in[0])

    mats = _s5_chunk_mats(lam_re[0], lam_im[0], log_dt[0], ssm_b_re[0], ssm_b_im[0], ssm_c_re[0], ssm_c_im[0])
    y5, pf_re, pf_im, sf_re, sf_im = _s5(xs, xsmeta, state_ssm_re[0], state_ssm_im[0], mats, n_pb, n_sb, seq)

    merged = _mixers(xb, z, y5, xs, sga, sgb, zmeta, state_conv[0].reshape(n_sb, 2 * D_CONV), conv_w[0], ssm_d,
                     w_conv_out[0], w_glu[0], w_ssm_out[0], t_p // TM, seq // TM)

    w_router = jnp.concatenate(
        [w_fine[0], w_coarse[0], jnp.zeros((D_MODEL, ROUTE_W - N_EXPERTS - N_EGROUPS), F32)], axis=1)
    w_router = jnp.concatenate(_split_bf16(w_router), axis=1)
    h2, v, rec, cnt = _out_proj(merged, xp, xsm, norm2, w_router, w_o[0])

    n_tiles = 2 * (t_p + t_s) // TM_MOE + N_EXPERTS
    plan = _moe_plan(rec, cnt, n_tiles)
    x_disp = _dispatch(v, plan, n_tiles)
    y_pairs = _moe(x_disp, plan, w_gate[0], w_up[0], w_down[0], n_tiles, t_p + t_s)
    y_p, y_s = _combine(h2, rec, y_pairs, final_norm.reshape(1, D_MODEL), t_p, t_s)

    new_conv_p = jnp.stack([z[(b + 1) * seq - 2:(b + 1) * seq] for b in range(n_pb)])
    new_conv_s = z[t_p:].reshape(n_sb, dec_seq, D_CONV)[:, dec_seq - 2:]
    return (y_p.reshape(n_pb, seq, D_MODEL), y_s.reshape(n_sb, dec_seq, D_MODEL),
            new_conv_p[None], pf_re, pf_im, new_conv_s[None], sf_re, sf_im)
```

```python
import functools

import jax
import jax.numpy as jnp
from jax import lax
from jax.experimental import pallas as pl
from jax.experimental.pallas import tpu as pltpu

F32 = jnp.float32
BF16 = jnp.bfloat16
I32 = jnp.int32
U32 = jnp.uint32

D_MODEL = 2048
D_CONV = 1024
D_SSM = 1024
SSM_H = 16
SSM_G = 64
SSM_P = 64
N_META = 16
N_EGROUPS = 4
EXPERTS_PER_GROUP = 8
N_EXPERTS = 32
D_EXPERT = 256
EPS = 1e-6

LANES = 128
SUBLANES = 8

CHUNK = 16
CHUNK_W = CHUNK * SSM_H

TM = 256
TM_MOE = 256
VMEM_LIMIT = 52 * 1024 * 1024


def _rmsnorm(x, g):
    return x * lax.rsqrt(jnp.mean(x * x, axis=-1, keepdims=True) + EPS) * g


def _sigmoid(x):
    return 1.0 / (1.0 + jnp.exp(-x))


def _gelu_tanh(x):
    c = 0.7978845608028654
    return 0.5 * x * (1.0 + jnp.tanh(c * (x + 0.044715 * (x * x * x))))


def _split_bf16(a):
    hi = a.astype(BF16)
    lo = (a - hi.astype(F32)).astype(BF16)
    return hi, lo


def _dot3(a, b):
    a_hi, a_lo = _split_bf16(a)
    b_hi, b_lo = _split_bf16(b)
    return (jnp.dot(a_hi, b_hi, preferred_element_type=F32) + jnp.dot(a_lo, b_hi, preferred_element_type=F32)
            + jnp.dot(a_hi, b_lo, preferred_element_type=F32))


HALF = D_MODEL // 2


def _pack_halves(a):
    return pltpu.pack_elementwise([a[:, :HALF], a[:, HALF:]], packed_dtype=BF16)


def _unpack_halves(p):
    return (pltpu.unpack_elementwise(p, index=0, packed_dtype=BF16, unpacked_dtype=F32),
            pltpu.unpack_elementwise(p, index=1, packed_dtype=BF16, unpacked_dtype=F32))


def _weight_copy(w_hbm, stage, sem, c, slot, rows, col0, ncols):
    return pltpu.make_async_copy(
        w_hbm.at[pl.ds(c * rows, rows), pl.ds(col0, ncols)], stage.at[slot], sem.at[slot])


def _load_weight_bf16(w_hbm, w_vmem, stage, sem, col0=0):
    k, n = w_vmem.shape
    rows = stage.shape[1]
    nchunk = k // rows
    _weight_copy(w_hbm, stage, sem, 0, 0, rows, col0, n).start()
    for c in range(nchunk):
        slot = c % 2
        if c + 1 < nchunk:
            _weight_copy(w_hbm, stage, sem, c + 1, 1 - slot, rows, col0, n).start()
        _weight_copy(w_hbm, stage, sem, c, slot, rows, col0, n).wait()
        w_vmem[pl.ds(c * rows, rows), :] = stage[slot].astype(BF16)


WEIGHT_CHUNK = 512


def _stream_weight_bf16(w_hbm, w_vmem, stage, sem, col0, on_chunk):
    _, n = w_vmem.shape
    w = stage.shape[2]
    nchunk = n // w

    def copy(c, slot):
        return pltpu.make_async_copy(w_hbm.at[:, pl.ds(col0 + c * w, w)], stage.at[slot], sem.at[slot])

    copy(0, 0).start()
    for c in range(nchunk):
        slot = c % 2
        if c + 1 < nchunk:
            copy(c + 1, 1 - slot).start()
        copy(c, slot).wait()
        w_vmem[:, c * w:(c + 1) * w] = stage[slot].astype(BF16)
        on_chunk(c)


def _in_proj_mix_body(n_p_tiles, xp_ref, xsm_ref, meta_ref, g_ref, w_hbm,
                      xb_out, z_out, xs_out, zmeta_out, xsmeta_out,
                      w_vmem, stage, sem):
    i = pl.program_id(0)
    g = g_ref[...]
    cw = stage.shape[2]

    @pl.when(i == 0)
    def _():
        u = _rmsnorm(xp_ref[...], g).astype(BF16)
        um = _rmsnorm(meta_ref[...], g).astype(BF16)

        def on_chunk(c):
            n0 = c * cw
            cols = slice(n0 % D_CONV, n0 % D_CONV + cw)
            p = jnp.dot(u, w_vmem[:, n0:n0 + cw], preferred_element_type=F32)
            pm = jnp.dot(um, w_vmem[:, n0:n0 + cw], preferred_element_type=F32)
            if n0 < D_CONV:
                xb_out[:, cols] = p.astype(BF16)
            elif n0 < 2 * D_CONV:
                z_out[:, cols] = p
                zmeta_out[:, cols] = pm
            elif n0 < 3 * D_CONV:
                z_out[:, cols] = z_out[:, cols] * p
                zmeta_out[:, cols] = zmeta_out[:, cols] * pm
            else:
                xs_out[:, cols] = p
                xsmeta_out[:, cols] = pm

        _stream_weight_bf16(w_hbm, w_vmem, stage, sem, 0, on_chunk)

    @pl.when(i > 0)
    def _():
        x = jnp.where(i < n_p_tiles, xp_ref[...], xsm_ref[...])
        u = _rmsnorm(x, g).astype(BF16)
        xb = jnp.dot(u, w_vmem[:, 0:D_CONV], preferred_element_type=F32)
        xc = jnp.dot(u, w_vmem[:, D_CONV:2 * D_CONV], preferred_element_type=F32)
        xv = jnp.dot(u, w_vmem[:, 2 * D_CONV:3 * D_CONV], preferred_element_type=F32)
        xs = jnp.dot(u, w_vmem[:, 3 * D_CONV:3 * D_CONV + D_SSM], preferred_element_type=F32)
        xb_out[...] = xb.astype(BF16)
        z_out[...] = xc * xv
        xs_out[...] = xs


def _in_proj_gate_body(n_p_tiles, xp_ref, xsm_ref, g_ref, w_hbm, ga_out, gb_out,
                       w_vmem, stage, sem):
    i = pl.program_id(0)
    cw = stage.shape[2]

    @pl.when(i == 0)
    def _():
        u = _rmsnorm(xp_ref[...], g_ref[...]).astype(BF16)

        def on_chunk(c):
            n0 = c * cw
            out = ga_out if n0 < D_MODEL else gb_out
            gate = _sigmoid(jnp.dot(u, w_vmem[:, n0:n0 + cw], preferred_element_type=F32))
            out[:, n0 % D_MODEL:n0 % D_MODEL + cw] = gate.astype(BF16)

        _stream_weight_bf16(w_hbm, w_vmem, stage, sem, 3 * D_CONV + D_SSM, on_chunk)

    @pl.when(i > 0)
    def _():
        x = jnp.where(i < n_p_tiles, xp_ref[...], xsm_ref[...])
        u = _rmsnorm(x, g_ref[...]).astype(BF16)
        gates = jnp.dot(u, w_vmem[...], preferred_element_type=F32)
        ga_out[...] = _sigmoid(gates[:, 0:D_MODEL]).astype(BF16)
        gb_out[...] = _sigmoid(gates[:, D_MODEL:2 * D_MODEL]).astype(BF16)


def _two_stream_specs(n_p_tiles, n_s_tiles, tm=TM):
    xp_spec = pl.BlockSpec((tm, D_MODEL), lambda i: (jnp.minimum(i, n_p_tiles - 1), 0))
    mode = pl.Buffered(1) if n_s_tiles == 1 else None
    xs_spec = pl.BlockSpec((tm, D_MODEL), lambda i: (jnp.clip(i - n_p_tiles, 0, n_s_tiles - 1), 0),
                           pipeline_mode=mode)
    return xp_spec, xs_spec


TM_IN = TM


def _in_proj(xp, xsm, meta, norm1, w_in):
    t_p, t_s = xp.shape[0], xsm.shape[0]
    n_p, n_s = t_p // TM_IN, t_s // TM_IN
    t = t_p + t_s
    half = 3 * D_CONV + D_SSM
    xp_spec, xs_spec = _two_stream_specs(n_p, n_s, TM_IN)
    g_spec = pl.BlockSpec((1, D_MODEL), lambda i: (0, 0))
    any_spec = pl.BlockSpec(memory_space=pl.ANY)
    row = lambda w: pl.BlockSpec((TM_IN, w), lambda i: (i, 0))
    const = lambda r, w: pl.BlockSpec((r, w), lambda i: (0, 0))
    params = pltpu.CompilerParams(dimension_semantics=("arbitrary",), vmem_limit_bytes=VMEM_LIMIT)
    scratch = [pltpu.VMEM((D_MODEL, half), BF16),
               pltpu.VMEM((2, D_MODEL, WEIGHT_CHUNK), F32),
               pltpu.SemaphoreType.DMA((2,))]

    xb, z, xs, zmeta, xsmeta = pl.pallas_call(
        functools.partial(_in_proj_mix_body, n_p),
        grid=(n_p + n_s,),
        in_specs=[xp_spec, xs_spec, const(N_META, D_MODEL), g_spec, any_spec],
        out_specs=[row(D_CONV), row(D_CONV), row(D_SSM), const(N_META, D_CONV), const(N_META, D_SSM)],
        out_shape=[jax.ShapeDtypeStruct((t, D_CONV), BF16),
                   jax.ShapeDtypeStruct((t, D_CONV), F32),
                   jax.ShapeDtypeStruct((t, D_SSM), F32),
                   jax.ShapeDtypeStruct((N_META, D_CONV), F32),
                   jax.ShapeDtypeStruct((N_META, D_SSM), F32)],
        scratch_shapes=scratch,
        compiler_params=params,
        name="in_proj_mix",
    )(xp, xsm, meta, norm1, w_in)

    sga, sgb = pl.pallas_call(
        functools.partial(_in_proj_gate_body, n_p),
        grid=(n_p + n_s,),
        in_specs=[xp_spec, xs_spec, g_spec, any_spec],
        out_specs=[row(D_MODEL), row(D_MODEL)],
        out_shape=[jax.ShapeDtypeStruct((t, D_MODEL), BF16),
                   jax.ShapeDtypeStruct((t, D_MODEL), BF16)],
        scratch_shapes=scratch,
        compiler_params=params,
        name="in_proj_gate",
    )(xp, xsm, norm1, w_in)
    return xb, z, xs, zmeta, xsmeta, sga, sgb


S5_GROUPS_PER_STEP = LANES // SSM_H
S5_PAIRS_PER_STEP = S5_GROUPS_PER_STEP // 2


def _s5_chunk_mats(lam_re, lam_im, log_dt, b_re, b_im, c_re, c_im):
    dt = jnp.exp(log_dt)[:, None]
    lr, li = lam_re, lam_im
    mag = jnp.exp(lr * dt)
    ab_re, ab_im = mag * jnp.cos(li * dt), mag * jnp.sin(li * dt)
    nr, ni = ab_re - 1.0, ab_im
    den = lr * lr + li * li
    k_re = (nr * lr + ni * li) / den
    k_im = (ni * lr - nr * li) / den
    bb_re = k_re[..., None] * b_re - k_im[..., None] * b_im
    bb_im = k_re[..., None] * b_im + k_im[..., None] * b_re
    return lr * dt, li * dt, bb_re, bb_im, c_re, c_im


def _cmul(ar, ai, br, bi):
    return ar * br - ai * bi, ar * bi + ai * br


def _s5_operators(pair, parity, params, toe_scr, p_scr, q_scr, a16_scr, g_scr, m_scr):
    k = 2 * pair + parity
    zr_ref, zi_ref, bbr_ref, bbi_ref, cr_ref, ci_ref = params
    zr, zi = zr_ref[k:k + 1, :], zi_ref[k:k + 1, :]
    mag = jnp.exp(zr)
    ar, ai = mag * jnp.cos(zi), mag * jnp.sin(zi)
    eye = lax.broadcasted_iota(I32, (SSM_P, SSM_P), 0) == lax.broadcasted_iota(I32, (SSM_P, SSM_P), 1)
    acr = jnp.sum(jnp.where(eye, ar, 0.0), axis=1, keepdims=True)
    aci = jnp.sum(jnp.where(eye, ai, 0.0), axis=1, keepdims=True)
    c_re, c_im = cr_ref[k], ci_ref[k]
    bb_re, bb_im = bbr_ref[k], bbi_ref[k]

    blk = lax.broadcasted_iota(I32, (1, CHUNK_W), 1) >> 4

    pr, pi = jnp.ones_like(ar), jnp.zeros_like(ar)
    pcr, pci = jnp.ones_like(acr), jnp.zeros_like(acr)
    pw_re = jnp.zeros((SSM_P, CHUNK_W), F32)
    pw_im = jnp.zeros((SSM_P, CHUNK_W), F32)
    for d in range(CHUNK + 1):
        g_scr[0, d * SSM_H:(d + 1) * SSM_H, :] = c_re * pr - c_im * pi
        g_scr[1, d * SSM_H:(d + 1) * SSM_H, :] = c_re * pi + c_im * pr
        if d < CHUNK:
            pw_re = jnp.where(blk == CHUNK - 1 - d, pcr, pw_re)
            pw_im = jnp.where(blk == CHUNK - 1 - d, pci, pw_im)
            pcr, pci = _cmul(pcr, pci, acr, aci)
            pr, pi = _cmul(pr, pi, ar, ai)

    half = parity * SSM_P
    a16_scr[pair, 0, :, half:half + SSM_P] = pr
    a16_scr[pair, 1, :, half:half + SSM_P] = pi
    q_scr[k] = jnp.zeros(q_scr.shape[1:], BF16)
    q_scr[k, 0, :, half:half + SSM_P] = g_scr[0, SSM_H:, :].astype(BF16)
    q_scr[k, 1, :, half:half + SSM_P] = (-g_scr[1, SSM_H:, :]).astype(BF16)

    rep = jnp.where(lax.broadcasted_iota(I32, (SSM_H, CHUNK_W), 0)
                    == (lax.broadcasted_iota(I32, (SSM_H, CHUNK_W), 1) & (SSM_H - 1)), 1.0, 0.0).astype(BF16)

    def widen(b):
        b_hi, b_lo = _split_bf16(b)
        return (jnp.dot(b_hi, rep, preferred_element_type=F32) + jnp.dot(b_lo, rep, preferred_element_type=F32))

    bw_re, bw_im = widen(bb_re), widen(bb_im)
    p_re, p_im = _cmul(pw_re, pw_im, bw_re, bw_im)
    p_scr[k, 0:SSM_P, :] = p_re.astype(BF16)
    p_scr[k, SSM_P:2 * SSM_P, :] = p_im.astype(BF16)

    m_scr[...] = _dot3(g_scr[0, 0:CHUNK_W, :], bw_re) - _dot3(g_scr[1, 0:CHUNK_W, :], bw_im)
    for t in range(CHUNK):
        acc = jnp.where(blk == 0, m_scr[t * SSM_H:(t + 1) * SSM_H, :], 0.0)
        for j in range(1, t + 1):
            acc = jnp.where(blk == j, m_scr[(t - j) * SSM_H:(t - j + 1) * SSM_H, :], acc)
        toe_scr[k, t * SSM_H:(t + 1) * SSM_H, :] = acc.astype(BF16)


def _s5_body(n_pc, n_pb, n_sb, xs_ref, xsmeta_ref, zr_ref, zi_ref, bbr_ref, bbi_ref, cr_ref, ci_ref,
             s0re_ref, s0im_ref,
             y_out, pfre_out, pfim_out, sfre_out, sfim_out,
             u_scr, sl_re, sl_im, sp_re, sp_im, yt_scr, toe_ref, p_ref, q_ref, a16_scr,
             g_scr, m_scr):
    gb = S5_GROUPS_PER_STEP
    n_p_rows = n_pb * n_pc
    row_s = n_p_rows
    row_m = row_s + n_sb
    t_p = n_p_rows * CHUNK
    rows_pad = u_scr.shape[2]

    for k in range(gb):
        _s5_operators(k // 2, k % 2, (zr_ref, zi_ref, bbr_ref, bbi_ref, cr_ref, ci_ref),
                      toe_ref, p_ref, q_ref, a16_scr, g_scr, m_scr)

    first_row = lax.broadcasted_iota(I32, (SUBLANES, 1), 0) == 0
    tail = jnp.zeros((rows_pad - row_m - SUBLANES, LANES), F32)
    for t in range(CHUNK):
        meta_rows = jnp.where(first_row, xsmeta_ref[t:t + 1, :], 0.0)
        rows_t = jnp.concatenate([xs_ref[pl.ds(t, n_p_rows, stride=CHUNK), :],
                                  xs_ref[pl.ds(t_p + t, n_sb, stride=CHUNK), :], meta_rows, tail], axis=0)
        xt = rows_t.T.astype(BF16)
        for k in range(gb):
            u_scr[k, t * SSM_H:(t + 1) * SSM_H, :] = xt[k * SSM_H:(k + 1) * SSM_H, :]

    npair = S5_PAIRS_PER_STEP
    for j in range(npair):
        sl0 = jnp.dot(p_ref[2 * j], u_scr[2 * j], preferred_element_type=F32)
        sl1 = jnp.dot(p_ref[2 * j + 1], u_scr[2 * j + 1], preferred_element_type=F32)
        sl_re[j] = jnp.concatenate([sl0[0:SSM_P, :], sl1[0:SSM_P, :]], axis=0).T
        sl_im[j] = jnp.concatenate([sl0[SSM_P:2 * SSM_P, :], sl1[SSM_P:2 * SSM_P, :]], axis=0).T
        sp_re[j, row_m:rows_pad, :] = jnp.zeros((rows_pad - row_m, 2 * SSM_P), F32)
        sp_im[j, row_m:rows_pad, :] = jnp.zeros((rows_pad - row_m, 2 * SSM_P), F32)

    ar = [a16_scr[j, 0] for j in range(npair)]
    ai = [a16_scr[j, 1] for j in range(npair)]
    sre = [jnp.broadcast_to(sl_re[j, row_m:row_m + 1, :], (n_pb, 2 * SSM_P)) for j in range(npair)]
    sim = [jnp.broadcast_to(sl_im[j, row_m:row_m + 1, :], (n_pb, 2 * SSM_P)) for j in range(npair)]
    for c in range(n_pc):
        rows = pl.ds(c, n_pb, stride=n_pc)
        for j in range(npair):
            sp_re[j, rows, :] = sre[j]
            sp_im[j, rows, :] = sim[j]
            nre = ar[j] * sre[j] - ai[j] * sim[j] + sl_re[j, rows, :]
            nim = ar[j] * sim[j] + ai[j] * sre[j] + sl_im[j, rows, :]
            sre[j], sim[j] = nre, nim
    for j in range(npair):
        pfre_out[j] = sre[j]
        pfim_out[j] = sim[j]
        s0r, s0i = s0re_ref[j], s0im_ref[j]
        sp_re[j, row_s:row_m, :] = s0r
        sp_im[j, row_s:row_m, :] = s0i
        sfre_out[j] = ar[j] * s0r - ai[j] * s0i + sl_re[j, row_s:row_m, :]
        sfim_out[j] = ar[j] * s0i + ai[j] * s0r + sl_im[j, row_s:row_m, :]

    nt = (((1,), (1,)), ((), ()))
    for k in range(gb):
        y = jnp.dot(toe_ref[k], u_scr[k], preferred_element_type=F32)
        y += lax.dot_general(q_ref[k, 0], sp_re[k // 2].astype(BF16), nt, preferred_element_type=F32)
        y += lax.dot_general(q_ref[k, 1], sp_im[k // 2].astype(BF16), nt, preferred_element_type=F32)
        for t in range(CHUNK):
            yt_scr[t, k * SSM_H:(k + 1) * SSM_H, :] = y[t * SSM_H:(t + 1) * SSM_H, :]
    for t in range(CHUNK):
        yt = yt_scr[t].T
        y_out[pl.ds(t, n_p_rows, stride=CHUNK), :] = yt[0:n_p_rows, :]
        y_out[pl.ds(t_p + t, n_sb, stride=CHUNK), :] = yt[row_s:row_m, :]


def _s5(xs, xsmeta, state_re, state_im, mats, n_pb, n_sb, seq):
    t = xs.shape[0]
    n_pc = seq // CHUNK
    rows = n_pc * n_pb + n_sb + 1
    rows_pad = -(-rows // LANES) * LANES
    gb, npair = S5_GROUPS_PER_STEP, S5_PAIRS_PER_STEP
    pairs = lambda s: jnp.transpose(s.reshape(n_sb, SSM_G // 2, 2 * SSM_P), (1, 0, 2))
    blk3 = lambda n, r, c: pl.BlockSpec((n, r, c), lambda i: (i, 0, 0))
    y, pfre, pfim, sfre, sfim = pl.pallas_call(
        functools.partial(_s5_body, n_pc, n_pb, n_sb),
        grid=(SSM_G // gb,),
        in_specs=[pl.BlockSpec((t, LANES), lambda i: (0, i)),
                  pl.BlockSpec((N_META, LANES), lambda i: (0, i)),
                  pl.BlockSpec((gb, SSM_P), lambda i: (i, 0)), pl.BlockSpec((gb, SSM_P), lambda i: (i, 0)),
                  blk3(gb, SSM_P, SSM_H), blk3(gb, SSM_P, SSM_H),
                  blk3(gb, SSM_H, SSM_P), blk3(gb, SSM_H, SSM_P),
                  blk3(npair, n_sb, 2 * SSM_P), blk3(npair, n_sb, 2 * SSM_P)],
        out_specs=[pl.BlockSpec((t, LANES), lambda i: (0, i)),
                   blk3(npair, n_pb, 2 * SSM_P), blk3(npair, n_pb, 2 * SSM_P),
                   blk3(npair, n_sb, 2 * SSM_P), blk3(npair, n_sb, 2 * SSM_P)],
        out_shape=[jax.ShapeDtypeStruct((t, D_SSM), F32),
                   jax.ShapeDtypeStruct((SSM_G // 2, n_pb, 2 * SSM_P), F32),
                   jax.ShapeDtypeStruct((SSM_G // 2, n_pb, 2 * SSM_P), F32),
                   jax.ShapeDtypeStruct((SSM_G // 2, n_sb, 2 * SSM_P), F32),
                   jax.ShapeDtypeStruct((SSM_G // 2, n_sb, 2 * SSM_P), F32)],
        scratch_shapes=[pltpu.VMEM((gb, CHUNK_W, rows_pad), BF16),
                        pltpu.VMEM((npair, rows_pad, 2 * SSM_P), F32),
                        pltpu.VMEM((npair, rows_pad, 2 * SSM_P), F32),
                        pltpu.VMEM((npair, rows_pad, 2 * SSM_P), F32),
                        pltpu.VMEM((npair, rows_pad, 2 * SSM_P), F32),
                        pltpu.VMEM((CHUNK, LANES, rows_pad), F32),
                        pltpu.VMEM((gb, CHUNK_W, CHUNK_W), BF16),
                        pltpu.VMEM((gb, 2 * SSM_P, CHUNK_W), BF16),
                        pltpu.VMEM((gb, 2, CHUNK_W, 2 * SSM_P), BF16),
                        pltpu.VMEM((npair, 2, 1, 2 * SSM_P), F32),
                        pltpu.VMEM((2, (CHUNK + 1) * SSM_H, SSM_P), F32),
                        pltpu.VMEM((CHUNK_W, CHUNK_W), F32)],
        compiler_params=pltpu.CompilerParams(dimension_semantics=("arbitrary",), vmem_limit_bytes=VMEM_LIMIT),
        name="s5_chunks",
    )(xs, xsmeta, *mats, pairs(state_re), pairs(state_im))
    unpair = lambda a: jnp.transpose(a, (1, 0, 2)).reshape(a.shape[1], SSM_G, SSM_P)[None]
    return y, unpair(pfre), unpair(pfim), unpair(sfre), unpair(sfim)


def _mixers_body(n_p_tiles, tiles_per_seq, xb_ref, z_ref, y5_ref, xs_ref, sga_ref, sgb_ref,
                 zmeta_ref, cstate_ref, cw_ref, dskip_ref,
                 wc_hbm, wg_hbm, wso_hbm, merged_out,
                 wc, wg, wso, stage_a, stage_b, sem, carry, inj_scr):
    i = pl.program_id(0)

    @pl.when(i == 0)
    def _():
        _load_weight_bf16(wc_hbm, wc, stage_a, sem)
        _load_weight_bf16(wg_hbm, wg, stage_b, sem)
        _load_weight_bf16(wso_hbm, wso, stage_a, sem)

    @pl.when(jnp.logical_and(i < n_p_tiles, i % tiles_per_seq == 0))
    def _():
        carry[0:2, :] = zmeta_ref[N_META - 2:N_META, :]

    is_s = i >= n_p_tiles

    @pl.when(i == 0)
    def _():
        inj_scr[...] = jnp.zeros(inj_scr.shape, F32)

    @pl.when(is_s)
    def _():
        seqs = TM // CHUNK
        rr = lax.broadcasted_iota(I32, (TM, seqs), 0)
        cc = lax.broadcasted_iota(I32, (TM, seqs), 1) * CHUNK
        at0 = jnp.where(rr == cc, 1.0, 0.0).astype(BF16)
        at1 = jnp.where(rr == cc + 1, 1.0, 0.0).astype(BF16)

        def place(sel, rows):
            r_hi, r_lo = _split_bf16(rows)
            return (jnp.dot(sel, r_hi, preferred_element_type=F32)
                    + jnp.dot(sel, r_lo, preferred_element_type=F32))

        old, new = cstate_ref[:, 0:D_CONV], cstate_ref[:, D_CONV:2 * D_CONV]
        inj_scr[0] = place(at0, new)
        inj_scr[1] = place(at0, old) + place(at1, new)

    z = z_ref[...]
    row = lax.broadcasted_iota(I32, (TM, 1), 0)
    r1 = pltpu.roll(z, 1, 0)
    r2 = pltpu.roll(z, 2, 0)
    c1 = carry[1:2, :]
    c2 = carry[0:1, :]
    pos = jnp.where(is_s, row & (CHUNK - 1), row)
    first1 = pos == 0
    first2 = pos < 2
    fill1 = jnp.where(is_s, inj_scr[0], jnp.broadcast_to(c1, z.shape))
    fill2 = jnp.where(is_s, inj_scr[1], jnp.where(row == 0, c2, c1))
    zp1 = jnp.where(first1, fill1, r1)
    zp2 = jnp.where(first2, fill2, r2)
    carry[0:2, :] = z[TM - 2:TM, :]

    cw = cw_ref[...]
    conv = cw[0:1, :] * zp2 + cw[1:2, :] * zp1 + cw[2:3, :] * z
    a_in = (xb_ref[...].astype(F32) * conv).astype(BF16)
    ya = jnp.dot(a_in, wc[...], preferred_element_type=F32)

    ys = y5_ref[...] + dskip_ref[...] * xs_ref[...]
    ys = _gelu_tanh(ys)
    glu = jnp.dot(ys.astype(BF16), wg[...], preferred_element_type=F32)
    ys = ys * _sigmoid(glu)
    yb = jnp.dot(ys.astype(BF16), wso[...], preferred_element_type=F32)

    merged = sga_ref[...].astype(F32) * ya + sgb_ref[...].astype(F32) * yb
    merged_out[...] = merged.astype(BF16)


def _mixers(xb, z, y5, xs, sga, sgb, zmeta, conv_state, conv_w, d_skip, w_conv_out, w_glu, w_ssm_out,
            n_p_tiles, tiles_per_seq):
    t = xb.shape[0]
    seqs = TM // CHUNK
    n_s_tiles = conv_state.shape[0] // seqs
    row = lambda w: pl.BlockSpec((TM, w), lambda i: (i, 0))
    const = lambda r, w: pl.BlockSpec((r, w), lambda i: (0, 0))
    cstate = pl.BlockSpec((seqs, 2 * D_CONV), lambda i: (jnp.clip(i - n_p_tiles, 0, n_s_tiles - 1), 0))
    any_spec = pl.BlockSpec(memory_space=pl.ANY)
    return pl.pallas_call(
        functools.partial(_mixers_body, n_p_tiles, tiles_per_seq),
        grid=(t // TM,),
        in_specs=[row(D_CONV), row(D_CONV), row(D_SSM), row(D_SSM), row(D_MODEL), row(D_MODEL),
                  const(N_META, D_CONV), cstate, const(3, D_CONV), const(1, D_SSM),
                  any_spec, any_spec, any_spec],
        out_specs=row(D_MODEL),
        out_shape=jax.ShapeDtypeStruct((t, D_MODEL), BF16),
        scratch_shapes=[pltpu.VMEM((D_CONV, D_MODEL), BF16),
                        pltpu.VMEM((D_SSM, D_SSM), BF16),
                        pltpu.VMEM((D_SSM, D_MODEL), BF16),
                        pltpu.VMEM((2, 256, D_MODEL), F32),
                        pltpu.VMEM((2, 256, D_SSM), F32),
                        pltpu.SemaphoreType.DMA((2,)),
                        pltpu.VMEM((8, D_CONV), F32),
                        pltpu.VMEM((2, TM, D_CONV), F32)],
        compiler_params=pltpu.CompilerParams(dimension_semantics=("arbitrary",), vmem_limit_bytes=VMEM_LIMIT),
        name="mixers",
    )(xb, z, y5, xs, sga, sgb, zmeta, conv_state, conv_w, d_skip, w_conv_out, w_glu, w_ssm_out)


ROUTE_W = LANES
COARSE0 = N_EXPERTS


def _route(logits, cnt):
    col = lax.broadcasted_iota(I32, logits.shape, 1)
    colf = col.astype(F32)
    neg = jnp.float32(-jnp.inf)
    big = jnp.float32(1 << 20)
    is_c = jnp.logical_and(col >= COARSE0, col < COARSE0 + N_EGROUPS)
    lc = jnp.where(is_c, logits, neg)
    cmax = jnp.max(lc, axis=-1, keepdims=True)
    gi = jnp.min(jnp.where(lc == cmax, colf - COARSE0, big), axis=-1, keepdims=True)
    pg = 1.0 / jnp.sum(jnp.where(is_c, jnp.exp(lc - cmax), 0.0), axis=-1, keepdims=True)
    grp = (col >> 3).astype(F32)
    in_g = jnp.logical_and(col < N_EXPERTS, grp == gi)
    lf = jnp.where(in_g, logits, neg)
    m1 = jnp.max(lf, axis=-1, keepdims=True)
    i1 = jnp.min(jnp.where(lf == m1, colf, big), axis=-1, keepdims=True)
    lf2 = jnp.where(colf == i1, neg, lf)
    m2 = jnp.max(lf2, axis=-1, keepdims=True)
    i2 = jnp.min(jnp.where(lf2 == m2, colf, big), axis=-1, keepdims=True)
    e2 = jnp.exp(m2 - m1)
    w1 = pg / (1.0 + e2)
    w2 = pg * e2 / (1.0 + e2)
    n = logits.shape[0]
    hit1 = colf == i1
    hit2 = colf == i2
    onehot = jnp.where(jnp.logical_or(hit1, hit2), 1.0, 0.0)
    rr = lax.broadcasted_iota(I32, (n, n), 0)
    cc = lax.broadcasted_iota(I32, (n, n), 1)
    tri = jnp.where(cc < rr, 1.0, 0.0).astype(BF16)
    pos = jnp.dot(tri, onehot.astype(BF16), preferred_element_type=F32) + cnt
    rank1 = jnp.sum(jnp.where(hit1, pos, 0.0), axis=-1, keepdims=True)
    rank2 = jnp.sum(jnp.where(hit2, pos, 0.0), axis=-1, keepdims=True)
    vals = (i1, i2, w1, w2, rank1, rank2)
    rec = jnp.zeros(logits.shape, F32)
    for c, val in enumerate(vals):
        rec = jnp.where(col == c, val, rec)
    return rec, cnt + jnp.sum(onehot, axis=0, keepdims=True)


def _out_proj_body(n_p_tiles, merged_ref, xp_ref, xsm_ref, g2_ref, wr_ref, wo_hbm,
                   h2_out, v_out, rec_out, cnt_out, wo, stage, sem, h2_scr):
    i = pl.program_id(0)
    n = pl.num_programs(0) - 1

    def route_prev():
        v = _rmsnorm(h2_scr[...], g2_ref[...])
        v_out[...] = _pack_halves(v)
        v_hi, v_lo = _split_bf16(v)
        both = jnp.dot(v_hi, wr_ref[...], preferred_element_type=F32)
        logits = (both[:, :ROUTE_W] + both[:, ROUTE_W:]
                  + jnp.dot(v_lo, wr_ref[:, :ROUTE_W], preferred_element_type=F32))
        rec, cnt = _route(logits, cnt_out[...])
        rec_out[...] = rec
        cnt_out[...] = cnt

    def project():
        x = jnp.where(i < n_p_tiles, xp_ref[...], xsm_ref[...])
        h2 = x + jnp.dot(merged_ref[...], wo[...], preferred_element_type=F32)
        h2_out[...] = h2
        return h2

    @pl.when(i == 0)
    def _():
        cnt_out[...] = jnp.zeros(cnt_out.shape, F32)
        cw = stage.shape[2]
        merged = merged_ref[...]

        def on_chunk(c):
            cols = slice(c * cw, (c + 1) * cw)
            h2 = xp_ref[:, cols] + jnp.dot(merged, wo[:, cols], preferred_element_type=F32)
            h2_out[:, cols] = h2
            h2_scr[:, cols] = h2

        _stream_weight_bf16(wo_hbm, wo, stage, sem, 0, on_chunk)

    @pl.when(jnp.logical_and(i > 0, i < n))
    def _():
        route_prev()
        h2_scr[...] = project()

    @pl.when(i == n)
    def _():
        route_prev()


def _out_proj(merged, xp, xsm, norm2, w_router, w_o):
    t_p, t_s = xp.shape[0], xsm.shape[0]
    n_p, n_s = t_p // TM, t_s // TM
    n = n_p + n_s
    t = t_p + t_s
    xp_spec, xs_spec = _two_stream_specs(n_p, n_s)
    cur = lambda w: pl.BlockSpec((TM, w), lambda i: (jnp.minimum(i, n - 1), 0))
    prev = lambda w: pl.BlockSpec((TM, w), lambda i: (jnp.maximum(i - 1, 0), 0))
    const = lambda r, w: pl.BlockSpec((r, w), lambda i: (0, 0))
    return pl.pallas_call(
        functools.partial(_out_proj_body, n_p),
        grid=(n + 1,),
        in_specs=[cur(D_MODEL), xp_spec, xs_spec, const(1, D_MODEL), const(D_MODEL, 2 * ROUTE_W),
                  pl.BlockSpec(memory_space=pl.ANY)],
        out_specs=[cur(D_MODEL), prev(HALF), prev(ROUTE_W), const(1, ROUTE_W)],
        out_shape=[jax.ShapeDtypeStruct((t, D_MODEL), F32),
                   jax.ShapeDtypeStruct((t, HALF), U32),
                   jax.ShapeDtypeStruct((t, ROUTE_W), F32),
                   jax.ShapeDtypeStruct((1, ROUTE_W), F32)],
        scratch_shapes=[pltpu.VMEM((D_MODEL, D_MODEL), BF16),
                        pltpu.VMEM((2, D_MODEL, WEIGHT_CHUNK), F32),
                        pltpu.SemaphoreType.DMA((2,)),
                        pltpu.VMEM((TM, D_MODEL), F32)],
        compiler_params=pltpu.CompilerParams(dimension_semantics=("arbitrary",), vmem_limit_bytes=VMEM_LIMIT),
        name="out_proj_route",
    )(merged, xp, xsm, norm2, w_router, w_o)


def _moe_plan(rec, cnt, n_tiles):
    t = rec.shape[0]
    n_pairs = 2 * t
    eid = rec[:, 0:2].astype(I32).reshape(-1)
    rank = rec[:, 4:6].astype(I32).reshape(-1)
    counts = cnt[0, :N_EXPERTS].astype(I32)
    pair_start = jnp.cumsum(counts) - counts
    experts = jnp.arange(N_EXPERTS, dtype=I32)
    onehot = (eid[:, None] == experts[None, :]).astype(I32)
    pos = rank + jnp.sum(onehot * pair_start[None, :], axis=1)
    _, order = lax.sort((pos, jnp.arange(n_pairs, dtype=I32)), num_keys=1)
    tiles_e = (counts + TM_MOE - 1) // TM_MOE
    tile_end = jnp.cumsum(tiles_e)
    tile_start = tile_end - tiles_e
    n_valid = tile_end[-1]
    tile_ids = jnp.arange(n_tiles, dtype=I32)
    tile_e = jnp.sum((tile_ids[:, None] >= tile_end[None, :]).astype(I32), axis=1)
    last_e = jnp.sum((n_valid - 1 >= tile_end).astype(I32))
    tile_e = jnp.minimum(jnp.where(tile_ids < n_valid, tile_e, last_e), N_EXPERTS - 1)
    tile_onehot = (tile_e[:, None] == experts[None, :]).astype(I32)
    tile_q0 = (tile_ids - jnp.sum(tile_onehot * tile_start[None, :], axis=1)) * TM_MOE
    tile_q0 = jnp.where(tile_ids < n_valid, tile_q0, 0)
    dst = rank + jnp.sum(onehot * (tile_start * TM_MOE)[None, :], axis=1)
    pad_start = tile_start * TM_MOE + counts
    pad_len = tiles_e * TM_MOE - counts
    return (dst.astype(I32), pad_start.astype(I32), pad_len.astype(I32), tile_e, tile_q0.astype(I32),
            n_valid.astype(I32).reshape(1), pair_start, counts, order)


DISPATCH_BUFS = 3
ROW_PIECES = tuple(TM_MOE >> (b + 1) for b in range(TM_MOE.bit_length() - 1))


def _dispatch_body(dst_ref, pad_start_ref, pad_len_ref, nvalid_ref, v_hbm, x_hbm, buf, zbuf, rsem, ssem, zsem):
    i = pl.program_id(0)
    n = pl.num_programs(0)

    def read(tile, slot):
        return pltpu.make_async_copy(v_hbm.at[pl.ds(pl.multiple_of(tile * TM, TM), TM)], buf.at[slot],
                                     rsem.at[slot])

    def row_write(slot, r, dst_row):
        return pltpu.make_async_copy(buf.at[slot, pl.ds(r, 1)], x_hbm.at[pl.ds(dst_row, 1)], ssem.at[slot])

    def drain(slot):
        for _ in range(2):
            pltpu.make_async_copy(buf.at[slot], x_hbm.at[pl.ds(0, TM)], ssem.at[slot]).wait()

    def pad_fill(go):
        def zero_rows(start, size):
            d = pltpu.make_async_copy(zbuf.at[pl.ds(0, size)], x_hbm.at[pl.ds(start, size)], zsem)
            d.start() if go else d.wait()

        def body(e, c):
            start, length = pad_start_ref[e], pad_len_ref[e]
            head = (-start) & (SUBLANES - 1)
            for h in range(SUBLANES - 1):
                @pl.when(h < head)
                def _(h=h):
                    zero_rows(start + h, 1)
            start, length = start + head, length - head
            for size in ROW_PIECES:
                if size >= SUBLANES:
                    @pl.when((length & size) != 0)
                    def _(size=size):
                        zero_rows(pl.multiple_of(start + (length & (-2 * size)), SUBLANES), size)
            return c
        lax.fori_loop(0, N_EXPERTS, body, 0)

        def unused(tile, c):
            for half in range(TM_MOE // ROW_PIECES[0]):
                zero_rows(pl.multiple_of(tile * TM_MOE + half * ROW_PIECES[0], SUBLANES), ROW_PIECES[0])
            return c
        lax.fori_loop(nvalid_ref[0], x_hbm.shape[0] // TM_MOE, unused, 0)

    @pl.when(i == 0)
    def _():
        zbuf[...] = jnp.zeros(zbuf.shape, U32)
        pad_fill(True)
        read(0, 0).start()

    @pl.when(i >= 2)
    def _():
        drain((i + 1) % DISPATCH_BUFS)

    @pl.when(i + 1 < n)
    def _():
        read(i + 1, (i + 1) % DISPATCH_BUFS).start()

    slot = i % DISPATCH_BUFS
    read(i, slot).wait()
    for r in range(TM):
        for k in range(2):
            row_write(slot, r, dst_ref[2 * (i * TM + r) + k]).start(priority=k)

    @pl.when(i == n - 1)
    def _():
        if n >= 2:
            drain((i - 1) % DISPATCH_BUFS)
        drain(slot)
        pad_fill(False)


def _dispatch(v, plan, n_tiles):
    dst, pad_start, pad_len, n_valid = plan[0], plan[1], plan[2], plan[5]
    t = v.shape[0]
    return pl.pallas_call(
        _dispatch_body,
        grid_spec=pltpu.PrefetchScalarGridSpec(
            num_scalar_prefetch=4,
            grid=(t // TM,),
            in_specs=[pl.BlockSpec(memory_space=pl.ANY)],
            out_specs=pl.BlockSpec(memory_space=pl.ANY),
            scratch_shapes=[pltpu.VMEM((DISPATCH_BUFS, TM, HALF), U32),
                            pltpu.VMEM((ROW_PIECES[0], HALF), U32),
                            pltpu.SemaphoreType.DMA((DISPATCH_BUFS,)),
                            pltpu.SemaphoreType.DMA((DISPATCH_BUFS,)),
                            pltpu.SemaphoreType.DMA(())]),
        out_shape=jax.ShapeDtypeStruct((n_tiles * TM_MOE, HALF), U32),
        compiler_params=pltpu.CompilerParams(dimension_semantics=("arbitrary",), vmem_limit_bytes=VMEM_LIMIT),
        name="moe_dispatch",
    )(dst, pad_start, pad_len, n_valid, v)


WEIGHT_SLOTS = 3


def _moe_body(n_tok, plane, tile_e_ref, tile_q0_ref, nvalid_ref, pstart_ref, cnt_ref, orow_ref,
              krank_ref, elist_ref, nexp_ref,
              x_ref, wg_hbm, wu_hbm, wd_hbm, o_hbm, ybuf0, ybuf1, ybuf2, ssem,
              stage_g, stage_u, stage_d, wsem, wg, wu, wd):
    i = pl.program_id(0)
    nv = nvalid_ref[0]
    bufs = (ybuf0, ybuf1, ybuf2)

    def scratch_row0(slot):
        return (slot & 1) * plane + n_tok + (slot >> 1) * TM_MOE

    def row_write(slot, r, dst_row):
        return pltpu.make_async_copy(bufs[slot].at[pl.ds(r, 1)], o_hbm.at[pl.ds(dst_row, 1)], ssem.at[slot])

    def scratch_rows(region):
        return pltpu.make_async_copy(ybuf0, o_hbm.at[pl.ds(scratch_row0(region), TM_MOE)], ssem.at[0])

    def start_writes(tile, slot, rows):
        e = tile_e_ref[tile]
        valid = cnt_ref[e] - tile_q0_ref[tile]
        first = pstart_ref[e] + tile_q0_ref[tile]
        for r in rows:
            row_write(slot, r, jnp.where(r < valid, orow_ref[first + r], scratch_row0(slot) + r)).start(
                priority=r % 2)

    def compute(slot, writes=None):
        quarter = TM_MOE // 4
        batch = lambda q: start_writes(*writes, range(q * quarter, (q + 1) * quarter)) if writes else None
        x_lo, x_hi = (h.astype(BF16) for h in _unpack_halves(x_ref[...]))
        batch(0)
        hg = (jnp.dot(x_lo, wg[0:HALF, :], preferred_element_type=F32)
              + jnp.dot(x_hi, wg[HALF:D_MODEL, :], preferred_element_type=F32))
        batch(1)
        hu = (jnp.dot(x_lo, wu[0:HALF, :], preferred_element_type=F32)
              + jnp.dot(x_hi, wu[HALF:D_MODEL, :], preferred_element_type=F32))
        batch(2)
        act = hg * _sigmoid(hg) * hu
        y = jnp.dot(act.astype(BF16), wd[...], preferred_element_type=F32)
        batch(3)
        bufs[slot][...] = _pack_halves(y)

    @pl.when(i == 0)
    def _():
        ybuf0[...] = jnp.zeros(ybuf0.shape, U32)
        for region in range(4):
            scratch_rows(region).start()
        for region in range(4):
            scratch_rows(region).wait()

    @pl.when(jnp.logical_and(i >= 3, i < nv + 3))
    def _():
        pltpu.make_async_copy(ybuf0, o_hbm.at[pl.ds(0, TM_MOE)], ssem.at[i % 3]).wait()

    def weight_copies(k, go):
        e = elist_ref[k]
        slot = k % WEIGHT_SLOTS
        for w_hbm, st in ((wg_hbm, stage_g), (wu_hbm, stage_u), (wd_hbm, stage_d)):
            d = pltpu.make_async_copy(w_hbm.at[e], st.at[slot], wsem.at[slot])
            d.start() if go else d.wait()

    @pl.when(i == 0)
    def _():
        for k in range(WEIGHT_SLOTS):
            @pl.when(k < nexp_ref[0])
            def _(k=k):
                weight_copies(k, True)

    @pl.when(i < nv)
    def _():
        prev_e = tile_e_ref[jnp.maximum(i - 1, 0)]

        @pl.when(jnp.logical_or(i == 0, tile_e_ref[i] != prev_e))
        def _():
            k = krank_ref[tile_e_ref[i]]
            slot = k % WEIGHT_SLOTS
            weight_copies(k, False)
            wg[...] = stage_g[slot].astype(BF16)
            wu[...] = stage_u[slot].astype(BF16)
            wd[...] = stage_d[slot].astype(BF16)

            @pl.when(k + WEIGHT_SLOTS < nexp_ref[0])
            def _():
                weight_copies(k + WEIGHT_SLOTS, True)

    @pl.when(i == 0)
    def _():
        compute(0)

    for slot in range(3):
        prev = (slot + 2) % 3

        @pl.when(jnp.logical_and(i % 3 == slot, jnp.logical_and(i >= 1, i < nv)))
        def _(slot=slot, prev=prev):
            compute(slot, writes=(i - 1, prev))

        @pl.when(jnp.logical_and(i % 3 == slot, i == nv))
        def _(prev=prev):
            start_writes(i - 1, prev, range(TM_MOE))


def _moe(x_disp, plan, w_gate, w_up, w_down, n_tiles, n_tok):
    tile_e, tile_q0, n_valid, pair_start, counts, order = plan[3:9]
    plane = n_tok + 2 * TM_MOE
    orow = jnp.pad((order & 1) * plane + (order >> 1), (0, TM_MOE))
    present = (counts > 0).astype(I32)
    krank = jnp.cumsum(present) - present
    experts = jnp.arange(N_EXPERTS, dtype=I32)
    elist = jnp.sum(jnp.where((krank[None, :] == experts[:, None]) & (present[None, :] > 0), experts[None, :], 0),
                    axis=1).astype(I32)
    nexp = jnp.sum(present).astype(I32).reshape(1)
    tile = lambda i, nv: jnp.minimum(i, jnp.maximum(nv[0] - 1, 0))
    any_spec = pl.BlockSpec(memory_space=pl.ANY)
    ybuf = pltpu.VMEM((TM_MOE, HALF), U32)
    return pl.pallas_call(
        functools.partial(_moe_body, n_tok, plane),
        grid_spec=pltpu.PrefetchScalarGridSpec(
            num_scalar_prefetch=9,
            grid=(n_tiles + 3,),
            in_specs=[pl.BlockSpec((TM_MOE, HALF), lambda i, te, tq, nv, *_: (tile(i, nv), 0)),
                      any_spec, any_spec, any_spec],
            out_specs=pl.BlockSpec(memory_space=pl.ANY),
            scratch_shapes=[ybuf, ybuf, ybuf,
                            pltpu.SemaphoreType.DMA((3,)),
                            pltpu.VMEM((WEIGHT_SLOTS, D_MODEL, D_EXPERT), F32),
                            pltpu.VMEM((WEIGHT_SLOTS, D_MODEL, D_EXPERT), F32),
                            pltpu.VMEM((WEIGHT_SLOTS, D_EXPERT, D_MODEL), F32),
                            pltpu.SemaphoreType.DMA((WEIGHT_SLOTS,)),
                            pltpu.VMEM((D_MODEL, D_EXPERT), BF16),
                            pltpu.VMEM((D_MODEL, D_EXPERT), BF16),
                            pltpu.VMEM((D_EXPERT, D_MODEL), BF16)]),
        out_shape=jax.ShapeDtypeStruct((2 * plane, HALF), U32),
        compiler_params=pltpu.CompilerParams(dimension_semantics=("arbitrary",), vmem_limit_bytes=VMEM_LIMIT),
        name="moe_experts",
    )(tile_e, tile_q0, n_valid, pair_start, counts, orow, krank.astype(I32), elist, nexp,
      x_disp, w_gate, w_up, w_down)


def _combine_body(n_p_tiles, h2_ref, rec_ref, gf_ref, y1_ref, y2_ref, outp_ref, outs_ref):
    i = pl.program_id(0)
    rec = rec_ref[...]
    y1 = jnp.concatenate(_unpack_halves(y1_ref[...]), axis=-1)
    y2 = jnp.concatenate(_unpack_halves(y2_ref[...]), axis=-1)
    h = rec[:, 2:3] * y1 + rec[:, 3:4] * y2
    out = _rmsnorm(h2_ref[...] + h, gf_ref[...])

    @pl.when(i < n_p_tiles)
    def _():
        outp_ref[...] = out

    @pl.when(i >= n_p_tiles)
    def _():
        outs_ref[...] = out


def _combine(h2, rec, y_pairs, final_norm, t_p, t_s):
    n_p, n_s = t_p // TM, t_s // TM
    plane_tiles = y_pairs.shape[0] // 2 // TM
    return pl.pallas_call(
        functools.partial(_combine_body, n_p),
        grid=(n_p + n_s,),
        in_specs=[pl.BlockSpec((TM, D_MODEL), lambda i: (i, 0)),
                  pl.BlockSpec((TM, ROUTE_W), lambda i: (i, 0)),
                  pl.BlockSpec((1, D_MODEL), lambda i: (0, 0)),
                  pl.BlockSpec((TM, HALF), lambda i: (i, 0)),
                  pl.BlockSpec((TM, HALF), lambda i: (i + plane_tiles, 0))],
        out_specs=[pl.BlockSpec((TM, D_MODEL), lambda i: (jnp.minimum(i, n_p - 1), 0)),
                   pl.BlockSpec((TM, D_MODEL), lambda i: (jnp.clip(i - n_p, 0, n_s - 1), 0))],
        out_shape=[jax.ShapeDtypeStruct((t_p, D_MODEL), F32),
                   jax.ShapeDtypeStruct((t_s, D_MODEL), F32)],
        compiler_params=pltpu.CompilerParams(dimension_semantics=("arbitrary",), vmem_limit_bytes=VMEM_LIMIT),
        name="combine_norm",
    )(h2, rec, final_norm, y_pairs, y_pairs)


def kernel(x_prompt, x_sample, state_conv, state_ssm_re, state_ssm_im, meta_tokens, norm1, w_in, conv_w,
           lam_re, lam_im, log_dt, ssm_b_re, ssm_b_im, ssm_c_re, ssm_c_im, ssm_d, w_glu, w_conv_out,
           w_ssm_out, w_o, norm2, w_coarse, w_fine, w_gate, w_up, w_down, final_norm):
    n_pb, seq, _ = x_prompt.shape
    n_sb, dec_seq, _ = x_sample.shape
    assert dec_seq == CHUNK and seq % TM == 0 and (n_sb * dec_seq) % TM == 0 and N_META == CHUNK
    t_p, t_s = n_pb * seq, n_sb * dec_seq
    xp = x_prompt.reshape(t_p, D_MODEL)
    xsm = x_sample.reshape(t_s, D_MODEL)

    xb, z, xs, zmeta, xsmeta, sga, sgb = _in_proj(xp, xsm, meta_tokens, norm1, w_in[0])

    mats = _s5_chunk_mats(lam_re[0], lam_im[0], log_dt[0], ssm_b_re[0], ssm_b_im[0], ssm_c_re[0], ssm_c_im[0])
    y5, pf_re, pf_im, sf_re, sf_im = _s5(xs, xsmeta, state_ssm_re[0], state_ssm_im[0], mats, n_pb, n_sb, seq)

    merged = _mixers(xb, z, y5, xs, sga, sgb, zmeta, state_conv[0].reshape(n_sb, 2 * D_CONV), conv_w[0], ssm_d,
                     w_conv_out[0], w_glu[0], w_ssm_out[0], t_p // TM, seq // TM)

    w_router = jnp.concatenate(
        [w_fine[0], w_coarse[0], jnp.zeros((D_MODEL, ROUTE_W - N_EXPERTS - N_EGROUPS), F32)], axis=1)
    w_router = jnp.concatenate(_split_bf16(w_router), axis=1)
    h2, v, rec, cnt = _out_proj(merged, xp, xsm, norm2, w_router, w_o[0])

    n_tiles = 2 * (t_p + t_s) // TM_MOE + N_EXPERTS
    plan = _moe_plan(rec, cnt, n_tiles)
    x_disp = _dispatch(v, plan, n_tiles)
    y_pairs = _moe(x_disp, plan, w_gate[0], w_up[0], w_down[0], n_tiles, t_p + t_s)
    y_p, y_s = _combine(h2, rec, y_pairs, final_norm.reshape(1, D_MODEL), t_p, t_s)

    new_conv_p = jnp.stack([z[(b + 1) * seq - 2:(b + 1) * seq] for b in range(n_pb)])
    new_conv_s = z[t_p:].reshape(n_sb, dec_seq, D_CONV)[:, dec_seq - 2:]
    return (y_p.reshape(n_pb, seq, D_MODEL), y_s.reshape(n_sb, dec_seq, D_MODEL),
            new_conv_p[None], pf_re, pf_im, new_conv_s[None], sf_re, sf_im)
```

```python
import functools

import jax
import jax.numpy as jnp
from jax import lax
from jax.experimental import pallas as pl
from jax.experimental.pallas import tpu as pltpu

F32 = jnp.float32
BF16 = jnp.bfloat16
I32 = jnp.int32
U32 = jnp.uint32

D_MODEL = 2048
D_CONV = 1024
D_SSM = 1024
SSM_H = 16
SSM_G = 64
SSM_P = 64
N_META = 16
N_EGROUPS = 4
EXPERTS_PER_GROUP = 8
N_EXPERTS = 32
D_EXPERT = 256
EPS = 1e-6

LANES = 128
SUBLANES = 8

CHUNK = 16
CHUNK_W = CHUNK * SSM_H

TM = 256
TM_MOE = 256
VMEM_LIMIT = 52 * 1024 * 1024


def _rmsnorm(x, g):
    return x * lax.rsqrt(jnp.mean(x * x, axis=-1, keepdims=True) + EPS) * g


def _sigmoid(x):
    return 1.0 / (1.0 + jnp.exp(-x))


def _gelu_tanh(x):
    c = 0.7978845608028654
    return 0.5 * x * (1.0 + jnp.tanh(c * (x + 0.044715 * (x * x * x))))


def _split_bf16(a):
    hi = a.astype(BF16)
    lo = (a - hi.astype(F32)).astype(BF16)
    return hi, lo


def _dot3(a, b):
    a_hi, a_lo = _split_bf16(a)
    b_hi, b_lo = _split_bf16(b)
    return (jnp.dot(a_hi, b_hi, preferred_element_type=F32) + jnp.dot(a_lo, b_hi, preferred_element_type=F32)
            + jnp.dot(a_hi, b_lo, preferred_element_type=F32))


HALF = D_MODEL // 2


def _pack_halves(a):
    return pltpu.pack_elementwise([a[:, :HALF], a[:, HALF:]], packed_dtype=BF16)


def _unpack_halves(p):
    return (pltpu.unpack_elementwise(p, index=0, packed_dtype=BF16, unpacked_dtype=F32),
            pltpu.unpack_elementwise(p, index=1, packed_dtype=BF16, unpacked_dtype=F32))


def _weight_copy(w_hbm, stage, sem, c, slot, rows, col0, ncols):
    return pltpu.make_async_copy(
        w_hbm.at[pl.ds(c * rows, rows), pl.ds(col0, ncols)], stage.at[slot], sem.at[slot])


def _load_weight_bf16(w_hbm, w_vmem, stage, sem, col0=0):
    k, n = w_vmem.shape
    rows = stage.shape[1]
    nchunk = k // rows
    _weight_copy(w_hbm, stage, sem, 0, 0, rows, col0, n).start()
    for c in range(nchunk):
        slot = c % 2
        if c + 1 < nchunk:
            _weight_copy(w_hbm, stage, sem, c + 1, 1 - slot, rows, col0, n).start()
        _weight_copy(w_hbm, stage, sem, c, slot, rows, col0, n).wait()
        w_vmem[pl.ds(c * rows, rows), :] = stage[slot].astype(BF16)


WEIGHT_CHUNK = 512


def _stream_weight_bf16(w_hbm, w_vmem, stage, sem, col0, on_chunk):
    _, n = w_vmem.shape
    w = stage.shape[2]
    nchunk = n // w

    def copy(c, slot):
        return pltpu.make_async_copy(w_hbm.at[:, pl.ds(col0 + c * w, w)], stage.at[slot], sem.at[slot])

    copy(0, 0).start()
    for c in range(nchunk):
        slot = c % 2
        if c + 1 < nchunk:
            copy(c + 1, 1 - slot).start()
        copy(c, slot).wait()
        w_vmem[:, c * w:(c + 1) * w] = stage[slot].astype(BF16)
        on_chunk(c)


def _in_proj_mix_body(n_p_tiles, xp_ref, xsm_ref, meta_ref, g_ref, w_hbm,
                      xb_out, z_out, xs_out, zmeta_out, xsmeta_out,
                      w_vmem, stage, sem):
    i = pl.program_id(0)
    g = g_ref[...]
    cw = stage.shape[2]

    @pl.when(i == 0)
    def _():
        u = _rmsnorm(xp_ref[...], g).astype(BF16)
        um = _rmsnorm(meta_ref[...], g).astype(BF16)

        def on_chunk(c):
            n0 = c * cw
            cols = slice(n0 % D_CONV, n0 % D_CONV + cw)
            p = jnp.dot(u, w_vmem[:, n0:n0 + cw], preferred_element_type=F32)
            pm = jnp.dot(um, w_vmem[:, n0:n0 + cw], preferred_element_type=F32)
            if n0 < D_CONV:
                xb_out[:, cols] = p.astype(BF16)
            elif n0 < 2 * D_CONV:
                z_out[:, cols] = p
                zmeta_out[:, cols] = pm
            elif n0 < 3 * D_CONV:
                z_out[:, cols] = z_out[:, cols] * p
                zmeta_out[:, cols] = zmeta_out[:, cols] * pm
            else:
                xs_out[:, cols] = p
                xsmeta_out[:, cols] = pm

        _stream_weight_bf16(w_hbm, w_vmem, stage, sem, 0, on_chunk)

    @pl.when(i > 0)
    def _():
        x = jnp.where(i < n_p_tiles, xp_ref[...], xsm_ref[...])
        u = _rmsnorm(x, g).astype(BF16)
        xb = jnp.dot(u, w_vmem[:, 0:D_CONV], preferred_element_type=F32)
        xc = jnp.dot(u, w_vmem[:, D_CONV:2 * D_CONV], preferred_element_type=F32)
        xv = jnp.dot(u, w_vmem[:, 2 * D_CONV:3 * D_CONV], preferred_element_type=F32)
        xs = jnp.dot(u, w_vmem[:, 3 * D_CONV:3 * D_CONV + D_SSM], preferred_element_type=F32)
        xb_out[...] = xb.astype(BF16)
        z_out[...] = xc * xv
        xs_out[...] = xs


def _in_proj_gate_body(n_p_tiles, xp_ref, xsm_ref, g_ref, w_hbm, ga_out, gb_out,
                       w_vmem, stage, sem):
    i = pl.program_id(0)
    cw = stage.shape[2]

    @pl.when(i == 0)
    def _():
        u = _rmsnorm(xp_ref[...], g_ref[...]).astype(BF16)

        def on_chunk(c):
            n0 = c * cw
            out = ga_out if n0 < D_MODEL else gb_out
            gate = _sigmoid(jnp.dot(u, w_vmem[:, n0:n0 + cw], preferred_element_type=F32))
            out[:, n0 % D_MODEL:n0 % D_MODEL + cw] = gate.astype(BF16)

        _stream_weight_bf16(w_hbm, w_vmem, stage, sem, 3 * D_CONV + D_SSM, on_chunk)

    @pl.when(i > 0)
    def _():
        x = jnp.where(i < n_p_tiles, xp_ref[...], xsm_ref[...])
        u = _rmsnorm(x, g_ref[...]).astype(BF16)
        gates = jnp.dot(u, w_vmem[...], preferred_element_type=F32)
        ga_out[...] = _sigmoid(gates[:, 0:D_MODEL]).astype(BF16)
        gb_out[...] = _sigmoid(gates[:, D_MODEL:2 * D_MODEL]).astype(BF16)


def _two_stream_specs(n_p_tiles, n_s_tiles, tm=TM):
    xp_spec = pl.BlockSpec((tm, D_MODEL), lambda i: (jnp.minimum(i, n_p_tiles - 1), 0))
    mode = pl.Buffered(1) if n_s_tiles == 1 else None
    xs_spec = pl.BlockSpec((tm, D_MODEL), lambda i: (jnp.clip(i - n_p_tiles, 0, n_s_tiles - 1), 0),
                           pipeline_mode=mode)
    return xp_spec, xs_spec


TM_IN = TM


def _in_proj(xp, xsm, meta, norm1, w_in):
    t_p, t_s = xp.shape[0], xsm.shape[0]
    n_p, n_s = t_p // TM_IN, t_s // TM_IN
    t = t_p + t_s
    half = 3 * D_CONV + D_SSM
    xp_spec, xs_spec = _two_stream_specs(n_p, n_s, TM_IN)
    g_spec = pl.BlockSpec((1, D_MODEL), lambda i: (0, 0))
    any_spec = pl.BlockSpec(memory_space=pl.ANY)
    row = lambda w: pl.BlockSpec((TM_IN, w), lambda i: (i, 0))
    const = lambda r, w: pl.BlockSpec((r, w), lambda i: (0, 0))
    params = pltpu.CompilerParams(dimension_semantics=("arbitrary",), vmem_limit_bytes=VMEM_LIMIT)
    scratch = [pltpu.VMEM((D_MODEL, half), BF16),
               pltpu.VMEM((2, D_MODEL, WEIGHT_CHUNK), F32),
               pltpu.SemaphoreType.DMA((2,))]

    xb, z, xs, zmeta, xsmeta = pl.pallas_call(
        functools.partial(_in_proj_mix_body, n_p),
        grid=(n_p + n_s,),
        in_specs=[xp_spec, xs_spec, const(N_META, D_MODEL), g_spec, any_spec],
        out_specs=[row(D_CONV), row(D_CONV), row(D_SSM), const(N_META, D_CONV), const(N_META, D_SSM)],
        out_shape=[jax.ShapeDtypeStruct((t, D_CONV), BF16),
                   jax.ShapeDtypeStruct((t, D_CONV), F32),
                   jax.ShapeDtypeStruct((t, D_SSM), F32),
                   jax.ShapeDtypeStruct((N_META, D_CONV), F32),
                   jax.ShapeDtypeStruct((N_META, D_SSM), F32)],
        scratch_shapes=scratch,
        compiler_params=params,
        name="in_proj_mix",
    )(xp, xsm, meta, norm1, w_in)

    sga, sgb = pl.pallas_call(
        functools.partial(_in_proj_gate_body, n_p),
        grid=(n_p + n_s,),
        in_specs=[xp_spec, xs_spec, g_spec, any_spec],
        out_specs=[row(D_MODEL), row(D_MODEL)],
        out_shape=[jax.ShapeDtypeStruct((t, D_MODEL), BF16),
                   jax.ShapeDtypeStruct((t, D_MODEL), BF16)],
        scratch_shapes=scratch,
        compiler_params=params,
        name="in_proj_gate",
    )(xp, xsm, norm1, w_in)
    return xb, z, xs, zmeta, xsmeta, sga, sgb


S5_GROUPS_PER_STEP = LANES // SSM_H
S5_PAIRS_PER_STEP = S5_GROUPS_PER_STEP // 2


def _s5_chunk_mats(lam_re, lam_im, log_dt, b_re, b_im, c_re, c_im):
    dt = jnp.exp(log_dt)[:, None]
    lr, li = lam_re, lam_im
    mag = jnp.exp(lr * dt)
    ab_re, ab_im = mag * jnp.cos(li * dt), mag * jnp.sin(li * dt)
    nr, ni = ab_re - 1.0, ab_im
    den = lr * lr + li * li
    k_re = (nr * lr + ni * li) / den
    k_im = (ni * lr - nr * li) / den
    bb_re = k_re[..., None] * b_re - k_im[..., None] * b_im
    bb_im = k_re[..., None] * b_im + k_im[..., None] * b_re
    return lr * dt, li * dt, bb_re, bb_im, c_re, c_im


def _cmul(ar, ai, br, bi):
    return ar * br - ai * bi, ar * bi + ai * br


def _s5_operators(pair, parity, params, toe_scr, p_scr, q_scr, a16_scr, g_scr, m_scr):
    k = 2 * pair + parity
    zr_ref, zi_ref, bbr_ref, bbi_ref, cr_ref, ci_ref = params
    zr, zi = zr_ref[k:k + 1, :], zi_ref[k:k + 1, :]
    mag = jnp.exp(zr)
    ar, ai = mag * jnp.cos(zi), mag * jnp.sin(zi)
    eye = lax.broadcasted_iota(I32, (SSM_P, SSM_P), 0) == lax.broadcasted_iota(I32, (SSM_P, SSM_P), 1)
    acr = jnp.sum(jnp.where(eye, ar, 0.0), axis=1, keepdims=True)
    aci = jnp.sum(jnp.where(eye, ai, 0.0), axis=1, keepdims=True)
    c_re, c_im = cr_ref[k], ci_ref[k]
    bb_re, bb_im = bbr_ref[k], bbi_ref[k]

    blk = lax.broadcasted_iota(I32, (1, CHUNK_W), 1) >> 4

    pr, pi = jnp.ones_like(ar), jnp.zeros_like(ar)
    pcr, pci = jnp.ones_like(acr), jnp.zeros_like(acr)
    pw_re = jnp.zeros((SSM_P, CHUNK_W), F32)
    pw_im = jnp.zeros((SSM_P, CHUNK_W), F32)
    for d in range(CHUNK + 1):
        g_scr[0, d * SSM_H:(d + 1) * SSM_H, :] = c_re * pr - c_im * pi
        g_scr[1, d * SSM_H:(d + 1) * SSM_H, :] = c_re * pi + c_im * pr
        if d < CHUNK:
            pw_re = jnp.where(blk == CHUNK - 1 - d, pcr, pw_re)
            pw_im = jnp.where(blk == CHUNK - 1 - d, pci, pw_im)
            pcr, pci = _cmul(pcr, pci, acr, aci)
            pr, pi = _cmul(pr, pi, ar, ai)

    half = parity * SSM_P
    a16_scr[pair, 0, :, half:half + SSM_P] = pr
    a16_scr[pair, 1, :, half:half + SSM_P] = pi
    q_scr[k] = jnp.zeros(q_scr.shape[1:], BF16)
    q_scr[k, 0, :, half:half + SSM_P] = g_scr[0, SSM_H:, :].astype(BF16)
    q_scr[k, 1, :, half:half + SSM_P] = (-g_scr[1, SSM_H:, :]).astype(BF16)

    rep = jnp.where(lax.broadcasted_iota(I32, (SSM_H, CHUNK_W), 0)
                    == (lax.broadcasted_iota(I32, (SSM_H, CHUNK_W), 1) & (SSM_H - 1)), 1.0, 0.0).astype(BF16)

    def widen(b):
        b_hi, b_lo = _split_bf16(b)
        return (jnp.dot(b_hi, rep, preferred_element_type=F32) + jnp.dot(b_lo, rep, preferred_element_type=F32))

    bw_re, bw_im = widen(bb_re), widen(bb_im)
    p_re, p_im = _cmul(pw_re, pw_im, bw_re, bw_im)
    p_scr[k, 0:SSM_P, :] = p_re.astype(BF16)
    p_scr[k, SSM_P:2 * SSM_P, :] = p_im.astype(BF16)

    m_scr[...] = _dot3(g_scr[0, 0:CHUNK_W, :], bw_re) - _dot3(g_scr[1, 0:CHUNK_W, :], bw_im)
    for t in range(CHUNK):
        acc = jnp.where(blk == 0, m_scr[t * SSM_H:(t + 1) * SSM_H, :], 0.0)
        for j in range(1, t + 1):
            acc = jnp.where(blk == j, m_scr[(t - j) * SSM_H:(t - j + 1) * SSM_H, :], acc)
        toe_scr[k, t * SSM_H:(t + 1) * SSM_H, :] = acc.astype(BF16)


def _s5_body(n_pc, n_pb, n_sb, xs_ref, xsmeta_ref, zr_ref, zi_ref, bbr_ref, bbi_ref, cr_ref, ci_ref,
             s0re_ref, s0im_ref,
             y_out, pfre_out, pfim_out, sfre_out, sfim_out,
             u_scr, sl_re, sl_im, sp_re, sp_im, yt_scr, toe_ref, p_ref, q_ref, a16_scr,
             g_scr, m_scr):
    gb = S5_GROUPS_PER_STEP
    n_p_rows = n_pb * n_pc
    row_s = n_p_rows
    row_m = row_s + n_sb
    t_p = n_p_rows * CHUNK
    rows_pad = u_scr.shape[2]

    for k in range(gb):
        _s5_operators(k // 2, k % 2, (zr_ref, zi_ref, bbr_ref, bbi_ref, cr_ref, ci_ref),
                      toe_ref, p_ref, q_ref, a16_scr, g_scr, m_scr)

    first_row = lax.broadcasted_iota(I32, (SUBLANES, 1), 0) == 0
    tail = jnp.zeros((rows_pad - row_m - SUBLANES, LANES), F32)
    for t in range(CHUNK):
        meta_rows = jnp.where(first_row, xsmeta_ref[t:t + 1, :], 0.0)
        rows_t = jnp.concatenate([xs_ref[pl.ds(t, n_p_rows, stride=CHUNK), :],
                                  xs_ref[pl.ds(t_p + t, n_sb, stride=CHUNK), :], meta_rows, tail], axis=0)
        xt = rows_t.T.astype(BF16)
        for k in range(gb):
            u_scr[k, t * SSM_H:(t + 1) * SSM_H, :] = xt[k * SSM_H:(k + 1) * SSM_H, :]

    npair = S5_PAIRS_PER_STEP
    for j in range(npair):
        sl0 = jnp.dot(p_ref[2 * j], u_scr[2 * j], preferred_element_type=F32)
        sl1 = jnp.dot(p_ref[2 * j + 1], u_scr[2 * j + 1], preferred_element_type=F32)
        sl_re[j] = jnp.concatenate([sl0[0:SSM_P, :], sl1[0:SSM_P, :]], axis=0).T
        sl_im[j] = jnp.concatenate([sl0[SSM_P:2 * SSM_P, :], sl1[SSM_P:2 * SSM_P, :]], axis=0).T
        sp_re[j, row_m:rows_pad, :] = jnp.zeros((rows_pad - row_m, 2 * SSM_P), F32)
        sp_im[j, row_m:rows_pad, :] = jnp.zeros((rows_pad - row_m, 2 * SSM_P), F32)

    ar = [a16_scr[j, 0] for j in range(npair)]
    ai = [a16_scr[j, 1] for j in range(npair)]
    sre = [jnp.broadcast_to(sl_re[j, row_m:row_m + 1, :], (n_pb, 2 * SSM_P)) for j in range(npair)]
    sim = [jnp.broadcast_to(sl_im[j, row_m:row_m + 1, :], (n_pb, 2 * SSM_P)) for j in range(npair)]
    for c in range(n_pc):
        rows = pl.ds(c, n_pb, stride=n_pc)
        for j in range(npair):
            sp_re[j, rows, :] = sre[j]
            sp_im[j, rows, :] = sim[j]
            nre = ar[j] * sre[j] - ai[j] * sim[j] + sl_re[j, rows, :]
            nim = ar[j] * sim[j] + ai[j] * sre[j] + sl_im[j, rows, :]
            sre[j], sim[j] = nre, nim
    def put_pair(out, j, val):
        out[:, 2 * j, :] = val[:, 0:SSM_P]
        out[:, 2 * j + 1, :] = val[:, SSM_P:2 * SSM_P]

    for j in range(npair):
        put_pair(pfre_out, j, sre[j])
        put_pair(pfim_out, j, sim[j])
        s0r = jnp.concatenate([s0re_ref[:, 2 * j, :], s0re_ref[:, 2 * j + 1, :]], axis=1)
        s0i = jnp.concatenate([s0im_ref[:, 2 * j, :], s0im_ref[:, 2 * j + 1, :]], axis=1)
        sp_re[j, row_s:row_m, :] = s0r
        sp_im[j, row_s:row_m, :] = s0i
        put_pair(sfre_out, j, ar[j] * s0r - ai[j] * s0i + sl_re[j, row_s:row_m, :])
        put_pair(sfim_out, j, ar[j] * s0i + ai[j] * s0r + sl_im[j, row_s:row_m, :])

    nt = (((1,), (1,)), ((), ()))
    for k in range(gb):
        y = jnp.dot(toe_ref[k], u_scr[k], preferred_element_type=F32)
        y += lax.dot_general(q_ref[k, 0], sp_re[k // 2].astype(BF16), nt, preferred_element_type=F32)
        y += lax.dot_general(q_ref[k, 1], sp_im[k // 2].astype(BF16), nt, preferred_element_type=F32)
        for t in range(CHUNK):
            yt_scr[t, k * SSM_H:(k + 1) * SSM_H, :] = y[t * SSM_H:(t + 1) * SSM_H, :]
    for t in range(CHUNK):
        yt = yt_scr[t].T
        y_out[pl.ds(t, n_p_rows, stride=CHUNK), :] = yt[0:n_p_rows, :]
        y_out[pl.ds(t_p + t, n_sb, stride=CHUNK), :] = yt[row_s:row_m, :]


def _s5(xs, xsmeta, state_re, state_im, mats, n_pb, n_sb, seq):
    t = xs.shape[0]
    n_pc = seq // CHUNK
    rows = n_pc * n_pb + n_sb + 1
    rows_pad = -(-rows // LANES) * LANES
    gb, npair = S5_GROUPS_PER_STEP, S5_PAIRS_PER_STEP
    blk3 = lambda n, r, c: pl.BlockSpec((n, r, c), lambda i: (i, 0, 0))
    state = lambda n: pl.BlockSpec((n, gb, SSM_P), lambda i: (0, i, 0))
    y, pfre, pfim, sfre, sfim = pl.pallas_call(
        functools.partial(_s5_body, n_pc, n_pb, n_sb),
        grid=(SSM_G // gb,),
        in_specs=[pl.BlockSpec((t, LANES), lambda i: (0, i)),
                  pl.BlockSpec((N_META, LANES), lambda i: (0, i)),
                  pl.BlockSpec((gb, SSM_P), lambda i: (i, 0)), pl.BlockSpec((gb, SSM_P), lambda i: (i, 0)),
                  blk3(gb, SSM_P, SSM_H), blk3(gb, SSM_P, SSM_H),
                  blk3(gb, SSM_H, SSM_P), blk3(gb, SSM_H, SSM_P),
                  state(n_sb), state(n_sb)],
        out_specs=[pl.BlockSpec((t, LANES), lambda i: (0, i)),
                   state(n_pb), state(n_pb), state(n_sb), state(n_sb)],
        out_shape=[jax.ShapeDtypeStruct((t, D_SSM), F32),
                   jax.ShapeDtypeStruct((n_pb, SSM_G, SSM_P), F32),
                   jax.ShapeDtypeStruct((n_pb, SSM_G, SSM_P), F32),
                   jax.ShapeDtypeStruct((n_sb, SSM_G, SSM_P), F32),
                   jax.ShapeDtypeStruct((n_sb, SSM_G, SSM_P), F32)],
        scratch_shapes=[pltpu.VMEM((gb, CHUNK_W, rows_pad), BF16),
                        pltpu.VMEM((npair, rows_pad, 2 * SSM_P), F32),
                        pltpu.VMEM((npair, rows_pad, 2 * SSM_P), F32),
                        pltpu.VMEM((npair, rows_pad, 2 * SSM_P), F32),
                        pltpu.VMEM((npair, rows_pad, 2 * SSM_P), F32),
                        pltpu.VMEM((CHUNK, LANES, rows_pad), F32),
                        pltpu.VMEM((gb, CHUNK_W, CHUNK_W), BF16),
                        pltpu.VMEM((gb, 2 * SSM_P, CHUNK_W), BF16),
                        pltpu.VMEM((gb, 2, CHUNK_W, 2 * SSM_P), BF16),
                        pltpu.VMEM((npair, 2, 1, 2 * SSM_P), F32),
                        pltpu.VMEM((2, (CHUNK + 1) * SSM_H, SSM_P), F32),
                        pltpu.VMEM((CHUNK_W, CHUNK_W), F32)],
        compiler_params=pltpu.CompilerParams(dimension_semantics=("arbitrary",), vmem_limit_bytes=VMEM_LIMIT),
        name="s5_chunks",
    )(xs, xsmeta, *mats, state_re, state_im)
    return y, pfre[None], pfim[None], sfre[None], sfim[None]


def _mixers_body(n_p_tiles, tiles_per_seq, xb_ref, z_ref, y5_ref, xs_ref, sga_ref, sgb_ref,
                 zmeta_ref, cstate_ref, cw_ref, dskip_ref,
                 wc_hbm, wg_hbm, wso_hbm, merged_out,
                 wc, wg, wso, stage_a, stage_b, sem, carry, inj_scr):
    i = pl.program_id(0)

    @pl.when(i == 0)
    def _():
        _load_weight_bf16(wc_hbm, wc, stage_a, sem)
        _load_weight_bf16(wg_hbm, wg, stage_b, sem)
        _load_weight_bf16(wso_hbm, wso, stage_a, sem)

    @pl.when(jnp.logical_and(i < n_p_tiles, i % tiles_per_seq == 0))
    def _():
        carry[0:2, :] = zmeta_ref[N_META - 2:N_META, :]

    is_s = i >= n_p_tiles

    @pl.when(i == 0)
    def _():
        inj_scr[...] = jnp.zeros(inj_scr.shape, F32)

    @pl.when(is_s)
    def _():
        seqs = TM // CHUNK
        rr = lax.broadcasted_iota(I32, (TM, seqs), 0)
        cc = lax.broadcasted_iota(I32, (TM, seqs), 1) * CHUNK
        at0 = jnp.where(rr == cc, 1.0, 0.0).astype(BF16)
        at1 = jnp.where(rr == cc + 1, 1.0, 0.0).astype(BF16)

        def place(sel, rows):
            r_hi, r_lo = _split_bf16(rows)
            return (jnp.dot(sel, r_hi, preferred_element_type=F32)
                    + jnp.dot(sel, r_lo, preferred_element_type=F32))

        old, new = cstate_ref[:, 0:D_CONV], cstate_ref[:, D_CONV:2 * D_CONV]
        inj_scr[0] = place(at0, new)
        inj_scr[1] = place(at0, old) + place(at1, new)

    z = z_ref[...]
    row = lax.broadcasted_iota(I32, (TM, 1), 0)
    r1 = pltpu.roll(z, 1, 0)
    r2 = pltpu.roll(z, 2, 0)
    c1 = carry[1:2, :]
    c2 = carry[0:1, :]
    pos = jnp.where(is_s, row & (CHUNK - 1), row)
    first1 = pos == 0
    first2 = pos < 2
    fill1 = jnp.where(is_s, inj_scr[0], jnp.broadcast_to(c1, z.shape))
    fill2 = jnp.where(is_s, inj_scr[1], jnp.where(row == 0, c2, c1))
    zp1 = jnp.where(first1, fill1, r1)
    zp2 = jnp.where(first2, fill2, r2)
    carry[0:2, :] = z[TM - 2:TM, :]

    cw = cw_ref[...]
    conv = cw[0:1, :] * zp2 + cw[1:2, :] * zp1 + cw[2:3, :] * z
    a_in = (xb_ref[...].astype(F32) * conv).astype(BF16)
    ya = jnp.dot(a_in, wc[...], preferred_element_type=F32)

    ys = y5_ref[...] + dskip_ref[...] * xs_ref[...]
    ys = _gelu_tanh(ys)
    glu = jnp.dot(ys.astype(BF16), wg[...], preferred_element_type=F32)
    ys = ys * _sigmoid(glu)
    yb = jnp.dot(ys.astype(BF16), wso[...], preferred_element_type=F32)

    merged = sga_ref[...].astype(F32) * ya + sgb_ref[...].astype(F32) * yb
    merged_out[...] = merged.astype(BF16)


def _mixers(xb, z, y5, xs, sga, sgb, zmeta, conv_state, conv_w, d_skip, w_conv_out, w_glu, w_ssm_out,
            n_p_tiles, tiles_per_seq):
    t = xb.shape[0]
    seqs = TM // CHUNK
    n_s_tiles = conv_state.shape[0] // seqs
    row = lambda w: pl.BlockSpec((TM, w), lambda i: (i, 0))
    const = lambda r, w: pl.BlockSpec((r, w), lambda i: (0, 0))
    cstate = pl.BlockSpec((seqs, 2 * D_CONV), lambda i: (jnp.clip(i - n_p_tiles, 0, n_s_tiles - 1), 0))
    any_spec = pl.BlockSpec(memory_space=pl.ANY)
    return pl.pallas_call(
        functools.partial(_mixers_body, n_p_tiles, tiles_per_seq),
        grid=(t // TM,),
        in_specs=[row(D_CONV), row(D_CONV), row(D_SSM), row(D_SSM), row(D_MODEL), row(D_MODEL),
                  const(N_META, D_CONV), cstate, const(3, D_CONV), const(1, D_SSM),
                  any_spec, any_spec, any_spec],
        out_specs=row(D_MODEL),
        out_shape=jax.ShapeDtypeStruct((t, D_MODEL), BF16),
        scratch_shapes=[pltpu.VMEM((D_CONV, D_MODEL), BF16),
                        pltpu.VMEM((D_SSM, D_SSM), BF16),
                        pltpu.VMEM((D_SSM, D_MODEL), BF16),
                        pltpu.VMEM((2, 256, D_MODEL), F32),
                        pltpu.VMEM((2, 256, D_SSM), F32),
                        pltpu.SemaphoreType.DMA((2,)),
                        pltpu.VMEM((8, D_CONV), F32),
                        pltpu.VMEM((2, TM, D_CONV), F32)],
        compiler_params=pltpu.CompilerParams(dimension_semantics=("arbitrary",), vmem_limit_bytes=VMEM_LIMIT),
        name="mixers",
    )(xb, z, y5, xs, sga, sgb, zmeta, conv_state, conv_w, d_skip, w_conv_out, w_glu, w_ssm_out)


ROUTE_W = LANES
COARSE0 = N_EXPERTS


def _route(logits, cnt):
    col = lax.broadcasted_iota(I32, logits.shape, 1)
    colf = col.astype(F32)
    neg = jnp.float32(-jnp.inf)
    big = jnp.float32(1 << 20)
    is_c = jnp.logical_and(col >= COARSE0, col < COARSE0 + N_EGROUPS)
    lc = jnp.where(is_c, logits, neg)
    cmax = jnp.max(lc, axis=-1, keepdims=True)
    gi = jnp.min(jnp.where(lc == cmax, colf - COARSE0, big), axis=-1, keepdims=True)
    pg = 1.0 / jnp.sum(jnp.where(is_c, jnp.exp(lc - cmax), 0.0), axis=-1, keepdims=True)
    grp = (col >> 3).astype(F32)
    in_g = jnp.logical_and(col < N_EXPERTS, grp == gi)
    lf = jnp.where(in_g, logits, neg)
    m1 = jnp.max(lf, axis=-1, keepdims=True)
    i1 = jnp.min(jnp.where(lf == m1, colf, big), axis=-1, keepdims=True)
    lf2 = jnp.where(colf == i1, neg, lf)
    m2 = jnp.max(lf2, axis=-1, keepdims=True)
    i2 = jnp.min(jnp.where(lf2 == m2, colf, big), axis=-1, keepdims=True)
    e2 = jnp.exp(m2 - m1)
    w1 = pg / (1.0 + e2)
    w2 = pg * e2 / (1.0 + e2)
    n = logits.shape[0]
    hit1 = colf == i1
    hit2 = colf == i2
    onehot = jnp.where(jnp.logical_or(hit1, hit2), 1.0, 0.0)
    rr = lax.broadcasted_iota(I32, (n, n), 0)
    cc = lax.broadcasted_iota(I32, (n, n), 1)
    tri = jnp.where(cc < rr, 1.0, 0.0).astype(BF16)
    pos = jnp.dot(tri, onehot.astype(BF16), preferred_element_type=F32) + cnt
    rank1 = jnp.sum(jnp.where(hit1, pos, 0.0), axis=-1, keepdims=True)
    rank2 = jnp.sum(jnp.where(hit2, pos, 0.0), axis=-1, keepdims=True)
    vals = (i1, i2, w1, w2, rank1, rank2)
    rec = jnp.zeros(logits.shape, F32)
    for c, val in enumerate(vals):
        rec = jnp.where(col == c, val, rec)
    return rec, cnt + jnp.sum(onehot, axis=0, keepdims=True)


def _out_proj_body(n_p_tiles, merged_ref, xp_ref, xsm_ref, g2_ref, wr_ref, wo_hbm,
                   h2_out, v_out, rec_out, ids_out, cnt_out, wo, stage, sem, h2_scr):
    i = pl.program_id(0)
    n = pl.num_programs(0) - 1

    def route_prev():
        v = _rmsnorm(h2_scr[...], g2_ref[...])
        v_out[...] = _pack_halves(v)
        v_hi, v_lo = _split_bf16(v)
        both = jnp.dot(v_hi, wr_ref[...], preferred_element_type=F32)
        logits = (both[:, :ROUTE_W] + both[:, ROUTE_W:]
                  + jnp.dot(v_lo, wr_ref[:, :ROUTE_W], preferred_element_type=F32))
        rec, cnt = _route(logits, cnt_out[...])
        rec_out[...] = rec
        ids_out[...] = rec[:, 0:SUBLANES]
        cnt_out[...] = cnt

    def project():
        x = jnp.where(i < n_p_tiles, xp_ref[...], xsm_ref[...])
        h2 = x + jnp.dot(merged_ref[...], wo[...], preferred_element_type=F32)
        h2_out[...] = h2
        return h2

    @pl.when(i == 0)
    def _():
        cnt_out[...] = jnp.zeros(cnt_out.shape, F32)
        cw = stage.shape[2]
        merged = merged_ref[...]

        def on_chunk(c):
            cols = slice(c * cw, (c + 1) * cw)
            h2 = xp_ref[:, cols] + jnp.dot(merged, wo[:, cols], preferred_element_type=F32)
            h2_out[:, cols] = h2
            h2_scr[:, cols] = h2

        _stream_weight_bf16(wo_hbm, wo, stage, sem, 0, on_chunk)

    @pl.when(jnp.logical_and(i > 0, i < n))
    def _():
        route_prev()
        h2_scr[...] = project()

    @pl.when(i == n)
    def _():
        route_prev()


def _out_proj(merged, xp, xsm, norm2, w_router, w_o):
    t_p, t_s = xp.shape[0], xsm.shape[0]
    n_p, n_s = t_p // TM, t_s // TM
    n = n_p + n_s
    t = t_p + t_s
    xp_spec, xs_spec = _two_stream_specs(n_p, n_s)
    cur = lambda w: pl.BlockSpec((TM, w), lambda i: (jnp.minimum(i, n - 1), 0))
    prev = lambda w: pl.BlockSpec((TM, w), lambda i: (jnp.maximum(i - 1, 0), 0))
    const = lambda r, w: pl.BlockSpec((r, w), lambda i: (0, 0))
    return pl.pallas_call(
        functools.partial(_out_proj_body, n_p),
        grid=(n + 1,),
        in_specs=[cur(D_MODEL), xp_spec, xs_spec, const(1, D_MODEL), const(D_MODEL, 2 * ROUTE_W),
                  pl.BlockSpec(memory_space=pl.ANY)],
        out_specs=[cur(D_MODEL), prev(HALF), prev(ROUTE_W), prev(SUBLANES), const(1, ROUTE_W)],
        out_shape=[jax.ShapeDtypeStruct((t, D_MODEL), F32),
                   jax.ShapeDtypeStruct((t, HALF), U32),
                   jax.ShapeDtypeStruct((t, ROUTE_W), F32),
                   jax.ShapeDtypeStruct((t, SUBLANES), F32),
                   jax.ShapeDtypeStruct((1, ROUTE_W), F32)],
        scratch_shapes=[pltpu.VMEM((D_MODEL, D_MODEL), BF16),
                        pltpu.VMEM((2, D_MODEL, WEIGHT_CHUNK), F32),
                        pltpu.SemaphoreType.DMA((2,)),
                        pltpu.VMEM((TM, D_MODEL), F32)],
        compiler_params=pltpu.CompilerParams(dimension_semantics=("arbitrary",), vmem_limit_bytes=VMEM_LIMIT),
        name="out_proj_route",
    )(merged, xp, xsm, norm2, w_router, w_o)


def _moe_plan(rec, cnt, n_tiles):
    t = rec.shape[0]
    n_pairs = 2 * t
    eid = rec[:, 0:2].astype(I32).reshape(-1)
    rank = rec[:, 4:6].astype(I32).reshape(-1)
    counts = cnt[0, :N_EXPERTS].astype(I32)
    pair_start = jnp.cumsum(counts) - counts
    experts = jnp.arange(N_EXPERTS, dtype=I32)
    onehot = (eid[:, None] == experts[None, :]).astype(I32)
    pos = rank + jnp.sum(onehot * pair_start[None, :], axis=1)
    _, order = lax.sort((pos, jnp.arange(n_pairs, dtype=I32)), num_keys=1)
    tiles_e = (counts + TM_MOE - 1) // TM_MOE
    tile_end = jnp.cumsum(tiles_e)
    tile_start = tile_end - tiles_e
    n_valid = tile_end[-1]
    tile_ids = jnp.arange(n_tiles, dtype=I32)
    tile_e = jnp.sum((tile_ids[:, None] >= tile_end[None, :]).astype(I32), axis=1)
    last_e = jnp.sum((n_valid - 1 >= tile_end).astype(I32))
    tile_e = jnp.minimum(jnp.where(tile_ids < n_valid, tile_e, last_e), N_EXPERTS - 1)
    tile_onehot = (tile_e[:, None] == experts[None, :]).astype(I32)
    tile_q0 = (tile_ids - jnp.sum(tile_onehot * tile_start[None, :], axis=1)) * TM_MOE
    tile_q0 = jnp.where(tile_ids < n_valid, tile_q0, 0)
    dst = rank + jnp.sum(onehot * (tile_start * TM_MOE)[None, :], axis=1)
    pad_start = tile_start * TM_MOE + counts
    pad_len = tiles_e * TM_MOE - counts
    return (dst.astype(I32), pad_start.astype(I32), pad_len.astype(I32), tile_e, tile_q0.astype(I32),
            n_valid.astype(I32).reshape(1), pair_start, counts, order)


DISPATCH_BUFS = 3
ROW_PIECES = tuple(TM_MOE >> (b + 1) for b in range(TM_MOE.bit_length() - 1))


def _dispatch_body(dst_ref, pad_start_ref, pad_len_ref, nvalid_ref, v_hbm, x_hbm, buf, zbuf, rsem, ssem, zsem):
    i = pl.program_id(0)
    n = pl.num_programs(0)

    def read(tile, slot):
        return pltpu.make_async_copy(v_hbm.at[pl.ds(pl.multiple_of(tile * TM, TM), TM)], buf.at[slot],
                                     rsem.at[slot])

    def row_write(slot, r, dst_row):
        return pltpu.make_async_copy(buf.at[slot, pl.ds(r, 1)], x_hbm.at[pl.ds(dst_row, 1)], ssem.at[slot])

    def drain(slot):
        for _ in range(2):
            pltpu.make_async_copy(buf.at[slot], x_hbm.at[pl.ds(0, TM)], ssem.at[slot]).wait()

    def pad_fill(go):
        def zero_rows(start, size):
            d = pltpu.make_async_copy(zbuf.at[pl.ds(0, size)], x_hbm.at[pl.ds(start, size)], zsem)
            d.start() if go else d.wait()

        def body(e, c):
            start, length = pad_start_ref[e], pad_len_ref[e]
            head = (-start) & (SUBLANES - 1)
            for h in range(SUBLANES - 1):
                @pl.when(h < head)
                def _(h=h):
                    zero_rows(start + h, 1)
            start, length = start + head, length - head
            for size in ROW_PIECES:
                if size >= SUBLANES:
                    @pl.when((length & size) != 0)
                    def _(size=size):
                        zero_rows(pl.multiple_of(start + (length & (-2 * size)), SUBLANES), size)
            return c
        lax.fori_loop(0, N_EXPERTS, body, 0)

        def unused(tile, c):
            for half in range(TM_MOE // ROW_PIECES[0]):
                zero_rows(pl.multiple_of(tile * TM_MOE + half * ROW_PIECES[0], SUBLANES), ROW_PIECES[0])
            return c
        lax.fori_loop(nvalid_ref[0], x_hbm.shape[0] // TM_MOE, unused, 0)

    @pl.when(i == 0)
    def _():
        zbuf[...] = jnp.zeros(zbuf.shape, U32)
        pad_fill(True)
        read(0, 0).start()

    @pl.when(i >= 2)
    def _():
        drain((i + 1) % DISPATCH_BUFS)

    @pl.when(i + 1 < n)
    def _():
        read(i + 1, (i + 1) % DISPATCH_BUFS).start()

    slot = i % DISPATCH_BUFS
    read(i, slot).wait()
    for r in range(TM):
        for k in range(2):
            row_write(slot, r, dst_ref[2 * (i * TM + r) + k]).start(priority=k)

    @pl.when(i == n - 1)
    def _():
        if n >= 2:
            drain((i - 1) % DISPATCH_BUFS)
        drain(slot)
        pad_fill(False)


def _dispatch(v, plan, n_tiles):
    dst, pad_start, pad_len, n_valid = plan[0], plan[1], plan[2], plan[5]
    t = v.shape[0]
    return pl.pallas_call(
        _dispatch_body,
        grid_spec=pltpu.PrefetchScalarGridSpec(
            num_scalar_prefetch=4,
            grid=(t // TM,),
            in_specs=[pl.BlockSpec(memory_space=pl.ANY)],
            out_specs=pl.BlockSpec(memory_space=pl.ANY),
            scratch_shapes=[pltpu.VMEM((DISPATCH_BUFS, TM, HALF), U32),
                            pltpu.VMEM((ROW_PIECES[0], HALF), U32),
                            pltpu.SemaphoreType.DMA((DISPATCH_BUFS,)),
                            pltpu.SemaphoreType.DMA((DISPATCH_BUFS,)),
                            pltpu.SemaphoreType.DMA(())]),
        out_shape=jax.ShapeDtypeStruct((n_tiles * TM_MOE, HALF), U32),
        compiler_params=pltpu.CompilerParams(dimension_semantics=("arbitrary",), vmem_limit_bytes=VMEM_LIMIT),
        name="moe_dispatch",
    )(dst, pad_start, pad_len, n_valid, v)


WEIGHT_SLOTS = 3


def _moe_body(n_tok, plane, tile_e_ref, tile_q0_ref, nvalid_ref, pstart_ref, cnt_ref, orow_ref,
              krank_ref, elist_ref, nexp_ref,
              x_ref, wg_hbm, wu_hbm, wd_hbm, o_hbm, ybuf0, ybuf1, ybuf2, ssem,
              stage_g, stage_u, stage_d, wsem, wg, wu, wd):
    i = pl.program_id(0)
    nv = nvalid_ref[0]
    bufs = (ybuf0, ybuf1, ybuf2)

    def scratch_row0(slot):
        return (slot & 1) * plane + n_tok + (slot >> 1) * TM_MOE

    def row_write(slot, r, dst_row):
        return pltpu.make_async_copy(bufs[slot].at[pl.ds(r, 1)], o_hbm.at[pl.ds(dst_row, 1)], ssem.at[slot])

    def scratch_rows(region):
        return pltpu.make_async_copy(ybuf0, o_hbm.at[pl.ds(scratch_row0(region), TM_MOE)], ssem.at[0])

    def start_writes(tile, slot, rows):
        e = tile_e_ref[tile]
        valid = cnt_ref[e] - tile_q0_ref[tile]
        first = pstart_ref[e] + tile_q0_ref[tile]
        for r in rows:
            row_write(slot, r, jnp.where(r < valid, orow_ref[first + r], scratch_row0(slot) + r)).start(
                priority=r % 2)

    def compute(slot, writes=None):
        quarter = TM_MOE // 4
        batch = lambda q: start_writes(*writes, range(q * quarter, (q + 1) * quarter)) if writes else None
        x_lo, x_hi = (h.astype(BF16) for h in _unpack_halves(x_ref[...]))
        batch(0)
        hg = (jnp.dot(x_lo, wg[0:HALF, :], preferred_element_type=F32)
              + jnp.dot(x_hi, wg[HALF:D_MODEL, :], preferred_element_type=F32))
        batch(1)
        hu = (jnp.dot(x_lo, wu[0:HALF, :], preferred_element_type=F32)
              + jnp.dot(x_hi, wu[HALF:D_MODEL, :], preferred_element_type=F32))
        batch(2)
        act = hg * _sigmoid(hg) * hu
        y = jnp.dot(act.astype(BF16), wd[...], preferred_element_type=F32)
        batch(3)
        bufs[slot][...] = _pack_halves(y)

    @pl.when(i == 0)
    def _():
        ybuf0[...] = jnp.zeros(ybuf0.shape, U32)
        for region in range(4):
            scratch_rows(region).start()
        for region in range(4):
            scratch_rows(region).wait()

    @pl.when(jnp.logical_and(i >= 3, i < nv + 3))
    def _():
        pltpu.make_async_copy(ybuf0, o_hbm.at[pl.ds(0, TM_MOE)], ssem.at[i % 3]).wait()

    def weight_copies(k, go):
        e = elist_ref[k]
        slot = k % WEIGHT_SLOTS
        for w_hbm, st in ((wg_hbm, stage_g), (wu_hbm, stage_u), (wd_hbm, stage_d)):
            d = pltpu.make_async_copy(w_hbm.at[e], st.at[slot], wsem.at[slot])
            d.start() if go else d.wait()

    @pl.when(i == 0)
    def _():
        for k in range(WEIGHT_SLOTS):
            @pl.when(k < nexp_ref[0])
            def _(k=k):
                weight_copies(k, True)

    @pl.when(i < nv)
    def _():
        prev_e = tile_e_ref[jnp.maximum(i - 1, 0)]

        @pl.when(jnp.logical_or(i == 0, tile_e_ref[i] != prev_e))
        def _():
            k = krank_ref[tile_e_ref[i]]
            slot = k % WEIGHT_SLOTS
            weight_copies(k, False)
            wg[...] = stage_g[slot].astype(BF16)
            wu[...] = stage_u[slot].astype(BF16)
            wd[...] = stage_d[slot].astype(BF16)

            @pl.when(k + WEIGHT_SLOTS < nexp_ref[0])
            def _():
                weight_copies(k + WEIGHT_SLOTS, True)

    @pl.when(i == 0)
    def _():
        compute(0)

    for slot in range(3):
        prev = (slot + 2) % 3

        @pl.when(jnp.logical_and(i % 3 == slot, jnp.logical_and(i >= 1, i < nv)))
        def _(slot=slot, prev=prev):
            compute(slot, writes=(i - 1, prev))

        @pl.when(jnp.logical_and(i % 3 == slot, i == nv))
        def _(prev=prev):
            start_writes(i - 1, prev, range(TM_MOE))


def _moe(x_disp, plan, w_gate, w_up, w_down, n_tiles, n_tok):
    tile_e, tile_q0, n_valid, pair_start, counts, order = plan[3:9]
    plane = n_tok + 2 * TM_MOE
    orow = jnp.pad((order & 1) * plane + (order >> 1), (0, TM_MOE))
    present = (counts > 0).astype(I32)
    krank = jnp.cumsum(present) - present
    experts = jnp.arange(N_EXPERTS, dtype=I32)
    elist = jnp.sum(jnp.where((krank[None, :] == experts[:, None]) & (present[None, :] > 0), experts[None, :], 0),
                    axis=1).astype(I32)
    nexp = jnp.sum(present).astype(I32).reshape(1)
    tile = lambda i, nv: jnp.minimum(i, jnp.maximum(nv[0] - 1, 0))
    any_spec = pl.BlockSpec(memory_space=pl.ANY)
    ybuf = pltpu.VMEM((TM_MOE, HALF), U32)
    return pl.pallas_call(
        functools.partial(_moe_body, n_tok, plane),
        grid_spec=pltpu.PrefetchScalarGridSpec(
            num_scalar_prefetch=9,
            grid=(n_tiles + 3,),
            in_specs=[pl.BlockSpec((TM_MOE, HALF), lambda i, te, tq, nv, *_: (tile(i, nv), 0)),
                      any_spec, any_spec, any_spec],
            out_specs=pl.BlockSpec(memory_space=pl.ANY),
            scratch_shapes=[ybuf, ybuf, ybuf,
                            pltpu.SemaphoreType.DMA((3,)),
                            pltpu.VMEM((WEIGHT_SLOTS, D_MODEL, D_EXPERT), F32),
                            pltpu.VMEM((WEIGHT_SLOTS, D_MODEL, D_EXPERT), F32),
                            pltpu.VMEM((WEIGHT_SLOTS, D_EXPERT, D_MODEL), F32),
                            pltpu.SemaphoreType.DMA((WEIGHT_SLOTS,)),
                            pltpu.VMEM((D_MODEL, D_EXPERT), BF16),
                            pltpu.VMEM((D_MODEL, D_EXPERT), BF16),
                            pltpu.VMEM((D_EXPERT, D_MODEL), BF16)]),
        out_shape=jax.ShapeDtypeStruct((2 * plane, HALF), U32),
        compiler_params=pltpu.CompilerParams(dimension_semantics=("arbitrary",), vmem_limit_bytes=VMEM_LIMIT),
        name="moe_experts",
    )(tile_e, tile_q0, n_valid, pair_start, counts, orow, krank.astype(I32), elist, nexp,
      x_disp, w_gate, w_up, w_down)


def _combine_body(n_p_tiles, h2_ref, rec_ref, gf_ref, y1_ref, y2_ref, outp_ref, outs_ref):
    i = pl.program_id(0)
    rec = rec_ref[...]
    y1 = jnp.concatenate(_unpack_halves(y1_ref[...]), axis=-1)
    y2 = jnp.concatenate(_unpack_halves(y2_ref[...]), axis=-1)
    h = rec[:, 2:3] * y1 + rec[:, 3:4] * y2
    out = _rmsnorm(h2_ref[...] + h, gf_ref[...])

    @pl.when(i < n_p_tiles)
    def _():
        outp_ref[...] = out

    @pl.when(i >= n_p_tiles)
    def _():
        outs_ref[...] = out


def _combine(h2, rec, y_pairs, final_norm, t_p, t_s):
    n_p, n_s = t_p // TM, t_s // TM
    plane_tiles = y_pairs.shape[0] // 2 // TM
    return pl.pallas_call(
        functools.partial(_combine_body, n_p),
        grid=(n_p + n_s,),
        in_specs=[pl.BlockSpec((TM, D_MODEL), lambda i: (i, 0)),
                  pl.BlockSpec((TM, ROUTE_W), lambda i: (i, 0)),
                  pl.BlockSpec((1, D_MODEL), lambda i: (0, 0)),
                  pl.BlockSpec((TM, HALF), lambda i: (i, 0)),
                  pl.BlockSpec((TM, HALF), lambda i: (i + plane_tiles, 0))],
        out_specs=[pl.BlockSpec((TM, D_MODEL), lambda i: (jnp.minimum(i, n_p - 1), 0)),
                   pl.BlockSpec((TM, D_MODEL), lambda i: (jnp.clip(i - n_p, 0, n_s - 1), 0))],
        out_shape=[jax.ShapeDtypeStruct((t_p, D_MODEL), F32),
                   jax.ShapeDtypeStruct((t_s, D_MODEL), F32)],
        compiler_params=pltpu.CompilerParams(dimension_semantics=("arbitrary",), vmem_limit_bytes=VMEM_LIMIT),
        name="combine_norm",
    )(h2, rec, final_norm, y_pairs, y_pairs)


def kernel(x_prompt, x_sample, state_conv, state_ssm_re, state_ssm_im, meta_tokens, norm1, w_in, conv_w,
           lam_re, lam_im, log_dt, ssm_b_re, ssm_b_im, ssm_c_re, ssm_c_im, ssm_d, w_glu, w_conv_out,
           w_ssm_out, w_o, norm2, w_coarse, w_fine, w_gate, w_up, w_down, final_norm):
    n_pb, seq, _ = x_prompt.shape
    n_sb, dec_seq, _ = x_sample.shape
    assert dec_seq == CHUNK and seq % TM == 0 and (n_sb * dec_seq) % TM == 0 and N_META == CHUNK
    t_p, t_s = n_pb * seq, n_sb * dec_seq
    xp = x_prompt.reshape(t_p, D_MODEL)
    xsm = x_sample.reshape(t_s, D_MODEL)

    xb, z, xs, zmeta, xsmeta, sga, sgb = _in_proj(xp, xsm, meta_tokens, norm1, w_in[0])

    mats = _s5_chunk_mats(lam_re[0], lam_im[0], log_dt[0], ssm_b_re[0], ssm_b_im[0], ssm_c_re[0], ssm_c_im[0])
    y5, pf_re, pf_im, sf_re, sf_im = _s5(xs, xsmeta, state_ssm_re[0], state_ssm_im[0], mats, n_pb, n_sb, seq)

    merged = _mixers(xb, z, y5, xs, sga, sgb, zmeta, state_conv[0].reshape(n_sb, 2 * D_CONV), conv_w[0], ssm_d,
                     w_conv_out[0], w_glu[0], w_ssm_out[0], t_p // TM, seq // TM)

    w_router = jnp.concatenate(
        [w_fine[0], w_coarse[0], jnp.zeros((D_MODEL, ROUTE_W - N_EXPERTS - N_EGROUPS), F32)], axis=1)
    w_router = jnp.concatenate(_split_bf16(w_router), axis=1)
    h2, v, rec, ids, cnt = _out_proj(merged, xp, xsm, norm2, w_router, w_o[0])

    n_tiles = 2 * (t_p + t_s) // TM_MOE + N_EXPERTS
    plan = _moe_plan(ids, cnt, n_tiles)
    x_disp = _dispatch(v, plan, n_tiles)
    y_pairs = _moe(x_disp, plan, w_gate[0], w_up[0], w_down[0], n_tiles, t_p + t_s)
    y_p, y_s = _combine(h2, rec, y_pairs, final_norm.reshape(1, D_MODEL), t_p, t_s)

    new_conv_p = jnp.stack([z[(b + 1) * seq - 2:(b + 1) * seq] for b in range(n_pb)])
    new_conv_s = z[t_p:].reshape(n_sb, dec_seq, D_CONV)[:, dec_seq - 2:]
    return (y_p.reshape(n_pb, seq, D_MODEL), y_s.reshape(n_sb, dec_seq, D_MODEL),
            new_conv_p[None], pf_re, pf_im, new_conv_s[None], sf_re, sf_im)
```

```python
import functools

import jax
import jax.numpy as jnp
from jax import lax
from jax.experimental import pallas as pl
from jax.experimental.pallas import tpu as pltpu

F32 = jnp.float32
BF16 = jnp.bfloat16
I32 = jnp.int32
U32 = jnp.uint32

D_MODEL = 2048
D_CONV = 1024
D_SSM = 1024
SSM_H = 16
SSM_G = 64
SSM_P = 64
N_META = 16
N_EGROUPS = 4
EXPERTS_PER_GROUP = 8
N_EXPERTS = 32
D_EXPERT = 256
EPS = 1e-6

LANES = 128
SUBLANES = 8

CHUNK = 16
CHUNK_W = CHUNK * SSM_H

TM = 256
TM_MOE = 256
VMEM_LIMIT = 52 * 1024 * 1024


def _rmsnorm(x, g):
    return x * lax.rsqrt(jnp.mean(x * x, axis=-1, keepdims=True) + EPS) * g


def _sigmoid(x):
    return 1.0 / (1.0 + jnp.exp(-x))


def _gelu_tanh(x):
    c = 0.7978845608028654
    return 0.5 * x * (1.0 + jnp.tanh(c * (x + 0.044715 * (x * x * x))))


def _split_bf16(a):
    hi = a.astype(BF16)
    lo = (a - hi.astype(F32)).astype(BF16)
    return hi, lo


def _dot3(a, b):
    a_hi, a_lo = _split_bf16(a)
    b_hi, b_lo = _split_bf16(b)
    return (jnp.dot(a_hi, b_hi, preferred_element_type=F32) + jnp.dot(a_lo, b_hi, preferred_element_type=F32)
            + jnp.dot(a_hi, b_lo, preferred_element_type=F32))


HALF = D_MODEL // 2


def _pack_halves(a):
    return pltpu.pack_elementwise([a[:, :HALF], a[:, HALF:]], packed_dtype=BF16)


def _unpack_halves(p):
    return (pltpu.unpack_elementwise(p, index=0, packed_dtype=BF16, unpacked_dtype=F32),
            pltpu.unpack_elementwise(p, index=1, packed_dtype=BF16, unpacked_dtype=F32))


def _weight_copy(w_hbm, stage, sem, c, slot, rows, col0, ncols):
    return pltpu.make_async_copy(
        w_hbm.at[pl.ds(c * rows, rows), pl.ds(col0, ncols)], stage.at[slot], sem.at[slot])


def _load_weight_bf16(w_hbm, w_vmem, stage, sem, col0=0):
    k, n = w_vmem.shape
    rows = stage.shape[1]
    nchunk = k // rows
    _weight_copy(w_hbm, stage, sem, 0, 0, rows, col0, n).start()
    for c in range(nchunk):
        slot = c % 2
        if c + 1 < nchunk:
            _weight_copy(w_hbm, stage, sem, c + 1, 1 - slot, rows, col0, n).start()
        _weight_copy(w_hbm, stage, sem, c, slot, rows, col0, n).wait()
        w_vmem[pl.ds(c * rows, rows), :] = stage[slot].astype(BF16)


WEIGHT_CHUNK = 512


def _stream_weight_bf16(w_hbm, w_vmem, stage, sem, col0, on_chunk):
    _, n = w_vmem.shape
    w = stage.shape[2]
    nchunk = n // w

    def copy(c, slot):
        return pltpu.make_async_copy(w_hbm.at[:, pl.ds(col0 + c * w, w)], stage.at[slot], sem.at[slot])

    copy(0, 0).start()
    for c in range(nchunk):
        slot = c % 2
        if c + 1 < nchunk:
            copy(c + 1, 1 - slot).start()
        copy(c, slot).wait()
        w_vmem[:, c * w:(c + 1) * w] = stage[slot].astype(BF16)
        on_chunk(c)


def _in_proj_mix_body(n_p_tiles, xp_ref, xsm_ref, meta_ref, g_ref, w_hbm,
                      xb_out, z_out, xs_out, zmeta_out, xsmeta_out,
                      w_vmem, stage, sem):
    i = pl.program_id(0)
    g = g_ref[...]
    cw = stage.shape[2]

    @pl.when(i == 0)
    def _():
        u = _rmsnorm(xp_ref[...], g).astype(BF16)
        um = _rmsnorm(meta_ref[...], g).astype(BF16)

        def on_chunk(c):
            n0 = c * cw
            cols = slice(n0 % D_CONV, n0 % D_CONV + cw)
            p = jnp.dot(u, w_vmem[:, n0:n0 + cw], preferred_element_type=F32)
            pm = jnp.dot(um, w_vmem[:, n0:n0 + cw], preferred_element_type=F32)
            if n0 < D_CONV:
                xb_out[:, cols] = p.astype(BF16)
            elif n0 < 2 * D_CONV:
                z_out[:, cols] = p
                zmeta_out[:, cols] = pm
            elif n0 < 3 * D_CONV:
                z_out[:, cols] = z_out[:, cols] * p
                zmeta_out[:, cols] = zmeta_out[:, cols] * pm
            else:
                xs_out[:, cols] = p
                xsmeta_out[:, cols] = pm

        _stream_weight_bf16(w_hbm, w_vmem, stage, sem, 0, on_chunk)

    @pl.when(i > 0)
    def _():
        x = jnp.where(i < n_p_tiles, xp_ref[...], xsm_ref[...])
        u = _rmsnorm(x, g).astype(BF16)
        xb = jnp.dot(u, w_vmem[:, 0:D_CONV], preferred_element_type=F32)
        xc = jnp.dot(u, w_vmem[:, D_CONV:2 * D_CONV], preferred_element_type=F32)
        xv = jnp.dot(u, w_vmem[:, 2 * D_CONV:3 * D_CONV], preferred_element_type=F32)
        xs = jnp.dot(u, w_vmem[:, 3 * D_CONV:3 * D_CONV + D_SSM], preferred_element_type=F32)
        xb_out[...] = xb.astype(BF16)
        z_out[...] = xc * xv
        xs_out[...] = xs


def _in_proj_gate_body(n_p_tiles, xp_ref, xsm_ref, g_ref, w_hbm, ga_out, gb_out,
                       w_vmem, stage, sem):
    i = pl.program_id(0)
    cw = stage.shape[2]

    @pl.when(i == 0)
    def _():
        u = _rmsnorm(xp_ref[...], g_ref[...]).astype(BF16)

        def on_chunk(c):
            n0 = c * cw
            out = ga_out if n0 < D_MODEL else gb_out
            gate = _sigmoid(jnp.dot(u, w_vmem[:, n0:n0 + cw], preferred_element_type=F32))
            out[:, n0 % D_MODEL:n0 % D_MODEL + cw] = gate.astype(BF16)

        _stream_weight_bf16(w_hbm, w_vmem, stage, sem, 3 * D_CONV + D_SSM, on_chunk)

    @pl.when(i > 0)
    def _():
        x = jnp.where(i < n_p_tiles, xp_ref[...], xsm_ref[...])
        u = _rmsnorm(x, g_ref[...]).astype(BF16)
        gates = jnp.dot(u, w_vmem[...], preferred_element_type=F32)
        ga_out[...] = _sigmoid(gates[:, 0:D_MODEL]).astype(BF16)
        gb_out[...] = _sigmoid(gates[:, D_MODEL:2 * D_MODEL]).astype(BF16)


def _two_stream_specs(n_p_tiles, n_s_tiles, tm=TM):
    xp_spec = pl.BlockSpec((tm, D_MODEL), lambda i: (jnp.minimum(i, n_p_tiles - 1), 0))
    mode = pl.Buffered(1) if n_s_tiles == 1 else None
    xs_spec = pl.BlockSpec((tm, D_MODEL), lambda i: (jnp.clip(i - n_p_tiles, 0, n_s_tiles - 1), 0),
                           pipeline_mode=mode)
    return xp_spec, xs_spec


TM_IN = TM


def _in_proj(xp, xsm, meta, norm1, w_in):
    t_p, t_s = xp.shape[0], xsm.shape[0]
    n_p, n_s = t_p // TM_IN, t_s // TM_IN
    t = t_p + t_s
    half = 3 * D_CONV + D_SSM
    xp_spec, xs_spec = _two_stream_specs(n_p, n_s, TM_IN)
    g_spec = pl.BlockSpec((1, D_MODEL), lambda i: (0, 0))
    any_spec = pl.BlockSpec(memory_space=pl.ANY)
    row = lambda w: pl.BlockSpec((TM_IN, w), lambda i: (i, 0))
    const = lambda r, w: pl.BlockSpec((r, w), lambda i: (0, 0))
    params = pltpu.CompilerParams(dimension_semantics=("arbitrary",), vmem_limit_bytes=VMEM_LIMIT)
    scratch = [pltpu.VMEM((D_MODEL, half), BF16),
               pltpu.VMEM((2, D_MODEL, WEIGHT_CHUNK), F32),
               pltpu.SemaphoreType.DMA((2,))]

    xb, z, xs, zmeta, xsmeta = pl.pallas_call(
        functools.partial(_in_proj_mix_body, n_p),
        grid=(n_p + n_s,),
        in_specs=[xp_spec, xs_spec, const(N_META, D_MODEL), g_spec, any_spec],
        out_specs=[row(D_CONV), row(D_CONV), row(D_SSM), const(N_META, D_CONV), const(N_META, D_SSM)],
        out_shape=[jax.ShapeDtypeStruct((t, D_CONV), BF16),
                   jax.ShapeDtypeStruct((t, D_CONV), F32),
                   jax.ShapeDtypeStruct((t, D_SSM), F32),
                   jax.ShapeDtypeStruct((N_META, D_CONV), F32),
                   jax.ShapeDtypeStruct((N_META, D_SSM), F32)],
        scratch_shapes=scratch,
        compiler_params=params,
        name="in_proj_mix",
    )(xp, xsm, meta, norm1, w_in)

    sga, sgb = pl.pallas_call(
        functools.partial(_in_proj_gate_body, n_p),
        grid=(n_p + n_s,),
        in_specs=[xp_spec, xs_spec, g_spec, any_spec],
        out_specs=[row(D_MODEL), row(D_MODEL)],
        out_shape=[jax.ShapeDtypeStruct((t, D_MODEL), BF16),
                   jax.ShapeDtypeStruct((t, D_MODEL), BF16)],
        scratch_shapes=scratch,
        compiler_params=params,
        name="in_proj_gate",
    )(xp, xsm, norm1, w_in)
    return xb, z, xs, zmeta, xsmeta, sga, sgb


S5_GROUPS_PER_STEP = LANES // SSM_H
S5_PAIRS_PER_STEP = S5_GROUPS_PER_STEP // 2


def _s5_chunk_mats(lam_re, lam_im, log_dt, b_re, b_im, c_re, c_im):
    dt = jnp.exp(log_dt)[:, None]
    lr, li = lam_re, lam_im
    mag = jnp.exp(lr * dt)
    ab_re, ab_im = mag * jnp.cos(li * dt), mag * jnp.sin(li * dt)
    nr, ni = ab_re - 1.0, ab_im
    den = lr * lr + li * li
    k_re = (nr * lr + ni * li) / den
    k_im = (ni * lr - nr * li) / den
    bb_re = k_re[..., None] * b_re - k_im[..., None] * b_im
    bb_im = k_re[..., None] * b_im + k_im[..., None] * b_re
    return lr * dt, li * dt, bb_re, bb_im, c_re, c_im


def _cmul(ar, ai, br, bi):
    return ar * br - ai * bi, ar * bi + ai * br


def _s5_operators(pair, parity, params, toe_scr, p_scr, q_scr, a16_scr, g_scr, m_scr):
    k = 2 * pair + parity
    zr_ref, zi_ref, bbr_ref, bbi_ref, cr_ref, ci_ref = params
    zr, zi = zr_ref[k:k + 1, :], zi_ref[k:k + 1, :]
    mag = jnp.exp(zr)
    ar, ai = mag * jnp.cos(zi), mag * jnp.sin(zi)
    eye = lax.broadcasted_iota(I32, (SSM_P, SSM_P), 0) == lax.broadcasted_iota(I32, (SSM_P, SSM_P), 1)
    acr = jnp.sum(jnp.where(eye, ar, 0.0), axis=1, keepdims=True)
    aci = jnp.sum(jnp.where(eye, ai, 0.0), axis=1, keepdims=True)
    c_re, c_im = cr_ref[k], ci_ref[k]
    bb_re, bb_im = bbr_ref[k], bbi_ref[k]

    blk = lax.broadcasted_iota(I32, (1, CHUNK_W), 1) >> 4

    pr, pi = jnp.ones_like(ar), jnp.zeros_like(ar)
    pcr, pci = jnp.ones_like(acr), jnp.zeros_like(acr)
    pw_re = jnp.zeros((SSM_P, CHUNK_W), F32)
    pw_im = jnp.zeros((SSM_P, CHUNK_W), F32)
    for d in range(CHUNK + 1):
        g_scr[0, d * SSM_H:(d + 1) * SSM_H, :] = c_re * pr - c_im * pi
        g_scr[1, d * SSM_H:(d + 1) * SSM_H, :] = c_re * pi + c_im * pr
        if d < CHUNK:
            pw_re = jnp.where(blk == CHUNK - 1 - d, pcr, pw_re)
            pw_im = jnp.where(blk == CHUNK - 1 - d, pci, pw_im)
            pcr, pci = _cmul(pcr, pci, acr, aci)
            pr, pi = _cmul(pr, pi, ar, ai)

    half = parity * SSM_P
    a16_scr[pair, 0, :, half:half + SSM_P] = pr
    a16_scr[pair, 1, :, half:half + SSM_P] = pi
    q_scr[k] = jnp.zeros(q_scr.shape[1:], BF16)
    q_scr[k, 0, :, half:half + SSM_P] = g_scr[0, SSM_H:, :].astype(BF16)
    q_scr[k, 1, :, half:half + SSM_P] = (-g_scr[1, SSM_H:, :]).astype(BF16)

    rep = jnp.where(lax.broadcasted_iota(I32, (SSM_H, CHUNK_W), 0)
                    == (lax.broadcasted_iota(I32, (SSM_H, CHUNK_W), 1) & (SSM_H - 1)), 1.0, 0.0).astype(BF16)

    def widen(b):
        b_hi, b_lo = _split_bf16(b)
        return (jnp.dot(b_hi, rep, preferred_element_type=F32) + jnp.dot(b_lo, rep, preferred_element_type=F32))

    bw_re, bw_im = widen(bb_re), widen(bb_im)
    p_re, p_im = _cmul(pw_re, pw_im, bw_re, bw_im)
    p_scr[k, 0:SSM_P, :] = p_re.astype(BF16)
    p_scr[k, SSM_P:2 * SSM_P, :] = p_im.astype(BF16)

    m_scr[...] = _dot3(g_scr[0, 0:CHUNK_W, :], bw_re) - _dot3(g_scr[1, 0:CHUNK_W, :], bw_im)
    for t in range(CHUNK):
        acc = jnp.where(blk == 0, m_scr[t * SSM_H:(t + 1) * SSM_H, :], 0.0)
        for j in range(1, t + 1):
            acc = jnp.where(blk == j, m_scr[(t - j) * SSM_H:(t - j + 1) * SSM_H, :], acc)
        toe_scr[k, t * SSM_H:(t + 1) * SSM_H, :] = acc.astype(BF16)


def _s5_body(n_pc, n_pb, n_sb, xs_ref, xsmeta_ref, zr_ref, zi_ref, bbr_ref, bbi_ref, cr_ref, ci_ref,
             s0re_ref, s0im_ref,
             y_out, pfre_out, pfim_out, sfre_out, sfim_out,
             u_scr, sl_re, sl_im, sp_re, sp_im, yt_scr, toe_ref, p_ref, q_ref, a16_scr,
             g_scr, m_scr):
    gb = S5_GROUPS_PER_STEP
    n_p_rows = n_pb * n_pc
    row_s = n_p_rows
    row_m = row_s + n_sb
    t_p = n_p_rows * CHUNK
    rows_pad = u_scr.shape[2]

    for k in range(gb):
        _s5_operators(k // 2, k % 2, (zr_ref, zi_ref, bbr_ref, bbi_ref, cr_ref, ci_ref),
                      toe_ref, p_ref, q_ref, a16_scr, g_scr, m_scr)

    first_row = lax.broadcasted_iota(I32, (SUBLANES, 1), 0) == 0
    tail = jnp.zeros((rows_pad - row_m - SUBLANES, LANES), F32)
    for t in range(CHUNK):
        meta_rows = jnp.where(first_row, xsmeta_ref[t:t + 1, :], 0.0)
        rows_t = jnp.concatenate([xs_ref[pl.ds(t, n_p_rows, stride=CHUNK), :],
                                  xs_ref[pl.ds(t_p + t, n_sb, stride=CHUNK), :], meta_rows, tail], axis=0)
        xt = rows_t.T.astype(BF16)
        for k in range(gb):
            u_scr[k, t * SSM_H:(t + 1) * SSM_H, :] = xt[k * SSM_H:(k + 1) * SSM_H, :]

    npair = S5_PAIRS_PER_STEP
    for j in range(npair):
        sl0 = jnp.dot(p_ref[2 * j], u_scr[2 * j], preferred_element_type=F32)
        sl1 = jnp.dot(p_ref[2 * j + 1], u_scr[2 * j + 1], preferred_element_type=F32)
        sl_re[j] = jnp.concatenate([sl0[0:SSM_P, :], sl1[0:SSM_P, :]], axis=0).T
        sl_im[j] = jnp.concatenate([sl0[SSM_P:2 * SSM_P, :], sl1[SSM_P:2 * SSM_P, :]], axis=0).T
        sp_re[j, row_m:rows_pad, :] = jnp.zeros((rows_pad - row_m, 2 * SSM_P), F32)
        sp_im[j, row_m:rows_pad, :] = jnp.zeros((rows_pad - row_m, 2 * SSM_P), F32)

    ar = [a16_scr[j, 0] for j in range(npair)]
    ai = [a16_scr[j, 1] for j in range(npair)]
    sre = [jnp.broadcast_to(sl_re[j, row_m:row_m + 1, :], (n_pb, 2 * SSM_P)) for j in range(npair)]
    sim = [jnp.broadcast_to(sl_im[j, row_m:row_m + 1, :], (n_pb, 2 * SSM_P)) for j in range(npair)]
    for c in range(n_pc):
        rows = pl.ds(c, n_pb, stride=n_pc)
        for j in range(npair):
            sp_re[j, rows, :] = sre[j]
            sp_im[j, rows, :] = sim[j]
            nre = ar[j] * sre[j] - ai[j] * sim[j] + sl_re[j, rows, :]
            nim = ar[j] * sim[j] + ai[j] * sre[j] + sl_im[j, rows, :]
            sre[j], sim[j] = nre, nim
    def put_pair(out, j, val):
        out[:, 2 * j, :] = val[:, 0:SSM_P]
        out[:, 2 * j + 1, :] = val[:, SSM_P:2 * SSM_P]

    for j in range(npair):
        put_pair(pfre_out, j, sre[j])
        put_pair(pfim_out, j, sim[j])
        s0r = jnp.concatenate([s0re_ref[:, 2 * j, :], s0re_ref[:, 2 * j + 1, :]], axis=1)
        s0i = jnp.concatenate([s0im_ref[:, 2 * j, :], s0im_ref[:, 2 * j + 1, :]], axis=1)
        sp_re[j, row_s:row_m, :] = s0r
        sp_im[j, row_s:row_m, :] = s0i
        put_pair(sfre_out, j, ar[j] * s0r - ai[j] * s0i + sl_re[j, row_s:row_m, :])
        put_pair(sfim_out, j, ar[j] * s0i + ai[j] * s0r + sl_im[j, row_s:row_m, :])

    nt = (((1,), (1,)), ((), ()))
    for k in range(gb):
        y = jnp.dot(toe_ref[k], u_scr[k], preferred_element_type=F32)
        y += lax.dot_general(q_ref[k, 0], sp_re[k // 2].astype(BF16), nt, preferred_element_type=F32)
        y += lax.dot_general(q_ref[k, 1], sp_im[k // 2].astype(BF16), nt, preferred_element_type=F32)
        for t in range(CHUNK):
            yt_scr[t, k * SSM_H:(k + 1) * SSM_H, :] = y[t * SSM_H:(t + 1) * SSM_H, :]
    for t in range(CHUNK):
        yt = yt_scr[t].T
        y_out[pl.ds(t, n_p_rows, stride=CHUNK), :] = yt[0:n_p_rows, :]
        y_out[pl.ds(t_p + t, n_sb, stride=CHUNK), :] = yt[row_s:row_m, :]


def _s5(xs, xsmeta, state_re, state_im, mats, n_pb, n_sb, seq):
    t = xs.shape[0]
    n_pc = seq // CHUNK
    rows = n_pc * n_pb + n_sb + 1
    rows_pad = -(-rows // LANES) * LANES
    gb, npair = S5_GROUPS_PER_STEP, S5_PAIRS_PER_STEP
    blk3 = lambda n, r, c: pl.BlockSpec((n, r, c), lambda i: (i, 0, 0))
    state = lambda n: pl.BlockSpec((n, gb, SSM_P), lambda i: (0, i, 0))
    y, pfre, pfim, sfre, sfim = pl.pallas_call(
        functools.partial(_s5_body, n_pc, n_pb, n_sb),
        grid=(SSM_G // gb,),
        in_specs=[pl.BlockSpec((t, LANES), lambda i: (0, i)),
                  pl.BlockSpec((N_META, LANES), lambda i: (0, i)),
                  pl.BlockSpec((gb, SSM_P), lambda i: (i, 0)), pl.BlockSpec((gb, SSM_P), lambda i: (i, 0)),
                  blk3(gb, SSM_P, SSM_H), blk3(gb, SSM_P, SSM_H),
                  blk3(gb, SSM_H, SSM_P), blk3(gb, SSM_H, SSM_P),
                  state(n_sb), state(n_sb)],
        out_specs=[pl.BlockSpec((t, LANES), lambda i: (0, i)),
                   state(n_pb), state(n_pb), state(n_sb), state(n_sb)],
        out_shape=[jax.ShapeDtypeStruct((t, D_SSM), F32),
                   jax.ShapeDtypeStruct((n_pb, SSM_G, SSM_P), F32),
                   jax.ShapeDtypeStruct((n_pb, SSM_G, SSM_P), F32),
                   jax.ShapeDtypeStruct((n_sb, SSM_G, SSM_P), F32),
                   jax.ShapeDtypeStruct((n_sb, SSM_G, SSM_P), F32)],
        scratch_shapes=[pltpu.VMEM((gb, CHUNK_W, rows_pad), BF16),
                        pltpu.VMEM((npair, rows_pad, 2 * SSM_P), F32),
                        pltpu.VMEM((npair, rows_pad, 2 * SSM_P), F32),
                        pltpu.VMEM((npair, rows_pad, 2 * SSM_P), F32),
                        pltpu.VMEM((npair, rows_pad, 2 * SSM_P), F32),
                        pltpu.VMEM((CHUNK, LANES, rows_pad), F32),
                        pltpu.VMEM((gb, CHUNK_W, CHUNK_W), BF16),
                        pltpu.VMEM((gb, 2 * SSM_P, CHUNK_W), BF16),
                        pltpu.VMEM((gb, 2, CHUNK_W, 2 * SSM_P), BF16),
                        pltpu.VMEM((npair, 2, 1, 2 * SSM_P), F32),
                        pltpu.VMEM((2, (CHUNK + 1) * SSM_H, SSM_P), F32),
                        pltpu.VMEM((CHUNK_W, CHUNK_W), F32)],
        compiler_params=pltpu.CompilerParams(dimension_semantics=("arbitrary",), vmem_limit_bytes=VMEM_LIMIT),
        name="s5_chunks",
    )(xs, xsmeta, *mats, state_re, state_im)
    return y, pfre[None], pfim[None], sfre[None], sfim[None]


def _mixers_body(n_p_tiles, tiles_per_seq, xb_ref, z_ref, y5_ref, xs_ref, sga_ref, sgb_ref,
                 zmeta_ref, cstate_ref, cw_ref, dskip_ref,
                 wc_hbm, wg_hbm, wso_hbm, merged_out,
                 wc, wg, wso, stage_a, stage_b, sem, carry, inj_scr):
    i = pl.program_id(0)

    @pl.when(i == 0)
    def _():
        _load_weight_bf16(wc_hbm, wc, stage_a, sem)
        _load_weight_bf16(wg_hbm, wg, stage_b, sem)
        _load_weight_bf16(wso_hbm, wso, stage_a, sem)

    @pl.when(jnp.logical_and(i < n_p_tiles, i % tiles_per_seq == 0))
    def _():
        carry[0:2, :] = zmeta_ref[N_META - 2:N_META, :]

    is_s = i >= n_p_tiles

    @pl.when(i == 0)
    def _():
        inj_scr[...] = jnp.zeros(inj_scr.shape, F32)

    @pl.when(is_s)
    def _():
        seqs = TM // CHUNK
        rr = lax.broadcasted_iota(I32, (TM, seqs), 0)
        cc = lax.broadcasted_iota(I32, (TM, seqs), 1) * CHUNK
        at0 = jnp.where(rr == cc, 1.0, 0.0).astype(BF16)
        at1 = jnp.where(rr == cc + 1, 1.0, 0.0).astype(BF16)

        def place(sel, rows):
            r_hi, r_lo = _split_bf16(rows)
            return (jnp.dot(sel, r_hi, preferred_element_type=F32)
                    + jnp.dot(sel, r_lo, preferred_element_type=F32))

        old, new = cstate_ref[:, 0:D_CONV], cstate_ref[:, D_CONV:2 * D_CONV]
        inj_scr[0] = place(at0, new)
        inj_scr[1] = place(at0, old) + place(at1, new)

    z = z_ref[...]
    row = lax.broadcasted_iota(I32, (TM, 1), 0)
    r1 = pltpu.roll(z, 1, 0)
    r2 = pltpu.roll(z, 2, 0)
    c1 = carry[1:2, :]
    c2 = carry[0:1, :]
    pos = jnp.where(is_s, row & (CHUNK - 1), row)
    first1 = pos == 0
    first2 = pos < 2
    fill1 = jnp.where(is_s, inj_scr[0], jnp.broadcast_to(c1, z.shape))
    fill2 = jnp.where(is_s, inj_scr[1], jnp.where(row == 0, c2, c1))
    zp1 = jnp.where(first1, fill1, r1)
    zp2 = jnp.where(first2, fill2, r2)
    carry[0:2, :] = z[TM - 2:TM, :]

    cw = cw_ref[...]
    conv = cw[0:1, :] * zp2 + cw[1:2, :] * zp1 + cw[2:3, :] * z
    a_in = (xb_ref[...].astype(F32) * conv).astype(BF16)
    ya = jnp.dot(a_in, wc[...], preferred_element_type=F32)

    ys = y5_ref[...] + dskip_ref[...] * xs_ref[...]
    ys = _gelu_tanh(ys)
    glu = jnp.dot(ys.astype(BF16), wg[...], preferred_element_type=F32)
    ys = ys * _sigmoid(glu)
    yb = jnp.dot(ys.astype(BF16), wso[...], preferred_element_type=F32)

    merged = sga_ref[...].astype(F32) * ya + sgb_ref[...].astype(F32) * yb
    merged_out[...] = merged.astype(BF16)


def _mixers(xb, z, y5, xs, sga, sgb, zmeta, conv_state, conv_w, d_skip, w_conv_out, w_glu, w_ssm_out,
            n_p_tiles, tiles_per_seq):
    t = xb.shape[0]
    seqs = TM // CHUNK
    n_s_tiles = conv_state.shape[0] // seqs
    row = lambda w: pl.BlockSpec((TM, w), lambda i: (i, 0))
    const = lambda r, w: pl.BlockSpec((r, w), lambda i: (0, 0))
    cstate = pl.BlockSpec((seqs, 2 * D_CONV), lambda i: (jnp.clip(i - n_p_tiles, 0, n_s_tiles - 1), 0))
    any_spec = pl.BlockSpec(memory_space=pl.ANY)
    return pl.pallas_call(
        functools.partial(_mixers_body, n_p_tiles, tiles_per_seq),
        grid=(t // TM,),
        in_specs=[row(D_CONV), row(D_CONV), row(D_SSM), row(D_SSM), row(D_MODEL), row(D_MODEL),
                  const(N_META, D_CONV), cstate, const(3, D_CONV), const(1, D_SSM),
                  any_spec, any_spec, any_spec],
        out_specs=row(D_MODEL),
        out_shape=jax.ShapeDtypeStruct((t, D_MODEL), BF16),
        scratch_shapes=[pltpu.VMEM((D_CONV, D_MODEL), BF16),
                        pltpu.VMEM((D_SSM, D_SSM), BF16),
                        pltpu.VMEM((D_SSM, D_MODEL), BF16),
                        pltpu.VMEM((2, 256, D_MODEL), F32),
                        pltpu.VMEM((2, 256, D_SSM), F32),
                        pltpu.SemaphoreType.DMA((2,)),
                        pltpu.VMEM((8, D_CONV), F32),
                        pltpu.VMEM((2, TM, D_CONV), F32)],
        compiler_params=pltpu.CompilerParams(dimension_semantics=("arbitrary",), vmem_limit_bytes=VMEM_LIMIT),
        name="mixers",
    )(xb, z, y5, xs, sga, sgb, zmeta, conv_state, conv_w, d_skip, w_conv_out, w_glu, w_ssm_out)


ROUTE_W = LANES
COARSE0 = N_EXPERTS


def _route(logits, cnt):
    col = lax.broadcasted_iota(I32, logits.shape, 1)
    colf = col.astype(F32)
    neg = jnp.float32(-jnp.inf)
    big = jnp.float32(1 << 20)
    is_c = jnp.logical_and(col >= COARSE0, col < COARSE0 + N_EGROUPS)
    lc = jnp.where(is_c, logits, neg)
    cmax = jnp.max(lc, axis=-1, keepdims=True)
    gi = jnp.min(jnp.where(lc == cmax, colf - COARSE0, big), axis=-1, keepdims=True)
    pg = 1.0 / jnp.sum(jnp.where(is_c, jnp.exp(lc - cmax), 0.0), axis=-1, keepdims=True)
    grp = (col >> 3).astype(F32)
    in_g = jnp.logical_and(col < N_EXPERTS, grp == gi)
    lf = jnp.where(in_g, logits, neg)
    m1 = jnp.max(lf, axis=-1, keepdims=True)
    i1 = jnp.min(jnp.where(lf == m1, colf, big), axis=-1, keepdims=True)
    lf2 = jnp.where(colf == i1, neg, lf)
    m2 = jnp.max(lf2, axis=-1, keepdims=True)
    i2 = jnp.min(jnp.where(lf2 == m2, colf, big), axis=-1, keepdims=True)
    e2 = jnp.exp(m2 - m1)
    w1 = pg / (1.0 + e2)
    w2 = pg * e2 / (1.0 + e2)
    n = logits.shape[0]
    hit1 = colf == i1
    hit2 = colf == i2
    onehot = jnp.where(jnp.logical_or(hit1, hit2), 1.0, 0.0)
    rr = lax.broadcasted_iota(I32, (n, n), 0)
    cc = lax.broadcasted_iota(I32, (n, n), 1)
    tri = jnp.where(cc < rr, 1.0, 0.0).astype(BF16)
    pos = jnp.dot(tri, onehot.astype(BF16), preferred_element_type=F32) + cnt
    rank1 = jnp.sum(jnp.where(hit1, pos, 0.0), axis=-1, keepdims=True)
    rank2 = jnp.sum(jnp.where(hit2, pos, 0.0), axis=-1, keepdims=True)
    vals = (i1, i2, w1, w2, rank1, rank2)
    rec = jnp.zeros(logits.shape, F32)
    for c, val in enumerate(vals):
        rec = jnp.where(col == c, val, rec)
    return rec, cnt + jnp.sum(onehot, axis=0, keepdims=True)


def _out_proj_body(n_p_tiles, merged_ref, xp_ref, xsm_ref, g2_ref, wr_ref, wo_hbm,
                   h2_out, v_out, rec_out, ids_out, cnt_out, wo, stage, sem, h2_scr):
    i = pl.program_id(0)
    n = pl.num_programs(0) - 1

    def route_prev():
        v = _rmsnorm(h2_scr[...], g2_ref[...])
        v_out[...] = _pack_halves(v)
        v_hi, v_lo = _split_bf16(v)
        both = jnp.dot(v_hi, wr_ref[...], preferred_element_type=F32)
        logits = (both[:, :ROUTE_W] + both[:, ROUTE_W:]
                  + jnp.dot(v_lo, wr_ref[:, :ROUTE_W], preferred_element_type=F32))
        rec, cnt = _route(logits, cnt_out[...])
        rec_out[...] = rec
        ids_out[...] = rec.T[0:SUBLANES, :]
        cnt_out[...] = cnt

    def project():
        x = jnp.where(i < n_p_tiles, xp_ref[...], xsm_ref[...])
        h2 = x + jnp.dot(merged_ref[...], wo[...], preferred_element_type=F32)
        h2_out[...] = h2
        return h2

    @pl.when(i == 0)
    def _():
        cnt_out[...] = jnp.zeros(cnt_out.shape, F32)
        cw = stage.shape[2]
        merged = merged_ref[...]

        def on_chunk(c):
            cols = slice(c * cw, (c + 1) * cw)
            h2 = xp_ref[:, cols] + jnp.dot(merged, wo[:, cols], preferred_element_type=F32)
            h2_out[:, cols] = h2
            h2_scr[:, cols] = h2

        _stream_weight_bf16(wo_hbm, wo, stage, sem, 0, on_chunk)

    @pl.when(jnp.logical_and(i > 0, i < n))
    def _():
        route_prev()
        h2_scr[...] = project()

    @pl.when(i == n)
    def _():
        route_prev()


def _out_proj(merged, xp, xsm, norm2, w_router, w_o):
    t_p, t_s = xp.shape[0], xsm.shape[0]
    n_p, n_s = t_p // TM, t_s // TM
    n = n_p + n_s
    t = t_p + t_s
    xp_spec, xs_spec = _two_stream_specs(n_p, n_s)
    cur = lambda w: pl.BlockSpec((TM, w), lambda i: (jnp.minimum(i, n - 1), 0))
    prev = lambda w: pl.BlockSpec((TM, w), lambda i: (jnp.maximum(i - 1, 0), 0))
    const = lambda r, w: pl.BlockSpec((r, w), lambda i: (0, 0))
    return pl.pallas_call(
        functools.partial(_out_proj_body, n_p),
        grid=(n + 1,),
        in_specs=[cur(D_MODEL), xp_spec, xs_spec, const(1, D_MODEL), const(D_MODEL, 2 * ROUTE_W),
                  pl.BlockSpec(memory_space=pl.ANY)],
        out_specs=[cur(D_MODEL), prev(HALF), prev(ROUTE_W),
                   pl.BlockSpec((SUBLANES, TM), lambda i: (0, jnp.maximum(i - 1, 0))), const(1, ROUTE_W)],
        out_shape=[jax.ShapeDtypeStruct((t, D_MODEL), F32),
                   jax.ShapeDtypeStruct((t, HALF), U32),
                   jax.ShapeDtypeStruct((t, ROUTE_W), F32),
                   jax.ShapeDtypeStruct((SUBLANES, t), F32),
                   jax.ShapeDtypeStruct((1, ROUTE_W), F32)],
        scratch_shapes=[pltpu.VMEM((D_MODEL, D_MODEL), BF16),
                        pltpu.VMEM((2, D_MODEL, WEIGHT_CHUNK), F32),
                        pltpu.SemaphoreType.DMA((2,)),
                        pltpu.VMEM((TM, D_MODEL), F32)],
        compiler_params=pltpu.CompilerParams(dimension_semantics=("arbitrary",), vmem_limit_bytes=VMEM_LIMIT),
        name="out_proj_route",
    )(merged, xp, xsm, norm2, w_router, w_o)


def _moe_plan(ids, cnt, n_tiles):
    t = ids.shape[1]
    n_pairs = 2 * t
    eid = ids[0:2].astype(I32).reshape(-1)
    rank = ids[4:6].astype(I32).reshape(-1)
    counts = cnt[0, :N_EXPERTS].astype(I32)
    pair_start = jnp.cumsum(counts) - counts
    experts = jnp.arange(N_EXPERTS, dtype=I32)
    onehot = (experts[:, None] == eid[None, :]).astype(I32)
    pos = rank + jnp.sum(onehot * pair_start[:, None], axis=0)
    _, order = lax.sort((pos, jnp.arange(n_pairs, dtype=I32)), num_keys=1)
    tiles_e = (counts + TM_MOE - 1) // TM_MOE
    tile_end = jnp.cumsum(tiles_e)
    tile_start = tile_end - tiles_e
    n_valid = tile_end[-1]
    tile_ids = jnp.arange(n_tiles, dtype=I32)
    tile_e = jnp.sum((tile_ids[:, None] >= tile_end[None, :]).astype(I32), axis=1)
    last_e = jnp.sum((n_valid - 1 >= tile_end).astype(I32))
    tile_e = jnp.minimum(jnp.where(tile_ids < n_valid, tile_e, last_e), N_EXPERTS - 1)
    tile_onehot = (tile_e[:, None] == experts[None, :]).astype(I32)
    tile_q0 = (tile_ids - jnp.sum(tile_onehot * tile_start[None, :], axis=1)) * TM_MOE
    tile_q0 = jnp.where(tile_ids < n_valid, tile_q0, 0)
    dst = rank + jnp.sum(onehot * (tile_start * TM_MOE)[:, None], axis=0)
    pad_start = tile_start * TM_MOE + counts
    pad_len = tiles_e * TM_MOE - counts
    return (dst.astype(I32), pad_start.astype(I32), pad_len.astype(I32), tile_e, tile_q0.astype(I32),
            n_valid.astype(I32).reshape(1), pair_start, counts, order)


DISPATCH_BUFS = 3
ROW_PIECES = tuple(TM_MOE >> (b + 1) for b in range(TM_MOE.bit_length() - 1))


def _dispatch_body(dst_ref, pad_start_ref, pad_len_ref, nvalid_ref, v_hbm, x_hbm, buf, zbuf, rsem, ssem, zsem):
    i = pl.program_id(0)
    n = pl.num_programs(0)

    def read(tile, slot):
        return pltpu.make_async_copy(v_hbm.at[pl.ds(pl.multiple_of(tile * TM, TM), TM)], buf.at[slot],
                                     rsem.at[slot])

    def row_write(slot, r, dst_row):
        return pltpu.make_async_copy(buf.at[slot, pl.ds(r, 1)], x_hbm.at[pl.ds(dst_row, 1)], ssem.at[slot])

    def drain(slot):
        for _ in range(2):
            pltpu.make_async_copy(buf.at[slot], x_hbm.at[pl.ds(0, TM)], ssem.at[slot]).wait()

    def pad_fill(go):
        def zero_rows(start, size):
            d = pltpu.make_async_copy(zbuf.at[pl.ds(0, size)], x_hbm.at[pl.ds(start, size)], zsem)
            d.start() if go else d.wait()

        def body(e, c):
            start, length = pad_start_ref[e], pad_len_ref[e]
            head = (-start) & (SUBLANES - 1)
            for h in range(SUBLANES - 1):
                @pl.when(h < head)
                def _(h=h):
                    zero_rows(start + h, 1)
            start, length = start + head, length - head
            for size in ROW_PIECES:
                if size >= SUBLANES:
                    @pl.when((length & size) != 0)
                    def _(size=size):
                        zero_rows(pl.multiple_of(start + (length & (-2 * size)), SUBLANES), size)
            return c
        lax.fori_loop(0, N_EXPERTS, body, 0)

        def unused(tile, c):
            for half in range(TM_MOE // ROW_PIECES[0]):
                zero_rows(pl.multiple_of(tile * TM_MOE + half * ROW_PIECES[0], SUBLANES), ROW_PIECES[0])
            return c
        lax.fori_loop(nvalid_ref[0], x_hbm.shape[0] // TM_MOE, unused, 0)

    @pl.when(i == 0)
    def _():
        zbuf[...] = jnp.zeros(zbuf.shape, U32)
        pad_fill(True)
        read(0, 0).start()

    @pl.when(i >= 2)
    def _():
        drain((i + 1) % DISPATCH_BUFS)

    @pl.when(i + 1 < n)
    def _():
        read(i + 1, (i + 1) % DISPATCH_BUFS).start()

    slot = i % DISPATCH_BUFS
    read(i, slot).wait()
    for r in range(TM):
        for k in range(2):
            row_write(slot, r, dst_ref[k * (n * TM) + i * TM + r]).start(priority=k)

    @pl.when(i == n - 1)
    def _():
        if n >= 2:
            drain((i - 1) % DISPATCH_BUFS)
        drain(slot)
        pad_fill(False)


def _dispatch(v, plan, n_tiles):
    dst, pad_start, pad_len, n_valid = plan[0], plan[1], plan[2], plan[5]
    t = v.shape[0]
    return pl.pallas_call(
        _dispatch_body,
        grid_spec=pltpu.PrefetchScalarGridSpec(
            num_scalar_prefetch=4,
            grid=(t // TM,),
            in_specs=[pl.BlockSpec(memory_space=pl.ANY)],
            out_specs=pl.BlockSpec(memory_space=pl.ANY),
            scratch_shapes=[pltpu.VMEM((DISPATCH_BUFS, TM, HALF), U32),
                            pltpu.VMEM((ROW_PIECES[0], HALF), U32),
                            pltpu.SemaphoreType.DMA((DISPATCH_BUFS,)),
                            pltpu.SemaphoreType.DMA((DISPATCH_BUFS,)),
                            pltpu.SemaphoreType.DMA(())]),
        out_shape=jax.ShapeDtypeStruct((n_tiles * TM_MOE, HALF), U32),
        compiler_params=pltpu.CompilerParams(dimension_semantics=("arbitrary",), vmem_limit_bytes=VMEM_LIMIT),
        name="moe_dispatch",
    )(dst, pad_start, pad_len, n_valid, v)


WEIGHT_SLOTS = 3


def _moe_body(n_tok, plane, tile_e_ref, tile_q0_ref, nvalid_ref, pstart_ref, cnt_ref, orow_ref,
              krank_ref, elist_ref, nexp_ref,
              x_ref, wg_hbm, wu_hbm, wd_hbm, o_hbm, ybuf0, ybuf1, ybuf2, ssem,
              stage_g, stage_u, stage_d, wsem, wg, wu, wd):
    i = pl.program_id(0)
    nv = nvalid_ref[0]
    bufs = (ybuf0, ybuf1, ybuf2)

    def scratch_row0(slot):
        return (slot & 1) * plane + n_tok + (slot >> 1) * TM_MOE

    def row_write(slot, r, dst_row):
        return pltpu.make_async_copy(bufs[slot].at[pl.ds(r, 1)], o_hbm.at[pl.ds(dst_row, 1)], ssem.at[slot])

    def scratch_rows(region):
        return pltpu.make_async_copy(ybuf0, o_hbm.at[pl.ds(scratch_row0(region), TM_MOE)], ssem.at[0])

    def start_writes(tile, slot, rows):
        e = tile_e_ref[tile]
        valid = cnt_ref[e] - tile_q0_ref[tile]
        first = pstart_ref[e] + tile_q0_ref[tile]
        for r in rows:
            row_write(slot, r, jnp.where(r < valid, orow_ref[first + r], scratch_row0(slot) + r)).start(
                priority=r % 2)

    def compute(slot, writes=None):
        quarter = TM_MOE // 4
        batch = lambda q: start_writes(*writes, range(q * quarter, (q + 1) * quarter)) if writes else None
        x_lo, x_hi = (h.astype(BF16) for h in _unpack_halves(x_ref[...]))
        batch(0)
        hg = (jnp.dot(x_lo, wg[0:HALF, :], preferred_element_type=F32)
              + jnp.dot(x_hi, wg[HALF:D_MODEL, :], preferred_element_type=F32))
        batch(1)
        hu = (jnp.dot(x_lo, wu[0:HALF, :], preferred_element_type=F32)
              + jnp.dot(x_hi, wu[HALF:D_MODEL, :], preferred_element_type=F32))
        batch(2)
        act = hg * _sigmoid(hg) * hu
        y = jnp.dot(act.astype(BF16), wd[...], preferred_element_type=F32)
        batch(3)
        bufs[slot][...] = _pack_halves(y)

    @pl.when(i == 0)
    def _():
        ybuf0[...] = jnp.zeros(ybuf0.shape, U32)
        for region in range(4):
            scratch_rows(region).start()
        for region in range(4):
            scratch_rows(region).wait()

    @pl.when(jnp.logical_and(i >= 3, i < nv + 3))
    def _():
        pltpu.make_async_copy(ybuf0, o_hbm.at[pl.ds(0, TM_MOE)], ssem.at[i % 3]).wait()

    def weight_copies(k, go):
        e = elist_ref[k]
        slot = k % WEIGHT_SLOTS
        for w_hbm, st in ((wg_hbm, stage_g), (wu_hbm, stage_u), (wd_hbm, stage_d)):
            d = pltpu.make_async_copy(w_hbm.at[e], st.at[slot], wsem.at[slot])
            d.start() if go else d.wait()

    @pl.when(i == 0)
    def _():
        for k in range(WEIGHT_SLOTS):
            @pl.when(k < nexp_ref[0])
            def _(k=k):
                weight_copies(k, True)

    @pl.when(i < nv)
    def _():
        prev_e = tile_e_ref[jnp.maximum(i - 1, 0)]

        @pl.when(jnp.logical_or(i == 0, tile_e_ref[i] != prev_e))
        def _():
            k = krank_ref[tile_e_ref[i]]
            slot = k % WEIGHT_SLOTS
            weight_copies(k, False)
            wg[...] = stage_g[slot].astype(BF16)
            wu[...] = stage_u[slot].astype(BF16)
            wd[...] = stage_d[slot].astype(BF16)

            @pl.when(k + WEIGHT_SLOTS < nexp_ref[0])
            def _():
                weight_copies(k + WEIGHT_SLOTS, True)

    @pl.when(i == 0)
    def _():
        compute(0)

    for slot in range(3):
        prev = (slot + 2) % 3

        @pl.when(jnp.logical_and(i % 3 == slot, jnp.logical_and(i >= 1, i < nv)))
        def _(slot=slot, prev=prev):
            compute(slot, writes=(i - 1, prev))

        @pl.when(jnp.logical_and(i % 3 == slot, i == nv))
        def _(prev=prev):
            start_writes(i - 1, prev, range(TM_MOE))


def _moe(x_disp, plan, w_gate, w_up, w_down, n_tiles, n_tok):
    tile_e, tile_q0, n_valid, pair_start, counts, order = plan[3:9]
    plane = n_tok + 2 * TM_MOE
    k_of = (order >= n_tok).astype(I32)
    orow = jnp.pad(k_of * plane + (order - k_of * n_tok), (0, TM_MOE))
    present = (counts > 0).astype(I32)
    krank = jnp.cumsum(present) - present
    experts = jnp.arange(N_EXPERTS, dtype=I32)
    elist = jnp.sum(jnp.where((krank[None, :] == experts[:, None]) & (present[None, :] > 0), experts[None, :], 0),
                    axis=1).astype(I32)
    nexp = jnp.sum(present).astype(I32).reshape(1)
    tile = lambda i, nv: jnp.minimum(i, jnp.maximum(nv[0] - 1, 0))
    any_spec = pl.BlockSpec(memory_space=pl.ANY)
    ybuf = pltpu.VMEM((TM_MOE, HALF), U32)
    return pl.pallas_call(
        functools.partial(_moe_body, n_tok, plane),
        grid_spec=pltpu.PrefetchScalarGridSpec(
            num_scalar_prefetch=9,
            grid=(n_tiles + 3,),
            in_specs=[pl.BlockSpec((TM_MOE, HALF), lambda i, te, tq, nv, *_: (tile(i, nv), 0)),
                      any_spec, any_spec, any_spec],
            out_specs=pl.BlockSpec(memory_space=pl.ANY),
            scratch_shapes=[ybuf, ybuf, ybuf,
                            pltpu.SemaphoreType.DMA((3,)),
                            pltpu.VMEM((WEIGHT_SLOTS, D_MODEL, D_EXPERT), F32),
                            pltpu.VMEM((WEIGHT_SLOTS, D_MODEL, D_EXPERT), F32),
                            pltpu.VMEM((WEIGHT_SLOTS, D_EXPERT, D_MODEL), F32),
                            pltpu.SemaphoreType.DMA((WEIGHT_SLOTS,)),
                            pltpu.VMEM((D_MODEL, D_EXPERT), BF16),
                            pltpu.VMEM((D_MODEL, D_EXPERT), BF16),
                            pltpu.VMEM((D_EXPERT, D_MODEL), BF16)]),
        out_shape=jax.ShapeDtypeStruct((2 * plane, HALF), U32),
        compiler_params=pltpu.CompilerParams(dimension_semantics=("arbitrary",), vmem_limit_bytes=VMEM_LIMIT),
        name="moe_experts",
    )(tile_e, tile_q0, n_valid, pair_start, counts, orow, krank.astype(I32), elist, nexp,
      x_disp, w_gate, w_up, w_down)


def _combine_body(n_p_tiles, h2_ref, rec_ref, gf_ref, y1_ref, y2_ref, outp_ref, outs_ref):
    i = pl.program_id(0)
    rec = rec_ref[...]
    y1 = jnp.concatenate(_unpack_halves(y1_ref[...]), axis=-1)
    y2 = jnp.concatenate(_unpack_halves(y2_ref[...]), axis=-1)
    h = rec[:, 2:3] * y1 + rec[:, 3:4] * y2
    out = _rmsnorm(h2_ref[...] + h, gf_ref[...])

    @pl.when(i < n_p_tiles)
    def _():
        outp_ref[...] = out

    @pl.when(i >= n_p_tiles)
    def _():
        outs_ref[...] = out


def _combine(h2, rec, y_pairs, final_norm, t_p, t_s):
    n_p, n_s = t_p // TM, t_s // TM
    plane_tiles = y_pairs.shape[0] // 2 // TM
    return pl.pallas_call(
        functools.partial(_combine_body, n_p),
        grid=(n_p + n_s,),
        in_specs=[pl.BlockSpec((TM, D_MODEL), lambda i: (i, 0)),
                  pl.BlockSpec((TM, ROUTE_W), lambda i: (i, 0)),
                  pl.BlockSpec((1, D_MODEL), lambda i: (0, 0)),
                  pl.BlockSpec((TM, HALF), lambda i: (i, 0)),
                  pl.BlockSpec((TM, HALF), lambda i: (i + plane_tiles, 0))],
        out_specs=[pl.BlockSpec((TM, D_MODEL), lambda i: (jnp.minimum(i, n_p - 1), 0)),
                   pl.BlockSpec((TM, D_MODEL), lambda i: (jnp.clip(i - n_p, 0, n_s - 1), 0))],
        out_shape=[jax.ShapeDtypeStruct((t_p, D_MODEL), F32),
                   jax.ShapeDtypeStruct((t_s, D_MODEL), F32)],
        compiler_params=pltpu.CompilerParams(dimension_semantics=("arbitrary",), vmem_limit_bytes=VMEM_LIMIT),
        name="combine_norm",
    )(h2, rec, final_norm, y_pairs, y_pairs)


def kernel(x_prompt, x_sample, state_conv, state_ssm_re, state_ssm_im, meta_tokens, norm1, w_in, conv_w,
           lam_re, lam_im, log_dt, ssm_b_re, ssm_b_im, ssm_c_re, ssm_c_im, ssm_d, w_glu, w_conv_out,
           w_ssm_out, w_o, norm2, w_coarse, w_fine, w_gate, w_up, w_down, final_norm):
    n_pb, seq, _ = x_prompt.shape
    n_sb, dec_seq, _ = x_sample.shape
    assert dec_seq == CHUNK and seq % TM == 0 and (n_sb * dec_seq) % TM == 0 and N_META == CHUNK
    t_p, t_s = n_pb * seq, n_sb * dec_seq
    xp = x_prompt.reshape(t_p, D_MODEL)
    xsm = x_sample.reshape(t_s, D_MODEL)

    xb, z, xs, zmeta, xsmeta, sga, sgb = _in_proj(xp, xsm, meta_tokens, norm1, w_in[0])

    mats = _s5_chunk_mats(lam_re[0], lam_im[0], log_dt[0], ssm_b_re[0], ssm_b_im[0], ssm_c_re[0], ssm_c_im[0])
    y5, pf_re, pf_im, sf_re, sf_im = _s5(xs, xsmeta, state_ssm_re[0], state_ssm_im[0], mats, n_pb, n_sb, seq)

    merged = _mixers(xb, z, y5, xs, sga, sgb, zmeta, state_conv[0].reshape(n_sb, 2 * D_CONV), conv_w[0], ssm_d,
                     w_conv_out[0], w_glu[0], w_ssm_out[0], t_p // TM, seq // TM)

    w_router = jnp.concatenate(
        [w_fine[0], w_coarse[0], jnp.zeros((D_MODEL, ROUTE_W - N_EXPERTS - N_EGROUPS), F32)], axis=1)
    w_router = jnp.concatenate(_split_bf16(w_router), axis=1)
    h2, v, rec, ids, cnt = _out_proj(merged, xp, xsm, norm2, w_router, w_o[0])

    n_tiles = 2 * (t_p + t_s) // TM_MOE + N_EXPERTS
    plan = _moe_plan(ids, cnt, n_tiles)
    x_disp = _dispatch(v, plan, n_tiles)
    y_pairs = _moe(x_disp, plan, w_gate[0], w_up[0], w_down[0], n_tiles, t_p + t_s)
    y_p, y_s = _combine(h2, rec, y_pairs, final_norm.reshape(1, D_MODEL), t_p, t_s)

    new_conv_p = jnp.stack([z[(b + 1) * seq - 2:(b + 1) * seq] for b in range(n_pb)])
    new_conv_s = z[t_p:].reshape(n_sb, dec_seq, D_CONV)[:, dec_seq - 2:]
    return (y_p.reshape(n_pb, seq, D_MODEL), y_s.reshape(n_sb, dec_seq, D_MODEL),
            new_conv_p[None], pf_re, pf_im, new_conv_s[None], sf_re, sf_im)
```
